```python
import jax, jax.numpy as jnp
from jax import lax
import numpy as np

D_MODEL = 1024
BATCH = 8
SEQ = 8192
DEPTH = 2

CHUNK = 64
CONV_W = 4
EPS = 1e-6
F32 = jnp.float32

GDN_HEADS = 4
GDN_DK = 128
GDN_DV = 128
GDN_W = GDN_HEADS * GDN_DV
SSD_HEADS = 16
SSD_P = 64
SSD_N = 128
SSD_GROUPS = 2
SSD_W = SSD_HEADS * SSD_P
RET_HEADS = 4
RET_DK = 128
RET_DV = 128
RET_W = RET_HEADS * RET_DV
ROPE_BASE = 10000.0

MIX_W = GDN_W + SSD_W + RET_W

GDN_SIZES = [GDN_HEADS * GDN_DK, GDN_HEADS * GDN_DK, GDN_W, GDN_W, GDN_HEADS, GDN_HEADS]
SSD_SIZES = [SSD_W, SSD_GROUPS * SSD_N, SSD_GROUPS * SSD_N, SSD_W, SSD_HEADS]
RET_SIZES = [RET_HEADS * RET_DK, RET_HEADS * RET_DK, RET_W, RET_W]
IN_SIZES = GDN_SIZES + SSD_SIZES + RET_SIZES
N_IN = sum(IN_SIZES)
GDN_CONV_CH = 2 * GDN_HEADS * GDN_DK + GDN_W
SSD_CONV_CH = SSD_W + 2 * SSD_GROUPS * SSD_N

kernel_name = "hybrid_gdn_ssd_retention_parallel_groups"


def rmsnorm(x, w):
    xf = x.astype(F32)
    return xf * lax.rsqrt(jnp.mean(xf * xf, axis=-1, keepdims=True) + EPS) * w.astype(F32)


def split_cols(t, sizes):
    out, start = [], 0
    for s in sizes:
        out.append(t[..., start:start + s])
        start += s
    return out


def causal_conv(x, w):
    k = w.shape[0]
    seq = x.shape[1]
    xp = jnp.pad(x, ((0, 0), (k - 1, 0), (0, 0)))
    return sum(xp[:, i:i + seq] * w[i].astype(F32) for i in range(k))


def to_chunks(t):
    b, l = t.shape[:2]
    return jnp.moveaxis(t.reshape(b, l // CHUNK, CHUNK, *t.shape[2:]), 1, 0)


def from_chunks(t):
    n, b, c = t.shape[:3]
    return jnp.moveaxis(t, 0, 1).reshape(b, n * c, *t.shape[3:])


def l2norm(t):
    return t * lax.rsqrt(jnp.sum(t * t, axis=-1, keepdims=True) + EPS)


def rotary(t, pos):
    half = t.shape[-1] // 2
    inv = ROPE_BASE ** (-jnp.arange(half, dtype=F32) / half)
    ang = pos.astype(F32)[:, None] * inv[None, :]
    cos = jnp.cos(ang)[None, :, None, :]
    sin = jnp.sin(ang)[None, :, None, :]
    t1, t2 = t[..., :half], t[..., half:]
    return jnp.concatenate([t1 * cos - t2 * sin, t1 * sin + t2 * cos], axis=-1)


def gated_deltanet(q, k, v, b_raw, a_raw, A_log, dt_bias):
    bsz = q.shape[0]
    q = l2norm(q) * (GDN_DK ** -0.5)
    k = l2norm(k)
    beta = jax.nn.sigmoid(b_raw)
    g = -jnp.exp(A_log.astype(F32)) * jax.nn.softplus(a_raw + dt_bias.astype(F32))
    causal = jnp.tril(jnp.ones((CHUNK, CHUNK), bool))
    strict = jnp.tril(jnp.ones((CHUNK, CHUNK), F32), -1)
    eye = jnp.eye(CHUNK, dtype=F32)

    def step(S, inp):
        qi, ki, vi, bi, gi = inp
        gcum = jnp.cumsum(gi, axis=1)
        gh = jnp.swapaxes(gcum, 1, 2)
        diff = gh[..., :, None] - gh[..., None, :]
        decay = jnp.exp(jnp.where(causal, diff, -jnp.inf))
        kb = ki * bi[..., None]
        a_low = jnp.einsum('bihd,bjhd->bhij', kb, ki) * decay * strict
        rhs = jnp.concatenate([vi * bi[..., None], kb * jnp.exp(gcum)[..., None]], axis=-1)
        rhs = jnp.swapaxes(rhs, 1, 2)
        sol = lax.linalg.triangular_solve(a_low + eye, rhs, left_side=True, lower=True)
        u, w = sol[..., :GDN_DV], sol[..., GDN_DV:]
        v_new = u - jnp.einsum('bhcd,bhdv->bhcv', w, S)
        attn = jnp.einsum('bihd,bjhd->bhij', qi, ki) * decay
        o = (jnp.einsum('bihd,bhdv->bihv', qi * jnp.exp(gcum)[..., None], S)
             + jnp.einsum('bhij,bhjv->bihv', attn, v_new))
        glast = gcum[:, -1]
        kdec = ki * jnp.exp(glast[:, None, :] - gcum)[..., None]
        S = S * jnp.exp(glast)[:, :, None, None] + jnp.einsum('bjhd,bhjv->bhdv', kdec, v_new)
        return S, o

    s0 = jnp.zeros((bsz, GDN_HEADS, GDN_DK, GDN_DV), F32)
    _, o = lax.scan(step, s0, (to_chunks(q), to_chunks(k), to_chunks(v), to_chunks(beta), to_chunks(g)))
    return from_chunks(o)


def ssd(x, Bm, Cm, dt_raw, A_log, dt_bias, D):
    bsz, seq = x.shape[:2]
    hg = SSD_HEADS // SSD_GROUPS
    dt = jax.nn.softplus(dt_raw + dt_bias.astype(F32))
    a = (dt * -jnp.exp(A_log.astype(F32))).reshape(bsz, seq, SSD_GROUPS, hg)
    xdt = (x * dt[..., None]).reshape(bsz, seq, SSD_GROUPS, hg, SSD_P)
    causal = jnp.tril(jnp.ones((CHUNK, CHUNK), bool))

    def step(hs, inp):
        xi, bi, ci, ai = inp
        acum = jnp.cumsum(ai, axis=1)
        diff = acum[:, :, None] - acum[:, None]
        lmat = jnp.exp(jnp.where(causal[None, :, :, None, None], diff, -jnp.inf))
        cb = jnp.einsum('bign,bjgn->bijg', ci, bi)
        y = jnp.einsum('bijg,bijgh,bjghp->bighp', cb, lmat, xi)
        y = y + jnp.einsum('bign,bghpn->bighp', ci, hs) * jnp.exp(acum)[..., None]
        alast = acum[:, -1]
        wdec = jnp.exp(alast[:, None] - acum)
        hs = hs * jnp.exp(alast)[..., None, None] + jnp.einsum('bjgn,bjgh,bjghp->bghpn', bi, wdec, xi)
        return hs, y

    h0 = jnp.zeros((bsz, SSD_GROUPS, hg, SSD_P, SSD_N), F32)
    _, y = lax.scan(step, h0, (to_chunks(xdt), to_chunks(Bm), to_chunks(Cm), to_chunks(a)))
    y = from_chunks(y).reshape(bsz, seq, SSD_HEADS, SSD_P)
    return y + x * D.astype(F32)[:, None]


def retention(q, k, v):
    bsz = q.shape[0]
    lg = jnp.log(1.0 - 2.0 ** (-5.0 - jnp.arange(RET_HEADS, dtype=F32)))
    idx = jnp.arange(CHUNK, dtype=F32)
    rel = idx[:, None] - idx[None, :]
    dmat = jnp.where(rel[None] >= 0, jnp.exp(jnp.maximum(rel, 0.0)[None] * lg[:, None, None]), 0.0)
    qdec = jnp.exp((idx[:, None] + 1.0) * lg[None, :])
    kdec = jnp.exp((CHUNK - 1.0 - idx)[:, None] * lg[None, :])
    cdec = jnp.exp(CHUNK * lg)
    k = k * (RET_DK ** -0.5)

    def step(R, inp):
        qi, ki, vi = inp
        s = jnp.einsum('bihd,bjhd->bhij', qi, ki) * dmat
        o = (jnp.einsum('bhij,bjhv->bihv', s, vi)
             + jnp.einsum('bihd,bhdv->bihv', qi, R) * qdec[None, :, :, None])
        R = R * cdec[None, :, None, None] + jnp.einsum('bjhd,bjhv->bhdv', ki * kdec[None, :, :, None], vi)
        return R, o

    r0 = jnp.zeros((bsz, RET_HEADS, RET_DK, RET_DV), F32)
    _, o = lax.scan(step, r0, (to_chunks(q), to_chunks(k), to_chunks(v)))
    return from_chunks(o)


def hybrid_layer(x, pre_norm, post_norm, w_in, gdn_conv, gdn_A_log, gdn_dt_bias, gdn_norm,
                 ssd_conv, ssd_conv_b, ssd_A_log, ssd_dt_bias, ssd_D, ssd_norm, ret_norm, w_out, pos):
    bsz, seq, _ = x.shape
    h = rmsnorm(x, pre_norm)
    proj = jnp.einsum('bld,de->ble', h, w_in.astype(F32))
    (gq, gk, gv, gz, gb, ga, sx, sB, sC, sz, sdt, rq, rk, rv, rg) = split_cols(proj, IN_SIZES)

    qkv = jax.nn.silu(causal_conv(jnp.concatenate([gq, gk, gv], axis=-1), gdn_conv))
    gq, gk, gv = split_cols(qkv, [GDN_HEADS * GDN_DK, GDN_HEADS * GDN_DK, GDN_W])
    o_a = gated_deltanet(gq.reshape(bsz, seq, GDN_HEADS, GDN_DK), gk.reshape(bsz, seq, GDN_HEADS, GDN_DK),
                         gv.reshape(bsz, seq, GDN_HEADS, GDN_DV), gb, ga, gdn_A_log, gdn_dt_bias)
    o_a = rmsnorm(o_a, gdn_norm) * jax.nn.silu(gz.reshape(bsz, seq, GDN_HEADS, GDN_DV))

    xbc = jax.nn.silu(causal_conv(jnp.concatenate([sx, sB, sC], axis=-1), ssd_conv) + ssd_conv_b.astype(F32))
    sx, sB, sC = split_cols(xbc, [SSD_W, SSD_GROUPS * SSD_N, SSD_GROUPS * SSD_N])
    y_b = ssd(sx.reshape(bsz, seq, SSD_HEADS, SSD_P), sB.reshape(bsz, seq, SSD_GROUPS, SSD_N),
              sC.reshape(bsz, seq, SSD_GROUPS, SSD_N), sdt, ssd_A_log, ssd_dt_bias, ssd_D)
    y_b = y_b * jax.nn.silu(sz.reshape(bsz, seq, SSD_HEADS, SSD_P))
    o_b = rmsnorm(y_b.reshape(bsz, seq, SSD_GROUPS, SSD_W // SSD_GROUPS),
                  ssd_norm.reshape(SSD_GROUPS, SSD_W // SSD_GROUPS))

    q_c = rotary(rq.reshape(bsz, seq, RET_HEADS, RET_DK), pos)
    k_c = rotary(rk.reshape(bsz, seq, RET_HEADS, RET_DK), pos)
    o_c = retention(q_c, k_c, rv.reshape(bsz, seq, RET_HEADS, RET_DV))
    o_c = rmsnorm(o_c, ret_norm) * jax.nn.silu(rg.reshape(bsz, seq, RET_HEADS, RET_DV))

    mixed = jnp.concatenate([o_a.reshape(bsz, seq, GDN_W), o_b.reshape(bsz, seq, SSD_W),
                             o_c.reshape(bsz, seq, RET_W)], axis=-1)
    out = jnp.einsum('ble,ed->bld', mixed, w_out.astype(F32))
    return x + rmsnorm(out, post_norm).astype(x.dtype)


def _fwd_setup_inputs(seed: int = 0) -> dict:
    key = jax.random.key(seed)
    ks = jax.random.split(key, 16)
    nrm = jax.random.normal

    def inv_softplus_dt(k, shape):
        dt = jnp.exp(jax.random.uniform(k, shape, F32, np.log(1e-3), np.log(1e-1)))
        return dt + jnp.log(-jnp.expm1(-dt))

    return {
        "x": nrm(ks[0], (BATCH, SEQ, D_MODEL), F32),
        "pre_norm": 1.0 + 0.05 * nrm(ks[1], (DEPTH, D_MODEL), F32),
        "post_norm": 1.0 + 0.05 * nrm(ks[2], (DEPTH, D_MODEL), F32),
        "w_in": nrm(ks[3], (DEPTH, D_MODEL, N_IN), F32) * D_MODEL ** -0.5,
        "gdn_conv": nrm(ks[4], (DEPTH, CONV_W, GDN_CONV_CH), F32) * CONV_W ** -0.5,
        "gdn_A_log": jnp.log(jax.random.uniform(ks[5], (DEPTH, GDN_HEADS), F32, 1.0, 16.0)),
        "gdn_dt_bias": inv_softplus_dt(ks[6], (DEPTH, GDN_HEADS)),
        "gdn_norm": 1.0 + 0.05 * nrm(ks[7], (DEPTH, GDN_DV), F32),
        "ssd_conv": nrm(ks[8], (DEPTH, CONV_W, SSD_CONV_CH), F32) * CONV_W ** -0.5,
        "ssd_conv_b": 0.02 * nrm(ks[9], (DEPTH, SSD_CONV_CH), F32),
        "ssd_A_log": jnp.log(jax.random.uniform(ks[10], (DEPTH, SSD_HEADS), F32, 1.0, 16.0)),
        "ssd_dt_bias": inv_softplus_dt(ks[11], (DEPTH, SSD_HEADS)),
        "ssd_D": 1.0 + 0.1 * nrm(ks[12], (DEPTH, SSD_HEADS), F32),
        "ssd_norm": 1.0 + 0.05 * nrm(ks[13], (DEPTH, SSD_W), F32),
        "ret_norm": 1.0 + 0.05 * nrm(ks[14], (DEPTH, RET_DV), F32),
        "w_out": nrm(ks[15], (DEPTH, MIX_W, D_MODEL), F32) * MIX_W ** -0.5,
    }


def _fwd_reference(x, pre_norm, post_norm, w_in, gdn_conv, gdn_A_log, gdn_dt_bias, gdn_norm,
              ssd_conv, ssd_conv_b, ssd_A_log, ssd_dt_bias, ssd_D, ssd_norm, ret_norm, w_out):
    pos = jnp.arange(x.shape[1], dtype=jnp.int32)
    for l in range(DEPTH):
        x = hybrid_layer(x, pre_norm[l], post_norm[l], w_in[l], gdn_conv[l], gdn_A_log[l], gdn_dt_bias[l],
                         gdn_norm[l], ssd_conv[l], ssd_conv_b[l], ssd_A_log[l], ssd_dt_bias[l], ssd_D[l],
                         ssd_norm[l], ret_norm[l], w_out[l], pos)
    return x


import jax as _jax
import jax.numpy as _jnp

TWIN_FORMAT = 'train_step'
FWD_PARAMS = ['x', 'pre_norm', 'post_norm', 'w_in', 'gdn_conv', 'gdn_A_log', 'gdn_dt_bias', 'gdn_norm', 'ssd_conv', 'ssd_conv_b', 'ssd_A_log', 'ssd_dt_bias', 'ssd_D', 'ssd_norm', 'ret_norm', 'w_out']
TWIN_WEIGHTS = ['pre_norm', 'post_norm', 'w_in', 'gdn_conv', 'gdn_A_log', 'gdn_dt_bias', 'gdn_norm', 'ssd_conv', 'ssd_conv_b', 'ssd_A_log', 'ssd_dt_bias', 'ssd_D', 'ssd_norm', 'ret_norm', 'w_out']
TWIN_DIFF_INPUT = 'x'
TWIN_INPUTS = ['x', 'pre_norm', 'post_norm', 'w_in', 'gdn_conv', 'gdn_A_log', 'gdn_dt_bias', 'gdn_norm', 'ssd_conv', 'ssd_conv_b', 'ssd_A_log', 'ssd_dt_bias', 'ssd_D', 'ssd_norm', 'ret_norm', 'w_out', 'loss_target', 'm_pre_norm', 'm_post_norm', 'm_w_in', 'm_gdn_conv', 'm_gdn_A_log', 'm_gdn_dt_bias', 'm_gdn_norm', 'm_ssd_conv', 'm_ssd_conv_b', 'm_ssd_A_log', 'm_ssd_dt_bias', 'm_ssd_D', 'm_ssd_norm', 'm_ret_norm', 'm_w_out', 'v_pre_norm', 'v_post_norm', 'v_w_in', 'v_gdn_conv', 'v_gdn_A_log', 'v_gdn_dt_bias', 'v_gdn_norm', 'v_ssd_conv', 'v_ssd_conv_b', 'v_ssd_A_log', 'v_ssd_dt_bias', 'v_ssd_D', 'v_ssd_norm', 'v_ret_norm', 'v_w_out']
TWIN_OUTPUTS = ['loss', 'grad_x', 'grad_pre_norm', 'grad_post_norm', 'grad_w_in', 'grad_gdn_conv', 'grad_gdn_A_log', 'grad_gdn_dt_bias', 'grad_gdn_norm', 'grad_ssd_conv', 'grad_ssd_conv_b', 'grad_ssd_A_log', 'grad_ssd_dt_bias', 'grad_ssd_D', 'grad_ssd_norm', 'grad_ret_norm', 'grad_w_out', 'delta_pre_norm', 'delta_post_norm', 'delta_w_in', 'delta_gdn_conv', 'delta_gdn_A_log', 'delta_gdn_dt_bias', 'delta_gdn_norm', 'delta_ssd_conv', 'delta_ssd_conv_b', 'delta_ssd_A_log', 'delta_ssd_dt_bias', 'delta_ssd_D', 'delta_ssd_norm', 'delta_ret_norm', 'delta_w_out', 'new_m_pre_norm', 'new_m_post_norm', 'new_m_w_in', 'new_m_gdn_conv', 'new_m_gdn_A_log', 'new_m_gdn_dt_bias', 'new_m_gdn_norm', 'new_m_ssd_conv', 'new_m_ssd_conv_b', 'new_m_ssd_A_log', 'new_m_ssd_dt_bias', 'new_m_ssd_D', 'new_m_ssd_norm', 'new_m_ret_norm', 'new_m_w_out', 'new_v_pre_norm', 'new_v_post_norm', 'new_v_w_in', 'new_v_gdn_conv', 'new_v_gdn_A_log', 'new_v_gdn_dt_bias', 'new_v_gdn_norm', 'new_v_ssd_conv', 'new_v_ssd_conv_b', 'new_v_ssd_A_log', 'new_v_ssd_dt_bias', 'new_v_ssd_D', 'new_v_ssd_norm', 'new_v_ret_norm', 'new_v_w_out']
TWIN_LEAF_KINDS = {'loss': 'loss', 'grad_x': 'grad_x', 'grad_pre_norm': 'grad_w', 'grad_post_norm': 'grad_w', 'grad_w_in': 'grad_w', 'grad_gdn_conv': 'grad_w', 'grad_gdn_A_log': 'grad_w', 'grad_gdn_dt_bias': 'grad_w', 'grad_gdn_norm': 'grad_w', 'grad_ssd_conv': 'grad_w', 'grad_ssd_conv_b': 'grad_w', 'grad_ssd_A_log': 'grad_w', 'grad_ssd_dt_bias': 'grad_w', 'grad_ssd_D': 'grad_w', 'grad_ssd_norm': 'grad_w', 'grad_ret_norm': 'grad_w', 'grad_w_out': 'grad_w', 'delta_pre_norm': 'delta_w', 'delta_post_norm': 'delta_w', 'delta_w_in': 'delta_w', 'delta_gdn_conv': 'delta_w', 'delta_gdn_A_log': 'delta_w', 'delta_gdn_dt_bias': 'delta_w', 'delta_gdn_norm': 'delta_w', 'delta_ssd_conv': 'delta_w', 'delta_ssd_conv_b': 'delta_w', 'delta_ssd_A_log': 'delta_w', 'delta_ssd_dt_bias': 'delta_w', 'delta_ssd_D': 'delta_w', 'delta_ssd_norm': 'delta_w', 'delta_ret_norm': 'delta_w', 'delta_w_out': 'delta_w', 'new_m_pre_norm': 'new_m', 'new_m_post_norm': 'new_m', 'new_m_w_in': 'new_m', 'new_m_gdn_conv': 'new_m', 'new_m_gdn_A_log': 'new_m', 'new_m_gdn_dt_bias': 'new_m', 'new_m_gdn_norm': 'new_m', 'new_m_ssd_conv': 'new_m', 'new_m_ssd_conv_b': 'new_m', 'new_m_ssd_A_log': 'new_m', 'new_m_ssd_dt_bias': 'new_m', 'new_m_ssd_D': 'new_m', 'new_m_ssd_norm': 'new_m', 'new_m_ret_norm': 'new_m', 'new_m_w_out': 'new_m', 'new_v_pre_norm': 'new_v', 'new_v_post_norm': 'new_v', 'new_v_w_in': 'new_v', 'new_v_gdn_conv': 'new_v', 'new_v_gdn_A_log': 'new_v', 'new_v_gdn_dt_bias': 'new_v', 'new_v_gdn_norm': 'new_v', 'new_v_ssd_conv': 'new_v', 'new_v_ssd_conv_b': 'new_v', 'new_v_ssd_A_log': 'new_v', 'new_v_ssd_dt_bias': 'new_v', 'new_v_ssd_D': 'new_v', 'new_v_ssd_norm': 'new_v', 'new_v_ret_norm': 'new_v', 'new_v_w_out': 'new_v'}


def _forward(args):
    return _fwd_reference(*[args[k] for k in FWD_PARAMS])


def _output_shape():
    out = _jax.eval_shape(lambda: _forward(_fwd_setup_inputs(0)))
    return out.shape, out.dtype

N_MICROBATCH = 1
ADAM_LR = 0.001
ADAM_B1 = 0.9
ADAM_B2 = 0.999
ADAM_EPS = 1e-08
ADAM_WD = 0.01
ADAM_STEP = 10
PER_EXAMPLE_BATCH_AXIS = {'x': 0, 'loss_target': 0}
SHARED_INPUTS = []
_WEIGHT_DTYPES = {'pre_norm': _jnp.float32, 'post_norm': _jnp.float32, 'w_in': _jnp.float32, 'gdn_conv': _jnp.float32, 'gdn_A_log': _jnp.float32, 'gdn_dt_bias': _jnp.float32, 'gdn_norm': _jnp.float32, 'ssd_conv': _jnp.float32, 'ssd_conv_b': _jnp.float32, 'ssd_A_log': _jnp.float32, 'ssd_dt_bias': _jnp.float32, 'ssd_D': _jnp.float32, 'ssd_norm': _jnp.float32, 'ret_norm': _jnp.float32, 'w_out': _jnp.float32}
MOMENT_SCALE = {'pre_norm': 9.733165e-01, 'post_norm': 6.397206e+01, 'w_in': 4.194637e-01, 'gdn_conv': 3.370524e-01, 'gdn_A_log': 2.597771e+00, 'gdn_dt_bias': 2.531148e+00, 'gdn_norm': 1.989052e+00, 'ssd_conv': 8.835521e-01, 'ssd_conv_b': 3.656260e+00, 'ssd_A_log': 8.754562e+00, 'ssd_dt_bias': 1.593543e+00, 'ssd_D': 6.556549e+00, 'ssd_norm': 1.657786e+00, 'ret_norm': 5.943003e-01, 'w_out': 1.683692e+00}


def _to_microbatches(a, axis):
    t = _jnp.moveaxis(a, axis, 0)
    t = t.reshape((N_MICROBATCH, t.shape[0] // N_MICROBATCH) + t.shape[1:])
    return _jnp.moveaxis(t, 1, axis + 1)


def setup_inputs(seed: int = 0) -> dict:
    inp = _fwd_setup_inputs(seed)
    key = _jax.random.fold_in(_jax.random.key(seed), 7919)
    shape, _ = _output_shape()
    out = dict(inp)
    out["loss_target"] = _jax.random.normal(_jax.random.fold_in(key, 0), shape, _jnp.float32)
    for i, name in enumerate(TWIN_WEIGHTS):
        w = inp[name].astype(_jnp.float32)
        if MOMENT_SCALE is None:
            s = _jnp.sqrt(_jnp.mean(_jnp.square(w)) + 1e-30)
        else:
            s = MOMENT_SCALE[name]
        km, kv = _jax.random.split(_jax.random.fold_in(key, i + 1))
        out[name] = w
        out["m_" + name] = s * _jax.random.normal(km, w.shape, _jnp.float32)
        out["v_" + name] = (s * s) * _jax.random.uniform(kv, w.shape, _jnp.float32, 0.5, 1.5)
    if N_MICROBATCH > 1:
        for name, axis in PER_EXAMPLE_BATCH_AXIS.items():
            out[name] = _to_microbatches(out[name], axis)
    return {'x': out['x'], 'pre_norm': out['pre_norm'], 'post_norm': out['post_norm'], 'w_in': out['w_in'], 'gdn_conv': out['gdn_conv'], 'gdn_A_log': out['gdn_A_log'], 'gdn_dt_bias': out['gdn_dt_bias'], 'gdn_norm': out['gdn_norm'], 'ssd_conv': out['ssd_conv'], 'ssd_conv_b': out['ssd_conv_b'], 'ssd_A_log': out['ssd_A_log'], 'ssd_dt_bias': out['ssd_dt_bias'], 'ssd_D': out['ssd_D'], 'ssd_norm': out['ssd_norm'], 'ret_norm': out['ret_norm'], 'w_out': out['w_out'], 'loss_target': out['loss_target'], 'm_pre_norm': out['m_pre_norm'], 'm_post_norm': out['m_post_norm'], 'm_w_in': out['m_w_in'], 'm_gdn_conv': out['m_gdn_conv'], 'm_gdn_A_log': out['m_gdn_A_log'], 'm_gdn_dt_bias': out['m_gdn_dt_bias'], 'm_gdn_norm': out['m_gdn_norm'], 'm_ssd_conv': out['m_ssd_conv'], 'm_ssd_conv_b': out['m_ssd_conv_b'], 'm_ssd_A_log': out['m_ssd_A_log'], 'm_ssd_dt_bias': out['m_ssd_dt_bias'], 'm_ssd_D': out['m_ssd_D'], 'm_ssd_norm': out['m_ssd_norm'], 'm_ret_norm': out['m_ret_norm'], 'm_w_out': out['m_w_out'], 'v_pre_norm': out['v_pre_norm'], 'v_post_norm': out['v_post_norm'], 'v_w_in': out['v_w_in'], 'v_gdn_conv': out['v_gdn_conv'], 'v_gdn_A_log': out['v_gdn_A_log'], 'v_gdn_dt_bias': out['v_gdn_dt_bias'], 'v_gdn_norm': out['v_gdn_norm'], 'v_ssd_conv': out['v_ssd_conv'], 'v_ssd_conv_b': out['v_ssd_conv_b'], 'v_ssd_A_log': out['v_ssd_A_log'], 'v_ssd_dt_bias': out['v_ssd_dt_bias'], 'v_ssd_D': out['v_ssd_D'], 'v_ssd_norm': out['v_ssd_norm'], 'v_ret_norm': out['v_ret_norm'], 'v_w_out': out['v_w_out']}


def _loss(weights, diff, rest, loss_target):
    with _jax.named_scope("forward"):
        args = {**rest, TWIN_DIFF_INPUT: diff, **{k: w.astype(_WEIGHT_DTYPES[k]) for k, w in weights.items()}}
        y = _forward(args)
    with _jax.named_scope("loss_head"):
        err = _jnp.square(y.astype(_jnp.float32) - loss_target)
        return 0.5 * _jnp.sum(_jnp.mean(err, axis=-1)) if err.ndim else 0.5 * err


def _adamw(w, g, m, v):
    m = ADAM_B1 * m + (1.0 - ADAM_B1) * g
    v = ADAM_B2 * v + (1.0 - ADAM_B2) * _jnp.square(g)
    m_hat = m / (1.0 - ADAM_B1 ** ADAM_STEP)
    v_hat = v / (1.0 - ADAM_B2 ** ADAM_STEP)
    delta = -ADAM_LR * (m_hat / (_jnp.sqrt(v_hat) + ADAM_EPS) + ADAM_WD * w)
    return delta, m, v


def reference(x, pre_norm, post_norm, w_in, gdn_conv, gdn_A_log, gdn_dt_bias, gdn_norm, ssd_conv, ssd_conv_b, ssd_A_log, ssd_dt_bias, ssd_D, ssd_norm, ret_norm, w_out, loss_target, m_pre_norm, m_post_norm, m_w_in, m_gdn_conv, m_gdn_A_log, m_gdn_dt_bias, m_gdn_norm, m_ssd_conv, m_ssd_conv_b, m_ssd_A_log, m_ssd_dt_bias, m_ssd_D, m_ssd_norm, m_ret_norm, m_w_out, v_pre_norm, v_post_norm, v_w_in, v_gdn_conv, v_gdn_A_log, v_gdn_dt_bias, v_gdn_norm, v_ssd_conv, v_ssd_conv_b, v_ssd_A_log, v_ssd_dt_bias, v_ssd_D, v_ssd_norm, v_ret_norm, v_w_out):
    given = dict(x=x, pre_norm=pre_norm, post_norm=post_norm, w_in=w_in, gdn_conv=gdn_conv, gdn_A_log=gdn_A_log, gdn_dt_bias=gdn_dt_bias, gdn_norm=gdn_norm, ssd_conv=ssd_conv, ssd_conv_b=ssd_conv_b, ssd_A_log=ssd_A_log, ssd_dt_bias=ssd_dt_bias, ssd_D=ssd_D, ssd_norm=ssd_norm, ret_norm=ret_norm, w_out=w_out, loss_target=loss_target, m_pre_norm=m_pre_norm, m_post_norm=m_post_norm, m_w_in=m_w_in, m_gdn_conv=m_gdn_conv, m_gdn_A_log=m_gdn_A_log, m_gdn_dt_bias=m_gdn_dt_bias, m_gdn_norm=m_gdn_norm, m_ssd_conv=m_ssd_conv, m_ssd_conv_b=m_ssd_conv_b, m_ssd_A_log=m_ssd_A_log, m_ssd_dt_bias=m_ssd_dt_bias, m_ssd_D=m_ssd_D, m_ssd_norm=m_ssd_norm, m_ret_norm=m_ret_norm, m_w_out=m_w_out, v_pre_norm=v_pre_norm, v_post_norm=v_post_norm, v_w_in=v_w_in, v_gdn_conv=v_gdn_conv, v_gdn_A_log=v_gdn_A_log, v_gdn_dt_bias=v_gdn_dt_bias, v_gdn_norm=v_gdn_norm, v_ssd_conv=v_ssd_conv, v_ssd_conv_b=v_ssd_conv_b, v_ssd_A_log=v_ssd_A_log, v_ssd_dt_bias=v_ssd_dt_bias, v_ssd_D=v_ssd_D, v_ssd_norm=v_ssd_norm, v_ret_norm=v_ret_norm, v_w_out=v_w_out)
    weights = {n: given[n] for n in TWIN_WEIGHTS}
    shared = {n: given[n] for n in SHARED_INPUTS}
    per_example = {n: given[n] for n in ['x']}
    grad_fn = _jax.value_and_grad(_loss, argnums=(0, 1))

    def one_microbatch(ex, loss_target):
        ex = dict(ex)
        diff = ex.pop(TWIN_DIFF_INPUT)
        return grad_fn(weights, diff, {**shared, **ex}, loss_target)

    if N_MICROBATCH == 1:
        loss, (grad_w, grad_x) = one_microbatch(per_example, given["loss_target"])
    else:
        def body(carry, xs):
            loss_sum, grad_sum = carry
            l_k, (gw_k, gx_k) = one_microbatch(xs[0], xs[1])
            with _jax.named_scope("update"):
                return (loss_sum + l_k, _jax.tree.map(_jnp.add, grad_sum, gw_k)), gx_k

        init = (_jnp.zeros((), _jnp.float32), _jax.tree.map(_jnp.zeros_like, weights))
        (loss, grad_w), grad_x = _jax.lax.scan(body, init, (per_example, given["loss_target"]))
    with _jax.named_scope("update"):
        delta_w, new_m, new_v = {}, {}, {}
        for n in TWIN_WEIGHTS:
            delta_w[n], new_m[n], new_v[n] = _adamw(weights[n], grad_w[n], given["m_" + n], given["v_" + n])
    return (loss, grad_x, *[grad_w[n] for n in TWIN_WEIGHTS], *[delta_w[n] for n in TWIN_WEIGHTS],
            *[new_m[n] for n in TWIN_WEIGHTS], *[new_v[n] for n in TWIN_WEIGHTS])
```

```python
import functools
import math

import numpy as np
import jax
import jax.numpy as jnp
from jax import lax
from jax.experimental import pallas as pl
from jax.experimental.pallas import tpu as pltpu

F32 = jnp.float32
BF16 = jnp.bfloat16

D_MODEL = 1024
DEPTH = 2
CHUNK = 64
CONV_W = 4
EPS = 1e-6
N_DEV = 8

GDN_HEADS = 4
GDN_DK = 128
SSD_HEADS = 16
SSD_P = 64
SSD_N = 128
SSD_GROUPS = 2
SSD_PAIRS = SSD_HEADS // 2
PAIRS_PER_GROUP = SSD_PAIRS // SSD_GROUPS
RET_HEADS = 4
RET_DK = 128
ROPE_BASE = 10000.0
MIX_W = 2048
N_IN = 6680
SHARD_IN = N_IN // N_DEV
SHARD_OUT = MIX_W // N_DEV
CONV_CH = 1536
SHARD_CONV = CONV_CH // N_DEV

ADAM_LR = 0.001
ADAM_B1 = 0.9
ADAM_B2 = 0.999
ADAM_EPS = 1e-08
ADAM_WD = 0.01
ADAM_STEP = 10

LANE = 128
SUBLANE = 8
VMEM_LIMIT = 56 * 1024 * 1024

_ORIG = dict(gq=(0, 512), gk=(512, 512), gv=(1024, 512), gz=(1536, 512), gb=(2048, 4), ga=(2052, 4),
             sx=(2056, 1024), sB=(3080, 256), sC=(3336, 256), sz=(3592, 1024), sdt=(4616, 16),
             rq=(4632, 512), rk=(5144, 512), rv=(5656, 512), rg=(6168, 512))
_ORDER = ["gq", "gk", "gv", "sx", "sB", "sC", "gz", "sz", "rq", "rk", "rv", "rg", "gb", "ga", "sdt"]
_MINE = {}
_off = 0
for _n in _ORDER:
    _MINE[_n] = _off
    _off += _ORIG[_n][1]
N_USED = _off
N_PAD = 7168
CONV_ALL = 2 * CONV_CH
SMALL_OFF = _MINE["gb"]
CB = lambda name: _MINE[name] // LANE


def _cparams(sem, vmem=None):
    return pltpu.CompilerParams(dimension_semantics=sem, vmem_limit_bytes=vmem)


def _make_mm(highest):
    def raw(a, b, ca, cb):
        if highest:
            return lax.dot_general(a, b, (((ca,), (cb,)), ((), ())), precision=lax.Precision.HIGHEST,
                                   preferred_element_type=F32)
        return lax.dot_general(a.astype(BF16), b.astype(BF16), (((ca,), (cb,)), ((), ())),
                               preferred_element_type=F32)

    @jax.custom_vjp
    def nn(a, b):
        return raw(a, b, 1, 0)

    @jax.custom_vjp
    def nt(a, b):
        return raw(a, b, 1, 1)

    @jax.custom_vjp
    def tn(a, b):
        return raw(a, b, 0, 0)

    nn.defvjp(lambda a, b: (raw(a, b, 1, 0), (a, b)), lambda r, g: (nt(g, r[1]), tn(r[0], g)))
    nt.defvjp(lambda a, b: (raw(a, b, 1, 1), (a, b)), lambda r, g: (nn(g, r[1]), tn(g, r[0])))
    tn.defvjp(lambda a, b: (raw(a, b, 0, 0), (a, b)), lambda r, g: (nt(r[1], g), nn(r[0], g)))
    return nn, nt, tn


_nn, _nt, _tn = _make_mm(False)
_nnH, _ntH, _tnH = _make_mm(True)


@jax.custom_vjp
def _swap_halves(t):
    return pltpu.roll(t, LANE // 2, 1)


_swap_halves.defvjp(lambda t: (pltpu.roll(t, LANE // 2, 1), None),
                    lambda _, g: (pltpu.roll(g, LANE // 2, 1),))


@jax.custom_vjp
def _tri_inv(a):
    return _tri_inv_impl(a)


def _tri_inv_impl(a):
    ii = lax.broadcasted_iota(jnp.int32, a.shape, 0)
    jj = lax.broadcasted_iota(jnp.int32, a.shape, 1)
    n = -a
    t = jnp.where(ii == jj, 1.0, 0.0).astype(F32) + n
    p = n
    steps = int(math.log2(a.shape[0])) - 1
    for _ in range(steps):
        p = _nnH(p, p)
        t = t + _nnH(t, p)
    return t


_tri_inv.defvjp(lambda a: (lambda t: (t, t))(_tri_inv_impl(a)),
                lambda t, g: (-_ntH(_tnH(t, g), t),))


def _silu(x):
    return x * jax.nn.sigmoid(x)


@jax.custom_vjp
def _softplus(x):
    return jnp.maximum(x, 0.0) + jnp.log1p(jnp.exp(-jnp.abs(x)))


_softplus.defvjp(lambda x: (jnp.maximum(x, 0.0) + jnp.log1p(jnp.exp(-jnp.abs(x))), x),
                 lambda x, g: (g * jax.nn.sigmoid(x),))


def _rms(x, w):
    return x * lax.rsqrt(jnp.mean(x * x, axis=-1, keepdims=True) + EPS) * w


def _chunk_masks(n):
    ii = lax.broadcasted_iota(jnp.int32, (n, n), 0)
    jj = lax.broadcasted_iota(jnp.int32, (n, n), 1)
    return ii >= jj, ii > jj, ii == jj, ii <= jj


def _cumsum_col(g, causal, eye, upper):
    g_row = jnp.sum(jnp.where(eye, g, 0.0), axis=0, keepdims=True)
    col = jnp.sum(jnp.where(causal, g_row, 0.0), axis=1, keepdims=True)
    row = jnp.sum(jnp.where(upper, g, 0.0), axis=0, keepdims=True)
    return col, row


def _gdn_chunk(q, k, v, gz, b_raw, a_raw, alog, dtb, nw, s):
    causal, strict, eye, upper = _chunk_masks(q.shape[0])
    qn = q * lax.rsqrt(jnp.sum(q * q, axis=-1, keepdims=True) + EPS) * (GDN_DK ** -0.5)
    kn = k * lax.rsqrt(jnp.sum(k * k, axis=-1, keepdims=True) + EPS)
    beta = jax.nn.sigmoid(b_raw)
    g = -jnp.exp(alog) * _softplus(a_raw + dtb)
    gcum, gcum_row = _cumsum_col(g, causal, eye, upper)
    decay = jnp.where(causal, jnp.exp(jnp.where(causal, gcum - gcum_row, 0.0)), 0.0)
    kb = kn * beta
    a_low = jnp.where(strict, _nt(kb, kn) * decay, 0.0)
    eg = jnp.exp(gcum)
    t = _tri_inv(a_low)
    u = _nn(t, v * beta)
    w = _nn(t, kb * eg)
    v_new = u - _nn(w, s)
    attn = _nt(qn, kn) * decay
    o = _nn(qn * eg, s) + _nn(attn, v_new)
    glast = jnp.sum(g, axis=0, keepdims=True)
    kdec = kn * jnp.exp(glast - gcum)
    s_new = s * jnp.exp(glast) + _tn(kdec, v_new)
    on = _rms(o, nw) * _silu(gz)
    return on, s_new


def _ssd_chunk(x, bm, cm, dtr, alog, dtb, dp, hs):
    c = x.shape[0]
    causal, _, eye, upper = _chunk_masks(c)
    lane_lo = lax.broadcasted_iota(jnp.int32, (1, LANE), 1) < SSD_P
    row_lo = lax.broadcasted_iota(jnp.int32, (LANE, 1), 0) < SSD_P
    sel0 = lax.broadcasted_iota(jnp.int32, (1, 2), 1) == 0
    dt = _softplus(dtr + dtb)
    a = dt * -jnp.exp(alog)
    cb = _nt(cm, bm)
    heads = []
    for h in range(2):
        sel = sel0 if h == 0 else jnp.logical_not(sel0)
        dt_h = jnp.sum(jnp.where(sel, dt, 0.0), axis=1, keepdims=True)
        a_h = jnp.sum(jnp.where(sel, a, 0.0), axis=1, keepdims=True)
        d_h = jnp.sum(jnp.where(sel, dp, 0.0), axis=1, keepdims=True)
        acum, acum_row = _cumsum_col(a_h, causal, eye, upper)
        lmat = jnp.where(causal, jnp.exp(jnp.where(causal, acum - acum_row, 0.0)), 0.0)
        alast = jnp.sum(a_h, axis=0, keepdims=True)
        heads.append((dt_h, d_h, acum, lmat, alast))
    pick = lambda i: jnp.where(lane_lo, heads[0][i], heads[1][i])
    xdt = x * pick(0)
    y = x * pick(1)
    for h in range(2):
        hm = lane_lo if h == 0 else jnp.logical_not(lane_lo)
        y = y + _nn(cb * heads[h][3], jnp.where(hm, xdt, 0.0))
    y = y + _nt(cm, hs) * jnp.exp(pick(2))
    wdec = jnp.exp(jnp.where(lane_lo, heads[0][4] - heads[0][2], heads[1][4] - heads[1][2]))
    scale = jnp.exp(jnp.where(row_lo, heads[0][4], heads[1][4]))
    hs_new = hs * scale + _tn(xdt * wdec, bm)
    return y, hs_new


def _ret_chunk(rq, rk, rv, rg, cos2, sin2, dmat, qdec, kdec, cdec, nw, r):
    q = rq * cos2 + _swap_halves(rq) * sin2
    k = (rk * cos2 + _swap_halves(rk) * sin2) * (RET_DK ** -0.5)
    s = _nt(q, k) * dmat
    o = _nn(s, rv) + _nn(q, r) * qdec
    r_new = r * cdec + _tn(k * kdec, rv)
    on = _rms(o, nw) * _silu(rg)
    return on, r_new


def _matmul(name, a, b, mode, m, n, k, tm, tn, tk):
    nk = k // tk
    ca, cb = {"nn": (1, 0), "nt": (1, 1), "tn": (0, 0)}[mode]

    def body(a_ref, b_ref, o_ref, acc_ref):
        kk = pl.program_id(2)

        @pl.when(kk == 0)
        def _():
            acc_ref[...] = jnp.zeros_like(acc_ref)

        acc_ref[...] += lax.dot_general(a_ref[...].astype(BF16), b_ref[...].astype(BF16),
                                        (((ca,), (cb,)), ((), ())), preferred_element_type=F32)

        @pl.when(kk == nk - 1)
        def _():
            o_ref[...] = acc_ref[...]

    a_spec = pl.BlockSpec((tk, tm), lambda i, j, kk: (kk, i)) if mode == "tn" else pl.BlockSpec((tm, tk), lambda i, j, kk: (i, kk))
    b_spec = pl.BlockSpec((tn, tk), lambda i, j, kk: (j, kk)) if mode == "nt" else pl.BlockSpec((tk, tn), lambda i, j, kk: (kk, j))
    return pl.pallas_call(
        body, name=name, grid=(m // tm, n // tn, nk),
        in_specs=[a_spec, b_spec], out_specs=pl.BlockSpec((tm, tn), lambda i, j, kk: (i, j)),
        out_shape=jax.ShapeDtypeStruct((m, n), F32), scratch_shapes=[pltpu.VMEM((tm, tn), F32)],
        compiler_params=_cparams(("parallel", "parallel", "arbitrary"), VMEM_LIMIT),
    )(a, b)


def _rowmap(name, fn, length, tm, rows, params, out_ws=(), acc_ws=(), out_dtypes=None, cts=None, nd_rows=()):
    nt_ = length // tm
    n_r, n_nd, n_p = len(rows), len(nd_rows), len(params)
    rspec = lambda bw, cbk: pl.BlockSpec((tm, bw), lambda i: (i, cbk))
    pspec = lambda w: pl.BlockSpec((1, w), lambda i: (0, 0))
    in_arrays = [r[0] for r in rows] + [r[0] for r in nd_rows] + list(params)
    in_specs = [rspec(r[1], r[2]) for r in rows] + [rspec(r[1], r[2]) for r in nd_rows] + [pspec(p.shape[1]) for p in params]

    def load(refs):
        return [r[...].astype(F32) for r in refs]

    if cts is None:
        n_o, n_a = len(out_ws), len(acc_ws)
        out_dtypes_ = out_dtypes or [F32] * n_o

        def body(*refs):
            ins = load(refs[:n_r + n_nd + n_p])
            outs, accs = fn(*ins)
            o_refs = refs[n_r + n_nd + n_p:]
            for o_ref, o in zip(o_refs[:n_o], outs):
                o_ref[...] = o.astype(o_ref.dtype)
            first = pl.program_id(0) == 0
            for a_ref, acc in zip(o_refs[n_o:], accs):
                @pl.when(first)
                def _(a_ref=a_ref):
                    a_ref[...] = jnp.zeros_like(a_ref)
                a_ref[...] += acc

        res = pl.pallas_call(
            body, name=name, grid=(nt_,), in_specs=in_specs,
            out_specs=[rspec(w, 0) for w in out_ws] + [pspec(w) for w in acc_ws],
            out_shape=[jax.ShapeDtypeStruct((length, w), dt) for w, dt in zip(out_ws, out_dtypes_)]
                      + [jax.ShapeDtypeStruct((1, w), F32) for w in acc_ws],
            compiler_params=_cparams(("arbitrary",), VMEM_LIMIT),
        )(*in_arrays)
        return res[:n_o], res[n_o:]

    n_c = len(cts)

    def body(*refs):
        ins = load(refs[:n_r + n_nd + n_p])
        ct_vals = load(refs[n_r + n_nd + n_p:n_r + n_nd + n_p + n_c])
        nd = ins[n_r:n_r + n_nd]
        f = lambda rs, ps: fn(*rs, *nd, *ps)[0]
        _, vjp = jax.vjp(f, ins[:n_r], ins[n_r + n_nd:])
        d_rows, d_params = vjp(tuple(ct_vals))
        o_refs = refs[n_r + n_nd + n_p + n_c:]
        for o_ref, d in zip(o_refs[:n_r], d_rows):
            o_ref[...] = d
        first = pl.program_id(0) == 0
        for a_ref, d in zip(o_refs[n_r:], d_params):
            @pl.when(first)
            def _(a_ref=a_ref):
                a_ref[...] = jnp.zeros_like(a_ref)
            a_ref[...] += d

    res = pl.pallas_call(
        body, name=name, grid=(nt_,),
        in_specs=in_specs + [rspec(c[1], c[2]) for c in cts],
        out_specs=[rspec(r[1], 0) for r in rows] + [pspec(p.shape[1]) for p in params],
        out_shape=[jax.ShapeDtypeStruct((length, r[1]), F32) for r in rows]
                  + [jax.ShapeDtypeStruct((1, p.shape[1]), F32) for p in params],
        compiler_params=_cparams(("arbitrary",), VMEM_LIMIT),
    )(*in_arrays, *[c[0] for c in cts])
    return res[:n_r], res[n_r:]


def _conv_shift(ext, k, tm, forward):
    s = CONV_W - 1 - k
    if forward:
        rolled = ext if s == 0 else pltpu.roll(ext, s, 0)
        return rolled[SUBLANE:, :]
    rolled = ext if s == 0 else pltpu.roll(ext, tm + SUBLANE - s, 0)
    return rolled[:tm, :]


def _conv_pre(x_ref, halo_ref, w_ref, b_ref, first, tm):
    halo = jnp.where(first, 0.0, halo_ref[...])
    ext = jnp.concatenate([halo, x_ref[...]], axis=0)
    w = w_ref[...]
    pre = b_ref[...] + jnp.zeros_like(x_ref[...])
    taps = []
    for k in range(CONV_W):
        tap = _conv_shift(ext, k, tm, True)
        taps.append(tap)
        pre = pre + tap * w[k:k + 1, :]
    return pre, taps


def _conv_fwd(proj, w, b, length, tm, tc):
    hb = tm // SUBLANE

    def body(x_ref, halo_ref, w_ref, b_ref, o_ref):
        pre, _ = _conv_pre(x_ref, halo_ref, w_ref, b_ref, pl.program_id(0) == 0, tm)
        o_ref[...] = _silu(pre)

    return pl.pallas_call(
        body, name="conv_fwd", grid=(length // tm, CONV_ALL // tc),
        in_specs=[pl.BlockSpec((tm, tc), lambda i, j: (i, j)),
                  pl.BlockSpec((SUBLANE, tc), lambda i, j: (jnp.maximum(i * hb - 1, 0), j)),
                  pl.BlockSpec((CONV_W, tc), lambda i, j: (0, j)),
                  pl.BlockSpec((1, tc), lambda i, j: (0, j))],
        out_specs=pl.BlockSpec((tm, tc), lambda i, j: (i, j)),
        out_shape=jax.ShapeDtypeStruct((length, CONV_ALL), F32),
        compiler_params=_cparams(("parallel", "parallel"), VMEM_LIMIT),
    )(proj, proj, w, b)


def _conv_bwd_pre(proj, w, b, dxc, length, tm, tc):
    hb = tm // SUBLANE

    def body(x_ref, halo_ref, w_ref, b_ref, dy_ref, dpre_ref, dw_ref, db_ref):
        i = pl.program_id(1)
        pre, taps = _conv_pre(x_ref, halo_ref, w_ref, b_ref, i == 0, tm)
        sg = jax.nn.sigmoid(pre)
        dpre = dy_ref[...] * (sg * (1.0 + pre * (1.0 - sg)))
        dpre_ref[...] = dpre

        @pl.when(i == 0)
        def _():
            dw_ref[...] = jnp.zeros_like(dw_ref)
            db_ref[...] = jnp.zeros_like(db_ref)

        for k in range(CONV_W):
            dw_ref[k:k + 1, :] += jnp.sum(dpre * taps[k], axis=0, keepdims=True)
        db_ref[...] += jnp.sum(dpre, axis=0, keepdims=True)

    return pl.pallas_call(
        body, name="conv_bwd_pre", grid=(CONV_ALL // tc, length // tm),
        in_specs=[pl.BlockSpec((tm, tc), lambda j, i: (i, j)),
                  pl.BlockSpec((SUBLANE, tc), lambda j, i: (jnp.maximum(i * hb - 1, 0), j)),
                  pl.BlockSpec((CONV_W, tc), lambda j, i: (0, j)),
                  pl.BlockSpec((1, tc), lambda j, i: (0, j)),
                  pl.BlockSpec((tm, tc), lambda j, i: (i, j))],
        out_specs=[pl.BlockSpec((tm, tc), lambda j, i: (i, j)),
                   pl.BlockSpec((CONV_W, tc), lambda j, i: (0, j)),
                   pl.BlockSpec((1, tc), lambda j, i: (0, j))],
        out_shape=[jax.ShapeDtypeStruct((length, CONV_ALL), F32),
                   jax.ShapeDtypeStruct((CONV_W, CONV_ALL), F32),
                   jax.ShapeDtypeStruct((1, CONV_ALL), F32)],
        compiler_params=_cparams(("parallel", "arbitrary"), VMEM_LIMIT),
    )(proj, proj, w, b, dxc)


def _conv_bwd_x(dpre, w, length, tm, tc):
    hb = tm // SUBLANE
    n_t = length // tm
    last_blk = length // SUBLANE - 1

    def body(d_ref, halo_ref, w_ref, o_ref):
        halo = jnp.where(pl.program_id(0) == n_t - 1, 0.0, halo_ref[...])
        ext = jnp.concatenate([d_ref[...], halo], axis=0)
        w = w_ref[...]
        acc = jnp.zeros_like(d_ref[...])
        for k in range(CONV_W):
            acc = acc + _conv_shift(ext, k, tm, False) * w[k:k + 1, :]
        o_ref[...] = acc

    return pl.pallas_call(
        body, name="conv_bwd_x", grid=(n_t, CONV_ALL // tc),
        in_specs=[pl.BlockSpec((tm, tc), lambda i, j: (i, j)),
                  pl.BlockSpec((SUBLANE, tc), lambda i, j: (jnp.minimum((i + 1) * hb, last_blk), j)),
                  pl.BlockSpec((CONV_W, tc), lambda i, j: (0, j))],
        out_specs=pl.BlockSpec((tm, tc), lambda i, j: (i, j)),
        out_shape=jax.ShapeDtypeStruct((length, CONV_ALL), F32),
        compiler_params=_cparams(("parallel", "parallel"), VMEM_LIMIT),
    )(dpre, dpre, w)


def _scan_specs(c, width, colmap):
    return pl.BlockSpec((c, width), colmap)


def _gdn_fwd(xc, proj, b_col, a_col, alog, dtb, nw, length):
    nc = length // CHUNK
    h_ = GDN_HEADS

    def body(q_ref, k_ref, v_ref, gz_ref, b_ref, a_ref, al_ref, dt_ref, nw_ref, on_ref, st_ref, s_ref):
        @pl.when(pl.program_id(1) == 0)
        def _():
            s_ref[...] = jnp.zeros_like(s_ref)

        st_ref[...] = s_ref[...]
        on, s_new = _gdn_chunk(q_ref[...], k_ref[...], v_ref[...], gz_ref[...], b_ref[...], a_ref[...],
                               al_ref[...], dt_ref[...], nw_ref[...], s_ref[...])
        on_ref[...] = on
        s_ref[...] = s_new

    blk = lambda cb0: pl.BlockSpec((CHUNK, LANE), lambda h, c: (c, cb0 + h))
    col = pl.BlockSpec((None, CHUNK, 1), lambda h, c: (h, c, 0))
    one = pl.BlockSpec((None, 1, 1), lambda h, c: (h, 0, 0))
    return pl.pallas_call(
        body, name="gdn_fwd", grid=(h_, nc),
        in_specs=[blk(CB("gq")), blk(CB("gk")), blk(CB("gv")), blk(CB("gz")), col, col, one, one,
                  pl.BlockSpec((1, LANE), lambda h, c: (0, 0))],
        out_specs=[pl.BlockSpec((CHUNK, LANE), lambda h, c: (c, h)),
                   pl.BlockSpec((None, None, LANE, LANE), lambda h, c: (h, c, 0, 0))],
        out_shape=[jax.ShapeDtypeStruct((length, h_ * LANE), F32),
                   jax.ShapeDtypeStruct((h_, nc, LANE, LANE), F32)],
        scratch_shapes=[pltpu.VMEM((LANE, LANE), F32)],
        compiler_params=_cparams(("parallel", "arbitrary"), VMEM_LIMIT),
    )(xc, xc, xc, proj, b_col, a_col, alog, dtb, nw)


def _gdn_bwd(xc, proj, b_col, a_col, alog, dtb, nw, st, dmix, length):
    nc = length // CHUNK
    h_ = GDN_HEADS

    def body(q_ref, k_ref, v_ref, gz_ref, b_ref, a_ref, al_ref, dt_ref, nw_ref, st_ref, do_ref,
             dq_ref, dk_ref, dv_ref, dgz_ref, db_ref, da_ref, dal_ref, ddt_ref, dnw_ref, ds_ref):
        first = pl.program_id(1) == 0

        @pl.when(first)
        def _():
            ds_ref[...] = jnp.zeros_like(ds_ref)
            dal_ref[...] = jnp.zeros_like(dal_ref)
            ddt_ref[...] = jnp.zeros_like(ddt_ref)
            dnw_ref[...] = jnp.zeros_like(dnw_ref)

        ins = (q_ref[...], k_ref[...], v_ref[...], gz_ref[...], b_ref[...], a_ref[...], al_ref[...],
               dt_ref[...], nw_ref[...], st_ref[...])
        _, vjp = jax.vjp(_gdn_chunk, *ins)
        dq, dk, dv, dgz, db, da, dal, ddt, dnw, ds = vjp((do_ref[...], ds_ref[...]))
        dq_ref[...] = dq
        dk_ref[...] = dk
        dv_ref[...] = dv
        dgz_ref[...] = dgz
        db_ref[...] = db
        da_ref[...] = da
        dal_ref[...] += dal
        ddt_ref[...] += ddt
        dnw_ref[...] += dnw
        ds_ref[...] = ds

    r = lambda c: nc - 1 - c
    blk = lambda cb0: pl.BlockSpec((CHUNK, LANE), lambda h, c: (r(c), cb0 + h))
    col = pl.BlockSpec((None, CHUNK, 1), lambda h, c: (h, r(c), 0))
    one = pl.BlockSpec((None, 1, 1), lambda h, c: (h, 0, 0))
    oblk = pl.BlockSpec((CHUNK, LANE), lambda h, c: (r(c), h))
    big = jax.ShapeDtypeStruct((length, h_ * LANE), F32)
    return pl.pallas_call(
        body, name="gdn_bwd", grid=(h_, nc),
        in_specs=[blk(CB("gq")), blk(CB("gk")), blk(CB("gv")), blk(CB("gz")), col, col, one, one,
                  pl.BlockSpec((1, LANE), lambda h, c: (0, 0)),
                  pl.BlockSpec((None, None, LANE, LANE), lambda h, c: (h, r(c), 0, 0)),
                  oblk],
        out_specs=[oblk, oblk, oblk, oblk, col, col, one, one,
                   pl.BlockSpec((None, 1, LANE), lambda h, c: (h, 0, 0))],
        out_shape=[big, big, big, big,
                   jax.ShapeDtypeStruct((h_, length, 1), F32), jax.ShapeDtypeStruct((h_, length, 1), F32),
                   jax.ShapeDtypeStruct((h_, 1, 1), F32), jax.ShapeDtypeStruct((h_, 1, 1), F32),
                   jax.ShapeDtypeStruct((h_, 1, LANE), F32)],
        scratch_shapes=[pltpu.VMEM((LANE, LANE), F32)],
        compiler_params=_cparams(("parallel", "arbitrary"), VMEM_LIMIT),
    )(xc, xc, xc, proj, b_col, a_col, alog, dtb, nw, st, dmix)


def _ssd_fwd(xc, dt_col, alog, dtb, dpar, length):
    nc = length // CHUNK
    ppg = PAIRS_PER_GROUP

    gw = ppg * LANE

    def body(x_ref, b_ref, c_ref, dt_ref, al_ref, db_ref, dp_ref, y_ref, st_ref, hs_ref):
        @pl.when(pl.program_id(1) == 0)
        def _():
            hs_ref[...] = jnp.zeros_like(hs_ref)

        bm, cm = b_ref[...], c_ref[...]
        for p in range(ppg):
            st_ref[p] = hs_ref[p]
            y, hs_new = _ssd_chunk(x_ref[:, p * LANE:(p + 1) * LANE], bm, cm, dt_ref[p], al_ref[p], db_ref[p],
                                   dp_ref[p], hs_ref[p])
            y_ref[:, p * LANE:(p + 1) * LANE] = y
            hs_ref[p] = hs_new

    two = pl.BlockSpec((None, ppg, 1, 2), lambda g, c: (g, 0, 0, 0))
    return pl.pallas_call(
        body, name="ssd_fwd", grid=(SSD_GROUPS, nc),
        in_specs=[pl.BlockSpec((CHUNK, gw), lambda g, c: (c, _MINE["sx"] // gw + g)),
                  pl.BlockSpec((CHUNK, LANE), lambda g, c: (c, CB("sB") + g)),
                  pl.BlockSpec((CHUNK, LANE), lambda g, c: (c, CB("sC") + g)),
                  pl.BlockSpec((None, ppg, CHUNK, 2), lambda g, c: (g, 0, c, 0)),
                  two, two, two],
        out_specs=[pl.BlockSpec((CHUNK, gw), lambda g, c: (c, g)),
                   pl.BlockSpec((None, ppg, None, LANE, SSD_N), lambda g, c: (g, 0, c, 0, 0))],
        out_shape=[jax.ShapeDtypeStruct((length, SSD_GROUPS * gw), F32),
                   jax.ShapeDtypeStruct((SSD_GROUPS, ppg, nc, LANE, SSD_N), F32)],
        scratch_shapes=[pltpu.VMEM((ppg, LANE, SSD_N), F32)],
        compiler_params=_cparams(("parallel", "arbitrary"), VMEM_LIMIT),
    )(xc, xc, xc, dt_col, alog, dtb, dpar)


def _ssd_bwd(xc, dt_col, alog, dtb, dpar, st, dy, length):
    nc = length // CHUNK
    ppg = PAIRS_PER_GROUP

    gw = ppg * LANE

    def body(x_ref, b_ref, c_ref, dt_ref, al_ref, db_ref, dp_ref, st_ref, dy_ref,
             dx_ref, dbm_ref, dcm_ref, ddt_ref, dal_ref, ddb_ref, ddp_ref, dhs_ref):
        @pl.when(pl.program_id(1) == 0)
        def _():
            dhs_ref[...] = jnp.zeros_like(dhs_ref)
            dal_ref[...] = jnp.zeros_like(dal_ref)
            ddb_ref[...] = jnp.zeros_like(ddb_ref)
            ddp_ref[...] = jnp.zeros_like(ddp_ref)

        bm, cm = b_ref[...], c_ref[...]
        dbm_tot = jnp.zeros_like(bm)
        dcm_tot = jnp.zeros_like(cm)
        for p in range(ppg):
            cols = slice(p * LANE, (p + 1) * LANE)
            ins = (x_ref[:, cols], bm, cm, dt_ref[p], al_ref[p], db_ref[p], dp_ref[p], st_ref[p])
            _, vjp = jax.vjp(_ssd_chunk, *ins)
            dx, dbm, dcm, ddt, dal, ddb, ddp, dhs = vjp((dy_ref[:, cols], dhs_ref[p]))
            dx_ref[:, cols] = dx
            ddt_ref[p] = ddt
            dbm_tot = dbm_tot + dbm
            dcm_tot = dcm_tot + dcm
            dhs_ref[p] = dhs
            dal_ref[p] += dal
            ddb_ref[p] += ddb
            ddp_ref[p] += ddp
        dbm_ref[...] = dbm_tot
        dcm_ref[...] = dcm_tot

    r = lambda c: nc - 1 - c
    two = pl.BlockSpec((None, ppg, 1, 2), lambda g, c: (g, 0, 0, 0))
    dtspec = pl.BlockSpec((None, ppg, CHUNK, 2), lambda g, c: (g, 0, r(c), 0))
    grp = pl.BlockSpec((CHUNK, LANE), lambda g, c: (r(c), g))
    prm = jax.ShapeDtypeStruct((SSD_GROUPS, ppg, 1, 2), F32)
    return pl.pallas_call(
        body, name="ssd_bwd", grid=(SSD_GROUPS, nc),
        in_specs=[pl.BlockSpec((CHUNK, gw), lambda g, c: (r(c), _MINE["sx"] // gw + g)),
                  pl.BlockSpec((CHUNK, LANE), lambda g, c: (r(c), CB("sB") + g)),
                  pl.BlockSpec((CHUNK, LANE), lambda g, c: (r(c), CB("sC") + g)),
                  dtspec, two, two, two,
                  pl.BlockSpec((None, ppg, None, LANE, SSD_N), lambda g, c: (g, 0, r(c), 0, 0)),
                  pl.BlockSpec((CHUNK, gw), lambda g, c: (r(c), g))],
        out_specs=[pl.BlockSpec((CHUNK, gw), lambda g, c: (r(c), g)), grp, grp, dtspec, two, two, two],
        out_shape=[jax.ShapeDtypeStruct((length, SSD_GROUPS * gw), F32),
                   jax.ShapeDtypeStruct((length, SSD_GROUPS * SSD_N), F32),
                   jax.ShapeDtypeStruct((length, SSD_GROUPS * SSD_N), F32),
                   jax.ShapeDtypeStruct((SSD_GROUPS, ppg, length, 2), F32), prm, prm, prm],
        scratch_shapes=[pltpu.VMEM((ppg, LANE, SSD_N), F32)],
        compiler_params=_cparams(("parallel", "arbitrary"), VMEM_LIMIT),
    )(xc, xc, xc, dt_col, alog, dtb, dpar, st, dy)


def _ret_fwd(proj, cos2, sin2, dmat, qdec, kdec, cdec, nw, length):
    nc = length // CHUNK
    h_ = RET_HEADS

    def body(q_ref, k_ref, v_ref, g_ref, cos_ref, sin_ref, dm_ref, qd_ref, kd_ref, cd_ref, nw_ref,
             on_ref, st_ref, r_ref):
        @pl.when(pl.program_id(1) == 0)
        def _():
            r_ref[...] = jnp.zeros_like(r_ref)

        st_ref[...] = r_ref[...]
        on, r_new = _ret_chunk(q_ref[...], k_ref[...], v_ref[...], g_ref[...], cos_ref[...], sin_ref[...],
                               dm_ref[...], qd_ref[...], kd_ref[...], cd_ref[...], nw_ref[...], r_ref[...])
        on_ref[...] = on
        r_ref[...] = r_new

    blk = lambda cb0: pl.BlockSpec((CHUNK, LANE), lambda h, c: (c, cb0 + h))
    tab = pl.BlockSpec((CHUNK, LANE), lambda h, c: (c, 0))
    return pl.pallas_call(
        body, name="ret_fwd", grid=(h_, nc),
        in_specs=[blk(CB("rq")), blk(CB("rk")), blk(CB("rv")), blk(CB("rg")), tab, tab,
                  pl.BlockSpec((None, CHUNK, CHUNK), lambda h, c: (h, 0, 0)),
                  pl.BlockSpec((None, CHUNK, 1), lambda h, c: (h, 0, 0)),
                  pl.BlockSpec((None, CHUNK, 1), lambda h, c: (h, 0, 0)),
                  pl.BlockSpec((None, 1, 1), lambda h, c: (h, 0, 0)),
                  pl.BlockSpec((1, LANE), lambda h, c: (0, 0))],
        out_specs=[pl.BlockSpec((CHUNK, LANE), lambda h, c: (c, h)),
                   pl.BlockSpec((None, None, LANE, LANE), lambda h, c: (h, c, 0, 0))],
        out_shape=[jax.ShapeDtypeStruct((length, h_ * LANE), F32),
                   jax.ShapeDtypeStruct((h_, nc, LANE, LANE), F32)],
        scratch_shapes=[pltpu.VMEM((LANE, LANE), F32)],
        compiler_params=_cparams(("parallel", "arbitrary"), VMEM_LIMIT),
    )(proj, proj, proj, proj, cos2, sin2, dmat, qdec, kdec, cdec, nw)


def _ret_bwd(proj, cos2, sin2, dmat, qdec, kdec, cdec, nw, st, dmix, length):
    nc = length // CHUNK
    h_ = RET_HEADS

    def body(q_ref, k_ref, v_ref, g_ref, cos_ref, sin_ref, dm_ref, qd_ref, kd_ref, cd_ref, nw_ref, st_ref,
             do_ref, dq_ref, dk_ref, dv_ref, dg_ref, dnw_ref, dr_ref):
        @pl.when(pl.program_id(1) == 0)
        def _():
            dr_ref[...] = jnp.zeros_like(dr_ref)
            dnw_ref[...] = jnp.zeros_like(dnw_ref)

        consts = (cos_ref[...], sin_ref[...], dm_ref[...], qd_ref[...], kd_ref[...], cd_ref[...])
        f = lambda q, k, v, g, nw_, r_: _ret_chunk(q, k, v, g, *consts, nw_, r_)
        _, vjp = jax.vjp(f, q_ref[...], k_ref[...], v_ref[...], g_ref[...], nw_ref[...], st_ref[...])
        dq, dk, dv, dg, dnw, dr = vjp((do_ref[...], dr_ref[...]))
        dq_ref[...] = dq
        dk_ref[...] = dk
        dv_ref[...] = dv
        dg_ref[...] = dg
        dnw_ref[...] += dnw
        dr_ref[...] = dr

    r = lambda c: nc - 1 - c
    blk = lambda cb0: pl.BlockSpec((CHUNK, LANE), lambda h, c: (r(c), cb0 + h))
    tab = pl.BlockSpec((CHUNK, LANE), lambda h, c: (r(c), 0))
    oblk = pl.BlockSpec((CHUNK, LANE), lambda h, c: (r(c), h))
    big = jax.ShapeDtypeStruct((length, h_ * LANE), F32)
    return pl.pallas_call(
        body, name="ret_bwd", grid=(h_, nc),
        in_specs=[blk(CB("rq")), blk(CB("rk")), blk(CB("rv")), blk(CB("rg")), tab, tab,
                  pl.BlockSpec((None, CHUNK, CHUNK), lambda h, c: (h, 0, 0)),
                  pl.BlockSpec((None, CHUNK, 1), lambda h, c: (h, 0, 0)),
                  pl.BlockSpec((None, CHUNK, 1), lambda h, c: (h, 0, 0)),
                  pl.BlockSpec((None, 1, 1), lambda h, c: (h, 0, 0)),
                  pl.BlockSpec((1, LANE), lambda h, c: (0, 0)),
                  pl.BlockSpec((None, None, LANE, LANE), lambda h, c: (h, r(c), 0, 0)),
                  pl.BlockSpec((CHUNK, LANE), lambda h, c: (r(c), 12 + h))],
        out_specs=[oblk, oblk, oblk, oblk, pl.BlockSpec((None, 1, LANE), lambda h, c: (h, 0, 0))],
        out_shape=[big, big, big, big, jax.ShapeDtypeStruct((h_, 1, LANE), F32)],
        scratch_shapes=[pltpu.VMEM((LANE, LANE), F32)],
        compiler_params=_cparams(("parallel", "arbitrary"), VMEM_LIMIT),
    )(proj, proj, proj, proj, cos2, sin2, dmat, qdec, kdec, cdec, nw, st, dmix)


def _mesh_pos():
    x, y, c = lax.axis_index("x"), lax.axis_index("y"), lax.axis_index("c")
    return x, y, c, 4 * x + 2 * y + c


def _peer(x, y, c, mask):
    return (x ^ ((mask >> 2) & 1), y ^ ((mask >> 1) & 1), c ^ (mask & 1))


def _all_gather(name, shards):
    n = len(shards)

    def body(*refs):
        ins, outs = refs[:n], refs[n:2 * n]
        send_sems, recv_sems, local_sems = refs[2 * n:]
        x, y, c, me = _mesh_pos()
        local = [pltpu.make_async_copy(ins[a], outs[a].at[me], local_sems.at[a]) for a in range(n)]
        for cp in local:
            cp.start()
        copies = []
        for mask in range(1, N_DEV):
            for a in range(n):
                k = (mask - 1) * n + a
                copies.append(pltpu.make_async_remote_copy(
                    src_ref=ins[a], dst_ref=outs[a].at[me], send_sem=send_sems.at[k], recv_sem=recv_sems.at[k],
                    device_id=_peer(x, y, c, mask), device_id_type=pl.DeviceIdType.MESH))
        for cp in copies:
            cp.start()
        for cp in copies:
            cp.wait_send()
        for mask in range(1, N_DEV):
            px, py, pc = _peer(x, y, c, mask)
            for a in range(n):
                k = (mask - 1) * n + a
                pltpu.make_async_remote_copy(
                    src_ref=ins[a], dst_ref=outs[a].at[4 * px + 2 * py + pc], send_sem=send_sems.at[k],
                    recv_sem=recv_sems.at[k], device_id=(px, py, pc), device_id_type=pl.DeviceIdType.MESH).wait_recv()
        for cp in local:
            cp.wait()

    hbm = pl.BlockSpec(memory_space=pl.ANY)
    k_tot = (N_DEV - 1) * n
    return pl.pallas_call(
        body, name=name, in_specs=[hbm] * n, out_specs=[hbm] * n,
        out_shape=[jax.ShapeDtypeStruct((N_DEV,) + s.shape, s.dtype) for s in shards],
        scratch_shapes=[pltpu.SemaphoreType.DMA((k_tot,)), pltpu.SemaphoreType.DMA((k_tot,)),
                        pltpu.SemaphoreType.DMA((n,))],
        compiler_params=pltpu.CompilerParams(has_side_effects=True),
    )(*shards)


def _exchange(name, slabs):
    n = len(slabs)

    def body(*refs):
        ins, outs = refs[:n], refs[n:2 * n]
        send_sems, recv_sems, local_sems = refs[2 * n:]
        x, y, c, me = _mesh_pos()
        local = [pltpu.make_async_copy(ins[a].at[me], outs[a].at[me], local_sems.at[a]) for a in range(n)]
        for cp in local:
            cp.start()
        copies = []
        for mask in range(1, N_DEV):
            px, py, pc = _peer(x, y, c, mask)
            for a in range(n):
                k = (mask - 1) * n + a
                copies.append(pltpu.make_async_remote_copy(
                    src_ref=ins[a].at[4 * px + 2 * py + pc], dst_ref=outs[a].at[me], send_sem=send_sems.at[k],
                    recv_sem=recv_sems.at[k], device_id=(px, py, pc), device_id_type=pl.DeviceIdType.MESH))
        for cp in copies:
            cp.start()
        for cp in copies:
            cp.wait_send()
        for mask in range(1, N_DEV):
            px, py, pc = _peer(x, y, c, mask)
            for a in range(n):
                k = (mask - 1) * n + a
                pltpu.make_async_remote_copy(
                    src_ref=ins[a].at[me], dst_ref=outs[a].at[4 * px + 2 * py + pc], send_sem=send_sems.at[k],
                    recv_sem=recv_sems.at[k], device_id=(px, py, pc), device_id_type=pl.DeviceIdType.MESH).wait_recv()
        for cp in local:
            cp.wait()

    hbm = pl.BlockSpec(memory_space=pl.ANY)
    k_tot = (N_DEV - 1) * n
    return pl.pallas_call(
        body, name=name, in_specs=[hbm] * n, out_specs=[hbm] * n,
        out_shape=[jax.ShapeDtypeStruct(s.shape, s.dtype) for s in slabs],
        scratch_shapes=[pltpu.SemaphoreType.DMA((k_tot,)), pltpu.SemaphoreType.DMA((k_tot,)),
                        pltpu.SemaphoreType.DMA((n,))],
        compiler_params=pltpu.CompilerParams(has_side_effects=True),
    )(*slabs)


def _adam(name, parts, w, m, v, tr):
    p_, r_, cw = parts.shape
    c1 = 1.0 - ADAM_B1 ** ADAM_STEP
    c2 = 1.0 - ADAM_B2 ** ADAM_STEP

    def body(p_ref, w_ref, m_ref, v_ref, g_ref, d_ref, nm_ref, nv_ref):
        g = p_ref[0]
        for i in range(1, p_):
            g = g + p_ref[i]
        nm = ADAM_B1 * m_ref[...] + (1.0 - ADAM_B1) * g
        nv = ADAM_B2 * v_ref[...] + (1.0 - ADAM_B2) * (g * g)
        d_ref[...] = -ADAM_LR * ((nm / c1) / (jnp.sqrt(nv / c2) + ADAM_EPS) + ADAM_WD * w_ref[...])
        g_ref[...] = g
        nm_ref[...] = nm
        nv_ref[...] = nv

    spec = pl.BlockSpec((tr, cw), lambda i: (i, 0))
    shp = jax.ShapeDtypeStruct((r_, cw), F32)
    return pl.pallas_call(
        body, name=name, grid=(r_ // tr,),
        in_specs=[pl.BlockSpec((p_, tr, cw), lambda i: (0, i, 0)), spec, spec, spec],
        out_specs=[spec] * 4, out_shape=[shp] * 4,
        compiler_params=_cparams(("parallel",), VMEM_LIMIT),
    )(parts, w, m, v)


def _perm_cols(w_full):
    parts = [w_full[:, _ORIG[n][0]:_ORIG[n][0] + _ORIG[n][1]] for n in _ORDER]
    parts.append(jnp.zeros((w_full.shape[0], N_PAD - N_USED), w_full.dtype))
    return jnp.concatenate(parts, axis=1)


def _unperm_cols(g):
    by_orig = sorted(_ORDER, key=lambda n: _ORIG[n][0])
    return jnp.concatenate([g[:, _MINE[n]:_MINE[n] + _ORIG[n][1]] for n in by_orig], axis=1)


def _ret_consts(length):
    lg = jnp.log(1.0 - 2.0 ** (-5.0 - jnp.arange(RET_HEADS, dtype=F32)))
    idx = jnp.arange(CHUNK, dtype=F32)
    rel = idx[:, None] - idx[None, :]
    dmat = jnp.where(rel[None] >= 0, jnp.exp(jnp.maximum(rel, 0.0)[None] * lg[:, None, None]), 0.0)
    qdec = jnp.exp((idx[:, None] + 1.0) * lg[None, :]).T[:, :, None]
    kdec = jnp.exp((CHUNK - 1.0 - idx)[:, None] * lg[None, :]).T[:, :, None]
    cdec = jnp.exp(CHUNK * lg)[:, None, None]
    half = RET_DK // 2
    inv = ROPE_BASE ** (-jnp.arange(half, dtype=F32) / half)
    ang = jnp.arange(length, dtype=jnp.int32).astype(F32)[:, None] * inv[None, :]
    cos, sin = jnp.cos(ang), jnp.sin(ang)
    return (jnp.concatenate([cos, cos], axis=1), jnp.concatenate([-sin, sin], axis=1), dmat, qdec, kdec, cdec)


def _tiles(length):
    tm = 512 if length % 512 == 0 else length
    return tm


def _layer_fwd(x, p, consts, length, last, target):
    tm = _tiles(length)
    (h,), _ = _rowmap("pre_norm", lambda x_, w_: ((_rms(x_, w_),), ()), length, tm,
                      [(x, D_MODEL, 0)], [p["pre_norm"]], out_ws=[D_MODEL], out_dtypes=[BF16])
    proj = _matmul("in_proj", h, p["w_in"], "nn", length, N_PAD, D_MODEL, tm, 512, D_MODEL)
    xc = _conv_fwd(proj, p["conv_w"], p["conv_b"], length, tm, 512)
    small = proj[:, SMALL_OFF:SMALL_OFF + LANE]
    b_col = small[:, 0:4].T[:, :, None]
    a_col = small[:, 4:8].T[:, :, None]
    dt_col = small[:, 8:24].reshape(length, SSD_GROUPS, PAIRS_PER_GROUP, 2).transpose(1, 2, 0, 3)
    o_a, st_a = _gdn_fwd(xc, proj, b_col, a_col, p["gdn_A_log"], p["gdn_dt_bias"], p["gdn_norm"], length)
    y_b, st_b = _ssd_fwd(xc, dt_col, p["ssd_A_log"], p["ssd_dt_bias"], p["ssd_D"], length)
    o_b = []
    for g in range(SSD_GROUPS):
        (o,), _ = _rowmap(f"ssd_gate{g}", lambda y_, z_, w_: ((_rms(y_ * _silu(z_), w_),), ()), length, tm,
                          [(y_b, 512, g), (proj, 512, _MINE["sz"] // 512 + g)], [p["ssd_norm"][g]], out_ws=[512])
        o_b.append(o)
    o_c, st_c = _ret_fwd(proj, *consts, p["ret_norm"], length)
    mixed = jnp.concatenate([o_a] + o_b + [o_c], axis=1)
    out = _matmul("out_proj", mixed, p["w_out"], "nn", length, D_MODEL, MIX_W, tm, 512, MIX_W)
    res = dict(x=x, h=h, proj=proj, xc=xc, b_col=b_col, a_col=a_col, dt_col=dt_col, st_a=st_a, st_b=st_b,
               st_c=st_c, y_b=y_b, mixed=mixed, out=out)
    if not last:
        (y,), _ = _rowmap("post_norm", lambda x_, o_, w_: ((x_ + _rms(o_, w_),), ()), length, tm,
                          [(x, D_MODEL, 0), (out, D_MODEL, 0)], [p["post_norm"]], out_ws=[D_MODEL])
        return y, res, None

    def head(x_, o_, t_, w_):
        e = x_ + _rms(o_, w_) - t_
        row = jnp.mean(e * e, axis=-1, keepdims=True)
        loss = 0.5 * jnp.sum(row, axis=0, keepdims=True)
        return (e * (1.0 / D_MODEL),), (loss + jnp.zeros((1, LANE), F32),)

    (dy,), (loss,) = _rowmap("loss_head", head, length, tm,
                             [(x, D_MODEL, 0), (out, D_MODEL, 0), (target, D_MODEL, 0)], [p["post_norm"]],
                             out_ws=[D_MODEL], acc_ws=[LANE])
    return None, res, (dy, loss[0, 0])


def _layer_bwd(dy, p, res, consts, length):
    tm = _tiles(length)
    g = {}
    (d_out,), (g["post_norm"],) = _rowmap("post_norm_bwd", lambda o_, w_: ((_rms(o_, w_),), ()), length, tm,
                                          [(res["out"], D_MODEL, 0)], [p["post_norm"]], cts=[(dy, D_MODEL, 0)])
    dmix = _matmul("out_proj_dx", d_out, p["w_out"], "nt", length, MIX_W, D_MODEL, tm, 512, D_MODEL)
    g["w_out"] = _matmul("out_proj_dw", res["mixed"], d_out, "tn", MIX_W, D_MODEL, length, 512, 512, tm)
    proj, xc = res["proj"], res["xc"]
    (dgq, dgk, dgv, dgz, db, da, g["gdn_A_log"], g["gdn_dt_bias"], dnw_a) = _gdn_bwd(
        xc, proj, res["b_col"], res["a_col"], p["gdn_A_log"], p["gdn_dt_bias"], p["gdn_norm"], res["st_a"], dmix, length)
    g["gdn_norm"] = jnp.sum(dnw_a, axis=0)
    dyb, dsz, dsn = [], [], []
    for gi in range(SSD_GROUPS):
        (dy_g, dz_g), (dn_g,) = _rowmap(
            f"ssd_gate_bwd{gi}", lambda y_, z_, w_: ((_rms(y_ * _silu(z_), w_),), ()), length, tm,
            [(res["y_b"], 512, gi), (proj, 512, _MINE["sz"] // 512 + gi)], [p["ssd_norm"][gi]],
            cts=[(dmix, 512, 1 + gi)])
        dyb.append(dy_g)
        dsz.append(dz_g)
        dsn.append(dn_g)
    g["ssd_norm"] = dsn
    dyb = jnp.concatenate(dyb, axis=1)
    dsx, dsb, dsc, ddt, g["ssd_A_log"], g["ssd_dt_bias"], g["ssd_D"] = _ssd_bwd(
        xc, res["dt_col"], p["ssd_A_log"], p["ssd_dt_bias"], p["ssd_D"], res["st_b"], dyb, length)
    drq, drk, drv, drg, dnw_c = _ret_bwd(proj, *consts, p["ret_norm"], res["st_c"], dmix, length)
    g["ret_norm"] = jnp.sum(dnw_c, axis=0)
    dxc = jnp.concatenate([dgq, dgk, dgv, dsx, dsb, dsc], axis=1)
    dpre, g["conv_w"], g["conv_b"] = _conv_bwd_pre(proj, p["conv_w"], p["conv_b"], dxc, length, tm, 512)
    dconv = _conv_bwd_x(dpre, p["conv_w"], length, tm, 512)
    dsmall = jnp.concatenate([db[:, :, 0].T, da[:, :, 0].T, ddt.transpose(2, 0, 1, 3).reshape(length, SSD_HEADS)], axis=1)
    dproj = jnp.concatenate([dconv, dgz] + dsz + [drq, drk, drv, drg, dsmall,
                                                  jnp.zeros((length, N_PAD - N_USED), F32)], axis=1)
    dh = _matmul("in_proj_dx", dproj, p["w_in"], "nt", length, D_MODEL, N_PAD, tm, 512, 512)
    g["w_in"] = _matmul("in_proj_dw", res["h"], dproj, "tn", D_MODEL, N_PAD, length, 512, 512, tm)
    (dx,), (g["pre_norm"],) = _rowmap(
        "pre_norm_bwd", lambda x_, w_: ((_rms(x_, w_), x_), ()), length, tm,
        [(res["x"], D_MODEL, 0)], [p["pre_norm"]], cts=[(dh, D_MODEL, 0), (dy, D_MODEL, 0)])
    return dx, g


def _make_layers(w_in_full, w_out_full, conv_full, sw):
    layers = []
    for l in range(DEPTH):
        layers.append(dict(
            pre_norm=sw["pre_norm"][l][None], post_norm=sw["post_norm"][l][None], w_in=_perm_cols(w_in_full[l]),
            w_out=w_out_full[l],
            conv_w=jnp.concatenate([conv_full[0, l], conv_full[1, l]], axis=1),
            conv_b=jnp.concatenate([jnp.zeros((CONV_CH,), F32), sw["ssd_conv_b"][l]])[None],
            gdn_A_log=sw["gdn_A_log"][l][:, None, None], gdn_dt_bias=sw["gdn_dt_bias"][l][:, None, None],
            gdn_norm=sw["gdn_norm"][l][None],
            ssd_A_log=sw["ssd_A_log"][l].reshape(SSD_GROUPS, PAIRS_PER_GROUP, 1, 2),
            ssd_dt_bias=sw["ssd_dt_bias"][l].reshape(SSD_GROUPS, PAIRS_PER_GROUP, 1, 2),
            ssd_D=sw["ssd_D"][l].reshape(SSD_GROUPS, PAIRS_PER_GROUP, 1, 2),
            ssd_norm=[sw["ssd_norm"][l][None, 512 * g:512 * (g + 1)] for g in range(SSD_GROUPS)],
            ret_norm=sw["ret_norm"][l][None]))
    return layers


def _local_step(xb, tgt, layers, length):
    consts = _ret_consts(length)
    cur = xb
    saved = []
    extras = None
    for l in range(DEPTH):
        cur, res, extras = _layer_fwd(cur, layers[l], consts, length, l == DEPTH - 1, tgt)
        saved.append(res)
    dy, loss_local = extras
    grads = [None] * DEPTH
    for l in reversed(range(DEPTH)):
        dy, grads[l] = _layer_bwd(dy, layers[l], saved[l], consts, length)
    return loss_local, dy, grads


_SMALL = ["pre_norm", "post_norm", "gdn_A_log", "gdn_dt_bias", "gdn_norm", "ssd_conv_b", "ssd_A_log",
          "ssd_dt_bias", "ssd_D", "ssd_norm", "ret_norm"]


def _pack_small(arrs):
    rows = []
    for n in _SMALL:
        flat = arrs[n].reshape(-1)
        pad = (-flat.shape[0]) % LANE
        rows.append(jnp.pad(flat, (0, pad)).reshape(-1, LANE))
    out = jnp.concatenate(rows, axis=0)
    return jnp.pad(out, ((0, (-out.shape[0]) % SUBLANE), (0, 0)))


def _unpack_small(packed, like):
    out, r = {}, 0
    for n in _SMALL:
        cnt = like[n].size
        nrow = -(-cnt // LANE)
        out[n] = packed[r:r + nrow].reshape(-1)[:cnt].reshape(like[n].shape)
        r += nrow
    return out


def kernel(x, pre_norm, post_norm, w_in, gdn_conv, gdn_A_log, gdn_dt_bias, gdn_norm, ssd_conv, ssd_conv_b, ssd_A_log, ssd_dt_bias, ssd_D, ssd_norm, ret_norm, w_out, loss_target, m_pre_norm, m_post_norm, m_w_in, m_gdn_conv, m_gdn_A_log, m_gdn_dt_bias, m_gdn_norm, m_ssd_conv, m_ssd_conv_b, m_ssd_A_log, m_ssd_dt_bias, m_ssd_D, m_ssd_norm, m_ret_norm, m_w_out, v_pre_norm, v_post_norm, v_w_in, v_gdn_conv, v_gdn_A_log, v_gdn_dt_bias, v_gdn_norm, v_ssd_conv, v_ssd_conv_b, v_ssd_A_log, v_ssd_dt_bias, v_ssd_D, v_ssd_norm, v_ret_norm, v_w_out):
    length = x.shape[1]
    xb = x[0]
    tgt = loss_target[0]
    small_w = dict(pre_norm=pre_norm, post_norm=post_norm, gdn_A_log=gdn_A_log, gdn_dt_bias=gdn_dt_bias,
                   gdn_norm=gdn_norm, ssd_conv_b=ssd_conv_b, ssd_A_log=ssd_A_log, ssd_dt_bias=ssd_dt_bias,
                   ssd_D=ssd_D, ssd_norm=ssd_norm, ret_norm=ret_norm)
    small_m = dict(pre_norm=m_pre_norm, post_norm=m_post_norm, gdn_A_log=m_gdn_A_log, gdn_dt_bias=m_gdn_dt_bias,
                   gdn_norm=m_gdn_norm, ssd_conv_b=m_ssd_conv_b, ssd_A_log=m_ssd_A_log, ssd_dt_bias=m_ssd_dt_bias,
                   ssd_D=m_ssd_D, ssd_norm=m_ssd_norm, ret_norm=m_ret_norm)
    small_v = dict(pre_norm=v_pre_norm, post_norm=v_post_norm, gdn_A_log=v_gdn_A_log, gdn_dt_bias=v_gdn_dt_bias,
                   gdn_norm=v_gdn_norm, ssd_conv_b=v_ssd_conv_b, ssd_A_log=v_ssd_A_log, ssd_dt_bias=v_ssd_dt_bias,
                   ssd_D=v_ssd_D, ssd_norm=v_ssd_norm, ret_norm=v_ret_norm)

    conv_shard = jnp.stack([gdn_conv, ssd_conv], axis=0)
    g_in, g_out, g_conv = _all_gather("gather_weights", [w_in.astype(BF16), w_out.astype(BF16), conv_shard])
    w_in_full = g_in.transpose(1, 2, 0, 3).reshape(DEPTH, D_MODEL, N_IN)
    w_out_full = g_out.transpose(1, 0, 2, 3).reshape(DEPTH, MIX_W, D_MODEL)
    conv_full = g_conv.transpose(1, 2, 3, 0, 4).reshape(2, DEPTH, CONV_W, CONV_CH)

    layers = _make_layers(w_in_full, w_out_full, conv_full, small_w)
    loss_local, dy, grads = _local_step(xb, tgt, layers, length)
    grad_x = dy[None]
    loss = lax.psum(loss_local, ("x", "y", "c"))

    gi = jnp.stack([_unperm_cols(grads[l]["w_in"]).reshape(D_MODEL, N_DEV, SHARD_IN).transpose(1, 0, 2)
                    for l in range(DEPTH)], axis=1).reshape(N_DEV, DEPTH * D_MODEL, SHARD_IN)
    go = jnp.stack([grads[l]["w_out"].reshape(N_DEV, SHARD_OUT, D_MODEL) for l in range(DEPTH)],
                   axis=1).reshape(N_DEV, DEPTH * SHARD_OUT, D_MODEL)
    gc = jnp.stack([jnp.stack([grads[l]["conv_w"][:, k * CONV_CH:(k + 1) * CONV_CH]
                               .reshape(CONV_W, N_DEV, SHARD_CONV).transpose(1, 0, 2) for l in range(DEPTH)], axis=1)
                    for k in range(2)], axis=1).reshape(N_DEV, 2 * DEPTH * CONV_W, SHARD_CONV)
    small_g = dict(
        pre_norm=jnp.concatenate([grads[l]["pre_norm"] for l in range(DEPTH)], axis=0),
        post_norm=jnp.concatenate([grads[l]["post_norm"] for l in range(DEPTH)], axis=0),
        gdn_A_log=jnp.stack([grads[l]["gdn_A_log"][:, 0, 0] for l in range(DEPTH)]),
        gdn_dt_bias=jnp.stack([grads[l]["gdn_dt_bias"][:, 0, 0] for l in range(DEPTH)]),
        gdn_norm=jnp.concatenate([grads[l]["gdn_norm"] for l in range(DEPTH)], axis=0),
        ssd_conv_b=jnp.concatenate([grads[l]["conv_b"][:, CONV_CH:] for l in range(DEPTH)], axis=0),
        ssd_A_log=jnp.stack([grads[l]["ssd_A_log"].reshape(SSD_HEADS) for l in range(DEPTH)]),
        ssd_dt_bias=jnp.stack([grads[l]["ssd_dt_bias"].reshape(SSD_HEADS) for l in range(DEPTH)]),
        ssd_D=jnp.stack([grads[l]["ssd_D"].reshape(SSD_HEADS) for l in range(DEPTH)]),
        ssd_norm=jnp.concatenate([jnp.concatenate(grads[l]["ssd_norm"], axis=1) for l in range(DEPTH)], axis=0),
        ret_norm=jnp.concatenate([grads[l]["ret_norm"] for l in range(DEPTH)], axis=0))
    gs = _pack_small(small_g)
    gs8 = jnp.broadcast_to(gs[None], (N_DEV,) + gs.shape)
    r_in, r_out, r_conv, r_small = _exchange("exchange_grads", [gi, go, gc, gs8])

    flat = lambda a, cw: a.reshape(-1, cw)
    o_in = _adam("adam_w_in", r_in, flat(w_in, SHARD_IN), flat(m_w_in, SHARD_IN), flat(v_w_in, SHARD_IN), 128)
    o_out = _adam("adam_w_out", r_out, flat(w_out, D_MODEL), flat(m_w_out, D_MODEL), flat(v_w_out, D_MODEL), 128)
    cstk = lambda a, b: flat(jnp.stack([a, b], axis=0), SHARD_CONV)
    o_conv = _adam("adam_conv", r_conv, cstk(gdn_conv, ssd_conv), cstk(m_gdn_conv, m_ssd_conv),
                   cstk(v_gdn_conv, v_ssd_conv), 2 * DEPTH * CONV_W)
    ps_w, ps_m, ps_v = _pack_small(small_w), _pack_small(small_m), _pack_small(small_v)
    o_small = _adam("adam_small", r_small, ps_w, ps_m, ps_v, ps_w.shape[0])

    names = ["pre_norm", "post_norm", "w_in", "gdn_conv", "gdn_A_log", "gdn_dt_bias", "gdn_norm", "ssd_conv",
             "ssd_conv_b", "ssd_A_log", "ssd_dt_bias", "ssd_D", "ssd_norm", "ret_norm", "w_out"]
    outs = []
    for kind in range(4):
        sm = _unpack_small(o_small[kind], small_w)
        cv = o_conv[kind].reshape(2, DEPTH, CONV_W, SHARD_CONV)
        d = dict(sm)
        d["w_in"] = o_in[kind].reshape(w_in.shape)
        d["w_out"] = o_out[kind].reshape(w_out.shape)
        d["gdn_conv"] = cv[0]
        d["ssd_conv"] = cv[1]
        outs.extend(d[n] for n in names)
    return (loss, grad_x, *outs)
```

```python
import functools
import math

import numpy as np
import jax
import jax.numpy as jnp
from jax import lax
from jax.experimental import pallas as pl
from jax.experimental.pallas import tpu as pltpu

F32 = jnp.float32
BF16 = jnp.bfloat16

D_MODEL = 1024
DEPTH = 2
CHUNK = 64
CONV_W = 4
EPS = 1e-6
N_DEV = 8

GDN_HEADS = 4
GDN_DK = 128
SSD_HEADS = 16
SSD_P = 64
SSD_N = 128
SSD_GROUPS = 2
SSD_PAIRS = SSD_HEADS // 2
PAIRS_PER_GROUP = SSD_PAIRS // SSD_GROUPS
RET_HEADS = 4
RET_DK = 128
ROPE_BASE = 10000.0
MIX_W = 2048
N_IN = 6680
SHARD_IN = N_IN // N_DEV
SHARD_OUT = MIX_W // N_DEV
CONV_CH = 1536
SHARD_CONV = CONV_CH // N_DEV

ADAM_LR = 0.001
ADAM_B1 = 0.9
ADAM_B2 = 0.999
ADAM_EPS = 1e-08
ADAM_WD = 0.01
ADAM_STEP = 10

LANE = 128
SUBLANE = 8
VMEM_LIMIT = 56 * 1024 * 1024

_ORIG = dict(gq=(0, 512), gk=(512, 512), gv=(1024, 512), gz=(1536, 512), gb=(2048, 4), ga=(2052, 4),
             sx=(2056, 1024), sB=(3080, 256), sC=(3336, 256), sz=(3592, 1024), sdt=(4616, 16),
             rq=(4632, 512), rk=(5144, 512), rv=(5656, 512), rg=(6168, 512))
_ORDER = ["gq", "gk", "gv", "sx", "sB", "sC", "gz", "sz", "rq", "rk", "rv", "rg", "gb", "ga", "sdt"]
_MINE = {}
_off = 0
for _n in _ORDER:
    _MINE[_n] = _off
    _off += _ORIG[_n][1]
N_USED = _off
N_PAD = 7168
CONV_ALL = 2 * CONV_CH
SMALL_OFF = _MINE["gb"]
CB = lambda name: _MINE[name] // LANE


def _cparams(sem, vmem=None):
    return pltpu.CompilerParams(dimension_semantics=sem, vmem_limit_bytes=vmem)


def _make_mm(highest):
    def raw(a, b, ca, cb):
        if highest:
            return lax.dot_general(a, b, (((ca,), (cb,)), ((), ())), precision=lax.Precision.HIGHEST,
                                   preferred_element_type=F32)
        return lax.dot_general(a.astype(BF16), b.astype(BF16), (((ca,), (cb,)), ((), ())),
                               preferred_element_type=F32)

    @jax.custom_vjp
    def nn(a, b):
        return raw(a, b, 1, 0)

    @jax.custom_vjp
    def nt(a, b):
        return raw(a, b, 1, 1)

    @jax.custom_vjp
    def tn(a, b):
        return raw(a, b, 0, 0)

    nn.defvjp(lambda a, b: (raw(a, b, 1, 0), (a, b)), lambda r, g: (nt(g, r[1]), tn(r[0], g)))
    nt.defvjp(lambda a, b: (raw(a, b, 1, 1), (a, b)), lambda r, g: (nn(g, r[1]), tn(g, r[0])))
    tn.defvjp(lambda a, b: (raw(a, b, 0, 0), (a, b)), lambda r, g: (nt(r[1], g), nn(r[0], g)))
    return nn, nt, tn


_nn, _nt, _tn = _make_mm(False)
_nnH, _ntH, _tnH = _make_mm(True)


@jax.custom_vjp
def _swap_halves(t):
    return pltpu.roll(t, LANE // 2, 1)


_swap_halves.defvjp(lambda t: (pltpu.roll(t, LANE // 2, 1), None),
                    lambda _, g: (pltpu.roll(g, LANE // 2, 1),))


@jax.custom_vjp
def _tri_inv(a):
    return _tri_inv_impl(a)


def _tri_inv_impl(a):
    ii = lax.broadcasted_iota(jnp.int32, a.shape, 0)
    jj = lax.broadcasted_iota(jnp.int32, a.shape, 1)
    n = -a
    t = jnp.where(ii == jj, 1.0, 0.0).astype(F32) + n
    p = n
    steps = int(math.log2(a.shape[0])) - 1
    for _ in range(steps):
        p = _nnH(p, p)
        t = t + _nnH(t, p)
    return t


_tri_inv.defvjp(lambda a: (lambda t: (t, t))(_tri_inv_impl(a)),
                lambda t, g: (-_ntH(_tnH(t, g), t),))


def _silu(x):
    return x * jax.nn.sigmoid(x)


@jax.custom_vjp
def _softplus(x):
    return jnp.maximum(x, 0.0) + jnp.log1p(jnp.exp(-jnp.abs(x)))


_softplus.defvjp(lambda x: (jnp.maximum(x, 0.0) + jnp.log1p(jnp.exp(-jnp.abs(x))), x),
                 lambda x, g: (g * jax.nn.sigmoid(x),))


def _rms(x, w):
    return x * lax.rsqrt(jnp.mean(x * x, axis=-1, keepdims=True) + EPS) * w


def _chunk_masks(n):
    ii = lax.broadcasted_iota(jnp.int32, (n, n), 0)
    jj = lax.broadcasted_iota(jnp.int32, (n, n), 1)
    return ii >= jj, ii > jj, ii == jj, ii <= jj


def _cumsum_col(g, causal, eye, upper):
    g_row = jnp.sum(jnp.where(eye, g, 0.0), axis=0, keepdims=True)
    col = jnp.sum(jnp.where(causal, g_row, 0.0), axis=1, keepdims=True)
    row = jnp.sum(jnp.where(upper, g, 0.0), axis=0, keepdims=True)
    return col, row


def _gdn_chunk(q, k, v, gz, b_raw, a_raw, alog, dtb, nw, s):
    causal, strict, eye, upper = _chunk_masks(q.shape[0])
    qn = q * lax.rsqrt(jnp.sum(q * q, axis=-1, keepdims=True) + EPS) * (GDN_DK ** -0.5)
    kn = k * lax.rsqrt(jnp.sum(k * k, axis=-1, keepdims=True) + EPS)
    beta = jax.nn.sigmoid(b_raw)
    g = -jnp.exp(alog) * _softplus(a_raw + dtb)
    gcum, gcum_row = _cumsum_col(g, causal, eye, upper)
    decay = jnp.where(causal, jnp.exp(jnp.where(causal, gcum - gcum_row, 0.0)), 0.0)
    kb = kn * beta
    a_low = jnp.where(strict, _nt(kb, kn) * decay, 0.0)
    eg = jnp.exp(gcum)
    t = _tri_inv(a_low)
    u = _nn(t, v * beta)
    w = _nn(t, kb * eg)
    v_new = u - _nn(w, s)
    attn = _nt(qn, kn) * decay
    o = _nn(qn * eg, s) + _nn(attn, v_new)
    glast = jnp.sum(g, axis=0, keepdims=True)
    kdec = kn * jnp.exp(glast - gcum)
    s_new = s * jnp.exp(glast) + _tn(kdec, v_new)
    on = _rms(o, nw) * _silu(gz)
    return on, s_new


def _ssd_chunk(x, bm, cm, dtr, alog, dtb, dp, hs):
    c = x.shape[0]
    causal, _, eye, upper = _chunk_masks(c)
    lane_lo = lax.broadcasted_iota(jnp.int32, (1, LANE), 1) < SSD_P
    row_lo = lax.broadcasted_iota(jnp.int32, (LANE, 1), 0) < SSD_P
    sel0 = lax.broadcasted_iota(jnp.int32, (1, 2), 1) == 0
    dt = _softplus(dtr + dtb)
    a = dt * -jnp.exp(alog)
    cb = _nt(cm, bm)
    heads = []
    for h in range(2):
        sel = sel0 if h == 0 else jnp.logical_not(sel0)
        dt_h = jnp.sum(jnp.where(sel, dt, 0.0), axis=1, keepdims=True)
        a_h = jnp.sum(jnp.where(sel, a, 0.0), axis=1, keepdims=True)
        d_h = jnp.sum(jnp.where(sel, dp, 0.0), axis=1, keepdims=True)
        acum, acum_row = _cumsum_col(a_h, causal, eye, upper)
        lmat = jnp.where(causal, jnp.exp(jnp.where(causal, acum - acum_row, 0.0)), 0.0)
        alast = jnp.sum(a_h, axis=0, keepdims=True)
        heads.append((dt_h, d_h, acum, lmat, alast))
    pick = lambda i: jnp.where(lane_lo, heads[0][i], heads[1][i])
    xdt = x * pick(0)
    y = x * pick(1)
    for h in range(2):
        hm = lane_lo if h == 0 else jnp.logical_not(lane_lo)
        y = y + _nn(cb * heads[h][3], jnp.where(hm, xdt, 0.0))
    y = y + _nt(cm, hs) * jnp.exp(pick(2))
    wdec = jnp.exp(jnp.where(lane_lo, heads[0][4] - heads[0][2], heads[1][4] - heads[1][2]))
    scale = jnp.exp(jnp.where(row_lo, heads[0][4], heads[1][4]))
    hs_new = hs * scale + _tn(xdt * wdec, bm)
    return y, hs_new


def _ret_chunk(rq, rk, rv, rg, cos2, sin2, dmat, qdec, kdec, cdec, nw, r):
    q = rq * cos2 + _swap_halves(rq) * sin2
    k = (rk * cos2 + _swap_halves(rk) * sin2) * (RET_DK ** -0.5)
    s = _nt(q, k) * dmat
    o = _nn(s, rv) + _nn(q, r) * qdec
    r_new = r * cdec + _tn(k * kdec, rv)
    on = _rms(o, nw) * _silu(rg)
    return on, r_new


MM_TILE = 1024


def _tile(n, pref=MM_TILE):
    return pref if n % pref == 0 else n


def _matmul(name, a, b, mode, m, n, k, tk=None):
    tm, tn = _tile(m), _tile(n)
    tk = _tile(k) if tk is None else tk
    nk = k // tk
    ca, cb = {"nn": (1, 0), "nt": (1, 1), "tn": (0, 0)}[mode]

    def body(a_ref, b_ref, o_ref):
        part = lax.dot_general(a_ref[...].astype(BF16), b_ref[...].astype(BF16),
                               (((ca,), (cb,)), ((), ())), preferred_element_type=F32)
        if nk == 1:
            o_ref[...] = part
        else:
            kk = pl.program_id(2)

            @pl.when(kk == 0)
            def _():
                o_ref[...] = part

            @pl.when(kk > 0)
            def _():
                o_ref[...] += part

    a_spec = pl.BlockSpec((tk, tm), lambda i, j, kk: (kk, i)) if mode == "tn" else pl.BlockSpec((tm, tk), lambda i, j, kk: (i, kk))
    b_spec = pl.BlockSpec((tn, tk), lambda i, j, kk: (j, kk)) if mode == "nt" else pl.BlockSpec((tk, tn), lambda i, j, kk: (kk, j))
    return pl.pallas_call(
        body, name=name, grid=(m // tm, n // tn, nk),
        in_specs=[a_spec, b_spec], out_specs=pl.BlockSpec((tm, tn), lambda i, j, kk: (i, j)),
        out_shape=jax.ShapeDtypeStruct((m, n), F32),
        compiler_params=_cparams(("parallel", "parallel", "arbitrary"), VMEM_LIMIT),
    )(a, b)


def _rowmap(name, fn, length, tm, rows, params, out_ws=(), acc_ws=(), out_dtypes=None, cts=None, nd_rows=()):
    nt_ = length // tm
    n_r, n_nd, n_p = len(rows), len(nd_rows), len(params)
    rspec = lambda bw, cbk: pl.BlockSpec((tm, bw), lambda i: (i, cbk))
    pspec = lambda w: pl.BlockSpec((1, w), lambda i: (0, 0))
    in_arrays = [r[0] for r in rows] + [r[0] for r in nd_rows] + list(params)
    in_specs = [rspec(r[1], r[2]) for r in rows] + [rspec(r[1], r[2]) for r in nd_rows] + [pspec(p.shape[1]) for p in params]

    def load(refs):
        return [r[...].astype(F32) for r in refs]

    if cts is None:
        n_o, n_a = len(out_ws), len(acc_ws)
        out_dtypes_ = out_dtypes or [F32] * n_o

        def body(*refs):
            ins = load(refs[:n_r + n_nd + n_p])
            outs, accs = fn(*ins)
            o_refs = refs[n_r + n_nd + n_p:]
            for o_ref, o in zip(o_refs[:n_o], outs):
                o_ref[...] = o.astype(o_ref.dtype)
            first = pl.program_id(0) == 0
            for a_ref, acc in zip(o_refs[n_o:], accs):
                @pl.when(first)
                def _(a_ref=a_ref):
                    a_ref[...] = jnp.zeros_like(a_ref)
                a_ref[...] += acc

        res = pl.pallas_call(
            body, name=name, grid=(nt_,), in_specs=in_specs,
            out_specs=[rspec(w, 0) for w in out_ws] + [pspec(w) for w in acc_ws],
            out_shape=[jax.ShapeDtypeStruct((length, w), dt) for w, dt in zip(out_ws, out_dtypes_)]
                      + [jax.ShapeDtypeStruct((1, w), F32) for w in acc_ws],
            compiler_params=_cparams(("arbitrary",), VMEM_LIMIT),
        )(*in_arrays)
        return res[:n_o], res[n_o:]

    n_c = len(cts)

    def body(*refs):
        ins = load(refs[:n_r + n_nd + n_p])
        ct_vals = load(refs[n_r + n_nd + n_p:n_r + n_nd + n_p + n_c])
        nd = ins[n_r:n_r + n_nd]
        f = lambda rs, ps: fn(*rs, *nd, *ps)[0]
        _, vjp = jax.vjp(f, ins[:n_r], ins[n_r + n_nd:])
        d_rows, d_params = vjp(tuple(ct_vals))
        o_refs = refs[n_r + n_nd + n_p + n_c:]
        for o_ref, d in zip(o_refs[:n_r], d_rows):
            o_ref[...] = d
        first = pl.program_id(0) == 0
        for a_ref, d in zip(o_refs[n_r:], d_params):
            @pl.when(first)
            def _(a_ref=a_ref):
                a_ref[...] = jnp.zeros_like(a_ref)
            a_ref[...] += d

    res = pl.pallas_call(
        body, name=name, grid=(nt_,),
        in_specs=in_specs + [rspec(c[1], c[2]) for c in cts],
        out_specs=[rspec(r[1], 0) for r in rows] + [pspec(p.shape[1]) for p in params],
        out_shape=[jax.ShapeDtypeStruct((length, r[1]), F32) for r in rows]
                  + [jax.ShapeDtypeStruct((1, p.shape[1]), F32) for p in params],
        compiler_params=_cparams(("arbitrary",), VMEM_LIMIT),
    )(*in_arrays, *[c[0] for c in cts])
    return res[:n_r], res[n_r:]


def _conv_shift(ext, k, tm, forward):
    s = CONV_W - 1 - k
    if forward:
        rolled = ext if s == 0 else pltpu.roll(ext, s, 0)
        return rolled[SUBLANE:, :]
    rolled = ext if s == 0 else pltpu.roll(ext, tm + SUBLANE - s, 0)
    return rolled[:tm, :]


def _conv_pre(x_ref, halo_ref, w_ref, b_ref, first, tm):
    halo = jnp.where(first, 0.0, halo_ref[...])
    ext = jnp.concatenate([halo, x_ref[...]], axis=0)
    w = w_ref[...]
    pre = b_ref[...] + jnp.zeros_like(x_ref[...])
    taps = []
    for k in range(CONV_W):
        tap = _conv_shift(ext, k, tm, True)
        taps.append(tap)
        pre = pre + tap * w[k:k + 1, :]
    return pre, taps


def _conv_fwd(proj, w, b, length, tm, tc):
    hb = tm // SUBLANE

    def body(x_ref, halo_ref, w_ref, b_ref, o_ref):
        pre, _ = _conv_pre(x_ref, halo_ref, w_ref, b_ref, pl.program_id(0) == 0, tm)
        o_ref[...] = _silu(pre)

    return pl.pallas_call(
        body, name="conv_fwd", grid=(length // tm, CONV_ALL // tc),
        in_specs=[pl.BlockSpec((tm, tc), lambda i, j: (i, j)),
                  pl.BlockSpec((SUBLANE, tc), lambda i, j: (jnp.maximum(i * hb - 1, 0), j)),
                  pl.BlockSpec((CONV_W, tc), lambda i, j: (0, j)),
                  pl.BlockSpec((1, tc), lambda i, j: (0, j))],
        out_specs=pl.BlockSpec((tm, tc), lambda i, j: (i, j)),
        out_shape=jax.ShapeDtypeStruct((length, CONV_ALL), F32),
        compiler_params=_cparams(("parallel", "parallel"), VMEM_LIMIT),
    )(proj, proj, w, b)


def _conv_bwd_pre(proj, w, b, dxc, length, tm, tc):
    hb = tm // SUBLANE

    def body(x_ref, halo_ref, w_ref, b_ref, dy_ref, dpre_ref, dw_ref, db_ref):
        i = pl.program_id(1)
        pre, taps = _conv_pre(x_ref, halo_ref, w_ref, b_ref, i == 0, tm)
        sg = jax.nn.sigmoid(pre)
        dpre = dy_ref[...] * (sg * (1.0 + pre * (1.0 - sg)))
        dpre_ref[...] = dpre

        @pl.when(i == 0)
        def _():
            dw_ref[...] = jnp.zeros_like(dw_ref)
            db_ref[...] = jnp.zeros_like(db_ref)

        for k in range(CONV_W):
            dw_ref[k:k + 1, :] += jnp.sum(dpre * taps[k], axis=0, keepdims=True)
        db_ref[...] += jnp.sum(dpre, axis=0, keepdims=True)

    return pl.pallas_call(
        body, name="conv_bwd_pre", grid=(CONV_ALL // tc, length // tm),
        in_specs=[pl.BlockSpec((tm, tc), lambda j, i: (i, j)),
                  pl.BlockSpec((SUBLANE, tc), lambda j, i: (jnp.maximum(i * hb - 1, 0), j)),
                  pl.BlockSpec((CONV_W, tc), lambda j, i: (0, j)),
                  pl.BlockSpec((1, tc), lambda j, i: (0, j)),
                  pl.BlockSpec((tm, tc), lambda j, i: (i, j))],
        out_specs=[pl.BlockSpec((tm, tc), lambda j, i: (i, j)),
                   pl.BlockSpec((CONV_W, tc), lambda j, i: (0, j)),
                   pl.BlockSpec((1, tc), lambda j, i: (0, j))],
        out_shape=[jax.ShapeDtypeStruct((length, CONV_ALL), F32),
                   jax.ShapeDtypeStruct((CONV_W, CONV_ALL), F32),
                   jax.ShapeDtypeStruct((1, CONV_ALL), F32)],
        compiler_params=_cparams(("parallel", "arbitrary"), VMEM_LIMIT),
    )(proj, proj, w, b, dxc)


def _conv_bwd_x(dpre, w, length, tm, tc):
    hb = tm // SUBLANE
    n_t = length // tm
    last_blk = length // SUBLANE - 1

    def body(d_ref, halo_ref, w_ref, o_ref):
        halo = jnp.where(pl.program_id(0) == n_t - 1, 0.0, halo_ref[...])
        ext = jnp.concatenate([d_ref[...], halo], axis=0)
        w = w_ref[...]
        acc = jnp.zeros_like(d_ref[...])
        for k in range(CONV_W):
            acc = acc + _conv_shift(ext, k, tm, False) * w[k:k + 1, :]
        o_ref[...] = acc

    return pl.pallas_call(
        body, name="conv_bwd_x", grid=(n_t, CONV_ALL // tc),
        in_specs=[pl.BlockSpec((tm, tc), lambda i, j: (i, j)),
                  pl.BlockSpec((SUBLANE, tc), lambda i, j: (jnp.minimum((i + 1) * hb, last_blk), j)),
                  pl.BlockSpec((CONV_W, tc), lambda i, j: (0, j))],
        out_specs=pl.BlockSpec((tm, tc), lambda i, j: (i, j)),
        out_shape=jax.ShapeDtypeStruct((length, CONV_ALL), F32),
        compiler_params=_cparams(("parallel", "parallel"), VMEM_LIMIT),
    )(dpre, dpre, w)


def _gdn_fwd(xc, proj, b_col, a_col, alog, dtb, nw, length):
    nc = length // CHUNK
    h_ = GDN_HEADS

    hw = h_ * LANE

    def body(q_ref, k_ref, v_ref, gz_ref, b_ref, a_ref, al_ref, dt_ref, nw_ref, on_ref, st_ref, s_ref):
        @pl.when(pl.program_id(0) == 0)
        def _():
            s_ref[...] = jnp.zeros_like(s_ref)

        nw = nw_ref[...]
        res = []
        for h in range(h_):
            cols = slice(h * LANE, (h + 1) * LANE)
            s_old = s_ref[h]
            st_ref[h] = s_old
            res.append(_gdn_chunk(q_ref[:, cols], k_ref[:, cols], v_ref[:, cols], gz_ref[:, cols], b_ref[h],
                                  a_ref[h], al_ref[h], dt_ref[h], nw, s_old))
        for h in range(h_):
            on_ref[:, h * LANE:(h + 1) * LANE] = res[h][0]
            s_ref[h] = res[h][1]

    blk = lambda name: pl.BlockSpec((CHUNK, hw), lambda c: (c, _MINE[name] // hw))
    col = pl.BlockSpec((h_, CHUNK, 1), lambda c: (0, c, 0))
    one = pl.BlockSpec((h_, 1, 1), lambda c: (0, 0, 0))
    return pl.pallas_call(
        body, name="gdn_fwd", grid=(nc,),
        in_specs=[blk("gq"), blk("gk"), blk("gv"), blk("gz"), col, col, one, one,
                  pl.BlockSpec((1, LANE), lambda c: (0, 0))],
        out_specs=[pl.BlockSpec((CHUNK, hw), lambda c: (c, 0)),
                   pl.BlockSpec((h_, None, LANE, LANE), lambda c: (0, c, 0, 0))],
        out_shape=[jax.ShapeDtypeStruct((length, hw), F32),
                   jax.ShapeDtypeStruct((h_, nc, LANE, LANE), F32)],
        scratch_shapes=[pltpu.VMEM((h_, LANE, LANE), F32)],
        compiler_params=_cparams(("arbitrary",), VMEM_LIMIT),
    )(xc, xc, xc, proj, b_col, a_col, alog, dtb, nw)


def _gdn_bwd(xc, proj, b_col, a_col, alog, dtb, nw, st, dmix, length):
    nc = length // CHUNK
    h_ = GDN_HEADS

    def body(q_ref, k_ref, v_ref, gz_ref, b_ref, a_ref, al_ref, dt_ref, nw_ref, st_ref, do_ref,
             dq_ref, dk_ref, dv_ref, dgz_ref, db_ref, da_ref, dal_ref, ddt_ref, dnw_ref, ds_ref):
        @pl.when(pl.program_id(0) == 0)
        def _():
            ds_ref[...] = jnp.zeros_like(ds_ref)
            dal_ref[...] = jnp.zeros_like(dal_ref)
            ddt_ref[...] = jnp.zeros_like(ddt_ref)
            dnw_ref[...] = jnp.zeros_like(dnw_ref)

        nw = nw_ref[...]
        res = []
        for h in range(h_):
            cols = slice(h * LANE, (h + 1) * LANE)
            ins = (q_ref[:, cols], k_ref[:, cols], v_ref[:, cols], gz_ref[:, cols], b_ref[h], a_ref[h], al_ref[h],
                   dt_ref[h], nw, st_ref[h])
            _, vjp = jax.vjp(_gdn_chunk, *ins)
            res.append(vjp((do_ref[:, cols], ds_ref[h])))
        dnw_tot = dnw_ref[...]
        for h in range(h_):
            cols = slice(h * LANE, (h + 1) * LANE)
            dq, dk, dv, dgz, db, da, dal, ddt, dnw, ds = res[h]
            dq_ref[:, cols] = dq
            dk_ref[:, cols] = dk
            dv_ref[:, cols] = dv
            dgz_ref[:, cols] = dgz
            db_ref[h] = db
            da_ref[h] = da
            dal_ref[h] += dal
            ddt_ref[h] += ddt
            dnw_tot = dnw_tot + dnw
            ds_ref[h] = ds
        dnw_ref[...] = dnw_tot

    hw = h_ * LANE
    r = lambda c: nc - 1 - c
    blk = lambda name: pl.BlockSpec((CHUNK, hw), lambda c: (r(c), _MINE[name] // hw))
    col = pl.BlockSpec((h_, CHUNK, 1), lambda c: (0, r(c), 0))
    one = pl.BlockSpec((h_, 1, 1), lambda c: (0, 0, 0))
    oblk = pl.BlockSpec((CHUNK, hw), lambda c: (r(c), 0))
    big = jax.ShapeDtypeStruct((length, hw), F32)
    return pl.pallas_call(
        body, name="gdn_bwd", grid=(nc,),
        in_specs=[blk("gq"), blk("gk"), blk("gv"), blk("gz"), col, col, one, one,
                  pl.BlockSpec((1, LANE), lambda c: (0, 0)),
                  pl.BlockSpec((h_, None, LANE, LANE), lambda c: (0, r(c), 0, 0)),
                  oblk],
        out_specs=[oblk, oblk, oblk, oblk, col, col, one, one, pl.BlockSpec((1, LANE), lambda c: (0, 0))],
        out_shape=[big, big, big, big,
                   jax.ShapeDtypeStruct((h_, length, 1), F32), jax.ShapeDtypeStruct((h_, length, 1), F32),
                   jax.ShapeDtypeStruct((h_, 1, 1), F32), jax.ShapeDtypeStruct((h_, 1, 1), F32),
                   jax.ShapeDtypeStruct((1, LANE), F32)],
        scratch_shapes=[pltpu.VMEM((h_, LANE, LANE), F32)],
        compiler_params=_cparams(("arbitrary",), VMEM_LIMIT),
    )(xc, xc, xc, proj, b_col, a_col, alog, dtb, nw, st, dmix)


def _ssd_fwd(xc, dt_col, alog, dtb, dpar, length):
    nc = length // CHUNK
    ppg = PAIRS_PER_GROUP

    gw = ppg * LANE

    def body(x_ref, b_ref, c_ref, dt_ref, al_ref, db_ref, dp_ref, y_ref, st_ref, hs_ref):
        @pl.when(pl.program_id(1) == 0)
        def _():
            hs_ref[...] = jnp.zeros_like(hs_ref)

        bm, cm = b_ref[...], c_ref[...]
        for p in range(ppg):
            st_ref[p] = hs_ref[p]
            y, hs_new = _ssd_chunk(x_ref[:, p * LANE:(p + 1) * LANE], bm, cm, dt_ref[p], al_ref[p], db_ref[p],
                                   dp_ref[p], hs_ref[p])
            y_ref[:, p * LANE:(p + 1) * LANE] = y
            hs_ref[p] = hs_new

    two = pl.BlockSpec((None, ppg, 1, 2), lambda g, c: (g, 0, 0, 0))
    return pl.pallas_call(
        body, name="ssd_fwd", grid=(SSD_GROUPS, nc),
        in_specs=[pl.BlockSpec((CHUNK, gw), lambda g, c: (c, _MINE["sx"] // gw + g)),
                  pl.BlockSpec((CHUNK, LANE), lambda g, c: (c, CB("sB") + g)),
                  pl.BlockSpec((CHUNK, LANE), lambda g, c: (c, CB("sC") + g)),
                  pl.BlockSpec((None, ppg, CHUNK, 2), lambda g, c: (g, 0, c, 0)),
                  two, two, two],
        out_specs=[pl.BlockSpec((CHUNK, gw), lambda g, c: (c, g)),
                   pl.BlockSpec((None, ppg, None, LANE, SSD_N), lambda g, c: (g, 0, c, 0, 0))],
        out_shape=[jax.ShapeDtypeStruct((length, SSD_GROUPS * gw), F32),
                   jax.ShapeDtypeStruct((SSD_GROUPS, ppg, nc, LANE, SSD_N), F32)],
        scratch_shapes=[pltpu.VMEM((ppg, LANE, SSD_N), F32)],
        compiler_params=_cparams(("parallel", "arbitrary"), VMEM_LIMIT),
    )(xc, xc, xc, dt_col, alog, dtb, dpar)


def _ssd_bwd(xc, dt_col, alog, dtb, dpar, st, dy, length):
    nc = length // CHUNK
    ppg = PAIRS_PER_GROUP

    gw = ppg * LANE

    def body(x_ref, b_ref, c_ref, dt_ref, al_ref, db_ref, dp_ref, st_ref, dy_ref,
             dx_ref, dbm_ref, dcm_ref, ddt_ref, dal_ref, ddb_ref, ddp_ref, dhs_ref):
        @pl.when(pl.program_id(1) == 0)
        def _():
            dhs_ref[...] = jnp.zeros_like(dhs_ref)
            dal_ref[...] = jnp.zeros_like(dal_ref)
            ddb_ref[...] = jnp.zeros_like(ddb_ref)
            ddp_ref[...] = jnp.zeros_like(ddp_ref)

        bm, cm = b_ref[...], c_ref[...]
        dbm_tot = jnp.zeros_like(bm)
        dcm_tot = jnp.zeros_like(cm)
        for p in range(ppg):
            cols = slice(p * LANE, (p + 1) * LANE)
            ins = (x_ref[:, cols], bm, cm, dt_ref[p], al_ref[p], db_ref[p], dp_ref[p], st_ref[p])
            _, vjp = jax.vjp(_ssd_chunk, *ins)
            dx, dbm, dcm, ddt, dal, ddb, ddp, dhs = vjp((dy_ref[:, cols], dhs_ref[p]))
            dx_ref[:, cols] = dx
            ddt_ref[p] = ddt
            dbm_tot = dbm_tot + dbm
            dcm_tot = dcm_tot + dcm
            dhs_ref[p] = dhs
            dal_ref[p] += dal
            ddb_ref[p] += ddb
            ddp_ref[p] += ddp
        dbm_ref[...] = dbm_tot
        dcm_ref[...] = dcm_tot

    r = lambda c: nc - 1 - c
    two = pl.BlockSpec((None, ppg, 1, 2), lambda g, c: (g, 0, 0, 0))
    dtspec = pl.BlockSpec((None, ppg, CHUNK, 2), lambda g, c: (g, 0, r(c), 0))
    grp = pl.BlockSpec((CHUNK, LANE), lambda g, c: (r(c), g))
    prm = jax.ShapeDtypeStruct((SSD_GROUPS, ppg, 1, 2), F32)
    return pl.pallas_call(
        body, name="ssd_bwd", grid=(SSD_GROUPS, nc),
        in_specs=[pl.BlockSpec((CHUNK, gw), lambda g, c: (r(c), _MINE["sx"] // gw + g)),
                  pl.BlockSpec((CHUNK, LANE), lambda g, c: (r(c), CB("sB") + g)),
                  pl.BlockSpec((CHUNK, LANE), lambda g, c: (r(c), CB("sC") + g)),
                  dtspec, two, two, two,
                  pl.BlockSpec((None, ppg, None, LANE, SSD_N), lambda g, c: (g, 0, r(c), 0, 0)),
                  pl.BlockSpec((CHUNK, gw), lambda g, c: (r(c), g))],
        out_specs=[pl.BlockSpec((CHUNK, gw), lambda g, c: (r(c), g)), grp, grp, dtspec, two, two, two],
        out_shape=[jax.ShapeDtypeStruct((length, SSD_GROUPS * gw), F32),
                   jax.ShapeDtypeStruct((length, SSD_GROUPS * SSD_N), F32),
                   jax.ShapeDtypeStruct((length, SSD_GROUPS * SSD_N), F32),
                   jax.ShapeDtypeStruct((SSD_GROUPS, ppg, length, 2), F32), prm, prm, prm],
        scratch_shapes=[pltpu.VMEM((ppg, LANE, SSD_N), F32)],
        compiler_params=_cparams(("parallel", "arbitrary"), VMEM_LIMIT),
    )(xc, xc, xc, dt_col, alog, dtb, dpar, st, dy)


def _ret_fwd(proj, cos2, sin2, dmat, qdec, kdec, cdec, nw, length):
    nc = length // CHUNK
    h_ = RET_HEADS

    def body(q_ref, k_ref, v_ref, g_ref, cos_ref, sin_ref, dm_ref, qd_ref, kd_ref, cd_ref, nw_ref,
             on_ref, st_ref, r_ref):
        @pl.when(pl.program_id(0) == 0)
        def _():
            r_ref[...] = jnp.zeros_like(r_ref)

        cos2_, sin2_, nw_ = cos_ref[...], sin_ref[...], nw_ref[...]
        res = []
        for h in range(h_):
            cols = slice(h * LANE, (h + 1) * LANE)
            r_old = r_ref[h]
            st_ref[h] = r_old
            res.append(_ret_chunk(q_ref[:, cols], k_ref[:, cols], v_ref[:, cols], g_ref[:, cols], cos2_, sin2_,
                                  dm_ref[h], qd_ref[h], kd_ref[h], cd_ref[h], nw_, r_old))
        for h in range(h_):
            on_ref[:, h * LANE:(h + 1) * LANE] = res[h][0]
            r_ref[h] = res[h][1]

    hw = h_ * LANE
    blk = lambda name: pl.BlockSpec((CHUNK, hw), lambda c: (c, _MINE[name] // hw))
    tab = pl.BlockSpec((CHUNK, LANE), lambda c: (c, 0))
    full = lambda *s: pl.BlockSpec(s, lambda c: (0,) * len(s))
    return pl.pallas_call(
        body, name="ret_fwd", grid=(nc,),
        in_specs=[blk("rq"), blk("rk"), blk("rv"), blk("rg"), tab, tab,
                  full(h_, CHUNK, CHUNK), full(h_, CHUNK, 1), full(h_, CHUNK, 1), full(h_, 1, 1), full(1, LANE)],
        out_specs=[pl.BlockSpec((CHUNK, hw), lambda c: (c, 0)),
                   pl.BlockSpec((h_, None, LANE, LANE), lambda c: (0, c, 0, 0))],
        out_shape=[jax.ShapeDtypeStruct((length, hw), F32),
                   jax.ShapeDtypeStruct((h_, nc, LANE, LANE), F32)],
        scratch_shapes=[pltpu.VMEM((h_, LANE, LANE), F32)],
        compiler_params=_cparams(("arbitrary",), VMEM_LIMIT),
    )(proj, proj, proj, proj, cos2, sin2, dmat, qdec, kdec, cdec, nw)


def _ret_bwd(proj, cos2, sin2, dmat, qdec, kdec, cdec, nw, st, dmix, length):
    nc = length // CHUNK
    h_ = RET_HEADS

    def body(q_ref, k_ref, v_ref, g_ref, cos_ref, sin_ref, dm_ref, qd_ref, kd_ref, cd_ref, nw_ref, st_ref,
             do_ref, dq_ref, dk_ref, dv_ref, dg_ref, dnw_ref, dr_ref):
        @pl.when(pl.program_id(0) == 0)
        def _():
            dr_ref[...] = jnp.zeros_like(dr_ref)
            dnw_ref[...] = jnp.zeros_like(dnw_ref)

        cos2_, sin2_, nw_ = cos_ref[...], sin_ref[...], nw_ref[...]
        res = []
        for h in range(h_):
            cols = slice(h * LANE, (h + 1) * LANE)
            consts = (cos2_, sin2_, dm_ref[h], qd_ref[h], kd_ref[h], cd_ref[h])
            f = lambda q, k, v, g, w_, r_, consts=consts: _ret_chunk(q, k, v, g, *consts, w_, r_)
            _, vjp = jax.vjp(f, q_ref[:, cols], k_ref[:, cols], v_ref[:, cols], g_ref[:, cols], nw_, st_ref[h])
            res.append(vjp((do_ref[:, cols], dr_ref[h])))
        dnw_tot = dnw_ref[...]
        for h in range(h_):
            cols = slice(h * LANE, (h + 1) * LANE)
            dq, dk, dv, dg, dnw, dr = res[h]
            dq_ref[:, cols] = dq
            dk_ref[:, cols] = dk
            dv_ref[:, cols] = dv
            dg_ref[:, cols] = dg
            dnw_tot = dnw_tot + dnw
            dr_ref[h] = dr
        dnw_ref[...] = dnw_tot

    hw = h_ * LANE
    r = lambda c: nc - 1 - c
    blk = lambda name: pl.BlockSpec((CHUNK, hw), lambda c: (r(c), _MINE[name] // hw))
    tab = pl.BlockSpec((CHUNK, LANE), lambda c: (r(c), 0))
    full = lambda *s: pl.BlockSpec(s, lambda c: (0,) * len(s))
    oblk = pl.BlockSpec((CHUNK, hw), lambda c: (r(c), 0))
    big = jax.ShapeDtypeStruct((length, hw), F32)
    return pl.pallas_call(
        body, name="ret_bwd", grid=(nc,),
        in_specs=[blk("rq"), blk("rk"), blk("rv"), blk("rg"), tab, tab,
                  full(h_, CHUNK, CHUNK), full(h_, CHUNK, 1), full(h_, CHUNK, 1), full(h_, 1, 1), full(1, LANE),
                  pl.BlockSpec((h_, None, LANE, LANE), lambda c: (0, r(c), 0, 0)),
                  pl.BlockSpec((CHUNK, hw), lambda c: (r(c), (MIX_W - hw) // hw))],
        out_specs=[oblk, oblk, oblk, oblk, full(1, LANE)],
        out_shape=[big, big, big, big, jax.ShapeDtypeStruct((1, LANE), F32)],
        scratch_shapes=[pltpu.VMEM((h_, LANE, LANE), F32)],
        compiler_params=_cparams(("arbitrary",), VMEM_LIMIT),
    )(proj, proj, proj, proj, cos2, sin2, dmat, qdec, kdec, cdec, nw, st, dmix)


def _mesh_pos():
    x, y, c = lax.axis_index("x"), lax.axis_index("y"), lax.axis_index("c")
    return x, y, c, 4 * x + 2 * y + c


def _peer(x, y, c, mask):
    return (x ^ ((mask >> 2) & 1), y ^ ((mask >> 1) & 1), c ^ (mask & 1))


def _all_gather(name, shards):
    n = len(shards)

    def body(*refs):
        ins, outs = refs[:n], refs[n:2 * n]
        send_sems, recv_sems, local_sems = refs[2 * n:]
        x, y, c, me = _mesh_pos()
        local = [pltpu.make_async_copy(ins[a], outs[a].at[me], local_sems.at[a]) for a in range(n)]
        for cp in local:
            cp.start()
        copies = []
        for mask in range(1, N_DEV):
            for a in range(n):
                k = (mask - 1) * n + a
                copies.append(pltpu.make_async_remote_copy(
                    src_ref=ins[a], dst_ref=outs[a].at[me], send_sem=send_sems.at[k], recv_sem=recv_sems.at[k],
                    device_id=_peer(x, y, c, mask), device_id_type=pl.DeviceIdType.MESH))
        for cp in copies:
            cp.start()
        for cp in copies:
            cp.wait_send()
        for mask in range(1, N_DEV):
            px, py, pc = _peer(x, y, c, mask)
            for a in range(n):
                k = (mask - 1) * n + a
                pltpu.make_async_remote_copy(
                    src_ref=ins[a], dst_ref=outs[a].at[4 * px + 2 * py + pc], send_sem=send_sems.at[k],
                    recv_sem=recv_sems.at[k], device_id=(px, py, pc), device_id_type=pl.DeviceIdType.MESH).wait_recv()
        for cp in local:
            cp.wait()

    hbm = pl.BlockSpec(memory_space=pl.ANY)
    k_tot = (N_DEV - 1) * n
    return pl.pallas_call(
        body, name=name, in_specs=[hbm] * n, out_specs=[hbm] * n,
        out_shape=[jax.ShapeDtypeStruct((N_DEV,) + s.shape, s.dtype) for s in shards],
        scratch_shapes=[pltpu.SemaphoreType.DMA((k_tot,)), pltpu.SemaphoreType.DMA((k_tot,)),
                        pltpu.SemaphoreType.DMA((n,))],
        compiler_params=pltpu.CompilerParams(has_side_effects=True),
    )(*shards)


def _exchange(name, slabs):
    n = len(slabs)

    def body(*refs):
        ins, outs = refs[:n], refs[n:2 * n]
        send_sems, recv_sems, local_sems = refs[2 * n:]
        x, y, c, me = _mesh_pos()
        local = [pltpu.make_async_copy(ins[a].at[me], outs[a].at[me], local_sems.at[a]) for a in range(n)]
        for cp in local:
            cp.start()
        copies = []
        for mask in range(1, N_DEV):
            px, py, pc = _peer(x, y, c, mask)
            for a in range(n):
                k = (mask - 1) * n + a
                copies.append(pltpu.make_async_remote_copy(
                    src_ref=ins[a].at[4 * px + 2 * py + pc], dst_ref=outs[a].at[me], send_sem=send_sems.at[k],
                    recv_sem=recv_sems.at[k], device_id=(px, py, pc), device_id_type=pl.DeviceIdType.MESH))
        for cp in copies:
            cp.start()
        for cp in copies:
            cp.wait_send()
        for mask in range(1, N_DEV):
            px, py, pc = _peer(x, y, c, mask)
            for a in range(n):
                k = (mask - 1) * n + a
                pltpu.make_async_remote_copy(
                    src_ref=ins[a].at[me], dst_ref=outs[a].at[4 * px + 2 * py + pc], send_sem=send_sems.at[k],
                    recv_sem=recv_sems.at[k], device_id=(px, py, pc), device_id_type=pl.DeviceIdType.MESH).wait_recv()
        for cp in local:
            cp.wait()

    hbm = pl.BlockSpec(memory_space=pl.ANY)
    k_tot = (N_DEV - 1) * n
    return pl.pallas_call(
        body, name=name, in_specs=[hbm] * n, out_specs=[hbm] * n,
        out_shape=[jax.ShapeDtypeStruct(s.shape, s.dtype) for s in slabs],
        scratch_shapes=[pltpu.SemaphoreType.DMA((k_tot,)), pltpu.SemaphoreType.DMA((k_tot,)),
                        pltpu.SemaphoreType.DMA((n,))],
        compiler_params=pltpu.CompilerParams(has_side_effects=True),
    )(*slabs)


def _adam(name, parts, w, m, v, tr):
    p_, r_, cw = parts.shape
    c1 = 1.0 - ADAM_B1 ** ADAM_STEP
    c2 = 1.0 - ADAM_B2 ** ADAM_STEP

    def body(p_ref, w_ref, m_ref, v_ref, g_ref, d_ref, nm_ref, nv_ref):
        g = p_ref[0]
        for i in range(1, p_):
            g = g + p_ref[i]
        nm = ADAM_B1 * m_ref[...] + (1.0 - ADAM_B1) * g
        nv = ADAM_B2 * v_ref[...] + (1.0 - ADAM_B2) * (g * g)
        d_ref[...] = -ADAM_LR * ((nm / c1) / (jnp.sqrt(nv / c2) + ADAM_EPS) + ADAM_WD * w_ref[...])
        g_ref[...] = g
        nm_ref[...] = nm
        nv_ref[...] = nv

    spec = pl.BlockSpec((tr, cw), lambda i: (i, 0))
    shp = jax.ShapeDtypeStruct((r_, cw), F32)
    return pl.pallas_call(
        body, name=name, grid=(r_ // tr,),
        in_specs=[pl.BlockSpec((p_, tr, cw), lambda i: (0, i, 0)), spec, spec, spec],
        out_specs=[spec] * 4, out_shape=[shp] * 4,
        compiler_params=_cparams(("parallel",), VMEM_LIMIT),
    )(parts, w, m, v)


def _perm_cols(w_full):
    parts = [w_full[:, _ORIG[n][0]:_ORIG[n][0] + _ORIG[n][1]] for n in _ORDER]
    parts.append(jnp.zeros((w_full.shape[0], N_PAD - N_USED), w_full.dtype))
    return jnp.concatenate(parts, axis=1)


def _unperm_cols(g):
    by_orig = sorted(_ORDER, key=lambda n: _ORIG[n][0])
    return jnp.concatenate([g[:, _MINE[n]:_MINE[n] + _ORIG[n][1]] for n in by_orig], axis=1)


def _ret_consts(length):
    lg = jnp.log(1.0 - 2.0 ** (-5.0 - jnp.arange(RET_HEADS, dtype=F32)))
    idx = jnp.arange(CHUNK, dtype=F32)
    rel = idx[:, None] - idx[None, :]
    dmat = jnp.where(rel[None] >= 0, jnp.exp(jnp.maximum(rel, 0.0)[None] * lg[:, None, None]), 0.0)
    qdec = jnp.exp((idx[:, None] + 1.0) * lg[None, :]).T[:, :, None]
    kdec = jnp.exp((CHUNK - 1.0 - idx)[:, None] * lg[None, :]).T[:, :, None]
    cdec = jnp.exp(CHUNK * lg)[:, None, None]
    half = RET_DK // 2
    inv = ROPE_BASE ** (-jnp.arange(half, dtype=F32) / half)
    ang = jnp.arange(length, dtype=jnp.int32).astype(F32)[:, None] * inv[None, :]
    cos, sin = jnp.cos(ang), jnp.sin(ang)
    return (jnp.concatenate([cos, cos], axis=1), jnp.concatenate([-sin, sin], axis=1), dmat, qdec, kdec, cdec)


def _tiles(length):
    tm = 512 if length % 512 == 0 else length
    return tm


def _layer_fwd(x, p, consts, length, last, target):
    tm = _tiles(length)
    (h,), _ = _rowmap("pre_norm", lambda x_, w_: ((_rms(x_, w_),), ()), length, tm,
                      [(x, D_MODEL, 0)], [p["pre_norm"]], out_ws=[D_MODEL], out_dtypes=[BF16])
    proj = _matmul("in_proj", h, p["w_in"], "nn", length, N_PAD, D_MODEL)
    xc = _conv_fwd(proj, p["conv_w"], p["conv_b"], length, tm, 512)
    small = proj[:, SMALL_OFF:SMALL_OFF + LANE]
    b_col = small[:, 0:4].T[:, :, None]
    a_col = small[:, 4:8].T[:, :, None]
    dt_col = small[:, 8:24].reshape(length, SSD_GROUPS, PAIRS_PER_GROUP, 2).transpose(1, 2, 0, 3)
    o_a, st_a = _gdn_fwd(xc, proj, b_col, a_col, p["gdn_A_log"], p["gdn_dt_bias"], p["gdn_norm"], length)
    y_b, st_b = _ssd_fwd(xc, dt_col, p["ssd_A_log"], p["ssd_dt_bias"], p["ssd_D"], length)
    o_b = []
    for g in range(SSD_GROUPS):
        (o,), _ = _rowmap(f"ssd_gate{g}", lambda y_, z_, w_: ((_rms(y_ * _silu(z_), w_),), ()), length, tm,
                          [(y_b, 512, g), (proj, 512, _MINE["sz"] // 512 + g)], [p["ssd_norm"][g]], out_ws=[512])
        o_b.append(o)
    o_c, st_c = _ret_fwd(proj, *consts, p["ret_norm"], length)
    mixed = jnp.concatenate([o.astype(BF16) for o in [o_a] + o_b + [o_c]], axis=1)
    out = _matmul("out_proj", mixed, p["w_out"], "nn", length, D_MODEL, MIX_W)
    res = dict(x=x, h=h, proj=proj, xc=xc, b_col=b_col, a_col=a_col, dt_col=dt_col, st_a=st_a, st_b=st_b,
               st_c=st_c, y_b=y_b, mixed=mixed, out=out)
    if not last:
        (y,), _ = _rowmap("post_norm", lambda x_, o_, w_: ((x_ + _rms(o_, w_),), ()), length, tm,
                          [(x, D_MODEL, 0), (out, D_MODEL, 0)], [p["post_norm"]], out_ws=[D_MODEL])
        return y, res, None

    def head(x_, o_, t_, w_):
        e = x_ + _rms(o_, w_) - t_
        row = jnp.mean(e * e, axis=-1, keepdims=True)
        loss = 0.5 * jnp.sum(row, axis=0, keepdims=True)
        return (e * (1.0 / D_MODEL),), (loss + jnp.zeros((1, LANE), F32),)

    (dy,), (loss,) = _rowmap("loss_head", head, length, tm,
                             [(x, D_MODEL, 0), (out, D_MODEL, 0), (target, D_MODEL, 0)], [p["post_norm"]],
                             out_ws=[D_MODEL], acc_ws=[LANE])
    return None, res, (dy, loss[0, 0])


def _layer_bwd(dy, p, res, consts, length):
    tm = _tiles(length)
    g = {}
    (d_out,), (g["post_norm"],) = _rowmap("post_norm_bwd", lambda o_, w_: ((_rms(o_, w_),), ()), length, tm,
                                          [(res["out"], D_MODEL, 0)], [p["post_norm"]], cts=[(dy, D_MODEL, 0)])
    dmix = _matmul("out_proj_dx", d_out, p["w_out"], "nt", length, MIX_W, D_MODEL)
    g["w_out"] = _matmul("out_proj_dw", res["mixed"], d_out, "tn", MIX_W, D_MODEL, length)
    proj, xc = res["proj"], res["xc"]
    (dgq, dgk, dgv, dgz, db, da, g["gdn_A_log"], g["gdn_dt_bias"], dnw_a) = _gdn_bwd(
        xc, proj, res["b_col"], res["a_col"], p["gdn_A_log"], p["gdn_dt_bias"], p["gdn_norm"], res["st_a"], dmix, length)
    g["gdn_norm"] = dnw_a
    dyb, dsz, dsn = [], [], []
    for gi in range(SSD_GROUPS):
        (dy_g, dz_g), (dn_g,) = _rowmap(
            f"ssd_gate_bwd{gi}", lambda y_, z_, w_: ((_rms(y_ * _silu(z_), w_),), ()), length, tm,
            [(res["y_b"], 512, gi), (proj, 512, _MINE["sz"] // 512 + gi)], [p["ssd_norm"][gi]],
            cts=[(dmix, 512, 1 + gi)])
        dyb.append(dy_g)
        dsz.append(dz_g)
        dsn.append(dn_g)
    g["ssd_norm"] = dsn
    dyb = jnp.concatenate(dyb, axis=1)
    dsx, dsb, dsc, ddt, g["ssd_A_log"], g["ssd_dt_bias"], g["ssd_D"] = _ssd_bwd(
        xc, res["dt_col"], p["ssd_A_log"], p["ssd_dt_bias"], p["ssd_D"], res["st_b"], dyb, length)
    drq, drk, drv, drg, dnw_c = _ret_bwd(proj, *consts, p["ret_norm"], res["st_c"], dmix, length)
    g["ret_norm"] = dnw_c
    dxc = jnp.concatenate([dgq, dgk, dgv, dsx, dsb, dsc], axis=1)
    dpre, g["conv_w"], g["conv_b"] = _conv_bwd_pre(proj, p["conv_w"], p["conv_b"], dxc, length, tm, 512)
    dconv = _conv_bwd_x(dpre, p["conv_w"], length, tm, 512)
    dsmall = jnp.concatenate([db[:, :, 0].T, da[:, :, 0].T, ddt.transpose(2, 0, 1, 3).reshape(length, SSD_HEADS)], axis=1)
    dproj = jnp.concatenate([d.astype(BF16) for d in [dconv, dgz] + dsz + [drq, drk, drv, drg, dsmall]]
                            + [jnp.zeros((length, N_PAD - N_USED), BF16)], axis=1)
    dh = _matmul("in_proj_dx", dproj, p["w_in"], "nt", length, D_MODEL, N_PAD)
    g["w_in"] = _matmul("in_proj_dw", res["h"], dproj, "tn", D_MODEL, N_PAD, length)
    (dx,), (g["pre_norm"],) = _rowmap(
        "pre_norm_bwd", lambda x_, w_: ((_rms(x_, w_), x_), ()), length, tm,
        [(res["x"], D_MODEL, 0)], [p["pre_norm"]], cts=[(dh, D_MODEL, 0), (dy, D_MODEL, 0)])
    return dx, g


def _make_layers(w_in_full, w_out_full, conv_full, sw):
    layers = []
    for l in range(DEPTH):
        layers.append(dict(
            pre_norm=sw["pre_norm"][l][None], post_norm=sw["post_norm"][l][None], w_in=_perm_cols(w_in_full[l]),
            w_out=w_out_full[l],
            conv_w=jnp.concatenate([conv_full[0, l], conv_full[1, l]], axis=1),
            conv_b=jnp.concatenate([jnp.zeros((CONV_CH,), F32), sw["ssd_conv_b"][l]])[None],
            gdn_A_log=sw["gdn_A_log"][l][:, None, None], gdn_dt_bias=sw["gdn_dt_bias"][l][:, None, None],
            gdn_norm=sw["gdn_norm"][l][None],
            ssd_A_log=sw["ssd_A_log"][l].reshape(SSD_GROUPS, PAIRS_PER_GROUP, 1, 2),
            ssd_dt_bias=sw["ssd_dt_bias"][l].reshape(SSD_GROUPS, PAIRS_PER_GROUP, 1, 2),
            ssd_D=sw["ssd_D"][l].reshape(SSD_GROUPS, PAIRS_PER_GROUP, 1, 2),
            ssd_norm=[sw["ssd_norm"][l][None, 512 * g:512 * (g + 1)] for g in range(SSD_GROUPS)],
            ret_norm=sw["ret_norm"][l][None]))
    return layers


def _local_step(xb, tgt, layers, length):
    consts = _ret_consts(length)
    cur = xb
    saved = []
    extras = None
    for l in range(DEPTH):
        cur, res, extras = _layer_fwd(cur, layers[l], consts, length, l == DEPTH - 1, tgt)
        saved.append(res)
    dy, loss_local = extras
    grads = [None] * DEPTH
    for l in reversed(range(DEPTH)):
        dy, grads[l] = _layer_bwd(dy, layers[l], saved[l], consts, length)
    return loss_local, dy, grads


_SMALL = ["pre_norm", "post_norm", "gdn_A_log", "gdn_dt_bias", "gdn_norm", "ssd_conv_b", "ssd_A_log",
          "ssd_dt_bias", "ssd_D", "ssd_norm", "ret_norm"]


def _pack_small(arrs):
    rows = []
    for n in _SMALL:
        flat = arrs[n].reshape(-1)
        pad = (-flat.shape[0]) % LANE
        rows.append(jnp.pad(flat, (0, pad)).reshape(-1, LANE))
    out = jnp.concatenate(rows, axis=0)
    return jnp.pad(out, ((0, (-out.shape[0]) % SUBLANE), (0, 0)))


def _unpack_small(packed, like):
    out, r = {}, 0
    for n in _SMALL:
        cnt = like[n].size
        nrow = -(-cnt // LANE)
        out[n] = packed[r:r + nrow].reshape(-1)[:cnt].reshape(like[n].shape)
        r += nrow
    return out


def kernel(x, pre_norm, post_norm, w_in, gdn_conv, gdn_A_log, gdn_dt_bias, gdn_norm, ssd_conv, ssd_conv_b, ssd_A_log, ssd_dt_bias, ssd_D, ssd_norm, ret_norm, w_out, loss_target, m_pre_norm, m_post_norm, m_w_in, m_gdn_conv, m_gdn_A_log, m_gdn_dt_bias, m_gdn_norm, m_ssd_conv, m_ssd_conv_b, m_ssd_A_log, m_ssd_dt_bias, m_ssd_D, m_ssd_norm, m_ret_norm, m_w_out, v_pre_norm, v_post_norm, v_w_in, v_gdn_conv, v_gdn_A_log, v_gdn_dt_bias, v_gdn_norm, v_ssd_conv, v_ssd_conv_b, v_ssd_A_log, v_ssd_dt_bias, v_ssd_D, v_ssd_norm, v_ret_norm, v_w_out):
    length = x.shape[1]
    xb = x[0]
    tgt = loss_target[0]
    small_w = dict(pre_norm=pre_norm, post_norm=post_norm, gdn_A_log=gdn_A_log, gdn_dt_bias=gdn_dt_bias,
                   gdn_norm=gdn_norm, ssd_conv_b=ssd_conv_b, ssd_A_log=ssd_A_log, ssd_dt_bias=ssd_dt_bias,
                   ssd_D=ssd_D, ssd_norm=ssd_norm, ret_norm=ret_norm)
    small_m = dict(pre_norm=m_pre_norm, post_norm=m_post_norm, gdn_A_log=m_gdn_A_log, gdn_dt_bias=m_gdn_dt_bias,
                   gdn_norm=m_gdn_norm, ssd_conv_b=m_ssd_conv_b, ssd_A_log=m_ssd_A_log, ssd_dt_bias=m_ssd_dt_bias,
                   ssd_D=m_ssd_D, ssd_norm=m_ssd_norm, ret_norm=m_ret_norm)
    small_v = dict(pre_norm=v_pre_norm, post_norm=v_post_norm, gdn_A_log=v_gdn_A_log, gdn_dt_bias=v_gdn_dt_bias,
                   gdn_norm=v_gdn_norm, ssd_conv_b=v_ssd_conv_b, ssd_A_log=v_ssd_A_log, ssd_dt_bias=v_ssd_dt_bias,
                   ssd_D=v_ssd_D, ssd_norm=v_ssd_norm, ret_norm=v_ret_norm)

    conv_shard = jnp.stack([gdn_conv, ssd_conv], axis=0)
    g_in, g_out, g_conv = _all_gather("gather_weights", [w_in.astype(BF16), w_out.astype(BF16), conv_shard])
    w_in_full = g_in.transpose(1, 2, 0, 3).reshape(DEPTH, D_MODEL, N_IN)
    w_out_full = g_out.transpose(1, 0, 2, 3).reshape(DEPTH, MIX_W, D_MODEL)
    conv_full = g_conv.transpose(1, 2, 3, 0, 4).reshape(2, DEPTH, CONV_W, CONV_CH)

    layers = _make_layers(w_in_full, w_out_full, conv_full, small_w)
    loss_local, dy, grads = _local_step(xb, tgt, layers, length)
    grad_x = dy[None]
    loss = lax.psum(loss_local, ("x", "y", "c"))

    gi = jnp.stack([_unperm_cols(grads[l]["w_in"]).reshape(D_MODEL, N_DEV, SHARD_IN).transpose(1, 0, 2)
                    for l in range(DEPTH)], axis=1).reshape(N_DEV, DEPTH * D_MODEL, SHARD_IN)
    go = jnp.stack([grads[l]["w_out"].reshape(N_DEV, SHARD_OUT, D_MODEL) for l in range(DEPTH)],
                   axis=1).reshape(N_DEV, DEPTH * SHARD_OUT, D_MODEL)
    gc = jnp.stack([jnp.stack([grads[l]["conv_w"][:, k * CONV_CH:(k + 1) * CONV_CH]
                               .reshape(CONV_W, N_DEV, SHARD_CONV).transpose(1, 0, 2) for l in range(DEPTH)], axis=1)
                    for k in range(2)], axis=1).reshape(N_DEV, 2 * DEPTH * CONV_W, SHARD_CONV)
    small_g = dict(
        pre_norm=jnp.concatenate([grads[l]["pre_norm"] for l in range(DEPTH)], axis=0),
        post_norm=jnp.concatenate([grads[l]["post_norm"] for l in range(DEPTH)], axis=0),
        gdn_A_log=jnp.stack([grads[l]["gdn_A_log"][:, 0, 0] for l in range(DEPTH)]),
        gdn_dt_bias=jnp.stack([grads[l]["gdn_dt_bias"][:, 0, 0] for l in range(DEPTH)]),
        gdn_norm=jnp.concatenate([grads[l]["gdn_norm"] for l in range(DEPTH)], axis=0),
        ssd_conv_b=jnp.concatenate([grads[l]["conv_b"][:, CONV_CH:] for l in range(DEPTH)], axis=0),
        ssd_A_log=jnp.stack([grads[l]["ssd_A_log"].reshape(SSD_HEADS) for l in range(DEPTH)]),
        ssd_dt_bias=jnp.stack([grads[l]["ssd_dt_bias"].reshape(SSD_HEADS) for l in range(DEPTH)]),
        ssd_D=jnp.stack([grads[l]["ssd_D"].reshape(SSD_HEADS) for l in range(DEPTH)]),
        ssd_norm=jnp.concatenate([jnp.concatenate(grads[l]["ssd_norm"], axis=1) for l in range(DEPTH)], axis=0),
        ret_norm=jnp.concatenate([grads[l]["ret_norm"] for l in range(DEPTH)], axis=0))
    gs = _pack_small(small_g)
    gs8 = jnp.broadcast_to(gs[None], (N_DEV,) + gs.shape)
    r_in, r_out, r_conv, r_small = _exchange("exchange_grads", [gi, go, gc, gs8])

    flat = lambda a, cw: a.reshape(-1, cw)
    o_in = _adam("adam_w_in", r_in, flat(w_in, SHARD_IN), flat(m_w_in, SHARD_IN), flat(v_w_in, SHARD_IN), 128)
    o_out = _adam("adam_w_out", r_out, flat(w_out, D_MODEL), flat(m_w_out, D_MODEL), flat(v_w_out, D_MODEL), 128)
    cstk = lambda a, b: flat(jnp.stack([a, b], axis=0), SHARD_CONV)
    o_conv = _adam("adam_conv", r_conv, cstk(gdn_conv, ssd_conv), cstk(m_gdn_conv, m_ssd_conv),
                   cstk(v_gdn_conv, v_ssd_conv), 2 * DEPTH * CONV_W)
    ps_w, ps_m, ps_v = _pack_small(small_w), _pack_small(small_m), _pack_small(small_v)
    o_small = _adam("adam_small", r_small, ps_w, ps_m, ps_v, ps_w.shape[0])

    names = ["pre_norm", "post_norm", "w_in", "gdn_conv", "gdn_A_log", "gdn_dt_bias", "gdn_norm", "ssd_conv",
             "ssd_conv_b", "ssd_A_log", "ssd_dt_bias", "ssd_D", "ssd_norm", "ret_norm", "w_out"]
    outs = []
    for kind in range(4):
        sm = _unpack_small(o_small[kind], small_w)
        cv = o_conv[kind].reshape(2, DEPTH, CONV_W, SHARD_CONV)
        d = dict(sm)
        d["w_in"] = o_in[kind].reshape(w_in.shape)
        d["w_out"] = o_out[kind].reshape(w_out.shape)
        d["gdn_conv"] = cv[0]
        d["ssd_conv"] = cv[1]
        outs.extend(d[n] for n in names)
    return (loss, grad_x, *outs)
```

```python
import functools
import math

import numpy as np
import jax
import jax.numpy as jnp
from jax import lax
from jax.experimental import pallas as pl
from jax.experimental.pallas import tpu as pltpu

F32 = jnp.float32
BF16 = jnp.bfloat16

D_MODEL = 1024
DEPTH = 2
CHUNK = 64
CONV_W = 4
EPS = 1e-6
N_DEV = 8

GDN_HEADS = 4
GDN_DK = 128
SSD_HEADS = 16
SSD_P = 64
SSD_N = 128
SSD_GROUPS = 2
SSD_PAIRS = SSD_HEADS // 2
PAIRS_PER_GROUP = SSD_PAIRS // SSD_GROUPS
RET_HEADS = 4
RET_DK = 128
ROPE_BASE = 10000.0
MIX_W = 2048
N_IN = 6680
SHARD_IN = N_IN // N_DEV
SHARD_OUT = MIX_W // N_DEV
CONV_CH = 1536
SHARD_CONV = CONV_CH // N_DEV

ADAM_LR = 0.001
ADAM_B1 = 0.9
ADAM_B2 = 0.999
ADAM_EPS = 1e-08
ADAM_WD = 0.01
ADAM_STEP = 10

LANE = 128
SUBLANE = 8
VMEM_LIMIT = 56 * 1024 * 1024

_ORIG = dict(gq=(0, 512), gk=(512, 512), gv=(1024, 512), gz=(1536, 512), gb=(2048, 4), ga=(2052, 4),
             sx=(2056, 1024), sB=(3080, 256), sC=(3336, 256), sz=(3592, 1024), sdt=(4616, 16),
             rq=(4632, 512), rk=(5144, 512), rv=(5656, 512), rg=(6168, 512))
_ORDER = ["gq", "gk", "gv", "sx", "sB", "sC", "gz", "sz", "rq", "rk", "rv", "rg", "gb", "ga", "sdt"]
_MINE = {}
_off = 0
for _n in _ORDER:
    _MINE[_n] = _off
    _off += _ORIG[_n][1]
N_USED = _off
N_PAD = 7168
CONV_ALL = 2 * CONV_CH
SMALL_OFF = _MINE["gb"]
CB = lambda name: _MINE[name] // LANE


def _cparams(sem, vmem=None):
    return pltpu.CompilerParams(dimension_semantics=sem, vmem_limit_bytes=vmem)


def _split_bf16(a):
    hi = a.astype(BF16)
    return hi, (a - hi.astype(F32)).astype(BF16)


def _make_mm(split):
    def raw(a, b, ca, cb):
        dot = lambda x, y: lax.dot_general(x, y, (((ca,), (cb,)), ((), ())), preferred_element_type=F32)
        if split:
            ah, al = _split_bf16(a)
            bh, bl = _split_bf16(b)
            return dot(ah, bh) + (dot(ah, bl) + dot(al, bh))
        return dot(a.astype(BF16), b.astype(BF16))

    @jax.custom_vjp
    def nn(a, b):
        return raw(a, b, 1, 0)

    @jax.custom_vjp
    def nt(a, b):
        return raw(a, b, 1, 1)

    @jax.custom_vjp
    def tn(a, b):
        return raw(a, b, 0, 0)

    nn.defvjp(lambda a, b: (raw(a, b, 1, 0), (a, b)), lambda r, g: (nt(g, r[1]), tn(r[0], g)))
    nt.defvjp(lambda a, b: (raw(a, b, 1, 1), (a, b)), lambda r, g: (nn(g, r[1]), tn(g, r[0])))
    tn.defvjp(lambda a, b: (raw(a, b, 0, 0), (a, b)), lambda r, g: (nt(r[1], g), nn(r[0], g)))
    return nn, nt, tn


_nn, _nt, _tn = _make_mm(False)
_nnH, _ntH, _tnH = _make_mm(True)


@jax.custom_vjp
def _swap_halves(t):
    return pltpu.roll(t, LANE // 2, 1)


_swap_halves.defvjp(lambda t: (pltpu.roll(t, LANE // 2, 1), None),
                    lambda _, g: (pltpu.roll(g, LANE // 2, 1),))


@jax.custom_vjp
def _split_rows(x):
    return tuple(x[i * CHUNK:(i + 1) * CHUNK] for i in range(x.shape[0] // CHUNK))


_split_rows.defvjp(lambda x: (tuple(x[i * CHUNK:(i + 1) * CHUNK] for i in range(x.shape[0] // CHUNK)), None),
                   lambda _, gs: (jnp.concatenate(gs, axis=0),))


@jax.custom_vjp
def _tri_inv(a):
    return _tri_inv_impl(a)


def _tri_inv_impl(a):
    ii = lax.broadcasted_iota(jnp.int32, a.shape, 0)
    jj = lax.broadcasted_iota(jnp.int32, a.shape, 1)
    n = -a
    t = jnp.where(ii == jj, 1.0, 0.0).astype(F32) + n
    p = n
    steps = int(math.log2(CHUNK)) - 1
    for _ in range(steps):
        p = _nnH(p, p)
        t = t + _nnH(t, p)
    return t


_tri_inv.defvjp(lambda a: (lambda t: (t, t))(_tri_inv_impl(a)),
                lambda t, g: (-_ntH(_tnH(t, g), t),))


def _silu(x):
    return x * jax.nn.sigmoid(x)


@jax.custom_vjp
def _softplus(x):
    return jnp.maximum(x, 0.0) + jnp.log1p(jnp.exp(-jnp.abs(x)))


_softplus.defvjp(lambda x: (jnp.maximum(x, 0.0) + jnp.log1p(jnp.exp(-jnp.abs(x))), x),
                 lambda x, g: (g * jax.nn.sigmoid(x),))


def _rms(x, w):
    return x * lax.rsqrt(jnp.mean(x * x, axis=-1, keepdims=True) + EPS) * w


def _chunk_masks(n):
    ii = lax.broadcasted_iota(jnp.int32, (n, n), 0)
    jj = lax.broadcasted_iota(jnp.int32, (n, n), 1)
    return ii >= jj, ii > jj, ii == jj, ii <= jj


def _cumsum_col(g, causal, eye, upper):
    g_row = jnp.sum(jnp.where(eye, g, 0.0), axis=0, keepdims=True)
    col = jnp.sum(jnp.where(causal, g_row, 0.0), axis=1, keepdims=True)
    row = jnp.sum(jnp.where(upper, g, 0.0), axis=0, keepdims=True)
    return col, row


def _gdn_chunk(q, k, v, gz, b_raw, a_raw, alog, dtb, nw, s):
    nh = len(q)
    c = q[0].shape[0]
    causal, _, eye, upper = _chunk_masks(c)
    stack = lambda xs: jnp.concatenate(xs, axis=0)
    qn, kn, kb, vb, kbeg, qeg, gcum, glast = [], [], [], [], [], [], [], []
    for h in range(nh):
        qn_h = q[h] * lax.rsqrt(jnp.sum(q[h] * q[h], axis=-1, keepdims=True) + EPS) * (GDN_DK ** -0.5)
        kn_h = k[h] * lax.rsqrt(jnp.sum(k[h] * k[h], axis=-1, keepdims=True) + EPS)
        beta = jax.nn.sigmoid(b_raw[h])
        g = -jnp.exp(alog[h]) * _softplus(a_raw[h] + dtb[h])
        gcum_h, _ = _cumsum_col(g, causal, eye, upper)
        eg = jnp.exp(gcum_h)
        qn.append(qn_h)
        kn.append(kn_h)
        kb.append(kn_h * beta)
        vb.append(v[h] * beta)
        kbeg.append(kn_h * beta * eg)
        qeg.append(qn_h * eg)
        gcum.append(gcum_h)
        glast.append(jnp.sum(g, axis=0, keepdims=True))
    n = nh * c
    ii = lax.broadcasted_iota(jnp.int32, (n, n), 0)
    jj = lax.broadcasted_iota(jnp.int32, (n, n), 1)
    sh = int(math.log2(c))
    same = lax.shift_right_logical(ii, sh) == lax.shift_right_logical(jj, sh)
    causal_bd = jnp.logical_and(same, ii >= jj)
    strict_bd = jnp.logical_and(same, ii > jj)
    gcum_all = stack(gcum)
    gcum_row = jnp.sum(jnp.where(ii == jj, gcum_all, 0.0), axis=0, keepdims=True)
    decay = jnp.where(causal_bd, jnp.exp(jnp.where(causal_bd, gcum_all - gcum_row, 0.0)), 0.0)
    kn_all = stack(kn)
    a_low = jnp.where(strict_bd, _nt(stack(kb), kn_all) * decay, 0.0)
    t = _tri_inv(a_low)
    u = _nn(t, stack(vb))
    w = _split_rows(_nn(t, stack(kbeg)))
    v_new_all = u - stack([_nn(w[h], s[h]) for h in range(nh)])
    attn = _nt(stack(qn), kn_all) * decay
    o = _split_rows(stack([_nn(qeg[h], s[h]) for h in range(nh)]) + _nn(attn, v_new_all))
    v_new = _split_rows(v_new_all)
    on, s_new = [], []
    for h in range(nh):
        kdec = kn[h] * jnp.exp(glast[h] - gcum[h])
        s_new.append(s[h] * jnp.exp(glast[h]) + _tn(kdec, v_new[h]))
        on.append(_rms(o[h], nw) * _silu(gz[h]))
    return on, s_new


def _ssd_chunk(x, bm, cm, dtr, alog, dtb, dp, hs):
    c = x.shape[0]
    causal, _, eye, upper = _chunk_masks(c)
    lane_lo = lax.broadcasted_iota(jnp.int32, (1, LANE), 1) < SSD_P
    row_lo = lax.broadcasted_iota(jnp.int32, (LANE, 1), 0) < SSD_P
    sel0 = lax.broadcasted_iota(jnp.int32, (1, 2), 1) == 0
    dt = _softplus(dtr + dtb)
    a = dt * -jnp.exp(alog)
    cb = _nt(cm, bm)
    heads = []
    for h in range(2):
        sel = sel0 if h == 0 else jnp.logical_not(sel0)
        dt_h = jnp.sum(jnp.where(sel, dt, 0.0), axis=1, keepdims=True)
        a_h = jnp.sum(jnp.where(sel, a, 0.0), axis=1, keepdims=True)
        d_h = jnp.sum(jnp.where(sel, dp, 0.0), axis=1, keepdims=True)
        acum, acum_row = _cumsum_col(a_h, causal, eye, upper)
        lmat = jnp.where(causal, jnp.exp(jnp.where(causal, acum - acum_row, 0.0)), 0.0)
        alast = jnp.sum(a_h, axis=0, keepdims=True)
        heads.append((dt_h, d_h, acum, lmat, alast))
    pick = lambda i: jnp.where(lane_lo, heads[0][i], heads[1][i])
    xdt = x * pick(0)
    y = x * pick(1)
    for h in range(2):
        hm = lane_lo if h == 0 else jnp.logical_not(lane_lo)
        y = y + _nn(cb * heads[h][3], jnp.where(hm, xdt, 0.0))
    y = y + _nt(cm, hs) * jnp.exp(pick(2))
    wdec = jnp.exp(jnp.where(lane_lo, heads[0][4] - heads[0][2], heads[1][4] - heads[1][2]))
    scale = jnp.exp(jnp.where(row_lo, heads[0][4], heads[1][4]))
    hs_new = hs * scale + _tn(xdt * wdec, bm)
    return y, hs_new


def _ret_chunk(rq, rk, rv, rg, cos2, sin2, dmat, qdec, kdec, cdec, nw, r):
    q = rq * cos2 + _swap_halves(rq) * sin2
    k = (rk * cos2 + _swap_halves(rk) * sin2) * (RET_DK ** -0.5)
    s = _nt(q, k) * dmat
    o = _nn(s, rv) + _nn(q, r) * qdec
    r_new = r * cdec + _tn(k * kdec, rv)
    on = _rms(o, nw) * _silu(rg)
    return on, r_new


MM_TILE = 1024


def _tile(n, pref=MM_TILE):
    return pref if n % pref == 0 else n


def _matmul(name, a, b, mode, m, n, k, tk=None):
    tm, tn = _tile(m), _tile(n)
    tk = _tile(k) if tk is None else tk
    nk = k // tk
    ca, cb = {"nn": (1, 0), "nt": (1, 1), "tn": (0, 0)}[mode]

    def body(a_ref, b_ref, o_ref):
        part = lax.dot_general(a_ref[...].astype(BF16), b_ref[...].astype(BF16),
                               (((ca,), (cb,)), ((), ())), preferred_element_type=F32)
        if nk == 1:
            o_ref[...] = part
        else:
            kk = pl.program_id(2)

            @pl.when(kk == 0)
            def _():
                o_ref[...] = part

            @pl.when(kk > 0)
            def _():
                o_ref[...] += part

    a_spec = pl.BlockSpec((tk, tm), lambda i, j, kk: (kk, i)) if mode == "tn" else pl.BlockSpec((tm, tk), lambda i, j, kk: (i, kk))
    b_spec = pl.BlockSpec((tn, tk), lambda i, j, kk: (j, kk)) if mode == "nt" else pl.BlockSpec((tk, tn), lambda i, j, kk: (kk, j))
    return pl.pallas_call(
        body, name=name, grid=(m // tm, n // tn, nk),
        in_specs=[a_spec, b_spec], out_specs=pl.BlockSpec((tm, tn), lambda i, j, kk: (i, j)),
        out_shape=jax.ShapeDtypeStruct((m, n), F32),
        compiler_params=_cparams(("parallel", "parallel", "arbitrary"), VMEM_LIMIT),
    )(a, b)


def _rowmap(name, fn, length, tm, rows, params, out_ws=(), acc_ws=(), out_dtypes=None, cts=None, nd_rows=(),
            d_dtypes=None):
    nt_ = length // tm
    n_r, n_nd, n_p = len(rows), len(nd_rows), len(params)
    rspec = lambda bw, cbk: pl.BlockSpec((tm, bw), lambda i: (i, cbk))
    pspec = lambda w: pl.BlockSpec((1, w), lambda i: (0, 0))
    in_arrays = [r[0] for r in rows] + [r[0] for r in nd_rows] + list(params)
    in_specs = [rspec(r[1], r[2]) for r in rows] + [rspec(r[1], r[2]) for r in nd_rows] + [pspec(p.shape[1]) for p in params]

    def load(refs):
        return [r[...].astype(F32) for r in refs]

    if cts is None:
        n_o, n_a = len(out_ws), len(acc_ws)
        out_dtypes_ = out_dtypes or [F32] * n_o

        def body(*refs):
            ins = load(refs[:n_r + n_nd + n_p])
            outs, accs = fn(*ins)
            o_refs = refs[n_r + n_nd + n_p:]
            for o_ref, o in zip(o_refs[:n_o], outs):
                o_ref[...] = o.astype(o_ref.dtype)
            first = pl.program_id(0) == 0
            for a_ref, acc in zip(o_refs[n_o:], accs):
                @pl.when(first)
                def _(a_ref=a_ref):
                    a_ref[...] = jnp.zeros_like(a_ref)
                a_ref[...] += acc

        res = pl.pallas_call(
            body, name=name, grid=(nt_,), in_specs=in_specs,
            out_specs=[rspec(w, 0) for w in out_ws] + [pspec(w) for w in acc_ws],
            out_shape=[jax.ShapeDtypeStruct((length, w), dt) for w, dt in zip(out_ws, out_dtypes_)]
                      + [jax.ShapeDtypeStruct((1, w), F32) for w in acc_ws],
            compiler_params=_cparams(("arbitrary",), VMEM_LIMIT),
        )(*in_arrays)
        return res[:n_o], res[n_o:]

    n_c = len(cts)

    def body(*refs):
        ins = load(refs[:n_r + n_nd + n_p])
        ct_vals = load(refs[n_r + n_nd + n_p:n_r + n_nd + n_p + n_c])
        nd = ins[n_r:n_r + n_nd]
        f = lambda rs, ps: fn(*rs, *nd, *ps)[0]
        _, vjp = jax.vjp(f, ins[:n_r], ins[n_r + n_nd:])
        d_rows, d_params = vjp(tuple(ct_vals))
        o_refs = refs[n_r + n_nd + n_p + n_c:]
        for o_ref, d in zip(o_refs[:n_r], d_rows):
            o_ref[...] = d.astype(o_ref.dtype)
        first = pl.program_id(0) == 0
        for a_ref, d in zip(o_refs[n_r:], d_params):
            @pl.when(first)
            def _(a_ref=a_ref):
                a_ref[...] = jnp.zeros_like(a_ref)
            a_ref[...] += d

    res = pl.pallas_call(
        body, name=name, grid=(nt_,),
        in_specs=in_specs + [rspec(c[1], c[2]) for c in cts],
        out_specs=[rspec(r[1], 0) for r in rows] + [pspec(p.shape[1]) for p in params],
        out_shape=[jax.ShapeDtypeStruct((length, r[1]), dt) for r, dt in zip(rows, d_dtypes or [F32] * n_r)]
                  + [jax.ShapeDtypeStruct((1, p.shape[1]), F32) for p in params],
        compiler_params=_cparams(("arbitrary",), VMEM_LIMIT),
    )(*in_arrays, *[c[0] for c in cts])
    return res[:n_r], res[n_r:]


def _conv_shift(ext, k, tm, forward):
    s = CONV_W - 1 - k
    if forward:
        rolled = ext if s == 0 else pltpu.roll(ext, s, 0)
        return rolled[SUBLANE:, :]
    rolled = ext if s == 0 else pltpu.roll(ext, tm + SUBLANE - s, 0)
    return rolled[:tm, :]


def _conv_pre(x_ref, halo_ref, w_ref, b_ref, first, tm):
    halo = jnp.where(first, 0.0, halo_ref[...])
    ext = jnp.concatenate([halo, x_ref[...]], axis=0)
    w = w_ref[...]
    pre = b_ref[...] + jnp.zeros_like(x_ref[...])
    taps = []
    for k in range(CONV_W):
        tap = _conv_shift(ext, k, tm, True)
        taps.append(tap)
        pre = pre + tap * w[k:k + 1, :]
    return pre, taps


def _conv_fwd(proj, w, b, length, tm, tc):
    hb = tm // SUBLANE

    def body(x_ref, halo_ref, w_ref, b_ref, o_ref):
        pre, _ = _conv_pre(x_ref, halo_ref, w_ref, b_ref, pl.program_id(0) == 0, tm)
        o_ref[...] = _silu(pre)

    return pl.pallas_call(
        body, name="conv_fwd", grid=(length // tm, CONV_ALL // tc),
        in_specs=[pl.BlockSpec((tm, tc), lambda i, j: (i, j)),
                  pl.BlockSpec((SUBLANE, tc), lambda i, j: (jnp.maximum(i * hb - 1, 0), j)),
                  pl.BlockSpec((CONV_W, tc), lambda i, j: (0, j)),
                  pl.BlockSpec((1, tc), lambda i, j: (0, j))],
        out_specs=pl.BlockSpec((tm, tc), lambda i, j: (i, j)),
        out_shape=jax.ShapeDtypeStruct((length, CONV_ALL), F32),
        compiler_params=_cparams(("parallel", "parallel"), VMEM_LIMIT),
    )(proj, proj, w, b)


def _conv_bwd_pre(proj, w, b, dxc, length, tm, tc):
    hb = tm // SUBLANE

    def body(x_ref, halo_ref, w_ref, b_ref, dy_ref, dpre_ref, dw_ref, db_ref):
        i = pl.program_id(1)
        pre, taps = _conv_pre(x_ref, halo_ref, w_ref, b_ref, i == 0, tm)
        sg = jax.nn.sigmoid(pre)
        dpre = dy_ref[...] * (sg * (1.0 + pre * (1.0 - sg)))
        dpre_ref[...] = dpre

        @pl.when(i == 0)
        def _():
            dw_ref[...] = jnp.zeros_like(dw_ref)
            db_ref[...] = jnp.zeros_like(db_ref)

        for k in range(CONV_W):
            dw_ref[k:k + 1, :] += jnp.sum(dpre * taps[k], axis=0, keepdims=True)
        db_ref[...] += jnp.sum(dpre, axis=0, keepdims=True)

    return pl.pallas_call(
        body, name="conv_bwd_pre", grid=(CONV_ALL // tc, length // tm),
        in_specs=[pl.BlockSpec((tm, tc), lambda j, i: (i, j)),
                  pl.BlockSpec((SUBLANE, tc), lambda j, i: (jnp.maximum(i * hb - 1, 0), j)),
                  pl.BlockSpec((CONV_W, tc), lambda j, i: (0, j)),
                  pl.BlockSpec((1, tc), lambda j, i: (0, j)),
                  pl.BlockSpec((tm, tc), lambda j, i: (i, j))],
        out_specs=[pl.BlockSpec((tm, tc), lambda j, i: (i, j)),
                   pl.BlockSpec((CONV_W, tc), lambda j, i: (0, j)),
                   pl.BlockSpec((1, tc), lambda j, i: (0, j))],
        out_shape=[jax.ShapeDtypeStruct((length, CONV_ALL), F32),
                   jax.ShapeDtypeStruct((CONV_W, CONV_ALL), F32),
                   jax.ShapeDtypeStruct((1, CONV_ALL), F32)],
        compiler_params=_cparams(("parallel", "arbitrary"), VMEM_LIMIT),
    )(proj, proj, w, b, dxc)


def _conv_bwd_x(dpre, w, length, tm, tc):
    hb = tm // SUBLANE
    n_t = length // tm
    last_blk = length // SUBLANE - 1

    def body(d_ref, halo_ref, w_ref, o_ref):
        halo = jnp.where(pl.program_id(0) == n_t - 1, 0.0, halo_ref[...])
        ext = jnp.concatenate([d_ref[...], halo], axis=0)
        w = w_ref[...]
        acc = jnp.zeros_like(d_ref[...])
        for k in range(CONV_W):
            acc = acc + _conv_shift(ext, k, tm, False) * w[k:k + 1, :]
        o_ref[...] = acc.astype(o_ref.dtype)

    return pl.pallas_call(
        body, name="conv_bwd_x", grid=(n_t, CONV_ALL // tc),
        in_specs=[pl.BlockSpec((tm, tc), lambda i, j: (i, j)),
                  pl.BlockSpec((SUBLANE, tc), lambda i, j: (jnp.minimum((i + 1) * hb, last_blk), j)),
                  pl.BlockSpec((CONV_W, tc), lambda i, j: (0, j))],
        out_specs=pl.BlockSpec((tm, tc), lambda i, j: (i, j)),
        out_shape=jax.ShapeDtypeStruct((length, CONV_ALL), BF16),
        compiler_params=_cparams(("parallel", "parallel"), VMEM_LIMIT),
    )(dpre, dpre, w)


def _gdn_fwd(xc, proj, b_col, a_col, alog, dtb, nw, length):
    nc = length // CHUNK
    h_ = GDN_HEADS

    hw = h_ * LANE

    def body(q_ref, k_ref, v_ref, gz_ref, b_ref, a_ref, al_ref, dt_ref, nw_ref, on_ref, st_ref, s_ref):
        @pl.when(pl.program_id(0) == 0)
        def _():
            s_ref[...] = jnp.zeros_like(s_ref)

        st_ref[...] = s_ref[...]
        heads = lambda ref: [ref[:, h * LANE:(h + 1) * LANE] for h in range(h_)]
        lead = lambda ref: [ref[h] for h in range(h_)]
        on, s_new = _gdn_chunk(heads(q_ref), heads(k_ref), heads(v_ref), heads(gz_ref), lead(b_ref), lead(a_ref),
                               lead(al_ref), lead(dt_ref), nw_ref[...], lead(s_ref))
        for h in range(h_):
            on_ref[:, h * LANE:(h + 1) * LANE] = on[h].astype(on_ref.dtype)
            s_ref[h] = s_new[h]

    blk = lambda name: pl.BlockSpec((CHUNK, hw), lambda c: (c, _MINE[name] // hw))
    col = pl.BlockSpec((h_, CHUNK, 1), lambda c: (0, c, 0))
    one = pl.BlockSpec((h_, 1, 1), lambda c: (0, 0, 0))
    return pl.pallas_call(
        body, name="gdn_fwd", grid=(nc,),
        in_specs=[blk("gq"), blk("gk"), blk("gv"), blk("gz"), col, col, one, one,
                  pl.BlockSpec((1, LANE), lambda c: (0, 0))],
        out_specs=[pl.BlockSpec((CHUNK, hw), lambda c: (c, 0)),
                   pl.BlockSpec((h_, None, LANE, LANE), lambda c: (0, c, 0, 0))],
        out_shape=[jax.ShapeDtypeStruct((length, hw), BF16),
                   jax.ShapeDtypeStruct((h_, nc, LANE, LANE), F32)],
        scratch_shapes=[pltpu.VMEM((h_, LANE, LANE), F32)],
        compiler_params=_cparams(("arbitrary",), VMEM_LIMIT),
    )(xc, xc, xc, proj, b_col, a_col, alog, dtb, nw)


def _gdn_bwd(xc, proj, b_col, a_col, alog, dtb, nw, st, dmix, length):
    nc = length // CHUNK
    h_ = GDN_HEADS

    def body(q_ref, k_ref, v_ref, gz_ref, b_ref, a_ref, al_ref, dt_ref, nw_ref, st_ref, do_ref,
             dq_ref, dk_ref, dv_ref, dgz_ref, db_ref, da_ref, dal_ref, ddt_ref, dnw_ref, ds_ref):
        @pl.when(pl.program_id(0) == 0)
        def _():
            ds_ref[...] = jnp.zeros_like(ds_ref)
            dal_ref[...] = jnp.zeros_like(dal_ref)
            ddt_ref[...] = jnp.zeros_like(ddt_ref)
            dnw_ref[...] = jnp.zeros_like(dnw_ref)

        heads = lambda ref: [ref[:, h * LANE:(h + 1) * LANE] for h in range(h_)]
        lead = lambda ref: [ref[h] for h in range(h_)]
        ins = (heads(q_ref), heads(k_ref), heads(v_ref), heads(gz_ref), lead(b_ref), lead(a_ref), lead(al_ref),
               lead(dt_ref), nw_ref[...], lead(st_ref))
        _, vjp = jax.vjp(_gdn_chunk, *ins)
        dq, dk, dv, dgz, db, da, dal, ddt, dnw, ds = vjp((heads(do_ref), lead(ds_ref)))
        for h in range(h_):
            cols = slice(h * LANE, (h + 1) * LANE)
            dq_ref[:, cols] = dq[h]
            dk_ref[:, cols] = dk[h]
            dv_ref[:, cols] = dv[h]
            dgz_ref[:, cols] = dgz[h].astype(dgz_ref.dtype)
            db_ref[h] = db[h]
            da_ref[h] = da[h]
            dal_ref[h] += dal[h]
            ddt_ref[h] += ddt[h]
            ds_ref[h] = ds[h]
        dnw_ref[...] += dnw

    hw = h_ * LANE
    r = lambda c: nc - 1 - c
    blk = lambda name: pl.BlockSpec((CHUNK, hw), lambda c: (r(c), _MINE[name] // hw))
    col = pl.BlockSpec((h_, CHUNK, 1), lambda c: (0, r(c), 0))
    one = pl.BlockSpec((h_, 1, 1), lambda c: (0, 0, 0))
    oblk = pl.BlockSpec((CHUNK, hw), lambda c: (r(c), 0))
    big = jax.ShapeDtypeStruct((length, hw), F32)
    return pl.pallas_call(
        body, name="gdn_bwd", grid=(nc,),
        in_specs=[blk("gq"), blk("gk"), blk("gv"), blk("gz"), col, col, one, one,
                  pl.BlockSpec((1, LANE), lambda c: (0, 0)),
                  pl.BlockSpec((h_, None, LANE, LANE), lambda c: (0, r(c), 0, 0)),
                  oblk],
        out_specs=[oblk, oblk, oblk, oblk, col, col, one, one, pl.BlockSpec((1, LANE), lambda c: (0, 0))],
        out_shape=[big, big, big, jax.ShapeDtypeStruct((length, hw), BF16),
                   jax.ShapeDtypeStruct((h_, length, 1), F32), jax.ShapeDtypeStruct((h_, length, 1), F32),
                   jax.ShapeDtypeStruct((h_, 1, 1), F32), jax.ShapeDtypeStruct((h_, 1, 1), F32),
                   jax.ShapeDtypeStruct((1, LANE), F32)],
        scratch_shapes=[pltpu.VMEM((h_, LANE, LANE), F32)],
        compiler_params=_cparams(("arbitrary",), VMEM_LIMIT),
    )(xc, xc, xc, proj, b_col, a_col, alog, dtb, nw, st, dmix)


def _ssd_fwd(xc, dt_col, alog, dtb, dpar, length):
    nc = length // CHUNK
    ppg = PAIRS_PER_GROUP

    gw = ppg * LANE

    def body(x_ref, b_ref, c_ref, dt_ref, al_ref, db_ref, dp_ref, y_ref, st_ref, hs_ref):
        @pl.when(pl.program_id(1) == 0)
        def _():
            hs_ref[...] = jnp.zeros_like(hs_ref)

        bm, cm = b_ref[...], c_ref[...]
        for p in range(ppg):
            st_ref[p] = hs_ref[p]
            y, hs_new = _ssd_chunk(x_ref[:, p * LANE:(p + 1) * LANE], bm, cm, dt_ref[p], al_ref[p], db_ref[p],
                                   dp_ref[p], hs_ref[p])
            y_ref[:, p * LANE:(p + 1) * LANE] = y
            hs_ref[p] = hs_new

    two = pl.BlockSpec((None, ppg, 1, 2), lambda g, c: (g, 0, 0, 0))
    return pl.pallas_call(
        body, name="ssd_fwd", grid=(SSD_GROUPS, nc),
        in_specs=[pl.BlockSpec((CHUNK, gw), lambda g, c: (c, _MINE["sx"] // gw + g)),
                  pl.BlockSpec((CHUNK, LANE), lambda g, c: (c, CB("sB") + g)),
                  pl.BlockSpec((CHUNK, LANE), lambda g, c: (c, CB("sC") + g)),
                  pl.BlockSpec((None, ppg, CHUNK, 2), lambda g, c: (g, 0, c, 0)),
                  two, two, two],
        out_specs=[pl.BlockSpec((CHUNK, gw), lambda g, c: (c, g)),
                   pl.BlockSpec((None, ppg, None, LANE, SSD_N), lambda g, c: (g, 0, c, 0, 0))],
        out_shape=[jax.ShapeDtypeStruct((length, SSD_GROUPS * gw), F32),
                   jax.ShapeDtypeStruct((SSD_GROUPS, ppg, nc, LANE, SSD_N), F32)],
        scratch_shapes=[pltpu.VMEM((ppg, LANE, SSD_N), F32)],
        compiler_params=_cparams(("parallel", "arbitrary"), VMEM_LIMIT),
    )(xc, xc, xc, dt_col, alog, dtb, dpar)


def _ssd_bwd(xc, dt_col, alog, dtb, dpar, st, dy, length):
    nc = length // CHUNK
    ppg = PAIRS_PER_GROUP

    gw = ppg * LANE

    def body(x_ref, b_ref, c_ref, dt_ref, al_ref, db_ref, dp_ref, st_ref, dy_ref,
             dx_ref, dbm_ref, dcm_ref, ddt_ref, dal_ref, ddb_ref, ddp_ref, dhs_ref):
        @pl.when(pl.program_id(1) == 0)
        def _():
            dhs_ref[...] = jnp.zeros_like(dhs_ref)
            dal_ref[...] = jnp.zeros_like(dal_ref)
            ddb_ref[...] = jnp.zeros_like(ddb_ref)
            ddp_ref[...] = jnp.zeros_like(ddp_ref)

        bm, cm = b_ref[...], c_ref[...]
        dbm_tot = jnp.zeros_like(bm)
        dcm_tot = jnp.zeros_like(cm)
        for p in range(ppg):
            cols = slice(p * LANE, (p + 1) * LANE)
            ins = (x_ref[:, cols], bm, cm, dt_ref[p], al_ref[p], db_ref[p], dp_ref[p], st_ref[p])
            _, vjp = jax.vjp(_ssd_chunk, *ins)
            dx, dbm, dcm, ddt, dal, ddb, ddp, dhs = vjp((dy_ref[:, cols], dhs_ref[p]))
            dx_ref[:, cols] = dx
            ddt_ref[p] = ddt
            dbm_tot = dbm_tot + dbm
            dcm_tot = dcm_tot + dcm
            dhs_ref[p] = dhs
            dal_ref[p] += dal
            ddb_ref[p] += ddb
            ddp_ref[p] += ddp
        dbm_ref[...] = dbm_tot
        dcm_ref[...] = dcm_tot

    r = lambda c: nc - 1 - c
    two = pl.BlockSpec((None, ppg, 1, 2), lambda g, c: (g, 0, 0, 0))
    dtspec = pl.BlockSpec((None, ppg, CHUNK, 2), lambda g, c: (g, 0, r(c), 0))
    grp = pl.BlockSpec((CHUNK, LANE), lambda g, c: (r(c), g))
    prm = jax.ShapeDtypeStruct((SSD_GROUPS, ppg, 1, 2), F32)
    return pl.pallas_call(
        body, name="ssd_bwd", grid=(SSD_GROUPS, nc),
        in_specs=[pl.BlockSpec((CHUNK, gw), lambda g, c: (r(c), _MINE["sx"] // gw + g)),
                  pl.BlockSpec((CHUNK, LANE), lambda g, c: (r(c), CB("sB") + g)),
                  pl.BlockSpec((CHUNK, LANE), lambda g, c: (r(c), CB("sC") + g)),
                  dtspec, two, two, two,
                  pl.BlockSpec((None, ppg, None, LANE, SSD_N), lambda g, c: (g, 0, r(c), 0, 0)),
                  pl.BlockSpec((CHUNK, gw), lambda g, c: (r(c), g))],
        out_specs=[pl.BlockSpec((CHUNK, gw), lambda g, c: (r(c), g)), grp, grp, dtspec, two, two, two],
        out_shape=[jax.ShapeDtypeStruct((length, SSD_GROUPS * gw), F32),
                   jax.ShapeDtypeStruct((length, SSD_GROUPS * SSD_N), F32),
                   jax.ShapeDtypeStruct((length, SSD_GROUPS * SSD_N), F32),
                   jax.ShapeDtypeStruct((SSD_GROUPS, ppg, length, 2), F32), prm, prm, prm],
        scratch_shapes=[pltpu.VMEM((ppg, LANE, SSD_N), F32)],
        compiler_params=_cparams(("parallel", "arbitrary"), VMEM_LIMIT),
    )(xc, xc, xc, dt_col, alog, dtb, dpar, st, dy)


def _ret_fwd(proj, cos2, sin2, dmat, qdec, kdec, cdec, nw, length):
    nc = length // CHUNK
    h_ = RET_HEADS

    def body(q_ref, k_ref, v_ref, g_ref, cos_ref, sin_ref, dm_ref, qd_ref, kd_ref, cd_ref, nw_ref,
             on_ref, st_ref, r_ref):
        @pl.when(pl.program_id(0) == 0)
        def _():
            r_ref[...] = jnp.zeros_like(r_ref)

        cos2_, sin2_, nw_ = cos_ref[...], sin_ref[...], nw_ref[...]
        res = []
        for h in range(h_):
            cols = slice(h * LANE, (h + 1) * LANE)
            r_old = r_ref[h]
            st_ref[h] = r_old
            res.append(_ret_chunk(q_ref[:, cols], k_ref[:, cols], v_ref[:, cols], g_ref[:, cols], cos2_, sin2_,
                                  dm_ref[h], qd_ref[h], kd_ref[h], cd_ref[h], nw_, r_old))
        for h in range(h_):
            on_ref[:, h * LANE:(h + 1) * LANE] = res[h][0].astype(on_ref.dtype)
            r_ref[h] = res[h][1]

    hw = h_ * LANE
    blk = lambda name: pl.BlockSpec((CHUNK, hw), lambda c: (c, _MINE[name] // hw))
    tab = pl.BlockSpec((CHUNK, LANE), lambda c: (c, 0))
    full = lambda *s: pl.BlockSpec(s, lambda c: (0,) * len(s))
    return pl.pallas_call(
        body, name="ret_fwd", grid=(nc,),
        in_specs=[blk("rq"), blk("rk"), blk("rv"), blk("rg"), tab, tab,
                  full(h_, CHUNK, CHUNK), full(h_, CHUNK, 1), full(h_, CHUNK, 1), full(h_, 1, 1), full(1, LANE)],
        out_specs=[pl.BlockSpec((CHUNK, hw), lambda c: (c, 0)),
                   pl.BlockSpec((h_, None, LANE, LANE), lambda c: (0, c, 0, 0))],
        out_shape=[jax.ShapeDtypeStruct((length, hw), BF16),
                   jax.ShapeDtypeStruct((h_, nc, LANE, LANE), F32)],
        scratch_shapes=[pltpu.VMEM((h_, LANE, LANE), F32)],
        compiler_params=_cparams(("arbitrary",), VMEM_LIMIT),
    )(proj, proj, proj, proj, cos2, sin2, dmat, qdec, kdec, cdec, nw)


def _ret_bwd(proj, cos2, sin2, dmat, qdec, kdec, cdec, nw, st, dmix, length):
    nc = length // CHUNK
    h_ = RET_HEADS

    def body(q_ref, k_ref, v_ref, g_ref, cos_ref, sin_ref, dm_ref, qd_ref, kd_ref, cd_ref, nw_ref, st_ref,
             do_ref, dq_ref, dk_ref, dv_ref, dg_ref, dnw_ref, dr_ref):
        @pl.when(pl.program_id(0) == 0)
        def _():
            dr_ref[...] = jnp.zeros_like(dr_ref)
            dnw_ref[...] = jnp.zeros_like(dnw_ref)

        cos2_, sin2_, nw_ = cos_ref[...], sin_ref[...], nw_ref[...]
        res = []
        for h in range(h_):
            cols = slice(h * LANE, (h + 1) * LANE)
            consts = (cos2_, sin2_, dm_ref[h], qd_ref[h], kd_ref[h], cd_ref[h])
            f = lambda q, k, v, g, w_, r_, consts=consts: _ret_chunk(q, k, v, g, *consts, w_, r_)
            _, vjp = jax.vjp(f, q_ref[:, cols], k_ref[:, cols], v_ref[:, cols], g_ref[:, cols], nw_, st_ref[h])
            res.append(vjp((do_ref[:, cols], dr_ref[h])))
        dnw_tot = dnw_ref[...]
        for h in range(h_):
            cols = slice(h * LANE, (h + 1) * LANE)
            dq, dk, dv, dg, dnw, dr = res[h]
            dq_ref[:, cols] = dq.astype(dq_ref.dtype)
            dk_ref[:, cols] = dk.astype(dk_ref.dtype)
            dv_ref[:, cols] = dv.astype(dv_ref.dtype)
            dg_ref[:, cols] = dg.astype(dg_ref.dtype)
            dnw_tot = dnw_tot + dnw
            dr_ref[h] = dr
        dnw_ref[...] = dnw_tot

    hw = h_ * LANE
    r = lambda c: nc - 1 - c
    blk = lambda name: pl.BlockSpec((CHUNK, hw), lambda c: (r(c), _MINE[name] // hw))
    tab = pl.BlockSpec((CHUNK, LANE), lambda c: (r(c), 0))
    full = lambda *s: pl.BlockSpec(s, lambda c: (0,) * len(s))
    oblk = pl.BlockSpec((CHUNK, hw), lambda c: (r(c), 0))
    big = jax.ShapeDtypeStruct((length, hw), BF16)
    return pl.pallas_call(
        body, name="ret_bwd", grid=(nc,),
        in_specs=[blk("rq"), blk("rk"), blk("rv"), blk("rg"), tab, tab,
                  full(h_, CHUNK, CHUNK), full(h_, CHUNK, 1), full(h_, CHUNK, 1), full(h_, 1, 1), full(1, LANE),
                  pl.BlockSpec((h_, None, LANE, LANE), lambda c: (0, r(c), 0, 0)),
                  pl.BlockSpec((CHUNK, hw), lambda c: (r(c), (MIX_W - hw) // hw))],
        out_specs=[oblk, oblk, oblk, oblk, full(1, LANE)],
        out_shape=[big, big, big, big, jax.ShapeDtypeStruct((1, LANE), F32)],
        scratch_shapes=[pltpu.VMEM((h_, LANE, LANE), F32)],
        compiler_params=_cparams(("arbitrary",), VMEM_LIMIT),
    )(proj, proj, proj, proj, cos2, sin2, dmat, qdec, kdec, cdec, nw, st, dmix)


def _mesh_pos():
    x, y, c = lax.axis_index("x"), lax.axis_index("y"), lax.axis_index("c")
    return x, y, c, 4 * x + 2 * y + c


def _peer(x, y, c, mask):
    return (x ^ ((mask >> 2) & 1), y ^ ((mask >> 1) & 1), c ^ (mask & 1))


def _all_gather(name, shards):
    n = len(shards)

    def body(*refs):
        ins, outs = refs[:n], refs[n:2 * n]
        send_sems, recv_sems, local_sems = refs[2 * n:]
        x, y, c, me = _mesh_pos()
        local = [pltpu.make_async_copy(ins[a], outs[a].at[me], local_sems.at[a]) for a in range(n)]
        for cp in local:
            cp.start()
        copies = []
        for mask in range(1, N_DEV):
            for a in range(n):
                k = (mask - 1) * n + a
                copies.append(pltpu.make_async_remote_copy(
                    src_ref=ins[a], dst_ref=outs[a].at[me], send_sem=send_sems.at[k], recv_sem=recv_sems.at[k],
                    device_id=_peer(x, y, c, mask), device_id_type=pl.DeviceIdType.MESH))
        for cp in copies:
            cp.start()
        for cp in copies:
            cp.wait_send()
        for mask in range(1, N_DEV):
            px, py, pc = _peer(x, y, c, mask)
            for a in range(n):
                k = (mask - 1) * n + a
                pltpu.make_async_remote_copy(
                    src_ref=ins[a], dst_ref=outs[a].at[4 * px + 2 * py + pc], send_sem=send_sems.at[k],
                    recv_sem=recv_sems.at[k], device_id=(px, py, pc), device_id_type=pl.DeviceIdType.MESH).wait_recv()
        for cp in local:
            cp.wait()

    hbm = pl.BlockSpec(memory_space=pl.ANY)
    k_tot = (N_DEV - 1) * n
    return pl.pallas_call(
        body, name=name, in_specs=[hbm] * n, out_specs=[hbm] * n,
        out_shape=[jax.ShapeDtypeStruct((N_DEV,) + s.shape, s.dtype) for s in shards],
        scratch_shapes=[pltpu.SemaphoreType.DMA((k_tot,)), pltpu.SemaphoreType.DMA((k_tot,)),
                        pltpu.SemaphoreType.DMA((n,))],
        compiler_params=pltpu.CompilerParams(has_side_effects=True),
    )(*shards)


def _exchange(name, slabs):
    n = len(slabs)

    def body(*refs):
        ins, outs = refs[:n], refs[n:2 * n]
        send_sems, recv_sems, local_sems = refs[2 * n:]
        x, y, c, me = _mesh_pos()
        local = [pltpu.make_async_copy(ins[a].at[me], outs[a].at[me], local_sems.at[a]) for a in range(n)]
        for cp in local:
            cp.start()
        copies = []
        for mask in range(1, N_DEV):
            px, py, pc = _peer(x, y, c, mask)
            for a in range(n):
                k = (mask - 1) * n + a
                copies.append(pltpu.make_async_remote_copy(
                    src_ref=ins[a].at[4 * px + 2 * py + pc], dst_ref=outs[a].at[me], send_sem=send_sems.at[k],
                    recv_sem=recv_sems.at[k], device_id=(px, py, pc), device_id_type=pl.DeviceIdType.MESH))
        for cp in copies:
            cp.start()
        for cp in copies:
            cp.wait_send()
        for mask in range(1, N_DEV):
            px, py, pc = _peer(x, y, c, mask)
            for a in range(n):
                k = (mask - 1) * n + a
                pltpu.make_async_remote_copy(
                    src_ref=ins[a].at[me], dst_ref=outs[a].at[4 * px + 2 * py + pc], send_sem=send_sems.at[k],
                    recv_sem=recv_sems.at[k], device_id=(px, py, pc), device_id_type=pl.DeviceIdType.MESH).wait_recv()
        for cp in local:
            cp.wait()

    hbm = pl.BlockSpec(memory_space=pl.ANY)
    k_tot = (N_DEV - 1) * n
    return pl.pallas_call(
        body, name=name, in_specs=[hbm] * n, out_specs=[hbm] * n,
        out_shape=[jax.ShapeDtypeStruct(s.shape, s.dtype) for s in slabs],
        scratch_shapes=[pltpu.SemaphoreType.DMA((k_tot,)), pltpu.SemaphoreType.DMA((k_tot,)),
                        pltpu.SemaphoreType.DMA((n,))],
        compiler_params=pltpu.CompilerParams(has_side_effects=True),
    )(*slabs)


def _adam(name, parts, w, m, v, tr):
    p_, r_, cw = parts.shape
    c1 = 1.0 - ADAM_B1 ** ADAM_STEP
    c2 = 1.0 - ADAM_B2 ** ADAM_STEP

    def body(p_ref, w_ref, m_ref, v_ref, g_ref, d_ref, nm_ref, nv_ref):
        g = p_ref[0]
        for i in range(1, p_):
            g = g + p_ref[i]
        nm = ADAM_B1 * m_ref[...] + (1.0 - ADAM_B1) * g
        nv = ADAM_B2 * v_ref[...] + (1.0 - ADAM_B2) * (g * g)
        d_ref[...] = -ADAM_LR * ((nm / c1) / (jnp.sqrt(nv / c2) + ADAM_EPS) + ADAM_WD * w_ref[...])
        g_ref[...] = g
        nm_ref[...] = nm
        nv_ref[...] = nv

    spec = pl.BlockSpec((tr, cw), lambda i: (i, 0))
    shp = jax.ShapeDtypeStruct((r_, cw), F32)
    return pl.pallas_call(
        body, name=name, grid=(r_ // tr,),
        in_specs=[pl.BlockSpec((p_, tr, cw), lambda i: (0, i, 0)), spec, spec, spec],
        out_specs=[spec] * 4, out_shape=[shp] * 4,
        compiler_params=_cparams(("parallel",), VMEM_LIMIT),
    )(parts, w, m, v)


def _perm_cols(w_full):
    parts = [w_full[:, _ORIG[n][0]:_ORIG[n][0] + _ORIG[n][1]] for n in _ORDER]
    parts.append(jnp.zeros((w_full.shape[0], N_PAD - N_USED), w_full.dtype))
    return jnp.concatenate(parts, axis=1)


def _unperm_cols(g):
    by_orig = sorted(_ORDER, key=lambda n: _ORIG[n][0])
    return jnp.concatenate([g[:, _MINE[n]:_MINE[n] + _ORIG[n][1]] for n in by_orig], axis=1)


def _ret_consts(length):
    lg = jnp.log(1.0 - 2.0 ** (-5.0 - jnp.arange(RET_HEADS, dtype=F32)))
    idx = jnp.arange(CHUNK, dtype=F32)
    rel = idx[:, None] - idx[None, :]
    dmat = jnp.where(rel[None] >= 0, jnp.exp(jnp.maximum(rel, 0.0)[None] * lg[:, None, None]), 0.0)
    qdec = jnp.exp((idx[:, None] + 1.0) * lg[None, :]).T[:, :, None]
    kdec = jnp.exp((CHUNK - 1.0 - idx)[:, None] * lg[None, :]).T[:, :, None]
    cdec = jnp.exp(CHUNK * lg)[:, None, None]
    half = RET_DK // 2
    inv = ROPE_BASE ** (-jnp.arange(half, dtype=F32) / half)
    ang = jnp.arange(length, dtype=jnp.int32).astype(F32)[:, None] * inv[None, :]
    cos, sin = jnp.cos(ang), jnp.sin(ang)
    return (jnp.concatenate([cos, cos], axis=1), jnp.concatenate([-sin, sin], axis=1), dmat, qdec, kdec, cdec)


def _tiles(length):
    tm = 512 if length % 512 == 0 else length
    return tm


def _layer_fwd(x, p, consts, length, last, target):
    tm = _tiles(length)
    (h,), _ = _rowmap("pre_norm", lambda x_, w_: ((_rms(x_, w_),), ()), length, tm,
                      [(x, D_MODEL, 0)], [p["pre_norm"]], out_ws=[D_MODEL], out_dtypes=[BF16])
    proj = _matmul("in_proj", h, p["w_in"], "nn", length, N_PAD, D_MODEL)
    xc = _conv_fwd(proj, p["conv_w"], p["conv_b"], length, tm, 512)
    small = proj[:, SMALL_OFF:SMALL_OFF + LANE]
    b_col = small[:, 0:4].T[:, :, None]
    a_col = small[:, 4:8].T[:, :, None]
    dt_col = small[:, 8:24].reshape(length, SSD_GROUPS, PAIRS_PER_GROUP, 2).transpose(1, 2, 0, 3)
    o_a, st_a = _gdn_fwd(xc, proj, b_col, a_col, p["gdn_A_log"], p["gdn_dt_bias"], p["gdn_norm"], length)
    y_b, st_b = _ssd_fwd(xc, dt_col, p["ssd_A_log"], p["ssd_dt_bias"], p["ssd_D"], length)
    o_b = []
    for g in range(SSD_GROUPS):
        (o,), _ = _rowmap(f"ssd_gate{g}", lambda y_, z_, w_: ((_rms(y_ * _silu(z_), w_),), ()), length, tm,
                          [(y_b, 512, g), (proj, 512, _MINE["sz"] // 512 + g)], [p["ssd_norm"][g]], out_ws=[512],
                          out_dtypes=[BF16])
        o_b.append(o)
    o_c, st_c = _ret_fwd(proj, *consts, p["ret_norm"], length)
    mixed = jnp.concatenate([o.astype(BF16) for o in [o_a] + o_b + [o_c]], axis=1)
    out = _matmul("out_proj", mixed, p["w_out"], "nn", length, D_MODEL, MIX_W)
    res = dict(x=x, h=h, proj=proj, xc=xc, b_col=b_col, a_col=a_col, dt_col=dt_col, st_a=st_a, st_b=st_b,
               st_c=st_c, y_b=y_b, mixed=mixed, out=out)
    if not last:
        (y,), _ = _rowmap("post_norm", lambda x_, o_, w_: ((x_ + _rms(o_, w_),), ()), length, tm,
                          [(x, D_MODEL, 0), (out, D_MODEL, 0)], [p["post_norm"]], out_ws=[D_MODEL])
        return y, res, None

    def head(x_, o_, t_, w_):
        e = x_ + _rms(o_, w_) - t_
        row = jnp.mean(e * e, axis=-1, keepdims=True)
        loss = 0.5 * jnp.sum(row, axis=0, keepdims=True)
        return (e * (1.0 / D_MODEL),), (loss + jnp.zeros((1, LANE), F32),)

    (dy,), (loss,) = _rowmap("loss_head", head, length, tm,
                             [(x, D_MODEL, 0), (out, D_MODEL, 0), (target, D_MODEL, 0)], [p["post_norm"]],
                             out_ws=[D_MODEL], acc_ws=[LANE])
    return None, res, (dy, loss[0, 0])


def _layer_bwd(dy, p, res, consts, length):
    tm = _tiles(length)
    g = {}
    (d_out,), (g["post_norm"],) = _rowmap("post_norm_bwd", lambda o_, w_: ((_rms(o_, w_),), ()), length, tm,
                                          [(res["out"], D_MODEL, 0)], [p["post_norm"]], cts=[(dy, D_MODEL, 0)])
    dmix = _matmul("out_proj_dx", d_out, p["w_out"], "nt", length, MIX_W, D_MODEL)
    g["w_out"] = _matmul("out_proj_dw", res["mixed"], d_out, "tn", MIX_W, D_MODEL, length)
    proj, xc = res["proj"], res["xc"]
    (dgq, dgk, dgv, dgz, db, da, g["gdn_A_log"], g["gdn_dt_bias"], dnw_a) = _gdn_bwd(
        xc, proj, res["b_col"], res["a_col"], p["gdn_A_log"], p["gdn_dt_bias"], p["gdn_norm"], res["st_a"], dmix, length)
    g["gdn_norm"] = dnw_a
    dyb, dsz, dsn = [], [], []
    for gi in range(SSD_GROUPS):
        (dy_g, dz_g), (dn_g,) = _rowmap(
            f"ssd_gate_bwd{gi}", lambda y_, z_, w_: ((_rms(y_ * _silu(z_), w_),), ()), length, tm,
            [(res["y_b"], 512, gi), (proj, 512, _MINE["sz"] // 512 + gi)], [p["ssd_norm"][gi]],
            cts=[(dmix, 512, 1 + gi)], d_dtypes=[F32, BF16])
        dyb.append(dy_g)
        dsz.append(dz_g)
        dsn.append(dn_g)
    g["ssd_norm"] = dsn
    dyb = jnp.concatenate(dyb, axis=1)
    dsx, dsb, dsc, ddt, g["ssd_A_log"], g["ssd_dt_bias"], g["ssd_D"] = _ssd_bwd(
        xc, res["dt_col"], p["ssd_A_log"], p["ssd_dt_bias"], p["ssd_D"], res["st_b"], dyb, length)
    drq, drk, drv, drg, dnw_c = _ret_bwd(proj, *consts, p["ret_norm"], res["st_c"], dmix, length)
    g["ret_norm"] = dnw_c
    dxc = jnp.concatenate([dgq, dgk, dgv, dsx, dsb, dsc], axis=1)
    dpre, g["conv_w"], g["conv_b"] = _conv_bwd_pre(proj, p["conv_w"], p["conv_b"], dxc, length, tm, 512)
    dconv = _conv_bwd_x(dpre, p["conv_w"], length, tm, 512)
    dsmall = jnp.concatenate([db[:, :, 0].T, da[:, :, 0].T, ddt.transpose(2, 0, 1, 3).reshape(length, SSD_HEADS)], axis=1)
    dproj = jnp.concatenate([d.astype(BF16) for d in [dconv, dgz] + dsz + [drq, drk, drv, drg, dsmall]]
                            + [jnp.zeros((length, N_PAD - N_USED), BF16)], axis=1)
    dh = _matmul("in_proj_dx", dproj, p["w_in"], "nt", length, D_MODEL, N_PAD)
    g["w_in"] = _matmul("in_proj_dw", res["h"], dproj, "tn", D_MODEL, N_PAD, length)
    (dx,), (g["pre_norm"],) = _rowmap(
        "pre_norm_bwd", lambda x_, w_: ((_rms(x_, w_), x_), ()), length, tm,
        [(res["x"], D_MODEL, 0)], [p["pre_norm"]], cts=[(dh, D_MODEL, 0), (dy, D_MODEL, 0)])
    return dx, g


def _make_layers(w_in_full, w_out_full, conv_full, sw):
    layers = []
    for l in range(DEPTH):
        layers.append(dict(
            pre_norm=sw["pre_norm"][l][None], post_norm=sw["post_norm"][l][None], w_in=_perm_cols(w_in_full[l]),
            w_out=w_out_full[l],
            conv_w=jnp.concatenate([conv_full[0, l], conv_full[1, l]], axis=1),
            conv_b=jnp.concatenate([jnp.zeros((CONV_CH,), F32), sw["ssd_conv_b"][l]])[None],
            gdn_A_log=sw["gdn_A_log"][l][:, None, None], gdn_dt_bias=sw["gdn_dt_bias"][l][:, None, None],
            gdn_norm=sw["gdn_norm"][l][None],
            ssd_A_log=sw["ssd_A_log"][l].reshape(SSD_GROUPS, PAIRS_PER_GROUP, 1, 2),
            ssd_dt_bias=sw["ssd_dt_bias"][l].reshape(SSD_GROUPS, PAIRS_PER_GROUP, 1, 2),
            ssd_D=sw["ssd_D"][l].reshape(SSD_GROUPS, PAIRS_PER_GROUP, 1, 2),
            ssd_norm=[sw["ssd_norm"][l][None, 512 * g:512 * (g + 1)] for g in range(SSD_GROUPS)],
            ret_norm=sw["ret_norm"][l][None]))
    return layers


def _local_step(xb, tgt, layers, length):
    consts = _ret_consts(length)
    cur = xb
    saved = []
    extras = None
    for l in range(DEPTH):
        cur, res, extras = _layer_fwd(cur, layers[l], consts, length, l == DEPTH - 1, tgt)
        saved.append(res)
    dy, loss_local = extras
    grads = [None] * DEPTH
    for l in reversed(range(DEPTH)):
        dy, grads[l] = _layer_bwd(dy, layers[l], saved[l], consts, length)
    return loss_local, dy, grads


_SMALL = ["pre_norm", "post_norm", "gdn_A_log", "gdn_dt_bias", "gdn_norm", "ssd_conv_b", "ssd_A_log",
          "ssd_dt_bias", "ssd_D", "ssd_norm", "ret_norm"]


def _pack_small(arrs):
    rows = []
    for n in _SMALL:
        flat = arrs[n].reshape(-1)
        pad = (-flat.shape[0]) % LANE
        rows.append(jnp.pad(flat, (0, pad)).reshape(-1, LANE))
    out = jnp.concatenate(rows, axis=0)
    return jnp.pad(out, ((0, (-out.shape[0]) % SUBLANE), (0, 0)))


def _unpack_small(packed, like):
    out, r = {}, 0
    for n in _SMALL:
        cnt = like[n].size
        nrow = -(-cnt // LANE)
        out[n] = packed[r:r + nrow].reshape(-1)[:cnt].reshape(like[n].shape)
        r += nrow
    return out


def kernel(x, pre_norm, post_norm, w_in, gdn_conv, gdn_A_log, gdn_dt_bias, gdn_norm, ssd_conv, ssd_conv_b, ssd_A_log, ssd_dt_bias, ssd_D, ssd_norm, ret_norm, w_out, loss_target, m_pre_norm, m_post_norm, m_w_in, m_gdn_conv, m_gdn_A_log, m_gdn_dt_bias, m_gdn_norm, m_ssd_conv, m_ssd_conv_b, m_ssd_A_log, m_ssd_dt_bias, m_ssd_D, m_ssd_norm, m_ret_norm, m_w_out, v_pre_norm, v_post_norm, v_w_in, v_gdn_conv, v_gdn_A_log, v_gdn_dt_bias, v_gdn_norm, v_ssd_conv, v_ssd_conv_b, v_ssd_A_log, v_ssd_dt_bias, v_ssd_D, v_ssd_norm, v_ret_norm, v_w_out):
    length = x.shape[1]
    xb = x[0]
    tgt = loss_target[0]
    small_w = dict(pre_norm=pre_norm, post_norm=post_norm, gdn_A_log=gdn_A_log, gdn_dt_bias=gdn_dt_bias,
                   gdn_norm=gdn_norm, ssd_conv_b=ssd_conv_b, ssd_A_log=ssd_A_log, ssd_dt_bias=ssd_dt_bias,
                   ssd_D=ssd_D, ssd_norm=ssd_norm, ret_norm=ret_norm)
    small_m = dict(pre_norm=m_pre_norm, post_norm=m_post_norm, gdn_A_log=m_gdn_A_log, gdn_dt_bias=m_gdn_dt_bias,
                   gdn_norm=m_gdn_norm, ssd_conv_b=m_ssd_conv_b, ssd_A_log=m_ssd_A_log, ssd_dt_bias=m_ssd_dt_bias,
                   ssd_D=m_ssd_D, ssd_norm=m_ssd_norm, ret_norm=m_ret_norm)
    small_v = dict(pre_norm=v_pre_norm, post_norm=v_post_norm, gdn_A_log=v_gdn_A_log, gdn_dt_bias=v_gdn_dt_bias,
                   gdn_norm=v_gdn_norm, ssd_conv_b=v_ssd_conv_b, ssd_A_log=v_ssd_A_log, ssd_dt_bias=v_ssd_dt_bias,
                   ssd_D=v_ssd_D, ssd_norm=v_ssd_norm, ret_norm=v_ret_norm)

    conv_shard = jnp.stack([gdn_conv, ssd_conv], axis=0)
    g_in, g_out, g_conv = _all_gather("gather_weights", [w_in.astype(BF16), w_out.astype(BF16), conv_shard])
    w_in_full = g_in.transpose(1, 2, 0, 3).reshape(DEPTH, D_MODEL, N_IN)
    w_out_full = g_out.transpose(1, 0, 2, 3).reshape(DEPTH, MIX_W, D_MODEL)
    conv_full = g_conv.transpose(1, 2, 3, 0, 4).reshape(2, DEPTH, CONV_W, CONV_CH)

    layers = _make_layers(w_in_full, w_out_full, conv_full, small_w)
    loss_local, dy, grads = _local_step(xb, tgt, layers, length)
    grad_x = dy[None]
    loss = lax.psum(loss_local, ("x", "y", "c"))

    gi = jnp.stack([_unperm_cols(grads[l]["w_in"]).reshape(D_MODEL, N_DEV, SHARD_IN).transpose(1, 0, 2)
                    for l in range(DEPTH)], axis=1).reshape(N_DEV, DEPTH * D_MODEL, SHARD_IN)
    go = jnp.stack([grads[l]["w_out"].reshape(N_DEV, SHARD_OUT, D_MODEL) for l in range(DEPTH)],
                   axis=1).reshape(N_DEV, DEPTH * SHARD_OUT, D_MODEL)
    gc = jnp.stack([jnp.stack([grads[l]["conv_w"][:, k * CONV_CH:(k + 1) * CONV_CH]
                               .reshape(CONV_W, N_DEV, SHARD_CONV).transpose(1, 0, 2) for l in range(DEPTH)], axis=1)
                    for k in range(2)], axis=1).reshape(N_DEV, 2 * DEPTH * CONV_W, SHARD_CONV)
    small_g = dict(
        pre_norm=jnp.concatenate([grads[l]["pre_norm"] for l in range(DEPTH)], axis=0),
        post_norm=jnp.concatenate([grads[l]["post_norm"] for l in range(DEPTH)], axis=0),
        gdn_A_log=jnp.stack([grads[l]["gdn_A_log"][:, 0, 0] for l in range(DEPTH)]),
        gdn_dt_bias=jnp.stack([grads[l]["gdn_dt_bias"][:, 0, 0] for l in range(DEPTH)]),
        gdn_norm=jnp.concatenate([grads[l]["gdn_norm"] for l in range(DEPTH)], axis=0),
        ssd_conv_b=jnp.concatenate([grads[l]["conv_b"][:, CONV_CH:] for l in range(DEPTH)], axis=0),
        ssd_A_log=jnp.stack([grads[l]["ssd_A_log"].reshape(SSD_HEADS) for l in range(DEPTH)]),
        ssd_dt_bias=jnp.stack([grads[l]["ssd_dt_bias"].reshape(SSD_HEADS) for l in range(DEPTH)]),
        ssd_D=jnp.stack([grads[l]["ssd_D"].reshape(SSD_HEADS) for l in range(DEPTH)]),
        ssd_norm=jnp.concatenate([jnp.concatenate(grads[l]["ssd_norm"], axis=1) for l in range(DEPTH)], axis=0),
        ret_norm=jnp.concatenate([grads[l]["ret_norm"] for l in range(DEPTH)], axis=0))
    gs = _pack_small(small_g)
    gs8 = jnp.broadcast_to(gs[None], (N_DEV,) + gs.shape)
    r_in, r_out, r_conv, r_small = _exchange("exchange_grads", [gi, go, gc, gs8])

    flat = lambda a, cw: a.reshape(-1, cw)
    o_in = _adam("adam_w_in", r_in, flat(w_in, SHARD_IN), flat(m_w_in, SHARD_IN), flat(v_w_in, SHARD_IN), 128)
    o_out = _adam("adam_w_out", r_out, flat(w_out, D_MODEL), flat(m_w_out, D_MODEL), flat(v_w_out, D_MODEL), 128)
    cstk = lambda a, b: flat(jnp.stack([a, b], axis=0), SHARD_CONV)
    o_conv = _adam("adam_conv", r_conv, cstk(gdn_conv, ssd_conv), cstk(m_gdn_conv, m_ssd_conv),
                   cstk(v_gdn_conv, v_ssd_conv), 2 * DEPTH * CONV_W)
    ps_w, ps_m, ps_v = _pack_small(small_w), _pack_small(small_m), _pack_small(small_v)
    o_small = _adam("adam_small", r_small, ps_w, ps_m, ps_v, ps_w.shape[0])

    names = ["pre_norm", "post_norm", "w_in", "gdn_conv", "gdn_A_log", "gdn_dt_bias", "gdn_norm", "ssd_conv",
             "ssd_conv_b", "ssd_A_log", "ssd_dt_bias", "ssd_D", "ssd_norm", "ret_norm", "w_out"]
    outs = []
    for kind in range(4):
        sm = _unpack_small(o_small[kind], small_w)
        cv = o_conv[kind].reshape(2, DEPTH, CONV_W, SHARD_CONV)
        d = dict(sm)
        d["w_in"] = o_in[kind].reshape(w_in.shape)
        d["w_out"] = o_out[kind].reshape(w_out.shape)
        d["gdn_conv"] = cv[0]
        d["ssd_conv"] = cv[1]
        outs.extend(d[n] for n in names)
    return (loss, grad_x, *outs)
```

```python
import functools
import math

import numpy as np
import jax
import jax.numpy as jnp
from jax import lax
from jax.experimental import pallas as pl
from jax.experimental.pallas import tpu as pltpu

F32 = jnp.float32
BF16 = jnp.bfloat16

D_MODEL = 1024
DEPTH = 2
CHUNK = 64
CONV_W = 4
EPS = 1e-6
N_DEV = 8

GDN_HEADS = 4
GDN_DK = 128
SSD_HEADS = 16
SSD_P = 64
SSD_N = 128
SSD_GROUPS = 2
SSD_PAIRS = SSD_HEADS // 2
PAIRS_PER_GROUP = SSD_PAIRS // SSD_GROUPS
RET_HEADS = 4
RET_DK = 128
ROPE_BASE = 10000.0
MIX_W = 2048
N_IN = 6680
SHARD_IN = N_IN // N_DEV
SHARD_OUT = MIX_W // N_DEV
CONV_CH = 1536
SHARD_CONV = CONV_CH // N_DEV

ADAM_LR = 0.001
ADAM_B1 = 0.9
ADAM_B2 = 0.999
ADAM_EPS = 1e-08
ADAM_WD = 0.01
ADAM_STEP = 10

LANE = 128
SUBLANE = 8
VMEM_LIMIT = 56 * 1024 * 1024

_ORIG = dict(gq=(0, 512), gk=(512, 512), gv=(1024, 512), gz=(1536, 512), gb=(2048, 4), ga=(2052, 4),
             sx=(2056, 1024), sB=(3080, 256), sC=(3336, 256), sz=(3592, 1024), sdt=(4616, 16),
             rq=(4632, 512), rk=(5144, 512), rv=(5656, 512), rg=(6168, 512))
_ORDER = ["gq", "gk", "gv", "sx", "sB", "sC", "gz", "sz", "rq", "rk", "rv", "rg", "gb", "ga", "sdt"]
_MINE = {}
_off = 0
for _n in _ORDER:
    _MINE[_n] = _off
    _off += _ORIG[_n][1]
N_USED = _off
N_PAD = 7168
CONV_ALL = 2 * CONV_CH
SMALL_OFF = _MINE["gb"]
CB = lambda name: _MINE[name] // LANE


def _cparams(sem, vmem=None):
    return pltpu.CompilerParams(dimension_semantics=sem, vmem_limit_bytes=vmem)


def _split_bf16(a):
    hi = a.astype(BF16)
    return hi, (a - hi.astype(F32)).astype(BF16)


def _make_mm(split):
    def raw(a, b, ca, cb):
        dot = lambda x, y: lax.dot_general(x, y, (((ca,), (cb,)), ((), ())), preferred_element_type=F32)
        if split:
            ah, al = _split_bf16(a)
            bh, bl = _split_bf16(b)
            return dot(ah, bh) + (dot(ah, bl) + dot(al, bh))
        return dot(a.astype(BF16), b.astype(BF16))

    @jax.custom_vjp
    def nn(a, b):
        return raw(a, b, 1, 0)

    @jax.custom_vjp
    def nt(a, b):
        return raw(a, b, 1, 1)

    @jax.custom_vjp
    def tn(a, b):
        return raw(a, b, 0, 0)

    nn.defvjp(lambda a, b: (raw(a, b, 1, 0), (a, b)), lambda r, g: (nt(g, r[1]), tn(r[0], g)))
    nt.defvjp(lambda a, b: (raw(a, b, 1, 1), (a, b)), lambda r, g: (nn(g, r[1]), tn(g, r[0])))
    tn.defvjp(lambda a, b: (raw(a, b, 0, 0), (a, b)), lambda r, g: (nt(r[1], g), nn(r[0], g)))
    return nn, nt, tn


_nn, _nt, _tn = _make_mm(False)
_nnH, _ntH, _tnH = _make_mm(True)


@jax.custom_vjp
def _swap_halves(t):
    return pltpu.roll(t, LANE // 2, 1)


_swap_halves.defvjp(lambda t: (pltpu.roll(t, LANE // 2, 1), None),
                    lambda _, g: (pltpu.roll(g, LANE // 2, 1),))


@jax.custom_vjp
def _split_rows(x):
    return tuple(x[i * CHUNK:(i + 1) * CHUNK] for i in range(x.shape[0] // CHUNK))


_split_rows.defvjp(lambda x: (tuple(x[i * CHUNK:(i + 1) * CHUNK] for i in range(x.shape[0] // CHUNK)), None),
                   lambda _, gs: (jnp.concatenate(gs, axis=0),))


@jax.custom_vjp
def _tri_inv(a):
    return _tri_inv_impl(a)


def _tri_inv_impl(a):
    ii = lax.broadcasted_iota(jnp.int32, a.shape, 0)
    jj = lax.broadcasted_iota(jnp.int32, a.shape, 1)
    n = -a
    t = jnp.where(ii == jj, 1.0, 0.0).astype(F32) + n
    p = n
    steps = int(math.log2(CHUNK)) - 1
    for _ in range(steps):
        p = _nnH(p, p)
        t = t + _nnH(t, p)
    return t


_tri_inv.defvjp(lambda a: (lambda t: (t, t))(_tri_inv_impl(a)),
                lambda t, g: (-_ntH(_tnH(t, g), t),))


def _silu(x):
    return x * jax.nn.sigmoid(x)


@jax.custom_vjp
def _softplus(x):
    return jnp.maximum(x, 0.0) + jnp.log1p(jnp.exp(-jnp.abs(x)))


_softplus.defvjp(lambda x: (jnp.maximum(x, 0.0) + jnp.log1p(jnp.exp(-jnp.abs(x))), x),
                 lambda x, g: (g * jax.nn.sigmoid(x),))


def _rms(x, w):
    return x * lax.rsqrt(jnp.mean(x * x, axis=-1, keepdims=True) + EPS) * w


def _chunk_masks(n):
    ii = lax.broadcasted_iota(jnp.int32, (n, n), 0)
    jj = lax.broadcasted_iota(jnp.int32, (n, n), 1)
    return ii >= jj, ii > jj, ii == jj, ii <= jj


def _cumsum_col(g, causal, eye, upper):
    g_row = jnp.sum(jnp.where(eye, g, 0.0), axis=0, keepdims=True)
    col = jnp.sum(jnp.where(causal, g_row, 0.0), axis=1, keepdims=True)
    row = jnp.sum(jnp.where(upper, g, 0.0), axis=0, keepdims=True)
    return col, row


def _gdn_chunk(q, k, v, gz, b_raw, a_raw, alog, dtb, nw, s):
    nh = len(q)
    c = q[0].shape[0]
    causal, _, eye, upper = _chunk_masks(c)
    stack = lambda xs: jnp.concatenate(xs, axis=0)
    qn, kn, kb, vb, kbeg, qeg, gcum, glast = [], [], [], [], [], [], [], []
    for h in range(nh):
        qn_h = q[h] * lax.rsqrt(jnp.sum(q[h] * q[h], axis=-1, keepdims=True) + EPS) * (GDN_DK ** -0.5)
        kn_h = k[h] * lax.rsqrt(jnp.sum(k[h] * k[h], axis=-1, keepdims=True) + EPS)
        beta = jax.nn.sigmoid(b_raw[h])
        g = -jnp.exp(alog[h]) * _softplus(a_raw[h] + dtb[h])
        gcum_h, _ = _cumsum_col(g, causal, eye, upper)
        eg = jnp.exp(gcum_h)
        qn.append(qn_h)
        kn.append(kn_h)
        kb.append(kn_h * beta)
        vb.append(v[h] * beta)
        kbeg.append(kn_h * beta * eg)
        qeg.append(qn_h * eg)
        gcum.append(gcum_h)
        glast.append(jnp.sum(g, axis=0, keepdims=True))
    n = nh * c
    ii = lax.broadcasted_iota(jnp.int32, (n, n), 0)
    jj = lax.broadcasted_iota(jnp.int32, (n, n), 1)
    sh = int(math.log2(c))
    same = lax.shift_right_logical(ii, sh) == lax.shift_right_logical(jj, sh)
    causal_bd = jnp.logical_and(same, ii >= jj)
    strict_bd = jnp.logical_and(same, ii > jj)
    gcum_all = stack(gcum)
    gcum_row = jnp.sum(jnp.where(ii == jj, gcum_all, 0.0), axis=0, keepdims=True)
    decay = jnp.where(causal_bd, jnp.exp(jnp.where(causal_bd, gcum_all - gcum_row, 0.0)), 0.0)
    kn_all = stack(kn)
    a_low = jnp.where(strict_bd, _nt(stack(kb), kn_all) * decay, 0.0)
    t = _tri_inv(a_low)
    u = _nn(t, stack(vb))
    w = _split_rows(_nn(t, stack(kbeg)))
    v_new_all = u - stack([_nn(w[h], s[h]) for h in range(nh)])
    attn = _nt(stack(qn), kn_all) * decay
    o = _split_rows(stack([_nn(qeg[h], s[h]) for h in range(nh)]) + _nn(attn, v_new_all))
    v_new = _split_rows(v_new_all)
    on, s_new = [], []
    for h in range(nh):
        kdec = kn[h] * jnp.exp(glast[h] - gcum[h])
        s_new.append(s[h] * jnp.exp(glast[h]) + _tn(kdec, v_new[h]))
        on.append(_rms(o[h], nw) * _silu(gz[h]))
    return on, s_new


def _ssd_chunk(x, bm, cm, dtr, alog, dtb, dp, hs):
    c = x.shape[0]
    lane_i = lax.broadcasted_iota(jnp.int32, (c, LANE), 1)
    lane_lo = lane_i < SSD_P
    causal2 = lax.broadcasted_iota(jnp.int32, (c, LANE), 0) >= jnp.bitwise_and(lane_i, c - 1)
    row_lo = lax.broadcasted_iota(jnp.int32, (LANE, 1), 0) < SSD_P
    eye2 = lax.broadcasted_iota(jnp.int32, (LANE, LANE), 0) == lax.broadcasted_iota(jnp.int32, (LANE, LANE), 1)
    to_row = lambda col2: jnp.sum(jnp.where(eye2, col2, 0.0), axis=0, keepdims=True)
    dt = [_softplus(dtr[h] + dtb[h]) for h in range(2)]
    a = [dt[h] * -jnp.exp(alog[h]) for h in range(2)]
    a_row = to_row(jnp.concatenate(a, axis=0))
    part = jnp.where(causal2, a_row, 0.0)
    acum = [jnp.sum(jnp.where(lane_lo, part, 0.0), axis=1, keepdims=True),
            jnp.sum(jnp.where(lane_lo, 0.0, part), axis=1, keepdims=True)]
    acum_row = to_row(jnp.concatenate(acum, axis=0))
    acum_col = jnp.where(lane_lo, acum[0], acum[1])
    lmat = jnp.where(causal2, jnp.exp(jnp.where(causal2, acum_col - acum_row, 0.0)), 0.0)
    alast = [jnp.sum(a[h], axis=0, keepdims=True) for h in range(2)]
    cb = _nt(cm, jnp.concatenate([bm, bm], axis=0))
    xdt = x * jnp.where(lane_lo, dt[0], dt[1])
    xdt_rows = jnp.concatenate([jnp.where(lane_lo, xdt, 0.0), jnp.where(lane_lo, 0.0, xdt)], axis=0)
    d_lane = jnp.where(lax.broadcasted_iota(jnp.int32, (1, LANE), 1) < SSD_P, dp[0], dp[1])
    y = x * d_lane + _nn(cb * lmat, xdt_rows) + _nt(cm, hs) * jnp.exp(acum_col)
    wdec = jnp.exp(jnp.where(lane_lo, alast[0] - acum[0], alast[1] - acum[1]))
    scale = jnp.exp(jnp.where(row_lo, alast[0], alast[1]))
    hs_new = hs * scale + _tn(xdt * wdec, bm)
    return y, hs_new


def _ret_chunk(rq, rk, rv, rg, cos2, sin2, dmat, qdec, kdec, cdec, nw, r):
    q = rq * cos2 + _swap_halves(rq) * sin2
    k = (rk * cos2 + _swap_halves(rk) * sin2) * (RET_DK ** -0.5)
    s = _nt(q, k) * dmat
    o = _nn(s, rv) + _nn(q, r) * qdec
    r_new = r * cdec + _tn(k * kdec, rv)
    on = _rms(o, nw) * _silu(rg)
    return on, r_new


MM_TILE = 1024


def _tile(n, pref=MM_TILE):
    return pref if n % pref == 0 else n


def _matmul(name, a, b, mode, m, n, k, tk=None):
    tm, tn = _tile(m), _tile(n)
    tk = _tile(k) if tk is None else tk
    nk = k // tk
    ca, cb = {"nn": (1, 0), "nt": (1, 1), "tn": (0, 0)}[mode]

    def body(a_ref, b_ref, o_ref):
        part = lax.dot_general(a_ref[...].astype(BF16), b_ref[...].astype(BF16),
                               (((ca,), (cb,)), ((), ())), preferred_element_type=F32)
        if nk == 1:
            o_ref[...] = part
        else:
            kk = pl.program_id(2)

            @pl.when(kk == 0)
            def _():
                o_ref[...] = part

            @pl.when(kk > 0)
            def _():
                o_ref[...] += part

    a_spec = pl.BlockSpec((tk, tm), lambda i, j, kk: (kk, i)) if mode == "tn" else pl.BlockSpec((tm, tk), lambda i, j, kk: (i, kk))
    b_spec = pl.BlockSpec((tn, tk), lambda i, j, kk: (j, kk)) if mode == "nt" else pl.BlockSpec((tk, tn), lambda i, j, kk: (kk, j))
    return pl.pallas_call(
        body, name=name, grid=(m // tm, n // tn, nk),
        in_specs=[a_spec, b_spec], out_specs=pl.BlockSpec((tm, tn), lambda i, j, kk: (i, j)),
        out_shape=jax.ShapeDtypeStruct((m, n), F32),
        compiler_params=_cparams(("parallel", "parallel", "arbitrary"), VMEM_LIMIT),
    )(a, b)


def _rowmap(name, fn, length, tm, rows, params, out_ws=(), acc_ws=(), out_dtypes=None, cts=None, nd_rows=(),
            d_dtypes=None):
    nt_ = length // tm
    n_r, n_nd, n_p = len(rows), len(nd_rows), len(params)
    rspec = lambda bw, cbk: pl.BlockSpec((tm, bw), lambda i: (i, cbk))
    pspec = lambda w: pl.BlockSpec((1, w), lambda i: (0, 0))
    in_arrays = [r[0] for r in rows] + [r[0] for r in nd_rows] + list(params)
    in_specs = [rspec(r[1], r[2]) for r in rows] + [rspec(r[1], r[2]) for r in nd_rows] + [pspec(p.shape[1]) for p in params]

    def load(refs):
        return [r[...].astype(F32) for r in refs]

    if cts is None:
        n_o, n_a = len(out_ws), len(acc_ws)
        out_dtypes_ = out_dtypes or [F32] * n_o

        def body(*refs):
            ins = load(refs[:n_r + n_nd + n_p])
            outs, accs = fn(*ins)
            o_refs = refs[n_r + n_nd + n_p:]
            for o_ref, o in zip(o_refs[:n_o], outs):
                o_ref[...] = o.astype(o_ref.dtype)
            first = pl.program_id(0) == 0
            for a_ref, acc in zip(o_refs[n_o:], accs):
                @pl.when(first)
                def _(a_ref=a_ref):
                    a_ref[...] = jnp.zeros_like(a_ref)
                a_ref[...] += acc

        res = pl.pallas_call(
            body, name=name, grid=(nt_,), in_specs=in_specs,
            out_specs=[rspec(w, 0) for w in out_ws] + [pspec(w) for w in acc_ws],
            out_shape=[jax.ShapeDtypeStruct((length, w), dt) for w, dt in zip(out_ws, out_dtypes_)]
                      + [jax.ShapeDtypeStruct((1, w), F32) for w in acc_ws],
            compiler_params=_cparams(("arbitrary",), VMEM_LIMIT),
        )(*in_arrays)
        return res[:n_o], res[n_o:]

    n_c = len(cts)

    def body(*refs):
        ins = load(refs[:n_r + n_nd + n_p])
        ct_vals = load(refs[n_r + n_nd + n_p:n_r + n_nd + n_p + n_c])
        nd = ins[n_r:n_r + n_nd]
        f = lambda rs, ps: fn(*rs, *nd, *ps)[0]
        _, vjp = jax.vjp(f, ins[:n_r], ins[n_r + n_nd:])
        d_rows, d_params = vjp(tuple(ct_vals))
        o_refs = refs[n_r + n_nd + n_p + n_c:]
        for o_ref, d in zip(o_refs[:n_r], d_rows):
            o_ref[...] = d.astype(o_ref.dtype)
        first = pl.program_id(0) == 0
        for a_ref, d in zip(o_refs[n_r:], d_params):
            @pl.when(first)
            def _(a_ref=a_ref):
                a_ref[...] = jnp.zeros_like(a_ref)
            a_ref[...] += d

    res = pl.pallas_call(
        body, name=name, grid=(nt_,),
        in_specs=in_specs + [rspec(c[1], c[2]) for c in cts],
        out_specs=[rspec(r[1], 0) for r in rows] + [pspec(p.shape[1]) for p in params],
        out_shape=[jax.ShapeDtypeStruct((length, r[1]), dt) for r, dt in zip(rows, d_dtypes or [F32] * n_r)]
                  + [jax.ShapeDtypeStruct((1, p.shape[1]), F32) for p in params],
        compiler_params=_cparams(("arbitrary",), VMEM_LIMIT),
    )(*in_arrays, *[c[0] for c in cts])
    return res[:n_r], res[n_r:]


def _conv_shift(ext, k, tm, forward):
    s = CONV_W - 1 - k
    if forward:
        rolled = ext if s == 0 else pltpu.roll(ext, s, 0)
        return rolled[SUBLANE:, :]
    rolled = ext if s == 0 else pltpu.roll(ext, tm + SUBLANE - s, 0)
    return rolled[:tm, :]


def _conv_pre(x_ref, halo_ref, w_ref, b_ref, first, tm):
    halo = jnp.where(first, 0.0, halo_ref[...])
    ext = jnp.concatenate([halo, x_ref[...]], axis=0)
    w = w_ref[...]
    pre = b_ref[...] + jnp.zeros_like(x_ref[...])
    taps = []
    for k in range(CONV_W):
        tap = _conv_shift(ext, k, tm, True)
        taps.append(tap)
        pre = pre + tap * w[k:k + 1, :]
    return pre, taps


def _conv_fwd(proj, w, b, length, tm, tc):
    hb = tm // SUBLANE

    def body(x_ref, halo_ref, w_ref, b_ref, o_ref):
        pre, _ = _conv_pre(x_ref, halo_ref, w_ref, b_ref, pl.program_id(0) == 0, tm)
        o_ref[...] = _silu(pre)

    return pl.pallas_call(
        body, name="conv_fwd", grid=(length // tm, CONV_ALL // tc),
        in_specs=[pl.BlockSpec((tm, tc), lambda i, j: (i, j)),
                  pl.BlockSpec((SUBLANE, tc), lambda i, j: (jnp.maximum(i * hb - 1, 0), j)),
                  pl.BlockSpec((CONV_W, tc), lambda i, j: (0, j)),
                  pl.BlockSpec((1, tc), lambda i, j: (0, j))],
        out_specs=pl.BlockSpec((tm, tc), lambda i, j: (i, j)),
        out_shape=jax.ShapeDtypeStruct((length, CONV_ALL), F32),
        compiler_params=_cparams(("parallel", "parallel"), VMEM_LIMIT),
    )(proj, proj, w, b)


def _conv_bwd_pre(proj, w, b, dxc, length, tm, tc):
    hb = tm // SUBLANE

    def body(x_ref, halo_ref, w_ref, b_ref, dy_ref, dpre_ref, dw_ref, db_ref):
        i = pl.program_id(1)
        pre, taps = _conv_pre(x_ref, halo_ref, w_ref, b_ref, i == 0, tm)
        sg = jax.nn.sigmoid(pre)
        dpre = dy_ref[...] * (sg * (1.0 + pre * (1.0 - sg)))
        dpre_ref[...] = dpre

        @pl.when(i == 0)
        def _():
            dw_ref[...] = jnp.zeros_like(dw_ref)
            db_ref[...] = jnp.zeros_like(db_ref)

        for k in range(CONV_W):
            dw_ref[k:k + 1, :] += jnp.sum(dpre * taps[k], axis=0, keepdims=True)
        db_ref[...] += jnp.sum(dpre, axis=0, keepdims=True)

    return pl.pallas_call(
        body, name="conv_bwd_pre", grid=(CONV_ALL // tc, length // tm),
        in_specs=[pl.BlockSpec((tm, tc), lambda j, i: (i, j)),
                  pl.BlockSpec((SUBLANE, tc), lambda j, i: (jnp.maximum(i * hb - 1, 0), j)),
                  pl.BlockSpec((CONV_W, tc), lambda j, i: (0, j)),
                  pl.BlockSpec((1, tc), lambda j, i: (0, j)),
                  pl.BlockSpec((tm, tc), lambda j, i: (i, j))],
        out_specs=[pl.BlockSpec((tm, tc), lambda j, i: (i, j)),
                   pl.BlockSpec((CONV_W, tc), lambda j, i: (0, j)),
                   pl.BlockSpec((1, tc), lambda j, i: (0, j))],
        out_shape=[jax.ShapeDtypeStruct((length, CONV_ALL), F32),
                   jax.ShapeDtypeStruct((CONV_W, CONV_ALL), F32),
                   jax.ShapeDtypeStruct((1, CONV_ALL), F32)],
        compiler_params=_cparams(("parallel", "arbitrary"), VMEM_LIMIT),
    )(proj, proj, w, b, dxc)


def _conv_bwd_x(dpre, w, length, tm, tc):
    hb = tm // SUBLANE
    n_t = length // tm
    last_blk = length // SUBLANE - 1

    def body(d_ref, halo_ref, w_ref, o_ref):
        halo = jnp.where(pl.program_id(0) == n_t - 1, 0.0, halo_ref[...])
        ext = jnp.concatenate([d_ref[...], halo], axis=0)
        w = w_ref[...]
        acc = jnp.zeros_like(d_ref[...])
        for k in range(CONV_W):
            acc = acc + _conv_shift(ext, k, tm, False) * w[k:k + 1, :]
        o_ref[...] = acc.astype(o_ref.dtype)

    return pl.pallas_call(
        body, name="conv_bwd_x", grid=(n_t, CONV_ALL // tc),
        in_specs=[pl.BlockSpec((tm, tc), lambda i, j: (i, j)),
                  pl.BlockSpec((SUBLANE, tc), lambda i, j: (jnp.minimum((i + 1) * hb, last_blk), j)),
                  pl.BlockSpec((CONV_W, tc), lambda i, j: (0, j))],
        out_specs=pl.BlockSpec((tm, tc), lambda i, j: (i, j)),
        out_shape=jax.ShapeDtypeStruct((length, CONV_ALL), BF16),
        compiler_params=_cparams(("parallel", "parallel"), VMEM_LIMIT),
    )(dpre, dpre, w)


def _gdn_scalars(sm):
    gb, ga = _MINE["gb"] - SMALL_OFF, _MINE["ga"] - SMALL_OFF
    return ([_lane_col(sm, gb + h) for h in range(GDN_HEADS)], [_lane_col(sm, ga + h) for h in range(GDN_HEADS)])


def _gdn_fwd(xc, proj, alog, dtb, nw, length, comm=None):
    nc = length // CHUNK
    h_ = GDN_HEADS
    hw = h_ * LANE
    n_comm = len(comm[1]) if comm else 0

    def body(*refs):
        q_ref, k_ref, v_ref, gz_ref, sm_ref, al_ref, dt_ref, nw_ref = refs[:8]
        c_in = refs[8:8 + n_comm]
        on_ref, st_ref = refs[8 + n_comm:10 + n_comm]
        c_out = refs[10 + n_comm:10 + 2 * n_comm]
        s_ref = refs[10 + 2 * n_comm]
        sems = refs[11 + 2 * n_comm:]

        @pl.when(pl.program_id(0) == 0)
        def _():
            s_ref[...] = jnp.zeros_like(s_ref)
            if comm:
                _comm_plan(comm[0], c_in, c_out, *sems)[0]()

        st_ref[...] = s_ref[...]
        heads = lambda ref: [ref[:, h * LANE:(h + 1) * LANE] for h in range(h_)]
        lead = lambda ref: [ref[h] for h in range(h_)]
        b_raw, a_raw = _gdn_scalars(sm_ref[...])
        on, s_new = _gdn_chunk(heads(q_ref), heads(k_ref), heads(v_ref), heads(gz_ref), b_raw, a_raw,
                               lead(al_ref), lead(dt_ref), nw_ref[...], lead(s_ref))
        for h in range(h_):
            on_ref[:, h * LANE:(h + 1) * LANE] = on[h].astype(on_ref.dtype)
            s_ref[h] = s_new[h]

        if comm:
            @pl.when(pl.program_id(0) == nc - 1)
            def _():
                _comm_plan(comm[0], c_in, c_out, *sems)[1]()

    blk = lambda name: pl.BlockSpec((CHUNK, hw), lambda c: (c, _MINE[name] // hw))
    one = pl.BlockSpec((h_, 1, 1), lambda c: (0, 0, 0))
    hbm = pl.BlockSpec(memory_space=pl.ANY)
    return pl.pallas_call(
        body, name="gdn_fwd_" + comm[0] if comm else "gdn_fwd", grid=(nc,),
        in_specs=[blk("gq"), blk("gk"), blk("gv"), blk("gz"),
                  pl.BlockSpec((CHUNK, LANE), lambda c: (c, SMALL_OFF // LANE)), one, one,
                  pl.BlockSpec((1, LANE), lambda c: (0, 0))] + [hbm] * n_comm,
        out_specs=[pl.BlockSpec((CHUNK, hw), lambda c: (c, 0)),
                   pl.BlockSpec((h_, None, LANE, LANE), lambda c: (0, c, 0, 0))] + [hbm] * n_comm,
        out_shape=[jax.ShapeDtypeStruct((length, hw), BF16),
                   jax.ShapeDtypeStruct((h_, nc, LANE, LANE), F32)]
                  + (_comm_out_shapes(*comm) if comm else []),
        scratch_shapes=[pltpu.VMEM((h_, LANE, LANE), F32)] + (_comm_scratch(n_comm) if comm else []),
        compiler_params=_cparams(("arbitrary",), VMEM_LIMIT),
    )(xc, xc, xc, proj, proj, alog, dtb, nw, *(comm[1] if comm else []))


def _gdn_bwd(xc, proj, alog, dtb, nw, st, dmix, length, comm=None):
    nc = length // CHUNK
    h_ = GDN_HEADS
    n_comm = len(comm[1]) if comm else 0
    n_in, n_out = 10, 8

    def body(*refs):
        q_ref, k_ref, v_ref, gz_ref, sm_ref, al_ref, dt_ref, nw_ref, st_ref, do_ref = refs[:n_in]
        c_in = refs[n_in:n_in + n_comm]
        o0 = n_in + n_comm
        dq_ref, dk_ref, dv_ref, dgz_ref, dsm_ref, dal_ref, ddt_ref, dnw_ref = refs[o0:o0 + n_out]
        c_out = refs[o0 + n_out:o0 + n_out + n_comm]
        ds_ref = refs[o0 + n_out + n_comm]
        sems = refs[o0 + n_out + n_comm + 1:]

        @pl.when(pl.program_id(0) == 0)
        def _():
            ds_ref[...] = jnp.zeros_like(ds_ref)
            dal_ref[...] = jnp.zeros_like(dal_ref)
            ddt_ref[...] = jnp.zeros_like(ddt_ref)
            dnw_ref[...] = jnp.zeros_like(dnw_ref)
            if comm:
                _comm_plan(comm[0], c_in, c_out, *sems)[0]()

        heads = lambda ref: [ref[:, h * LANE:(h + 1) * LANE] for h in range(h_)]
        lead = lambda ref: [ref[h] for h in range(h_)]
        b_raw, a_raw = _gdn_scalars(sm_ref[...])
        ins = (heads(q_ref), heads(k_ref), heads(v_ref), heads(gz_ref), b_raw, a_raw, lead(al_ref),
               lead(dt_ref), nw_ref[...], lead(st_ref))
        _, vjp = jax.vjp(_gdn_chunk, *ins)
        dq, dk, dv, dgz, db, da, dal, ddt, dnw, ds = vjp((heads(do_ref), lead(ds_ref)))
        lane = lax.broadcasted_iota(jnp.int32, (1, LANE), 1)
        gb, ga = _MINE["gb"] - SMALL_OFF, _MINE["ga"] - SMALL_OFF
        dsm = jnp.zeros(dsm_ref.shape, F32)
        for h in range(h_):
            cols = slice(h * LANE, (h + 1) * LANE)
            dq_ref[:, cols] = dq[h]
            dk_ref[:, cols] = dk[h]
            dv_ref[:, cols] = dv[h]
            dgz_ref[:, cols] = dgz[h].astype(dgz_ref.dtype)
            dsm = dsm + jnp.where(lane == gb + h, db[h], 0.0) + jnp.where(lane == ga + h, da[h], 0.0)
            dal_ref[h] += dal[h]
            ddt_ref[h] += ddt[h]
            ds_ref[h] = ds[h]
        dsm_ref[...] = dsm
        dnw_ref[...] += dnw

        if comm:
            @pl.when(pl.program_id(0) == nc - 1)
            def _():
                _comm_plan(comm[0], c_in, c_out, *sems)[1]()

    hw = h_ * LANE
    r = lambda c: nc - 1 - c
    blk = lambda name: pl.BlockSpec((CHUNK, hw), lambda c: (r(c), _MINE[name] // hw))
    one = pl.BlockSpec((h_, 1, 1), lambda c: (0, 0, 0))
    oblk = pl.BlockSpec((CHUNK, hw), lambda c: (r(c), 0))
    big = jax.ShapeDtypeStruct((length, hw), F32)
    hbm = pl.BlockSpec(memory_space=pl.ANY)
    return pl.pallas_call(
        body, name="gdn_bwd_" + comm[0] if comm else "gdn_bwd", grid=(nc,),
        in_specs=[blk("gq"), blk("gk"), blk("gv"), blk("gz"),
                  pl.BlockSpec((CHUNK, LANE), lambda c: (r(c), SMALL_OFF // LANE)), one, one,
                  pl.BlockSpec((1, LANE), lambda c: (0, 0)),
                  pl.BlockSpec((h_, None, LANE, LANE), lambda c: (0, r(c), 0, 0)),
                  oblk] + [hbm] * n_comm,
        out_specs=[oblk, oblk, oblk, oblk, pl.BlockSpec((CHUNK, LANE), lambda c: (r(c), 0)), one, one,
                   pl.BlockSpec((1, LANE), lambda c: (0, 0))] + [hbm] * n_comm,
        out_shape=[big, big, big, jax.ShapeDtypeStruct((length, hw), BF16),
                   jax.ShapeDtypeStruct((length, LANE), F32),
                   jax.ShapeDtypeStruct((h_, 1, 1), F32), jax.ShapeDtypeStruct((h_, 1, 1), F32),
                   jax.ShapeDtypeStruct((1, LANE), F32)] + (_comm_out_shapes(*comm) if comm else []),
        scratch_shapes=[pltpu.VMEM((h_, LANE, LANE), F32)] + (_comm_scratch(n_comm) if comm else []),
        compiler_params=_cparams(("arbitrary",), VMEM_LIMIT),
    )(xc, xc, xc, proj, proj, alog, dtb, nw, st, dmix, *(comm[1] if comm else []))


def _lane_col(block, lane):
    pick = lax.broadcasted_iota(jnp.int32, (1, block.shape[1]), 1) == lane
    return jnp.sum(jnp.where(pick, block, 0.0), axis=1, keepdims=True)


def _ssd_head_lane(g, p, j):
    return (_MINE["sdt"] - SMALL_OFF) + (g * PAIRS_PER_GROUP + p) * 2 + j


def _ssd_fwd(xc, proj, alog, dtb, dpar, length):
    nc = length // CHUNK
    ppg = PAIRS_PER_GROUP

    gw = ppg * LANE

    def body(x_ref, b_ref, c_ref, sm_ref, al_ref, db_ref, dp_ref, y_ref, st_ref, hs_ref):
        g = pl.program_id(0)

        @pl.when(pl.program_id(1) == 0)
        def _():
            hs_ref[...] = jnp.zeros_like(hs_ref)

        bm, cm, sm = b_ref[...], c_ref[...], sm_ref[...]
        st_ref[...] = hs_ref[...]
        for p in range(ppg):
            two = lambda ref: [_lane_col(ref[p], j) for j in range(2)]
            dtr = [_lane_col(sm, _ssd_head_lane(g, p, j)) for j in range(2)]
            y, hs_new = _ssd_chunk(x_ref[:, p * LANE:(p + 1) * LANE], bm, cm, dtr, two(al_ref), two(db_ref),
                                   two(dp_ref), hs_ref[p])
            y_ref[:, p * LANE:(p + 1) * LANE] = y
            hs_ref[p] = hs_new

    two = pl.BlockSpec((None, ppg, 1, 2), lambda g, c: (g, 0, 0, 0))
    return pl.pallas_call(
        body, name="ssd_fwd", grid=(SSD_GROUPS, nc),
        in_specs=[pl.BlockSpec((CHUNK, gw), lambda g, c: (c, _MINE["sx"] // gw + g)),
                  pl.BlockSpec((CHUNK, LANE), lambda g, c: (c, CB("sB") + g)),
                  pl.BlockSpec((CHUNK, LANE), lambda g, c: (c, CB("sC") + g)),
                  pl.BlockSpec((CHUNK, LANE), lambda g, c: (c, SMALL_OFF // LANE)),
                  two, two, two],
        out_specs=[pl.BlockSpec((CHUNK, gw), lambda g, c: (c, g)),
                   pl.BlockSpec((None, ppg, None, LANE, SSD_N), lambda g, c: (g, 0, c, 0, 0))],
        out_shape=[jax.ShapeDtypeStruct((length, SSD_GROUPS * gw), F32),
                   jax.ShapeDtypeStruct((SSD_GROUPS, ppg, nc, LANE, SSD_N), F32)],
        scratch_shapes=[pltpu.VMEM((ppg, LANE, SSD_N), F32)],
        compiler_params=_cparams(("parallel", "arbitrary"), VMEM_LIMIT),
    )(xc, xc, xc, proj, alog, dtb, dpar)


def _ssd_bwd(xc, proj, alog, dtb, dpar, st, dy, length):
    nc = length // CHUNK
    ppg = PAIRS_PER_GROUP

    gw = ppg * LANE

    def body(x_ref, b_ref, c_ref, sm_ref, al_ref, db_ref, dp_ref, st_ref, dy_ref,
             dx_ref, dbm_ref, dcm_ref, dsm_ref, dal_ref, ddb_ref, ddp_ref, dhs_ref):
        g = pl.program_id(0)

        @pl.when(pl.program_id(1) == 0)
        def _():
            dhs_ref[...] = jnp.zeros_like(dhs_ref)
            dal_ref[...] = jnp.zeros_like(dal_ref)
            ddb_ref[...] = jnp.zeros_like(ddb_ref)
            ddp_ref[...] = jnp.zeros_like(ddp_ref)

        bm, cm, sm = b_ref[...], c_ref[...], sm_ref[...]
        lane = lax.broadcasted_iota(jnp.int32, (1, LANE), 1)
        first = lax.broadcasted_iota(jnp.int32, (1, 2), 1) == 0
        dbm_tot = jnp.zeros_like(bm)
        dcm_tot = jnp.zeros_like(cm)
        dsm = jnp.zeros_like(sm)
        for p in range(ppg):
            cols = slice(p * LANE, (p + 1) * LANE)
            two = lambda ref: [_lane_col(ref[p], j) for j in range(2)]
            dtr = [_lane_col(sm, _ssd_head_lane(g, p, j)) for j in range(2)]
            ins = (x_ref[:, cols], bm, cm, dtr, two(al_ref), two(db_ref), two(dp_ref), st_ref[p])
            _, vjp = jax.vjp(_ssd_chunk, *ins)
            dx, dbm, dcm, ddt, dal, ddb, ddp, dhs = vjp((dy_ref[:, cols], dhs_ref[p]))
            dx_ref[:, cols] = dx
            for j in range(2):
                dsm = dsm + jnp.where(lane == _ssd_head_lane(g, p, j), ddt[j], 0.0)
            dbm_tot = dbm_tot + dbm
            dcm_tot = dcm_tot + dcm
            dhs_ref[p] = dhs
            dal_ref[p] += jnp.where(first, dal[0], dal[1])
            ddb_ref[p] += jnp.where(first, ddb[0], ddb[1])
            ddp_ref[p] += jnp.where(first, ddp[0], ddp[1])
        dbm_ref[...] = dbm_tot
        dcm_ref[...] = dcm_tot
        dsm_ref[...] = dsm

    r = lambda c: nc - 1 - c
    two = pl.BlockSpec((None, ppg, 1, 2), lambda g, c: (g, 0, 0, 0))
    grp = pl.BlockSpec((CHUNK, LANE), lambda g, c: (r(c), g))
    prm = jax.ShapeDtypeStruct((SSD_GROUPS, ppg, 1, 2), F32)
    return pl.pallas_call(
        body, name="ssd_bwd", grid=(SSD_GROUPS, nc),
        in_specs=[pl.BlockSpec((CHUNK, gw), lambda g, c: (r(c), _MINE["sx"] // gw + g)),
                  pl.BlockSpec((CHUNK, LANE), lambda g, c: (r(c), CB("sB") + g)),
                  pl.BlockSpec((CHUNK, LANE), lambda g, c: (r(c), CB("sC") + g)),
                  pl.BlockSpec((CHUNK, LANE), lambda g, c: (r(c), SMALL_OFF // LANE)),
                  two, two, two,
                  pl.BlockSpec((None, ppg, None, LANE, SSD_N), lambda g, c: (g, 0, r(c), 0, 0)),
                  pl.BlockSpec((CHUNK, gw), lambda g, c: (r(c), g))],
        out_specs=[pl.BlockSpec((CHUNK, gw), lambda g, c: (r(c), g)), grp, grp,
                   pl.BlockSpec((None, CHUNK, LANE), lambda g, c: (g, r(c), 0)), two, two, two],
        out_shape=[jax.ShapeDtypeStruct((length, SSD_GROUPS * gw), F32),
                   jax.ShapeDtypeStruct((length, SSD_GROUPS * SSD_N), F32),
                   jax.ShapeDtypeStruct((length, SSD_GROUPS * SSD_N), F32),
                   jax.ShapeDtypeStruct((SSD_GROUPS, length, LANE), F32), prm, prm, prm],
        scratch_shapes=[pltpu.VMEM((ppg, LANE, SSD_N), F32)],
        compiler_params=_cparams(("parallel", "arbitrary"), VMEM_LIMIT),
    )(xc, xc, xc, proj, alog, dtb, dpar, st, dy)


def _ret_fwd(proj, cos2, sin2, dmat, qdec, kdec, cdec, nw, length):
    nc = length // CHUNK
    h_ = RET_HEADS

    def body(q_ref, k_ref, v_ref, g_ref, cos_ref, sin_ref, dm_ref, qd_ref, kd_ref, cd_ref, nw_ref,
             on_ref, st_ref, r_ref):
        @pl.when(pl.program_id(0) == 0)
        def _():
            r_ref[...] = jnp.zeros_like(r_ref)

        cos2_, sin2_, nw_ = cos_ref[...], sin_ref[...], nw_ref[...]
        res = []
        for h in range(h_):
            cols = slice(h * LANE, (h + 1) * LANE)
            r_old = r_ref[h]
            st_ref[h] = r_old
            res.append(_ret_chunk(q_ref[:, cols], k_ref[:, cols], v_ref[:, cols], g_ref[:, cols], cos2_, sin2_,
                                  dm_ref[h], qd_ref[h], kd_ref[h], cd_ref[h], nw_, r_old))
        for h in range(h_):
            on_ref[:, h * LANE:(h + 1) * LANE] = res[h][0].astype(on_ref.dtype)
            r_ref[h] = res[h][1]

    hw = h_ * LANE
    blk = lambda name: pl.BlockSpec((CHUNK, hw), lambda c: (c, _MINE[name] // hw))
    tab = pl.BlockSpec((CHUNK, LANE), lambda c: (c, 0))
    full = lambda *s: pl.BlockSpec(s, lambda c: (0,) * len(s))
    return pl.pallas_call(
        body, name="ret_fwd", grid=(nc,),
        in_specs=[blk("rq"), blk("rk"), blk("rv"), blk("rg"), tab, tab,
                  full(h_, CHUNK, CHUNK), full(h_, CHUNK, 1), full(h_, CHUNK, 1), full(h_, 1, 1), full(1, LANE)],
        out_specs=[pl.BlockSpec((CHUNK, hw), lambda c: (c, 0)),
                   pl.BlockSpec((h_, None, LANE, LANE), lambda c: (0, c, 0, 0))],
        out_shape=[jax.ShapeDtypeStruct((length, hw), BF16),
                   jax.ShapeDtypeStruct((h_, nc, LANE, LANE), F32)],
        scratch_shapes=[pltpu.VMEM((h_, LANE, LANE), F32)],
        compiler_params=_cparams(("arbitrary",), VMEM_LIMIT),
    )(proj, proj, proj, proj, cos2, sin2, dmat, qdec, kdec, cdec, nw)


def _ret_bwd(proj, cos2, sin2, dmat, qdec, kdec, cdec, nw, st, dmix, length):
    nc = length // CHUNK
    h_ = RET_HEADS

    def body(q_ref, k_ref, v_ref, g_ref, cos_ref, sin_ref, dm_ref, qd_ref, kd_ref, cd_ref, nw_ref, st_ref,
             do_ref, dq_ref, dk_ref, dv_ref, dg_ref, dnw_ref, dr_ref):
        @pl.when(pl.program_id(0) == 0)
        def _():
            dr_ref[...] = jnp.zeros_like(dr_ref)
            dnw_ref[...] = jnp.zeros_like(dnw_ref)

        cos2_, sin2_, nw_ = cos_ref[...], sin_ref[...], nw_ref[...]
        res = []
        for h in range(h_):
            cols = slice(h * LANE, (h + 1) * LANE)
            consts = (cos2_, sin2_, dm_ref[h], qd_ref[h], kd_ref[h], cd_ref[h])
            f = lambda q, k, v, g, w_, r_, consts=consts: _ret_chunk(q, k, v, g, *consts, w_, r_)
            _, vjp = jax.vjp(f, q_ref[:, cols], k_ref[:, cols], v_ref[:, cols], g_ref[:, cols], nw_, st_ref[h])
            res.append(vjp((do_ref[:, cols], dr_ref[h])))
        dnw_tot = dnw_ref[...]
        for h in range(h_):
            cols = slice(h * LANE, (h + 1) * LANE)
            dq, dk, dv, dg, dnw, dr = res[h]
            dq_ref[:, cols] = dq.astype(dq_ref.dtype)
            dk_ref[:, cols] = dk.astype(dk_ref.dtype)
            dv_ref[:, cols] = dv.astype(dv_ref.dtype)
            dg_ref[:, cols] = dg.astype(dg_ref.dtype)
            dnw_tot = dnw_tot + dnw
            dr_ref[h] = dr
        dnw_ref[...] = dnw_tot

    hw = h_ * LANE
    r = lambda c: nc - 1 - c
    blk = lambda name: pl.BlockSpec((CHUNK, hw), lambda c: (r(c), _MINE[name] // hw))
    tab = pl.BlockSpec((CHUNK, LANE), lambda c: (r(c), 0))
    full = lambda *s: pl.BlockSpec(s, lambda c: (0,) * len(s))
    oblk = pl.BlockSpec((CHUNK, hw), lambda c: (r(c), 0))
    big = jax.ShapeDtypeStruct((length, hw), BF16)
    return pl.pallas_call(
        body, name="ret_bwd", grid=(nc,),
        in_specs=[blk("rq"), blk("rk"), blk("rv"), blk("rg"), tab, tab,
                  full(h_, CHUNK, CHUNK), full(h_, CHUNK, 1), full(h_, CHUNK, 1), full(h_, 1, 1), full(1, LANE),
                  pl.BlockSpec((h_, None, LANE, LANE), lambda c: (0, r(c), 0, 0)),
                  pl.BlockSpec((CHUNK, hw), lambda c: (r(c), (MIX_W - hw) // hw))],
        out_specs=[oblk, oblk, oblk, oblk, full(1, LANE)],
        out_shape=[big, big, big, big, jax.ShapeDtypeStruct((1, LANE), F32)],
        scratch_shapes=[pltpu.VMEM((h_, LANE, LANE), F32)],
        compiler_params=_cparams(("arbitrary",), VMEM_LIMIT),
    )(proj, proj, proj, proj, cos2, sin2, dmat, qdec, kdec, cdec, nw, st, dmix)


def _mesh_pos():
    x, y, c = lax.axis_index("x"), lax.axis_index("y"), lax.axis_index("c")
    return x, y, c, 4 * x + 2 * y + c


def _peer(x, y, c, mask):
    return (x ^ ((mask >> 2) & 1), y ^ ((mask >> 1) & 1), c ^ (mask & 1))


def _comm_plan(kind, ins, outs, send_sems, recv_sems, local_sems):
    x, y, c, me = _mesh_pos()
    n = len(ins)
    src = (lambda a, idx: ins[a]) if kind == "gather" else (lambda a, idx: ins[a].at[idx])
    local = [pltpu.make_async_copy(src(a, me), outs[a].at[me], local_sems.at[a]) for a in range(n)]
    sends, recvs = [], []
    for mask in range(1, N_DEV):
        px, py, pc = _peer(x, y, c, mask)
        pidx = 4 * px + 2 * py + pc
        for a in range(n):
            k = (mask - 1) * n + a
            copy = lambda s, d, k=k: pltpu.make_async_remote_copy(
                src_ref=s, dst_ref=d, send_sem=send_sems.at[k], recv_sem=recv_sems.at[k],
                device_id=(px, py, pc), device_id_type=pl.DeviceIdType.MESH)
            sends.append(copy(src(a, pidx), outs[a].at[me]))
            recvs.append(copy(src(a, me), outs[a].at[pidx]))

    def start():
        for cp in local + sends:
            cp.start()

    def wait():
        for cp in sends:
            cp.wait_send()
        for cp in recvs:
            cp.wait_recv()
        for cp in local:
            cp.wait()

    return start, wait


def _comm_out_shapes(kind, arrays):
    return [jax.ShapeDtypeStruct(((N_DEV,) + s.shape) if kind == "gather" else s.shape, s.dtype) for s in arrays]


def _comm_scratch(n):
    k_tot = (N_DEV - 1) * n
    return [pltpu.SemaphoreType.DMA((k_tot,)), pltpu.SemaphoreType.DMA((k_tot,)), pltpu.SemaphoreType.DMA((n,))]


def _comm_call(name, kind, arrays):
    n = len(arrays)

    def body(*refs):
        start, wait = _comm_plan(kind, refs[:n], refs[n:2 * n], *refs[2 * n:])
        start()
        wait()

    hbm = pl.BlockSpec(memory_space=pl.ANY)
    return pl.pallas_call(
        body, name=name, in_specs=[hbm] * n, out_specs=[hbm] * n, out_shape=_comm_out_shapes(kind, arrays),
        scratch_shapes=_comm_scratch(n), compiler_params=pltpu.CompilerParams(has_side_effects=True),
    )(*arrays)


def _adam(name, parts, w, m, v, tr):
    p_, r_, cw = parts.shape
    c1 = 1.0 - ADAM_B1 ** ADAM_STEP
    c2 = 1.0 - ADAM_B2 ** ADAM_STEP

    def body(p_ref, w_ref, m_ref, v_ref, g_ref, d_ref, nm_ref, nv_ref):
        g = p_ref[0].astype(F32)
        for i in range(1, p_):
            g = g + p_ref[i].astype(F32)
        nm = ADAM_B1 * m_ref[...] + (1.0 - ADAM_B1) * g
        nv = ADAM_B2 * v_ref[...] + (1.0 - ADAM_B2) * (g * g)
        d_ref[...] = -ADAM_LR * ((nm / c1) / (jnp.sqrt(nv / c2) + ADAM_EPS) + ADAM_WD * w_ref[...])
        g_ref[...] = g
        nm_ref[...] = nm
        nv_ref[...] = nv

    spec = pl.BlockSpec((tr, cw), lambda i: (i, 0))
    shp = jax.ShapeDtypeStruct((r_, cw), F32)
    return pl.pallas_call(
        body, name=name, grid=(r_ // tr,),
        in_specs=[pl.BlockSpec((p_, tr, cw), lambda i: (0, i, 0)), spec, spec, spec],
        out_specs=[spec] * 4, out_shape=[shp] * 4,
        compiler_params=_cparams(("parallel",), VMEM_LIMIT),
    )(parts, w, m, v)


def _perm_cols(w_full):
    parts = [w_full[:, _ORIG[n][0]:_ORIG[n][0] + _ORIG[n][1]] for n in _ORDER]
    parts.append(jnp.zeros((w_full.shape[0], N_PAD - N_USED), w_full.dtype))
    return jnp.concatenate(parts, axis=1)


def _unperm_cols(g):
    by_orig = sorted(_ORDER, key=lambda n: _ORIG[n][0])
    return jnp.concatenate([g[:, _MINE[n]:_MINE[n] + _ORIG[n][1]] for n in by_orig], axis=1)


def _ret_consts(length):
    lg = jnp.log(1.0 - 2.0 ** (-5.0 - jnp.arange(RET_HEADS, dtype=F32)))
    idx = jnp.arange(CHUNK, dtype=F32)
    rel = idx[:, None] - idx[None, :]
    dmat = jnp.where(rel[None] >= 0, jnp.exp(jnp.maximum(rel, 0.0)[None] * lg[:, None, None]), 0.0)
    qdec = jnp.exp((idx[:, None] + 1.0) * lg[None, :]).T[:, :, None]
    kdec = jnp.exp((CHUNK - 1.0 - idx)[:, None] * lg[None, :]).T[:, :, None]
    cdec = jnp.exp(CHUNK * lg)[:, None, None]
    half = RET_DK // 2
    inv = ROPE_BASE ** (-jnp.arange(half, dtype=F32) / half)
    ang = jnp.arange(length, dtype=jnp.int32).astype(F32)[:, None] * inv[None, :]
    cos, sin = jnp.cos(ang), jnp.sin(ang)
    return (jnp.concatenate([cos, cos], axis=1), jnp.concatenate([-sin, sin], axis=1), dmat, qdec, kdec, cdec)


def _tiles(length):
    tm = 512 if length % 512 == 0 else length
    return tm


def _layer_fwd(x, p, consts, length, last, target, comm=None):
    tm = _tiles(length)
    (h,), _ = _rowmap("pre_norm", lambda x_, w_: ((_rms(x_, w_),), ()), length, tm,
                      [(x, D_MODEL, 0)], [p["pre_norm"]], out_ws=[D_MODEL], out_dtypes=[BF16])
    proj = _matmul("in_proj", h, p["w_in"], "nn", length, N_PAD, D_MODEL)
    xc = _conv_fwd(proj, p["conv_w"], p["conv_b"], length, tm, 512)
    o_a, st_a, *got = _gdn_fwd(xc, proj, p["gdn_A_log"], p["gdn_dt_bias"], p["gdn_norm"], length, comm)
    y_b, st_b = _ssd_fwd(xc, proj, p["ssd_A_log"], p["ssd_dt_bias"], p["ssd_D"], length)
    o_b = []
    for g in range(SSD_GROUPS):
        (o,), _ = _rowmap(f"ssd_gate{g}", lambda y_, z_, w_: ((_rms(y_ * _silu(z_), w_),), ()), length, tm,
                          [(y_b, 512, g), (proj, 512, _MINE["sz"] // 512 + g)], [p["ssd_norm"][g]], out_ws=[512],
                          out_dtypes=[BF16])
        o_b.append(o)
    o_c, st_c = _ret_fwd(proj, *consts, p["ret_norm"], length)
    mixed = jnp.concatenate([o.astype(BF16) for o in [o_a] + o_b + [o_c]], axis=1)
    out = _matmul("out_proj", mixed, p["w_out"], "nn", length, D_MODEL, MIX_W)
    res = dict(x=x, h=h, proj=proj, xc=xc, st_a=st_a, st_b=st_b, st_c=st_c, y_b=y_b, mixed=mixed, out=out)
    if not last:
        (y,), _ = _rowmap("post_norm", lambda x_, o_, w_: ((x_ + _rms(o_, w_),), ()), length, tm,
                          [(x, D_MODEL, 0), (out, D_MODEL, 0)], [p["post_norm"]], out_ws=[D_MODEL])
        return y, res, None, got

    def head(x_, o_, t_, w_):
        e = x_ + _rms(o_, w_) - t_
        row = jnp.mean(e * e, axis=-1, keepdims=True)
        loss = 0.5 * jnp.sum(row, axis=0, keepdims=True)
        return (e * (1.0 / D_MODEL),), (loss + jnp.zeros((1, LANE), F32),)

    (dy,), (loss,) = _rowmap("loss_head", head, length, tm,
                             [(x, D_MODEL, 0), (out, D_MODEL, 0), (target, D_MODEL, 0)], [p["post_norm"]],
                             out_ws=[D_MODEL], acc_ws=[LANE])
    return None, res, (dy, loss[0, 0]), got


def _layer_bwd(dy, p, res, consts, length, comm=None):
    tm = _tiles(length)
    g = {}
    (d_out,), (g["post_norm"],) = _rowmap("post_norm_bwd", lambda o_, w_: ((_rms(o_, w_),), ()), length, tm,
                                          [(res["out"], D_MODEL, 0)], [p["post_norm"]], cts=[(dy, D_MODEL, 0)])
    dmix = _matmul("out_proj_dx", d_out, p["w_out"], "nt", length, MIX_W, D_MODEL)
    g["w_out"] = _matmul("out_proj_dw", res["mixed"], d_out, "tn", MIX_W, D_MODEL, length)
    proj, xc = res["proj"], res["xc"]
    (dgq, dgk, dgv, dgz, dsm_a, g["gdn_A_log"], g["gdn_dt_bias"], g["gdn_norm"], *got) = _gdn_bwd(
        xc, proj, p["gdn_A_log"], p["gdn_dt_bias"], p["gdn_norm"], res["st_a"], dmix, length, comm)
    dyb, dsz, dsn = [], [], []
    for gi in range(SSD_GROUPS):
        (dy_g, dz_g), (dn_g,) = _rowmap(
            f"ssd_gate_bwd{gi}", lambda y_, z_, w_: ((_rms(y_ * _silu(z_), w_),), ()), length, tm,
            [(res["y_b"], 512, gi), (proj, 512, _MINE["sz"] // 512 + gi)], [p["ssd_norm"][gi]],
            cts=[(dmix, 512, 1 + gi)], d_dtypes=[F32, BF16])
        dyb.append(dy_g)
        dsz.append(dz_g)
        dsn.append(dn_g)
    g["ssd_norm"] = dsn
    dyb = jnp.concatenate(dyb, axis=1)
    dsx, dsb, dsc, dsm_b, g["ssd_A_log"], g["ssd_dt_bias"], g["ssd_D"] = _ssd_bwd(
        xc, proj, p["ssd_A_log"], p["ssd_dt_bias"], p["ssd_D"], res["st_b"], dyb, length)
    drq, drk, drv, drg, dnw_c = _ret_bwd(proj, *consts, p["ret_norm"], res["st_c"], dmix, length)
    g["ret_norm"] = dnw_c
    dxc = jnp.concatenate([dgq, dgk, dgv, dsx, dsb, dsc], axis=1)
    dpre, g["conv_w"], g["conv_b"] = _conv_bwd_pre(proj, p["conv_w"], p["conv_b"], dxc, length, tm, 512)
    dconv = _conv_bwd_x(dpre, p["conv_w"], length, tm, 512)
    dsmall = dsm_a + dsm_b[0] + dsm_b[1]
    dproj = jnp.concatenate([d.astype(BF16) for d in [dconv, dgz] + dsz + [drq, drk, drv, drg, dsmall]]
                            + [jnp.zeros((length, N_PAD - SMALL_OFF - LANE), BF16)], axis=1)
    dh = _matmul("in_proj_dx", dproj, p["w_in"], "nt", length, D_MODEL, N_PAD)
    g["w_in"] = _matmul("in_proj_dw", res["h"], dproj, "tn", D_MODEL, N_PAD, length)
    (dx,), (g["pre_norm"],) = _rowmap(
        "pre_norm_bwd", lambda x_, w_: ((_rms(x_, w_), x_), ()), length, tm,
        [(res["x"], D_MODEL, 0)], [p["pre_norm"]], cts=[(dh, D_MODEL, 0), (dy, D_MODEL, 0)])
    return dx, g, got


def _make_layer(l, w_in_full, w_out_full, conv_full, sw):
    return dict(
        pre_norm=sw["pre_norm"][l][None], post_norm=sw["post_norm"][l][None], w_in=_perm_cols(w_in_full),
        w_out=w_out_full,
        conv_w=jnp.concatenate([conv_full[0], conv_full[1]], axis=1),
        conv_b=jnp.concatenate([jnp.zeros((CONV_CH,), F32), sw["ssd_conv_b"][l]])[None],
        gdn_A_log=sw["gdn_A_log"][l][:, None, None], gdn_dt_bias=sw["gdn_dt_bias"][l][:, None, None],
        gdn_norm=sw["gdn_norm"][l][None],
        ssd_A_log=sw["ssd_A_log"][l].reshape(SSD_GROUPS, PAIRS_PER_GROUP, 1, 2),
        ssd_dt_bias=sw["ssd_dt_bias"][l].reshape(SSD_GROUPS, PAIRS_PER_GROUP, 1, 2),
        ssd_D=sw["ssd_D"][l].reshape(SSD_GROUPS, PAIRS_PER_GROUP, 1, 2),
        ssd_norm=[sw["ssd_norm"][l][None, 512 * g:512 * (g + 1)] for g in range(SSD_GROUPS)],
        ret_norm=sw["ret_norm"][l][None])


def _local_step(xb, tgt, layer0, make_layer1, length, fwd_comm=None, make_bwd_comm=None):
    consts = _ret_consts(length)
    y0, res0, _, got = _layer_fwd(xb, layer0, consts, length, False, tgt, fwd_comm)
    layer1 = make_layer1(got)
    _, res1, (dy, loss_local), _ = _layer_fwd(y0, layer1, consts, length, True, tgt)
    dy, grads1, _ = _layer_bwd(dy, layer1, res1, consts, length)
    dx, grads0, recv = _layer_bwd(dy, layer0, res0, consts, length, make_bwd_comm(grads1) if make_bwd_comm else None)
    return loss_local, dx, [grads0, grads1], recv


_SMALL = ["pre_norm", "post_norm", "gdn_A_log", "gdn_dt_bias", "gdn_norm", "ssd_conv_b", "ssd_A_log",
          "ssd_dt_bias", "ssd_D", "ssd_norm", "ret_norm"]


def _pack_small(arrs):
    rows = []
    for n in _SMALL:
        flat = arrs[n].reshape(-1)
        pad = (-flat.shape[0]) % LANE
        rows.append(jnp.pad(flat, (0, pad)).reshape(-1, LANE))
    out = jnp.concatenate(rows, axis=0)
    return jnp.pad(out, ((0, (-out.shape[0]) % SUBLANE), (0, 0)))


def _unpack_small(packed, like):
    out, r = {}, 0
    for n in _SMALL:
        cnt = like[n].size
        nrow = -(-cnt // LANE)
        out[n] = packed[r:r + nrow].reshape(-1)[:cnt].reshape(like[n].shape)
        r += nrow
    return out


def kernel(x, pre_norm, post_norm, w_in, gdn_conv, gdn_A_log, gdn_dt_bias, gdn_norm, ssd_conv, ssd_conv_b, ssd_A_log, ssd_dt_bias, ssd_D, ssd_norm, ret_norm, w_out, loss_target, m_pre_norm, m_post_norm, m_w_in, m_gdn_conv, m_gdn_A_log, m_gdn_dt_bias, m_gdn_norm, m_ssd_conv, m_ssd_conv_b, m_ssd_A_log, m_ssd_dt_bias, m_ssd_D, m_ssd_norm, m_ret_norm, m_w_out, v_pre_norm, v_post_norm, v_w_in, v_gdn_conv, v_gdn_A_log, v_gdn_dt_bias, v_gdn_norm, v_ssd_conv, v_ssd_conv_b, v_ssd_A_log, v_ssd_dt_bias, v_ssd_D, v_ssd_norm, v_ret_norm, v_w_out):
    length = x.shape[1]
    xb = x[0]
    tgt = loss_target[0]
    small_w = dict(pre_norm=pre_norm, post_norm=post_norm, gdn_A_log=gdn_A_log, gdn_dt_bias=gdn_dt_bias,
                   gdn_norm=gdn_norm, ssd_conv_b=ssd_conv_b, ssd_A_log=ssd_A_log, ssd_dt_bias=ssd_dt_bias,
                   ssd_D=ssd_D, ssd_norm=ssd_norm, ret_norm=ret_norm)
    small_m = dict(pre_norm=m_pre_norm, post_norm=m_post_norm, gdn_A_log=m_gdn_A_log, gdn_dt_bias=m_gdn_dt_bias,
                   gdn_norm=m_gdn_norm, ssd_conv_b=m_ssd_conv_b, ssd_A_log=m_ssd_A_log, ssd_dt_bias=m_ssd_dt_bias,
                   ssd_D=m_ssd_D, ssd_norm=m_ssd_norm, ret_norm=m_ret_norm)
    small_v = dict(pre_norm=v_pre_norm, post_norm=v_post_norm, gdn_A_log=v_gdn_A_log, gdn_dt_bias=v_gdn_dt_bias,
                   gdn_norm=v_gdn_norm, ssd_conv_b=v_ssd_conv_b, ssd_A_log=v_ssd_A_log, ssd_dt_bias=v_ssd_dt_bias,
                   ssd_D=v_ssd_D, ssd_norm=v_ssd_norm, ret_norm=v_ret_norm)

    w_in_b, w_out_b = w_in.astype(BF16), w_out.astype(BF16)
    conv_shard = jnp.stack([gdn_conv, ssd_conv], axis=1)
    shards = lambda l: [w_in_b[l], w_out_b[l], conv_shard[l]]

    def assemble(l, g_in, g_out, g_conv):
        w_in_full = g_in.transpose(1, 0, 2).reshape(D_MODEL, N_IN)
        w_out_full = g_out.reshape(MIX_W, D_MODEL)
        conv_full = g_conv.transpose(1, 2, 0, 3).reshape(2, CONV_W, CONV_CH)
        return _make_layer(l, w_in_full, w_out_full, conv_full, small_w)

    def grad_slabs(g):
        gi = _unperm_cols(g["w_in"]).reshape(D_MODEL, N_DEV, SHARD_IN).transpose(1, 0, 2).astype(BF16)
        go = g["w_out"].reshape(N_DEV, SHARD_OUT, D_MODEL).astype(BF16)
        gc = jnp.stack([g["conv_w"][:, k * CONV_CH:(k + 1) * CONV_CH].reshape(CONV_W, N_DEV, SHARD_CONV)
                        .transpose(1, 0, 2) for k in range(2)], axis=1).reshape(N_DEV, 2 * CONV_W, SHARD_CONV)
        return [gi, go, gc]

    layer0 = assemble(0, *_comm_call("gather_layer0", "gather", shards(0)))
    loss_local, dx, grads, recv1 = _local_step(
        xb, tgt, layer0, lambda got: assemble(1, *got), length,
        fwd_comm=("gather", shards(1)), make_bwd_comm=lambda g1: ("scatter", grad_slabs(g1)))
    grad_x = dx[None]
    loss = lax.psum(loss_local, ("x", "y", "c"))

    small_g = dict(
        pre_norm=jnp.concatenate([grads[l]["pre_norm"] for l in range(DEPTH)], axis=0),
        post_norm=jnp.concatenate([grads[l]["post_norm"] for l in range(DEPTH)], axis=0),
        gdn_A_log=jnp.stack([grads[l]["gdn_A_log"][:, 0, 0] for l in range(DEPTH)]),
        gdn_dt_bias=jnp.stack([grads[l]["gdn_dt_bias"][:, 0, 0] for l in range(DEPTH)]),
        gdn_norm=jnp.concatenate([grads[l]["gdn_norm"] for l in range(DEPTH)], axis=0),
        ssd_conv_b=jnp.concatenate([grads[l]["conv_b"][:, CONV_CH:] for l in range(DEPTH)], axis=0),
        ssd_A_log=jnp.stack([grads[l]["ssd_A_log"].reshape(SSD_HEADS) for l in range(DEPTH)]),
        ssd_dt_bias=jnp.stack([grads[l]["ssd_dt_bias"].reshape(SSD_HEADS) for l in range(DEPTH)]),
        ssd_D=jnp.stack([grads[l]["ssd_D"].reshape(SSD_HEADS) for l in range(DEPTH)]),
        ssd_norm=jnp.concatenate([jnp.concatenate(grads[l]["ssd_norm"], axis=1) for l in range(DEPTH)], axis=0),
        ret_norm=jnp.concatenate([grads[l]["ret_norm"] for l in range(DEPTH)], axis=0))
    gs = _pack_small(small_g)
    gs8 = jnp.broadcast_to(gs[None], (N_DEV,) + gs.shape)
    *recv0, r_small = _comm_call("exchange_layer0", "scatter", grad_slabs(grads[0]) + [gs8])
    recv = [recv0, recv1]

    conv_w = lambda g_, s_, l: jnp.stack([g_[l], s_[l]], axis=0).reshape(2 * CONV_W, SHARD_CONV)
    o_in, o_out, o_conv = [], [], []
    for l in range(DEPTH):
        o_in.append(_adam(f"adam_w_in{l}", recv[l][0], w_in[l], m_w_in[l], v_w_in[l], 128))
        o_out.append(_adam(f"adam_w_out{l}", recv[l][1], w_out[l], m_w_out[l], v_w_out[l], 128))
        o_conv.append(_adam(f"adam_conv{l}", recv[l][2], conv_w(gdn_conv, ssd_conv, l),
                            conv_w(m_gdn_conv, m_ssd_conv, l), conv_w(v_gdn_conv, v_ssd_conv, l), 2 * CONV_W))
    ps_w, ps_m, ps_v = _pack_small(small_w), _pack_small(small_m), _pack_small(small_v)
    o_small = _adam("adam_small", r_small, ps_w, ps_m, ps_v, ps_w.shape[0])

    names = ["pre_norm", "post_norm", "w_in", "gdn_conv", "gdn_A_log", "gdn_dt_bias", "gdn_norm", "ssd_conv",
             "ssd_conv_b", "ssd_A_log", "ssd_dt_bias", "ssd_D", "ssd_norm", "ret_norm", "w_out"]
    outs = []
    for kind in range(4):
        d = _unpack_small(o_small[kind], small_w)
        cv = jnp.stack([o_conv[l][kind].reshape(2, CONV_W, SHARD_CONV) for l in range(DEPTH)], axis=1)
        d["w_in"] = jnp.stack([o_in[l][kind] for l in range(DEPTH)], axis=0)
        d["w_out"] = jnp.stack([o_out[l][kind] for l in range(DEPTH)], axis=0)
        d["gdn_conv"] = cv[0]
        d["ssd_conv"] = cv[1]
        outs.extend(d[n] for n in names)
    return (loss, grad_x, *outs)
```

```python
import functools
import math

import numpy as np
import jax
import jax.numpy as jnp
from jax import lax
from jax.experimental import pallas as pl
from jax.experimental.pallas import tpu as pltpu

F32 = jnp.float32
BF16 = jnp.bfloat16

D_MODEL = 1024
DEPTH = 2
CHUNK = 64
CONV_W = 4
EPS = 1e-6
N_DEV = 8

GDN_HEADS = 4
GDN_DK = 128
SSD_HEADS = 16
SSD_P = 64
SSD_N = 128
SSD_GROUPS = 2
SSD_PAIRS = SSD_HEADS // 2
PAIRS_PER_GROUP = SSD_PAIRS // SSD_GROUPS
RET_HEADS = 4
RET_DK = 128
ROPE_BASE = 10000.0
MIX_W = 2048
N_IN = 6680
SHARD_IN = N_IN // N_DEV
SHARD_OUT = MIX_W // N_DEV
CONV_CH = 1536
SHARD_CONV = CONV_CH // N_DEV

ADAM_LR = 0.001
ADAM_B1 = 0.9
ADAM_B2 = 0.999
ADAM_EPS = 1e-08
ADAM_WD = 0.01
ADAM_STEP = 10

LANE = 128
SUBLANE = 8
VMEM_LIMIT = 56 * 1024 * 1024

_ORIG = dict(gq=(0, 512), gk=(512, 512), gv=(1024, 512), gz=(1536, 512), gb=(2048, 4), ga=(2052, 4),
             sx=(2056, 1024), sB=(3080, 256), sC=(3336, 256), sz=(3592, 1024), sdt=(4616, 16),
             rq=(4632, 512), rk=(5144, 512), rv=(5656, 512), rg=(6168, 512))
_ORDER = ["gq", "gk", "gv", "sx", "sB", "sC", "gz", "sz", "rq", "rk", "rv", "rg", "gb", "ga", "sdt"]
_MINE = {}
_off = 0
for _n in _ORDER:
    _MINE[_n] = _off
    _off += _ORIG[_n][1]
N_USED = _off
N_PAD = 7168
CONV_ALL = 2 * CONV_CH
SMALL_OFF = _MINE["gb"]
CB = lambda name: _MINE[name] // LANE


def _cparams(sem, vmem=None):
    return pltpu.CompilerParams(dimension_semantics=sem, vmem_limit_bytes=vmem)


def _split_bf16(a):
    hi = a.astype(BF16)
    return hi, (a - hi.astype(F32)).astype(BF16)


def _make_mm(split):
    def raw(a, b, ca, cb):
        dot = lambda x, y: lax.dot_general(x, y, (((ca,), (cb,)), ((), ())), preferred_element_type=F32)
        if split:
            ah, al = _split_bf16(a)
            bh, bl = _split_bf16(b)
            return dot(ah, bh) + (dot(ah, bl) + dot(al, bh))
        return dot(a.astype(BF16), b.astype(BF16))

    @jax.custom_vjp
    def nn(a, b):
        return raw(a, b, 1, 0)

    @jax.custom_vjp
    def nt(a, b):
        return raw(a, b, 1, 1)

    @jax.custom_vjp
    def tn(a, b):
        return raw(a, b, 0, 0)

    nn.defvjp(lambda a, b: (raw(a, b, 1, 0), (a, b)), lambda r, g: (nt(g, r[1]), tn(r[0], g)))
    nt.defvjp(lambda a, b: (raw(a, b, 1, 1), (a, b)), lambda r, g: (nn(g, r[1]), tn(g, r[0])))
    tn.defvjp(lambda a, b: (raw(a, b, 0, 0), (a, b)), lambda r, g: (nt(r[1], g), nn(r[0], g)))
    return nn, nt, tn


_nn, _nt, _tn = _make_mm(False)
_nnH, _ntH, _tnH = _make_mm(True)


@jax.custom_vjp
def _swap_halves(t):
    return pltpu.roll(t, LANE // 2, 1)


_swap_halves.defvjp(lambda t: (pltpu.roll(t, LANE // 2, 1), None),
                    lambda _, g: (pltpu.roll(g, LANE // 2, 1),))


@jax.custom_vjp
def _split_rows(x):
    return tuple(x[i * CHUNK:(i + 1) * CHUNK] for i in range(x.shape[0] // CHUNK))


_split_rows.defvjp(lambda x: (tuple(x[i * CHUNK:(i + 1) * CHUNK] for i in range(x.shape[0] // CHUNK)), None),
                   lambda _, gs: (jnp.concatenate(gs, axis=0),))


@jax.custom_vjp
def _tri_inv(a):
    return _tri_inv_impl(a)


def _tri_inv_impl(a):
    ii = lax.broadcasted_iota(jnp.int32, a.shape, 0)
    jj = lax.broadcasted_iota(jnp.int32, a.shape, 1)
    n = -a
    t = jnp.where(ii == jj, 1.0, 0.0).astype(F32) + n
    p = n
    steps = int(math.log2(CHUNK)) - 1
    for _ in range(steps):
        p = _nnH(p, p)
        t = t + _nnH(t, p)
    return t


_tri_inv.defvjp(lambda a: (lambda t: (t, t))(_tri_inv_impl(a)),
                lambda t, g: (-_ntH(_tnH(t, g), t),))


def _silu(x):
    return x * jax.nn.sigmoid(x)


@jax.custom_vjp
def _softplus(x):
    return jnp.maximum(x, 0.0) + jnp.log1p(jnp.exp(-jnp.abs(x)))


_softplus.defvjp(lambda x: (jnp.maximum(x, 0.0) + jnp.log1p(jnp.exp(-jnp.abs(x))), x),
                 lambda x, g: (g * jax.nn.sigmoid(x),))


def _rms(x, w):
    return x * lax.rsqrt(jnp.mean(x * x, axis=-1, keepdims=True) + EPS) * w


def _chunk_masks(n):
    ii = lax.broadcasted_iota(jnp.int32, (n, n), 0)
    jj = lax.broadcasted_iota(jnp.int32, (n, n), 1)
    return ii >= jj, ii > jj, ii == jj, ii <= jj


def _cumsum_col(g, causal, eye, upper):
    g_row = jnp.sum(jnp.where(eye, g, 0.0), axis=0, keepdims=True)
    col = jnp.sum(jnp.where(causal, g_row, 0.0), axis=1, keepdims=True)
    row = jnp.sum(jnp.where(upper, g, 0.0), axis=0, keepdims=True)
    return col, row


def _gdn_chunk(q, k, v, gz, b_raw, a_raw, alog, dtb, nw, s):
    nh = len(q)
    c = q[0].shape[0]
    causal, _, eye, upper = _chunk_masks(c)
    stack = lambda xs: jnp.concatenate(xs, axis=0)
    qn, kn, kb, vb, kbeg, qeg, gcum, glast = [], [], [], [], [], [], [], []
    for h in range(nh):
        qn_h = q[h] * lax.rsqrt(jnp.sum(q[h] * q[h], axis=-1, keepdims=True) + EPS) * (GDN_DK ** -0.5)
        kn_h = k[h] * lax.rsqrt(jnp.sum(k[h] * k[h], axis=-1, keepdims=True) + EPS)
        beta = jax.nn.sigmoid(b_raw[h])
        g = -jnp.exp(alog[h]) * _softplus(a_raw[h] + dtb[h])
        gcum_h, _ = _cumsum_col(g, causal, eye, upper)
        eg = jnp.exp(gcum_h)
        qn.append(qn_h)
        kn.append(kn_h)
        kb.append(kn_h * beta)
        vb.append(v[h] * beta)
        kbeg.append(kn_h * beta * eg)
        qeg.append(qn_h * eg)
        gcum.append(gcum_h)
        glast.append(jnp.sum(g, axis=0, keepdims=True))
    n = nh * c
    ii = lax.broadcasted_iota(jnp.int32, (n, n), 0)
    jj = lax.broadcasted_iota(jnp.int32, (n, n), 1)
    sh = int(math.log2(c))
    same = lax.shift_right_logical(ii, sh) == lax.shift_right_logical(jj, sh)
    causal_bd = jnp.logical_and(same, ii >= jj)
    strict_bd = jnp.logical_and(same, ii > jj)
    gcum_all = stack(gcum)
    gcum_row = jnp.sum(jnp.where(ii == jj, gcum_all, 0.0), axis=0, keepdims=True)
    decay = jnp.where(causal_bd, jnp.exp(jnp.where(causal_bd, gcum_all - gcum_row, 0.0)), 0.0)
    kn_all = stack(kn)
    a_low = jnp.where(strict_bd, _nt(stack(kb), kn_all) * decay, 0.0)
    t = _tri_inv(a_low)
    u = _nn(t, stack(vb))
    w = _split_rows(_nn(t, stack(kbeg)))
    v_new_all = u - stack([_nn(w[h], s[h]) for h in range(nh)])
    attn = _nt(stack(qn), kn_all) * decay
    o = _split_rows(stack([_nn(qeg[h], s[h]) for h in range(nh)]) + _nn(attn, v_new_all))
    v_new = _split_rows(v_new_all)
    on, s_new = [], []
    for h in range(nh):
        kdec = kn[h] * jnp.exp(glast[h] - gcum[h])
        s_new.append(s[h] * jnp.exp(glast[h]) + _tn(kdec, v_new[h]))
        on.append(_rms(o[h], nw) * _silu(gz[h]))
    return on, s_new


def _ssd_chunk(x, bm, cm, dtr, alog, dtb, dp, hs):
    pairs = range(len(x))
    c = x[0].shape[0]
    lane_i = lax.broadcasted_iota(jnp.int32, (c, LANE), 1)
    lane_lo = lane_i < SSD_P
    lane_lo1 = lax.broadcasted_iota(jnp.int32, (1, LANE), 1) < SSD_P
    causal2 = lax.broadcasted_iota(jnp.int32, (c, LANE), 0) >= jnp.bitwise_and(lane_i, c - 1)
    row_lo = lax.broadcasted_iota(jnp.int32, (LANE, 1), 0) < SSD_P
    eye2 = lax.broadcasted_iota(jnp.int32, (LANE, LANE), 0) == lax.broadcasted_iota(jnp.int32, (LANE, LANE), 1)
    to_row = lambda cols: jnp.sum(jnp.where(eye2, jnp.concatenate(cols, axis=0), 0.0), axis=0, keepdims=True)
    dt = [[_softplus(dtr[p][h] + dtb[p][h]) for h in range(2)] for p in pairs]
    a = [[dt[p][h] * -jnp.exp(alog[p][h]) for h in range(2)] for p in pairs]
    a_row = [to_row(a[p]) for p in pairs]
    part = [jnp.where(causal2, a_row[p], 0.0) for p in pairs]
    acum = [[jnp.sum(jnp.where(lane_lo, part[p], 0.0), axis=1, keepdims=True),
             jnp.sum(jnp.where(lane_lo, 0.0, part[p]), axis=1, keepdims=True)] for p in pairs]
    acum_row = [to_row(acum[p]) for p in pairs]
    acum_col = [jnp.where(lane_lo, acum[p][0], acum[p][1]) for p in pairs]
    lmat = [jnp.where(causal2, jnp.exp(jnp.where(causal2, acum_col[p] - acum_row[p], 0.0)), 0.0) for p in pairs]
    alast = [[jnp.sum(a[p][h], axis=0, keepdims=True) for h in range(2)] for p in pairs]
    cb = _nt(cm, jnp.concatenate([bm, bm], axis=0))
    xdt = [x[p] * jnp.where(lane_lo, dt[p][0], dt[p][1]) for p in pairs]
    xdt_rows = [jnp.concatenate([jnp.where(lane_lo, xdt[p], 0.0), jnp.where(lane_lo, 0.0, xdt[p])], axis=0)
                for p in pairs]
    intra = [_nn(cb * lmat[p], xdt_rows[p]) for p in pairs]
    inter = [_nt(cm, hs[p]) * jnp.exp(acum_col[p]) for p in pairs]
    y = [x[p] * jnp.where(lane_lo1, dp[p][0], dp[p][1]) + intra[p] + inter[p] for p in pairs]
    wdec = [jnp.exp(jnp.where(lane_lo, alast[p][0] - acum[p][0], alast[p][1] - acum[p][1])) for p in pairs]
    scale = [jnp.exp(jnp.where(row_lo, alast[p][0], alast[p][1])) for p in pairs]
    upd = [_tn(xdt[p] * wdec[p], bm) for p in pairs]
    hs_new = [hs[p] * scale[p] + upd[p] for p in pairs]
    return y, hs_new


def _ret_chunk(rq, rk, rv, rg, cos2, sin2, dmat, qdec, kdec, cdec, nw, r):
    hs = range(len(rq))
    q = [rq[h] * cos2 + _swap_halves(rq[h]) * sin2 for h in hs]
    k = [(rk[h] * cos2 + _swap_halves(rk[h]) * sin2) * (RET_DK ** -0.5) for h in hs]
    s = [_nt(q[h], k[h]) * dmat[h] for h in hs]
    intra = [_nn(s[h], rv[h]) for h in hs]
    inter = [_nn(q[h], r[h]) * qdec[h] for h in hs]
    upd = [_tn(k[h] * kdec[h], rv[h]) for h in hs]
    r_new = [r[h] * cdec[h] + upd[h] for h in hs]
    on = [_rms(intra[h] + inter[h], nw) * _silu(rg[h]) for h in hs]
    return on, r_new


MM_TILE = 1024


def _tile(n, pref=MM_TILE):
    return pref if n % pref == 0 else n


def _matmul(name, a, b, mode, m, n, k, tk=None):
    tm, tn = _tile(m), _tile(n)
    tk = _tile(k) if tk is None else tk
    nk = k // tk
    ca, cb = {"nn": (1, 0), "nt": (1, 1), "tn": (0, 0)}[mode]

    def body(a_ref, b_ref, o_ref):
        part = lax.dot_general(a_ref[...].astype(BF16), b_ref[...].astype(BF16),
                               (((ca,), (cb,)), ((), ())), preferred_element_type=F32)
        if nk == 1:
            o_ref[...] = part
        else:
            kk = pl.program_id(2)

            @pl.when(kk == 0)
            def _():
                o_ref[...] = part

            @pl.when(kk > 0)
            def _():
                o_ref[...] += part

    a_spec = pl.BlockSpec((tk, tm), lambda i, j, kk: (kk, i)) if mode == "tn" else pl.BlockSpec((tm, tk), lambda i, j, kk: (i, kk))
    b_spec = pl.BlockSpec((tn, tk), lambda i, j, kk: (j, kk)) if mode == "nt" else pl.BlockSpec((tk, tn), lambda i, j, kk: (kk, j))
    return pl.pallas_call(
        body, name=name, grid=(m // tm, n // tn, nk),
        in_specs=[a_spec, b_spec], out_specs=pl.BlockSpec((tm, tn), lambda i, j, kk: (i, j)),
        out_shape=jax.ShapeDtypeStruct((m, n), F32),
        compiler_params=_cparams(("parallel", "parallel", "arbitrary"), VMEM_LIMIT),
    )(a, b)


def _rowmap(name, fn, length, tm, rows, params, out_ws=(), acc_ws=(), out_dtypes=None, cts=None, nd_rows=(),
            d_dtypes=None):
    nt_ = length // tm
    n_r, n_nd, n_p = len(rows), len(nd_rows), len(params)
    rspec = lambda bw, cbk: pl.BlockSpec((tm, bw), lambda i: (i, cbk))
    pspec = lambda w: pl.BlockSpec((1, w), lambda i: (0, 0))
    in_arrays = [r[0] for r in rows] + [r[0] for r in nd_rows] + list(params)
    in_specs = [rspec(r[1], r[2]) for r in rows] + [rspec(r[1], r[2]) for r in nd_rows] + [pspec(p.shape[1]) for p in params]

    def load(refs):
        return [r[...].astype(F32) for r in refs]

    if cts is None:
        n_o, n_a = len(out_ws), len(acc_ws)
        out_dtypes_ = out_dtypes or [F32] * n_o

        def body(*refs):
            ins = load(refs[:n_r + n_nd + n_p])
            outs, accs = fn(*ins)
            o_refs = refs[n_r + n_nd + n_p:]
            for o_ref, o in zip(o_refs[:n_o], outs):
                o_ref[...] = o.astype(o_ref.dtype)
            first = pl.program_id(0) == 0
            for a_ref, acc in zip(o_refs[n_o:], accs):
                @pl.when(first)
                def _(a_ref=a_ref):
                    a_ref[...] = jnp.zeros_like(a_ref)
                a_ref[...] += acc

        res = pl.pallas_call(
            body, name=name, grid=(nt_,), in_specs=in_specs,
            out_specs=[rspec(w, 0) for w in out_ws] + [pspec(w) for w in acc_ws],
            out_shape=[jax.ShapeDtypeStruct((length, w), dt) for w, dt in zip(out_ws, out_dtypes_)]
                      + [jax.ShapeDtypeStruct((1, w), F32) for w in acc_ws],
            compiler_params=_cparams(("arbitrary",), VMEM_LIMIT),
        )(*in_arrays)
        return res[:n_o], res[n_o:]

    n_c = len(cts)

    def body(*refs):
        ins = load(refs[:n_r + n_nd + n_p])
        ct_vals = load(refs[n_r + n_nd + n_p:n_r + n_nd + n_p + n_c])
        nd = ins[n_r:n_r + n_nd]
        f = lambda rs, ps: fn(*rs, *nd, *ps)[0]
        _, vjp = jax.vjp(f, ins[:n_r], ins[n_r + n_nd:])
        d_rows, d_params = vjp(tuple(ct_vals))
        o_refs = refs[n_r + n_nd + n_p + n_c:]
        for o_ref, d in zip(o_refs[:n_r], d_rows):
            o_ref[...] = d.astype(o_ref.dtype)
        first = pl.program_id(0) == 0
        for a_ref, d in zip(o_refs[n_r:], d_params):
            @pl.when(first)
            def _(a_ref=a_ref):
                a_ref[...] = jnp.zeros_like(a_ref)
            a_ref[...] += d

    res = pl.pallas_call(
        body, name=name, grid=(nt_,),
        in_specs=in_specs + [rspec(c[1], c[2]) for c in cts],
        out_specs=[rspec(r[1], 0) for r in rows] + [pspec(p.shape[1]) for p in params],
        out_shape=[jax.ShapeDtypeStruct((length, r[1]), dt) for r, dt in zip(rows, d_dtypes or [F32] * n_r)]
                  + [jax.ShapeDtypeStruct((1, p.shape[1]), F32) for p in params],
        compiler_params=_cparams(("arbitrary",), VMEM_LIMIT),
    )(*in_arrays, *[c[0] for c in cts])
    return res[:n_r], res[n_r:]


def _conv_shift(ext, k, tm, forward):
    s = CONV_W - 1 - k
    if forward:
        rolled = ext if s == 0 else pltpu.roll(ext, s, 0)
        return rolled[SUBLANE:, :]
    rolled = ext if s == 0 else pltpu.roll(ext, tm + SUBLANE - s, 0)
    return rolled[:tm, :]


def _conv_pre(x_ref, halo_ref, w_ref, b_ref, first, tm):
    halo = jnp.where(first, 0.0, halo_ref[...])
    ext = jnp.concatenate([halo, x_ref[...]], axis=0)
    w = w_ref[...]
    pre = b_ref[...] + jnp.zeros_like(x_ref[...])
    taps = []
    for k in range(CONV_W):
        tap = _conv_shift(ext, k, tm, True)
        taps.append(tap)
        pre = pre + tap * w[k:k + 1, :]
    return pre, taps


def _conv_fwd(proj, w, b, length, tm, tc):
    hb = tm // SUBLANE

    def body(x_ref, halo_ref, w_ref, b_ref, o_ref):
        pre, _ = _conv_pre(x_ref, halo_ref, w_ref, b_ref, pl.program_id(0) == 0, tm)
        o_ref[...] = _silu(pre)

    return pl.pallas_call(
        body, name="conv_fwd", grid=(length // tm, CONV_ALL // tc),
        in_specs=[pl.BlockSpec((tm, tc), lambda i, j: (i, j)),
                  pl.BlockSpec((SUBLANE, tc), lambda i, j: (jnp.maximum(i * hb - 1, 0), j)),
                  pl.BlockSpec((CONV_W, tc), lambda i, j: (0, j)),
                  pl.BlockSpec((1, tc), lambda i, j: (0, j))],
        out_specs=pl.BlockSpec((tm, tc), lambda i, j: (i, j)),
        out_shape=jax.ShapeDtypeStruct((length, CONV_ALL), F32),
        compiler_params=_cparams(("parallel", "parallel"), VMEM_LIMIT),
    )(proj, proj, w, b)


def _conv_bwd_pre(proj, w, b, dxc, length, tm, tc):
    hb = tm // SUBLANE

    def body(x_ref, halo_ref, w_ref, b_ref, dy_ref, dpre_ref, dw_ref, db_ref):
        i = pl.program_id(1)
        pre, taps = _conv_pre(x_ref, halo_ref, w_ref, b_ref, i == 0, tm)
        sg = jax.nn.sigmoid(pre)
        dpre = dy_ref[...] * (sg * (1.0 + pre * (1.0 - sg)))
        dpre_ref[...] = dpre

        @pl.when(i == 0)
        def _():
            dw_ref[...] = jnp.zeros_like(dw_ref)
            db_ref[...] = jnp.zeros_like(db_ref)

        for k in range(CONV_W):
            dw_ref[k:k + 1, :] += jnp.sum(dpre * taps[k], axis=0, keepdims=True)
        db_ref[...] += jnp.sum(dpre, axis=0, keepdims=True)

    return pl.pallas_call(
        body, name="conv_bwd_pre", grid=(CONV_ALL // tc, length // tm),
        in_specs=[pl.BlockSpec((tm, tc), lambda j, i: (i, j)),
                  pl.BlockSpec((SUBLANE, tc), lambda j, i: (jnp.maximum(i * hb - 1, 0), j)),
                  pl.BlockSpec((CONV_W, tc), lambda j, i: (0, j)),
                  pl.BlockSpec((1, tc), lambda j, i: (0, j)),
                  pl.BlockSpec((tm, tc), lambda j, i: (i, j))],
        out_specs=[pl.BlockSpec((tm, tc), lambda j, i: (i, j)),
                   pl.BlockSpec((CONV_W, tc), lambda j, i: (0, j)),
                   pl.BlockSpec((1, tc), lambda j, i: (0, j))],
        out_shape=[jax.ShapeDtypeStruct((length, CONV_ALL), F32),
                   jax.ShapeDtypeStruct((CONV_W, CONV_ALL), F32),
                   jax.ShapeDtypeStruct((1, CONV_ALL), F32)],
        compiler_params=_cparams(("parallel", "arbitrary"), VMEM_LIMIT),
    )(proj, proj, w, b, dxc)


def _conv_bwd_x(dpre, w, length, tm, tc):
    hb = tm // SUBLANE
    n_t = length // tm
    last_blk = length // SUBLANE - 1

    def body(d_ref, halo_ref, w_ref, o_ref):
        halo = jnp.where(pl.program_id(0) == n_t - 1, 0.0, halo_ref[...])
        ext = jnp.concatenate([d_ref[...], halo], axis=0)
        w = w_ref[...]
        acc = jnp.zeros_like(d_ref[...])
        for k in range(CONV_W):
            acc = acc + _conv_shift(ext, k, tm, False) * w[k:k + 1, :]
        o_ref[...] = acc.astype(o_ref.dtype)

    return pl.pallas_call(
        body, name="conv_bwd_x", grid=(n_t, CONV_ALL // tc),
        in_specs=[pl.BlockSpec((tm, tc), lambda i, j: (i, j)),
                  pl.BlockSpec((SUBLANE, tc), lambda i, j: (jnp.minimum((i + 1) * hb, last_blk), j)),
                  pl.BlockSpec((CONV_W, tc), lambda i, j: (0, j))],
        out_specs=pl.BlockSpec((tm, tc), lambda i, j: (i, j)),
        out_shape=jax.ShapeDtypeStruct((length, CONV_ALL), BF16),
        compiler_params=_cparams(("parallel", "parallel"), VMEM_LIMIT),
    )(dpre, dpre, w)


def _gdn_scalars(sm):
    gb, ga = _MINE["gb"] - SMALL_OFF, _MINE["ga"] - SMALL_OFF
    return ([_lane_col(sm, gb + h) for h in range(GDN_HEADS)], [_lane_col(sm, ga + h) for h in range(GDN_HEADS)])


def _gdn_fwd(xc, proj, alog, dtb, nw, length, comm=None):
    nc = length // CHUNK
    h_ = GDN_HEADS
    hw = h_ * LANE
    n_comm = len(comm[1]) if comm else 0

    def body(*refs):
        q_ref, k_ref, v_ref, gz_ref, sm_ref, al_ref, dt_ref, nw_ref = refs[:8]
        c_in = refs[8:8 + n_comm]
        on_ref, st_ref = refs[8 + n_comm:10 + n_comm]
        c_out = refs[10 + n_comm:10 + 2 * n_comm]
        s_ref = refs[10 + 2 * n_comm]
        sems = refs[11 + 2 * n_comm:]

        @pl.when(pl.program_id(0) == 0)
        def _():
            s_ref[...] = jnp.zeros_like(s_ref)
            if comm:
                _comm_plan(comm[0], c_in, c_out, *sems)[0]()

        st_ref[...] = s_ref[...]
        heads = lambda ref: [ref[:, h * LANE:(h + 1) * LANE] for h in range(h_)]
        lead = lambda ref: [ref[h] for h in range(h_)]
        b_raw, a_raw = _gdn_scalars(sm_ref[...])
        on, s_new = _gdn_chunk(heads(q_ref), heads(k_ref), heads(v_ref), heads(gz_ref), b_raw, a_raw,
                               lead(al_ref), lead(dt_ref), nw_ref[...], lead(s_ref))
        for h in range(h_):
            on_ref[:, h * LANE:(h + 1) * LANE] = on[h].astype(on_ref.dtype)
            s_ref[h] = s_new[h]

        if comm:
            @pl.when(pl.program_id(0) == nc - 1)
            def _():
                _comm_plan(comm[0], c_in, c_out, *sems)[1]()

    blk = lambda name: pl.BlockSpec((CHUNK, hw), lambda c: (c, _MINE[name] // hw))
    one = pl.BlockSpec((h_, 1, 1), lambda c: (0, 0, 0))
    hbm = pl.BlockSpec(memory_space=pl.ANY)
    return pl.pallas_call(
        body, name="gdn_fwd_" + comm[0] if comm else "gdn_fwd", grid=(nc,),
        in_specs=[blk("gq"), blk("gk"), blk("gv"), blk("gz"),
                  pl.BlockSpec((CHUNK, LANE), lambda c: (c, SMALL_OFF // LANE)), one, one,
                  pl.BlockSpec((1, LANE), lambda c: (0, 0))] + [hbm] * n_comm,
        out_specs=[pl.BlockSpec((CHUNK, hw), lambda c: (c, 0)),
                   pl.BlockSpec((h_, None, LANE, LANE), lambda c: (0, c, 0, 0))] + [hbm] * n_comm,
        out_shape=[jax.ShapeDtypeStruct((length, hw), BF16),
                   jax.ShapeDtypeStruct((h_, nc, LANE, LANE), F32)]
                  + (_comm_out_shapes(*comm) if comm else []),
        scratch_shapes=[pltpu.VMEM((h_, LANE, LANE), F32)] + (_comm_scratch(n_comm) if comm else []),
        compiler_params=_cparams(("arbitrary",), VMEM_LIMIT),
    )(xc, xc, xc, proj, proj, alog, dtb, nw, *(comm[1] if comm else []))


def _gdn_bwd(xc, proj, alog, dtb, nw, st, dmix, length, comm=None):
    nc = length // CHUNK
    h_ = GDN_HEADS
    n_comm = len(comm[1]) if comm else 0
    n_in, n_out = 10, 8

    def body(*refs):
        q_ref, k_ref, v_ref, gz_ref, sm_ref, al_ref, dt_ref, nw_ref, st_ref, do_ref = refs[:n_in]
        c_in = refs[n_in:n_in + n_comm]
        o0 = n_in + n_comm
        dq_ref, dk_ref, dv_ref, dgz_ref, dsm_ref, dal_ref, ddt_ref, dnw_ref = refs[o0:o0 + n_out]
        c_out = refs[o0 + n_out:o0 + n_out + n_comm]
        ds_ref = refs[o0 + n_out + n_comm]
        sems = refs[o0 + n_out + n_comm + 1:]

        @pl.when(pl.program_id(0) == 0)
        def _():
            ds_ref[...] = jnp.zeros_like(ds_ref)
            dal_ref[...] = jnp.zeros_like(dal_ref)
            ddt_ref[...] = jnp.zeros_like(ddt_ref)
            dnw_ref[...] = jnp.zeros_like(dnw_ref)
            if comm:
                _comm_plan(comm[0], c_in, c_out, *sems)[0]()

        heads = lambda ref: [ref[:, h * LANE:(h + 1) * LANE] for h in range(h_)]
        lead = lambda ref: [ref[h] for h in range(h_)]
        b_raw, a_raw = _gdn_scalars(sm_ref[...])
        ins = (heads(q_ref), heads(k_ref), heads(v_ref), heads(gz_ref), b_raw, a_raw, lead(al_ref),
               lead(dt_ref), nw_ref[...], lead(st_ref))
        _, vjp = jax.vjp(_gdn_chunk, *ins)
        dq, dk, dv, dgz, db, da, dal, ddt, dnw, ds = vjp((heads(do_ref), lead(ds_ref)))
        lane = lax.broadcasted_iota(jnp.int32, (1, LANE), 1)
        gb, ga = _MINE["gb"] - SMALL_OFF, _MINE["ga"] - SMALL_OFF
        dsm = jnp.zeros(dsm_ref.shape, F32)
        for h in range(h_):
            cols = slice(h * LANE, (h + 1) * LANE)
            dq_ref[:, cols] = dq[h]
            dk_ref[:, cols] = dk[h]
            dv_ref[:, cols] = dv[h]
            dgz_ref[:, cols] = dgz[h].astype(dgz_ref.dtype)
            dsm = dsm + jnp.where(lane == gb + h, db[h], 0.0) + jnp.where(lane == ga + h, da[h], 0.0)
            dal_ref[h] += dal[h]
            ddt_ref[h] += ddt[h]
            ds_ref[h] = ds[h]
        dsm_ref[...] = dsm
        dnw_ref[...] += dnw

        if comm:
            @pl.when(pl.program_id(0) == nc - 1)
            def _():
                _comm_plan(comm[0], c_in, c_out, *sems)[1]()

    hw = h_ * LANE
    r = lambda c: nc - 1 - c
    blk = lambda name: pl.BlockSpec((CHUNK, hw), lambda c: (r(c), _MINE[name] // hw))
    one = pl.BlockSpec((h_, 1, 1), lambda c: (0, 0, 0))
    oblk = pl.BlockSpec((CHUNK, hw), lambda c: (r(c), 0))
    big = jax.ShapeDtypeStruct((length, hw), F32)
    hbm = pl.BlockSpec(memory_space=pl.ANY)
    return pl.pallas_call(
        body, name="gdn_bwd_" + comm[0] if comm else "gdn_bwd", grid=(nc,),
        in_specs=[blk("gq"), blk("gk"), blk("gv"), blk("gz"),
                  pl.BlockSpec((CHUNK, LANE), lambda c: (r(c), SMALL_OFF // LANE)), one, one,
                  pl.BlockSpec((1, LANE), lambda c: (0, 0)),
                  pl.BlockSpec((h_, None, LANE, LANE), lambda c: (0, r(c), 0, 0)),
                  oblk] + [hbm] * n_comm,
        out_specs=[oblk, oblk, oblk, oblk, pl.BlockSpec((CHUNK, LANE), lambda c: (r(c), 0)), one, one,
                   pl.BlockSpec((1, LANE), lambda c: (0, 0))] + [hbm] * n_comm,
        out_shape=[big, big, big, jax.ShapeDtypeStruct((length, hw), BF16),
                   jax.ShapeDtypeStruct((length, LANE), F32),
                   jax.ShapeDtypeStruct((h_, 1, 1), F32), jax.ShapeDtypeStruct((h_, 1, 1), F32),
                   jax.ShapeDtypeStruct((1, LANE), F32)] + (_comm_out_shapes(*comm) if comm else []),
        scratch_shapes=[pltpu.VMEM((h_, LANE, LANE), F32)] + (_comm_scratch(n_comm) if comm else []),
        compiler_params=_cparams(("arbitrary",), VMEM_LIMIT),
    )(xc, xc, xc, proj, proj, alog, dtb, nw, st, dmix, *(comm[1] if comm else []))


def _lane_col(block, lane):
    pick = lax.broadcasted_iota(jnp.int32, (1, block.shape[1]), 1) == lane
    return jnp.sum(jnp.where(pick, block, 0.0), axis=1, keepdims=True)


def _ssd_head_lane(g, p, j):
    return (_MINE["sdt"] - SMALL_OFF) + (g * PAIRS_PER_GROUP + p) * 2 + j


def _ssd_fwd(xc, proj, alog, dtb, dpar, length):
    nc = length // CHUNK
    ppg = PAIRS_PER_GROUP

    gw = ppg * LANE

    def body(x_ref, b_ref, c_ref, sm_ref, al_ref, db_ref, dp_ref, y_ref, st_ref, hs_ref):
        g = pl.program_id(0)

        @pl.when(pl.program_id(1) == 0)
        def _():
            hs_ref[...] = jnp.zeros_like(hs_ref)

        bm, cm, sm = b_ref[...], c_ref[...], sm_ref[...]
        st_ref[...] = hs_ref[...]
        two = lambda ref: [[_lane_col(ref[p], j) for j in range(2)] for p in range(ppg)]
        dtr = [[_lane_col(sm, _ssd_head_lane(g, p, j)) for j in range(2)] for p in range(ppg)]
        y, hs_new = _ssd_chunk([x_ref[:, p * LANE:(p + 1) * LANE] for p in range(ppg)], bm, cm, dtr, two(al_ref),
                               two(db_ref), two(dp_ref), [hs_ref[p] for p in range(ppg)])
        for p in range(ppg):
            y_ref[:, p * LANE:(p + 1) * LANE] = y[p]
            hs_ref[p] = hs_new[p]

    two = pl.BlockSpec((None, ppg, 1, 2), lambda g, c: (g, 0, 0, 0))
    return pl.pallas_call(
        body, name="ssd_fwd", grid=(SSD_GROUPS, nc),
        in_specs=[pl.BlockSpec((CHUNK, gw), lambda g, c: (c, _MINE["sx"] // gw + g)),
                  pl.BlockSpec((CHUNK, LANE), lambda g, c: (c, CB("sB") + g)),
                  pl.BlockSpec((CHUNK, LANE), lambda g, c: (c, CB("sC") + g)),
                  pl.BlockSpec((CHUNK, LANE), lambda g, c: (c, SMALL_OFF // LANE)),
                  two, two, two],
        out_specs=[pl.BlockSpec((CHUNK, gw), lambda g, c: (c, g)),
                   pl.BlockSpec((None, ppg, None, LANE, SSD_N), lambda g, c: (g, 0, c, 0, 0))],
        out_shape=[jax.ShapeDtypeStruct((length, SSD_GROUPS * gw), F32),
                   jax.ShapeDtypeStruct((SSD_GROUPS, ppg, nc, LANE, SSD_N), F32)],
        scratch_shapes=[pltpu.VMEM((ppg, LANE, SSD_N), F32)],
        compiler_params=_cparams(("parallel", "arbitrary"), VMEM_LIMIT),
    )(xc, xc, xc, proj, alog, dtb, dpar)


def _ssd_bwd(xc, proj, alog, dtb, dpar, st, dy, length):
    nc = length // CHUNK
    ppg = PAIRS_PER_GROUP

    gw = ppg * LANE

    def body(x_ref, b_ref, c_ref, sm_ref, al_ref, db_ref, dp_ref, st_ref, dy_ref,
             dx_ref, dbm_ref, dcm_ref, dsm_ref, dal_ref, ddb_ref, ddp_ref, dhs_ref):
        g = pl.program_id(0)

        @pl.when(pl.program_id(1) == 0)
        def _():
            dhs_ref[...] = jnp.zeros_like(dhs_ref)
            dal_ref[...] = jnp.zeros_like(dal_ref)
            ddb_ref[...] = jnp.zeros_like(ddb_ref)
            ddp_ref[...] = jnp.zeros_like(ddp_ref)

        bm, cm, sm = b_ref[...], c_ref[...], sm_ref[...]
        lane = lax.broadcasted_iota(jnp.int32, (1, LANE), 1)
        first = lax.broadcasted_iota(jnp.int32, (1, 2), 1) == 0
        pairs = range(ppg)
        two = lambda ref: [[_lane_col(ref[p], j) for j in range(2)] for p in pairs]
        dtr = [[_lane_col(sm, _ssd_head_lane(g, p, j)) for j in range(2)] for p in pairs]
        cols = lambda ref: [ref[:, p * LANE:(p + 1) * LANE] for p in pairs]
        ins = (cols(x_ref), bm, cm, dtr, two(al_ref), two(db_ref), two(dp_ref), [st_ref[p] for p in pairs])
        _, vjp = jax.vjp(_ssd_chunk, *ins)
        dx, dbm, dcm, ddt, dal, ddb, ddp, dhs = vjp((cols(dy_ref), [dhs_ref[p] for p in pairs]))
        dsm = jnp.zeros_like(sm)
        for p in pairs:
            dx_ref[:, p * LANE:(p + 1) * LANE] = dx[p]
            for j in range(2):
                dsm = dsm + jnp.where(lane == _ssd_head_lane(g, p, j), ddt[p][j], 0.0)
            dhs_ref[p] = dhs[p]
            dal_ref[p] += jnp.where(first, dal[p][0], dal[p][1])
            ddb_ref[p] += jnp.where(first, ddb[p][0], ddb[p][1])
            ddp_ref[p] += jnp.where(first, ddp[p][0], ddp[p][1])
        dbm_ref[...] = dbm
        dcm_ref[...] = dcm
        dsm_ref[...] = dsm

    r = lambda c: nc - 1 - c
    two = pl.BlockSpec((None, ppg, 1, 2), lambda g, c: (g, 0, 0, 0))
    grp = pl.BlockSpec((CHUNK, LANE), lambda g, c: (r(c), g))
    prm = jax.ShapeDtypeStruct((SSD_GROUPS, ppg, 1, 2), F32)
    return pl.pallas_call(
        body, name="ssd_bwd", grid=(SSD_GROUPS, nc),
        in_specs=[pl.BlockSpec((CHUNK, gw), lambda g, c: (r(c), _MINE["sx"] // gw + g)),
                  pl.BlockSpec((CHUNK, LANE), lambda g, c: (r(c), CB("sB") + g)),
                  pl.BlockSpec((CHUNK, LANE), lambda g, c: (r(c), CB("sC") + g)),
                  pl.BlockSpec((CHUNK, LANE), lambda g, c: (r(c), SMALL_OFF // LANE)),
                  two, two, two,
                  pl.BlockSpec((None, ppg, None, LANE, SSD_N), lambda g, c: (g, 0, r(c), 0, 0)),
                  pl.BlockSpec((CHUNK, gw), lambda g, c: (r(c), g))],
        out_specs=[pl.BlockSpec((CHUNK, gw), lambda g, c: (r(c), g)), grp, grp,
                   pl.BlockSpec((None, CHUNK, LANE), lambda g, c: (g, r(c), 0)), two, two, two],
        out_shape=[jax.ShapeDtypeStruct((length, SSD_GROUPS * gw), F32),
                   jax.ShapeDtypeStruct((length, SSD_GROUPS * SSD_N), F32),
                   jax.ShapeDtypeStruct((length, SSD_GROUPS * SSD_N), F32),
                   jax.ShapeDtypeStruct((SSD_GROUPS, length, LANE), F32), prm, prm, prm],
        scratch_shapes=[pltpu.VMEM((ppg, LANE, SSD_N), F32)],
        compiler_params=_cparams(("parallel", "arbitrary"), VMEM_LIMIT),
    )(xc, xc, xc, proj, alog, dtb, dpar, st, dy)


def _ret_fwd(proj, cos2, sin2, dmat, qdec, kdec, cdec, nw, length):
    nc = length // CHUNK
    h_ = RET_HEADS

    def body(q_ref, k_ref, v_ref, g_ref, cos_ref, sin_ref, dm_ref, qd_ref, kd_ref, cd_ref, nw_ref,
             on_ref, st_ref, r_ref):
        @pl.when(pl.program_id(0) == 0)
        def _():
            r_ref[...] = jnp.zeros_like(r_ref)

        st_ref[...] = r_ref[...]
        heads = lambda ref: [ref[:, h * LANE:(h + 1) * LANE] for h in range(h_)]
        lead = lambda ref: [ref[h] for h in range(h_)]
        on, r_new = _ret_chunk(heads(q_ref), heads(k_ref), heads(v_ref), heads(g_ref), cos_ref[...], sin_ref[...],
                               lead(dm_ref), lead(qd_ref), lead(kd_ref), lead(cd_ref), nw_ref[...], lead(r_ref))
        for h in range(h_):
            on_ref[:, h * LANE:(h + 1) * LANE] = on[h].astype(on_ref.dtype)
            r_ref[h] = r_new[h]

    hw = h_ * LANE
    blk = lambda name: pl.BlockSpec((CHUNK, hw), lambda c: (c, _MINE[name] // hw))
    tab = pl.BlockSpec((CHUNK, LANE), lambda c: (c, 0))
    full = lambda *s: pl.BlockSpec(s, lambda c: (0,) * len(s))
    return pl.pallas_call(
        body, name="ret_fwd", grid=(nc,),
        in_specs=[blk("rq"), blk("rk"), blk("rv"), blk("rg"), tab, tab,
                  full(h_, CHUNK, CHUNK), full(h_, CHUNK, 1), full(h_, CHUNK, 1), full(h_, 1, 1), full(1, LANE)],
        out_specs=[pl.BlockSpec((CHUNK, hw), lambda c: (c, 0)),
                   pl.BlockSpec((h_, None, LANE, LANE), lambda c: (0, c, 0, 0))],
        out_shape=[jax.ShapeDtypeStruct((length, hw), BF16),
                   jax.ShapeDtypeStruct((h_, nc, LANE, LANE), F32)],
        scratch_shapes=[pltpu.VMEM((h_, LANE, LANE), F32)],
        compiler_params=_cparams(("arbitrary",), VMEM_LIMIT),
    )(proj, proj, proj, proj, cos2, sin2, dmat, qdec, kdec, cdec, nw)


def _ret_bwd(proj, cos2, sin2, dmat, qdec, kdec, cdec, nw, st, dmix, length):
    nc = length // CHUNK
    h_ = RET_HEADS

    def body(q_ref, k_ref, v_ref, g_ref, cos_ref, sin_ref, dm_ref, qd_ref, kd_ref, cd_ref, nw_ref, st_ref,
             do_ref, dq_ref, dk_ref, dv_ref, dg_ref, dnw_ref, dr_ref):
        @pl.when(pl.program_id(0) == 0)
        def _():
            dr_ref[...] = jnp.zeros_like(dr_ref)
            dnw_ref[...] = jnp.zeros_like(dnw_ref)

        heads = lambda ref: [ref[:, h * LANE:(h + 1) * LANE] for h in range(h_)]
        lead = lambda ref: [ref[h] for h in range(h_)]
        consts = (cos_ref[...], sin_ref[...], lead(dm_ref), lead(qd_ref), lead(kd_ref), lead(cd_ref))
        f = lambda q, k, v, g, w_, r_: _ret_chunk(q, k, v, g, *consts, w_, r_)
        _, vjp = jax.vjp(f, heads(q_ref), heads(k_ref), heads(v_ref), heads(g_ref), nw_ref[...], lead(st_ref))
        dq, dk, dv, dg, dnw, dr = vjp((heads(do_ref), lead(dr_ref)))
        for h in range(h_):
            cols = slice(h * LANE, (h + 1) * LANE)
            dq_ref[:, cols] = dq[h].astype(dq_ref.dtype)
            dk_ref[:, cols] = dk[h].astype(dk_ref.dtype)
            dv_ref[:, cols] = dv[h].astype(dv_ref.dtype)
            dg_ref[:, cols] = dg[h].astype(dg_ref.dtype)
            dr_ref[h] = dr[h]
        dnw_ref[...] += dnw

    hw = h_ * LANE
    r = lambda c: nc - 1 - c
    blk = lambda name: pl.BlockSpec((CHUNK, hw), lambda c: (r(c), _MINE[name] // hw))
    tab = pl.BlockSpec((CHUNK, LANE), lambda c: (r(c), 0))
    full = lambda *s: pl.BlockSpec(s, lambda c: (0,) * len(s))
    oblk = pl.BlockSpec((CHUNK, hw), lambda c: (r(c), 0))
    big = jax.ShapeDtypeStruct((length, hw), BF16)
    return pl.pallas_call(
        body, name="ret_bwd", grid=(nc,),
        in_specs=[blk("rq"), blk("rk"), blk("rv"), blk("rg"), tab, tab,
                  full(h_, CHUNK, CHUNK), full(h_, CHUNK, 1), full(h_, CHUNK, 1), full(h_, 1, 1), full(1, LANE),
                  pl.BlockSpec((h_, None, LANE, LANE), lambda c: (0, r(c), 0, 0)),
                  pl.BlockSpec((CHUNK, hw), lambda c: (r(c), (MIX_W - hw) // hw))],
        out_specs=[oblk, oblk, oblk, oblk, full(1, LANE)],
        out_shape=[big, big, big, big, jax.ShapeDtypeStruct((1, LANE), F32)],
        scratch_shapes=[pltpu.VMEM((h_, LANE, LANE), F32)],
        compiler_params=_cparams(("arbitrary",), VMEM_LIMIT),
    )(proj, proj, proj, proj, cos2, sin2, dmat, qdec, kdec, cdec, nw, st, dmix)


def _mesh_pos():
    x, y, c = lax.axis_index("x"), lax.axis_index("y"), lax.axis_index("c")
    return x, y, c, 4 * x + 2 * y + c


def _peer(x, y, c, mask):
    return (x ^ ((mask >> 2) & 1), y ^ ((mask >> 1) & 1), c ^ (mask & 1))


def _comm_plan(kind, ins, outs, send_sems, recv_sems, local_sems):
    x, y, c, me = _mesh_pos()
    n = len(ins)
    src = (lambda a, idx: ins[a]) if kind == "gather" else (lambda a, idx: ins[a].at[idx])
    local = [pltpu.make_async_copy(src(a, me), outs[a].at[me], local_sems.at[a]) for a in range(n)]
    sends, recvs = [], []
    for mask in range(1, N_DEV):
        px, py, pc = _peer(x, y, c, mask)
        pidx = 4 * px + 2 * py + pc
        for a in range(n):
            k = (mask - 1) * n + a
            copy = lambda s, d, k=k: pltpu.make_async_remote_copy(
                src_ref=s, dst_ref=d, send_sem=send_sems.at[k], recv_sem=recv_sems.at[k],
                device_id=(px, py, pc), device_id_type=pl.DeviceIdType.MESH)
            sends.append(copy(src(a, pidx), outs[a].at[me]))
            recvs.append(copy(src(a, me), outs[a].at[pidx]))

    def start():
        for cp in local + sends:
            cp.start()

    def wait():
        for cp in sends:
            cp.wait_send()
        for cp in recvs:
            cp.wait_recv()
        for cp in local:
            cp.wait()

    return start, wait


def _comm_out_shapes(kind, arrays):
    return [jax.ShapeDtypeStruct(((N_DEV,) + s.shape) if kind == "gather" else s.shape, s.dtype) for s in arrays]


def _comm_scratch(n):
    k_tot = (N_DEV - 1) * n
    return [pltpu.SemaphoreType.DMA((k_tot,)), pltpu.SemaphoreType.DMA((k_tot,)), pltpu.SemaphoreType.DMA((n,))]


def _comm_call(name, kind, arrays):
    n = len(arrays)

    def body(*refs):
        start, wait = _comm_plan(kind, refs[:n], refs[n:2 * n], *refs[2 * n:])
        start()
        wait()

    hbm = pl.BlockSpec(memory_space=pl.ANY)
    return pl.pallas_call(
        body, name=name, in_specs=[hbm] * n, out_specs=[hbm] * n, out_shape=_comm_out_shapes(kind, arrays),
        scratch_shapes=_comm_scratch(n), compiler_params=pltpu.CompilerParams(has_side_effects=True),
    )(*arrays)


def _adam(name, parts, w, m, v, tr):
    p_, r_, cw = parts.shape
    c1 = 1.0 - ADAM_B1 ** ADAM_STEP
    c2 = 1.0 - ADAM_B2 ** ADAM_STEP

    def body(p_ref, w_ref, m_ref, v_ref, g_ref, d_ref, nm_ref, nv_ref):
        g = p_ref[0].astype(F32)
        for i in range(1, p_):
            g = g + p_ref[i].astype(F32)
        nm = ADAM_B1 * m_ref[...] + (1.0 - ADAM_B1) * g
        nv = ADAM_B2 * v_ref[...] + (1.0 - ADAM_B2) * (g * g)
        d_ref[...] = -ADAM_LR * ((nm / c1) / (jnp.sqrt(nv / c2) + ADAM_EPS) + ADAM_WD * w_ref[...])
        g_ref[...] = g
        nm_ref[...] = nm
        nv_ref[...] = nv

    spec = pl.BlockSpec((tr, cw), lambda i: (i, 0))
    shp = jax.ShapeDtypeStruct((r_, cw), F32)
    return pl.pallas_call(
        body, name=name, grid=(r_ // tr,),
        in_specs=[pl.BlockSpec((p_, tr, cw), lambda i: (0, i, 0)), spec, spec, spec],
        out_specs=[spec] * 4, out_shape=[shp] * 4,
        compiler_params=_cparams(("parallel",), VMEM_LIMIT),
    )(parts, w, m, v)


def _perm_cols(w_full):
    parts = [w_full[:, _ORIG[n][0]:_ORIG[n][0] + _ORIG[n][1]] for n in _ORDER]
    parts.append(jnp.zeros((w_full.shape[0], N_PAD - N_USED), w_full.dtype))
    return jnp.concatenate(parts, axis=1)


def _unperm_cols(g):
    by_orig = sorted(_ORDER, key=lambda n: _ORIG[n][0])
    return jnp.concatenate([g[:, _MINE[n]:_MINE[n] + _ORIG[n][1]] for n in by_orig], axis=1)


def _ret_consts(length):
    lg = jnp.log(1.0 - 2.0 ** (-5.0 - jnp.arange(RET_HEADS, dtype=F32)))
    idx = jnp.arange(CHUNK, dtype=F32)
    rel = idx[:, None] - idx[None, :]
    dmat = jnp.where(rel[None] >= 0, jnp.exp(jnp.maximum(rel, 0.0)[None] * lg[:, None, None]), 0.0)
    qdec = jnp.exp((idx[:, None] + 1.0) * lg[None, :]).T[:, :, None]
    kdec = jnp.exp((CHUNK - 1.0 - idx)[:, None] * lg[None, :]).T[:, :, None]
    cdec = jnp.exp(CHUNK * lg)[:, None, None]
    half = RET_DK // 2
    inv = ROPE_BASE ** (-jnp.arange(half, dtype=F32) / half)
    ang = jnp.arange(length, dtype=jnp.int32).astype(F32)[:, None] * inv[None, :]
    cos, sin = jnp.cos(ang), jnp.sin(ang)
    return (jnp.concatenate([cos, cos], axis=1), jnp.concatenate([-sin, sin], axis=1), dmat, qdec, kdec, cdec)


def _tiles(length):
    tm = 512 if length % 512 == 0 else length
    return tm


def _layer_fwd(x, p, consts, length, last, target, comm=None):
    tm = _tiles(length)
    (h,), _ = _rowmap("pre_norm", lambda x_, w_: ((_rms(x_, w_),), ()), length, tm,
                      [(x, D_MODEL, 0)], [p["pre_norm"]], out_ws=[D_MODEL], out_dtypes=[BF16])
    proj = _matmul("in_proj", h, p["w_in"], "nn", length, N_PAD, D_MODEL)
    xc = _conv_fwd(proj, p["conv_w"], p["conv_b"], length, tm, 512)
    o_a, st_a, *got = _gdn_fwd(xc, proj, p["gdn_A_log"], p["gdn_dt_bias"], p["gdn_norm"], length, comm)
    y_b, st_b = _ssd_fwd(xc, proj, p["ssd_A_log"], p["ssd_dt_bias"], p["ssd_D"], length)
    o_b = []
    for g in range(SSD_GROUPS):
        (o,), _ = _rowmap(f"ssd_gate{g}", lambda y_, z_, w_: ((_rms(y_ * _silu(z_), w_),), ()), length, tm,
                          [(y_b, 512, g), (proj, 512, _MINE["sz"] // 512 + g)], [p["ssd_norm"][g]], out_ws=[512],
                          out_dtypes=[BF16])
        o_b.append(o)
    o_c, st_c = _ret_fwd(proj, *consts, p["ret_norm"], length)
    mixed = jnp.concatenate([o.astype(BF16) for o in [o_a] + o_b + [o_c]], axis=1)
    out = _matmul("out_proj", mixed, p["w_out"], "nn", length, D_MODEL, MIX_W)
    res = dict(x=x, h=h, proj=proj, xc=xc, st_a=st_a, st_b=st_b, st_c=st_c, y_b=y_b, mixed=mixed, out=out)
    if not last:
        (y,), _ = _rowmap("post_norm", lambda x_, o_, w_: ((x_ + _rms(o_, w_),), ()), length, tm,
                          [(x, D_MODEL, 0), (out, D_MODEL, 0)], [p["post_norm"]], out_ws=[D_MODEL])
        return y, res, None, got

    def head(x_, o_, t_, w_):
        e = x_ + _rms(o_, w_) - t_
        row = jnp.mean(e * e, axis=-1, keepdims=True)
        loss = 0.5 * jnp.sum(row, axis=0, keepdims=True)
        return (e * (1.0 / D_MODEL),), (loss + jnp.zeros((1, LANE), F32),)

    (dy,), (loss,) = _rowmap("loss_head", head, length, tm,
                             [(x, D_MODEL, 0), (out, D_MODEL, 0), (target, D_MODEL, 0)], [p["post_norm"]],
                             out_ws=[D_MODEL], acc_ws=[LANE])
    return None, res, (dy, loss[0, 0]), got


def _layer_bwd(dy, p, res, consts, length, comm=None):
    tm = _tiles(length)
    g = {}
    (d_out,), (g["post_norm"],) = _rowmap("post_norm_bwd", lambda o_, w_: ((_rms(o_, w_),), ()), length, tm,
                                          [(res["out"], D_MODEL, 0)], [p["post_norm"]], cts=[(dy, D_MODEL, 0)])
    dmix = _matmul("out_proj_dx", d_out, p["w_out"], "nt", length, MIX_W, D_MODEL)
    g["w_out"] = _matmul("out_proj_dw", res["mixed"], d_out, "tn", MIX_W, D_MODEL, length)
    proj, xc = res["proj"], res["xc"]
    (dgq, dgk, dgv, dgz, dsm_a, g["gdn_A_log"], g["gdn_dt_bias"], g["gdn_norm"], *got) = _gdn_bwd(
        xc, proj, p["gdn_A_log"], p["gdn_dt_bias"], p["gdn_norm"], res["st_a"], dmix, length, comm)
    dyb, dsz, dsn = [], [], []
    for gi in range(SSD_GROUPS):
        (dy_g, dz_g), (dn_g,) = _rowmap(
            f"ssd_gate_bwd{gi}", lambda y_, z_, w_: ((_rms(y_ * _silu(z_), w_),), ()), length, tm,
            [(res["y_b"], 512, gi), (proj, 512, _MINE["sz"] // 512 + gi)], [p["ssd_norm"][gi]],
            cts=[(dmix, 512, 1 + gi)], d_dtypes=[F32, BF16])
        dyb.append(dy_g)
        dsz.append(dz_g)
        dsn.append(dn_g)
    g["ssd_norm"] = dsn
    dyb = jnp.concatenate(dyb, axis=1)
    dsx, dsb, dsc, dsm_b, g["ssd_A_log"], g["ssd_dt_bias"], g["ssd_D"] = _ssd_bwd(
        xc, proj, p["ssd_A_log"], p["ssd_dt_bias"], p["ssd_D"], res["st_b"], dyb, length)
    drq, drk, drv, drg, dnw_c = _ret_bwd(proj, *consts, p["ret_norm"], res["st_c"], dmix, length)
    g["ret_norm"] = dnw_c
    dxc = jnp.concatenate([dgq, dgk, dgv, dsx, dsb, dsc], axis=1)
    dpre, g["conv_w"], g["conv_b"] = _conv_bwd_pre(proj, p["conv_w"], p["conv_b"], dxc, length, tm, 512)
    dconv = _conv_bwd_x(dpre, p["conv_w"], length, tm, 512)
    dsmall = dsm_a + dsm_b[0] + dsm_b[1]
    dproj = jnp.concatenate([d.astype(BF16) for d in [dconv, dgz] + dsz + [drq, drk, drv, drg, dsmall]]
                            + [jnp.zeros((length, N_PAD - SMALL_OFF - LANE), BF16)], axis=1)
    dh = _matmul("in_proj_dx", dproj, p["w_in"], "nt", length, D_MODEL, N_PAD)
    g["w_in"] = _matmul("in_proj_dw", res["h"], dproj, "tn", D_MODEL, N_PAD, length)
    (dx,), (g["pre_norm"],) = _rowmap(
        "pre_norm_bwd", lambda x_, w_: ((_rms(x_, w_), x_), ()), length, tm,
        [(res["x"], D_MODEL, 0)], [p["pre_norm"]], cts=[(dh, D_MODEL, 0), (dy, D_MODEL, 0)])
    return dx, g, got


def _make_layer(l, w_in_full, w_out_full, conv_full, sw):
    return dict(
        pre_norm=sw["pre_norm"][l][None], post_norm=sw["post_norm"][l][None], w_in=_perm_cols(w_in_full),
        w_out=w_out_full,
        conv_w=jnp.concatenate([conv_full[0], conv_full[1]], axis=1),
        conv_b=jnp.concatenate([jnp.zeros((CONV_CH,), F32), sw["ssd_conv_b"][l]])[None],
        gdn_A_log=sw["gdn_A_log"][l][:, None, None], gdn_dt_bias=sw["gdn_dt_bias"][l][:, None, None],
        gdn_norm=sw["gdn_norm"][l][None],
        ssd_A_log=sw["ssd_A_log"][l].reshape(SSD_GROUPS, PAIRS_PER_GROUP, 1, 2),
        ssd_dt_bias=sw["ssd_dt_bias"][l].reshape(SSD_GROUPS, PAIRS_PER_GROUP, 1, 2),
        ssd_D=sw["ssd_D"][l].reshape(SSD_GROUPS, PAIRS_PER_GROUP, 1, 2),
        ssd_norm=[sw["ssd_norm"][l][None, 512 * g:512 * (g + 1)] for g in range(SSD_GROUPS)],
        ret_norm=sw["ret_norm"][l][None])


def _local_step(xb, tgt, layer0, make_layer1, length, fwd_comm=None, make_bwd_comm=None):
    consts = _ret_consts(length)
    y0, res0, _, got = _layer_fwd(xb, layer0, consts, length, False, tgt, fwd_comm)
    layer1 = make_layer1(got)
    _, res1, (dy, loss_local), _ = _layer_fwd(y0, layer1, consts, length, True, tgt)
    dy, grads1, _ = _layer_bwd(dy, layer1, res1, consts, length)
    dx, grads0, recv = _layer_bwd(dy, layer0, res0, consts, length, make_bwd_comm(grads1) if make_bwd_comm else None)
    return loss_local, dx, [grads0, grads1], recv


_SMALL = ["pre_norm", "post_norm", "gdn_A_log", "gdn_dt_bias", "gdn_norm", "ssd_conv_b", "ssd_A_log",
          "ssd_dt_bias", "ssd_D", "ssd_norm", "ret_norm"]


def _pack_small(arrs):
    rows = []
    for n in _SMALL:
        flat = arrs[n].reshape(-1)
        pad = (-flat.shape[0]) % LANE
        rows.append(jnp.pad(flat, (0, pad)).reshape(-1, LANE))
    out = jnp.concatenate(rows, axis=0)
    return jnp.pad(out, ((0, (-out.shape[0]) % SUBLANE), (0, 0)))


def _unpack_small(packed, like):
    out, r = {}, 0
    for n in _SMALL:
        cnt = like[n].size
        nrow = -(-cnt // LANE)
        out[n] = packed[r:r + nrow].reshape(-1)[:cnt].reshape(like[n].shape)
        r += nrow
    return out


def kernel(x, pre_norm, post_norm, w_in, gdn_conv, gdn_A_log, gdn_dt_bias, gdn_norm, ssd_conv, ssd_conv_b, ssd_A_log, ssd_dt_bias, ssd_D, ssd_norm, ret_norm, w_out, loss_target, m_pre_norm, m_post_norm, m_w_in, m_gdn_conv, m_gdn_A_log, m_gdn_dt_bias, m_gdn_norm, m_ssd_conv, m_ssd_conv_b, m_ssd_A_log, m_ssd_dt_bias, m_ssd_D, m_ssd_norm, m_ret_norm, m_w_out, v_pre_norm, v_post_norm, v_w_in, v_gdn_conv, v_gdn_A_log, v_gdn_dt_bias, v_gdn_norm, v_ssd_conv, v_ssd_conv_b, v_ssd_A_log, v_ssd_dt_bias, v_ssd_D, v_ssd_norm, v_ret_norm, v_w_out):
    length = x.shape[1]
    xb = x[0]
    tgt = loss_target[0]
    small_w = dict(pre_norm=pre_norm, post_norm=post_norm, gdn_A_log=gdn_A_log, gdn_dt_bias=gdn_dt_bias,
                   gdn_norm=gdn_norm, ssd_conv_b=ssd_conv_b, ssd_A_log=ssd_A_log, ssd_dt_bias=ssd_dt_bias,
                   ssd_D=ssd_D, ssd_norm=ssd_norm, ret_norm=ret_norm)
    small_m = dict(pre_norm=m_pre_norm, post_norm=m_post_norm, gdn_A_log=m_gdn_A_log, gdn_dt_bias=m_gdn_dt_bias,
                   gdn_norm=m_gdn_norm, ssd_conv_b=m_ssd_conv_b, ssd_A_log=m_ssd_A_log, ssd_dt_bias=m_ssd_dt_bias,
                   ssd_D=m_ssd_D, ssd_norm=m_ssd_norm, ret_norm=m_ret_norm)
    small_v = dict(pre_norm=v_pre_norm, post_norm=v_post_norm, gdn_A_log=v_gdn_A_log, gdn_dt_bias=v_gdn_dt_bias,
                   gdn_norm=v_gdn_norm, ssd_conv_b=v_ssd_conv_b, ssd_A_log=v_ssd_A_log, ssd_dt_bias=v_ssd_dt_bias,
                   ssd_D=v_ssd_D, ssd_norm=v_ssd_norm, ret_norm=v_ret_norm)

    w_in_b, w_out_b = w_in.astype(BF16), w_out.astype(BF16)
    conv_shard = jnp.stack([gdn_conv, ssd_conv], axis=1)
    shards = lambda l: [w_in_b[l], w_out_b[l], conv_shard[l]]

    def assemble(l, g_in, g_out, g_conv):
        w_in_full = g_in.transpose(1, 0, 2).reshape(D_MODEL, N_IN)
        w_out_full = g_out.reshape(MIX_W, D_MODEL)
        conv_full = g_conv.transpose(1, 2, 0, 3).reshape(2, CONV_W, CONV_CH)
        return _make_layer(l, w_in_full, w_out_full, conv_full, small_w)

    def grad_slabs(g):
        gi = _unperm_cols(g["w_in"]).reshape(D_MODEL, N_DEV, SHARD_IN).transpose(1, 0, 2).astype(BF16)
        go = g["w_out"].reshape(N_DEV, SHARD_OUT, D_MODEL).astype(BF16)
        gc = jnp.stack([g["conv_w"][:, k * CONV_CH:(k + 1) * CONV_CH].reshape(CONV_W, N_DEV, SHARD_CONV)
                        .transpose(1, 0, 2) for k in range(2)], axis=1).reshape(N_DEV, 2 * CONV_W, SHARD_CONV)
        return [gi, go, gc]

    layer0 = assemble(0, *_comm_call("gather_layer0", "gather", shards(0)))
    loss_local, dx, grads, recv1 = _local_step(
        xb, tgt, layer0, lambda got: assemble(1, *got), length,
        fwd_comm=("gather", shards(1)), make_bwd_comm=lambda g1: ("scatter", grad_slabs(g1)))
    grad_x = dx[None]
    loss = lax.psum(loss_local, ("x", "y", "c"))

    small_g = dict(
        pre_norm=jnp.concatenate([grads[l]["pre_norm"] for l in range(DEPTH)], axis=0),
        post_norm=jnp.concatenate([grads[l]["post_norm"] for l in range(DEPTH)], axis=0),
        gdn_A_log=jnp.stack([grads[l]["gdn_A_log"][:, 0, 0] for l in range(DEPTH)]),
        gdn_dt_bias=jnp.stack([grads[l]["gdn_dt_bias"][:, 0, 0] for l in range(DEPTH)]),
        gdn_norm=jnp.concatenate([grads[l]["gdn_norm"] for l in range(DEPTH)], axis=0),
        ssd_conv_b=jnp.concatenate([grads[l]["conv_b"][:, CONV_CH:] for l in range(DEPTH)], axis=0),
        ssd_A_log=jnp.stack([grads[l]["ssd_A_log"].reshape(SSD_HEADS) for l in range(DEPTH)]),
        ssd_dt_bias=jnp.stack([grads[l]["ssd_dt_bias"].reshape(SSD_HEADS) for l in range(DEPTH)]),
        ssd_D=jnp.stack([grads[l]["ssd_D"].reshape(SSD_HEADS) for l in range(DEPTH)]),
        ssd_norm=jnp.concatenate([jnp.concatenate(grads[l]["ssd_norm"], axis=1) for l in range(DEPTH)], axis=0),
        ret_norm=jnp.concatenate([grads[l]["ret_norm"] for l in range(DEPTH)], axis=0))
    gs = _pack_small(small_g)
    gs8 = jnp.broadcast_to(gs[None], (N_DEV,) + gs.shape)
    *recv0, r_small = _comm_call("exchange_layer0", "scatter", grad_slabs(grads[0]) + [gs8])
    recv = [recv0, recv1]

    conv_w = lambda g_, s_, l: jnp.stack([g_[l], s_[l]], axis=0).reshape(2 * CONV_W, SHARD_CONV)
    o_in, o_out, o_conv = [], [], []
    for l in range(DEPTH):
        o_in.append(_adam(f"adam_w_in{l}", recv[l][0], w_in[l], m_w_in[l], v_w_in[l], 128))
        o_out.append(_adam(f"adam_w_out{l}", recv[l][1], w_out[l], m_w_out[l], v_w_out[l], 128))
        o_conv.append(_adam(f"adam_conv{l}", recv[l][2], conv_w(gdn_conv, ssd_conv, l),
                            conv_w(m_gdn_conv, m_ssd_conv, l), conv_w(v_gdn_conv, v_ssd_conv, l), 2 * CONV_W))
    ps_w, ps_m, ps_v = _pack_small(small_w), _pack_small(small_m), _pack_small(small_v)
    o_small = _adam("adam_small", r_small, ps_w, ps_m, ps_v, ps_w.shape[0])

    names = ["pre_norm", "post_norm", "w_in", "gdn_conv", "gdn_A_log", "gdn_dt_bias", "gdn_norm", "ssd_conv",
             "ssd_conv_b", "ssd_A_log", "ssd_dt_bias", "ssd_D", "ssd_norm", "ret_norm", "w_out"]
    outs = []
    for kind in range(4):
        d = _unpack_small(o_small[kind], small_w)
        cv = jnp.stack([o_conv[l][kind].reshape(2, CONV_W, SHARD_CONV) for l in range(DEPTH)], axis=1)
        d["w_in"] = jnp.stack([o_in[l][kind] for l in range(DEPTH)], axis=0)
        d["w_out"] = jnp.stack([o_out[l][kind] for l in range(DEPTH)], axis=0)
        d["gdn_conv"] = cv[0]
        d["ssd_conv"] = cv[1]
        outs.extend(d[n] for n in names)
    return (loss, grad_x, *outs)
```

```python
import functools
import math

import numpy as np
import jax
import jax.numpy as jnp
from jax import lax
from jax.experimental import pallas as pl
from jax.experimental.pallas import tpu as pltpu

F32 = jnp.float32
BF16 = jnp.bfloat16

D_MODEL = 1024
DEPTH = 2
CHUNK = 64
CONV_W = 4
EPS = 1e-6
N_DEV = 8

GDN_HEADS = 4
GDN_DK = 128
SSD_HEADS = 16
SSD_P = 64
SSD_N = 128
SSD_GROUPS = 2
SSD_PAIRS = SSD_HEADS // 2
PAIRS_PER_GROUP = SSD_PAIRS // SSD_GROUPS
RET_HEADS = 4
RET_DK = 128
ROPE_BASE = 10000.0
MIX_W = 2048
N_IN = 6680
SHARD_IN = N_IN // N_DEV
SHARD_OUT = MIX_W // N_DEV
CONV_CH = 1536
SHARD_CONV = CONV_CH // N_DEV

ADAM_LR = 0.001
ADAM_B1 = 0.9
ADAM_B2 = 0.999
ADAM_EPS = 1e-08
ADAM_WD = 0.01
ADAM_STEP = 10

LANE = 128
SUBLANE = 8
VMEM_LIMIT = 56 * 1024 * 1024

_ORIG = dict(gq=(0, 512), gk=(512, 512), gv=(1024, 512), gz=(1536, 512), gb=(2048, 4), ga=(2052, 4),
             sx=(2056, 1024), sB=(3080, 256), sC=(3336, 256), sz=(3592, 1024), sdt=(4616, 16),
             rq=(4632, 512), rk=(5144, 512), rv=(5656, 512), rg=(6168, 512))
_ORDER = ["gq", "gk", "gv", "sx", "sB", "sC", "gz", "sz", "rq", "rk", "rv", "rg", "gb", "ga", "sdt"]
_MINE = {}
_off = 0
for _n in _ORDER:
    _MINE[_n] = _off
    _off += _ORIG[_n][1]
N_USED = _off
N_PAD = 7168
CONV_ALL = 2 * CONV_CH
SMALL_OFF = _MINE["gb"]
CB = lambda name: _MINE[name] // LANE


def _cparams(sem, vmem=None):
    return pltpu.CompilerParams(dimension_semantics=sem, vmem_limit_bytes=vmem)


def _split_bf16(a):
    hi = a.astype(BF16)
    return hi, (a - hi.astype(F32)).astype(BF16)


def _make_mm():
    def raw(a, b, ca, cb):
        return lax.dot_general(a.astype(BF16), b.astype(BF16), (((ca,), (cb,)), ((), ())),
                               preferred_element_type=F32)

    @jax.custom_vjp
    def nn(a, b):
        return raw(a, b, 1, 0)

    @jax.custom_vjp
    def nt(a, b):
        return raw(a, b, 1, 1)

    @jax.custom_vjp
    def tn(a, b):
        return raw(a, b, 0, 0)

    nn.defvjp(lambda a, b: (raw(a, b, 1, 0), (a, b)), lambda r, g: (nt(g, r[1]), tn(r[0], g)))
    nt.defvjp(lambda a, b: (raw(a, b, 1, 1), (a, b)), lambda r, g: (nn(g, r[1]), tn(g, r[0])))
    tn.defvjp(lambda a, b: (raw(a, b, 0, 0), (a, b)), lambda r, g: (nt(r[1], g), nn(r[0], g)))
    return nn, nt, tn


_nn, _nt, _tn = _make_mm()


@jax.custom_vjp
def _swap_halves(t):
    return pltpu.roll(t, LANE // 2, 1)


_swap_halves.defvjp(lambda t: (pltpu.roll(t, LANE // 2, 1), None),
                    lambda _, g: (pltpu.roll(g, LANE // 2, 1),))


@jax.custom_vjp
def _split_rows(x):
    return tuple(x[i * CHUNK:(i + 1) * CHUNK] for i in range(x.shape[0] // CHUNK))


_split_rows.defvjp(lambda x: (tuple(x[i * CHUNK:(i + 1) * CHUNK] for i in range(x.shape[0] // CHUNK)), None),
                   lambda _, gs: (jnp.concatenate(gs, axis=0),))


def _dot3(a, b, ca, cb):
    dot = lambda x, y: lax.dot_general(x, y, (((ca,), (cb,)), ((), ())), preferred_element_type=F32)
    return dot(a[0], b[0]) + (dot(a[0], b[1]) + dot(a[1], b[0]))


@jax.custom_vjp
def _tri_inv(mats):
    return _tri_inv_impl(mats)


def _tri_inv_impl(mats):
    ii = lax.broadcasted_iota(jnp.int32, mats[0].shape, 0)
    jj = lax.broadcasted_iota(jnp.int32, mats[0].shape, 1)
    eye = jnp.where(ii == jj, 1.0, 0.0).astype(F32)
    ts = [eye - a for a in mats]
    ps = [_split_bf16(-a) for a in mats]
    for _ in range(int(math.log2(CHUNK)) - 1):
        ps = [_split_bf16(_dot3(p, p, 1, 0)) for p in ps]
        ts = [t + _dot3(_split_bf16(t), p, 1, 0) for t, p in zip(ts, ps)]
    return ts


def _tri_inv_bwd(ts, gs):
    tsp = [_split_bf16(t) for t in ts]
    xs = [_dot3(t, _split_bf16(g), 0, 0) for t, g in zip(tsp, gs)]
    return ([-_dot3(_split_bf16(x), t, 1, 1) for x, t in zip(xs, tsp)],)


_tri_inv.defvjp(lambda mats: (lambda ts: (ts, ts))(_tri_inv_impl(mats)), _tri_inv_bwd)


def _silu(x):
    return x * jax.nn.sigmoid(x)


@jax.custom_vjp
def _softplus(x):
    return jnp.maximum(x, 0.0) + jnp.log1p(jnp.exp(-jnp.abs(x)))


_softplus.defvjp(lambda x: (jnp.maximum(x, 0.0) + jnp.log1p(jnp.exp(-jnp.abs(x))), x),
                 lambda x, g: (g * jax.nn.sigmoid(x),))


def _rms(x, w):
    return x * lax.rsqrt(jnp.mean(x * x, axis=-1, keepdims=True) + EPS) * w


def _chunk_masks(n):
    ii = lax.broadcasted_iota(jnp.int32, (n, n), 0)
    jj = lax.broadcasted_iota(jnp.int32, (n, n), 1)
    return ii >= jj, ii > jj, ii == jj, ii <= jj


def _cumsum_col(g, causal, eye, upper):
    g_row = jnp.sum(jnp.where(eye, g, 0.0), axis=0, keepdims=True)
    col = jnp.sum(jnp.where(causal, g_row, 0.0), axis=1, keepdims=True)
    row = jnp.sum(jnp.where(upper, g, 0.0), axis=0, keepdims=True)
    return col, row


def _gdn_chunk(q, k, v, gz, b_raw, a_raw, alog, dtb, nw, s):
    subs, nh = range(len(q)), len(q[0])
    c = q[0][0].shape[0]
    causal, _, eye, upper = _chunk_masks(c)
    stack = lambda xs: jnp.concatenate(xs, axis=0)
    per = lambda f: [[f(j, h) for h in range(nh)] for j in subs]
    qn = per(lambda j, h: q[j][h] * lax.rsqrt(jnp.sum(q[j][h] * q[j][h], axis=-1, keepdims=True) + EPS)
             * (GDN_DK ** -0.5))
    kn = per(lambda j, h: k[j][h] * lax.rsqrt(jnp.sum(k[j][h] * k[j][h], axis=-1, keepdims=True) + EPS))
    beta = per(lambda j, h: jax.nn.sigmoid(b_raw[j][h]))
    g = per(lambda j, h: -jnp.exp(alog[h]) * _softplus(a_raw[j][h] + dtb[h]))
    gcum = per(lambda j, h: _cumsum_col(g[j][h], causal, eye, upper)[0])
    eg = per(lambda j, h: jnp.exp(gcum[j][h]))
    glast = per(lambda j, h: jnp.sum(g[j][h], axis=0, keepdims=True))
    kb = per(lambda j, h: kn[j][h] * beta[j][h])
    n = nh * c
    ii = lax.broadcasted_iota(jnp.int32, (n, n), 0)
    jj = lax.broadcasted_iota(jnp.int32, (n, n), 1)
    sh = int(math.log2(c))
    same = lax.shift_right_logical(ii, sh) == lax.shift_right_logical(jj, sh)
    causal_bd = jnp.logical_and(same, ii >= jj)
    strict_bd = jnp.logical_and(same, ii > jj)
    gcum_all = [stack(gcum[j]) for j in subs]
    gcum_row = [jnp.sum(jnp.where(ii == jj, gcum_all[j], 0.0), axis=0, keepdims=True) for j in subs]
    decay = [jnp.where(causal_bd, jnp.exp(jnp.where(causal_bd, gcum_all[j] - gcum_row[j], 0.0)), 0.0) for j in subs]
    kn_all = [stack(kn[j]) for j in subs]
    a_low = [jnp.where(strict_bd, _nt(stack(kb[j]), kn_all[j]) * decay[j], 0.0) for j in subs]
    t = _tri_inv(a_low)
    u = [_nn(t[j], stack([v[j][h] * beta[j][h] for h in range(nh)])) for j in subs]
    w = [_split_rows(_nn(t[j], stack([kb[j][h] * eg[j][h] for h in range(nh)]))) for j in subs]
    attn = [_nt(stack(qn[j]), kn_all[j]) * decay[j] for j in subs]
    kdec = per(lambda j, h: kn[j][h] * jnp.exp(glast[j][h] - gcum[j][h]))
    on = []
    for j in subs:
        v_new_all = u[j] - stack([_nn(w[j][h], s[h]) for h in range(nh)])
        o = _split_rows(stack([_nn(qn[j][h] * eg[j][h], s[h]) for h in range(nh)]) + _nn(attn[j], v_new_all))
        v_new = _split_rows(v_new_all)
        s = [s[h] * jnp.exp(glast[j][h]) + _tn(kdec[j][h], v_new[h]) for h in range(nh)]
        on.append([_rms(o[h], nw) * _silu(gz[j][h]) for h in range(nh)])
    return on, s


def _ssd_chunk(x, bm, cm, dtr, alog, dtb, dp, hs):
    pairs = range(len(x))
    c = x[0].shape[0]
    lane_i = lax.broadcasted_iota(jnp.int32, (c, LANE), 1)
    lane_lo = lane_i < SSD_P
    lane_lo1 = lax.broadcasted_iota(jnp.int32, (1, LANE), 1) < SSD_P
    causal2 = lax.broadcasted_iota(jnp.int32, (c, LANE), 0) >= jnp.bitwise_and(lane_i, c - 1)
    row_lo = lax.broadcasted_iota(jnp.int32, (LANE, 1), 0) < SSD_P
    eye2 = lax.broadcasted_iota(jnp.int32, (LANE, LANE), 0) == lax.broadcasted_iota(jnp.int32, (LANE, LANE), 1)
    to_row = lambda cols: jnp.sum(jnp.where(eye2, jnp.concatenate(cols, axis=0), 0.0), axis=0, keepdims=True)
    dt = [[_softplus(dtr[p][h] + dtb[p][h]) for h in range(2)] for p in pairs]
    a = [[dt[p][h] * -jnp.exp(alog[p][h]) for h in range(2)] for p in pairs]
    a_row = [to_row(a[p]) for p in pairs]
    part = [jnp.where(causal2, a_row[p], 0.0) for p in pairs]
    acum = [[jnp.sum(jnp.where(lane_lo, part[p], 0.0), axis=1, keepdims=True),
             jnp.sum(jnp.where(lane_lo, 0.0, part[p]), axis=1, keepdims=True)] for p in pairs]
    acum_row = [to_row(acum[p]) for p in pairs]
    acum_col = [jnp.where(lane_lo, acum[p][0], acum[p][1]) for p in pairs]
    lmat = [jnp.where(causal2, jnp.exp(jnp.where(causal2, acum_col[p] - acum_row[p], 0.0)), 0.0) for p in pairs]
    alast = [[jnp.sum(a[p][h], axis=0, keepdims=True) for h in range(2)] for p in pairs]
    cb = _nt(cm, jnp.concatenate([bm, bm], axis=0))
    xdt = [x[p] * jnp.where(lane_lo, dt[p][0], dt[p][1]) for p in pairs]
    xdt_rows = [jnp.concatenate([jnp.where(lane_lo, xdt[p], 0.0), jnp.where(lane_lo, 0.0, xdt[p])], axis=0)
                for p in pairs]
    intra = [_nn(cb * lmat[p], xdt_rows[p]) for p in pairs]
    inter = [_nt(cm, hs[p]) * jnp.exp(acum_col[p]) for p in pairs]
    y = [x[p] * jnp.where(lane_lo1, dp[p][0], dp[p][1]) + intra[p] + inter[p] for p in pairs]
    wdec = [jnp.exp(jnp.where(lane_lo, alast[p][0] - acum[p][0], alast[p][1] - acum[p][1])) for p in pairs]
    scale = [jnp.exp(jnp.where(row_lo, alast[p][0], alast[p][1])) for p in pairs]
    upd = [_tn(xdt[p] * wdec[p], bm) for p in pairs]
    hs_new = [hs[p] * scale[p] + upd[p] for p in pairs]
    return y, hs_new


def _ret_chunk(rq, rk, rv, rg, cos2, sin2, dmat, qdec, kdec, cdec, nw, r):
    hs = range(len(rq))
    q = [rq[h] * cos2 + _swap_halves(rq[h]) * sin2 for h in hs]
    k = [(rk[h] * cos2 + _swap_halves(rk[h]) * sin2) * (RET_DK ** -0.5) for h in hs]
    s = [_nt(q[h], k[h]) * dmat[h] for h in hs]
    intra = [_nn(s[h], rv[h]) for h in hs]
    inter = [_nn(q[h], r[h]) * qdec[h] for h in hs]
    upd = [_tn(k[h] * kdec[h], rv[h]) for h in hs]
    r_new = [r[h] * cdec[h] + upd[h] for h in hs]
    on = [_rms(intra[h] + inter[h], nw) * _silu(rg[h]) for h in hs]
    return on, r_new


MM_TILE = 1024


def _tile(n, pref=MM_TILE):
    return pref if n % pref == 0 else n


def _matmul(name, a, b, mode, m, n, k, tk=None):
    tm, tn = _tile(m), _tile(n)
    tk = _tile(k) if tk is None else tk
    nk = k // tk
    ca, cb = {"nn": (1, 0), "nt": (1, 1), "tn": (0, 0)}[mode]

    def body(a_ref, b_ref, o_ref):
        part = lax.dot_general(a_ref[...].astype(BF16), b_ref[...].astype(BF16),
                               (((ca,), (cb,)), ((), ())), preferred_element_type=F32)
        if nk == 1:
            o_ref[...] = part
        else:
            kk = pl.program_id(2)

            @pl.when(kk == 0)
            def _():
                o_ref[...] = part

            @pl.when(kk > 0)
            def _():
                o_ref[...] += part

    a_spec = pl.BlockSpec((tk, tm), lambda i, j, kk: (kk, i)) if mode == "tn" else pl.BlockSpec((tm, tk), lambda i, j, kk: (i, kk))
    b_spec = pl.BlockSpec((tn, tk), lambda i, j, kk: (j, kk)) if mode == "nt" else pl.BlockSpec((tk, tn), lambda i, j, kk: (kk, j))
    return pl.pallas_call(
        body, name=name, grid=(m // tm, n // tn, nk),
        in_specs=[a_spec, b_spec], out_specs=pl.BlockSpec((tm, tn), lambda i, j, kk: (i, j)),
        out_shape=jax.ShapeDtypeStruct((m, n), F32),
        compiler_params=_cparams(("parallel", "parallel", "arbitrary"), VMEM_LIMIT),
    )(a, b)


def _rowmap(name, fn, length, tm, rows, params, out_ws=(), acc_ws=(), out_dtypes=None, cts=None, nd_rows=(),
            d_dtypes=None):
    nt_ = length // tm
    n_r, n_nd, n_p = len(rows), len(nd_rows), len(params)
    rspec = lambda bw, cbk: pl.BlockSpec((tm, bw), lambda i: (i, cbk))
    pspec = lambda w: pl.BlockSpec((1, w), lambda i: (0, 0))
    in_arrays = [r[0] for r in rows] + [r[0] for r in nd_rows] + list(params)
    in_specs = [rspec(r[1], r[2]) for r in rows] + [rspec(r[1], r[2]) for r in nd_rows] + [pspec(p.shape[1]) for p in params]

    def load(refs):
        return [r[...].astype(F32) for r in refs]

    if cts is None:
        n_o, n_a = len(out_ws), len(acc_ws)
        out_dtypes_ = out_dtypes or [F32] * n_o

        def body(*refs):
            ins = load(refs[:n_r + n_nd + n_p])
            outs, accs = fn(*ins)
            o_refs = refs[n_r + n_nd + n_p:]
            for o_ref, o in zip(o_refs[:n_o], outs):
                o_ref[...] = o.astype(o_ref.dtype)
            first = pl.program_id(0) == 0
            for a_ref, acc in zip(o_refs[n_o:], accs):
                @pl.when(first)
                def _(a_ref=a_ref):
                    a_ref[...] = jnp.zeros_like(a_ref)
                a_ref[...] += acc

        res = pl.pallas_call(
            body, name=name, grid=(nt_,), in_specs=in_specs,
            out_specs=[rspec(w, 0) for w in out_ws] + [pspec(w) for w in acc_ws],
            out_shape=[jax.ShapeDtypeStruct((length, w), dt) for w, dt in zip(out_ws, out_dtypes_)]
                      + [jax.ShapeDtypeStruct((1, w), F32) for w in acc_ws],
            compiler_params=_cparams(("arbitrary",), VMEM_LIMIT),
        )(*in_arrays)
        return res[:n_o], res[n_o:]

    n_c = len(cts)

    def body(*refs):
        ins = load(refs[:n_r + n_nd + n_p])
        ct_vals = load(refs[n_r + n_nd + n_p:n_r + n_nd + n_p + n_c])
        nd = ins[n_r:n_r + n_nd]
        f = lambda rs, ps: fn(*rs, *nd, *ps)[0]
        _, vjp = jax.vjp(f, ins[:n_r], ins[n_r + n_nd:])
        d_rows, d_params = vjp(tuple(ct_vals))
        o_refs = refs[n_r + n_nd + n_p + n_c:]
        for o_ref, d in zip(o_refs[:n_r], d_rows):
            o_ref[...] = d.astype(o_ref.dtype)
        first = pl.program_id(0) == 0
        for a_ref, d in zip(o_refs[n_r:], d_params):
            @pl.when(first)
            def _(a_ref=a_ref):
                a_ref[...] = jnp.zeros_like(a_ref)
            a_ref[...] += d

    res = pl.pallas_call(
        body, name=name, grid=(nt_,),
        in_specs=in_specs + [rspec(c[1], c[2]) for c in cts],
        out_specs=[rspec(r[1], 0) for r in rows] + [pspec(p.shape[1]) for p in params],
        out_shape=[jax.ShapeDtypeStruct((length, r[1]), dt) for r, dt in zip(rows, d_dtypes or [F32] * n_r)]
                  + [jax.ShapeDtypeStruct((1, p.shape[1]), F32) for p in params],
        compiler_params=_cparams(("arbitrary",), VMEM_LIMIT),
    )(*in_arrays, *[c[0] for c in cts])
    return res[:n_r], res[n_r:]


def _conv_shift(ext, k, tm, forward):
    s = CONV_W - 1 - k
    if forward:
        rolled = ext if s == 0 else pltpu.roll(ext, s, 0)
        return rolled[SUBLANE:, :]
    rolled = ext if s == 0 else pltpu.roll(ext, tm + SUBLANE - s, 0)
    return rolled[:tm, :]


def _conv_pre(x_ref, halo_ref, w_ref, b_ref, first, tm):
    halo = jnp.where(first, 0.0, halo_ref[...])
    ext = jnp.concatenate([halo, x_ref[...]], axis=0)
    w = w_ref[...]
    pre = b_ref[...] + jnp.zeros_like(x_ref[...])
    taps = []
    for k in range(CONV_W):
        tap = _conv_shift(ext, k, tm, True)
        taps.append(tap)
        pre = pre + tap * w[k:k + 1, :]
    return pre, taps


def _conv_fwd(proj, w, b, length, tm, tc):
    hb = tm // SUBLANE

    def body(x_ref, halo_ref, w_ref, b_ref, o_ref):
        pre, _ = _conv_pre(x_ref, halo_ref, w_ref, b_ref, pl.program_id(0) == 0, tm)
        o_ref[...] = _silu(pre)

    return pl.pallas_call(
        body, name="conv_fwd", grid=(length // tm, CONV_ALL // tc),
        in_specs=[pl.BlockSpec((tm, tc), lambda i, j: (i, j)),
                  pl.BlockSpec((SUBLANE, tc), lambda i, j: (jnp.maximum(i * hb - 1, 0), j)),
                  pl.BlockSpec((CONV_W, tc), lambda i, j: (0, j)),
                  pl.BlockSpec((1, tc), lambda i, j: (0, j))],
        out_specs=pl.BlockSpec((tm, tc), lambda i, j: (i, j)),
        out_shape=jax.ShapeDtypeStruct((length, CONV_ALL), F32),
        compiler_params=_cparams(("parallel", "parallel"), VMEM_LIMIT),
    )(proj, proj, w, b)


def _conv_bwd_pre(proj, w, b, dxc, length, tm, tc):
    hb = tm // SUBLANE

    def body(x_ref, halo_ref, w_ref, b_ref, dy_ref, dpre_ref, dw_ref, db_ref):
        i = pl.program_id(1)
        pre, taps = _conv_pre(x_ref, halo_ref, w_ref, b_ref, i == 0, tm)
        sg = jax.nn.sigmoid(pre)
        dpre = dy_ref[...] * (sg * (1.0 + pre * (1.0 - sg)))
        dpre_ref[...] = dpre

        @pl.when(i == 0)
        def _():
            dw_ref[...] = jnp.zeros_like(dw_ref)
            db_ref[...] = jnp.zeros_like(db_ref)

        for k in range(CONV_W):
            dw_ref[k:k + 1, :] += jnp.sum(dpre * taps[k], axis=0, keepdims=True)
        db_ref[...] += jnp.sum(dpre, axis=0, keepdims=True)

    return pl.pallas_call(
        body, name="conv_bwd_pre", grid=(CONV_ALL // tc, length // tm),
        in_specs=[pl.BlockSpec((tm, tc), lambda j, i: (i, j)),
                  pl.BlockSpec((SUBLANE, tc), lambda j, i: (jnp.maximum(i * hb - 1, 0), j)),
                  pl.BlockSpec((CONV_W, tc), lambda j, i: (0, j)),
                  pl.BlockSpec((1, tc), lambda j, i: (0, j)),
                  pl.BlockSpec((tm, tc), lambda j, i: (i, j))],
        out_specs=[pl.BlockSpec((tm, tc), lambda j, i: (i, j)),
                   pl.BlockSpec((CONV_W, tc), lambda j, i: (0, j)),
                   pl.BlockSpec((1, tc), lambda j, i: (0, j))],
        out_shape=[jax.ShapeDtypeStruct((length, CONV_ALL), F32),
                   jax.ShapeDtypeStruct((CONV_W, CONV_ALL), F32),
                   jax.ShapeDtypeStruct((1, CONV_ALL), F32)],
        compiler_params=_cparams(("parallel", "arbitrary"), VMEM_LIMIT),
    )(proj, proj, w, b, dxc)


def _conv_bwd_x(dpre, w, length, tm, tc):
    hb = tm // SUBLANE
    n_t = length // tm
    last_blk = length // SUBLANE - 1

    def body(d_ref, halo_ref, w_ref, o_ref):
        halo = jnp.where(pl.program_id(0) == n_t - 1, 0.0, halo_ref[...])
        ext = jnp.concatenate([d_ref[...], halo], axis=0)
        w = w_ref[...]
        acc = jnp.zeros_like(d_ref[...])
        for k in range(CONV_W):
            acc = acc + _conv_shift(ext, k, tm, False) * w[k:k + 1, :]
        o_ref[...] = acc.astype(o_ref.dtype)

    return pl.pallas_call(
        body, name="conv_bwd_x", grid=(n_t, CONV_ALL // tc),
        in_specs=[pl.BlockSpec((tm, tc), lambda i, j: (i, j)),
                  pl.BlockSpec((SUBLANE, tc), lambda i, j: (jnp.minimum((i + 1) * hb, last_blk), j)),
                  pl.BlockSpec((CONV_W, tc), lambda i, j: (0, j))],
        out_specs=pl.BlockSpec((tm, tc), lambda i, j: (i, j)),
        out_shape=jax.ShapeDtypeStruct((length, CONV_ALL), BF16),
        compiler_params=_cparams(("parallel", "parallel"), VMEM_LIMIT),
    )(dpre, dpre, w)


GDN_CHUNKS_PER_STEP = 4


def _gdn_steps(length):
    nc = length // CHUNK
    gs = GDN_CHUNKS_PER_STEP if nc % GDN_CHUNKS_PER_STEP == 0 else 1
    return gs, nc // gs, gs * CHUNK


def _gdn_tiles(ref, gs):
    return [[ref[j * CHUNK:(j + 1) * CHUNK, h * LANE:(h + 1) * LANE] for h in range(GDN_HEADS)] for j in range(gs)]


def _gdn_scalars(sm):
    gb, ga = _MINE["gb"] - SMALL_OFF, _MINE["ga"] - SMALL_OFF
    return ([_lane_col(sm, gb + h) for h in range(GDN_HEADS)], [_lane_col(sm, ga + h) for h in range(GDN_HEADS)])


def _gdn_fwd(xc, proj, alog, dtb, nw, length, comm=None):
    gs, steps, rows = _gdn_steps(length)
    h_ = GDN_HEADS
    hw = h_ * LANE
    n_comm = len(comm[1]) if comm else 0

    def body(*refs):
        q_ref, k_ref, v_ref, gz_ref, sm_ref, al_ref, dt_ref, nw_ref = refs[:8]
        c_in = refs[8:8 + n_comm]
        on_ref, st_ref = refs[8 + n_comm:10 + n_comm]
        c_out = refs[10 + n_comm:10 + 2 * n_comm]
        s_ref = refs[10 + 2 * n_comm]
        sems = refs[11 + 2 * n_comm:]

        @pl.when(pl.program_id(0) == 0)
        def _():
            s_ref[...] = jnp.zeros_like(s_ref)
            if comm:
                _comm_plan(comm[0], c_in, c_out, *sems)[0]()

        st_ref[...] = s_ref[...]
        lead = lambda ref: [ref[h] for h in range(h_)]
        scal = [_gdn_scalars(sm_ref[j * CHUNK:(j + 1) * CHUNK, :]) for j in range(gs)]
        on, s_new = _gdn_chunk(_gdn_tiles(q_ref, gs), _gdn_tiles(k_ref, gs), _gdn_tiles(v_ref, gs),
                               _gdn_tiles(gz_ref, gs), [sc[0] for sc in scal], [sc[1] for sc in scal],
                               lead(al_ref), lead(dt_ref), nw_ref[...], lead(s_ref))
        for h in range(h_):
            for j in range(gs):
                on_ref[j * CHUNK:(j + 1) * CHUNK, h * LANE:(h + 1) * LANE] = on[j][h].astype(on_ref.dtype)
            s_ref[h] = s_new[h]

        if comm:
            @pl.when(pl.program_id(0) == steps - 1)
            def _():
                _comm_plan(comm[0], c_in, c_out, *sems)[1]()

    blk = lambda name: pl.BlockSpec((rows, hw), lambda c: (c, _MINE[name] // hw))
    one = pl.BlockSpec((h_, 1, 1), lambda c: (0, 0, 0))
    hbm = pl.BlockSpec(memory_space=pl.ANY)
    return pl.pallas_call(
        body, name="gdn_fwd_" + comm[0] if comm else "gdn_fwd", grid=(steps,),
        in_specs=[blk("gq"), blk("gk"), blk("gv"), blk("gz"),
                  pl.BlockSpec((rows, LANE), lambda c: (c, SMALL_OFF // LANE)), one, one,
                  pl.BlockSpec((1, LANE), lambda c: (0, 0))] + [hbm] * n_comm,
        out_specs=[pl.BlockSpec((rows, hw), lambda c: (c, 0)),
                   pl.BlockSpec((h_, None, LANE, LANE), lambda c: (0, c, 0, 0))] + [hbm] * n_comm,
        out_shape=[jax.ShapeDtypeStruct((length, hw), BF16),
                   jax.ShapeDtypeStruct((h_, steps, LANE, LANE), F32)]
                  + (_comm_out_shapes(*comm) if comm else []),
        scratch_shapes=[pltpu.VMEM((h_, LANE, LANE), F32)] + (_comm_scratch(n_comm) if comm else []),
        compiler_params=_cparams(("arbitrary",), VMEM_LIMIT),
    )(xc, xc, xc, proj, proj, alog, dtb, nw, *(comm[1] if comm else []))


def _gdn_bwd(xc, proj, alog, dtb, nw, st, dmix, length, comm=None):
    gs, steps, rows = _gdn_steps(length)
    h_ = GDN_HEADS
    n_comm = len(comm[1]) if comm else 0
    n_in, n_out = 10, 8

    def body(*refs):
        q_ref, k_ref, v_ref, gz_ref, sm_ref, al_ref, dt_ref, nw_ref, st_ref, do_ref = refs[:n_in]
        c_in = refs[n_in:n_in + n_comm]
        o0 = n_in + n_comm
        dq_ref, dk_ref, dv_ref, dgz_ref, dsm_ref, dal_ref, ddt_ref, dnw_ref = refs[o0:o0 + n_out]
        c_out = refs[o0 + n_out:o0 + n_out + n_comm]
        ds_ref = refs[o0 + n_out + n_comm]
        sems = refs[o0 + n_out + n_comm + 1:]

        @pl.when(pl.program_id(0) == 0)
        def _():
            ds_ref[...] = jnp.zeros_like(ds_ref)
            dal_ref[...] = jnp.zeros_like(dal_ref)
            ddt_ref[...] = jnp.zeros_like(ddt_ref)
            dnw_ref[...] = jnp.zeros_like(dnw_ref)
            if comm:
                _comm_plan(comm[0], c_in, c_out, *sems)[0]()

        lead = lambda ref: [ref[h] for h in range(h_)]
        scal = [_gdn_scalars(sm_ref[j * CHUNK:(j + 1) * CHUNK, :]) for j in range(gs)]
        ins = (_gdn_tiles(q_ref, gs), _gdn_tiles(k_ref, gs), _gdn_tiles(v_ref, gs), _gdn_tiles(gz_ref, gs),
               [sc[0] for sc in scal], [sc[1] for sc in scal], lead(al_ref), lead(dt_ref), nw_ref[...], lead(st_ref))
        _, vjp = jax.vjp(_gdn_chunk, *ins)
        dq, dk, dv, dgz, db, da, dal, ddt, dnw, ds = vjp((_gdn_tiles(do_ref, gs), lead(ds_ref)))
        lane = lax.broadcasted_iota(jnp.int32, (1, LANE), 1)
        gb, ga = _MINE["gb"] - SMALL_OFF, _MINE["ga"] - SMALL_OFF
        for j in range(gs):
            rws = slice(j * CHUNK, (j + 1) * CHUNK)
            dsm = jnp.zeros((CHUNK, LANE), F32)
            for h in range(h_):
                cols = slice(h * LANE, (h + 1) * LANE)
                dq_ref[rws, cols] = dq[j][h]
                dk_ref[rws, cols] = dk[j][h]
                dv_ref[rws, cols] = dv[j][h]
                dgz_ref[rws, cols] = dgz[j][h].astype(dgz_ref.dtype)
                dsm = dsm + jnp.where(lane == gb + h, db[j][h], 0.0) + jnp.where(lane == ga + h, da[j][h], 0.0)
            dsm_ref[rws, :] = dsm
        for h in range(h_):
            dal_ref[h] += dal[h]
            ddt_ref[h] += ddt[h]
            ds_ref[h] = ds[h]
        dnw_ref[...] += dnw

        if comm:
            @pl.when(pl.program_id(0) == steps - 1)
            def _():
                _comm_plan(comm[0], c_in, c_out, *sems)[1]()

    hw = h_ * LANE
    r = lambda c: steps - 1 - c
    blk = lambda name: pl.BlockSpec((rows, hw), lambda c: (r(c), _MINE[name] // hw))
    one = pl.BlockSpec((h_, 1, 1), lambda c: (0, 0, 0))
    oblk = pl.BlockSpec((rows, hw), lambda c: (r(c), 0))
    big = jax.ShapeDtypeStruct((length, hw), F32)
    hbm = pl.BlockSpec(memory_space=pl.ANY)
    return pl.pallas_call(
        body, name="gdn_bwd_" + comm[0] if comm else "gdn_bwd", grid=(steps,),
        in_specs=[blk("gq"), blk("gk"), blk("gv"), blk("gz"),
                  pl.BlockSpec((rows, LANE), lambda c: (r(c), SMALL_OFF // LANE)), one, one,
                  pl.BlockSpec((1, LANE), lambda c: (0, 0)),
                  pl.BlockSpec((h_, None, LANE, LANE), lambda c: (0, r(c), 0, 0)),
                  oblk] + [hbm] * n_comm,
        out_specs=[oblk, oblk, oblk, oblk, pl.BlockSpec((rows, LANE), lambda c: (r(c), 0)), one, one,
                   pl.BlockSpec((1, LANE), lambda c: (0, 0))] + [hbm] * n_comm,
        out_shape=[big, big, big, jax.ShapeDtypeStruct((length, hw), BF16),
                   jax.ShapeDtypeStruct((length, LANE), F32),
                   jax.ShapeDtypeStruct((h_, 1, 1), F32), jax.ShapeDtypeStruct((h_, 1, 1), F32),
                   jax.ShapeDtypeStruct((1, LANE), F32)] + (_comm_out_shapes(*comm) if comm else []),
        scratch_shapes=[pltpu.VMEM((h_, LANE, LANE), F32)] + (_comm_scratch(n_comm) if comm else []),
        compiler_params=_cparams(("arbitrary",), VMEM_LIMIT),
    )(xc, xc, xc, proj, proj, alog, dtb, nw, st, dmix, *(comm[1] if comm else []))


def _lane_col(block, lane):
    pick = lax.broadcasted_iota(jnp.int32, (1, block.shape[1]), 1) == lane
    return jnp.sum(jnp.where(pick, block, 0.0), axis=1, keepdims=True)


def _ssd_head_lane(g, p, j):
    return (_MINE["sdt"] - SMALL_OFF) + (g * PAIRS_PER_GROUP + p) * 2 + j


def _ssd_fwd(xc, proj, alog, dtb, dpar, length):
    nc = length // CHUNK
    ppg = PAIRS_PER_GROUP

    gw = ppg * LANE

    def body(x_ref, b_ref, c_ref, sm_ref, al_ref, db_ref, dp_ref, y_ref, st_ref, hs_ref):
        g = pl.program_id(0)

        @pl.when(pl.program_id(1) == 0)
        def _():
            hs_ref[...] = jnp.zeros_like(hs_ref)

        bm, cm, sm = b_ref[...], c_ref[...], sm_ref[...]
        st_ref[...] = hs_ref[...]
        two = lambda ref: [[_lane_col(ref[p], j) for j in range(2)] for p in range(ppg)]
        dtr = [[_lane_col(sm, _ssd_head_lane(g, p, j)) for j in range(2)] for p in range(ppg)]
        y, hs_new = _ssd_chunk([x_ref[:, p * LANE:(p + 1) * LANE] for p in range(ppg)], bm, cm, dtr, two(al_ref),
                               two(db_ref), two(dp_ref), [hs_ref[p] for p in range(ppg)])
        for p in range(ppg):
            y_ref[:, p * LANE:(p + 1) * LANE] = y[p]
            hs_ref[p] = hs_new[p]

    two = pl.BlockSpec((None, ppg, 1, 2), lambda g, c: (g, 0, 0, 0))
    return pl.pallas_call(
        body, name="ssd_fwd", grid=(SSD_GROUPS, nc),
        in_specs=[pl.BlockSpec((CHUNK, gw), lambda g, c: (c, _MINE["sx"] // gw + g)),
                  pl.BlockSpec((CHUNK, LANE), lambda g, c: (c, CB("sB") + g)),
                  pl.BlockSpec((CHUNK, LANE), lambda g, c: (c, CB("sC") + g)),
                  pl.BlockSpec((CHUNK, LANE), lambda g, c: (c, SMALL_OFF // LANE)),
                  two, two, two],
        out_specs=[pl.BlockSpec((CHUNK, gw), lambda g, c: (c, g)),
                   pl.BlockSpec((None, ppg, None, LANE, SSD_N), lambda g, c: (g, 0, c, 0, 0))],
        out_shape=[jax.ShapeDtypeStruct((length, SSD_GROUPS * gw), F32),
                   jax.ShapeDtypeStruct((SSD_GROUPS, ppg, nc, LANE, SSD_N), F32)],
        scratch_shapes=[pltpu.VMEM((ppg, LANE, SSD_N), F32)],
        compiler_params=_cparams(("parallel", "arbitrary"), VMEM_LIMIT),
    )(xc, xc, xc, proj, alog, dtb, dpar)


def _ssd_bwd(xc, proj, alog, dtb, dpar, st, dy, length):
    nc = length // CHUNK
    ppg = PAIRS_PER_GROUP

    gw = ppg * LANE

    def body(x_ref, b_ref, c_ref, sm_ref, al_ref, db_ref, dp_ref, st_ref, dy_ref,
             dx_ref, dbm_ref, dcm_ref, dsm_ref, dal_ref, ddb_ref, ddp_ref, dhs_ref):
        g = pl.program_id(0)

        @pl.when(pl.program_id(1) == 0)
        def _():
            dhs_ref[...] = jnp.zeros_like(dhs_ref)
            dal_ref[...] = jnp.zeros_like(dal_ref)
            ddb_ref[...] = jnp.zeros_like(ddb_ref)
            ddp_ref[...] = jnp.zeros_like(ddp_ref)

        bm, cm, sm = b_ref[...], c_ref[...], sm_ref[...]
        lane = lax.broadcasted_iota(jnp.int32, (1, LANE), 1)
        first = lax.broadcasted_iota(jnp.int32, (1, 2), 1) == 0
        pairs = range(ppg)
        two = lambda ref: [[_lane_col(ref[p], j) for j in range(2)] for p in pairs]
        dtr = [[_lane_col(sm, _ssd_head_lane(g, p, j)) for j in range(2)] for p in pairs]
        cols = lambda ref: [ref[:, p * LANE:(p + 1) * LANE] for p in pairs]
        ins = (cols(x_ref), bm, cm, dtr, two(al_ref), two(db_ref), two(dp_ref), [st_ref[p] for p in pairs])
        _, vjp = jax.vjp(_ssd_chunk, *ins)
        dx, dbm, dcm, ddt, dal, ddb, ddp, dhs = vjp((cols(dy_ref), [dhs_ref[p] for p in pairs]))
        dsm = jnp.zeros_like(sm)
        for p in pairs:
            dx_ref[:, p * LANE:(p + 1) * LANE] = dx[p]
            for j in range(2):
                dsm = dsm + jnp.where(lane == _ssd_head_lane(g, p, j), ddt[p][j], 0.0)
            dhs_ref[p] = dhs[p]
            dal_ref[p] += jnp.where(first, dal[p][0], dal[p][1])
            ddb_ref[p] += jnp.where(first, ddb[p][0], ddb[p][1])
            ddp_ref[p] += jnp.where(first, ddp[p][0], ddp[p][1])
        dbm_ref[...] = dbm
        dcm_ref[...] = dcm
        dsm_ref[...] = dsm

    r = lambda c: nc - 1 - c
    two = pl.BlockSpec((None, ppg, 1, 2), lambda g, c: (g, 0, 0, 0))
    grp = pl.BlockSpec((CHUNK, LANE), lambda g, c: (r(c), g))
    prm = jax.ShapeDtypeStruct((SSD_GROUPS, ppg, 1, 2), F32)
    return pl.pallas_call(
        body, name="ssd_bwd", grid=(SSD_GROUPS, nc),
        in_specs=[pl.BlockSpec((CHUNK, gw), lambda g, c: (r(c), _MINE["sx"] // gw + g)),
                  pl.BlockSpec((CHUNK, LANE), lambda g, c: (r(c), CB("sB") + g)),
                  pl.BlockSpec((CHUNK, LANE), lambda g, c: (r(c), CB("sC") + g)),
                  pl.BlockSpec((CHUNK, LANE), lambda g, c: (r(c), SMALL_OFF // LANE)),
                  two, two, two,
                  pl.BlockSpec((None, ppg, None, LANE, SSD_N), lambda g, c: (g, 0, r(c), 0, 0)),
                  pl.BlockSpec((CHUNK, gw), lambda g, c: (r(c), g))],
        out_specs=[pl.BlockSpec((CHUNK, gw), lambda g, c: (r(c), g)), grp, grp,
                   pl.BlockSpec((None, CHUNK, LANE), lambda g, c: (g, r(c), 0)), two, two, two],
        out_shape=[jax.ShapeDtypeStruct((length, SSD_GROUPS * gw), F32),
                   jax.ShapeDtypeStruct((length, SSD_GROUPS * SSD_N), F32),
                   jax.ShapeDtypeStruct((length, SSD_GROUPS * SSD_N), F32),
                   jax.ShapeDtypeStruct((SSD_GROUPS, length, LANE), F32), prm, prm, prm],
        scratch_shapes=[pltpu.VMEM((ppg, LANE, SSD_N), F32)],
        compiler_params=_cparams(("parallel", "arbitrary"), VMEM_LIMIT),
    )(xc, xc, xc, proj, alog, dtb, dpar, st, dy)


def _ret_fwd(proj, cos2, sin2, dmat, qdec, kdec, cdec, nw, length):
    nc = length // CHUNK
    h_ = RET_HEADS

    def body(q_ref, k_ref, v_ref, g_ref, cos_ref, sin_ref, dm_ref, qd_ref, kd_ref, cd_ref, nw_ref,
             on_ref, st_ref, r_ref):
        @pl.when(pl.program_id(0) == 0)
        def _():
            r_ref[...] = jnp.zeros_like(r_ref)

        st_ref[...] = r_ref[...]
        heads = lambda ref: [ref[:, h * LANE:(h + 1) * LANE] for h in range(h_)]
        lead = lambda ref: [ref[h] for h in range(h_)]
        on, r_new = _ret_chunk(heads(q_ref), heads(k_ref), heads(v_ref), heads(g_ref), cos_ref[...], sin_ref[...],
                               lead(dm_ref), lead(qd_ref), lead(kd_ref), lead(cd_ref), nw_ref[...], lead(r_ref))
        for h in range(h_):
            on_ref[:, h * LANE:(h + 1) * LANE] = on[h].astype(on_ref.dtype)
            r_ref[h] = r_new[h]

    hw = h_ * LANE
    blk = lambda name: pl.BlockSpec((CHUNK, hw), lambda c: (c, _MINE[name] // hw))
    tab = pl.BlockSpec((CHUNK, LANE), lambda c: (c, 0))
    full = lambda *s: pl.BlockSpec(s, lambda c: (0,) * len(s))
    return pl.pallas_call(
        body, name="ret_fwd", grid=(nc,),
        in_specs=[blk("rq"), blk("rk"), blk("rv"), blk("rg"), tab, tab,
                  full(h_, CHUNK, CHUNK), full(h_, CHUNK, 1), full(h_, CHUNK, 1), full(h_, 1, 1), full(1, LANE)],
        out_specs=[pl.BlockSpec((CHUNK, hw), lambda c: (c, 0)),
                   pl.BlockSpec((h_, None, LANE, LANE), lambda c: (0, c, 0, 0))],
        out_shape=[jax.ShapeDtypeStruct((length, hw), BF16),
                   jax.ShapeDtypeStruct((h_, nc, LANE, LANE), F32)],
        scratch_shapes=[pltpu.VMEM((h_, LANE, LANE), F32)],
        compiler_params=_cparams(("arbitrary",), VMEM_LIMIT),
    )(proj, proj, proj, proj, cos2, sin2, dmat, qdec, kdec, cdec, nw)


def _ret_bwd(proj, cos2, sin2, dmat, qdec, kdec, cdec, nw, st, dmix, length):
    nc = length // CHUNK
    h_ = RET_HEADS

    def body(q_ref, k_ref, v_ref, g_ref, cos_ref, sin_ref, dm_ref, qd_ref, kd_ref, cd_ref, nw_ref, st_ref,
             do_ref, dq_ref, dk_ref, dv_ref, dg_ref, dnw_ref, dr_ref):
        @pl.when(pl.program_id(0) == 0)
        def _():
            dr_ref[...] = jnp.zeros_like(dr_ref)
            dnw_ref[...] = jnp.zeros_like(dnw_ref)

        heads = lambda ref: [ref[:, h * LANE:(h + 1) * LANE] for h in range(h_)]
        lead = lambda ref: [ref[h] for h in range(h_)]
        consts = (cos_ref[...], sin_ref[...], lead(dm_ref), lead(qd_ref), lead(kd_ref), lead(cd_ref))
        f = lambda q, k, v, g, w_, r_: _ret_chunk(q, k, v, g, *consts, w_, r_)
        _, vjp = jax.vjp(f, heads(q_ref), heads(k_ref), heads(v_ref), heads(g_ref), nw_ref[...], lead(st_ref))
        dq, dk, dv, dg, dnw, dr = vjp((heads(do_ref), lead(dr_ref)))
        for h in range(h_):
            cols = slice(h * LANE, (h + 1) * LANE)
            dq_ref[:, cols] = dq[h].astype(dq_ref.dtype)
            dk_ref[:, cols] = dk[h].astype(dk_ref.dtype)
            dv_ref[:, cols] = dv[h].astype(dv_ref.dtype)
            dg_ref[:, cols] = dg[h].astype(dg_ref.dtype)
            dr_ref[h] = dr[h]
        dnw_ref[...] += dnw

    hw = h_ * LANE
    r = lambda c: nc - 1 - c
    blk = lambda name: pl.BlockSpec((CHUNK, hw), lambda c: (r(c), _MINE[name] // hw))
    tab = pl.BlockSpec((CHUNK, LANE), lambda c: (r(c), 0))
    full = lambda *s: pl.BlockSpec(s, lambda c: (0,) * len(s))
    oblk = pl.BlockSpec((CHUNK, hw), lambda c: (r(c), 0))
    big = jax.ShapeDtypeStruct((length, hw), BF16)
    return pl.pallas_call(
        body, name="ret_bwd", grid=(nc,),
        in_specs=[blk("rq"), blk("rk"), blk("rv"), blk("rg"), tab, tab,
                  full(h_, CHUNK, CHUNK), full(h_, CHUNK, 1), full(h_, CHUNK, 1), full(h_, 1, 1), full(1, LANE),
                  pl.BlockSpec((h_, None, LANE, LANE), lambda c: (0, r(c), 0, 0)),
                  pl.BlockSpec((CHUNK, hw), lambda c: (r(c), (MIX_W - hw) // hw))],
        out_specs=[oblk, oblk, oblk, oblk, full(1, LANE)],
        out_shape=[big, big, big, big, jax.ShapeDtypeStruct((1, LANE), F32)],
        scratch_shapes=[pltpu.VMEM((h_, LANE, LANE), F32)],
        compiler_params=_cparams(("arbitrary",), VMEM_LIMIT),
    )(proj, proj, proj, proj, cos2, sin2, dmat, qdec, kdec, cdec, nw, st, dmix)


def _mesh_pos():
    x, y, c = lax.axis_index("x"), lax.axis_index("y"), lax.axis_index("c")
    return x, y, c, 4 * x + 2 * y + c


def _peer(x, y, c, mask):
    return (x ^ ((mask >> 2) & 1), y ^ ((mask >> 1) & 1), c ^ (mask & 1))


def _comm_plan(kind, ins, outs, send_sems, recv_sems, local_sems):
    x, y, c, me = _mesh_pos()
    n = len(ins)
    src = (lambda a, idx: ins[a]) if kind == "gather" else (lambda a, idx: ins[a].at[idx])
    local = [pltpu.make_async_copy(src(a, me), outs[a].at[me], local_sems.at[a]) for a in range(n)]
    sends, recvs = [], []
    for mask in range(1, N_DEV):
        px, py, pc = _peer(x, y, c, mask)
        pidx = 4 * px + 2 * py + pc
        for a in range(n):
            k = (mask - 1) * n + a
            copy = lambda s, d, k=k: pltpu.make_async_remote_copy(
                src_ref=s, dst_ref=d, send_sem=send_sems.at[k], recv_sem=recv_sems.at[k],
                device_id=(px, py, pc), device_id_type=pl.DeviceIdType.MESH)
            sends.append(copy(src(a, pidx), outs[a].at[me]))
            recvs.append(copy(src(a, me), outs[a].at[pidx]))

    def start():
        for cp in local + sends:
            cp.start()

    def wait():
        for cp in sends:
            cp.wait_send()
        for cp in recvs:
            cp.wait_recv()
        for cp in local:
            cp.wait()

    return start, wait


def _comm_out_shapes(kind, arrays):
    return [jax.ShapeDtypeStruct(((N_DEV,) + s.shape) if kind == "gather" else s.shape, s.dtype) for s in arrays]


def _comm_scratch(n):
    k_tot = (N_DEV - 1) * n
    return [pltpu.SemaphoreType.DMA((k_tot,)), pltpu.SemaphoreType.DMA((k_tot,)), pltpu.SemaphoreType.DMA((n,))]


def _comm_call(name, kind, arrays):
    n = len(arrays)

    def body(*refs):
        start, wait = _comm_plan(kind, refs[:n], refs[n:2 * n], *refs[2 * n:])
        start()
        wait()

    hbm = pl.BlockSpec(memory_space=pl.ANY)
    return pl.pallas_call(
        body, name=name, in_specs=[hbm] * n, out_specs=[hbm] * n, out_shape=_comm_out_shapes(kind, arrays),
        scratch_shapes=_comm_scratch(n), compiler_params=pltpu.CompilerParams(has_side_effects=True),
    )(*arrays)


def _adam(name, parts, w, m, v, tr, layer=None, prev=None):
    p_, r_, cw = parts.shape
    c1 = 1.0 - ADAM_B1 ** ADAM_STEP
    c2 = 1.0 - ADAM_B2 ** ADAM_STEP
    n_prev = 4 if prev else 0

    def body(p_ref, w_ref, m_ref, v_ref, *rest):
        g_ref, d_ref, nm_ref, nv_ref = rest[n_prev:]
        g = p_ref[0].astype(F32)
        for i in range(1, p_):
            g = g + p_ref[i].astype(F32)
        nm = ADAM_B1 * m_ref[...] + (1.0 - ADAM_B1) * g
        nv = ADAM_B2 * v_ref[...] + (1.0 - ADAM_B2) * (g * g)
        d_ref[...] = -ADAM_LR * ((nm / c1) / (jnp.sqrt(nv / c2) + ADAM_EPS) + ADAM_WD * w_ref[...])
        g_ref[...] = g
        nm_ref[...] = nm
        nv_ref[...] = nv

    if layer is None:
        spec = pl.BlockSpec((tr, cw), lambda i: (i, 0))
        shp = jax.ShapeDtypeStruct((r_, cw), F32)
    else:
        spec = pl.BlockSpec((None, tr, cw), lambda i: (layer, i, 0))
        shp = jax.ShapeDtypeStruct((DEPTH, r_, cw), F32)
    return pl.pallas_call(
        body, name=name, grid=(r_ // tr,),
        in_specs=[pl.BlockSpec((p_, tr, cw), lambda i: (0, i, 0)), spec, spec, spec]
                 + [pl.BlockSpec(memory_space=pl.ANY)] * n_prev,
        out_specs=[spec] * 4, out_shape=[shp] * 4,
        input_output_aliases={4 + i: i for i in range(n_prev)},
        compiler_params=_cparams(("parallel",), VMEM_LIMIT),
    )(parts, w, m, v, *(prev or ()))


def _perm_cols(w_full):
    parts = [w_full[:, _ORIG[n][0]:_ORIG[n][0] + _ORIG[n][1]] for n in _ORDER]
    parts.append(jnp.zeros((w_full.shape[0], N_PAD - N_USED), w_full.dtype))
    return jnp.concatenate(parts, axis=1)


def _unperm_cols(g):
    by_orig = sorted(_ORDER, key=lambda n: _ORIG[n][0])
    return jnp.concatenate([g[:, _MINE[n]:_MINE[n] + _ORIG[n][1]] for n in by_orig], axis=1)


def _ret_consts(length):
    lg = jnp.log(1.0 - 2.0 ** (-5.0 - jnp.arange(RET_HEADS, dtype=F32)))
    idx = jnp.arange(CHUNK, dtype=F32)
    rel = idx[:, None] - idx[None, :]
    dmat = jnp.where(rel[None] >= 0, jnp.exp(jnp.maximum(rel, 0.0)[None] * lg[:, None, None]), 0.0)
    qdec = jnp.exp((idx[:, None] + 1.0) * lg[None, :]).T[:, :, None]
    kdec = jnp.exp((CHUNK - 1.0 - idx)[:, None] * lg[None, :]).T[:, :, None]
    cdec = jnp.exp(CHUNK * lg)[:, None, None]
    half = RET_DK // 2
    inv = ROPE_BASE ** (-jnp.arange(half, dtype=F32) / half)
    ang = jnp.arange(length, dtype=jnp.int32).astype(F32)[:, None] * inv[None, :]
    cos, sin = jnp.cos(ang), jnp.sin(ang)
    return (jnp.concatenate([cos, cos], axis=1), jnp.concatenate([-sin, sin], axis=1), dmat, qdec, kdec, cdec)


def _tiles(length):
    tm = 512 if length % 512 == 0 else length
    return tm


def _layer_fwd(x, p, consts, length, last, target, comm=None):
    tm = _tiles(length)
    (h,), _ = _rowmap("pre_norm", lambda x_, w_: ((_rms(x_, w_),), ()), length, tm,
                      [(x, D_MODEL, 0)], [p["pre_norm"]], out_ws=[D_MODEL], out_dtypes=[BF16])
    proj = _matmul("in_proj", h, p["w_in"], "nn", length, N_PAD, D_MODEL)
    xc = _conv_fwd(proj, p["conv_w"], p["conv_b"], length, tm, 512)
    o_a, st_a, *got = _gdn_fwd(xc, proj, p["gdn_A_log"], p["gdn_dt_bias"], p["gdn_norm"], length, comm)
    y_b, st_b = _ssd_fwd(xc, proj, p["ssd_A_log"], p["ssd_dt_bias"], p["ssd_D"], length)
    o_b = []
    for g in range(SSD_GROUPS):
        (o,), _ = _rowmap(f"ssd_gate{g}", lambda y_, z_, w_: ((_rms(y_ * _silu(z_), w_),), ()), length, tm,
                          [(y_b, 512, g), (proj, 512, _MINE["sz"] // 512 + g)], [p["ssd_norm"][g]], out_ws=[512],
                          out_dtypes=[BF16])
        o_b.append(o)
    o_c, st_c = _ret_fwd(proj, *consts, p["ret_norm"], length)
    mixed = jnp.concatenate([o.astype(BF16) for o in [o_a] + o_b + [o_c]], axis=1)
    out = _matmul("out_proj", mixed, p["w_out"], "nn", length, D_MODEL, MIX_W)
    res = dict(x=x, h=h, proj=proj, xc=xc, st_a=st_a, st_b=st_b, st_c=st_c, y_b=y_b, mixed=mixed, out=out)
    if not last:
        (y,), _ = _rowmap("post_norm", lambda x_, o_, w_: ((x_ + _rms(o_, w_),), ()), length, tm,
                          [(x, D_MODEL, 0), (out, D_MODEL, 0)], [p["post_norm"]], out_ws=[D_MODEL])
        return y, res, None, got

    def head(x_, o_, t_, w_):
        e = x_ + _rms(o_, w_) - t_
        row = jnp.mean(e * e, axis=-1, keepdims=True)
        loss = 0.5 * jnp.sum(row, axis=0, keepdims=True)
        return (e * (1.0 / D_MODEL),), (loss + jnp.zeros((1, LANE), F32),)

    (dy,), (loss,) = _rowmap("loss_head", head, length, tm,
                             [(x, D_MODEL, 0), (out, D_MODEL, 0), (target, D_MODEL, 0)], [p["post_norm"]],
                             out_ws=[D_MODEL], acc_ws=[LANE])
    return None, res, (dy, loss[0, 0]), got


def _layer_bwd(dy, p, res, consts, length, comm=None):
    tm = _tiles(length)
    g = {}
    (d_out,), (g["post_norm"],) = _rowmap("post_norm_bwd", lambda o_, w_: ((_rms(o_, w_),), ()), length, tm,
                                          [(res["out"], D_MODEL, 0)], [p["post_norm"]], cts=[(dy, D_MODEL, 0)])
    dmix = _matmul("out_proj_dx", d_out, p["w_out"], "nt", length, MIX_W, D_MODEL)
    g["w_out"] = _matmul("out_proj_dw", res["mixed"], d_out, "tn", MIX_W, D_MODEL, length)
    proj, xc = res["proj"], res["xc"]
    (dgq, dgk, dgv, dgz, dsm_a, g["gdn_A_log"], g["gdn_dt_bias"], g["gdn_norm"], *got) = _gdn_bwd(
        xc, proj, p["gdn_A_log"], p["gdn_dt_bias"], p["gdn_norm"], res["st_a"], dmix, length, comm)
    dyb, dsz, dsn = [], [], []
    for gi in range(SSD_GROUPS):
        (dy_g, dz_g), (dn_g,) = _rowmap(
            f"ssd_gate_bwd{gi}", lambda y_, z_, w_: ((_rms(y_ * _silu(z_), w_),), ()), length, tm,
            [(res["y_b"], 512, gi), (proj, 512, _MINE["sz"] // 512 + gi)], [p["ssd_norm"][gi]],
            cts=[(dmix, 512, 1 + gi)], d_dtypes=[F32, BF16])
        dyb.append(dy_g)
        dsz.append(dz_g)
        dsn.append(dn_g)
    g["ssd_norm"] = dsn
    dyb = jnp.concatenate(dyb, axis=1)
    dsx, dsb, dsc, dsm_b, g["ssd_A_log"], g["ssd_dt_bias"], g["ssd_D"] = _ssd_bwd(
        xc, proj, p["ssd_A_log"], p["ssd_dt_bias"], p["ssd_D"], res["st_b"], dyb, length)
    drq, drk, drv, drg, dnw_c = _ret_bwd(proj, *consts, p["ret_norm"], res["st_c"], dmix, length)
    g["ret_norm"] = dnw_c
    dxc = jnp.concatenate([dgq, dgk, dgv, dsx, dsb, dsc], axis=1)
    dpre, g["conv_w"], g["conv_b"] = _conv_bwd_pre(proj, p["conv_w"], p["conv_b"], dxc, length, tm, 512)
    dconv = _conv_bwd_x(dpre, p["conv_w"], length, tm, 512)
    dsmall = dsm_a + dsm_b[0] + dsm_b[1]
    dproj = jnp.concatenate([d.astype(BF16) for d in [dconv, dgz] + dsz + [drq, drk, drv, drg, dsmall]]
                            + [jnp.zeros((length, N_PAD - SMALL_OFF - LANE), BF16)], axis=1)
    dh = _matmul("in_proj_dx", dproj, p["w_in"], "nt", length, D_MODEL, N_PAD)
    g["w_in"] = _matmul("in_proj_dw", res["h"], dproj, "tn", D_MODEL, N_PAD, length)
    (dx,), (g["pre_norm"],) = _rowmap(
        "pre_norm_bwd", lambda x_, w_: ((_rms(x_, w_), x_), ()), length, tm,
        [(res["x"], D_MODEL, 0)], [p["pre_norm"]], cts=[(dh, D_MODEL, 0), (dy, D_MODEL, 0)])
    return dx, g, got


def _make_layer(l, w_in_full, w_out_full, conv_full, sw):
    return dict(
        pre_norm=sw["pre_norm"][l][None], post_norm=sw["post_norm"][l][None], w_in=_perm_cols(w_in_full),
        w_out=w_out_full,
        conv_w=jnp.concatenate([conv_full[0], conv_full[1]], axis=1),
        conv_b=jnp.concatenate([jnp.zeros((CONV_CH,), F32), sw["ssd_conv_b"][l]])[None],
        gdn_A_log=sw["gdn_A_log"][l][:, None, None], gdn_dt_bias=sw["gdn_dt_bias"][l][:, None, None],
        gdn_norm=sw["gdn_norm"][l][None],
        ssd_A_log=sw["ssd_A_log"][l].reshape(SSD_GROUPS, PAIRS_PER_GROUP, 1, 2),
        ssd_dt_bias=sw["ssd_dt_bias"][l].reshape(SSD_GROUPS, PAIRS_PER_GROUP, 1, 2),
        ssd_D=sw["ssd_D"][l].reshape(SSD_GROUPS, PAIRS_PER_GROUP, 1, 2),
        ssd_norm=[sw["ssd_norm"][l][None, 512 * g:512 * (g + 1)] for g in range(SSD_GROUPS)],
        ret_norm=sw["ret_norm"][l][None])


def _local_step(xb, tgt, layer0, make_layer1, length, fwd_comm=None, make_bwd_comm=None):
    consts = _ret_consts(length)
    y0, res0, _, got = _layer_fwd(xb, layer0, consts, length, False, tgt, fwd_comm)
    layer1 = make_layer1(got)
    _, res1, (dy, loss_local), _ = _layer_fwd(y0, layer1, consts, length, True, tgt)
    dy, grads1, _ = _layer_bwd(dy, layer1, res1, consts, length)
    dx, grads0, recv = _layer_bwd(dy, layer0, res0, consts, length, make_bwd_comm(grads1) if make_bwd_comm else None)
    return loss_local, dx, [grads0, grads1], recv


_SMALL = ["pre_norm", "post_norm", "gdn_A_log", "gdn_dt_bias", "gdn_norm", "ssd_conv_b", "ssd_A_log",
          "ssd_dt_bias", "ssd_D", "ssd_norm", "ret_norm"]


def _pack_small(arrs):
    rows = []
    for n in _SMALL:
        flat = arrs[n].reshape(-1)
        pad = (-flat.shape[0]) % LANE
        rows.append(jnp.pad(flat, (0, pad)).reshape(-1, LANE))
    out = jnp.concatenate(rows, axis=0)
    return jnp.pad(out, ((0, (-out.shape[0]) % SUBLANE), (0, 0)))


def _unpack_small(packed, like):
    out, r = {}, 0
    for n in _SMALL:
        cnt = like[n].size
        nrow = -(-cnt // LANE)
        out[n] = packed[r:r + nrow].reshape(-1)[:cnt].reshape(like[n].shape)
        r += nrow
    return out


def kernel(x, pre_norm, post_norm, w_in, gdn_conv, gdn_A_log, gdn_dt_bias, gdn_norm, ssd_conv, ssd_conv_b, ssd_A_log, ssd_dt_bias, ssd_D, ssd_norm, ret_norm, w_out, loss_target, m_pre_norm, m_post_norm, m_w_in, m_gdn_conv, m_gdn_A_log, m_gdn_dt_bias, m_gdn_norm, m_ssd_conv, m_ssd_conv_b, m_ssd_A_log, m_ssd_dt_bias, m_ssd_D, m_ssd_norm, m_ret_norm, m_w_out, v_pre_norm, v_post_norm, v_w_in, v_gdn_conv, v_gdn_A_log, v_gdn_dt_bias, v_gdn_norm, v_ssd_conv, v_ssd_conv_b, v_ssd_A_log, v_ssd_dt_bias, v_ssd_D, v_ssd_norm, v_ret_norm, v_w_out):
    length = x.shape[1]
    xb = x[0]
    tgt = loss_target[0]
    small_w = dict(pre_norm=pre_norm, post_norm=post_norm, gdn_A_log=gdn_A_log, gdn_dt_bias=gdn_dt_bias,
                   gdn_norm=gdn_norm, ssd_conv_b=ssd_conv_b, ssd_A_log=ssd_A_log, ssd_dt_bias=ssd_dt_bias,
                   ssd_D=ssd_D, ssd_norm=ssd_norm, ret_norm=ret_norm)
    small_m = dict(pre_norm=m_pre_norm, post_norm=m_post_norm, gdn_A_log=m_gdn_A_log, gdn_dt_bias=m_gdn_dt_bias,
                   gdn_norm=m_gdn_norm, ssd_conv_b=m_ssd_conv_b, ssd_A_log=m_ssd_A_log, ssd_dt_bias=m_ssd_dt_bias,
                   ssd_D=m_ssd_D, ssd_norm=m_ssd_norm, ret_norm=m_ret_norm)
    small_v = dict(pre_norm=v_pre_norm, post_norm=v_post_norm, gdn_A_log=v_gdn_A_log, gdn_dt_bias=v_gdn_dt_bias,
                   gdn_norm=v_gdn_norm, ssd_conv_b=v_ssd_conv_b, ssd_A_log=v_ssd_A_log, ssd_dt_bias=v_ssd_dt_bias,
                   ssd_D=v_ssd_D, ssd_norm=v_ssd_norm, ret_norm=v_ret_norm)

    w_in_b, w_out_b = w_in.astype(BF16), w_out.astype(BF16)
    conv_shard = jnp.stack([gdn_conv, ssd_conv], axis=1)
    shards = lambda l: [w_in_b[l], w_out_b[l], conv_shard[l]]

    def assemble(l, g_in, g_out, g_conv):
        w_in_full = g_in.transpose(1, 0, 2).reshape(D_MODEL, N_IN)
        w_out_full = g_out.reshape(MIX_W, D_MODEL)
        conv_full = g_conv.transpose(1, 2, 0, 3).reshape(2, CONV_W, CONV_CH)
        return _make_layer(l, w_in_full, w_out_full, conv_full, small_w)

    def grad_slabs(g):
        gi = _unperm_cols(g["w_in"]).reshape(D_MODEL, N_DEV, SHARD_IN).transpose(1, 0, 2).astype(BF16)
        go = g["w_out"].reshape(N_DEV, SHARD_OUT, D_MODEL).astype(BF16)
        gc = jnp.stack([g["conv_w"][:, k * CONV_CH:(k + 1) * CONV_CH].reshape(CONV_W, N_DEV, SHARD_CONV)
                        .transpose(1, 0, 2) for k in range(2)], axis=1).reshape(N_DEV, 2 * CONV_W, SHARD_CONV)
        return [gi, go, gc]

    layer0 = assemble(0, *_comm_call("gather_layer0", "gather", shards(0)))
    loss_local, dx, grads, recv1 = _local_step(
        xb, tgt, layer0, lambda got: assemble(1, *got), length,
        fwd_comm=("gather", shards(1)), make_bwd_comm=lambda g1: ("scatter", grad_slabs(g1)))
    grad_x = dx[None]
    loss = lax.psum(loss_local, ("x", "y", "c"))

    small_g = dict(
        pre_norm=jnp.concatenate([grads[l]["pre_norm"] for l in range(DEPTH)], axis=0),
        post_norm=jnp.concatenate([grads[l]["post_norm"] for l in range(DEPTH)], axis=0),
        gdn_A_log=jnp.stack([grads[l]["gdn_A_log"][:, 0, 0] for l in range(DEPTH)]),
        gdn_dt_bias=jnp.stack([grads[l]["gdn_dt_bias"][:, 0, 0] for l in range(DEPTH)]),
        gdn_norm=jnp.concatenate([grads[l]["gdn_norm"] for l in range(DEPTH)], axis=0),
        ssd_conv_b=jnp.concatenate([grads[l]["conv_b"][:, CONV_CH:] for l in range(DEPTH)], axis=0),
        ssd_A_log=jnp.stack([grads[l]["ssd_A_log"].reshape(SSD_HEADS) for l in range(DEPTH)]),
        ssd_dt_bias=jnp.stack([grads[l]["ssd_dt_bias"].reshape(SSD_HEADS) for l in range(DEPTH)]),
        ssd_D=jnp.stack([grads[l]["ssd_D"].reshape(SSD_HEADS) for l in range(DEPTH)]),
        ssd_norm=jnp.concatenate([jnp.concatenate(grads[l]["ssd_norm"], axis=1) for l in range(DEPTH)], axis=0),
        ret_norm=jnp.concatenate([grads[l]["ret_norm"] for l in range(DEPTH)], axis=0))
    gs = _pack_small(small_g)
    gs8 = jnp.broadcast_to(gs[None], (N_DEV,) + gs.shape)
    *recv0, r_small = _comm_call("exchange_layer0", "scatter", grad_slabs(grads[0]) + [gs8])
    recv = [recv0, recv1]

    conv_w = lambda g_, s_, l: jnp.stack([g_[l], s_[l]], axis=0).reshape(2 * CONV_W, SHARD_CONV)
    o_in, o_out, o_conv = None, None, []
    for l in range(DEPTH):
        o_in = _adam(f"adam_w_in{l}", recv[l][0], w_in, m_w_in, v_w_in, 128, l, o_in)
        o_out = _adam(f"adam_w_out{l}", recv[l][1], w_out, m_w_out, v_w_out, 128, l, o_out)
        o_conv.append(_adam(f"adam_conv{l}", recv[l][2], conv_w(gdn_conv, ssd_conv, l),
                            conv_w(m_gdn_conv, m_ssd_conv, l), conv_w(v_gdn_conv, v_ssd_conv, l), 2 * CONV_W))
    ps_w, ps_m, ps_v = _pack_small(small_w), _pack_small(small_m), _pack_small(small_v)
    o_small = _adam("adam_small", r_small, ps_w, ps_m, ps_v, ps_w.shape[0])

    names = ["pre_norm", "post_norm", "w_in", "gdn_conv", "gdn_A_log", "gdn_dt_bias", "gdn_norm", "ssd_conv",
             "ssd_conv_b", "ssd_A_log", "ssd_dt_bias", "ssd_D", "ssd_norm", "ret_norm", "w_out"]
    outs = []
    for kind in range(4):
        d = _unpack_small(o_small[kind], small_w)
        cv = jnp.stack([o_conv[l][kind].reshape(2, CONV_W, SHARD_CONV) for l in range(DEPTH)], axis=1)
        d["w_in"] = o_in[kind]
        d["w_out"] = o_out[kind]
        d["gdn_conv"] = cv[0]
        d["ssd_conv"] = cv[1]
        outs.extend(d[n] for n in names)
    return (loss, grad_x, *outs)
```

```python
import functools
import math

import numpy as np
import jax
import jax.numpy as jnp
from jax import lax
from jax.experimental import pallas as pl
from jax.experimental.pallas import tpu as pltpu

F32 = jnp.float32
BF16 = jnp.bfloat16

D_MODEL = 1024
DEPTH = 2
CHUNK = 64
CONV_W = 4
EPS = 1e-6
N_DEV = 8

GDN_HEADS = 4
GDN_DK = 128
SSD_HEADS = 16
SSD_P = 64
SSD_N = 128
SSD_GROUPS = 2
SSD_PAIRS = SSD_HEADS // 2
PAIRS_PER_GROUP = SSD_PAIRS // SSD_GROUPS
RET_HEADS = 4
RET_DK = 128
ROPE_BASE = 10000.0
MIX_W = 2048
N_IN = 6680
SHARD_IN = N_IN // N_DEV
SHARD_OUT = MIX_W // N_DEV
CONV_CH = 1536
SHARD_CONV = CONV_CH // N_DEV

ADAM_LR = 0.001
ADAM_B1 = 0.9
ADAM_B2 = 0.999
ADAM_EPS = 1e-08
ADAM_WD = 0.01
ADAM_STEP = 10

LANE = 128
SUBLANE = 8
VMEM_LIMIT = 56 * 1024 * 1024

_ORIG = dict(gq=(0, 512), gk=(512, 512), gv=(1024, 512), gz=(1536, 512), gb=(2048, 4), ga=(2052, 4),
             sx=(2056, 1024), sB=(3080, 256), sC=(3336, 256), sz=(3592, 1024), sdt=(4616, 16),
             rq=(4632, 512), rk=(5144, 512), rv=(5656, 512), rg=(6168, 512))
_ORDER = ["gq", "gk", "gv", "sx", "sB", "sC", "gz", "sz", "rq", "rk", "rv", "rg", "gb", "ga", "sdt"]
_MINE = {}
_off = 0
for _n in _ORDER:
    _MINE[_n] = _off
    _off += _ORIG[_n][1]
N_USED = _off
N_PAD = 7168
CONV_ALL = 2 * CONV_CH
SMALL_OFF = _MINE["gb"]
CB = lambda name: _MINE[name] // LANE


def _cparams(sem, vmem=None):
    return pltpu.CompilerParams(dimension_semantics=sem, vmem_limit_bytes=vmem)


def _split_bf16(a):
    hi = a.astype(BF16)
    return hi, (a - hi.astype(F32)).astype(BF16)


def _make_mm():
    def raw(a, b, ca, cb):
        return lax.dot_general(a.astype(BF16), b.astype(BF16), (((ca,), (cb,)), ((), ())),
                               preferred_element_type=F32)

    @jax.custom_vjp
    def nn(a, b):
        return raw(a, b, 1, 0)

    @jax.custom_vjp
    def nt(a, b):
        return raw(a, b, 1, 1)

    @jax.custom_vjp
    def tn(a, b):
        return raw(a, b, 0, 0)

    nn.defvjp(lambda a, b: (raw(a, b, 1, 0), (a, b)), lambda r, g: (nt(g, r[1]), tn(r[0], g)))
    nt.defvjp(lambda a, b: (raw(a, b, 1, 1), (a, b)), lambda r, g: (nn(g, r[1]), tn(g, r[0])))
    tn.defvjp(lambda a, b: (raw(a, b, 0, 0), (a, b)), lambda r, g: (nt(r[1], g), nn(r[0], g)))
    return nn, nt, tn


_nn, _nt, _tn = _make_mm()


@jax.custom_vjp
def _swap_halves(t):
    return pltpu.roll(t, LANE // 2, 1)


_swap_halves.defvjp(lambda t: (pltpu.roll(t, LANE // 2, 1), None),
                    lambda _, g: (pltpu.roll(g, LANE // 2, 1),))


@jax.custom_vjp
def _split_rows(x):
    return tuple(x[i * CHUNK:(i + 1) * CHUNK] for i in range(x.shape[0] // CHUNK))


_split_rows.defvjp(lambda x: (tuple(x[i * CHUNK:(i + 1) * CHUNK] for i in range(x.shape[0] // CHUNK)), None),
                   lambda _, gs: (jnp.concatenate(gs, axis=0),))


def _dot3(a, b, ca, cb):
    dot = lambda x, y: lax.dot_general(x, y, (((ca,), (cb,)), ((), ())), preferred_element_type=F32)
    return dot(a[0], b[0]) + (dot(a[0], b[1]) + dot(a[1], b[0]))


@jax.custom_vjp
def _tri_inv(mats):
    return _tri_inv_impl(mats)


def _tri_inv_impl(mats):
    ii = lax.broadcasted_iota(jnp.int32, mats[0].shape, 0)
    jj = lax.broadcasted_iota(jnp.int32, mats[0].shape, 1)
    eye = jnp.where(ii == jj, 1.0, 0.0).astype(F32)
    ts = [eye - a for a in mats]
    ps = [_split_bf16(-a) for a in mats]
    for _ in range(int(math.log2(CHUNK)) - 1):
        ps = [_split_bf16(_dot3(p, p, 1, 0)) for p in ps]
        ts = [t + _dot3(_split_bf16(t), p, 1, 0) for t, p in zip(ts, ps)]
    return ts


def _tri_inv_bwd(ts, gs):
    tsp = [_split_bf16(t) for t in ts]
    xs = [_dot3(t, _split_bf16(g), 0, 0) for t, g in zip(tsp, gs)]
    return ([-_dot3(_split_bf16(x), t, 1, 1) for x, t in zip(xs, tsp)],)


_tri_inv.defvjp(lambda mats: (lambda ts: (ts, ts))(_tri_inv_impl(mats)), _tri_inv_bwd)


def _silu(x):
    return x * jax.nn.sigmoid(x)


@jax.custom_vjp
def _softplus(x):
    return jnp.maximum(x, 0.0) + jnp.log1p(jnp.exp(-jnp.abs(x)))


_softplus.defvjp(lambda x: (jnp.maximum(x, 0.0) + jnp.log1p(jnp.exp(-jnp.abs(x))), x),
                 lambda x, g: (g * jax.nn.sigmoid(x),))


def _rms(x, w):
    return x * lax.rsqrt(jnp.mean(x * x, axis=-1, keepdims=True) + EPS) * w


def _chunk_masks(n):
    ii = lax.broadcasted_iota(jnp.int32, (n, n), 0)
    jj = lax.broadcasted_iota(jnp.int32, (n, n), 1)
    return ii >= jj, ii > jj, ii == jj, ii <= jj


def _cumsum_col(g, causal, eye, upper):
    g_row = jnp.sum(jnp.where(eye, g, 0.0), axis=0, keepdims=True)
    col = jnp.sum(jnp.where(causal, g_row, 0.0), axis=1, keepdims=True)
    row = jnp.sum(jnp.where(upper, g, 0.0), axis=0, keepdims=True)
    return col, row


def _lane_col(block, lane):
    pick = lax.broadcasted_iota(jnp.int32, (1, block.shape[1]), 1) == lane
    return jnp.sum(jnp.where(pick, block, 0.0), axis=1, keepdims=True)


def _gdn_chunk(q, k, v, gz, sm, alog_row, dtb_row, nw, s):
    subs, nh = range(len(q)), len(q[0])
    c = q[0][0].shape[0]
    causal, _, eye, upper = _chunk_masks(c)
    stack = lambda xs: jnp.concatenate(xs, axis=0)
    per = lambda f: [[f(j, h) for h in range(nh)] for j in subs]
    qn = per(lambda j, h: q[j][h] * lax.rsqrt(jnp.sum(q[j][h] * q[j][h], axis=-1, keepdims=True) + EPS)
             * (GDN_DK ** -0.5))
    kn = per(lambda j, h: k[j][h] * lax.rsqrt(jnp.sum(k[j][h] * k[j][h], axis=-1, keepdims=True) + EPS))
    beta_blk = [jax.nn.sigmoid(sm[j]) for j in subs]
    g_blk = [-jnp.exp(alog_row) * _softplus(sm[j] + dtb_row) for j in subs]
    beta = per(lambda j, h: _lane_col(beta_blk[j], _MINE["gb"] - SMALL_OFF + h))
    g = per(lambda j, h: _lane_col(g_blk[j], _MINE["ga"] - SMALL_OFF + h))
    gcum = per(lambda j, h: _cumsum_col(g[j][h], causal, eye, upper)[0])
    eg = per(lambda j, h: jnp.exp(gcum[j][h]))
    glast = per(lambda j, h: jnp.sum(g[j][h], axis=0, keepdims=True))
    kb = per(lambda j, h: kn[j][h] * beta[j][h])
    n = nh * c
    ii = lax.broadcasted_iota(jnp.int32, (n, n), 0)
    jj = lax.broadcasted_iota(jnp.int32, (n, n), 1)
    sh = int(math.log2(c))
    same = lax.shift_right_logical(ii, sh) == lax.shift_right_logical(jj, sh)
    causal_bd = jnp.logical_and(same, ii >= jj)
    strict_bd = jnp.logical_and(same, ii > jj)
    gcum_all = [stack(gcum[j]) for j in subs]
    gcum_row = [jnp.sum(jnp.where(ii == jj, gcum_all[j], 0.0), axis=0, keepdims=True) for j in subs]
    decay = [jnp.where(causal_bd, jnp.exp(jnp.where(causal_bd, gcum_all[j] - gcum_row[j], 0.0)), 0.0) for j in subs]
    kn_all = [stack(kn[j]) for j in subs]
    a_low = [jnp.where(strict_bd, _nt(stack(kb[j]), kn_all[j]) * decay[j], 0.0) for j in subs]
    t = _tri_inv(a_low)
    u = [_nn(t[j], stack([v[j][h] * beta[j][h] for h in range(nh)])) for j in subs]
    w = [_split_rows(_nn(t[j], stack([kb[j][h] * eg[j][h] for h in range(nh)]))) for j in subs]
    attn = [_nt(stack(qn[j]), kn_all[j]) * decay[j] for j in subs]
    kdec = per(lambda j, h: kn[j][h] * jnp.exp(glast[j][h] - gcum[j][h]))
    on = []
    for j in subs:
        v_new_all = u[j] - stack([_nn(w[j][h], s[h]) for h in range(nh)])
        o = _split_rows(stack([_nn(qn[j][h] * eg[j][h], s[h]) for h in range(nh)]) + _nn(attn[j], v_new_all))
        v_new = _split_rows(v_new_all)
        s = [s[h] * jnp.exp(glast[j][h]) + _tn(kdec[j][h], v_new[h]) for h in range(nh)]
        on.append([_rms(o[h], nw) * _silu(gz[j][h]) for h in range(nh)])
    return on, s


def _ssd_chunk(lanes, x, bm, cm, sm, alog_row, dtb_row, d_row, hs):
    subs, pairs = range(len(x)), range(len(x[0]))
    c = x[0][0].shape[0]
    lane_i = lax.broadcasted_iota(jnp.int32, (c, LANE), 1)
    lane_lo = lane_i < SSD_P
    lane_lo1 = lax.broadcasted_iota(jnp.int32, (1, LANE), 1) < SSD_P
    causal2 = lax.broadcasted_iota(jnp.int32, (c, LANE), 0) >= jnp.bitwise_and(lane_i, c - 1)
    row_lo = lax.broadcasted_iota(jnp.int32, (LANE, 1), 0) < SSD_P
    eye2 = lax.broadcasted_iota(jnp.int32, (LANE, LANE), 0) == lax.broadcasted_iota(jnp.int32, (LANE, LANE), 1)
    to_row = lambda cols: jnp.sum(jnp.where(eye2, jnp.concatenate(cols, axis=0), 0.0), axis=0, keepdims=True)
    per = lambda f: [[f(j, p) for p in pairs] for j in subs]
    both = lambda blk, p: [_lane_col(blk, lanes[p][h]) for h in range(2)]
    dt_blk = [_softplus(sm[j] + dtb_row) for j in subs]
    a_blk = [dt_blk[j] * -jnp.exp(alog_row) for j in subs]
    alast_blk = [jnp.sum(a_blk[j], axis=0, keepdims=True) for j in subs]
    dt = per(lambda j, p: both(dt_blk[j], p))
    a = per(lambda j, p: both(a_blk[j], p))
    alast = per(lambda j, p: both(alast_blk[j], p))
    dp = [both(d_row, p) for p in pairs]
    a_row = per(lambda j, p: to_row(a[j][p]))
    part = per(lambda j, p: jnp.where(causal2, a_row[j][p], 0.0))
    acum = per(lambda j, p: [jnp.sum(jnp.where(lane_lo, part[j][p], 0.0), axis=1, keepdims=True),
                             jnp.sum(jnp.where(lane_lo, 0.0, part[j][p]), axis=1, keepdims=True)])
    acum_row = per(lambda j, p: to_row(acum[j][p]))
    acum_col = per(lambda j, p: jnp.where(lane_lo, acum[j][p][0], acum[j][p][1]))
    lmat = per(lambda j, p: jnp.where(causal2, jnp.exp(jnp.where(causal2, acum_col[j][p] - acum_row[j][p], 0.0)), 0.0))
    cb = [_nt(cm[j], jnp.concatenate([bm[j], bm[j]], axis=0)) for j in subs]
    xdt = per(lambda j, p: x[j][p] * jnp.where(lane_lo, dt[j][p][0], dt[j][p][1]))
    xdt_rows = per(lambda j, p: jnp.concatenate([jnp.where(lane_lo, xdt[j][p], 0.0),
                                                 jnp.where(lane_lo, 0.0, xdt[j][p])], axis=0))
    intra = per(lambda j, p: _nn(cb[j] * lmat[j][p], xdt_rows[j][p]))
    skip = per(lambda j, p: x[j][p] * jnp.where(lane_lo1, dp[p][0], dp[p][1]))
    eacc = per(lambda j, p: jnp.exp(acum_col[j][p]))
    wdec = per(lambda j, p: jnp.exp(jnp.where(lane_lo, alast[j][p][0] - acum[j][p][0],
                                              alast[j][p][1] - acum[j][p][1])))
    scale = per(lambda j, p: jnp.exp(jnp.where(row_lo, alast[j][p][0], alast[j][p][1])))
    upd = per(lambda j, p: _tn(xdt[j][p] * wdec[j][p], bm[j]))
    y = []
    for j in subs:
        y.append([skip[j][p] + intra[j][p] + _nt(cm[j], hs[p]) * eacc[j][p] for p in pairs])
        hs = [hs[p] * scale[j][p] + upd[j][p] for p in pairs]
    return y, hs


def _ret_chunk(rq, rk, rv, rg, cos2, sin2, dmat, qdec, kdec, cdec, nw, r):
    subs, hs = range(len(rq)), range(len(rq[0]))
    per = lambda f: [[f(j, h) for h in hs] for j in subs]
    q = per(lambda j, h: rq[j][h] * cos2[j] + _swap_halves(rq[j][h]) * sin2[j])
    k = per(lambda j, h: (rk[j][h] * cos2[j] + _swap_halves(rk[j][h]) * sin2[j]) * (RET_DK ** -0.5))
    s = per(lambda j, h: _nt(q[j][h], k[j][h]) * dmat[h])
    intra = per(lambda j, h: _nn(s[j][h], rv[j][h]))
    upd = per(lambda j, h: _tn(k[j][h] * kdec[h], rv[j][h]))
    gate = per(lambda j, h: _silu(rg[j][h]))
    on = []
    for j in subs:
        on.append([_rms(intra[j][h] + _nn(q[j][h], r[h]) * qdec[h], nw) * gate[j][h] for h in hs])
        r = [r[h] * cdec[h] + upd[j][h] for h in hs]
    return on, r


MM_TILE = 1024


def _tile(n, pref=MM_TILE):
    return pref if n % pref == 0 else n


def _matmul(name, a, b, mode, m, n, k, tk=None):
    tm, tn = _tile(m), _tile(n)
    tk = _tile(k) if tk is None else tk
    nk = k // tk
    ca, cb = {"nn": (1, 0), "nt": (1, 1), "tn": (0, 0)}[mode]

    def body(a_ref, b_ref, o_ref):
        part = lax.dot_general(a_ref[...].astype(BF16), b_ref[...].astype(BF16),
                               (((ca,), (cb,)), ((), ())), preferred_element_type=F32)
        if nk == 1:
            o_ref[...] = part
        else:
            kk = pl.program_id(2)

            @pl.when(kk == 0)
            def _():
                o_ref[...] = part

            @pl.when(kk > 0)
            def _():
                o_ref[...] += part

    a_spec = pl.BlockSpec((tk, tm), lambda i, j, kk: (kk, i)) if mode == "tn" else pl.BlockSpec((tm, tk), lambda i, j, kk: (i, kk))
    b_spec = pl.BlockSpec((tn, tk), lambda i, j, kk: (j, kk)) if mode == "nt" else pl.BlockSpec((tk, tn), lambda i, j, kk: (kk, j))
    return pl.pallas_call(
        body, name=name, grid=(m // tm, n // tn, nk),
        in_specs=[a_spec, b_spec], out_specs=pl.BlockSpec((tm, tn), lambda i, j, kk: (i, j)),
        out_shape=jax.ShapeDtypeStruct((m, n), F32),
        compiler_params=_cparams(("parallel", "parallel", "arbitrary"), VMEM_LIMIT),
    )(a, b)


def _rowmap(name, fn, length, tm, rows, params, out_ws=(), acc_ws=(), out_dtypes=None, cts=None, nd_rows=(),
            d_dtypes=None):
    nt_ = length // tm
    n_r, n_nd, n_p = len(rows), len(nd_rows), len(params)
    rspec = lambda bw, cbk: pl.BlockSpec((tm, bw), lambda i: (i, cbk))
    pspec = lambda w: pl.BlockSpec((1, w), lambda i: (0, 0))
    in_arrays = [r[0] for r in rows] + [r[0] for r in nd_rows] + list(params)
    in_specs = [rspec(r[1], r[2]) for r in rows] + [rspec(r[1], r[2]) for r in nd_rows] + [pspec(p.shape[1]) for p in params]

    def load(refs):
        return [r[...].astype(F32) for r in refs]

    if cts is None:
        n_o, n_a = len(out_ws), len(acc_ws)
        out_dtypes_ = out_dtypes or [F32] * n_o

        def body(*refs):
            ins = load(refs[:n_r + n_nd + n_p])
            outs, accs = fn(*ins)
            o_refs = refs[n_r + n_nd + n_p:]
            for o_ref, o in zip(o_refs[:n_o], outs):
                o_ref[...] = o.astype(o_ref.dtype)
            first = pl.program_id(0) == 0
            for a_ref, acc in zip(o_refs[n_o:], accs):
                @pl.when(first)
                def _(a_ref=a_ref):
                    a_ref[...] = jnp.zeros_like(a_ref)
                a_ref[...] += acc

        res = pl.pallas_call(
            body, name=name, grid=(nt_,), in_specs=in_specs,
            out_specs=[rspec(w, 0) for w in out_ws] + [pspec(w) for w in acc_ws],
            out_shape=[jax.ShapeDtypeStruct((length, w), dt) for w, dt in zip(out_ws, out_dtypes_)]
                      + [jax.ShapeDtypeStruct((1, w), F32) for w in acc_ws],
            compiler_params=_cparams(("arbitrary",), VMEM_LIMIT),
        )(*in_arrays)
        return res[:n_o], res[n_o:]

    n_c = len(cts)

    def body(*refs):
        ins = load(refs[:n_r + n_nd + n_p])
        ct_vals = load(refs[n_r + n_nd + n_p:n_r + n_nd + n_p + n_c])
        nd = ins[n_r:n_r + n_nd]
        f = lambda rs, ps: fn(*rs, *nd, *ps)[0]
        _, vjp = jax.vjp(f, ins[:n_r], ins[n_r + n_nd:])
        d_rows, d_params = vjp(tuple(ct_vals))
        o_refs = refs[n_r + n_nd + n_p + n_c:]
        for o_ref, d in zip(o_refs[:n_r], d_rows):
            o_ref[...] = d.astype(o_ref.dtype)
        first = pl.program_id(0) == 0
        for a_ref, d in zip(o_refs[n_r:], d_params):
            @pl.when(first)
            def _(a_ref=a_ref):
                a_ref[...] = jnp.zeros_like(a_ref)
            a_ref[...] += d

    res = pl.pallas_call(
        body, name=name, grid=(nt_,),
        in_specs=in_specs + [rspec(c[1], c[2]) for c in cts],
        out_specs=[rspec(r[1], 0) for r in rows] + [pspec(p.shape[1]) for p in params],
        out_shape=[jax.ShapeDtypeStruct((length, r[1]), dt) for r, dt in zip(rows, d_dtypes or [F32] * n_r)]
                  + [jax.ShapeDtypeStruct((1, p.shape[1]), F32) for p in params],
        compiler_params=_cparams(("arbitrary",), VMEM_LIMIT),
    )(*in_arrays, *[c[0] for c in cts])
    return res[:n_r], res[n_r:]


def _conv_shift(ext, k, tm, forward):
    s = CONV_W - 1 - k
    if forward:
        rolled = ext if s == 0 else pltpu.roll(ext, s, 0)
        return rolled[SUBLANE:, :]
    rolled = ext if s == 0 else pltpu.roll(ext, tm + SUBLANE - s, 0)
    return rolled[:tm, :]


def _conv_pre(x_ref, halo_ref, w_ref, b_ref, first, tm):
    halo = jnp.where(first, 0.0, halo_ref[...])
    ext = jnp.concatenate([halo, x_ref[...]], axis=0)
    w = w_ref[...]
    pre = b_ref[...] + jnp.zeros_like(x_ref[...])
    taps = []
    for k in range(CONV_W):
        tap = _conv_shift(ext, k, tm, True)
        taps.append(tap)
        pre = pre + tap * w[k:k + 1, :]
    return pre, taps


def _conv_fwd(proj, w, b, length, tm, tc):
    hb = tm // SUBLANE

    def body(x_ref, halo_ref, w_ref, b_ref, o_ref):
        pre, _ = _conv_pre(x_ref, halo_ref, w_ref, b_ref, pl.program_id(0) == 0, tm)
        o_ref[...] = _silu(pre)

    return pl.pallas_call(
        body, name="conv_fwd", grid=(length // tm, CONV_ALL // tc),
        in_specs=[pl.BlockSpec((tm, tc), lambda i, j: (i, j)),
                  pl.BlockSpec((SUBLANE, tc), lambda i, j: (jnp.maximum(i * hb - 1, 0), j)),
                  pl.BlockSpec((CONV_W, tc), lambda i, j: (0, j)),
                  pl.BlockSpec((1, tc), lambda i, j: (0, j))],
        out_specs=pl.BlockSpec((tm, tc), lambda i, j: (i, j)),
        out_shape=jax.ShapeDtypeStruct((length, CONV_ALL), F32),
        compiler_params=_cparams(("parallel", "parallel"), VMEM_LIMIT),
    )(proj, proj, w, b)


def _conv_bwd_pre(proj, w, b, dxc, length, tm, tc):
    hb = tm // SUBLANE

    def body(x_ref, halo_ref, w_ref, b_ref, dy_ref, dpre_ref, dw_ref, db_ref):
        i = pl.program_id(1)
        pre, taps = _conv_pre(x_ref, halo_ref, w_ref, b_ref, i == 0, tm)
        sg = jax.nn.sigmoid(pre)
        dpre = dy_ref[...] * (sg * (1.0 + pre * (1.0 - sg)))
        dpre_ref[...] = dpre

        @pl.when(i == 0)
        def _():
            dw_ref[...] = jnp.zeros_like(dw_ref)
            db_ref[...] = jnp.zeros_like(db_ref)

        for k in range(CONV_W):
            dw_ref[k:k + 1, :] += jnp.sum(dpre * taps[k], axis=0, keepdims=True)
        db_ref[...] += jnp.sum(dpre, axis=0, keepdims=True)

    return pl.pallas_call(
        body, name="conv_bwd_pre", grid=(CONV_ALL // tc, length // tm),
        in_specs=[pl.BlockSpec((tm, tc), lambda j, i: (i, j)),
                  pl.BlockSpec((SUBLANE, tc), lambda j, i: (jnp.maximum(i * hb - 1, 0), j)),
                  pl.BlockSpec((CONV_W, tc), lambda j, i: (0, j)),
                  pl.BlockSpec((1, tc), lambda j, i: (0, j)),
                  pl.BlockSpec((tm, tc), lambda j, i: (i, j))],
        out_specs=[pl.BlockSpec((tm, tc), lambda j, i: (i, j)),
                   pl.BlockSpec((CONV_W, tc), lambda j, i: (0, j)),
                   pl.BlockSpec((1, tc), lambda j, i: (0, j))],
        out_shape=[jax.ShapeDtypeStruct((length, CONV_ALL), F32),
                   jax.ShapeDtypeStruct((CONV_W, CONV_ALL), F32),
                   jax.ShapeDtypeStruct((1, CONV_ALL), F32)],
        compiler_params=_cparams(("parallel", "arbitrary"), VMEM_LIMIT),
    )(proj, proj, w, b, dxc)


def _conv_bwd_x(dpre, w, length, tm, tc):
    hb = tm // SUBLANE
    n_t = length // tm
    last_blk = length // SUBLANE - 1

    def body(d_ref, halo_ref, w_ref, o_ref):
        halo = jnp.where(pl.program_id(0) == n_t - 1, 0.0, halo_ref[...])
        ext = jnp.concatenate([d_ref[...], halo], axis=0)
        w = w_ref[...]
        acc = jnp.zeros_like(d_ref[...])
        for k in range(CONV_W):
            acc = acc + _conv_shift(ext, k, tm, False) * w[k:k + 1, :]
        o_ref[...] = acc.astype(o_ref.dtype)

    return pl.pallas_call(
        body, name="conv_bwd_x", grid=(n_t, CONV_ALL // tc),
        in_specs=[pl.BlockSpec((tm, tc), lambda i, j: (i, j)),
                  pl.BlockSpec((SUBLANE, tc), lambda i, j: (jnp.minimum((i + 1) * hb, last_blk), j)),
                  pl.BlockSpec((CONV_W, tc), lambda i, j: (0, j))],
        out_specs=pl.BlockSpec((tm, tc), lambda i, j: (i, j)),
        out_shape=jax.ShapeDtypeStruct((length, CONV_ALL), BF16),
        compiler_params=_cparams(("parallel", "parallel"), VMEM_LIMIT),
    )(dpre, dpre, w)


GDN_CHUNKS_PER_STEP = 4


def _gdn_steps(length):
    nc = length // CHUNK
    gs = GDN_CHUNKS_PER_STEP if nc % GDN_CHUNKS_PER_STEP == 0 else 1
    return gs, nc // gs, gs * CHUNK


def _gdn_tiles(ref, gs):
    return [[ref[j * CHUNK:(j + 1) * CHUNK, h * LANE:(h + 1) * LANE] for h in range(GDN_HEADS)] for j in range(gs)]


def _chunk_rows(ref, gs):
    return [ref[j * CHUNK:(j + 1) * CHUNK, :] for j in range(gs)]


def _gdn_fwd(xc, proj, alog, dtb, nw, length, comm=None):
    gs, steps, rows = _gdn_steps(length)
    h_ = GDN_HEADS
    hw = h_ * LANE
    n_comm = len(comm[1]) if comm else 0

    def body(*refs):
        q_ref, k_ref, v_ref, gz_ref, sm_ref, al_ref, dt_ref, nw_ref = refs[:8]
        c_in = refs[8:8 + n_comm]
        on_ref, st_ref = refs[8 + n_comm:10 + n_comm]
        c_out = refs[10 + n_comm:10 + 2 * n_comm]
        s_ref = refs[10 + 2 * n_comm]
        sems = refs[11 + 2 * n_comm:]

        @pl.when(pl.program_id(0) == 0)
        def _():
            s_ref[...] = jnp.zeros_like(s_ref)
            if comm:
                _comm_plan(comm[0], c_in, c_out, *sems)[0]()

        st_ref[...] = s_ref[...]
        lead = lambda ref: [ref[h] for h in range(h_)]
        on, s_new = _gdn_chunk(_gdn_tiles(q_ref, gs), _gdn_tiles(k_ref, gs), _gdn_tiles(v_ref, gs),
                               _gdn_tiles(gz_ref, gs), _chunk_rows(sm_ref, gs), al_ref[...], dt_ref[...],
                               nw_ref[...], lead(s_ref))
        for h in range(h_):
            for j in range(gs):
                on_ref[j * CHUNK:(j + 1) * CHUNK, h * LANE:(h + 1) * LANE] = on[j][h].astype(on_ref.dtype)
            s_ref[h] = s_new[h]

        if comm:
            @pl.when(pl.program_id(0) == steps - 1)
            def _():
                _comm_plan(comm[0], c_in, c_out, *sems)[1]()

    blk = lambda name: pl.BlockSpec((rows, hw), lambda c: (c, _MINE[name] // hw))
    one = pl.BlockSpec((1, LANE), lambda c: (0, 0))
    hbm = pl.BlockSpec(memory_space=pl.ANY)
    return pl.pallas_call(
        body, name="gdn_fwd_" + comm[0] if comm else "gdn_fwd", grid=(steps,),
        in_specs=[blk("gq"), blk("gk"), blk("gv"), blk("gz"),
                  pl.BlockSpec((rows, LANE), lambda c: (c, SMALL_OFF // LANE)), one, one,
                  pl.BlockSpec((1, LANE), lambda c: (0, 0))] + [hbm] * n_comm,
        out_specs=[pl.BlockSpec((rows, hw), lambda c: (c, 0)),
                   pl.BlockSpec((h_, None, LANE, LANE), lambda c: (0, c, 0, 0))] + [hbm] * n_comm,
        out_shape=[jax.ShapeDtypeStruct((length, hw), BF16),
                   jax.ShapeDtypeStruct((h_, steps, LANE, LANE), F32)]
                  + (_comm_out_shapes(*comm) if comm else []),
        scratch_shapes=[pltpu.VMEM((h_, LANE, LANE), F32)] + (_comm_scratch(n_comm) if comm else []),
        compiler_params=_cparams(("arbitrary",), VMEM_LIMIT),
    )(xc, xc, xc, proj, proj, alog, dtb, nw, *(comm[1] if comm else []))


def _gdn_bwd(xc, proj, alog, dtb, nw, st, dmix, length, comm=None):
    gs, steps, rows = _gdn_steps(length)
    h_ = GDN_HEADS
    n_comm = len(comm[1]) if comm else 0
    n_in, n_out = 10, 8

    def body(*refs):
        q_ref, k_ref, v_ref, gz_ref, sm_ref, al_ref, dt_ref, nw_ref, st_ref, do_ref = refs[:n_in]
        c_in = refs[n_in:n_in + n_comm]
        o0 = n_in + n_comm
        dq_ref, dk_ref, dv_ref, dgz_ref, dsm_ref, dal_ref, ddt_ref, dnw_ref = refs[o0:o0 + n_out]
        c_out = refs[o0 + n_out:o0 + n_out + n_comm]
        ds_ref = refs[o0 + n_out + n_comm]
        sems = refs[o0 + n_out + n_comm + 1:]

        @pl.when(pl.program_id(0) == 0)
        def _():
            ds_ref[...] = jnp.zeros_like(ds_ref)
            dal_ref[...] = jnp.zeros_like(dal_ref)
            ddt_ref[...] = jnp.zeros_like(ddt_ref)
            dnw_ref[...] = jnp.zeros_like(dnw_ref)
            if comm:
                _comm_plan(comm[0], c_in, c_out, *sems)[0]()

        lead = lambda ref: [ref[h] for h in range(h_)]
        ins = (_gdn_tiles(q_ref, gs), _gdn_tiles(k_ref, gs), _gdn_tiles(v_ref, gs), _gdn_tiles(gz_ref, gs),
               _chunk_rows(sm_ref, gs), al_ref[...], dt_ref[...], nw_ref[...], lead(st_ref))
        _, vjp = jax.vjp(_gdn_chunk, *ins)
        dq, dk, dv, dgz, dsm, dal, ddt, dnw, ds = vjp((_gdn_tiles(do_ref, gs), lead(ds_ref)))
        for j in range(gs):
            rws = slice(j * CHUNK, (j + 1) * CHUNK)
            for h in range(h_):
                cols = slice(h * LANE, (h + 1) * LANE)
                dq_ref[rws, cols] = dq[j][h]
                dk_ref[rws, cols] = dk[j][h]
                dv_ref[rws, cols] = dv[j][h]
                dgz_ref[rws, cols] = dgz[j][h].astype(dgz_ref.dtype)
            dsm_ref[rws, :] = dsm[j]
        for h in range(h_):
            ds_ref[h] = ds[h]
        dal_ref[...] += dal
        ddt_ref[...] += ddt
        dnw_ref[...] += dnw

        if comm:
            @pl.when(pl.program_id(0) == steps - 1)
            def _():
                _comm_plan(comm[0], c_in, c_out, *sems)[1]()

    hw = h_ * LANE
    r = lambda c: steps - 1 - c
    blk = lambda name: pl.BlockSpec((rows, hw), lambda c: (r(c), _MINE[name] // hw))
    one = pl.BlockSpec((1, LANE), lambda c: (0, 0))
    oblk = pl.BlockSpec((rows, hw), lambda c: (r(c), 0))
    big = jax.ShapeDtypeStruct((length, hw), F32)
    hbm = pl.BlockSpec(memory_space=pl.ANY)
    return pl.pallas_call(
        body, name="gdn_bwd_" + comm[0] if comm else "gdn_bwd", grid=(steps,),
        in_specs=[blk("gq"), blk("gk"), blk("gv"), blk("gz"),
                  pl.BlockSpec((rows, LANE), lambda c: (r(c), SMALL_OFF // LANE)), one, one,
                  pl.BlockSpec((1, LANE), lambda c: (0, 0)),
                  pl.BlockSpec((h_, None, LANE, LANE), lambda c: (0, r(c), 0, 0)),
                  oblk] + [hbm] * n_comm,
        out_specs=[oblk, oblk, oblk, oblk, pl.BlockSpec((rows, LANE), lambda c: (r(c), 0)), one, one,
                   pl.BlockSpec((1, LANE), lambda c: (0, 0))] + [hbm] * n_comm,
        out_shape=[big, big, big, jax.ShapeDtypeStruct((length, hw), BF16),
                   jax.ShapeDtypeStruct((length, LANE), F32),
                   jax.ShapeDtypeStruct((1, LANE), F32), jax.ShapeDtypeStruct((1, LANE), F32),
                   jax.ShapeDtypeStruct((1, LANE), F32)] + (_comm_out_shapes(*comm) if comm else []),
        scratch_shapes=[pltpu.VMEM((h_, LANE, LANE), F32)] + (_comm_scratch(n_comm) if comm else []),
        compiler_params=_cparams(("arbitrary",), VMEM_LIMIT),
    )(xc, xc, xc, proj, proj, alog, dtb, nw, st, dmix, *(comm[1] if comm else []))


SSD_CHUNKS_PER_STEP = 4


def _ssd_steps(length):
    nc = length // CHUNK
    gs = SSD_CHUNKS_PER_STEP if nc % SSD_CHUNKS_PER_STEP == 0 else 1
    return gs, nc // gs, gs * CHUNK


def _ssd_lanes(g):
    base = _MINE["sdt"] - SMALL_OFF
    return [[base + (g * PAIRS_PER_GROUP + p) * 2 + j for j in range(2)] for p in range(PAIRS_PER_GROUP)]


def _ssd_tiles(ref, gs):
    return [[ref[j * CHUNK:(j + 1) * CHUNK, p * LANE:(p + 1) * LANE] for p in range(PAIRS_PER_GROUP)]
            for j in range(gs)]


def _ssd_fwd(xc, proj, alog_row, dtb_row, d_row, length):
    gs, steps, rows = _ssd_steps(length)
    ppg = PAIRS_PER_GROUP
    gw = ppg * LANE

    def body(x_ref, b_ref, c_ref, sm_ref, al_ref, db_ref, dp_ref, y_ref, st_ref, hs_ref):
        @pl.when(pl.program_id(1) == 0)
        def _():
            hs_ref[...] = jnp.zeros_like(hs_ref)

        st_ref[...] = hs_ref[...]
        y, hs_new = _ssd_chunk(_ssd_lanes(pl.program_id(0)), _ssd_tiles(x_ref, gs), _chunk_rows(b_ref, gs),
                               _chunk_rows(c_ref, gs), _chunk_rows(sm_ref, gs), al_ref[...], db_ref[...],
                               dp_ref[...], [hs_ref[p] for p in range(ppg)])
        for p in range(ppg):
            for j in range(gs):
                y_ref[j * CHUNK:(j + 1) * CHUNK, p * LANE:(p + 1) * LANE] = y[j][p]
            hs_ref[p] = hs_new[p]

    one = pl.BlockSpec((1, LANE), lambda g, c: (0, 0))
    return pl.pallas_call(
        body, name="ssd_fwd", grid=(SSD_GROUPS, steps),
        in_specs=[pl.BlockSpec((rows, gw), lambda g, c: (c, _MINE["sx"] // gw + g)),
                  pl.BlockSpec((rows, LANE), lambda g, c: (c, CB("sB") + g)),
                  pl.BlockSpec((rows, LANE), lambda g, c: (c, CB("sC") + g)),
                  pl.BlockSpec((rows, LANE), lambda g, c: (c, SMALL_OFF // LANE)),
                  one, one, one],
        out_specs=[pl.BlockSpec((rows, gw), lambda g, c: (c, g)),
                   pl.BlockSpec((None, ppg, None, LANE, SSD_N), lambda g, c: (g, 0, c, 0, 0))],
        out_shape=[jax.ShapeDtypeStruct((length, SSD_GROUPS * gw), F32),
                   jax.ShapeDtypeStruct((SSD_GROUPS, ppg, steps, LANE, SSD_N), F32)],
        scratch_shapes=[pltpu.VMEM((ppg, LANE, SSD_N), F32)],
        compiler_params=_cparams(("parallel", "arbitrary"), VMEM_LIMIT),
    )(xc, xc, xc, proj, alog_row, dtb_row, d_row)


def _ssd_bwd(xc, proj, alog_row, dtb_row, d_row, st, dy, length):
    gs, steps, rows = _ssd_steps(length)
    ppg = PAIRS_PER_GROUP
    gw = ppg * LANE

    def body(x_ref, b_ref, c_ref, sm_ref, al_ref, db_ref, dp_ref, st_ref, dy_ref,
             dx_ref, dbm_ref, dcm_ref, dsm_ref, dal_ref, ddb_ref, ddp_ref, dhs_ref):
        @pl.when(pl.program_id(1) == 0)
        def _():
            dhs_ref[...] = jnp.zeros_like(dhs_ref)
            dal_ref[...] = jnp.zeros_like(dal_ref)
            ddb_ref[...] = jnp.zeros_like(ddb_ref)
            ddp_ref[...] = jnp.zeros_like(ddp_ref)

        pairs = range(ppg)
        ins = (_ssd_tiles(x_ref, gs), _chunk_rows(b_ref, gs), _chunk_rows(c_ref, gs), _chunk_rows(sm_ref, gs),
               al_ref[...], db_ref[...], dp_ref[...], [st_ref[p] for p in pairs])
        _, vjp = jax.vjp(functools.partial(_ssd_chunk, _ssd_lanes(pl.program_id(0))), *ins)
        dx, dbm, dcm, dsm, dal, ddb, ddp, dhs = vjp((_ssd_tiles(dy_ref, gs), [dhs_ref[p] for p in pairs]))
        for j in range(gs):
            rws = slice(j * CHUNK, (j + 1) * CHUNK)
            for p in pairs:
                dx_ref[rws, p * LANE:(p + 1) * LANE] = dx[j][p]
            dbm_ref[rws, :] = dbm[j]
            dcm_ref[rws, :] = dcm[j]
            dsm_ref[rws, :] = dsm[j]
        for p in pairs:
            dhs_ref[p] = dhs[p]
        dal_ref[...] += dal
        ddb_ref[...] += ddb
        ddp_ref[...] += ddp

    r = lambda c: steps - 1 - c
    one = pl.BlockSpec((1, LANE), lambda g, c: (0, 0))
    acc = pl.BlockSpec((None, 1, LANE), lambda g, c: (g, 0, 0))
    grp = pl.BlockSpec((rows, LANE), lambda g, c: (r(c), g))
    prm = jax.ShapeDtypeStruct((SSD_GROUPS, 1, LANE), F32)
    return pl.pallas_call(
        body, name="ssd_bwd", grid=(SSD_GROUPS, steps),
        in_specs=[pl.BlockSpec((rows, gw), lambda g, c: (r(c), _MINE["sx"] // gw + g)),
                  pl.BlockSpec((rows, LANE), lambda g, c: (r(c), CB("sB") + g)),
                  pl.BlockSpec((rows, LANE), lambda g, c: (r(c), CB("sC") + g)),
                  pl.BlockSpec((rows, LANE), lambda g, c: (r(c), SMALL_OFF // LANE)),
                  one, one, one,
                  pl.BlockSpec((None, ppg, None, LANE, SSD_N), lambda g, c: (g, 0, r(c), 0, 0)),
                  pl.BlockSpec((rows, gw), lambda g, c: (r(c), g))],
        out_specs=[pl.BlockSpec((rows, gw), lambda g, c: (r(c), g)), grp, grp,
                   pl.BlockSpec((None, rows, LANE), lambda g, c: (g, r(c), 0)), acc, acc, acc],
        out_shape=[jax.ShapeDtypeStruct((length, SSD_GROUPS * gw), F32),
                   jax.ShapeDtypeStruct((length, SSD_GROUPS * SSD_N), F32),
                   jax.ShapeDtypeStruct((length, SSD_GROUPS * SSD_N), F32),
                   jax.ShapeDtypeStruct((SSD_GROUPS, length, LANE), F32), prm, prm, prm],
        scratch_shapes=[pltpu.VMEM((ppg, LANE, SSD_N), F32)],
        compiler_params=_cparams(("parallel", "arbitrary"), VMEM_LIMIT),
    )(xc, xc, xc, proj, alog_row, dtb_row, d_row, st, dy)


def _ret_fwd(proj, cos2, sin2, dmat, qdec, kdec, cdec, nw, length):
    gs, steps, rows = _gdn_steps(length)
    h_ = RET_HEADS

    def body(q_ref, k_ref, v_ref, g_ref, cos_ref, sin_ref, dm_ref, qd_ref, kd_ref, cd_ref, nw_ref,
             on_ref, st_ref, r_ref):
        @pl.when(pl.program_id(0) == 0)
        def _():
            r_ref[...] = jnp.zeros_like(r_ref)

        st_ref[...] = r_ref[...]
        lead = lambda ref: [ref[h] for h in range(h_)]
        on, r_new = _ret_chunk(_gdn_tiles(q_ref, gs), _gdn_tiles(k_ref, gs), _gdn_tiles(v_ref, gs),
                               _gdn_tiles(g_ref, gs), _chunk_rows(cos_ref, gs), _chunk_rows(sin_ref, gs),
                               lead(dm_ref), lead(qd_ref), lead(kd_ref), lead(cd_ref), nw_ref[...], lead(r_ref))
        for h in range(h_):
            for j in range(gs):
                on_ref[j * CHUNK:(j + 1) * CHUNK, h * LANE:(h + 1) * LANE] = on[j][h].astype(on_ref.dtype)
            r_ref[h] = r_new[h]

    hw = h_ * LANE
    blk = lambda name: pl.BlockSpec((rows, hw), lambda c: (c, _MINE[name] // hw))
    tab = pl.BlockSpec((rows, LANE), lambda c: (c, 0))
    full = lambda *s: pl.BlockSpec(s, lambda c: (0,) * len(s))
    return pl.pallas_call(
        body, name="ret_fwd", grid=(steps,),
        in_specs=[blk("rq"), blk("rk"), blk("rv"), blk("rg"), tab, tab,
                  full(h_, CHUNK, CHUNK), full(h_, CHUNK, 1), full(h_, CHUNK, 1), full(h_, 1, 1), full(1, LANE)],
        out_specs=[pl.BlockSpec((rows, hw), lambda c: (c, 0)),
                   pl.BlockSpec((h_, None, LANE, LANE), lambda c: (0, c, 0, 0))],
        out_shape=[jax.ShapeDtypeStruct((length, hw), BF16),
                   jax.ShapeDtypeStruct((h_, steps, LANE, LANE), F32)],
        scratch_shapes=[pltpu.VMEM((h_, LANE, LANE), F32)],
        compiler_params=_cparams(("arbitrary",), VMEM_LIMIT),
    )(proj, proj, proj, proj, cos2, sin2, dmat, qdec, kdec, cdec, nw)


def _ret_bwd(proj, cos2, sin2, dmat, qdec, kdec, cdec, nw, st, dmix, length):
    gs, steps, rows = _gdn_steps(length)
    h_ = RET_HEADS

    def body(q_ref, k_ref, v_ref, g_ref, cos_ref, sin_ref, dm_ref, qd_ref, kd_ref, cd_ref, nw_ref, st_ref,
             do_ref, dq_ref, dk_ref, dv_ref, dg_ref, dnw_ref, dr_ref):
        @pl.when(pl.program_id(0) == 0)
        def _():
            dr_ref[...] = jnp.zeros_like(dr_ref)
            dnw_ref[...] = jnp.zeros_like(dnw_ref)

        lead = lambda ref: [ref[h] for h in range(h_)]
        consts = (_chunk_rows(cos_ref, gs), _chunk_rows(sin_ref, gs), lead(dm_ref), lead(qd_ref), lead(kd_ref),
                  lead(cd_ref))
        f = lambda q, k, v, g, w_, r_: _ret_chunk(q, k, v, g, *consts, w_, r_)
        _, vjp = jax.vjp(f, _gdn_tiles(q_ref, gs), _gdn_tiles(k_ref, gs), _gdn_tiles(v_ref, gs),
                         _gdn_tiles(g_ref, gs), nw_ref[...], lead(st_ref))
        dq, dk, dv, dg, dnw, dr = vjp((_gdn_tiles(do_ref, gs), lead(dr_ref)))
        for h in range(h_):
            cols = slice(h * LANE, (h + 1) * LANE)
            for j in range(gs):
                rws = slice(j * CHUNK, (j + 1) * CHUNK)
                dq_ref[rws, cols] = dq[j][h].astype(dq_ref.dtype)
                dk_ref[rws, cols] = dk[j][h].astype(dk_ref.dtype)
                dv_ref[rws, cols] = dv[j][h].astype(dv_ref.dtype)
                dg_ref[rws, cols] = dg[j][h].astype(dg_ref.dtype)
            dr_ref[h] = dr[h]
        dnw_ref[...] += dnw

    hw = h_ * LANE
    r = lambda c: steps - 1 - c
    blk = lambda name: pl.BlockSpec((rows, hw), lambda c: (r(c), _MINE[name] // hw))
    tab = pl.BlockSpec((rows, LANE), lambda c: (r(c), 0))
    full = lambda *s: pl.BlockSpec(s, lambda c: (0,) * len(s))
    oblk = pl.BlockSpec((rows, hw), lambda c: (r(c), 0))
    big = jax.ShapeDtypeStruct((length, hw), BF16)
    return pl.pallas_call(
        body, name="ret_bwd", grid=(steps,),
        in_specs=[blk("rq"), blk("rk"), blk("rv"), blk("rg"), tab, tab,
                  full(h_, CHUNK, CHUNK), full(h_, CHUNK, 1), full(h_, CHUNK, 1), full(h_, 1, 1), full(1, LANE),
                  pl.BlockSpec((h_, None, LANE, LANE), lambda c: (0, r(c), 0, 0)),
                  pl.BlockSpec((rows, hw), lambda c: (r(c), (MIX_W - hw) // hw))],
        out_specs=[oblk, oblk, oblk, oblk, full(1, LANE)],
        out_shape=[big, big, big, big, jax.ShapeDtypeStruct((1, LANE), F32)],
        scratch_shapes=[pltpu.VMEM((h_, LANE, LANE), F32)],
        compiler_params=_cparams(("arbitrary",), VMEM_LIMIT),
    )(proj, proj, proj, proj, cos2, sin2, dmat, qdec, kdec, cdec, nw, st, dmix)


def _mesh_pos():
    x, y, c = lax.axis_index("x"), lax.axis_index("y"), lax.axis_index("c")
    return x, y, c, 4 * x + 2 * y + c


def _peer(x, y, c, mask):
    return (x ^ ((mask >> 2) & 1), y ^ ((mask >> 1) & 1), c ^ (mask & 1))


def _comm_plan(kind, ins, outs, send_sems, recv_sems, local_sems):
    x, y, c, me = _mesh_pos()
    n = len(ins)
    src = (lambda a, idx: ins[a]) if kind == "gather" else (lambda a, idx: ins[a].at[idx])
    local = [pltpu.make_async_copy(src(a, me), outs[a].at[me], local_sems.at[a]) for a in range(n)]
    sends, recvs = [], []
    for mask in range(1, N_DEV):
        px, py, pc = _peer(x, y, c, mask)
        pidx = 4 * px + 2 * py + pc
        for a in range(n):
            k = (mask - 1) * n + a
            copy = lambda s, d, k=k: pltpu.make_async_remote_copy(
                src_ref=s, dst_ref=d, send_sem=send_sems.at[k], recv_sem=recv_sems.at[k],
                device_id=(px, py, pc), device_id_type=pl.DeviceIdType.MESH)
            sends.append(copy(src(a, pidx), outs[a].at[me]))
            recvs.append(copy(src(a, me), outs[a].at[pidx]))

    def start():
        for cp in local + sends:
            cp.start()

    def wait():
        for cp in sends:
            cp.wait_send()
        for cp in recvs:
            cp.wait_recv()
        for cp in local:
            cp.wait()

    return start, wait


def _comm_out_shapes(kind, arrays):
    return [jax.ShapeDtypeStruct(((N_DEV,) + s.shape) if kind == "gather" else s.shape, s.dtype) for s in arrays]


def _comm_scratch(n):
    k_tot = (N_DEV - 1) * n
    return [pltpu.SemaphoreType.DMA((k_tot,)), pltpu.SemaphoreType.DMA((k_tot,)), pltpu.SemaphoreType.DMA((n,))]


def _comm_call(name, kind, arrays):
    n = len(arrays)

    def body(*refs):
        start, wait = _comm_plan(kind, refs[:n], refs[n:2 * n], *refs[2 * n:])
        start()
        wait()

    hbm = pl.BlockSpec(memory_space=pl.ANY)
    return pl.pallas_call(
        body, name=name, in_specs=[hbm] * n, out_specs=[hbm] * n, out_shape=_comm_out_shapes(kind, arrays),
        scratch_shapes=_comm_scratch(n), compiler_params=pltpu.CompilerParams(has_side_effects=True),
    )(*arrays)


def _adam(name, parts, w, m, v, tr, layer=None, prev=None):
    p_, r_, cw = parts.shape
    c1 = 1.0 - ADAM_B1 ** ADAM_STEP
    c2 = 1.0 - ADAM_B2 ** ADAM_STEP
    n_prev = 4 if prev else 0

    def body(p_ref, w_ref, m_ref, v_ref, *rest):
        g_ref, d_ref, nm_ref, nv_ref = rest[n_prev:]
        g = p_ref[0].astype(F32)
        for i in range(1, p_):
            g = g + p_ref[i].astype(F32)
        nm = ADAM_B1 * m_ref[...] + (1.0 - ADAM_B1) * g
        nv = ADAM_B2 * v_ref[...] + (1.0 - ADAM_B2) * (g * g)
        d_ref[...] = -ADAM_LR * ((nm / c1) / (jnp.sqrt(nv / c2) + ADAM_EPS) + ADAM_WD * w_ref[...])
        g_ref[...] = g
        nm_ref[...] = nm
        nv_ref[...] = nv

    if layer is None:
        spec = pl.BlockSpec((tr, cw), lambda i: (i, 0))
        shp = jax.ShapeDtypeStruct((r_, cw), F32)
    else:
        spec = pl.BlockSpec((None, tr, cw), lambda i: (layer, i, 0))
        shp = jax.ShapeDtypeStruct((DEPTH, r_, cw), F32)
    return pl.pallas_call(
        body, name=name, grid=(r_ // tr,),
        in_specs=[pl.BlockSpec((p_, tr, cw), lambda i: (0, i, 0)), spec, spec, spec]
                 + [pl.BlockSpec(memory_space=pl.ANY)] * n_prev,
        out_specs=[spec] * 4, out_shape=[shp] * 4,
        input_output_aliases={4 + i: i for i in range(n_prev)},
        compiler_params=_cparams(("parallel",), VMEM_LIMIT),
    )(parts, w, m, v, *(prev or ()))


def _perm_cols(w_full):
    parts = [w_full[:, _ORIG[n][0]:_ORIG[n][0] + _ORIG[n][1]] for n in _ORDER]
    parts.append(jnp.zeros((w_full.shape[0], N_PAD - N_USED), w_full.dtype))
    return jnp.concatenate(parts, axis=1)


def _unperm_cols(g):
    by_orig = sorted(_ORDER, key=lambda n: _ORIG[n][0])
    return jnp.concatenate([g[:, _MINE[n]:_MINE[n] + _ORIG[n][1]] for n in by_orig], axis=1)


def _ret_consts(length):
    lg = jnp.log(1.0 - 2.0 ** (-5.0 - jnp.arange(RET_HEADS, dtype=F32)))
    idx = jnp.arange(CHUNK, dtype=F32)
    rel = idx[:, None] - idx[None, :]
    dmat = jnp.where(rel[None] >= 0, jnp.exp(jnp.maximum(rel, 0.0)[None] * lg[:, None, None]), 0.0)
    qdec = jnp.exp((idx[:, None] + 1.0) * lg[None, :]).T[:, :, None]
    kdec = jnp.exp((CHUNK - 1.0 - idx)[:, None] * lg[None, :]).T[:, :, None]
    cdec = jnp.exp(CHUNK * lg)[:, None, None]
    half = RET_DK // 2
    inv = ROPE_BASE ** (-jnp.arange(half, dtype=F32) / half)
    ang = jnp.arange(length, dtype=jnp.int32).astype(F32)[:, None] * inv[None, :]
    cos, sin = jnp.cos(ang), jnp.sin(ang)
    return (jnp.concatenate([cos, cos], axis=1), jnp.concatenate([-sin, sin], axis=1), dmat, qdec, kdec, cdec)


def _tiles(length):
    tm = 512 if length % 512 == 0 else length
    return tm


def _layer_fwd(x, p, consts, length, last, target, comm=None):
    tm = _tiles(length)
    (h,), _ = _rowmap("pre_norm", lambda x_, w_: ((_rms(x_, w_),), ()), length, tm,
                      [(x, D_MODEL, 0)], [p["pre_norm"]], out_ws=[D_MODEL], out_dtypes=[BF16])
    proj = _matmul("in_proj", h, p["w_in"], "nn", length, N_PAD, D_MODEL)
    xc = _conv_fwd(proj, p["conv_w"], p["conv_b"], length, tm, 512)
    o_a, st_a, *got = _gdn_fwd(xc, proj, p["gdn_A_log"], p["gdn_dt_bias"], p["gdn_norm"], length, comm)
    y_b, st_b = _ssd_fwd(xc, proj, p["ssd_A_log"], p["ssd_dt_bias"], p["ssd_D"], length)
    o_b = []
    for g in range(SSD_GROUPS):
        (o,), _ = _rowmap(f"ssd_gate{g}", lambda y_, z_, w_: ((_rms(y_ * _silu(z_), w_),), ()), length, tm,
                          [(y_b, 512, g), (proj, 512, _MINE["sz"] // 512 + g)], [p["ssd_norm"][g]], out_ws=[512],
                          out_dtypes=[BF16])
        o_b.append(o)
    o_c, st_c = _ret_fwd(proj, *consts, p["ret_norm"], length)
    mixed = jnp.concatenate([o.astype(BF16) for o in [o_a] + o_b + [o_c]], axis=1)
    out = _matmul("out_proj", mixed, p["w_out"], "nn", length, D_MODEL, MIX_W)
    res = dict(x=x, h=h, proj=proj, xc=xc, st_a=st_a, st_b=st_b, st_c=st_c, y_b=y_b, mixed=mixed, out=out)
    if not last:
        (y,), _ = _rowmap("post_norm", lambda x_, o_, w_: ((x_ + _rms(o_, w_),), ()), length, tm,
                          [(x, D_MODEL, 0), (out, D_MODEL, 0)], [p["post_norm"]], out_ws=[D_MODEL])
        return y, res, None, got

    def head(x_, o_, t_, w_):
        e = x_ + _rms(o_, w_) - t_
        row = jnp.mean(e * e, axis=-1, keepdims=True)
        loss = 0.5 * jnp.sum(row, axis=0, keepdims=True)
        return (e * (1.0 / D_MODEL),), (loss + jnp.zeros((1, LANE), F32),)

    (dy,), (loss,) = _rowmap("loss_head", head, length, tm,
                             [(x, D_MODEL, 0), (out, D_MODEL, 0), (target, D_MODEL, 0)], [p["post_norm"]],
                             out_ws=[D_MODEL], acc_ws=[LANE])
    return None, res, (dy, loss[0, 0]), got


def _layer_bwd(dy, p, res, consts, length, comm=None):
    tm = _tiles(length)
    g = {}
    (d_out,), (g["post_norm"],) = _rowmap("post_norm_bwd", lambda o_, w_: ((_rms(o_, w_),), ()), length, tm,
                                          [(res["out"], D_MODEL, 0)], [p["post_norm"]], cts=[(dy, D_MODEL, 0)])
    dmix = _matmul("out_proj_dx", d_out, p["w_out"], "nt", length, MIX_W, D_MODEL)
    g["w_out"] = _matmul("out_proj_dw", res["mixed"], d_out, "tn", MIX_W, D_MODEL, length)
    proj, xc = res["proj"], res["xc"]
    (dgq, dgk, dgv, dgz, dsm_a, g["gdn_A_log"], g["gdn_dt_bias"], g["gdn_norm"], *got) = _gdn_bwd(
        xc, proj, p["gdn_A_log"], p["gdn_dt_bias"], p["gdn_norm"], res["st_a"], dmix, length, comm)
    dyb, dsz, dsn = [], [], []
    for gi in range(SSD_GROUPS):
        (dy_g, dz_g), (dn_g,) = _rowmap(
            f"ssd_gate_bwd{gi}", lambda y_, z_, w_: ((_rms(y_ * _silu(z_), w_),), ()), length, tm,
            [(res["y_b"], 512, gi), (proj, 512, _MINE["sz"] // 512 + gi)], [p["ssd_norm"][gi]],
            cts=[(dmix, 512, 1 + gi)], d_dtypes=[F32, BF16])
        dyb.append(dy_g)
        dsz.append(dz_g)
        dsn.append(dn_g)
    g["ssd_norm"] = dsn
    dyb = jnp.concatenate(dyb, axis=1)
    dsx, dsb, dsc, dsm_b, g["ssd_A_log"], g["ssd_dt_bias"], g["ssd_D"] = _ssd_bwd(
        xc, proj, p["ssd_A_log"], p["ssd_dt_bias"], p["ssd_D"], res["st_b"], dyb, length)
    drq, drk, drv, drg, dnw_c = _ret_bwd(proj, *consts, p["ret_norm"], res["st_c"], dmix, length)
    g["ret_norm"] = dnw_c
    dxc = jnp.concatenate([dgq, dgk, dgv, dsx, dsb, dsc], axis=1)
    dpre, g["conv_w"], g["conv_b"] = _conv_bwd_pre(proj, p["conv_w"], p["conv_b"], dxc, length, tm, 512)
    dconv = _conv_bwd_x(dpre, p["conv_w"], length, tm, 512)
    dsmall = dsm_a + dsm_b[0] + dsm_b[1]
    dproj = jnp.concatenate([d.astype(BF16) for d in [dconv, dgz] + dsz + [drq, drk, drv, drg, dsmall]]
                            + [jnp.zeros((length, N_PAD - SMALL_OFF - LANE), BF16)], axis=1)
    dh = _matmul("in_proj_dx", dproj, p["w_in"], "nt", length, D_MODEL, N_PAD)
    g["w_in"] = _matmul("in_proj_dw", res["h"], dproj, "tn", D_MODEL, N_PAD, length)
    (dx,), (g["pre_norm"],) = _rowmap(
        "pre_norm_bwd", lambda x_, w_: ((_rms(x_, w_), x_), ()), length, tm,
        [(res["x"], D_MODEL, 0)], [p["pre_norm"]], cts=[(dh, D_MODEL, 0), (dy, D_MODEL, 0)])
    return dx, g, got


def _lane_row(vals, piece):
    lo = _MINE[piece] - SMALL_OFF
    return jnp.pad(vals[None], ((0, 0), (lo, LANE - lo - vals.shape[0])))


def _from_lane_row(row, piece, n):
    lo = _MINE[piece] - SMALL_OFF
    return row.reshape(-1, LANE).sum(axis=0)[lo:lo + n]


def _make_layer(l, w_in_full, w_out_full, conv_full, sw):
    return dict(
        pre_norm=sw["pre_norm"][l][None], post_norm=sw["post_norm"][l][None], w_in=_perm_cols(w_in_full),
        w_out=w_out_full,
        conv_w=jnp.concatenate([conv_full[0], conv_full[1]], axis=1),
        conv_b=jnp.concatenate([jnp.zeros((CONV_CH,), F32), sw["ssd_conv_b"][l]])[None],
        gdn_A_log=_lane_row(sw["gdn_A_log"][l], "ga"), gdn_dt_bias=_lane_row(sw["gdn_dt_bias"][l], "ga"),
        gdn_norm=sw["gdn_norm"][l][None],
        ssd_A_log=_lane_row(sw["ssd_A_log"][l], "sdt"), ssd_dt_bias=_lane_row(sw["ssd_dt_bias"][l], "sdt"),
        ssd_D=_lane_row(sw["ssd_D"][l], "sdt"),
        ssd_norm=[sw["ssd_norm"][l][None, 512 * g:512 * (g + 1)] for g in range(SSD_GROUPS)],
        ret_norm=sw["ret_norm"][l][None])


def _local_step(xb, tgt, layer0, make_layer1, length, fwd_comm=None, make_bwd_comm=None):
    consts = _ret_consts(length)
    y0, res0, _, got = _layer_fwd(xb, layer0, consts, length, False, tgt, fwd_comm)
    layer1 = make_layer1(got)
    _, res1, (dy, loss_local), _ = _layer_fwd(y0, layer1, consts, length, True, tgt)
    dy, grads1, _ = _layer_bwd(dy, layer1, res1, consts, length)
    dx, grads0, recv = _layer_bwd(dy, layer0, res0, consts, length, make_bwd_comm(grads1) if make_bwd_comm else None)
    return loss_local, dx, [grads0, grads1], recv


_SMALL = ["pre_norm", "post_norm", "gdn_A_log", "gdn_dt_bias", "gdn_norm", "ssd_conv_b", "ssd_A_log",
          "ssd_dt_bias", "ssd_D", "ssd_norm", "ret_norm"]


def _pack_small(arrs):
    rows = []
    for n in _SMALL:
        flat = arrs[n].reshape(-1)
        pad = (-flat.shape[0]) % LANE
        rows.append(jnp.pad(flat, (0, pad)).reshape(-1, LANE))
    out = jnp.concatenate(rows, axis=0)
    return jnp.pad(out, ((0, (-out.shape[0]) % SUBLANE), (0, 0)))


def _unpack_small(packed, like):
    out, r = {}, 0
    for n in _SMALL:
        cnt = like[n].size
        nrow = -(-cnt // LANE)
        out[n] = packed[r:r + nrow].reshape(-1)[:cnt].reshape(like[n].shape)
        r += nrow
    return out


def kernel(x, pre_norm, post_norm, w_in, gdn_conv, gdn_A_log, gdn_dt_bias, gdn_norm, ssd_conv, ssd_conv_b, ssd_A_log, ssd_dt_bias, ssd_D, ssd_norm, ret_norm, w_out, loss_target, m_pre_norm, m_post_norm, m_w_in, m_gdn_conv, m_gdn_A_log, m_gdn_dt_bias, m_gdn_norm, m_ssd_conv, m_ssd_conv_b, m_ssd_A_log, m_ssd_dt_bias, m_ssd_D, m_ssd_norm, m_ret_norm, m_w_out, v_pre_norm, v_post_norm, v_w_in, v_gdn_conv, v_gdn_A_log, v_gdn_dt_bias, v_gdn_norm, v_ssd_conv, v_ssd_conv_b, v_ssd_A_log, v_ssd_dt_bias, v_ssd_D, v_ssd_norm, v_ret_norm, v_w_out):
    length = x.shape[1]
    xb = x[0]
    tgt = loss_target[0]
    small_w = dict(pre_norm=pre_norm, post_norm=post_norm, gdn_A_log=gdn_A_log, gdn_dt_bias=gdn_dt_bias,
                   gdn_norm=gdn_norm, ssd_conv_b=ssd_conv_b, ssd_A_log=ssd_A_log, ssd_dt_bias=ssd_dt_bias,
                   ssd_D=ssd_D, ssd_norm=ssd_norm, ret_norm=ret_norm)
    small_m = dict(pre_norm=m_pre_norm, post_norm=m_post_norm, gdn_A_log=m_gdn_A_log, gdn_dt_bias=m_gdn_dt_bias,
                   gdn_norm=m_gdn_norm, ssd_conv_b=m_ssd_conv_b, ssd_A_log=m_ssd_A_log, ssd_dt_bias=m_ssd_dt_bias,
                   ssd_D=m_ssd_D, ssd_norm=m_ssd_norm, ret_norm=m_ret_norm)
    small_v = dict(pre_norm=v_pre_norm, post_norm=v_post_norm, gdn_A_log=v_gdn_A_log, gdn_dt_bias=v_gdn_dt_bias,
                   gdn_norm=v_gdn_norm, ssd_conv_b=v_ssd_conv_b, ssd_A_log=v_ssd_A_log, ssd_dt_bias=v_ssd_dt_bias,
                   ssd_D=v_ssd_D, ssd_norm=v_ssd_norm, ret_norm=v_ret_norm)

    w_in_b, w_out_b = w_in.astype(BF16), w_out.astype(BF16)
    conv_shard = jnp.stack([gdn_conv, ssd_conv], axis=1)
    shards = lambda l: [w_in_b[l], w_out_b[l], conv_shard[l]]

    def assemble(l, g_in, g_out, g_conv):
        w_in_full = g_in.transpose(1, 0, 2).reshape(D_MODEL, N_IN)
        w_out_full = g_out.reshape(MIX_W, D_MODEL)
        conv_full = g_conv.transpose(1, 2, 0, 3).reshape(2, CONV_W, CONV_CH)
        return _make_layer(l, w_in_full, w_out_full, conv_full, small_w)

    def grad_slabs(g):
        gi = _unperm_cols(g["w_in"]).reshape(D_MODEL, N_DEV, SHARD_IN).transpose(1, 0, 2).astype(BF16)
        go = g["w_out"].reshape(N_DEV, SHARD_OUT, D_MODEL).astype(BF16)
        gc = jnp.stack([g["conv_w"][:, k * CONV_CH:(k + 1) * CONV_CH].reshape(CONV_W, N_DEV, SHARD_CONV)
                        .transpose(1, 0, 2) for k in range(2)], axis=1).reshape(N_DEV, 2 * CONV_W, SHARD_CONV)
        return [gi, go, gc]

    layer0 = assemble(0, *_comm_call("gather_layer0", "gather", shards(0)))
    loss_local, dx, grads, recv1 = _local_step(
        xb, tgt, layer0, lambda got: assemble(1, *got), length,
        fwd_comm=("gather", shards(1)), make_bwd_comm=lambda g1: ("scatter", grad_slabs(g1)))
    grad_x = dx[None]
    loss = lax.psum(loss_local, ("x", "y", "c"))

    small_g = dict(
        pre_norm=jnp.concatenate([grads[l]["pre_norm"] for l in range(DEPTH)], axis=0),
        post_norm=jnp.concatenate([grads[l]["post_norm"] for l in range(DEPTH)], axis=0),
        gdn_A_log=jnp.stack([_from_lane_row(grads[l]["gdn_A_log"], "ga", GDN_HEADS) for l in range(DEPTH)]),
        gdn_dt_bias=jnp.stack([_from_lane_row(grads[l]["gdn_dt_bias"], "ga", GDN_HEADS) for l in range(DEPTH)]),
        gdn_norm=jnp.concatenate([grads[l]["gdn_norm"] for l in range(DEPTH)], axis=0),
        ssd_conv_b=jnp.concatenate([grads[l]["conv_b"][:, CONV_CH:] for l in range(DEPTH)], axis=0),
        ssd_A_log=jnp.stack([_from_lane_row(grads[l]["ssd_A_log"], "sdt", SSD_HEADS) for l in range(DEPTH)]),
        ssd_dt_bias=jnp.stack([_from_lane_row(grads[l]["ssd_dt_bias"], "sdt", SSD_HEADS) for l in range(DEPTH)]),
        ssd_D=jnp.stack([_from_lane_row(grads[l]["ssd_D"], "sdt", SSD_HEADS) for l in range(DEPTH)]),
        ssd_norm=jnp.concatenate([jnp.concatenate(grads[l]["ssd_norm"], axis=1) for l in range(DEPTH)], axis=0),
        ret_norm=jnp.concatenate([grads[l]["ret_norm"] for l in range(DEPTH)], axis=0))
    gs = _pack_small(small_g)
    gs8 = jnp.broadcast_to(gs[None], (N_DEV,) + gs.shape)
    *recv0, r_small = _comm_call("exchange_layer0", "scatter", grad_slabs(grads[0]) + [gs8])
    recv = [recv0, recv1]

    conv_w = lambda g_, s_, l: jnp.stack([g_[l], s_[l]], axis=0).reshape(2 * CONV_W, SHARD_CONV)
    o_in, o_out, o_conv = None, None, []
    for l in range(DEPTH):
        o_in = _adam(f"adam_w_in{l}", recv[l][0], w_in, m_w_in, v_w_in, 128, l, o_in)
        o_out = _adam(f"adam_w_out{l}", recv[l][1], w_out, m_w_out, v_w_out, 128, l, o_out)
        o_conv.append(_adam(f"adam_conv{l}", recv[l][2], conv_w(gdn_conv, ssd_conv, l),
                            conv_w(m_gdn_conv, m_ssd_conv, l), conv_w(v_gdn_conv, v_ssd_conv, l), 2 * CONV_W))
    ps_w, ps_m, ps_v = _pack_small(small_w), _pack_small(small_m), _pack_small(small_v)
    o_small = _adam("adam_small", r_small, ps_w, ps_m, ps_v, ps_w.shape[0])

    names = ["pre_norm", "post_norm", "w_in", "gdn_conv", "gdn_A_log", "gdn_dt_bias", "gdn_norm", "ssd_conv",
             "ssd_conv_b", "ssd_A_log", "ssd_dt_bias", "ssd_D", "ssd_norm", "ret_norm", "w_out"]
    outs = []
    for kind in range(4):
        d = _unpack_small(o_small[kind], small_w)
        cv = jnp.stack([o_conv[l][kind].reshape(2, CONV_W, SHARD_CONV) for l in range(DEPTH)], axis=1)
        d["w_in"] = o_in[kind]
        d["w_out"] = o_out[kind]
        d["gdn_conv"] = cv[0]
        d["ssd_conv"] = cv[1]
        outs.extend(d[n] for n in names)
    return (loss, grad_x, *outs)
```

```python
import functools
import math

import numpy as np
import jax
import jax.numpy as jnp
from jax import lax
from jax.experimental import pallas as pl
from jax.experimental.pallas import tpu as pltpu

F32 = jnp.float32
BF16 = jnp.bfloat16

D_MODEL = 1024
DEPTH = 2
CHUNK = 64
CONV_W = 4
EPS = 1e-6
N_DEV = 8

GDN_HEADS = 4
GDN_DK = 128
SSD_HEADS = 16
SSD_P = 64
SSD_N = 128
SSD_GROUPS = 2
SSD_PAIRS = SSD_HEADS // 2
PAIRS_PER_GROUP = SSD_PAIRS // SSD_GROUPS
RET_HEADS = 4
RET_DK = 128
ROPE_BASE = 10000.0
MIX_W = 2048
N_IN = 6680
SHARD_IN = N_IN // N_DEV
SHARD_OUT = MIX_W // N_DEV
CONV_CH = 1536
SHARD_CONV = CONV_CH // N_DEV

ADAM_LR = 0.001
ADAM_B1 = 0.9
ADAM_B2 = 0.999
ADAM_EPS = 1e-08
ADAM_WD = 0.01
ADAM_STEP = 10

LANE = 128
SUBLANE = 8
VMEM_LIMIT = 56 * 1024 * 1024

_ORIG = dict(gq=(0, 512), gk=(512, 512), gv=(1024, 512), gz=(1536, 512), gb=(2048, 4), ga=(2052, 4),
             sx=(2056, 1024), sB=(3080, 256), sC=(3336, 256), sz=(3592, 1024), sdt=(4616, 16),
             rq=(4632, 512), rk=(5144, 512), rv=(5656, 512), rg=(6168, 512))
_ORDER = ["rq", "rk", "rv", "rg", "gq", "gk", "gv", "sx", "sB", "sC", "gz", "sz", "gb", "ga", "sdt"]
_MINE = {}
_off = 0
for _n in _ORDER:
    _MINE[_n] = _off
    _off += _ORIG[_n][1]
N_USED = _off
N_PAD = 7168
CONV_ALL = 2 * CONV_CH
SMALL_OFF = _MINE["gb"]
CONV_OFF = _MINE["gq"]
XC = lambda name: _MINE[name] - CONV_OFF


def _cparams(sem, vmem=None):
    return pltpu.CompilerParams(dimension_semantics=sem, vmem_limit_bytes=vmem)


def _split_bf16(a):
    hi = a.astype(BF16)
    return hi, (a - hi.astype(F32)).astype(BF16)


def _make_mm():
    def raw(a, b, ca, cb):
        return lax.dot_general(a.astype(BF16), b.astype(BF16), (((ca,), (cb,)), ((), ())),
                               preferred_element_type=F32)

    @jax.custom_vjp
    def nn(a, b):
        return raw(a, b, 1, 0)

    @jax.custom_vjp
    def nt(a, b):
        return raw(a, b, 1, 1)

    @jax.custom_vjp
    def tn(a, b):
        return raw(a, b, 0, 0)

    nn.defvjp(lambda a, b: (raw(a, b, 1, 0), (a, b)), lambda r, g: (nt(g, r[1]), tn(r[0], g)))
    nt.defvjp(lambda a, b: (raw(a, b, 1, 1), (a, b)), lambda r, g: (nn(g, r[1]), tn(g, r[0])))
    tn.defvjp(lambda a, b: (raw(a, b, 0, 0), (a, b)), lambda r, g: (nt(r[1], g), nn(r[0], g)))
    return nn, nt, tn


_nn, _nt, _tn = _make_mm()


@jax.custom_vjp
def _swap_halves(t):
    return pltpu.roll(t, LANE // 2, 1)


_swap_halves.defvjp(lambda t: (pltpu.roll(t, LANE // 2, 1), None),
                    lambda _, g: (pltpu.roll(g, LANE // 2, 1),))


@jax.custom_vjp
def _split_rows(x):
    return tuple(x[i * CHUNK:(i + 1) * CHUNK] for i in range(x.shape[0] // CHUNK))


_split_rows.defvjp(lambda x: (tuple(x[i * CHUNK:(i + 1) * CHUNK] for i in range(x.shape[0] // CHUNK)), None),
                   lambda _, gs: (jnp.concatenate(gs, axis=0),))


def _dot3(a, b, ca, cb):
    dot = lambda x, y: lax.dot_general(x, y, (((ca,), (cb,)), ((), ())), preferred_element_type=F32)
    return dot(a[0], b[0]) + (dot(a[0], b[1]) + dot(a[1], b[0]))


@jax.custom_vjp
def _tri_inv(mats):
    return _tri_inv_impl(mats)


def _tri_inv_impl(mats):
    ii = lax.broadcasted_iota(jnp.int32, mats[0].shape, 0)
    jj = lax.broadcasted_iota(jnp.int32, mats[0].shape, 1)
    eye = jnp.where(ii == jj, 1.0, 0.0).astype(F32)
    ts = [eye - a for a in mats]
    ps = [_split_bf16(-a) for a in mats]
    for _ in range(int(math.log2(CHUNK)) - 1):
        ps = [_split_bf16(_dot3(p, p, 1, 0)) for p in ps]
        ts = [t + _dot3(_split_bf16(t), p, 1, 0) for t, p in zip(ts, ps)]
    return ts


def _tri_inv_bwd(ts, gs):
    tsp = [_split_bf16(t) for t in ts]
    xs = [_dot3(t, _split_bf16(g), 0, 0) for t, g in zip(tsp, gs)]
    return ([-_dot3(_split_bf16(x), t, 1, 1) for x, t in zip(xs, tsp)],)


_tri_inv.defvjp(lambda mats: (lambda ts: (ts, ts))(_tri_inv_impl(mats)), _tri_inv_bwd)


def _silu(x):
    return x * jax.nn.sigmoid(x)


@jax.custom_vjp
def _softplus(x):
    return jnp.maximum(x, 0.0) + jnp.log1p(jnp.exp(-jnp.abs(x)))


_softplus.defvjp(lambda x: (jnp.maximum(x, 0.0) + jnp.log1p(jnp.exp(-jnp.abs(x))), x),
                 lambda x, g: (g * jax.nn.sigmoid(x),))


def _rms(x, w):
    return x * lax.rsqrt(jnp.mean(x * x, axis=-1, keepdims=True) + EPS) * w


def _chunk_masks(n):
    ii = lax.broadcasted_iota(jnp.int32, (n, n), 0)
    jj = lax.broadcasted_iota(jnp.int32, (n, n), 1)
    return ii >= jj, ii > jj, ii == jj, ii <= jj


def _cumsum_col(g, causal, eye, upper):
    g_row = jnp.sum(jnp.where(eye, g, 0.0), axis=0, keepdims=True)
    col = jnp.sum(jnp.where(causal, g_row, 0.0), axis=1, keepdims=True)
    row = jnp.sum(jnp.where(upper, g, 0.0), axis=0, keepdims=True)
    return col, row


def _lane_col(block, lane):
    pick = lax.broadcasted_iota(jnp.int32, (1, block.shape[1]), 1) == lane
    return jnp.sum(jnp.where(pick, block, 0.0), axis=1, keepdims=True)


def _gdn_chunk(q, k, v, gz, sm, alog_row, dtb_row, nw, s):
    subs, nh = range(len(q)), len(q[0])
    c = q[0][0].shape[0]
    causal, _, eye, upper = _chunk_masks(c)
    stack = lambda xs: jnp.concatenate(xs, axis=0)
    per = lambda f: [[f(j, h) for h in range(nh)] for j in subs]
    qn = per(lambda j, h: q[j][h] * lax.rsqrt(jnp.sum(q[j][h] * q[j][h], axis=-1, keepdims=True) + EPS)
             * (GDN_DK ** -0.5))
    kn = per(lambda j, h: k[j][h] * lax.rsqrt(jnp.sum(k[j][h] * k[j][h], axis=-1, keepdims=True) + EPS))
    beta_blk = [jax.nn.sigmoid(sm[j]) for j in subs]
    g_blk = [-jnp.exp(alog_row) * _softplus(sm[j] + dtb_row) for j in subs]
    beta = per(lambda j, h: _lane_col(beta_blk[j], _MINE["gb"] - SMALL_OFF + h))
    g = per(lambda j, h: _lane_col(g_blk[j], _MINE["ga"] - SMALL_OFF + h))
    gcum = per(lambda j, h: _cumsum_col(g[j][h], causal, eye, upper)[0])
    eg = per(lambda j, h: jnp.exp(gcum[j][h]))
    glast = per(lambda j, h: jnp.sum(g[j][h], axis=0, keepdims=True))
    kb = per(lambda j, h: kn[j][h] * beta[j][h])
    n = nh * c
    ii = lax.broadcasted_iota(jnp.int32, (n, n), 0)
    jj = lax.broadcasted_iota(jnp.int32, (n, n), 1)
    sh = int(math.log2(c))
    same = lax.shift_right_logical(ii, sh) == lax.shift_right_logical(jj, sh)
    causal_bd = jnp.logical_and(same, ii >= jj)
    strict_bd = jnp.logical_and(same, ii > jj)
    gcum_all = [stack(gcum[j]) for j in subs]
    gcum_row = [jnp.sum(jnp.where(ii == jj, gcum_all[j], 0.0), axis=0, keepdims=True) for j in subs]
    decay = [jnp.where(causal_bd, jnp.exp(jnp.where(causal_bd, gcum_all[j] - gcum_row[j], 0.0)), 0.0) for j in subs]
    kn_all = [stack(kn[j]) for j in subs]
    a_low = [jnp.where(strict_bd, _nt(stack(kb[j]), kn_all[j]) * decay[j], 0.0) for j in subs]
    t = _tri_inv(a_low)
    u = [_nn(t[j], stack([v[j][h] * beta[j][h] for h in range(nh)])) for j in subs]
    w = [_split_rows(_nn(t[j], stack([kb[j][h] * eg[j][h] for h in range(nh)]))) for j in subs]
    attn = [_nt(stack(qn[j]), kn_all[j]) * decay[j] for j in subs]
    kdec = per(lambda j, h: kn[j][h] * jnp.exp(glast[j][h] - gcum[j][h]))
    on = []
    for j in subs:
        v_new_all = u[j] - stack([_nn(w[j][h], s[h]) for h in range(nh)])
        o = _split_rows(stack([_nn(qn[j][h] * eg[j][h], s[h]) for h in range(nh)]) + _nn(attn[j], v_new_all))
        v_new = _split_rows(v_new_all)
        s = [s[h] * jnp.exp(glast[j][h]) + _tn(kdec[j][h], v_new[h]) for h in range(nh)]
        on.append([_rms(o[h], nw) * _silu(gz[j][h]) for h in range(nh)])
    return on, s


def _ssd_chunk(lanes, x, bm, cm, sm, alog_row, dtb_row, d_row, hs):
    subs, pairs = range(len(x)), range(len(x[0]))
    c = x[0][0].shape[0]
    lane_i = lax.broadcasted_iota(jnp.int32, (c, LANE), 1)
    lane_lo = lane_i < SSD_P
    lane_lo1 = lax.broadcasted_iota(jnp.int32, (1, LANE), 1) < SSD_P
    causal2 = lax.broadcasted_iota(jnp.int32, (c, LANE), 0) >= jnp.bitwise_and(lane_i, c - 1)
    row_lo = lax.broadcasted_iota(jnp.int32, (LANE, 1), 0) < SSD_P
    eye2 = lax.broadcasted_iota(jnp.int32, (LANE, LANE), 0) == lax.broadcasted_iota(jnp.int32, (LANE, LANE), 1)
    to_row = lambda cols: jnp.sum(jnp.where(eye2, jnp.concatenate(cols, axis=0), 0.0), axis=0, keepdims=True)
    per = lambda f: [[f(j, p) for p in pairs] for j in subs]
    both = lambda blk, p: [_lane_col(blk, lanes[p][h]) for h in range(2)]
    dt_blk = [_softplus(sm[j] + dtb_row) for j in subs]
    a_blk = [dt_blk[j] * -jnp.exp(alog_row) for j in subs]
    alast_blk = [jnp.sum(a_blk[j], axis=0, keepdims=True) for j in subs]
    dt = per(lambda j, p: both(dt_blk[j], p))
    a = per(lambda j, p: both(a_blk[j], p))
    alast = per(lambda j, p: both(alast_blk[j], p))
    dp = [both(d_row, p) for p in pairs]
    a_row = per(lambda j, p: to_row(a[j][p]))
    part = per(lambda j, p: jnp.where(causal2, a_row[j][p], 0.0))
    acum = per(lambda j, p: [jnp.sum(jnp.where(lane_lo, part[j][p], 0.0), axis=1, keepdims=True),
                             jnp.sum(jnp.where(lane_lo, 0.0, part[j][p]), axis=1, keepdims=True)])
    acum_row = per(lambda j, p: to_row(acum[j][p]))
    acum_col = per(lambda j, p: jnp.where(lane_lo, acum[j][p][0], acum[j][p][1]))
    lmat = per(lambda j, p: jnp.where(causal2, jnp.exp(jnp.where(causal2, acum_col[j][p] - acum_row[j][p], 0.0)), 0.0))
    cb = [_nt(cm[j], jnp.concatenate([bm[j], bm[j]], axis=0)) for j in subs]
    xdt = per(lambda j, p: x[j][p] * jnp.where(lane_lo, dt[j][p][0], dt[j][p][1]))
    xdt_rows = per(lambda j, p: jnp.concatenate([jnp.where(lane_lo, xdt[j][p], 0.0),
                                                 jnp.where(lane_lo, 0.0, xdt[j][p])], axis=0))
    intra = per(lambda j, p: _nn(cb[j] * lmat[j][p], xdt_rows[j][p]))
    skip = per(lambda j, p: x[j][p] * jnp.where(lane_lo1, dp[p][0], dp[p][1]))
    eacc = per(lambda j, p: jnp.exp(acum_col[j][p]))
    wdec = per(lambda j, p: jnp.exp(jnp.where(lane_lo, alast[j][p][0] - acum[j][p][0],
                                              alast[j][p][1] - acum[j][p][1])))
    scale = per(lambda j, p: jnp.exp(jnp.where(row_lo, alast[j][p][0], alast[j][p][1])))
    upd = per(lambda j, p: _tn(xdt[j][p] * wdec[j][p], bm[j]))
    y = []
    for j in subs:
        y.append([skip[j][p] + intra[j][p] + _nt(cm[j], hs[p]) * eacc[j][p] for p in pairs])
        hs = [hs[p] * scale[j][p] + upd[j][p] for p in pairs]
    return y, hs


def _ret_chunk(rq, rk, rv, rg, cos2, sin2, dmat, qdec, kdec, cdec, nw, r):
    subs, hs = range(len(rq)), range(len(rq[0]))
    per = lambda f: [[f(j, h) for h in hs] for j in subs]
    q = per(lambda j, h: rq[j][h] * cos2[j] + _swap_halves(rq[j][h]) * sin2[j])
    k = per(lambda j, h: (rk[j][h] * cos2[j] + _swap_halves(rk[j][h]) * sin2[j]) * (RET_DK ** -0.5))
    s = per(lambda j, h: _nt(q[j][h], k[j][h]) * dmat[h])
    intra = per(lambda j, h: _nn(s[j][h], rv[j][h]))
    upd = per(lambda j, h: _tn(k[j][h] * kdec[h], rv[j][h]))
    gate = per(lambda j, h: _silu(rg[j][h]))
    on = []
    for j in subs:
        on.append([_rms(intra[j][h] + _nn(q[j][h], r[h]) * qdec[h], nw) * gate[j][h] for h in hs])
        r = [r[h] * cdec[h] + upd[j][h] for h in hs]
    return on, r


MM_TILE = 1024


def _tile(n, pref=MM_TILE):
    return pref if n % pref == 0 else n


def _matmul(name, a, b, mode, m, n, k, tk=None):
    tm, tn = _tile(m), _tile(n)
    tk = _tile(k) if tk is None else tk
    nk = k // tk
    ca, cb = {"nn": (1, 0), "nt": (1, 1), "tn": (0, 0)}[mode]

    def body(a_ref, b_ref, o_ref):
        part = lax.dot_general(a_ref[...].astype(BF16), b_ref[...].astype(BF16),
                               (((ca,), (cb,)), ((), ())), preferred_element_type=F32)
        if nk == 1:
            o_ref[...] = part
        else:
            kk = pl.program_id(2)

            @pl.when(kk == 0)
            def _():
                o_ref[...] = part

            @pl.when(kk > 0)
            def _():
                o_ref[...] += part

    a_spec = pl.BlockSpec((tk, tm), lambda i, j, kk: (kk, i)) if mode == "tn" else pl.BlockSpec((tm, tk), lambda i, j, kk: (i, kk))
    b_spec = pl.BlockSpec((tn, tk), lambda i, j, kk: (j, kk)) if mode == "nt" else pl.BlockSpec((tk, tn), lambda i, j, kk: (kk, j))
    return pl.pallas_call(
        body, name=name, grid=(m // tm, n // tn, nk),
        in_specs=[a_spec, b_spec], out_specs=pl.BlockSpec((tm, tn), lambda i, j, kk: (i, j)),
        out_shape=jax.ShapeDtypeStruct((m, n), F32),
        compiler_params=_cparams(("parallel", "parallel", "arbitrary"), VMEM_LIMIT),
    )(a, b)


def _rowmap(name, fn, length, tm, rows, params, out_ws=(), acc_ws=(), out_dtypes=None, cts=None, nd_rows=(),
            d_dtypes=None, place=None):
    nt_ = length // tm
    n_r, n_nd, n_p = len(rows), len(nd_rows), len(params)
    rspec = lambda bw, cbk: pl.BlockSpec((tm, bw), lambda i: (i, cbk))
    pspec = lambda w: pl.BlockSpec((1, w), lambda i: (0, 0))
    in_arrays = [r[0] for r in rows] + [r[0] for r in nd_rows] + list(params)
    in_specs = [rspec(r[1], r[2]) for r in rows] + [rspec(r[1], r[2]) for r in nd_rows] + [pspec(p.shape[1]) for p in params]
    place = place or {}

    def load(refs):
        return [r[...].astype(F32) for r in refs]

    def placed(widths, dtypes, n_before):
        specs, shapes, extra, alias = [], [], [], {}
        for k, (w, dt) in enumerate(zip(widths, dtypes)):
            arr, total, cbk = place.get(k, (None, w, 0))
            specs.append(rspec(w, cbk))
            shapes.append(jax.ShapeDtypeStruct((length, total), dt))
            if arr is not None:
                alias[n_before + len(extra)] = k
                extra.append(arr)
        return specs, shapes, extra, alias

    if cts is None:
        n_o, n_a = len(out_ws), len(acc_ws)
        out_dtypes_ = out_dtypes or [F32] * n_o

        o_specs, o_shapes, extra, alias = placed(out_ws, out_dtypes_, n_r + n_nd + n_p)

        def body(*refs):
            ins = load(refs[:n_r + n_nd + n_p])
            outs, accs = fn(*ins)
            o_refs = refs[n_r + n_nd + n_p + len(extra):]
            for o_ref, o in zip(o_refs[:n_o], outs):
                o_ref[...] = o.astype(o_ref.dtype)
            first = pl.program_id(0) == 0
            for a_ref, acc in zip(o_refs[n_o:], accs):
                @pl.when(first)
                def _(a_ref=a_ref):
                    a_ref[...] = jnp.zeros_like(a_ref)
                a_ref[...] += acc

        res = pl.pallas_call(
            body, name=name, grid=(nt_,), in_specs=in_specs + [pl.BlockSpec(memory_space=pl.ANY)] * len(extra),
            out_specs=o_specs + [pspec(w) for w in acc_ws],
            out_shape=o_shapes + [jax.ShapeDtypeStruct((1, w), F32) for w in acc_ws],
            input_output_aliases=alias,
            compiler_params=_cparams(("arbitrary",), VMEM_LIMIT),
        )(*in_arrays, *extra)
        return res[:n_o], res[n_o:]

    n_c = len(cts)
    o_specs, o_shapes, extra, alias = placed([r[1] for r in rows], d_dtypes or [F32] * n_r, n_r + n_nd + n_p + n_c)

    def body(*refs):
        ins = load(refs[:n_r + n_nd + n_p])
        ct_vals = load(refs[n_r + n_nd + n_p:n_r + n_nd + n_p + n_c])
        nd = ins[n_r:n_r + n_nd]
        f = lambda rs, ps: fn(*rs, *nd, *ps)[0]
        _, vjp = jax.vjp(f, ins[:n_r], ins[n_r + n_nd:])
        d_rows, d_params = vjp(tuple(ct_vals))
        o_refs = refs[n_r + n_nd + n_p + n_c + len(extra):]
        for o_ref, d in zip(o_refs[:n_r], d_rows):
            o_ref[...] = d.astype(o_ref.dtype)
        first = pl.program_id(0) == 0
        for a_ref, d in zip(o_refs[n_r:], d_params):
            @pl.when(first)
            def _(a_ref=a_ref):
                a_ref[...] = jnp.zeros_like(a_ref)
            a_ref[...] += d

    res = pl.pallas_call(
        body, name=name, grid=(nt_,),
        in_specs=in_specs + [rspec(c[1], c[2]) for c in cts] + [pl.BlockSpec(memory_space=pl.ANY)] * len(extra),
        out_specs=o_specs + [pspec(p.shape[1]) for p in params],
        out_shape=o_shapes + [jax.ShapeDtypeStruct((1, p.shape[1]), F32) for p in params],
        input_output_aliases=alias,
        compiler_params=_cparams(("arbitrary",), VMEM_LIMIT),
    )(*in_arrays, *[c[0] for c in cts], *extra)
    return res[:n_r], res[n_r:]


def _conv_shift(ext, k, tm, forward):
    s = CONV_W - 1 - k
    if forward:
        rolled = ext if s == 0 else pltpu.roll(ext, s, 0)
        return rolled[SUBLANE:, :]
    rolled = ext if s == 0 else pltpu.roll(ext, tm + SUBLANE - s, 0)
    return rolled[:tm, :]


def _conv_pre(x_ref, halo_ref, w_ref, b_ref, first, tm):
    halo = jnp.where(first, 0.0, halo_ref[...])
    ext = jnp.concatenate([halo, x_ref[...]], axis=0)
    w = w_ref[...]
    pre = b_ref[...] + jnp.zeros_like(x_ref[...])
    taps = []
    for k in range(CONV_W):
        tap = _conv_shift(ext, k, tm, True)
        taps.append(tap)
        pre = pre + tap * w[k:k + 1, :]
    return pre, taps


def _conv_fwd(proj, w, b, length, tm, tc):
    hb = tm // SUBLANE
    j0 = CONV_OFF // tc

    def body(x_ref, halo_ref, w_ref, b_ref, o_ref):
        pre, _ = _conv_pre(x_ref, halo_ref, w_ref, b_ref, pl.program_id(0) == 0, tm)
        o_ref[...] = _silu(pre)

    return pl.pallas_call(
        body, name="conv_fwd", grid=(length // tm, CONV_ALL // tc),
        in_specs=[pl.BlockSpec((tm, tc), lambda i, j: (i, j0 + j)),
                  pl.BlockSpec((SUBLANE, tc), lambda i, j: (jnp.maximum(i * hb - 1, 0), j0 + j)),
                  pl.BlockSpec((CONV_W, tc), lambda i, j: (0, j)),
                  pl.BlockSpec((1, tc), lambda i, j: (0, j))],
        out_specs=pl.BlockSpec((tm, tc), lambda i, j: (i, j)),
        out_shape=jax.ShapeDtypeStruct((length, CONV_ALL), F32),
        compiler_params=_cparams(("parallel", "parallel"), VMEM_LIMIT),
    )(proj, proj, w, b)


def _conv_bwd_pre(proj, w, b, dxc_a, dxc_b, length, tm, tc):
    hb = tm // SUBLANE
    j0 = CONV_OFF // tc
    jh = CONV_CH // tc

    def body(x_ref, halo_ref, w_ref, b_ref, dya_ref, dyb_ref, dpre_ref, dw_ref, db_ref):
        i = pl.program_id(1)
        pre, taps = _conv_pre(x_ref, halo_ref, w_ref, b_ref, i == 0, tm)
        sg = jax.nn.sigmoid(pre)
        dy = jnp.where(pl.program_id(0) < jh, dya_ref[...], dyb_ref[...])
        dpre = dy * (sg * (1.0 + pre * (1.0 - sg)))
        dpre_ref[...] = dpre

        @pl.when(i == 0)
        def _():
            dw_ref[...] = jnp.zeros_like(dw_ref)
            db_ref[...] = jnp.zeros_like(db_ref)

        for k in range(CONV_W):
            dw_ref[k:k + 1, :] += jnp.sum(dpre * taps[k], axis=0, keepdims=True)
        db_ref[...] += jnp.sum(dpre, axis=0, keepdims=True)

    return pl.pallas_call(
        body, name="conv_bwd_pre", grid=(CONV_ALL // tc, length // tm),
        in_specs=[pl.BlockSpec((tm, tc), lambda j, i: (i, j0 + j)),
                  pl.BlockSpec((SUBLANE, tc), lambda j, i: (jnp.maximum(i * hb - 1, 0), j0 + j)),
                  pl.BlockSpec((CONV_W, tc), lambda j, i: (0, j)),
                  pl.BlockSpec((1, tc), lambda j, i: (0, j)),
                  pl.BlockSpec((tm, tc), lambda j, i: (jnp.where(j < jh, i, 0), jnp.minimum(j, jh - 1))),
                  pl.BlockSpec((tm, tc), lambda j, i: (jnp.where(j < jh, 0, i), jnp.maximum(j - jh, 0)))],
        out_specs=[pl.BlockSpec((tm, tc), lambda j, i: (i, j)),
                   pl.BlockSpec((CONV_W, tc), lambda j, i: (0, j)),
                   pl.BlockSpec((1, tc), lambda j, i: (0, j))],
        out_shape=[jax.ShapeDtypeStruct((length, CONV_ALL), F32),
                   jax.ShapeDtypeStruct((CONV_W, CONV_ALL), F32),
                   jax.ShapeDtypeStruct((1, CONV_ALL), F32)],
        compiler_params=_cparams(("parallel", "arbitrary"), VMEM_LIMIT),
    )(proj, proj, w, b, dxc_a, dxc_b)


def _conv_bwd_x(dpre, w, dproj, length, tm, tc):
    hb = tm // SUBLANE
    n_t = length // tm
    last_blk = length // SUBLANE - 1
    j0 = CONV_OFF // tc

    def body(d_ref, halo_ref, w_ref, _, o_ref):
        halo = jnp.where(pl.program_id(0) == n_t - 1, 0.0, halo_ref[...])
        ext = jnp.concatenate([d_ref[...], halo], axis=0)
        w = w_ref[...]
        acc = jnp.zeros_like(d_ref[...])
        for k in range(CONV_W):
            acc = acc + _conv_shift(ext, k, tm, False) * w[k:k + 1, :]
        o_ref[...] = acc.astype(o_ref.dtype)

    return pl.pallas_call(
        body, name="conv_bwd_x", grid=(n_t, CONV_ALL // tc),
        in_specs=[pl.BlockSpec((tm, tc), lambda i, j: (i, j)),
                  pl.BlockSpec((SUBLANE, tc), lambda i, j: (jnp.minimum((i + 1) * hb, last_blk), j)),
                  pl.BlockSpec((CONV_W, tc), lambda i, j: (0, j)),
                  pl.BlockSpec(memory_space=pl.ANY)],
        out_specs=pl.BlockSpec((tm, tc), lambda i, j: (i, j0 + j)),
        out_shape=jax.ShapeDtypeStruct(dproj.shape, dproj.dtype), input_output_aliases={3: 0},
        compiler_params=_cparams(("parallel", "parallel"), VMEM_LIMIT),
    )(dpre, dpre, w, dproj)


GDN_CHUNKS_PER_STEP = 4


def _gdn_steps(length):
    nc = length // CHUNK
    gs = GDN_CHUNKS_PER_STEP if nc % GDN_CHUNKS_PER_STEP == 0 else 1
    return gs, nc // gs, gs * CHUNK


def _gdn_tiles(ref, gs):
    return [[ref[j * CHUNK:(j + 1) * CHUNK, h * LANE:(h + 1) * LANE] for h in range(GDN_HEADS)] for j in range(gs)]


def _chunk_rows(ref, gs):
    return [ref[j * CHUNK:(j + 1) * CHUNK, :] for j in range(gs)]


def _gdn_fwd(xc, proj, alog, dtb, nw, length, comm=None):
    gs, steps, rows = _gdn_steps(length)
    h_ = GDN_HEADS
    hw = h_ * LANE
    n_comm = len(comm[1]) if comm else 0

    def body(*refs):
        q_ref, k_ref, v_ref, gz_ref, sm_ref, al_ref, dt_ref, nw_ref = refs[:8]
        c_in = refs[8:8 + n_comm]
        on_ref, st_ref = refs[8 + n_comm:10 + n_comm]
        c_out = refs[10 + n_comm:10 + 2 * n_comm]
        s_ref = refs[10 + 2 * n_comm]
        sems = refs[11 + 2 * n_comm:]

        @pl.when(pl.program_id(0) == 0)
        def _():
            s_ref[...] = jnp.zeros_like(s_ref)
            if comm:
                _comm_plan(comm[0], c_in, c_out, *sems)[0]()

        st_ref[...] = s_ref[...]
        lead = lambda ref: [ref[h] for h in range(h_)]
        on, s_new = _gdn_chunk(_gdn_tiles(q_ref, gs), _gdn_tiles(k_ref, gs), _gdn_tiles(v_ref, gs),
                               _gdn_tiles(gz_ref, gs), _chunk_rows(sm_ref, gs), al_ref[...], dt_ref[...],
                               nw_ref[...], lead(s_ref))
        for h in range(h_):
            for j in range(gs):
                on_ref[j * CHUNK:(j + 1) * CHUNK, h * LANE:(h + 1) * LANE] = on[j][h].astype(on_ref.dtype)
            s_ref[h] = s_new[h]

        if comm:
            @pl.when(pl.program_id(0) == steps - 1)
            def _():
                _comm_plan(comm[0], c_in, c_out, *sems)[1]()

    blk = lambda col: pl.BlockSpec((rows, hw), lambda c: (c, col // hw))
    one = pl.BlockSpec((1, LANE), lambda c: (0, 0))
    hbm = pl.BlockSpec(memory_space=pl.ANY)
    return pl.pallas_call(
        body, name="gdn_fwd_" + comm[0] if comm else "gdn_fwd", grid=(steps,),
        in_specs=[blk(XC("gq")), blk(XC("gk")), blk(XC("gv")), blk(_MINE["gz"]),
                  pl.BlockSpec((rows, LANE), lambda c: (c, SMALL_OFF // LANE)), one, one,
                  pl.BlockSpec((1, LANE), lambda c: (0, 0))] + [hbm] * n_comm,
        out_specs=[pl.BlockSpec((rows, hw), lambda c: (c, 0)),
                   pl.BlockSpec((h_, None, LANE, LANE), lambda c: (0, c, 0, 0))] + [hbm] * n_comm,
        out_shape=[jax.ShapeDtypeStruct((length, MIX_W), BF16),
                   jax.ShapeDtypeStruct((h_, steps, LANE, LANE), F32)]
                  + (_comm_out_shapes(*comm) if comm else []),
        scratch_shapes=[pltpu.VMEM((h_, LANE, LANE), F32)] + (_comm_scratch(n_comm) if comm else []),
        compiler_params=_cparams(("arbitrary",), VMEM_LIMIT),
    )(xc, xc, xc, proj, proj, alog, dtb, nw, *(comm[1] if comm else []))


def _gdn_bwd(xc, proj, alog, dtb, nw, st, dmix, length, comm=None):
    gs, steps, rows = _gdn_steps(length)
    h_ = GDN_HEADS
    hw = h_ * LANE
    n_comm = len(comm[1]) if comm else 0
    n_in, n_out = 10, 6

    def body(*refs):
        q_ref, k_ref, v_ref, gz_ref, sm_ref, al_ref, dt_ref, nw_ref, st_ref, do_ref = refs[:n_in]
        c_in = refs[n_in:n_in + n_comm]
        o0 = n_in + n_comm
        dqkv_ref, dgz_ref, dsm_ref, dal_ref, ddt_ref, dnw_ref = refs[o0:o0 + n_out]
        c_out = refs[o0 + n_out:o0 + n_out + n_comm]
        ds_ref = refs[o0 + n_out + n_comm]
        sems = refs[o0 + n_out + n_comm + 1:]

        @pl.when(pl.program_id(0) == 0)
        def _():
            ds_ref[...] = jnp.zeros_like(ds_ref)
            dal_ref[...] = jnp.zeros_like(dal_ref)
            ddt_ref[...] = jnp.zeros_like(ddt_ref)
            dnw_ref[...] = jnp.zeros_like(dnw_ref)
            if comm:
                _comm_plan(comm[0], c_in, c_out, *sems)[0]()

        lead = lambda ref: [ref[h] for h in range(h_)]
        ins = (_gdn_tiles(q_ref, gs), _gdn_tiles(k_ref, gs), _gdn_tiles(v_ref, gs), _gdn_tiles(gz_ref, gs),
               _chunk_rows(sm_ref, gs), al_ref[...], dt_ref[...], nw_ref[...], lead(st_ref))
        _, vjp = jax.vjp(_gdn_chunk, *ins)
        dq, dk, dv, dgz, dsm, dal, ddt, dnw, ds = vjp((_gdn_tiles(do_ref, gs), lead(ds_ref)))
        for j in range(gs):
            rws = slice(j * CHUNK, (j + 1) * CHUNK)
            for h in range(h_):
                cols = slice(h * LANE, (h + 1) * LANE)
                for part, d in enumerate((dq, dk, dv)):
                    dqkv_ref[rws, part * hw + h * LANE:part * hw + (h + 1) * LANE] = d[j][h]
                dgz_ref[rws, cols] = dgz[j][h].astype(dgz_ref.dtype)
            dsm_ref[rws, :] = dsm[j]
        for h in range(h_):
            ds_ref[h] = ds[h]
        dal_ref[...] += dal
        ddt_ref[...] += ddt
        dnw_ref[...] += dnw

        if comm:
            @pl.when(pl.program_id(0) == steps - 1)
            def _():
                _comm_plan(comm[0], c_in, c_out, *sems)[1]()

    r = lambda c: steps - 1 - c
    blk = lambda col: pl.BlockSpec((rows, hw), lambda c: (r(c), col // hw))
    one = pl.BlockSpec((1, LANE), lambda c: (0, 0))
    hbm = pl.BlockSpec(memory_space=pl.ANY)
    return pl.pallas_call(
        body, name="gdn_bwd_" + comm[0] if comm else "gdn_bwd", grid=(steps,),
        in_specs=[blk(XC("gq")), blk(XC("gk")), blk(XC("gv")), blk(_MINE["gz"]),
                  pl.BlockSpec((rows, LANE), lambda c: (r(c), SMALL_OFF // LANE)), one, one,
                  pl.BlockSpec((1, LANE), lambda c: (0, 0)),
                  pl.BlockSpec((h_, None, LANE, LANE), lambda c: (0, r(c), 0, 0)),
                  blk(0)] + [hbm] * n_comm,
        out_specs=[pl.BlockSpec((rows, 3 * hw), lambda c: (r(c), 0)), blk(_MINE["gz"]),
                   pl.BlockSpec((rows, LANE), lambda c: (r(c), 0)), one, one,
                   pl.BlockSpec((1, LANE), lambda c: (0, 0))] + [hbm] * n_comm,
        out_shape=[jax.ShapeDtypeStruct((length, 3 * hw), F32), jax.ShapeDtypeStruct((length, N_PAD), BF16),
                   jax.ShapeDtypeStruct((length, LANE), F32),
                   jax.ShapeDtypeStruct((1, LANE), F32), jax.ShapeDtypeStruct((1, LANE), F32),
                   jax.ShapeDtypeStruct((1, LANE), F32)] + (_comm_out_shapes(*comm) if comm else []),
        scratch_shapes=[pltpu.VMEM((h_, LANE, LANE), F32)] + (_comm_scratch(n_comm) if comm else []),
        compiler_params=_cparams(("arbitrary",), VMEM_LIMIT),
    )(xc, xc, xc, proj, proj, alog, dtb, nw, st, dmix, *(comm[1] if comm else []))


SSD_CHUNKS_PER_STEP = 4


def _ssd_steps(length):
    nc = length // CHUNK
    gs = SSD_CHUNKS_PER_STEP if nc % SSD_CHUNKS_PER_STEP == 0 else 1
    return gs, nc // gs, gs * CHUNK


def _ssd_lanes(g):
    base = _MINE["sdt"] - SMALL_OFF
    return [[base + (g * PAIRS_PER_GROUP + p) * 2 + j for j in range(2)] for p in range(PAIRS_PER_GROUP)]


def _ssd_tiles(ref, gs):
    return [[ref[j * CHUNK:(j + 1) * CHUNK, p * LANE:(p + 1) * LANE] for p in range(PAIRS_PER_GROUP)]
            for j in range(gs)]


def _ssd_fwd(xc, proj, alog_row, dtb_row, d_row, length):
    gs, steps, rows = _ssd_steps(length)
    ppg = PAIRS_PER_GROUP
    gw = ppg * LANE

    def body(x_ref, b_ref, c_ref, sm_ref, al_ref, db_ref, dp_ref, y_ref, st_ref, hs_ref):
        @pl.when(pl.program_id(1) == 0)
        def _():
            hs_ref[...] = jnp.zeros_like(hs_ref)

        st_ref[...] = hs_ref[...]
        y, hs_new = _ssd_chunk(_ssd_lanes(pl.program_id(0)), _ssd_tiles(x_ref, gs), _chunk_rows(b_ref, gs),
                               _chunk_rows(c_ref, gs), _chunk_rows(sm_ref, gs), al_ref[...], db_ref[...],
                               dp_ref[...], [hs_ref[p] for p in range(ppg)])
        for p in range(ppg):
            for j in range(gs):
                y_ref[j * CHUNK:(j + 1) * CHUNK, p * LANE:(p + 1) * LANE] = y[j][p]
            hs_ref[p] = hs_new[p]

    one = pl.BlockSpec((1, LANE), lambda g, c: (0, 0))
    return pl.pallas_call(
        body, name="ssd_fwd", grid=(SSD_GROUPS, steps),
        in_specs=[pl.BlockSpec((rows, gw), lambda g, c: (c, XC("sx") // gw + g)),
                  pl.BlockSpec((rows, LANE), lambda g, c: (c, XC("sB") // LANE + g)),
                  pl.BlockSpec((rows, LANE), lambda g, c: (c, XC("sC") // LANE + g)),
                  pl.BlockSpec((rows, LANE), lambda g, c: (c, SMALL_OFF // LANE)),
                  one, one, one],
        out_specs=[pl.BlockSpec((rows, gw), lambda g, c: (c, g)),
                   pl.BlockSpec((None, ppg, None, LANE, SSD_N), lambda g, c: (g, 0, c, 0, 0))],
        out_shape=[jax.ShapeDtypeStruct((length, SSD_GROUPS * gw), F32),
                   jax.ShapeDtypeStruct((SSD_GROUPS, ppg, steps, LANE, SSD_N), F32)],
        scratch_shapes=[pltpu.VMEM((ppg, LANE, SSD_N), F32)],
        compiler_params=_cparams(("parallel", "arbitrary"), VMEM_LIMIT),
    )(xc, xc, xc, proj, alog_row, dtb_row, d_row)


def _ssd_bwd(xc, proj, alog_row, dtb_row, d_row, st, dy, length):
    gs, steps, rows = _ssd_steps(length)
    ppg = PAIRS_PER_GROUP
    gw = ppg * LANE

    def body(x_ref, b_ref, c_ref, sm_ref, al_ref, db_ref, dp_ref, st_ref, dy_ref,
             dxc_ref, dsm_ref, dal_ref, ddb_ref, ddp_ref, dhs_ref):
        g = pl.program_id(1)

        @pl.when(jnp.logical_and(pl.program_id(0) == 0, g == 0))
        def _():
            dhs_ref[...] = jnp.zeros_like(dhs_ref)
            dal_ref[...] = jnp.zeros_like(dal_ref)
            ddb_ref[...] = jnp.zeros_like(ddb_ref)
            ddp_ref[...] = jnp.zeros_like(ddp_ref)

        pairs = range(ppg)
        ins = (_ssd_tiles(x_ref, gs), _chunk_rows(b_ref, gs), _chunk_rows(c_ref, gs), _chunk_rows(sm_ref, gs),
               al_ref[...], db_ref[...], dp_ref[...], [st_ref[p] for p in pairs])
        _, vjp = jax.vjp(functools.partial(_ssd_chunk, _ssd_lanes(g)), *ins)
        dx, dbm, dcm, dsm, dal, ddb, ddp, dhs = vjp((_ssd_tiles(dy_ref, gs), [dhs_ref[g, p] for p in pairs]))
        for grp in range(SSD_GROUPS):
            @pl.when(g == grp)
            def _(grp=grp):
                for j in range(gs):
                    rws = slice(j * CHUNK, (j + 1) * CHUNK)
                    for p in pairs:
                        col = grp * gw + p * LANE
                        dxc_ref[rws, col:col + LANE] = dx[j][p]
                    cb_ = XC("sB") - XC("sx") + grp * SSD_N
                    cc_ = XC("sC") - XC("sx") + grp * SSD_N
                    dxc_ref[rws, cb_:cb_ + SSD_N] = dbm[j]
                    dxc_ref[rws, cc_:cc_ + SSD_N] = dcm[j]
        for j in range(gs):
            dsm_ref[j * CHUNK:(j + 1) * CHUNK, :] = dsm[j]
        for p in pairs:
            dhs_ref[g, p] = dhs[p]
        dal_ref[g] += dal
        ddb_ref[g] += ddb
        ddp_ref[g] += ddp

    r = lambda c: steps - 1 - c
    one = pl.BlockSpec((1, LANE), lambda c, g: (0, 0))
    acc = pl.BlockSpec((SSD_GROUPS, 1, LANE), lambda c, g: (0, 0, 0))
    prm = jax.ShapeDtypeStruct((SSD_GROUPS, 1, LANE), F32)
    return pl.pallas_call(
        body, name="ssd_bwd", grid=(steps, SSD_GROUPS),
        in_specs=[pl.BlockSpec((rows, gw), lambda c, g: (r(c), XC("sx") // gw + g)),
                  pl.BlockSpec((rows, LANE), lambda c, g: (r(c), XC("sB") // LANE + g)),
                  pl.BlockSpec((rows, LANE), lambda c, g: (r(c), XC("sC") // LANE + g)),
                  pl.BlockSpec((rows, LANE), lambda c, g: (r(c), SMALL_OFF // LANE)),
                  one, one, one,
                  pl.BlockSpec((None, ppg, None, LANE, SSD_N), lambda c, g: (g, 0, r(c), 0, 0)),
                  pl.BlockSpec((rows, gw), lambda c, g: (r(c), g))],
        out_specs=[pl.BlockSpec((rows, CONV_CH), lambda c, g: (r(c), 0)),
                   pl.BlockSpec((None, rows, LANE), lambda c, g: (g, r(c), 0)), acc, acc, acc],
        out_shape=[jax.ShapeDtypeStruct((length, CONV_CH), F32),
                   jax.ShapeDtypeStruct((SSD_GROUPS, length, LANE), F32), prm, prm, prm],
        scratch_shapes=[pltpu.VMEM((SSD_GROUPS, ppg, LANE, SSD_N), F32)],
        compiler_params=_cparams(("arbitrary", "arbitrary"), VMEM_LIMIT),
    )(xc, xc, xc, proj, alog_row, dtb_row, d_row, st, dy)


def _ret_fwd(proj, cos2, sin2, dmat, qdec, kdec, cdec, nw, mixed, length):
    gs, steps, rows = _gdn_steps(length)
    h_ = RET_HEADS

    def body(q_ref, k_ref, v_ref, g_ref, cos_ref, sin_ref, dm_ref, qd_ref, kd_ref, cd_ref, nw_ref, _,
             on_ref, st_ref, r_ref):
        @pl.when(pl.program_id(0) == 0)
        def _():
            r_ref[...] = jnp.zeros_like(r_ref)

        st_ref[...] = r_ref[...]
        lead = lambda ref: [ref[h] for h in range(h_)]
        on, r_new = _ret_chunk(_gdn_tiles(q_ref, gs), _gdn_tiles(k_ref, gs), _gdn_tiles(v_ref, gs),
                               _gdn_tiles(g_ref, gs), _chunk_rows(cos_ref, gs), _chunk_rows(sin_ref, gs),
                               lead(dm_ref), lead(qd_ref), lead(kd_ref), lead(cd_ref), nw_ref[...], lead(r_ref))
        for h in range(h_):
            for j in range(gs):
                on_ref[j * CHUNK:(j + 1) * CHUNK, h * LANE:(h + 1) * LANE] = on[j][h].astype(on_ref.dtype)
            r_ref[h] = r_new[h]

    hw = h_ * LANE
    blk = lambda name: pl.BlockSpec((rows, hw), lambda c: (c, _MINE[name] // hw))
    tab = pl.BlockSpec((rows, LANE), lambda c: (c, 0))
    full = lambda *s: pl.BlockSpec(s, lambda c: (0,) * len(s))
    return pl.pallas_call(
        body, name="ret_fwd", grid=(steps,),
        in_specs=[blk("rq"), blk("rk"), blk("rv"), blk("rg"), tab, tab,
                  full(h_, CHUNK, CHUNK), full(h_, CHUNK, 1), full(h_, CHUNK, 1), full(h_, 1, 1), full(1, LANE),
                  pl.BlockSpec(memory_space=pl.ANY)],
        out_specs=[pl.BlockSpec((rows, hw), lambda c: (c, (MIX_W - hw) // hw)),
                   pl.BlockSpec((h_, None, LANE, LANE), lambda c: (0, c, 0, 0))],
        out_shape=[jax.ShapeDtypeStruct(mixed.shape, mixed.dtype),
                   jax.ShapeDtypeStruct((h_, steps, LANE, LANE), F32)],
        input_output_aliases={11: 0},
        scratch_shapes=[pltpu.VMEM((h_, LANE, LANE), F32)],
        compiler_params=_cparams(("arbitrary",), VMEM_LIMIT),
    )(proj, proj, proj, proj, cos2, sin2, dmat, qdec, kdec, cdec, nw, mixed)


def _ret_bwd(proj, cos2, sin2, dmat, qdec, kdec, cdec, nw, st, dmix, dproj, length):
    gs, steps, rows = _gdn_steps(length)
    h_ = RET_HEADS
    hw = h_ * LANE

    def body(q_ref, k_ref, v_ref, g_ref, cos_ref, sin_ref, dm_ref, qd_ref, kd_ref, cd_ref, nw_ref, st_ref,
             do_ref, _, dqkvg_ref, dnw_ref, dr_ref):
        @pl.when(pl.program_id(0) == 0)
        def _():
            dr_ref[...] = jnp.zeros_like(dr_ref)
            dnw_ref[...] = jnp.zeros_like(dnw_ref)

        lead = lambda ref: [ref[h] for h in range(h_)]
        consts = (_chunk_rows(cos_ref, gs), _chunk_rows(sin_ref, gs), lead(dm_ref), lead(qd_ref), lead(kd_ref),
                  lead(cd_ref))
        f = lambda q, k, v, g, w_, r_: _ret_chunk(q, k, v, g, *consts, w_, r_)
        _, vjp = jax.vjp(f, _gdn_tiles(q_ref, gs), _gdn_tiles(k_ref, gs), _gdn_tiles(v_ref, gs),
                         _gdn_tiles(g_ref, gs), nw_ref[...], lead(st_ref))
        dq, dk, dv, dg, dnw, dr = vjp((_gdn_tiles(do_ref, gs), lead(dr_ref)))
        for h in range(h_):
            for j in range(gs):
                rws = slice(j * CHUNK, (j + 1) * CHUNK)
                for part, d in enumerate((dq, dk, dv, dg)):
                    col = part * hw + h * LANE
                    dqkvg_ref[rws, col:col + LANE] = d[j][h].astype(dqkvg_ref.dtype)
            dr_ref[h] = dr[h]
        dnw_ref[...] += dnw

    r = lambda c: steps - 1 - c
    blk = lambda name: pl.BlockSpec((rows, hw), lambda c: (r(c), _MINE[name] // hw))
    tab = pl.BlockSpec((rows, LANE), lambda c: (r(c), 0))
    full = lambda *s: pl.BlockSpec(s, lambda c: (0,) * len(s))
    assert _MINE["rq"] % (4 * hw) == 0 and [_MINE[n] - _MINE["rq"] for n in ("rk", "rv", "rg")] == [hw, 2 * hw, 3 * hw]
    return pl.pallas_call(
        body, name="ret_bwd", grid=(steps,),
        in_specs=[blk("rq"), blk("rk"), blk("rv"), blk("rg"), tab, tab,
                  full(h_, CHUNK, CHUNK), full(h_, CHUNK, 1), full(h_, CHUNK, 1), full(h_, 1, 1), full(1, LANE),
                  pl.BlockSpec((h_, None, LANE, LANE), lambda c: (0, r(c), 0, 0)),
                  pl.BlockSpec((rows, hw), lambda c: (r(c), (MIX_W - hw) // hw)),
                  pl.BlockSpec(memory_space=pl.ANY)],
        out_specs=[pl.BlockSpec((rows, 4 * hw), lambda c: (r(c), _MINE["rq"] // (4 * hw))), full(1, LANE)],
        out_shape=[jax.ShapeDtypeStruct(dproj.shape, dproj.dtype), jax.ShapeDtypeStruct((1, LANE), F32)],
        input_output_aliases={13: 0},
        scratch_shapes=[pltpu.VMEM((h_, LANE, LANE), F32)],
        compiler_params=_cparams(("arbitrary",), VMEM_LIMIT),
    )(proj, proj, proj, proj, cos2, sin2, dmat, qdec, kdec, cdec, nw, st, dmix, dproj)


def _mesh_pos():
    x, y, c = lax.axis_index("x"), lax.axis_index("y"), lax.axis_index("c")
    return x, y, c, 4 * x + 2 * y + c


def _peer(x, y, c, mask):
    return (x ^ ((mask >> 2) & 1), y ^ ((mask >> 1) & 1), c ^ (mask & 1))


def _comm_plan(kind, ins, outs, send_sems, recv_sems, local_sems):
    x, y, c, me = _mesh_pos()
    n = len(ins)
    src = (lambda a, idx: ins[a]) if kind == "gather" else (lambda a, idx: ins[a].at[idx])
    local = [pltpu.make_async_copy(src(a, me), outs[a].at[me], local_sems.at[a]) for a in range(n)]
    sends, recvs = [], []
    for mask in range(1, N_DEV):
        px, py, pc = _peer(x, y, c, mask)
        pidx = 4 * px + 2 * py + pc
        for a in range(n):
            k = (mask - 1) * n + a
            copy = lambda s, d, k=k: pltpu.make_async_remote_copy(
                src_ref=s, dst_ref=d, send_sem=send_sems.at[k], recv_sem=recv_sems.at[k],
                device_id=(px, py, pc), device_id_type=pl.DeviceIdType.MESH)
            sends.append(copy(src(a, pidx), outs[a].at[me]))
            recvs.append(copy(src(a, me), outs[a].at[pidx]))

    def start():
        for cp in local + sends:
            cp.start()

    def wait():
        for cp in sends:
            cp.wait_send()
        for cp in recvs:
            cp.wait_recv()
        for cp in local:
            cp.wait()

    return start, wait


def _comm_out_shapes(kind, arrays):
    return [jax.ShapeDtypeStruct(((N_DEV,) + s.shape) if kind == "gather" else s.shape, s.dtype) for s in arrays]


def _comm_scratch(n):
    k_tot = (N_DEV - 1) * n
    return [pltpu.SemaphoreType.DMA((k_tot,)), pltpu.SemaphoreType.DMA((k_tot,)), pltpu.SemaphoreType.DMA((n,))]


def _comm_call(name, kind, arrays):
    n = len(arrays)

    def body(*refs):
        start, wait = _comm_plan(kind, refs[:n], refs[n:2 * n], *refs[2 * n:])
        start()
        wait()

    hbm = pl.BlockSpec(memory_space=pl.ANY)
    return pl.pallas_call(
        body, name=name, in_specs=[hbm] * n, out_specs=[hbm] * n, out_shape=_comm_out_shapes(kind, arrays),
        scratch_shapes=_comm_scratch(n), compiler_params=pltpu.CompilerParams(has_side_effects=True),
    )(*arrays)


def _adam(name, parts, w, m, v, tr, layer=None, prev=None):
    p_, r_, cw = parts.shape
    c1 = 1.0 - ADAM_B1 ** ADAM_STEP
    c2 = 1.0 - ADAM_B2 ** ADAM_STEP
    n_prev = 4 if prev else 0

    def body(p_ref, w_ref, m_ref, v_ref, *rest):
        g_ref, d_ref, nm_ref, nv_ref = rest[n_prev:]
        g = p_ref[0].astype(F32)
        for i in range(1, p_):
            g = g + p_ref[i].astype(F32)
        nm = ADAM_B1 * m_ref[...] + (1.0 - ADAM_B1) * g
        nv = ADAM_B2 * v_ref[...] + (1.0 - ADAM_B2) * (g * g)
        d_ref[...] = -ADAM_LR * ((nm / c1) / (jnp.sqrt(nv / c2) + ADAM_EPS) + ADAM_WD * w_ref[...])
        g_ref[...] = g
        nm_ref[...] = nm
        nv_ref[...] = nv

    if layer is None:
        spec = pl.BlockSpec((tr, cw), lambda i: (i, 0))
        shp = jax.ShapeDtypeStruct((r_, cw), F32)
    else:
        spec = pl.BlockSpec((None, tr, cw), lambda i: (layer, i, 0))
        shp = jax.ShapeDtypeStruct((DEPTH, r_, cw), F32)
    return pl.pallas_call(
        body, name=name, grid=(r_ // tr,),
        in_specs=[pl.BlockSpec((p_, tr, cw), lambda i: (0, i, 0)), spec, spec, spec]
                 + [pl.BlockSpec(memory_space=pl.ANY)] * n_prev,
        out_specs=[spec] * 4, out_shape=[shp] * 4,
        input_output_aliases={4 + i: i for i in range(n_prev)},
        compiler_params=_cparams(("parallel",), VMEM_LIMIT),
    )(parts, w, m, v, *(prev or ()))


def _perm_cols(w_full):
    parts = [w_full[:, _ORIG[n][0]:_ORIG[n][0] + _ORIG[n][1]] for n in _ORDER]
    parts.append(jnp.zeros((w_full.shape[0], N_PAD - N_USED), w_full.dtype))
    return jnp.concatenate(parts, axis=1)


def _unperm_cols(g):
    by_orig = sorted(_ORDER, key=lambda n: _ORIG[n][0])
    return jnp.concatenate([g[:, _MINE[n]:_MINE[n] + _ORIG[n][1]] for n in by_orig], axis=1)


def _ret_consts(length):
    lg = jnp.log(1.0 - 2.0 ** (-5.0 - jnp.arange(RET_HEADS, dtype=F32)))
    idx = jnp.arange(CHUNK, dtype=F32)
    rel = idx[:, None] - idx[None, :]
    dmat = jnp.where(rel[None] >= 0, jnp.exp(jnp.maximum(rel, 0.0)[None] * lg[:, None, None]), 0.0)
    qdec = jnp.exp((idx[:, None] + 1.0) * lg[None, :]).T[:, :, None]
    kdec = jnp.exp((CHUNK - 1.0 - idx)[:, None] * lg[None, :]).T[:, :, None]
    cdec = jnp.exp(CHUNK * lg)[:, None, None]
    half = RET_DK // 2
    inv = ROPE_BASE ** (-jnp.arange(half, dtype=F32) / half)
    ang = jnp.arange(length, dtype=jnp.int32).astype(F32)[:, None] * inv[None, :]
    cos, sin = jnp.cos(ang), jnp.sin(ang)
    return (jnp.concatenate([cos, cos], axis=1), jnp.concatenate([-sin, sin], axis=1), dmat, qdec, kdec, cdec)


def _tiles(length, pref=512):
    return pref if length % pref == 0 else length


def _layer_fwd(x, p, consts, length, last, target, comm=None):
    tm = _tiles(length)
    (h,), _ = _rowmap("pre_norm", lambda x_, w_: ((_rms(x_, w_),), ()), length, tm,
                      [(x, D_MODEL, 0)], [p["pre_norm"]], out_ws=[D_MODEL], out_dtypes=[BF16])
    proj = _matmul("in_proj", h, p["w_in"], "nn", length, N_PAD, D_MODEL)
    xc = _conv_fwd(proj, p["conv_w"], p["conv_b"], length, _tiles(length, 1024), 512)
    mixed, st_a, *got = _gdn_fwd(xc, proj, p["gdn_A_log"], p["gdn_dt_bias"], p["gdn_norm"], length, comm)
    y_b, st_b = _ssd_fwd(xc, proj, p["ssd_A_log"], p["ssd_dt_bias"], p["ssd_D"], length)
    for g in range(SSD_GROUPS):
        (mixed,), _ = _rowmap(f"ssd_gate{g}", lambda y_, z_, w_: ((_rms(y_ * _silu(z_), w_),), ()), length, tm,
                              [(y_b, 512, g), (proj, 512, _MINE["sz"] // 512 + g)], [p["ssd_norm"][g]],
                              out_ws=[512], out_dtypes=[BF16], place={0: (mixed, MIX_W, 1 + g)})
    mixed, st_c = _ret_fwd(proj, *consts, p["ret_norm"], mixed, length)
    out = _matmul("out_proj", mixed, p["w_out"], "nn", length, D_MODEL, MIX_W)
    res = dict(x=x, h=h, proj=proj, xc=xc, st_a=st_a, st_b=st_b, st_c=st_c, y_b=y_b, mixed=mixed, out=out)
    if not last:
        (y,), _ = _rowmap("post_norm", lambda x_, o_, w_: ((x_ + _rms(o_, w_),), ()), length, tm,
                          [(x, D_MODEL, 0), (out, D_MODEL, 0)], [p["post_norm"]], out_ws=[D_MODEL])
        return y, res, None, got

    def head(x_, o_, t_, w_):
        e = x_ + _rms(o_, w_) - t_
        row = jnp.mean(e * e, axis=-1, keepdims=True)
        loss = 0.5 * jnp.sum(row, axis=0, keepdims=True)
        return (e * (1.0 / D_MODEL),), (loss + jnp.zeros((1, LANE), F32),)

    (dy,), (loss,) = _rowmap("loss_head", head, length, tm,
                             [(x, D_MODEL, 0), (out, D_MODEL, 0), (target, D_MODEL, 0)], [p["post_norm"]],
                             out_ws=[D_MODEL], acc_ws=[LANE])
    return None, res, (dy, loss[0, 0]), got


def _layer_bwd(dy, p, res, consts, length, comm=None):
    tm = _tiles(length)
    g = {}
    (d_out,), (g["post_norm"],) = _rowmap("post_norm_bwd", lambda o_, w_: ((_rms(o_, w_),), ()), length, tm,
                                          [(res["out"], D_MODEL, 0)], [p["post_norm"]], cts=[(dy, D_MODEL, 0)])
    dmix = _matmul("out_proj_dx", d_out, p["w_out"], "nt", length, MIX_W, D_MODEL)
    g["w_out"] = _matmul("out_proj_dw", res["mixed"], d_out, "tn", MIX_W, D_MODEL, length)
    proj, xc = res["proj"], res["xc"]
    (dxc_a, dproj, dsm_a, g["gdn_A_log"], g["gdn_dt_bias"], g["gdn_norm"], *got) = _gdn_bwd(
        xc, proj, p["gdn_A_log"], p["gdn_dt_bias"], p["gdn_norm"], res["st_a"], dmix, length, comm)
    dyb, dsn = None, []
    for gi in range(SSD_GROUPS):
        (dyb, dproj), (dn_g,) = _rowmap(
            f"ssd_gate_bwd{gi}", lambda y_, z_, w_: ((_rms(y_ * _silu(z_), w_),), ()), length, tm,
            [(res["y_b"], 512, gi), (proj, 512, _MINE["sz"] // 512 + gi)], [p["ssd_norm"][gi]],
            cts=[(dmix, 512, 1 + gi)], d_dtypes=[F32, BF16],
            place={0: (dyb, SSD_GROUPS * 512, gi), 1: (dproj, N_PAD, _MINE["sz"] // 512 + gi)})
        dsn.append(dn_g)
    g["ssd_norm"] = dsn
    dxc_b, dsm_b, g["ssd_A_log"], g["ssd_dt_bias"], g["ssd_D"] = _ssd_bwd(
        xc, proj, p["ssd_A_log"], p["ssd_dt_bias"], p["ssd_D"], res["st_b"], dyb, length)
    dproj, g["ret_norm"] = _ret_bwd(proj, *consts, p["ret_norm"], res["st_c"], dmix, dproj, length)
    tmc = _tiles(length, 1024)
    dpre, g["conv_w"], g["conv_b"] = _conv_bwd_pre(proj, p["conv_w"], p["conv_b"], dxc_a, dxc_b, length, tmc, 512)
    dproj = _conv_bwd_x(dpre, p["conv_w"], dproj, length, tmc, 512)
    pad = SMALL_OFF % 512 + 512 - LANE
    (dproj,), _ = _rowmap("dsmall", lambda a_, b_, c_: ((jnp.concatenate([a_ + b_ + c_, jnp.zeros((a_.shape[0], pad), F32)], axis=1),), ()),
                          length, tm, [(dsm_a, LANE, 0), (dsm_b[0], LANE, 0), (dsm_b[1], LANE, 0)], [],
                          out_ws=[512], out_dtypes=[BF16], place={0: (dproj, N_PAD, SMALL_OFF // 512)})
    dh = _matmul("in_proj_dx", dproj, p["w_in"], "nt", length, D_MODEL, N_PAD)
    g["w_in"] = _matmul("in_proj_dw", res["h"], dproj, "tn", D_MODEL, N_PAD, length)
    (dx,), (g["pre_norm"],) = _rowmap(
        "pre_norm_bwd", lambda x_, w_: ((_rms(x_, w_), x_), ()), length, tm,
        [(res["x"], D_MODEL, 0)], [p["pre_norm"]], cts=[(dh, D_MODEL, 0), (dy, D_MODEL, 0)])
    return dx, g, got


def _lane_row(vals, piece):
    lo = _MINE[piece] - SMALL_OFF
    return jnp.pad(vals[None], ((0, 0), (lo, LANE - lo - vals.shape[0])))


def _from_lane_row(row, piece, n):
    lo = _MINE[piece] - SMALL_OFF
    return row.reshape(-1, LANE).sum(axis=0)[lo:lo + n]


def _make_layer(l, w_in_full, w_out_full, conv_full, sw):
    return dict(
        pre_norm=sw["pre_norm"][l][None], post_norm=sw["post_norm"][l][None], w_in=_perm_cols(w_in_full),
        w_out=w_out_full,
        conv_w=jnp.concatenate([conv_full[0], conv_full[1]], axis=1),
        conv_b=jnp.concatenate([jnp.zeros((CONV_CH,), F32), sw["ssd_conv_b"][l]])[None],
        gdn_A_log=_lane_row(sw["gdn_A_log"][l], "ga"), gdn_dt_bias=_lane_row(sw["gdn_dt_bias"][l], "ga"),
        gdn_norm=sw["gdn_norm"][l][None],
        ssd_A_log=_lane_row(sw["ssd_A_log"][l], "sdt"), ssd_dt_bias=_lane_row(sw["ssd_dt_bias"][l], "sdt"),
        ssd_D=_lane_row(sw["ssd_D"][l], "sdt"),
        ssd_norm=[sw["ssd_norm"][l][None, 512 * g:512 * (g + 1)] for g in range(SSD_GROUPS)],
        ret_norm=sw["ret_norm"][l][None])


def _local_step(xb, tgt, layer0, make_layer1, length, fwd_comm=None, make_bwd_comm=None):
    consts = _ret_consts(length)
    y0, res0, _, got = _layer_fwd(xb, layer0, consts, length, False, tgt, fwd_comm)
    layer1 = make_layer1(got)
    _, res1, (dy, loss_local), _ = _layer_fwd(y0, layer1, consts, length, True, tgt)
    dy, grads1, _ = _layer_bwd(dy, layer1, res1, consts, length)
    dx, grads0, recv = _layer_bwd(dy, layer0, res0, consts, length, make_bwd_comm(grads1) if make_bwd_comm else None)
    return loss_local, dx, [grads0, grads1], recv


_SMALL = ["pre_norm", "post_norm", "gdn_A_log", "gdn_dt_bias", "gdn_norm", "ssd_conv_b", "ssd_A_log",
          "ssd_dt_bias", "ssd_D", "ssd_norm", "ret_norm"]


def _pack_small(arrs):
    rows = []
    for n in _SMALL:
        flat = arrs[n].reshape(-1)
        pad = (-flat.shape[0]) % LANE
        rows.append(jnp.pad(flat, (0, pad)).reshape(-1, LANE))
    out = jnp.concatenate(rows, axis=0)
    return jnp.pad(out, ((0, (-out.shape[0]) % SUBLANE), (0, 0)))


def _unpack_small(packed, like):
    out, r = {}, 0
    for n in _SMALL:
        cnt = like[n].size
        nrow = -(-cnt // LANE)
        out[n] = packed[r:r + nrow].reshape(-1)[:cnt].reshape(like[n].shape)
        r += nrow
    return out


def kernel(x, pre_norm, post_norm, w_in, gdn_conv, gdn_A_log, gdn_dt_bias, gdn_norm, ssd_conv, ssd_conv_b, ssd_A_log, ssd_dt_bias, ssd_D, ssd_norm, ret_norm, w_out, loss_target, m_pre_norm, m_post_norm, m_w_in, m_gdn_conv, m_gdn_A_log, m_gdn_dt_bias, m_gdn_norm, m_ssd_conv, m_ssd_conv_b, m_ssd_A_log, m_ssd_dt_bias, m_ssd_D, m_ssd_norm, m_ret_norm, m_w_out, v_pre_norm, v_post_norm, v_w_in, v_gdn_conv, v_gdn_A_log, v_gdn_dt_bias, v_gdn_norm, v_ssd_conv, v_ssd_conv_b, v_ssd_A_log, v_ssd_dt_bias, v_ssd_D, v_ssd_norm, v_ret_norm, v_w_out):
    length = x.shape[1]
    xb = x[0]
    tgt = loss_target[0]
    small_w = dict(pre_norm=pre_norm, post_norm=post_norm, gdn_A_log=gdn_A_log, gdn_dt_bias=gdn_dt_bias,
                   gdn_norm=gdn_norm, ssd_conv_b=ssd_conv_b, ssd_A_log=ssd_A_log, ssd_dt_bias=ssd_dt_bias,
                   ssd_D=ssd_D, ssd_norm=ssd_norm, ret_norm=ret_norm)
    small_m = dict(pre_norm=m_pre_norm, post_norm=m_post_norm, gdn_A_log=m_gdn_A_log, gdn_dt_bias=m_gdn_dt_bias,
                   gdn_norm=m_gdn_norm, ssd_conv_b=m_ssd_conv_b, ssd_A_log=m_ssd_A_log, ssd_dt_bias=m_ssd_dt_bias,
                   ssd_D=m_ssd_D, ssd_norm=m_ssd_norm, ret_norm=m_ret_norm)
    small_v = dict(pre_norm=v_pre_norm, post_norm=v_post_norm, gdn_A_log=v_gdn_A_log, gdn_dt_bias=v_gdn_dt_bias,
                   gdn_norm=v_gdn_norm, ssd_conv_b=v_ssd_conv_b, ssd_A_log=v_ssd_A_log, ssd_dt_bias=v_ssd_dt_bias,
                   ssd_D=v_ssd_D, ssd_norm=v_ssd_norm, ret_norm=v_ret_norm)

    w_in_b, w_out_b = w_in.astype(BF16), w_out.astype(BF16)
    conv_shard = jnp.stack([gdn_conv, ssd_conv], axis=1)
    shards = lambda l: [w_in_b[l], w_out_b[l], conv_shard[l]]

    def assemble(l, g_in, g_out, g_conv):
        w_in_full = g_in.transpose(1, 0, 2).reshape(D_MODEL, N_IN)
        w_out_full = g_out.reshape(MIX_W, D_MODEL)
        conv_full = g_conv.transpose(1, 2, 0, 3).reshape(2, CONV_W, CONV_CH)
        return _make_layer(l, w_in_full, w_out_full, conv_full, small_w)

    def grad_slabs(g):
        gi = _unperm_cols(g["w_in"]).reshape(D_MODEL, N_DEV, SHARD_IN).transpose(1, 0, 2).astype(BF16)
        go = g["w_out"].reshape(N_DEV, SHARD_OUT, D_MODEL).astype(BF16)
        gc = jnp.stack([g["conv_w"][:, k * CONV_CH:(k + 1) * CONV_CH].reshape(CONV_W, N_DEV, SHARD_CONV)
                        .transpose(1, 0, 2) for k in range(2)], axis=1).reshape(N_DEV, 2 * CONV_W, SHARD_CONV)
        return [gi, go, gc]

    layer0 = assemble(0, *_comm_call("gather_layer0", "gather", shards(0)))
    loss_local, dx, grads, recv1 = _local_step(
        xb, tgt, layer0, lambda got: assemble(1, *got), length,
        fwd_comm=("gather", shards(1)), make_bwd_comm=lambda g1: ("scatter", grad_slabs(g1)))
    grad_x = dx[None]
    loss = lax.psum(loss_local, ("x", "y", "c"))

    small_g = dict(
        pre_norm=jnp.concatenate([grads[l]["pre_norm"] for l in range(DEPTH)], axis=0),
        post_norm=jnp.concatenate([grads[l]["post_norm"] for l in range(DEPTH)], axis=0),
        gdn_A_log=jnp.stack([_from_lane_row(grads[l]["gdn_A_log"], "ga", GDN_HEADS) for l in range(DEPTH)]),
        gdn_dt_bias=jnp.stack([_from_lane_row(grads[l]["gdn_dt_bias"], "ga", GDN_HEADS) for l in range(DEPTH)]),
        gdn_norm=jnp.concatenate([grads[l]["gdn_norm"] for l in range(DEPTH)], axis=0),
        ssd_conv_b=jnp.concatenate([grads[l]["conv_b"][:, CONV_CH:] for l in range(DEPTH)], axis=0),
        ssd_A_log=jnp.stack([_from_lane_row(grads[l]["ssd_A_log"], "sdt", SSD_HEADS) for l in range(DEPTH)]),
        ssd_dt_bias=jnp.stack([_from_lane_row(grads[l]["ssd_dt_bias"], "sdt", SSD_HEADS) for l in range(DEPTH)]),
        ssd_D=jnp.stack([_from_lane_row(grads[l]["ssd_D"], "sdt", SSD_HEADS) for l in range(DEPTH)]),
        ssd_norm=jnp.concatenate([jnp.concatenate(grads[l]["ssd_norm"], axis=1) for l in range(DEPTH)], axis=0),
        ret_norm=jnp.concatenate([grads[l]["ret_norm"] for l in range(DEPTH)], axis=0))
    gs = _pack_small(small_g)
    gs8 = jnp.broadcast_to(gs[None], (N_DEV,) + gs.shape)
    *recv0, r_small = _comm_call("exchange_layer0", "scatter", grad_slabs(grads[0]) + [gs8])
    recv = [recv0, recv1]

    conv_w = lambda g_, s_, l: jnp.stack([g_[l], s_[l]], axis=0).reshape(2 * CONV_W, SHARD_CONV)
    o_in, o_out, o_conv = None, None, []
    for l in range(DEPTH):
        o_in = _adam(f"adam_w_in{l}", recv[l][0], w_in, m_w_in, v_w_in, 128, l, o_in)
        o_out = _adam(f"adam_w_out{l}", recv[l][1], w_out, m_w_out, v_w_out, 128, l, o_out)
        o_conv.append(_adam(f"adam_conv{l}", recv[l][2], conv_w(gdn_conv, ssd_conv, l),
                            conv_w(m_gdn_conv, m_ssd_conv, l), conv_w(v_gdn_conv, v_ssd_conv, l), 2 * CONV_W))
    ps_w, ps_m, ps_v = _pack_small(small_w), _pack_small(small_m), _pack_small(small_v)
    o_small = _adam("adam_small", r_small, ps_w, ps_m, ps_v, ps_w.shape[0])

    names = ["pre_norm", "post_norm", "w_in", "gdn_conv", "gdn_A_log", "gdn_dt_bias", "gdn_norm", "ssd_conv",
             "ssd_conv_b", "ssd_A_log", "ssd_dt_bias", "ssd_D", "ssd_norm", "ret_norm", "w_out"]
    outs = []
    for kind in range(4):
        d = _unpack_small(o_small[kind], small_w)
        cv = jnp.stack([o_conv[l][kind].reshape(2, CONV_W, SHARD_CONV) for l in range(DEPTH)], axis=1)
        d["w_in"] = o_in[kind]
        d["w_out"] = o_out[kind]
        d["gdn_conv"] = cv[0]
        d["ssd_conv"] = cv[1]
        outs.extend(d[n] for n in names)
    return (loss, grad_x, *outs)
```

```python
import functools
import math

import numpy as np
import jax
import jax.numpy as jnp
from jax import lax
from jax.experimental import pallas as pl
from jax.experimental.pallas import tpu as pltpu

F32 = jnp.float32
BF16 = jnp.bfloat16

D_MODEL = 1024
DEPTH = 2
CHUNK = 64
CONV_W = 4
EPS = 1e-6
N_DEV = 8

GDN_HEADS = 4
GDN_DK = 128
SSD_HEADS = 16
SSD_P = 64
SSD_N = 128
SSD_GROUPS = 2
SSD_PAIRS = SSD_HEADS // 2
PAIRS_PER_GROUP = SSD_PAIRS // SSD_GROUPS
RET_HEADS = 4
RET_DK = 128
ROPE_BASE = 10000.0
MIX_W = 2048
N_IN = 6680
SHARD_IN = N_IN // N_DEV
SHARD_OUT = MIX_W // N_DEV
CONV_CH = 1536
SHARD_CONV = CONV_CH // N_DEV

ADAM_LR = 0.001
ADAM_B1 = 0.9
ADAM_B2 = 0.999
ADAM_EPS = 1e-08
ADAM_WD = 0.01
ADAM_STEP = 10

LANE = 128
SUBLANE = 8
VMEM_LIMIT = 56 * 1024 * 1024

_ORIG = dict(gq=(0, 512), gk=(512, 512), gv=(1024, 512), gz=(1536, 512), gb=(2048, 4), ga=(2052, 4),
             sx=(2056, 1024), sB=(3080, 256), sC=(3336, 256), sz=(3592, 1024), sdt=(4616, 16),
             rq=(4632, 512), rk=(5144, 512), rv=(5656, 512), rg=(6168, 512))
_ORDER = ["rq", "rk", "rv", "rg", "gq", "gk", "gv", "sx", "sB", "sC", "gz", "sz", "gb", "ga", "sdt"]
_MINE = {}
_off = 0
for _n in _ORDER:
    _MINE[_n] = _off
    _off += _ORIG[_n][1]
N_USED = _off
N_PAD = 7168
CONV_ALL = 2 * CONV_CH
SMALL_OFF = _MINE["gb"]
CONV_OFF = _MINE["gq"]
XC = lambda name: _MINE[name] - CONV_OFF


def _cparams(sem, vmem=None):
    return pltpu.CompilerParams(dimension_semantics=sem, vmem_limit_bytes=vmem)


def _split_bf16(a):
    hi = a.astype(BF16)
    return hi, (a - hi.astype(F32)).astype(BF16)


def _make_mm():
    def raw(a, b, ca, cb):
        return lax.dot_general(a.astype(BF16), b.astype(BF16), (((ca,), (cb,)), ((), ())),
                               preferred_element_type=F32)

    @jax.custom_vjp
    def nn(a, b):
        return raw(a, b, 1, 0)

    @jax.custom_vjp
    def nt(a, b):
        return raw(a, b, 1, 1)

    @jax.custom_vjp
    def tn(a, b):
        return raw(a, b, 0, 0)

    nn.defvjp(lambda a, b: (raw(a, b, 1, 0), (a, b)), lambda r, g: (nt(g, r[1]), tn(r[0], g)))
    nt.defvjp(lambda a, b: (raw(a, b, 1, 1), (a, b)), lambda r, g: (nn(g, r[1]), tn(g, r[0])))
    tn.defvjp(lambda a, b: (raw(a, b, 0, 0), (a, b)), lambda r, g: (nt(r[1], g), nn(r[0], g)))
    return nn, nt, tn


_nn, _nt, _tn = _make_mm()


@jax.custom_vjp
def _swap_halves(t):
    return pltpu.roll(t, LANE // 2, 1)


_swap_halves.defvjp(lambda t: (pltpu.roll(t, LANE // 2, 1), None),
                    lambda _, g: (pltpu.roll(g, LANE // 2, 1),))


@jax.custom_vjp
def _split_rows(x):
    return tuple(x[i * CHUNK:(i + 1) * CHUNK] for i in range(x.shape[0] // CHUNK))


_split_rows.defvjp(lambda x: (tuple(x[i * CHUNK:(i + 1) * CHUNK] for i in range(x.shape[0] // CHUNK)), None),
                   lambda _, gs: (jnp.concatenate(gs, axis=0),))


def _dot3(a, b, ca, cb):
    dot = lambda x, y: lax.dot_general(x, y, (((ca,), (cb,)), ((), ())), preferred_element_type=F32)
    return dot(a[0], b[0]) + (dot(a[0], b[1]) + dot(a[1], b[0]))


@jax.custom_vjp
def _tri_inv(mats):
    return _tri_inv_impl(mats)


def _tri_inv_impl(mats):
    ii = lax.broadcasted_iota(jnp.int32, mats[0].shape, 0)
    jj = lax.broadcasted_iota(jnp.int32, mats[0].shape, 1)
    eye = jnp.where(ii == jj, 1.0, 0.0).astype(F32)
    ts = [eye - a for a in mats]
    ps = [_split_bf16(-a) for a in mats]
    for _ in range(int(math.log2(CHUNK)) - 1):
        ps = [_split_bf16(_dot3(p, p, 1, 0)) for p in ps]
        ts = [t + _dot3(_split_bf16(t), p, 1, 0) for t, p in zip(ts, ps)]
    return ts


def _tri_inv_bwd(ts, gs):
    tsp = [_split_bf16(t) for t in ts]
    xs = [_dot3(t, _split_bf16(g), 0, 0) for t, g in zip(tsp, gs)]
    return ([-_dot3(_split_bf16(x), t, 1, 1) for x, t in zip(xs, tsp)],)


_tri_inv.defvjp(lambda mats: (lambda ts: (ts, ts))(_tri_inv_impl(mats)), _tri_inv_bwd)


@jax.custom_vjp
def _tri_inv_saved(mats, ts):
    return ts


_tri_inv_saved.defvjp(lambda mats, ts: (ts, ts),
                      lambda ts, gs: (_tri_inv_bwd(ts, gs)[0], [jnp.zeros_like(t) for t in ts]))


def _silu(x):
    return x * jax.nn.sigmoid(x)


@jax.custom_vjp
def _softplus(x):
    return jnp.maximum(x, 0.0) + jnp.log1p(jnp.exp(-jnp.abs(x)))


_softplus.defvjp(lambda x: (jnp.maximum(x, 0.0) + jnp.log1p(jnp.exp(-jnp.abs(x))), x),
                 lambda x, g: (g * jax.nn.sigmoid(x),))


def _rms(x, w):
    return x * lax.rsqrt(jnp.mean(x * x, axis=-1, keepdims=True) + EPS) * w


def _chunk_masks(n):
    ii = lax.broadcasted_iota(jnp.int32, (n, n), 0)
    jj = lax.broadcasted_iota(jnp.int32, (n, n), 1)
    return ii >= jj, ii > jj, ii == jj, ii <= jj


def _cumsum_col(g, causal, eye, upper):
    g_row = jnp.sum(jnp.where(eye, g, 0.0), axis=0, keepdims=True)
    col = jnp.sum(jnp.where(causal, g_row, 0.0), axis=1, keepdims=True)
    row = jnp.sum(jnp.where(upper, g, 0.0), axis=0, keepdims=True)
    return col, row


def _lane_col(block, lane):
    pick = lax.broadcasted_iota(jnp.int32, (1, block.shape[1]), 1) == lane
    return jnp.sum(jnp.where(pick, block, 0.0), axis=1, keepdims=True)


def _gdn_chunk(q, k, v, gz, sm, alog_row, dtb_row, nw, s, t_saved=None):
    subs, nh = range(len(q)), len(q[0])
    c = q[0][0].shape[0]
    causal, _, eye, upper = _chunk_masks(c)
    stack = lambda xs: jnp.concatenate(xs, axis=0)
    per = lambda f: [[f(j, h) for h in range(nh)] for j in subs]
    qn = per(lambda j, h: q[j][h] * lax.rsqrt(jnp.sum(q[j][h] * q[j][h], axis=-1, keepdims=True) + EPS)
             * (GDN_DK ** -0.5))
    kn = per(lambda j, h: k[j][h] * lax.rsqrt(jnp.sum(k[j][h] * k[j][h], axis=-1, keepdims=True) + EPS))
    beta_blk = [jax.nn.sigmoid(sm[j]) for j in subs]
    g_blk = [-jnp.exp(alog_row) * _softplus(sm[j] + dtb_row) for j in subs]
    beta = per(lambda j, h: _lane_col(beta_blk[j], _MINE["gb"] - SMALL_OFF + h))
    g = per(lambda j, h: _lane_col(g_blk[j], _MINE["ga"] - SMALL_OFF + h))
    gcum = per(lambda j, h: _cumsum_col(g[j][h], causal, eye, upper)[0])
    eg = per(lambda j, h: jnp.exp(gcum[j][h]))
    glast = per(lambda j, h: jnp.sum(g[j][h], axis=0, keepdims=True))
    kb = per(lambda j, h: kn[j][h] * beta[j][h])
    n = nh * c
    ii = lax.broadcasted_iota(jnp.int32, (n, n), 0)
    jj = lax.broadcasted_iota(jnp.int32, (n, n), 1)
    sh = int(math.log2(c))
    same = lax.shift_right_logical(ii, sh) == lax.shift_right_logical(jj, sh)
    causal_bd = jnp.logical_and(same, ii >= jj)
    strict_bd = jnp.logical_and(same, ii > jj)
    gcum_all = [stack(gcum[j]) for j in subs]
    gcum_row = [jnp.sum(jnp.where(ii == jj, gcum_all[j], 0.0), axis=0, keepdims=True) for j in subs]
    decay = [jnp.where(causal_bd, jnp.exp(jnp.where(causal_bd, gcum_all[j] - gcum_row[j], 0.0)), 0.0) for j in subs]
    kn_all = [stack(kn[j]) for j in subs]
    a_low = [jnp.where(strict_bd, _nt(stack(kb[j]), kn_all[j]) * decay[j], 0.0) for j in subs]
    t = _tri_inv(a_low) if t_saved is None else _tri_inv_saved(a_low, t_saved)
    u = [_nn(t[j], stack([v[j][h] * beta[j][h] for h in range(nh)])) for j in subs]
    w = [_split_rows(_nn(t[j], stack([kb[j][h] * eg[j][h] for h in range(nh)]))) for j in subs]
    attn = [_nt(stack(qn[j]), kn_all[j]) * decay[j] for j in subs]
    kdec = per(lambda j, h: kn[j][h] * jnp.exp(glast[j][h] - gcum[j][h]))
    on = []
    for j in subs:
        v_new_all = u[j] - stack([_nn(w[j][h], s[h]) for h in range(nh)])
        o = _split_rows(stack([_nn(qn[j][h] * eg[j][h], s[h]) for h in range(nh)]) + _nn(attn[j], v_new_all))
        v_new = _split_rows(v_new_all)
        s = [s[h] * jnp.exp(glast[j][h]) + _tn(kdec[j][h], v_new[h]) for h in range(nh)]
        on.append([_rms(o[h], nw) * _silu(gz[j][h]) for h in range(nh)])
    return on, s, t


def _ssd_chunk(lanes, x, bm, cm, sm, alog_row, dtb_row, d_row, hs):
    subs, pairs = range(len(x)), range(len(x[0]))
    c = x[0][0].shape[0]
    lane_i = lax.broadcasted_iota(jnp.int32, (c, LANE), 1)
    lane_lo = lane_i < SSD_P
    lane_lo1 = lax.broadcasted_iota(jnp.int32, (1, LANE), 1) < SSD_P
    causal2 = lax.broadcasted_iota(jnp.int32, (c, LANE), 0) >= jnp.bitwise_and(lane_i, c - 1)
    row_lo = lax.broadcasted_iota(jnp.int32, (LANE, 1), 0) < SSD_P
    eye2 = lax.broadcasted_iota(jnp.int32, (LANE, LANE), 0) == lax.broadcasted_iota(jnp.int32, (LANE, LANE), 1)
    to_row = lambda cols: jnp.sum(jnp.where(eye2, jnp.concatenate(cols, axis=0), 0.0), axis=0, keepdims=True)
    per = lambda f: [[f(j, p) for p in pairs] for j in subs]
    both = lambda blk, p: [_lane_col(blk, lanes[p][h]) for h in range(2)]
    dt_blk = [_softplus(sm[j] + dtb_row) for j in subs]
    a_blk = [dt_blk[j] * -jnp.exp(alog_row) for j in subs]
    alast_blk = [jnp.sum(a_blk[j], axis=0, keepdims=True) for j in subs]
    dt = per(lambda j, p: both(dt_blk[j], p))
    a = per(lambda j, p: both(a_blk[j], p))
    alast = per(lambda j, p: both(alast_blk[j], p))
    dp = [both(d_row, p) for p in pairs]
    a_row = per(lambda j, p: to_row(a[j][p]))
    part = per(lambda j, p: jnp.where(causal2, a_row[j][p], 0.0))
    acum = per(lambda j, p: [jnp.sum(jnp.where(lane_lo, part[j][p], 0.0), axis=1, keepdims=True),
                             jnp.sum(jnp.where(lane_lo, 0.0, part[j][p]), axis=1, keepdims=True)])
    acum_row = per(lambda j, p: to_row(acum[j][p]))
    acum_col = per(lambda j, p: jnp.where(lane_lo, acum[j][p][0], acum[j][p][1]))
    lmat = per(lambda j, p: jnp.where(causal2, jnp.exp(jnp.where(causal2, acum_col[j][p] - acum_row[j][p], 0.0)), 0.0))
    cb = [_nt(cm[j], jnp.concatenate([bm[j], bm[j]], axis=0)) for j in subs]
    xdt = per(lambda j, p: x[j][p] * jnp.where(lane_lo, dt[j][p][0], dt[j][p][1]))
    xdt_rows = per(lambda j, p: jnp.concatenate([jnp.where(lane_lo, xdt[j][p], 0.0),
                                                 jnp.where(lane_lo, 0.0, xdt[j][p])], axis=0))
    intra = per(lambda j, p: _nn(cb[j] * lmat[j][p], xdt_rows[j][p]))
    skip = per(lambda j, p: x[j][p] * jnp.where(lane_lo1, dp[p][0], dp[p][1]))
    eacc = per(lambda j, p: jnp.exp(acum_col[j][p]))
    wdec = per(lambda j, p: jnp.exp(jnp.where(lane_lo, alast[j][p][0] - acum[j][p][0],
                                              alast[j][p][1] - acum[j][p][1])))
    scale = per(lambda j, p: jnp.exp(jnp.where(row_lo, alast[j][p][0], alast[j][p][1])))
    upd = per(lambda j, p: _tn(xdt[j][p] * wdec[j][p], bm[j]))
    y = []
    for j in subs:
        y.append([skip[j][p] + intra[j][p] + _nt(cm[j], hs[p]) * eacc[j][p] for p in pairs])
        hs = [hs[p] * scale[j][p] + upd[j][p] for p in pairs]
    return y, hs


def _ret_chunk(rq, rk, rv, rg, cos2, sin2, dmat, qdec, kdec, cdec, nw, r):
    subs, hs = range(len(rq)), range(len(rq[0]))
    per = lambda f: [[f(j, h) for h in hs] for j in subs]
    q = per(lambda j, h: rq[j][h] * cos2[j] + _swap_halves(rq[j][h]) * sin2[j])
    k = per(lambda j, h: (rk[j][h] * cos2[j] + _swap_halves(rk[j][h]) * sin2[j]) * (RET_DK ** -0.5))
    s = per(lambda j, h: _nt(q[j][h], k[j][h]) * dmat[h])
    intra = per(lambda j, h: _nn(s[j][h], rv[j][h]))
    upd = per(lambda j, h: _tn(k[j][h] * kdec[h], rv[j][h]))
    gate = per(lambda j, h: _silu(rg[j][h]))
    on = []
    for j in subs:
        on.append([_rms(intra[j][h] + _nn(q[j][h], r[h]) * qdec[h], nw) * gate[j][h] for h in hs])
        r = [r[h] * cdec[h] + upd[j][h] for h in hs]
    return on, r


MM_TILE = 1024


def _tile(n, pref=MM_TILE):
    return pref if n % pref == 0 else n


def _matmul(name, a, b, mode, m, n, k, tk=None):
    tm, tn = _tile(m), _tile(n)
    tk = _tile(k) if tk is None else tk
    nk = k // tk
    ca, cb = {"nn": (1, 0), "nt": (1, 1), "tn": (0, 0)}[mode]

    def body(a_ref, b_ref, o_ref):
        part = lax.dot_general(a_ref[...].astype(BF16), b_ref[...].astype(BF16),
                               (((ca,), (cb,)), ((), ())), preferred_element_type=F32)
        if nk == 1:
            o_ref[...] = part
        else:
            kk = pl.program_id(2)

            @pl.when(kk == 0)
            def _():
                o_ref[...] = part

            @pl.when(kk > 0)
            def _():
                o_ref[...] += part

    a_spec = pl.BlockSpec((tk, tm), lambda i, j, kk: (kk, i)) if mode == "tn" else pl.BlockSpec((tm, tk), lambda i, j, kk: (i, kk))
    b_spec = pl.BlockSpec((tn, tk), lambda i, j, kk: (j, kk)) if mode == "nt" else pl.BlockSpec((tk, tn), lambda i, j, kk: (kk, j))
    return pl.pallas_call(
        body, name=name, grid=(m // tm, n // tn, nk),
        in_specs=[a_spec, b_spec], out_specs=pl.BlockSpec((tm, tn), lambda i, j, kk: (i, j)),
        out_shape=jax.ShapeDtypeStruct((m, n), F32),
        compiler_params=_cparams(("parallel", "parallel", "arbitrary"), VMEM_LIMIT),
    )(a, b)


def _rowmap(name, fn, length, tm, rows, params, out_ws=(), acc_ws=(), out_dtypes=None, cts=None, nd_rows=(),
            d_dtypes=None, place=None):
    nt_ = length // tm
    n_r, n_nd, n_p = len(rows), len(nd_rows), len(params)
    rspec = lambda bw, cbk: pl.BlockSpec((tm, bw), lambda i: (i, cbk))
    pspec = lambda w: pl.BlockSpec((1, w), lambda i: (0, 0))
    in_arrays = [r[0] for r in rows] + [r[0] for r in nd_rows] + list(params)
    in_specs = [rspec(r[1], r[2]) for r in rows] + [rspec(r[1], r[2]) for r in nd_rows] + [pspec(p.shape[1]) for p in params]
    place = place or {}

    def load(refs):
        return [r[...].astype(F32) for r in refs]

    def placed(widths, dtypes, n_before):
        specs, shapes, extra, alias = [], [], [], {}
        for k, (w, dt) in enumerate(zip(widths, dtypes)):
            arr, total, cbk = place.get(k, (None, w, 0))
            specs.append(rspec(w, cbk))
            shapes.append(jax.ShapeDtypeStruct((length, total), dt))
            if arr is not None:
                alias[n_before + len(extra)] = k
                extra.append(arr)
        return specs, shapes, extra, alias

    if cts is None:
        n_o, n_a = len(out_ws), len(acc_ws)
        out_dtypes_ = out_dtypes or [F32] * n_o

        o_specs, o_shapes, extra, alias = placed(out_ws, out_dtypes_, n_r + n_nd + n_p)

        def body(*refs):
            ins = load(refs[:n_r + n_nd + n_p])
            outs, accs = fn(*ins)
            o_refs = refs[n_r + n_nd + n_p + len(extra):]
            for o_ref, o in zip(o_refs[:n_o], outs):
                o_ref[...] = o.astype(o_ref.dtype)
            first = pl.program_id(0) == 0
            for a_ref, acc in zip(o_refs[n_o:], accs):
                @pl.when(first)
                def _(a_ref=a_ref):
                    a_ref[...] = jnp.zeros_like(a_ref)
                a_ref[...] += acc

        res = pl.pallas_call(
            body, name=name, grid=(nt_,), in_specs=in_specs + [pl.BlockSpec(memory_space=pl.ANY)] * len(extra),
            out_specs=o_specs + [pspec(w) for w in acc_ws],
            out_shape=o_shapes + [jax.ShapeDtypeStruct((1, w), F32) for w in acc_ws],
            input_output_aliases=alias,
            compiler_params=_cparams(("arbitrary",), VMEM_LIMIT),
        )(*in_arrays, *extra)
        return res[:n_o], res[n_o:]

    n_c = len(cts)
    o_specs, o_shapes, extra, alias = placed([r[1] for r in rows], d_dtypes or [F32] * n_r, n_r + n_nd + n_p + n_c)

    def body(*refs):
        ins = load(refs[:n_r + n_nd + n_p])
        ct_vals = load(refs[n_r + n_nd + n_p:n_r + n_nd + n_p + n_c])
        nd = ins[n_r:n_r + n_nd]
        f = lambda rs, ps: fn(*rs, *nd, *ps)[0]
        _, vjp = jax.vjp(f, ins[:n_r], ins[n_r + n_nd:])
        d_rows, d_params = vjp(tuple(ct_vals))
        o_refs = refs[n_r + n_nd + n_p + n_c + len(extra):]
        for o_ref, d in zip(o_refs[:n_r], d_rows):
            o_ref[...] = d.astype(o_ref.dtype)
        first = pl.program_id(0) == 0
        for a_ref, d in zip(o_refs[n_r:], d_params):
            @pl.when(first)
            def _(a_ref=a_ref):
                a_ref[...] = jnp.zeros_like(a_ref)
            a_ref[...] += d

    res = pl.pallas_call(
        body, name=name, grid=(nt_,),
        in_specs=in_specs + [rspec(c[1], c[2]) for c in cts] + [pl.BlockSpec(memory_space=pl.ANY)] * len(extra),
        out_specs=o_specs + [pspec(p.shape[1]) for p in params],
        out_shape=o_shapes + [jax.ShapeDtypeStruct((1, p.shape[1]), F32) for p in params],
        input_output_aliases=alias,
        compiler_params=_cparams(("arbitrary",), VMEM_LIMIT),
    )(*in_arrays, *[c[0] for c in cts], *extra)
    return res[:n_r], res[n_r:]


def _conv_shift(ext, k, tm, forward):
    s = CONV_W - 1 - k
    if forward:
        rolled = ext if s == 0 else pltpu.roll(ext, s, 0)
        return rolled[SUBLANE:, :]
    rolled = ext if s == 0 else pltpu.roll(ext, tm + SUBLANE - s, 0)
    return rolled[:tm, :]


def _conv_pre(x_ref, halo_ref, w_ref, b_ref, first, tm):
    halo = jnp.where(first, 0.0, halo_ref[...])
    ext = jnp.concatenate([halo, x_ref[...]], axis=0)
    w = w_ref[...]
    pre = b_ref[...] + jnp.zeros_like(x_ref[...])
    taps = []
    for k in range(CONV_W):
        tap = _conv_shift(ext, k, tm, True)
        taps.append(tap)
        pre = pre + tap * w[k:k + 1, :]
    return pre, taps


def _conv_fwd(proj, w, b, length, tm, tc):
    hb = tm // SUBLANE
    j0 = CONV_OFF // tc

    def body(x_ref, halo_ref, w_ref, b_ref, o_ref):
        pre, _ = _conv_pre(x_ref, halo_ref, w_ref, b_ref, pl.program_id(0) == 0, tm)
        o_ref[...] = _silu(pre)

    return pl.pallas_call(
        body, name="conv_fwd", grid=(length // tm, CONV_ALL // tc),
        in_specs=[pl.BlockSpec((tm, tc), lambda i, j: (i, j0 + j)),
                  pl.BlockSpec((SUBLANE, tc), lambda i, j: (jnp.maximum(i * hb - 1, 0), j0 + j)),
                  pl.BlockSpec((CONV_W, tc), lambda i, j: (0, j)),
                  pl.BlockSpec((1, tc), lambda i, j: (0, j))],
        out_specs=pl.BlockSpec((tm, tc), lambda i, j: (i, j)),
        out_shape=jax.ShapeDtypeStruct((length, CONV_ALL), F32),
        compiler_params=_cparams(("parallel", "parallel"), VMEM_LIMIT),
    )(proj, proj, w, b)


def _conv_bwd_pre(proj, w, b, dxc_a, dxc_b, length, tm, tc):
    hb = tm // SUBLANE
    j0 = CONV_OFF // tc
    jh = CONV_CH // tc

    def body(x_ref, halo_ref, w_ref, b_ref, dya_ref, dyb_ref, dpre_ref, dw_ref, db_ref):
        i = pl.program_id(1)
        pre, taps = _conv_pre(x_ref, halo_ref, w_ref, b_ref, i == 0, tm)
        sg = jax.nn.sigmoid(pre)
        dy = jnp.where(pl.program_id(0) < jh, dya_ref[...], dyb_ref[...])
        dpre = dy * (sg * (1.0 + pre * (1.0 - sg)))
        dpre_ref[...] = dpre

        @pl.when(i == 0)
        def _():
            dw_ref[...] = jnp.zeros_like(dw_ref)
            db_ref[...] = jnp.zeros_like(db_ref)

        for k in range(CONV_W):
            dw_ref[k:k + 1, :] += jnp.sum(dpre * taps[k], axis=0, keepdims=True)
        db_ref[...] += jnp.sum(dpre, axis=0, keepdims=True)

    return pl.pallas_call(
        body, name="conv_bwd_pre", grid=(CONV_ALL // tc, length // tm),
        in_specs=[pl.BlockSpec((tm, tc), lambda j, i: (i, j0 + j)),
                  pl.BlockSpec((SUBLANE, tc), lambda j, i: (jnp.maximum(i * hb - 1, 0), j0 + j)),
                  pl.BlockSpec((CONV_W, tc), lambda j, i: (0, j)),
                  pl.BlockSpec((1, tc), lambda j, i: (0, j)),
                  pl.BlockSpec((tm, tc), lambda j, i: (jnp.where(j < jh, i, 0), jnp.minimum(j, jh - 1))),
                  pl.BlockSpec((tm, tc), lambda j, i: (jnp.where(j < jh, 0, i), jnp.maximum(j - jh, 0)))],
        out_specs=[pl.BlockSpec((tm, tc), lambda j, i: (i, j)),
                   pl.BlockSpec((CONV_W, tc), lambda j, i: (0, j)),
                   pl.BlockSpec((1, tc), lambda j, i: (0, j))],
        out_shape=[jax.ShapeDtypeStruct((length, CONV_ALL), F32),
                   jax.ShapeDtypeStruct((CONV_W, CONV_ALL), F32),
                   jax.ShapeDtypeStruct((1, CONV_ALL), F32)],
        compiler_params=_cparams(("parallel", "arbitrary"), VMEM_LIMIT),
    )(proj, proj, w, b, dxc_a, dxc_b)


def _conv_bwd_x(dpre, w, dproj, length, tm, tc):
    hb = tm // SUBLANE
    n_t = length // tm
    last_blk = length // SUBLANE - 1
    j0 = CONV_OFF // tc

    def body(d_ref, halo_ref, w_ref, _, o_ref):
        halo = jnp.where(pl.program_id(0) == n_t - 1, 0.0, halo_ref[...])
        ext = jnp.concatenate([d_ref[...], halo], axis=0)
        w = w_ref[...]
        acc = jnp.zeros_like(d_ref[...])
        for k in range(CONV_W):
            acc = acc + _conv_shift(ext, k, tm, False) * w[k:k + 1, :]
        o_ref[...] = acc.astype(o_ref.dtype)

    return pl.pallas_call(
        body, name="conv_bwd_x", grid=(n_t, CONV_ALL // tc),
        in_specs=[pl.BlockSpec((tm, tc), lambda i, j: (i, j)),
                  pl.BlockSpec((SUBLANE, tc), lambda i, j: (jnp.minimum((i + 1) * hb, last_blk), j)),
                  pl.BlockSpec((CONV_W, tc), lambda i, j: (0, j)),
                  pl.BlockSpec(memory_space=pl.ANY)],
        out_specs=pl.BlockSpec((tm, tc), lambda i, j: (i, j0 + j)),
        out_shape=jax.ShapeDtypeStruct(dproj.shape, dproj.dtype), input_output_aliases={3: 0},
        compiler_params=_cparams(("parallel", "parallel"), VMEM_LIMIT),
    )(dpre, dpre, w, dproj)


GDN_CHUNKS_PER_STEP = 4


def _gdn_steps(length):
    nc = length // CHUNK
    gs = GDN_CHUNKS_PER_STEP if nc % GDN_CHUNKS_PER_STEP == 0 else 1
    return gs, nc // gs, gs * CHUNK


def _gdn_tiles(ref, gs):
    return [[ref[j * CHUNK:(j + 1) * CHUNK, h * LANE:(h + 1) * LANE] for h in range(GDN_HEADS)] for j in range(gs)]


def _chunk_rows(ref, gs):
    return [ref[j * CHUNK:(j + 1) * CHUNK, :] for j in range(gs)]


def _gdn_fwd(xc, proj, alog, dtb, nw, length, comm=None):
    gs, steps, rows = _gdn_steps(length)
    h_ = GDN_HEADS
    hw = h_ * LANE
    tn_ = h_ * CHUNK
    n_comm = len(comm[1]) if comm else 0

    def body(*refs):
        q_ref, k_ref, v_ref, gz_ref, sm_ref, al_ref, dt_ref, nw_ref = refs[:8]
        c_in = refs[8:8 + n_comm]
        on_ref, st_ref, t_ref = refs[8 + n_comm:11 + n_comm]
        c_out = refs[11 + n_comm:11 + 2 * n_comm]
        s_ref = refs[11 + 2 * n_comm]
        sems = refs[12 + 2 * n_comm:]

        @pl.when(pl.program_id(0) == 0)
        def _():
            s_ref[...] = jnp.zeros_like(s_ref)
            if comm:
                _comm_plan(comm[0], c_in, c_out, *sems)[0]()

        st_ref[...] = s_ref[...]
        lead = lambda ref: [ref[h] for h in range(h_)]
        on, s_new, t = _gdn_chunk(_gdn_tiles(q_ref, gs), _gdn_tiles(k_ref, gs), _gdn_tiles(v_ref, gs),
                                  _gdn_tiles(gz_ref, gs), _chunk_rows(sm_ref, gs), al_ref[...], dt_ref[...],
                                  nw_ref[...], lead(s_ref))
        for j in range(gs):
            t_ref[j] = t[j]
        for h in range(h_):
            for j in range(gs):
                on_ref[j * CHUNK:(j + 1) * CHUNK, h * LANE:(h + 1) * LANE] = on[j][h].astype(on_ref.dtype)
            s_ref[h] = s_new[h]

        if comm:
            @pl.when(pl.program_id(0) == steps - 1)
            def _():
                _comm_plan(comm[0], c_in, c_out, *sems)[1]()

    blk = lambda col: pl.BlockSpec((rows, hw), lambda c: (c, col // hw))
    one = pl.BlockSpec((1, LANE), lambda c: (0, 0))
    hbm = pl.BlockSpec(memory_space=pl.ANY)
    return pl.pallas_call(
        body, name="gdn_fwd_" + comm[0] if comm else "gdn_fwd", grid=(steps,),
        in_specs=[blk(XC("gq")), blk(XC("gk")), blk(XC("gv")), blk(_MINE["gz"]),
                  pl.BlockSpec((rows, LANE), lambda c: (c, SMALL_OFF // LANE)), one, one,
                  pl.BlockSpec((1, LANE), lambda c: (0, 0))] + [hbm] * n_comm,
        out_specs=[pl.BlockSpec((rows, hw), lambda c: (c, 0)),
                   pl.BlockSpec((h_, None, LANE, LANE), lambda c: (0, c, 0, 0)),
                   pl.BlockSpec((gs, tn_, tn_), lambda c: (c, 0, 0))] + [hbm] * n_comm,
        out_shape=[jax.ShapeDtypeStruct((length, MIX_W), BF16),
                   jax.ShapeDtypeStruct((h_, steps, LANE, LANE), F32),
                   jax.ShapeDtypeStruct((length // CHUNK, tn_, tn_), F32)]
                  + (_comm_out_shapes(*comm) if comm else []),
        scratch_shapes=[pltpu.VMEM((h_, LANE, LANE), F32)] + (_comm_scratch(n_comm) if comm else []),
        compiler_params=_cparams(("arbitrary",), VMEM_LIMIT),
    )(xc, xc, xc, proj, proj, alog, dtb, nw, *(comm[1] if comm else []))


def _gdn_bwd(xc, proj, alog, dtb, nw, st, tinv, dmix, length, comm=None):
    gs, steps, rows = _gdn_steps(length)
    h_ = GDN_HEADS
    hw = h_ * LANE
    tn_ = h_ * CHUNK
    n_comm = len(comm[1]) if comm else 0
    n_in, n_out = 11, 6

    def body(*refs):
        q_ref, k_ref, v_ref, gz_ref, sm_ref, al_ref, dt_ref, nw_ref, st_ref, t_ref, do_ref = refs[:n_in]
        c_in = refs[n_in:n_in + n_comm]
        o0 = n_in + n_comm
        dqkv_ref, dgz_ref, dsm_ref, dal_ref, ddt_ref, dnw_ref = refs[o0:o0 + n_out]
        c_out = refs[o0 + n_out:o0 + n_out + n_comm]
        ds_ref = refs[o0 + n_out + n_comm]
        sems = refs[o0 + n_out + n_comm + 1:]

        @pl.when(pl.program_id(0) == 0)
        def _():
            ds_ref[...] = jnp.zeros_like(ds_ref)
            dal_ref[...] = jnp.zeros_like(dal_ref)
            ddt_ref[...] = jnp.zeros_like(ddt_ref)
            dnw_ref[...] = jnp.zeros_like(dnw_ref)
            if comm:
                _comm_plan(comm[0], c_in, c_out, *sems)[0]()

        lead = lambda ref: [ref[h] for h in range(h_)]
        ins = (_gdn_tiles(q_ref, gs), _gdn_tiles(k_ref, gs), _gdn_tiles(v_ref, gs), _gdn_tiles(gz_ref, gs),
               _chunk_rows(sm_ref, gs), al_ref[...], dt_ref[...], nw_ref[...], lead(st_ref))
        t_saved = [t_ref[j] for j in range(gs)]
        _, vjp = jax.vjp(lambda *a: _gdn_chunk(*a, t_saved=t_saved)[:2], *ins)
        dq, dk, dv, dgz, dsm, dal, ddt, dnw, ds = vjp((_gdn_tiles(do_ref, gs), lead(ds_ref)))
        for j in range(gs):
            rws = slice(j * CHUNK, (j + 1) * CHUNK)
            for h in range(h_):
                cols = slice(h * LANE, (h + 1) * LANE)
                for part, d in enumerate((dq, dk, dv)):
                    dqkv_ref[rws, part * hw + h * LANE:part * hw + (h + 1) * LANE] = d[j][h]
                dgz_ref[rws, cols] = dgz[j][h].astype(dgz_ref.dtype)
            dsm_ref[rws, :] = dsm[j]
        for h in range(h_):
            ds_ref[h] = ds[h]
        dal_ref[...] += dal
        ddt_ref[...] += ddt
        dnw_ref[...] += dnw

        if comm:
            @pl.when(pl.program_id(0) == steps - 1)
            def _():
                _comm_plan(comm[0], c_in, c_out, *sems)[1]()

    r = lambda c: steps - 1 - c
    blk = lambda col: pl.BlockSpec((rows, hw), lambda c: (r(c), col // hw))
    one = pl.BlockSpec((1, LANE), lambda c: (0, 0))
    hbm = pl.BlockSpec(memory_space=pl.ANY)
    return pl.pallas_call(
        body, name="gdn_bwd_" + comm[0] if comm else "gdn_bwd", grid=(steps,),
        in_specs=[blk(XC("gq")), blk(XC("gk")), blk(XC("gv")), blk(_MINE["gz"]),
                  pl.BlockSpec((rows, LANE), lambda c: (r(c), SMALL_OFF // LANE)), one, one,
                  pl.BlockSpec((1, LANE), lambda c: (0, 0)),
                  pl.BlockSpec((h_, None, LANE, LANE), lambda c: (0, r(c), 0, 0)),
                  pl.BlockSpec((gs, tn_, tn_), lambda c: (r(c), 0, 0)),
                  blk(0)] + [hbm] * n_comm,
        out_specs=[pl.BlockSpec((rows, 3 * hw), lambda c: (r(c), 0)), blk(_MINE["gz"]),
                   pl.BlockSpec((rows, LANE), lambda c: (r(c), 0)), one, one,
                   pl.BlockSpec((1, LANE), lambda c: (0, 0))] + [hbm] * n_comm,
        out_shape=[jax.ShapeDtypeStruct((length, 3 * hw), F32), jax.ShapeDtypeStruct((length, N_PAD), BF16),
                   jax.ShapeDtypeStruct((length, LANE), F32),
                   jax.ShapeDtypeStruct((1, LANE), F32), jax.ShapeDtypeStruct((1, LANE), F32),
                   jax.ShapeDtypeStruct((1, LANE), F32)] + (_comm_out_shapes(*comm) if comm else []),
        scratch_shapes=[pltpu.VMEM((h_, LANE, LANE), F32)] + (_comm_scratch(n_comm) if comm else []),
        compiler_params=_cparams(("arbitrary",), VMEM_LIMIT),
    )(xc, xc, xc, proj, proj, alog, dtb, nw, st, tinv, dmix, *(comm[1] if comm else []))


SSD_CHUNKS_PER_STEP = 4


def _ssd_steps(length):
    nc = length // CHUNK
    gs = SSD_CHUNKS_PER_STEP if nc % SSD_CHUNKS_PER_STEP == 0 else 1
    return gs, nc // gs, gs * CHUNK


def _ssd_lanes(g):
    base = _MINE["sdt"] - SMALL_OFF
    return [[base + (g * PAIRS_PER_GROUP + p) * 2 + j for j in range(2)] for p in range(PAIRS_PER_GROUP)]


def _ssd_tiles(ref, gs):
    return [[ref[j * CHUNK:(j + 1) * CHUNK, p * LANE:(p + 1) * LANE] for p in range(PAIRS_PER_GROUP)]
            for j in range(gs)]


def _ssd_fwd(xc, proj, alog_row, dtb_row, d_row, length):
    gs, steps, rows = _ssd_steps(length)
    ppg = PAIRS_PER_GROUP
    gw = ppg * LANE

    def body(x_ref, b_ref, c_ref, sm_ref, al_ref, db_ref, dp_ref, y_ref, st_ref, hs_ref):
        @pl.when(pl.program_id(1) == 0)
        def _():
            hs_ref[...] = jnp.zeros_like(hs_ref)

        st_ref[...] = hs_ref[...]
        y, hs_new = _ssd_chunk(_ssd_lanes(pl.program_id(0)), _ssd_tiles(x_ref, gs), _chunk_rows(b_ref, gs),
                               _chunk_rows(c_ref, gs), _chunk_rows(sm_ref, gs), al_ref[...], db_ref[...],
                               dp_ref[...], [hs_ref[p] for p in range(ppg)])
        for p in range(ppg):
            for j in range(gs):
                y_ref[j * CHUNK:(j + 1) * CHUNK, p * LANE:(p + 1) * LANE] = y[j][p]
            hs_ref[p] = hs_new[p]

    one = pl.BlockSpec((1, LANE), lambda g, c: (0, 0))
    return pl.pallas_call(
        body, name="ssd_fwd", grid=(SSD_GROUPS, steps),
        in_specs=[pl.BlockSpec((rows, gw), lambda g, c: (c, XC("sx") // gw + g)),
                  pl.BlockSpec((rows, LANE), lambda g, c: (c, XC("sB") // LANE + g)),
                  pl.BlockSpec((rows, LANE), lambda g, c: (c, XC("sC") // LANE + g)),
                  pl.BlockSpec((rows, LANE), lambda g, c: (c, SMALL_OFF // LANE)),
                  one, one, one],
        out_specs=[pl.BlockSpec((rows, gw), lambda g, c: (c, g)),
                   pl.BlockSpec((None, ppg, None, LANE, SSD_N), lambda g, c: (g, 0, c, 0, 0))],
        out_shape=[jax.ShapeDtypeStruct((length, SSD_GROUPS * gw), F32),
                   jax.ShapeDtypeStruct((SSD_GROUPS, ppg, steps, LANE, SSD_N), F32)],
        scratch_shapes=[pltpu.VMEM((ppg, LANE, SSD_N), F32)],
        compiler_params=_cparams(("parallel", "arbitrary"), VMEM_LIMIT),
    )(xc, xc, xc, proj, alog_row, dtb_row, d_row)


def _ssd_bwd(xc, proj, alog_row, dtb_row, d_row, st, dy, length):
    gs, steps, rows = _ssd_steps(length)
    ppg = PAIRS_PER_GROUP
    gw = ppg * LANE

    def body(x_ref, b_ref, c_ref, sm_ref, al_ref, db_ref, dp_ref, st_ref, dy_ref,
             dxc_ref, dsm_ref, dal_ref, ddb_ref, ddp_ref, dhs_ref):
        g = pl.program_id(1)

        @pl.when(jnp.logical_and(pl.program_id(0) == 0, g == 0))
        def _():
            dhs_ref[...] = jnp.zeros_like(dhs_ref)
            dal_ref[...] = jnp.zeros_like(dal_ref)
            ddb_ref[...] = jnp.zeros_like(ddb_ref)
            ddp_ref[...] = jnp.zeros_like(ddp_ref)

        pairs = range(ppg)
        ins = (_ssd_tiles(x_ref, gs), _chunk_rows(b_ref, gs), _chunk_rows(c_ref, gs), _chunk_rows(sm_ref, gs),
               al_ref[...], db_ref[...], dp_ref[...], [st_ref[p] for p in pairs])
        _, vjp = jax.vjp(functools.partial(_ssd_chunk, _ssd_lanes(g)), *ins)
        dx, dbm, dcm, dsm, dal, ddb, ddp, dhs = vjp((_ssd_tiles(dy_ref, gs), [dhs_ref[g, p] for p in pairs]))
        for grp in range(SSD_GROUPS):
            @pl.when(g == grp)
            def _(grp=grp):
                for j in range(gs):
                    rws = slice(j * CHUNK, (j + 1) * CHUNK)
                    for p in pairs:
                        col = grp * gw + p * LANE
                        dxc_ref[rws, col:col + LANE] = dx[j][p]
                    cb_ = XC("sB") - XC("sx") + grp * SSD_N
                    cc_ = XC("sC") - XC("sx") + grp * SSD_N
                    dxc_ref[rws, cb_:cb_ + SSD_N] = dbm[j]
                    dxc_ref[rws, cc_:cc_ + SSD_N] = dcm[j]
        for j in range(gs):
            dsm_ref[j * CHUNK:(j + 1) * CHUNK, :] = dsm[j]
        for p in pairs:
            dhs_ref[g, p] = dhs[p]
        dal_ref[g] += dal
        ddb_ref[g] += ddb
        ddp_ref[g] += ddp

    r = lambda c: steps - 1 - c
    one = pl.BlockSpec((1, LANE), lambda c, g: (0, 0))
    acc = pl.BlockSpec((SSD_GROUPS, 1, LANE), lambda c, g: (0, 0, 0))
    prm = jax.ShapeDtypeStruct((SSD_GROUPS, 1, LANE), F32)
    return pl.pallas_call(
        body, name="ssd_bwd", grid=(steps, SSD_GROUPS),
        in_specs=[pl.BlockSpec((rows, gw), lambda c, g: (r(c), XC("sx") // gw + g)),
                  pl.BlockSpec((rows, LANE), lambda c, g: (r(c), XC("sB") // LANE + g)),
                  pl.BlockSpec((rows, LANE), lambda c, g: (r(c), XC("sC") // LANE + g)),
                  pl.BlockSpec((rows, LANE), lambda c, g: (r(c), SMALL_OFF // LANE)),
                  one, one, one,
                  pl.BlockSpec((None, ppg, None, LANE, SSD_N), lambda c, g: (g, 0, r(c), 0, 0)),
                  pl.BlockSpec((rows, gw), lambda c, g: (r(c), g))],
        out_specs=[pl.BlockSpec((rows, CONV_CH), lambda c, g: (r(c), 0)),
                   pl.BlockSpec((None, rows, LANE), lambda c, g: (g, r(c), 0)), acc, acc, acc],
        out_shape=[jax.ShapeDtypeStruct((length, CONV_CH), F32),
                   jax.ShapeDtypeStruct((SSD_GROUPS, length, LANE), F32), prm, prm, prm],
        scratch_shapes=[pltpu.VMEM((SSD_GROUPS, ppg, LANE, SSD_N), F32)],
        compiler_params=_cparams(("arbitrary", "arbitrary"), VMEM_LIMIT),
    )(xc, xc, xc, proj, alog_row, dtb_row, d_row, st, dy)


def _ret_fwd(proj, cos2, sin2, dmat, qdec, kdec, cdec, nw, mixed, length):
    gs, steps, rows = _gdn_steps(length)
    h_ = RET_HEADS

    def body(q_ref, k_ref, v_ref, g_ref, cos_ref, sin_ref, dm_ref, qd_ref, kd_ref, cd_ref, nw_ref, _,
             on_ref, st_ref, r_ref):
        @pl.when(pl.program_id(0) == 0)
        def _():
            r_ref[...] = jnp.zeros_like(r_ref)

        st_ref[...] = r_ref[...]
        lead = lambda ref: [ref[h] for h in range(h_)]
        on, r_new = _ret_chunk(_gdn_tiles(q_ref, gs), _gdn_tiles(k_ref, gs), _gdn_tiles(v_ref, gs),
                               _gdn_tiles(g_ref, gs), _chunk_rows(cos_ref, gs), _chunk_rows(sin_ref, gs),
                               lead(dm_ref), lead(qd_ref), lead(kd_ref), lead(cd_ref), nw_ref[...], lead(r_ref))
        for h in range(h_):
            for j in range(gs):
                on_ref[j * CHUNK:(j + 1) * CHUNK, h * LANE:(h + 1) * LANE] = on[j][h].astype(on_ref.dtype)
            r_ref[h] = r_new[h]

    hw = h_ * LANE
    blk = lambda name: pl.BlockSpec((rows, hw), lambda c: (c, _MINE[name] // hw))
    tab = pl.BlockSpec((rows, LANE), lambda c: (c, 0))
    full = lambda *s: pl.BlockSpec(s, lambda c: (0,) * len(s))
    return pl.pallas_call(
        body, name="ret_fwd", grid=(steps,),
        in_specs=[blk("rq"), blk("rk"), blk("rv"), blk("rg"), tab, tab,
                  full(h_, CHUNK, CHUNK), full(h_, CHUNK, 1), full(h_, CHUNK, 1), full(h_, 1, 1), full(1, LANE),
                  pl.BlockSpec(memory_space=pl.ANY)],
        out_specs=[pl.BlockSpec((rows, hw), lambda c: (c, (MIX_W - hw) // hw)),
                   pl.BlockSpec((h_, None, LANE, LANE), lambda c: (0, c, 0, 0))],
        out_shape=[jax.ShapeDtypeStruct(mixed.shape, mixed.dtype),
                   jax.ShapeDtypeStruct((h_, steps, LANE, LANE), F32)],
        input_output_aliases={11: 0},
        scratch_shapes=[pltpu.VMEM((h_, LANE, LANE), F32)],
        compiler_params=_cparams(("arbitrary",), VMEM_LIMIT),
    )(proj, proj, proj, proj, cos2, sin2, dmat, qdec, kdec, cdec, nw, mixed)


def _ret_bwd(proj, cos2, sin2, dmat, qdec, kdec, cdec, nw, st, dmix, dproj, length):
    gs, steps, rows = _gdn_steps(length)
    h_ = RET_HEADS
    hw = h_ * LANE

    def body(q_ref, k_ref, v_ref, g_ref, cos_ref, sin_ref, dm_ref, qd_ref, kd_ref, cd_ref, nw_ref, st_ref,
             do_ref, _, dqkvg_ref, dnw_ref, dr_ref):
        @pl.when(pl.program_id(0) == 0)
        def _():
            dr_ref[...] = jnp.zeros_like(dr_ref)
            dnw_ref[...] = jnp.zeros_like(dnw_ref)

        lead = lambda ref: [ref[h] for h in range(h_)]
        consts = (_chunk_rows(cos_ref, gs), _chunk_rows(sin_ref, gs), lead(dm_ref), lead(qd_ref), lead(kd_ref),
                  lead(cd_ref))
        f = lambda q, k, v, g, w_, r_: _ret_chunk(q, k, v, g, *consts, w_, r_)
        _, vjp = jax.vjp(f, _gdn_tiles(q_ref, gs), _gdn_tiles(k_ref, gs), _gdn_tiles(v_ref, gs),
                         _gdn_tiles(g_ref, gs), nw_ref[...], lead(st_ref))
        dq, dk, dv, dg, dnw, dr = vjp((_gdn_tiles(do_ref, gs), lead(dr_ref)))
        for h in range(h_):
            for j in range(gs):
                rws = slice(j * CHUNK, (j + 1) * CHUNK)
                for part, d in enumerate((dq, dk, dv, dg)):
                    col = part * hw + h * LANE
                    dqkvg_ref[rws, col:col + LANE] = d[j][h].astype(dqkvg_ref.dtype)
            dr_ref[h] = dr[h]
        dnw_ref[...] += dnw

    r = lambda c: steps - 1 - c
    blk = lambda name: pl.BlockSpec((rows, hw), lambda c: (r(c), _MINE[name] // hw))
    tab = pl.BlockSpec((rows, LANE), lambda c: (r(c), 0))
    full = lambda *s: pl.BlockSpec(s, lambda c: (0,) * len(s))
    assert _MINE["rq"] % (4 * hw) == 0 and [_MINE[n] - _MINE["rq"] for n in ("rk", "rv", "rg")] == [hw, 2 * hw, 3 * hw]
    return pl.pallas_call(
        body, name="ret_bwd", grid=(steps,),
        in_specs=[blk("rq"), blk("rk"), blk("rv"), blk("rg"), tab, tab,
                  full(h_, CHUNK, CHUNK), full(h_, CHUNK, 1), full(h_, CHUNK, 1), full(h_, 1, 1), full(1, LANE),
                  pl.BlockSpec((h_, None, LANE, LANE), lambda c: (0, r(c), 0, 0)),
                  pl.BlockSpec((rows, hw), lambda c: (r(c), (MIX_W - hw) // hw)),
                  pl.BlockSpec(memory_space=pl.ANY)],
        out_specs=[pl.BlockSpec((rows, 4 * hw), lambda c: (r(c), _MINE["rq"] // (4 * hw))), full(1, LANE)],
        out_shape=[jax.ShapeDtypeStruct(dproj.shape, dproj.dtype), jax.ShapeDtypeStruct((1, LANE), F32)],
        input_output_aliases={13: 0},
        scratch_shapes=[pltpu.VMEM((h_, LANE, LANE), F32)],
        compiler_params=_cparams(("arbitrary",), VMEM_LIMIT),
    )(proj, proj, proj, proj, cos2, sin2, dmat, qdec, kdec, cdec, nw, st, dmix, dproj)


def _mesh_pos():
    x, y, c = lax.axis_index("x"), lax.axis_index("y"), lax.axis_index("c")
    return x, y, c, 4 * x + 2 * y + c


def _peer(x, y, c, mask):
    return (x ^ ((mask >> 2) & 1), y ^ ((mask >> 1) & 1), c ^ (mask & 1))


def _comm_plan(kind, ins, outs, send_sems, recv_sems, local_sems):
    x, y, c, me = _mesh_pos()
    n = len(ins)
    src = (lambda a, idx: ins[a]) if kind == "gather" else (lambda a, idx: ins[a].at[idx])
    local = [pltpu.make_async_copy(src(a, me), outs[a].at[me], local_sems.at[a]) for a in range(n)]
    sends, recvs = [], []
    for mask in range(1, N_DEV):
        px, py, pc = _peer(x, y, c, mask)
        pidx = 4 * px + 2 * py + pc
        for a in range(n):
            k = (mask - 1) * n + a
            copy = lambda s, d, k=k: pltpu.make_async_remote_copy(
                src_ref=s, dst_ref=d, send_sem=send_sems.at[k], recv_sem=recv_sems.at[k],
                device_id=(px, py, pc), device_id_type=pl.DeviceIdType.MESH)
            sends.append(copy(src(a, pidx), outs[a].at[me]))
            recvs.append(copy(src(a, me), outs[a].at[pidx]))

    def start():
        for cp in local + sends:
            cp.start()

    def wait():
        for cp in sends:
            cp.wait_send()
        for cp in recvs:
            cp.wait_recv()
        for cp in local:
            cp.wait()

    return start, wait


def _comm_out_shapes(kind, arrays):
    return [jax.ShapeDtypeStruct(((N_DEV,) + s.shape) if kind == "gather" else s.shape, s.dtype) for s in arrays]


def _comm_scratch(n):
    k_tot = (N_DEV - 1) * n
    return [pltpu.SemaphoreType.DMA((k_tot,)), pltpu.SemaphoreType.DMA((k_tot,)), pltpu.SemaphoreType.DMA((n,))]


def _comm_call(name, kind, arrays):
    n = len(arrays)

    def body(*refs):
        start, wait = _comm_plan(kind, refs[:n], refs[n:2 * n], *refs[2 * n:])
        start()
        wait()

    hbm = pl.BlockSpec(memory_space=pl.ANY)
    return pl.pallas_call(
        body, name=name, in_specs=[hbm] * n, out_specs=[hbm] * n, out_shape=_comm_out_shapes(kind, arrays),
        scratch_shapes=_comm_scratch(n), compiler_params=pltpu.CompilerParams(has_side_effects=True),
    )(*arrays)


def _adam(name, parts, w, m, v, tr, layer=None, prev=None):
    p_, r_, cw = parts.shape
    c1 = 1.0 - ADAM_B1 ** ADAM_STEP
    c2 = 1.0 - ADAM_B2 ** ADAM_STEP
    n_prev = 4 if prev else 0

    def body(p_ref, w_ref, m_ref, v_ref, *rest):
        g_ref, d_ref, nm_ref, nv_ref = rest[n_prev:]
        g = p_ref[0].astype(F32)
        for i in range(1, p_):
            g = g + p_ref[i].astype(F32)
        nm = ADAM_B1 * m_ref[...] + (1.0 - ADAM_B1) * g
        nv = ADAM_B2 * v_ref[...] + (1.0 - ADAM_B2) * (g * g)
        d_ref[...] = -ADAM_LR * ((nm / c1) / (jnp.sqrt(nv / c2) + ADAM_EPS) + ADAM_WD * w_ref[...])
        g_ref[...] = g
        nm_ref[...] = nm
        nv_ref[...] = nv

    if layer is None:
        spec = pl.BlockSpec((tr, cw), lambda i: (i, 0))
        shp = jax.ShapeDtypeStruct((r_, cw), F32)
    else:
        spec = pl.BlockSpec((None, tr, cw), lambda i: (layer, i, 0))
        shp = jax.ShapeDtypeStruct((DEPTH, r_, cw), F32)
    return pl.pallas_call(
        body, name=name, grid=(r_ // tr,),
        in_specs=[pl.BlockSpec((p_, tr, cw), lambda i: (0, i, 0)), spec, spec, spec]
                 + [pl.BlockSpec(memory_space=pl.ANY)] * n_prev,
        out_specs=[spec] * 4, out_shape=[shp] * 4,
        input_output_aliases={4 + i: i for i in range(n_prev)},
        compiler_params=_cparams(("parallel",), VMEM_LIMIT),
    )(parts, w, m, v, *(prev or ()))


def _perm_cols(w_full):
    parts = [w_full[:, _ORIG[n][0]:_ORIG[n][0] + _ORIG[n][1]] for n in _ORDER]
    parts.append(jnp.zeros((w_full.shape[0], N_PAD - N_USED), w_full.dtype))
    return jnp.concatenate(parts, axis=1)


def _unperm_cols(g):
    by_orig = sorted(_ORDER, key=lambda n: _ORIG[n][0])
    return jnp.concatenate([g[:, _MINE[n]:_MINE[n] + _ORIG[n][1]] for n in by_orig], axis=1)


def _ret_consts(length):
    lg = jnp.log(1.0 - 2.0 ** (-5.0 - jnp.arange(RET_HEADS, dtype=F32)))
    idx = jnp.arange(CHUNK, dtype=F32)
    rel = idx[:, None] - idx[None, :]
    dmat = jnp.where(rel[None] >= 0, jnp.exp(jnp.maximum(rel, 0.0)[None] * lg[:, None, None]), 0.0)
    qdec = jnp.exp((idx[:, None] + 1.0) * lg[None, :]).T[:, :, None]
    kdec = jnp.exp((CHUNK - 1.0 - idx)[:, None] * lg[None, :]).T[:, :, None]
    cdec = jnp.exp(CHUNK * lg)[:, None, None]
    half = RET_DK // 2
    inv = ROPE_BASE ** (-jnp.arange(half, dtype=F32) / half)
    ang = jnp.arange(length, dtype=jnp.int32).astype(F32)[:, None] * inv[None, :]
    cos, sin = jnp.cos(ang), jnp.sin(ang)
    return (jnp.concatenate([cos, cos], axis=1), jnp.concatenate([-sin, sin], axis=1), dmat, qdec, kdec, cdec)


def _tiles(length, pref=512):
    return pref if length % pref == 0 else length


def _layer_fwd(x, p, consts, length, last, target, comm=None):
    tm = _tiles(length)
    (h,), _ = _rowmap("pre_norm", lambda x_, w_: ((_rms(x_, w_),), ()), length, tm,
                      [(x, D_MODEL, 0)], [p["pre_norm"]], out_ws=[D_MODEL], out_dtypes=[BF16])
    proj = _matmul("in_proj", h, p["w_in"], "nn", length, N_PAD, D_MODEL)
    xc = _conv_fwd(proj, p["conv_w"], p["conv_b"], length, _tiles(length, 1024), 512)
    mixed, st_a, tinv, *got = _gdn_fwd(xc, proj, p["gdn_A_log"], p["gdn_dt_bias"], p["gdn_norm"], length, comm)
    y_b, st_b = _ssd_fwd(xc, proj, p["ssd_A_log"], p["ssd_dt_bias"], p["ssd_D"], length)
    for g in range(SSD_GROUPS):
        (mixed,), _ = _rowmap(f"ssd_gate{g}", lambda y_, z_, w_: ((_rms(y_ * _silu(z_), w_),), ()), length, tm,
                              [(y_b, 512, g), (proj, 512, _MINE["sz"] // 512 + g)], [p["ssd_norm"][g]],
                              out_ws=[512], out_dtypes=[BF16], place={0: (mixed, MIX_W, 1 + g)})
    mixed, st_c = _ret_fwd(proj, *consts, p["ret_norm"], mixed, length)
    out = _matmul("out_proj", mixed, p["w_out"], "nn", length, D_MODEL, MIX_W)
    res = dict(x=x, h=h, proj=proj, xc=xc, st_a=st_a, tinv=tinv, st_b=st_b, st_c=st_c, y_b=y_b, mixed=mixed,
               out=out)
    if not last:
        (y,), _ = _rowmap("post_norm", lambda x_, o_, w_: ((x_ + _rms(o_, w_),), ()), length, tm,
                          [(x, D_MODEL, 0), (out, D_MODEL, 0)], [p["post_norm"]], out_ws=[D_MODEL])
        return y, res, None, got

    def head(x_, o_, t_, w_):
        e = x_ + _rms(o_, w_) - t_
        row = jnp.mean(e * e, axis=-1, keepdims=True)
        loss = 0.5 * jnp.sum(row, axis=0, keepdims=True)
        return (e * (1.0 / D_MODEL),), (loss + jnp.zeros((1, LANE), F32),)

    (dy,), (loss,) = _rowmap("loss_head", head, length, tm,
                             [(x, D_MODEL, 0), (out, D_MODEL, 0), (target, D_MODEL, 0)], [p["post_norm"]],
                             out_ws=[D_MODEL], acc_ws=[LANE])
    return None, res, (dy, loss[0, 0]), got


def _layer_bwd(dy, p, res, consts, length, comm=None):
    tm = _tiles(length)
    g = {}
    (d_out,), (g["post_norm"],) = _rowmap("post_norm_bwd", lambda o_, w_: ((_rms(o_, w_),), ()), length, tm,
                                          [(res["out"], D_MODEL, 0)], [p["post_norm"]], cts=[(dy, D_MODEL, 0)])
    dmix = _matmul("out_proj_dx", d_out, p["w_out"], "nt", length, MIX_W, D_MODEL)
    g["w_out"] = _matmul("out_proj_dw", res["mixed"], d_out, "tn", MIX_W, D_MODEL, length)
    proj, xc = res["proj"], res["xc"]
    (dxc_a, dproj, dsm_a, g["gdn_A_log"], g["gdn_dt_bias"], g["gdn_norm"], *got) = _gdn_bwd(
        xc, proj, p["gdn_A_log"], p["gdn_dt_bias"], p["gdn_norm"], res["st_a"], res["tinv"], dmix, length, comm)
    dyb, dsn = None, []
    for gi in range(SSD_GROUPS):
        (dyb, dproj), (dn_g,) = _rowmap(
            f"ssd_gate_bwd{gi}", lambda y_, z_, w_: ((_rms(y_ * _silu(z_), w_),), ()), length, tm,
            [(res["y_b"], 512, gi), (proj, 512, _MINE["sz"] // 512 + gi)], [p["ssd_norm"][gi]],
            cts=[(dmix, 512, 1 + gi)], d_dtypes=[F32, BF16],
            place={0: (dyb, SSD_GROUPS * 512, gi), 1: (dproj, N_PAD, _MINE["sz"] // 512 + gi)})
        dsn.append(dn_g)
    g["ssd_norm"] = dsn
    dxc_b, dsm_b, g["ssd_A_log"], g["ssd_dt_bias"], g["ssd_D"] = _ssd_bwd(
        xc, proj, p["ssd_A_log"], p["ssd_dt_bias"], p["ssd_D"], res["st_b"], dyb, length)
    dproj, g["ret_norm"] = _ret_bwd(proj, *consts, p["ret_norm"], res["st_c"], dmix, dproj, length)
    tmc = _tiles(length, 1024)
    dpre, g["conv_w"], g["conv_b"] = _conv_bwd_pre(proj, p["conv_w"], p["conv_b"], dxc_a, dxc_b, length, tmc, 512)
    dproj = _conv_bwd_x(dpre, p["conv_w"], dproj, length, tmc, 512)
    pad = SMALL_OFF % 512 + 512 - LANE
    (dproj,), _ = _rowmap("dsmall", lambda a_, b_, c_: ((jnp.concatenate([a_ + b_ + c_, jnp.zeros((a_.shape[0], pad), F32)], axis=1),), ()),
                          length, tm, [(dsm_a, LANE, 0), (dsm_b[0], LANE, 0), (dsm_b[1], LANE, 0)], [],
                          out_ws=[512], out_dtypes=[BF16], place={0: (dproj, N_PAD, SMALL_OFF // 512)})
    dh = _matmul("in_proj_dx", dproj, p["w_in"], "nt", length, D_MODEL, N_PAD)
    g["w_in"] = _matmul("in_proj_dw", res["h"], dproj, "tn", D_MODEL, N_PAD, length)
    (dx,), (g["pre_norm"],) = _rowmap(
        "pre_norm_bwd", lambda x_, w_: ((_rms(x_, w_), x_), ()), length, tm,
        [(res["x"], D_MODEL, 0)], [p["pre_norm"]], cts=[(dh, D_MODEL, 0), (dy, D_MODEL, 0)])
    return dx, g, got


def _lane_row(vals, piece):
    lo = _MINE[piece] - SMALL_OFF
    return jnp.pad(vals[None], ((0, 0), (lo, LANE - lo - vals.shape[0])))


def _from_lane_row(row, piece, n):
    lo = _MINE[piece] - SMALL_OFF
    return row.reshape(-1, LANE).sum(axis=0)[lo:lo + n]


def _make_layer(l, w_in_full, w_out_full, conv_full, sw):
    return dict(
        pre_norm=sw["pre_norm"][l][None], post_norm=sw["post_norm"][l][None], w_in=_perm_cols(w_in_full),
        w_out=w_out_full,
        conv_w=jnp.concatenate([conv_full[0], conv_full[1]], axis=1),
        conv_b=jnp.concatenate([jnp.zeros((CONV_CH,), F32), sw["ssd_conv_b"][l]])[None],
        gdn_A_log=_lane_row(sw["gdn_A_log"][l], "ga"), gdn_dt_bias=_lane_row(sw["gdn_dt_bias"][l], "ga"),
        gdn_norm=sw["gdn_norm"][l][None],
        ssd_A_log=_lane_row(sw["ssd_A_log"][l], "sdt"), ssd_dt_bias=_lane_row(sw["ssd_dt_bias"][l], "sdt"),
        ssd_D=_lane_row(sw["ssd_D"][l], "sdt"),
        ssd_norm=[sw["ssd_norm"][l][None, 512 * g:512 * (g + 1)] for g in range(SSD_GROUPS)],
        ret_norm=sw["ret_norm"][l][None])


def _local_step(xb, tgt, layer0, make_layer1, length, fwd_comm=None, make_bwd_comm=None):
    consts = _ret_consts(length)
    y0, res0, _, got = _layer_fwd(xb, layer0, consts, length, False, tgt, fwd_comm)
    layer1 = make_layer1(got)
    _, res1, (dy, loss_local), _ = _layer_fwd(y0, layer1, consts, length, True, tgt)
    dy, grads1, _ = _layer_bwd(dy, layer1, res1, consts, length)
    dx, grads0, recv = _layer_bwd(dy, layer0, res0, consts, length, make_bwd_comm(grads1) if make_bwd_comm else None)
    return loss_local, dx, [grads0, grads1], recv


_SMALL = ["pre_norm", "post_norm", "gdn_A_log", "gdn_dt_bias", "gdn_norm", "ssd_conv_b", "ssd_A_log",
          "ssd_dt_bias", "ssd_D", "ssd_norm", "ret_norm"]


def _pack_small(arrs):
    rows = []
    for n in _SMALL:
        flat = arrs[n].reshape(-1)
        pad = (-flat.shape[0]) % LANE
        rows.append(jnp.pad(flat, (0, pad)).reshape(-1, LANE))
    out = jnp.concatenate(rows, axis=0)
    return jnp.pad(out, ((0, (-out.shape[0]) % SUBLANE), (0, 0)))


def _unpack_small(packed, like):
    out, r = {}, 0
    for n in _SMALL:
        cnt = like[n].size
        nrow = -(-cnt // LANE)
        out[n] = packed[r:r + nrow].reshape(-1)[:cnt].reshape(like[n].shape)
        r += nrow
    return out


def kernel(x, pre_norm, post_norm, w_in, gdn_conv, gdn_A_log, gdn_dt_bias, gdn_norm, ssd_conv, ssd_conv_b, ssd_A_log, ssd_dt_bias, ssd_D, ssd_norm, ret_norm, w_out, loss_target, m_pre_norm, m_post_norm, m_w_in, m_gdn_conv, m_gdn_A_log, m_gdn_dt_bias, m_gdn_norm, m_ssd_conv, m_ssd_conv_b, m_ssd_A_log, m_ssd_dt_bias, m_ssd_D, m_ssd_norm, m_ret_norm, m_w_out, v_pre_norm, v_post_norm, v_w_in, v_gdn_conv, v_gdn_A_log, v_gdn_dt_bias, v_gdn_norm, v_ssd_conv, v_ssd_conv_b, v_ssd_A_log, v_ssd_dt_bias, v_ssd_D, v_ssd_norm, v_ret_norm, v_w_out):
    length = x.shape[1]
    xb = x[0]
    tgt = loss_target[0]
    small_w = dict(pre_norm=pre_norm, post_norm=post_norm, gdn_A_log=gdn_A_log, gdn_dt_bias=gdn_dt_bias,
                   gdn_norm=gdn_norm, ssd_conv_b=ssd_conv_b, ssd_A_log=ssd_A_log, ssd_dt_bias=ssd_dt_bias,
                   ssd_D=ssd_D, ssd_norm=ssd_norm, ret_norm=ret_norm)
    small_m = dict(pre_norm=m_pre_norm, post_norm=m_post_norm, gdn_A_log=m_gdn_A_log, gdn_dt_bias=m_gdn_dt_bias,
                   gdn_norm=m_gdn_norm, ssd_conv_b=m_ssd_conv_b, ssd_A_log=m_ssd_A_log, ssd_dt_bias=m_ssd_dt_bias,
                   ssd_D=m_ssd_D, ssd_norm=m_ssd_norm, ret_norm=m_ret_norm)
    small_v = dict(pre_norm=v_pre_norm, post_norm=v_post_norm, gdn_A_log=v_gdn_A_log, gdn_dt_bias=v_gdn_dt_bias,
                   gdn_norm=v_gdn_norm, ssd_conv_b=v_ssd_conv_b, ssd_A_log=v_ssd_A_log, ssd_dt_bias=v_ssd_dt_bias,
                   ssd_D=v_ssd_D, ssd_norm=v_ssd_norm, ret_norm=v_ret_norm)

    w_in_b, w_out_b = w_in.astype(BF16), w_out.astype(BF16)
    conv_shard = jnp.stack([gdn_conv, ssd_conv], axis=1)
    shards = lambda l: [w_in_b[l], w_out_b[l], conv_shard[l]]

    def assemble(l, g_in, g_out, g_conv):
        w_in_full = g_in.transpose(1, 0, 2).reshape(D_MODEL, N_IN)
        w_out_full = g_out.reshape(MIX_W, D_MODEL)
        conv_full = g_conv.transpose(1, 2, 0, 3).reshape(2, CONV_W, CONV_CH)
        return _make_layer(l, w_in_full, w_out_full, conv_full, small_w)

    def grad_slabs(g):
        gi = _unperm_cols(g["w_in"]).reshape(D_MODEL, N_DEV, SHARD_IN).transpose(1, 0, 2).astype(BF16)
        go = g["w_out"].reshape(N_DEV, SHARD_OUT, D_MODEL).astype(BF16)
        gc = jnp.stack([g["conv_w"][:, k * CONV_CH:(k + 1) * CONV_CH].reshape(CONV_W, N_DEV, SHARD_CONV)
                        .transpose(1, 0, 2) for k in range(2)], axis=1).reshape(N_DEV, 2 * CONV_W, SHARD_CONV)
        return [gi, go, gc]

    layer0 = assemble(0, *_comm_call("gather_layer0", "gather", shards(0)))
    loss_local, dx, grads, recv1 = _local_step(
        xb, tgt, layer0, lambda got: assemble(1, *got), length,
        fwd_comm=("gather", shards(1)), make_bwd_comm=lambda g1: ("scatter", grad_slabs(g1)))
    grad_x = dx[None]
    loss = lax.psum(loss_local, ("x", "y", "c"))

    small_g = dict(
        pre_norm=jnp.concatenate([grads[l]["pre_norm"] for l in range(DEPTH)], axis=0),
        post_norm=jnp.concatenate([grads[l]["post_norm"] for l in range(DEPTH)], axis=0),
        gdn_A_log=jnp.stack([_from_lane_row(grads[l]["gdn_A_log"], "ga", GDN_HEADS) for l in range(DEPTH)]),
        gdn_dt_bias=jnp.stack([_from_lane_row(grads[l]["gdn_dt_bias"], "ga", GDN_HEADS) for l in range(DEPTH)]),
        gdn_norm=jnp.concatenate([grads[l]["gdn_norm"] for l in range(DEPTH)], axis=0),
        ssd_conv_b=jnp.concatenate([grads[l]["conv_b"][:, CONV_CH:] for l in range(DEPTH)], axis=0),
        ssd_A_log=jnp.stack([_from_lane_row(grads[l]["ssd_A_log"], "sdt", SSD_HEADS) for l in range(DEPTH)]),
        ssd_dt_bias=jnp.stack([_from_lane_row(grads[l]["ssd_dt_bias"], "sdt", SSD_HEADS) for l in range(DEPTH)]),
        ssd_D=jnp.stack([_from_lane_row(grads[l]["ssd_D"], "sdt", SSD_HEADS) for l in range(DEPTH)]),
        ssd_norm=jnp.concatenate([jnp.concatenate(grads[l]["ssd_norm"], axis=1) for l in range(DEPTH)], axis=0),
        ret_norm=jnp.concatenate([grads[l]["ret_norm"] for l in range(DEPTH)], axis=0))
    gs = _pack_small(small_g)
    gs8 = jnp.broadcast_to(gs[None], (N_DEV,) + gs.shape)
    *recv0, r_small = _comm_call("exchange_layer0", "scatter", grad_slabs(grads[0]) + [gs8])
    recv = [recv0, recv1]

    conv_w = lambda g_, s_, l: jnp.stack([g_[l], s_[l]], axis=0).reshape(2 * CONV_W, SHARD_CONV)
    o_in, o_out, o_conv = None, None, []
    for l in range(DEPTH):
        o_in = _adam(f"adam_w_in{l}", recv[l][0], w_in, m_w_in, v_w_in, 128, l, o_in)
        o_out = _adam(f"adam_w_out{l}", recv[l][1], w_out, m_w_out, v_w_out, 128, l, o_out)
        o_conv.append(_adam(f"adam_conv{l}", recv[l][2], conv_w(gdn_conv, ssd_conv, l),
                            conv_w(m_gdn_conv, m_ssd_conv, l), conv_w(v_gdn_conv, v_ssd_conv, l), 2 * CONV_W))
    ps_w, ps_m, ps_v = _pack_small(small_w), _pack_small(small_m), _pack_small(small_v)
    o_small = _adam("adam_small", r_small, ps_w, ps_m, ps_v, ps_w.shape[0])

    names = ["pre_norm", "post_norm", "w_in", "gdn_conv", "gdn_A_log", "gdn_dt_bias", "gdn_norm", "ssd_conv",
             "ssd_conv_b", "ssd_A_log", "ssd_dt_bias", "ssd_D", "ssd_norm", "ret_norm", "w_out"]
    outs = []
    for kind in range(4):
        d = _unpack_small(o_small[kind], small_w)
        cv = jnp.stack([o_conv[l][kind].reshape(2, CONV_W, SHARD_CONV) for l in range(DEPTH)], axis=1)
        d["w_in"] = o_in[kind]
        d["w_out"] = o_out[kind]
        d["gdn_conv"] = cv[0]
        d["ssd_conv"] = cv[1]
        outs.extend(d[n] for n in names)
    return (loss, grad_x, *outs)
```

```python
import functools
import math

import numpy as np
import jax
import jax.numpy as jnp
from jax import lax
from jax.experimental import pallas as pl
from jax.experimental.pallas import tpu as pltpu

F32 = jnp.float32
BF16 = jnp.bfloat16

D_MODEL = 1024
DEPTH = 2
CHUNK = 64
CONV_W = 4
EPS = 1e-6
N_DEV = 8

GDN_HEADS = 4
GDN_DK = 128
SSD_HEADS = 16
SSD_P = 64
SSD_N = 128
SSD_GROUPS = 2
SSD_PAIRS = SSD_HEADS // 2
PAIRS_PER_GROUP = SSD_PAIRS // SSD_GROUPS
RET_HEADS = 4
RET_DK = 128
ROPE_BASE = 10000.0
MIX_W = 2048
N_IN = 6680
SHARD_IN = N_IN // N_DEV
SHARD_OUT = MIX_W // N_DEV
CONV_CH = 1536
SHARD_CONV = CONV_CH // N_DEV

ADAM_LR = 0.001
ADAM_B1 = 0.9
ADAM_B2 = 0.999
ADAM_EPS = 1e-08
ADAM_WD = 0.01
ADAM_STEP = 10

LANE = 128
SUBLANE = 8
VMEM_LIMIT = 56 * 1024 * 1024

_ORIG = dict(gq=(0, 512), gk=(512, 512), gv=(1024, 512), gz=(1536, 512), gb=(2048, 4), ga=(2052, 4),
             sx=(2056, 1024), sB=(3080, 256), sC=(3336, 256), sz=(3592, 1024), sdt=(4616, 16),
             rq=(4632, 512), rk=(5144, 512), rv=(5656, 512), rg=(6168, 512))
_ORDER = ["rq", "rk", "rv", "rg", "gq", "gk", "gv", "sx", "sB", "sC", "gz", "sz", "gb", "ga", "sdt"]
_MINE = {}
_off = 0
for _n in _ORDER:
    _MINE[_n] = _off
    _off += _ORIG[_n][1]
N_USED = _off
N_PAD = 7168
CONV_ALL = 2 * CONV_CH
SMALL_OFF = _MINE["gb"]
CONV_OFF = _MINE["gq"]
XC = lambda name: _MINE[name] - CONV_OFF


def _cparams(sem, vmem=None):
    return pltpu.CompilerParams(dimension_semantics=sem, vmem_limit_bytes=vmem)


def _split_bf16(a):
    hi = a.astype(BF16)
    return hi, (a - hi.astype(F32)).astype(BF16)


def _make_mm():
    def raw(a, b, ca, cb):
        return lax.dot_general(a.astype(BF16), b.astype(BF16), (((ca,), (cb,)), ((), ())),
                               preferred_element_type=F32)

    @jax.custom_vjp
    def nn(a, b):
        return raw(a, b, 1, 0)

    @jax.custom_vjp
    def nt(a, b):
        return raw(a, b, 1, 1)

    @jax.custom_vjp
    def tn(a, b):
        return raw(a, b, 0, 0)

    nn.defvjp(lambda a, b: (raw(a, b, 1, 0), (a, b)), lambda r, g: (nt(g, r[1]), tn(r[0], g)))
    nt.defvjp(lambda a, b: (raw(a, b, 1, 1), (a, b)), lambda r, g: (nn(g, r[1]), tn(g, r[0])))
    tn.defvjp(lambda a, b: (raw(a, b, 0, 0), (a, b)), lambda r, g: (nt(r[1], g), nn(r[0], g)))
    return nn, nt, tn


_nn, _nt, _tn = _make_mm()


@jax.custom_vjp
def _swap_halves(t):
    return pltpu.roll(t, LANE // 2, 1)


_swap_halves.defvjp(lambda t: (pltpu.roll(t, LANE // 2, 1), None),
                    lambda _, g: (pltpu.roll(g, LANE // 2, 1),))


@jax.custom_vjp
def _split_rows(x):
    return tuple(x[i * CHUNK:(i + 1) * CHUNK] for i in range(x.shape[0] // CHUNK))


_split_rows.defvjp(lambda x: (tuple(x[i * CHUNK:(i + 1) * CHUNK] for i in range(x.shape[0] // CHUNK)), None),
                   lambda _, gs: (jnp.concatenate(gs, axis=0),))


def _dot3(a, b, ca, cb):
    dot = lambda x, y: lax.dot_general(x, y, (((ca,), (cb,)), ((), ())), preferred_element_type=F32)
    return dot(a[0], b[0]) + (dot(a[0], b[1]) + dot(a[1], b[0]))


@jax.custom_vjp
def _tri_inv(mats):
    return _tri_inv_impl(mats)


def _tri_inv_impl(mats):
    ii = lax.broadcasted_iota(jnp.int32, mats[0].shape, 0)
    jj = lax.broadcasted_iota(jnp.int32, mats[0].shape, 1)
    eye = jnp.where(ii == jj, 1.0, 0.0).astype(F32)
    ts = [eye - a for a in mats]
    ps = [_split_bf16(-a) for a in mats]
    for _ in range(int(math.log2(CHUNK)) - 1):
        ps = [_split_bf16(_dot3(p, p, 1, 0)) for p in ps]
        ts = [t + _dot3(_split_bf16(t), p, 1, 0) for t, p in zip(ts, ps)]
    return ts


def _tri_inv_bwd(ts, gs):
    tsp = [_split_bf16(t) for t in ts]
    xs = [_dot3(t, _split_bf16(g), 0, 0) for t, g in zip(tsp, gs)]
    return ([-_dot3(_split_bf16(x), t, 1, 1) for x, t in zip(xs, tsp)],)


_tri_inv.defvjp(lambda mats: (lambda ts: (ts, ts))(_tri_inv_impl(mats)), _tri_inv_bwd)


@jax.custom_vjp
def _tri_inv_saved(mats, ts):
    return ts


_tri_inv_saved.defvjp(lambda mats, ts: (ts, ts),
                      lambda ts, gs: (_tri_inv_bwd(ts, gs)[0], [jnp.zeros_like(t) for t in ts]))


def _silu(x):
    return x * jax.nn.sigmoid(x)


@jax.custom_vjp
def _softplus(x):
    return jnp.maximum(x, 0.0) + jnp.log1p(jnp.exp(-jnp.abs(x)))


_softplus.defvjp(lambda x: (jnp.maximum(x, 0.0) + jnp.log1p(jnp.exp(-jnp.abs(x))), x),
                 lambda x, g: (g * jax.nn.sigmoid(x),))


def _rms(x, w):
    return x * lax.rsqrt(jnp.mean(x * x, axis=-1, keepdims=True) + EPS) * w


def _chunk_masks(n):
    ii = lax.broadcasted_iota(jnp.int32, (n, n), 0)
    jj = lax.broadcasted_iota(jnp.int32, (n, n), 1)
    return ii >= jj, ii > jj, ii == jj, ii <= jj


def _cumsum_col(g, causal, eye, upper):
    g_row = jnp.sum(jnp.where(eye, g, 0.0), axis=0, keepdims=True)
    col = jnp.sum(jnp.where(causal, g_row, 0.0), axis=1, keepdims=True)
    row = jnp.sum(jnp.where(upper, g, 0.0), axis=0, keepdims=True)
    return col, row


def _lane_col(block, lane):
    pick = lax.broadcasted_iota(jnp.int32, (1, block.shape[1]), 1) == lane
    return jnp.sum(jnp.where(pick, block, 0.0), axis=1, keepdims=True)


def _gdn_chunk(q, k, v, gz, sm, alog_row, dtb_row, nw, s, t_saved=None):
    subs, nh = range(len(q)), len(q[0])
    c = q[0][0].shape[0]
    causal, _, eye, upper = _chunk_masks(c)
    stack = lambda xs: jnp.concatenate(xs, axis=0)
    per = lambda f: [[f(j, h) for h in range(nh)] for j in subs]
    qn = per(lambda j, h: q[j][h] * lax.rsqrt(jnp.sum(q[j][h] * q[j][h], axis=-1, keepdims=True) + EPS)
             * (GDN_DK ** -0.5))
    kn = per(lambda j, h: k[j][h] * lax.rsqrt(jnp.sum(k[j][h] * k[j][h], axis=-1, keepdims=True) + EPS))
    beta_blk = [jax.nn.sigmoid(sm[j]) for j in subs]
    g_blk = [-jnp.exp(alog_row) * _softplus(sm[j] + dtb_row) for j in subs]
    beta = per(lambda j, h: _lane_col(beta_blk[j], _MINE["gb"] - SMALL_OFF + h))
    g = per(lambda j, h: _lane_col(g_blk[j], _MINE["ga"] - SMALL_OFF + h))
    gcum = per(lambda j, h: _cumsum_col(g[j][h], causal, eye, upper)[0])
    eg = per(lambda j, h: jnp.exp(gcum[j][h]))
    glast = per(lambda j, h: jnp.sum(g[j][h], axis=0, keepdims=True))
    kb = per(lambda j, h: kn[j][h] * beta[j][h])
    n = nh * c
    ii = lax.broadcasted_iota(jnp.int32, (n, n), 0)
    jj = lax.broadcasted_iota(jnp.int32, (n, n), 1)
    sh = int(math.log2(c))
    same = lax.shift_right_logical(ii, sh) == lax.shift_right_logical(jj, sh)
    causal_bd = jnp.logical_and(same, ii >= jj)
    strict_bd = jnp.logical_and(same, ii > jj)
    gcum_all = [stack(gcum[j]) for j in subs]
    gcum_row = [jnp.sum(jnp.where(ii == jj, gcum_all[j], 0.0), axis=0, keepdims=True) for j in subs]
    decay = [jnp.where(causal_bd, jnp.exp(jnp.where(causal_bd, gcum_all[j] - gcum_row[j], 0.0)), 0.0) for j in subs]
    kn_all = [stack(kn[j]) for j in subs]
    a_low = [jnp.where(strict_bd, _nt(stack(kb[j]), kn_all[j]) * decay[j], 0.0) for j in subs]
    t = _tri_inv(a_low) if t_saved is None else _tri_inv_saved(a_low, t_saved)
    u = [_nn(t[j], stack([v[j][h] * beta[j][h] for h in range(nh)])) for j in subs]
    w = [_split_rows(_nn(t[j], stack([kb[j][h] * eg[j][h] for h in range(nh)]))) for j in subs]
    attn = [_nt(stack(qn[j]), kn_all[j]) * decay[j] for j in subs]
    kdec = per(lambda j, h: kn[j][h] * jnp.exp(glast[j][h] - gcum[j][h]))
    on = []
    for j in subs:
        v_new_all = u[j] - stack([_nn(w[j][h], s[h]) for h in range(nh)])
        o = _split_rows(stack([_nn(qn[j][h] * eg[j][h], s[h]) for h in range(nh)]) + _nn(attn[j], v_new_all))
        v_new = _split_rows(v_new_all)
        s = [s[h] * jnp.exp(glast[j][h]) + _tn(kdec[j][h], v_new[h]) for h in range(nh)]
        on.append([_rms(o[h], nw) * _silu(gz[j][h]) for h in range(nh)])
    return on, s, t


def _ssd_chunk(lanes, x, bm, cm, sm, alog_row, dtb_row, d_row, hs):
    subs, pairs = range(len(x)), range(len(x[0]))
    c = x[0][0].shape[0]
    lane_i = lax.broadcasted_iota(jnp.int32, (c, LANE), 1)
    lane_lo = lane_i < SSD_P
    lane_lo1 = lax.broadcasted_iota(jnp.int32, (1, LANE), 1) < SSD_P
    causal2 = lax.broadcasted_iota(jnp.int32, (c, LANE), 0) >= jnp.bitwise_and(lane_i, c - 1)
    row_lo = lax.broadcasted_iota(jnp.int32, (LANE, 1), 0) < SSD_P
    eye2 = lax.broadcasted_iota(jnp.int32, (LANE, LANE), 0) == lax.broadcasted_iota(jnp.int32, (LANE, LANE), 1)
    to_row = lambda cols: jnp.sum(jnp.where(eye2, jnp.concatenate(cols, axis=0), 0.0), axis=0, keepdims=True)
    per = lambda f: [[f(j, p) for p in pairs] for j in subs]
    both = lambda blk, p: [_lane_col(blk, lanes[p][h]) for h in range(2)]
    dt_blk = [_softplus(sm[j] + dtb_row) for j in subs]
    a_blk = [dt_blk[j] * -jnp.exp(alog_row) for j in subs]
    alast_blk = [jnp.sum(a_blk[j], axis=0, keepdims=True) for j in subs]
    dt = per(lambda j, p: both(dt_blk[j], p))
    a = per(lambda j, p: both(a_blk[j], p))
    alast = per(lambda j, p: both(alast_blk[j], p))
    dp = [both(d_row, p) for p in pairs]
    a_row = per(lambda j, p: to_row(a[j][p]))
    part = per(lambda j, p: jnp.where(causal2, a_row[j][p], 0.0))
    acum = per(lambda j, p: [jnp.sum(jnp.where(lane_lo, part[j][p], 0.0), axis=1, keepdims=True),
                             jnp.sum(jnp.where(lane_lo, 0.0, part[j][p]), axis=1, keepdims=True)])
    acum_row = per(lambda j, p: to_row(acum[j][p]))
    acum_col = per(lambda j, p: jnp.where(lane_lo, acum[j][p][0], acum[j][p][1]))
    lmat = per(lambda j, p: jnp.where(causal2, jnp.exp(jnp.where(causal2, acum_col[j][p] - acum_row[j][p], 0.0)), 0.0))
    cb = [_nt(cm[j], jnp.concatenate([bm[j], bm[j]], axis=0)) for j in subs]
    xdt = per(lambda j, p: x[j][p] * jnp.where(lane_lo, dt[j][p][0], dt[j][p][1]))
    xdt_rows = per(lambda j, p: jnp.concatenate([jnp.where(lane_lo, xdt[j][p], 0.0),
                                                 jnp.where(lane_lo, 0.0, xdt[j][p])], axis=0))
    intra = per(lambda j, p: _nn(cb[j] * lmat[j][p], xdt_rows[j][p]))
    skip = per(lambda j, p: x[j][p] * jnp.where(lane_lo1, dp[p][0], dp[p][1]))
    eacc = per(lambda j, p: jnp.exp(acum_col[j][p]))
    wdec = per(lambda j, p: jnp.exp(jnp.where(lane_lo, alast[j][p][0] - acum[j][p][0],
                                              alast[j][p][1] - acum[j][p][1])))
    scale = per(lambda j, p: jnp.exp(jnp.where(row_lo, alast[j][p][0], alast[j][p][1])))
    upd = per(lambda j, p: _tn(xdt[j][p] * wdec[j][p], bm[j]))
    y = []
    for j in subs:
        y.append([skip[j][p] + intra[j][p] + _nt(cm[j], hs[p]) * eacc[j][p] for p in pairs])
        hs = [hs[p] * scale[j][p] + upd[j][p] for p in pairs]
    return y, hs


def _ret_chunk(rq, rk, rv, rg, cos2, sin2, dmat, qdec, kdec, cdec, nw, r):
    subs, hs = range(len(rq)), range(len(rq[0]))
    per = lambda f: [[f(j, h) for h in hs] for j in subs]
    q = per(lambda j, h: rq[j][h] * cos2[j] + _swap_halves(rq[j][h]) * sin2[j])
    k = per(lambda j, h: (rk[j][h] * cos2[j] + _swap_halves(rk[j][h]) * sin2[j]) * (RET_DK ** -0.5))
    s = per(lambda j, h: _nt(q[j][h], k[j][h]) * dmat[h])
    intra = per(lambda j, h: _nn(s[j][h], rv[j][h]))
    upd = per(lambda j, h: _tn(k[j][h] * kdec[h], rv[j][h]))
    gate = per(lambda j, h: _silu(rg[j][h]))
    on = []
    for j in subs:
        on.append([_rms(intra[j][h] + _nn(q[j][h], r[h]) * qdec[h], nw) * gate[j][h] for h in hs])
        r = [r[h] * cdec[h] + upd[j][h] for h in hs]
    return on, r


MM_TILE = 1024


def _tile(n, pref=MM_TILE):
    return pref if n % pref == 0 else n


def _matmul(name, a, b, mode, m, n, k, tk=None, comm=None):
    tm, tn = _tile(m), _tile(n)
    tk = _tile(k) if tk is None else tk
    nk = k // tk
    ca, cb = {"nn": (1, 0), "nt": (1, 1), "tn": (0, 0)}[mode]
    grid = (m // tm, n // tn, nk)
    n_comm = len(comm[1]) if comm else 0

    def at_step(which):
        return functools.reduce(jnp.logical_and, [pl.program_id(d) == (0 if which == 0 else grid[d] - 1)
                                                  for d in range(3)])

    def body(*refs):
        a_ref, b_ref = refs[:2]
        c_in = refs[2:2 + n_comm]
        o_ref = refs[2 + n_comm]
        c_out = refs[3 + n_comm:3 + 2 * n_comm]
        sems = refs[3 + 2 * n_comm:]
        if comm:
            @pl.when(at_step(0))
            def _():
                _comm_plan(comm[0], c_in, c_out, *sems)[0]()

        part = lax.dot_general(a_ref[...].astype(BF16), b_ref[...].astype(BF16),
                               (((ca,), (cb,)), ((), ())), preferred_element_type=F32)
        if nk == 1:
            o_ref[...] = part
        else:
            kk = pl.program_id(2)

            @pl.when(kk == 0)
            def _():
                o_ref[...] = part

            @pl.when(kk > 0)
            def _():
                o_ref[...] += part

        if comm:
            @pl.when(at_step(1))
            def _():
                _comm_plan(comm[0], c_in, c_out, *sems)[1]()

    a_spec = pl.BlockSpec((tk, tm), lambda i, j, kk: (kk, i)) if mode == "tn" else pl.BlockSpec((tm, tk), lambda i, j, kk: (i, kk))
    b_spec = pl.BlockSpec((tn, tk), lambda i, j, kk: (j, kk)) if mode == "nt" else pl.BlockSpec((tk, tn), lambda i, j, kk: (kk, j))
    hbm = pl.BlockSpec(memory_space=pl.ANY)
    res = pl.pallas_call(
        body, name=name, grid=grid,
        in_specs=[a_spec, b_spec] + [hbm] * n_comm,
        out_specs=[pl.BlockSpec((tm, tn), lambda i, j, kk: (i, j))] + [hbm] * n_comm,
        out_shape=[jax.ShapeDtypeStruct((m, n), F32)] + (_comm_out_shapes(*comm) if comm else []),
        scratch_shapes=_comm_scratch(n_comm) if comm else [],
        compiler_params=_cparams(("arbitrary",) * 3 if comm else ("parallel", "parallel", "arbitrary"), VMEM_LIMIT),
    )(a, b, *(comm[1] if comm else []))
    return res if comm else res[0]


def _rowmap(name, fn, length, tm, rows, params, out_ws=(), acc_ws=(), out_dtypes=None, cts=None, nd_rows=(),
            d_dtypes=None, place=None):
    nt_ = length // tm
    n_r, n_nd, n_p = len(rows), len(nd_rows), len(params)
    rspec = lambda bw, cbk: pl.BlockSpec((tm, bw), lambda i: (i, cbk))
    pspec = lambda w: pl.BlockSpec((1, w), lambda i: (0, 0))
    in_arrays = [r[0] for r in rows] + [r[0] for r in nd_rows] + list(params)
    in_specs = [rspec(r[1], r[2]) for r in rows] + [rspec(r[1], r[2]) for r in nd_rows] + [pspec(p.shape[1]) for p in params]
    place = place or {}

    def load(refs):
        return [r[...].astype(F32) for r in refs]

    def placed(widths, dtypes, n_before):
        specs, shapes, extra, alias = [], [], [], {}
        for k, (w, dt) in enumerate(zip(widths, dtypes)):
            arr, total, cbk = place.get(k, (None, w, 0))
            specs.append(rspec(w, cbk))
            shapes.append(jax.ShapeDtypeStruct((length, total), dt))
            if arr is not None:
                alias[n_before + len(extra)] = k
                extra.append(arr)
        return specs, shapes, extra, alias

    if cts is None:
        n_o, n_a = len(out_ws), len(acc_ws)
        out_dtypes_ = out_dtypes or [F32] * n_o

        o_specs, o_shapes, extra, alias = placed(out_ws, out_dtypes_, n_r + n_nd + n_p)

        def body(*refs):
            ins = load(refs[:n_r + n_nd + n_p])
            outs, accs = fn(*ins)
            o_refs = refs[n_r + n_nd + n_p + len(extra):]
            for o_ref, o in zip(o_refs[:n_o], outs):
                o_ref[...] = o.astype(o_ref.dtype)
            first = pl.program_id(0) == 0
            for a_ref, acc in zip(o_refs[n_o:], accs):
                @pl.when(first)
                def _(a_ref=a_ref):
                    a_ref[...] = jnp.zeros_like(a_ref)
                a_ref[...] += acc

        res = pl.pallas_call(
            body, name=name, grid=(nt_,), in_specs=in_specs + [pl.BlockSpec(memory_space=pl.ANY)] * len(extra),
            out_specs=o_specs + [pspec(w) for w in acc_ws],
            out_shape=o_shapes + [jax.ShapeDtypeStruct((1, w), F32) for w in acc_ws],
            input_output_aliases=alias,
            compiler_params=_cparams(("arbitrary",), VMEM_LIMIT),
        )(*in_arrays, *extra)
        return res[:n_o], res[n_o:]

    n_c = len(cts)
    o_specs, o_shapes, extra, alias = placed([r[1] for r in rows], d_dtypes or [F32] * n_r, n_r + n_nd + n_p + n_c)

    def body(*refs):
        ins = load(refs[:n_r + n_nd + n_p])
        ct_vals = load(refs[n_r + n_nd + n_p:n_r + n_nd + n_p + n_c])
        nd = ins[n_r:n_r + n_nd]
        f = lambda rs, ps: fn(*rs, *nd, *ps)[0]
        _, vjp = jax.vjp(f, ins[:n_r], ins[n_r + n_nd:])
        d_rows, d_params = vjp(tuple(ct_vals))
        o_refs = refs[n_r + n_nd + n_p + n_c + len(extra):]
        for o_ref, d in zip(o_refs[:n_r], d_rows):
            o_ref[...] = d.astype(o_ref.dtype)
        first = pl.program_id(0) == 0
        for a_ref, d in zip(o_refs[n_r:], d_params):
            @pl.when(first)
            def _(a_ref=a_ref):
                a_ref[...] = jnp.zeros_like(a_ref)
            a_ref[...] += d

    res = pl.pallas_call(
        body, name=name, grid=(nt_,),
        in_specs=in_specs + [rspec(c[1], c[2]) for c in cts] + [pl.BlockSpec(memory_space=pl.ANY)] * len(extra),
        out_specs=o_specs + [pspec(p.shape[1]) for p in params],
        out_shape=o_shapes + [jax.ShapeDtypeStruct((1, p.shape[1]), F32) for p in params],
        input_output_aliases=alias,
        compiler_params=_cparams(("arbitrary",), VMEM_LIMIT),
    )(*in_arrays, *[c[0] for c in cts], *extra)
    return res[:n_r], res[n_r:]


def _conv_shift(ext, k, tm, forward):
    s = CONV_W - 1 - k
    if forward:
        rolled = ext if s == 0 else pltpu.roll(ext, s, 0)
        return rolled[SUBLANE:, :]
    rolled = ext if s == 0 else pltpu.roll(ext, tm + SUBLANE - s, 0)
    return rolled[:tm, :]


def _conv_pre(x_ref, halo_ref, w_ref, b_ref, first, tm):
    halo = jnp.where(first, 0.0, halo_ref[...])
    ext = jnp.concatenate([halo, x_ref[...]], axis=0)
    w = w_ref[...]
    pre = b_ref[...] + jnp.zeros_like(x_ref[...])
    taps = []
    for k in range(CONV_W):
        tap = _conv_shift(ext, k, tm, True)
        taps.append(tap)
        pre = pre + tap * w[k:k + 1, :]
    return pre, taps


def _conv_fwd(proj, w, b, length, tm, tc):
    hb = tm // SUBLANE
    j0 = CONV_OFF // tc

    def body(x_ref, halo_ref, w_ref, b_ref, o_ref):
        pre, _ = _conv_pre(x_ref, halo_ref, w_ref, b_ref, pl.program_id(0) == 0, tm)
        o_ref[...] = _silu(pre)

    return pl.pallas_call(
        body, name="conv_fwd", grid=(length // tm, CONV_ALL // tc),
        in_specs=[pl.BlockSpec((tm, tc), lambda i, j: (i, j0 + j)),
                  pl.BlockSpec((SUBLANE, tc), lambda i, j: (jnp.maximum(i * hb - 1, 0), j0 + j)),
                  pl.BlockSpec((CONV_W, tc), lambda i, j: (0, j)),
                  pl.BlockSpec((1, tc), lambda i, j: (0, j))],
        out_specs=pl.BlockSpec((tm, tc), lambda i, j: (i, j)),
        out_shape=jax.ShapeDtypeStruct((length, CONV_ALL), F32),
        compiler_params=_cparams(("parallel", "parallel"), VMEM_LIMIT),
    )(proj, proj, w, b)


def _conv_bwd_pre(proj, w, b, dxc_a, dxc_b, length, tm, tc):
    hb = tm // SUBLANE
    j0 = CONV_OFF // tc
    jh = CONV_CH // tc

    def body(x_ref, halo_ref, w_ref, b_ref, dya_ref, dyb_ref, dpre_ref, dw_ref, db_ref):
        i = pl.program_id(1)
        pre, taps = _conv_pre(x_ref, halo_ref, w_ref, b_ref, i == 0, tm)
        sg = jax.nn.sigmoid(pre)
        dy = jnp.where(pl.program_id(0) < jh, dya_ref[...], dyb_ref[...])
        dpre = dy * (sg * (1.0 + pre * (1.0 - sg)))
        dpre_ref[...] = dpre

        @pl.when(i == 0)
        def _():
            dw_ref[...] = jnp.zeros_like(dw_ref)
            db_ref[...] = jnp.zeros_like(db_ref)

        for k in range(CONV_W):
            dw_ref[k:k + 1, :] += jnp.sum(dpre * taps[k], axis=0, keepdims=True)
        db_ref[...] += jnp.sum(dpre, axis=0, keepdims=True)

    return pl.pallas_call(
        body, name="conv_bwd_pre", grid=(CONV_ALL // tc, length // tm),
        in_specs=[pl.BlockSpec((tm, tc), lambda j, i: (i, j0 + j)),
                  pl.BlockSpec((SUBLANE, tc), lambda j, i: (jnp.maximum(i * hb - 1, 0), j0 + j)),
                  pl.BlockSpec((CONV_W, tc), lambda j, i: (0, j)),
                  pl.BlockSpec((1, tc), lambda j, i: (0, j)),
                  pl.BlockSpec((tm, tc), lambda j, i: (jnp.where(j < jh, i, 0), jnp.minimum(j, jh - 1))),
                  pl.BlockSpec((tm, tc), lambda j, i: (jnp.where(j < jh, 0, i), jnp.maximum(j - jh, 0)))],
        out_specs=[pl.BlockSpec((tm, tc), lambda j, i: (i, j)),
                   pl.BlockSpec((CONV_W, tc), lambda j, i: (0, j)),
                   pl.BlockSpec((1, tc), lambda j, i: (0, j))],
        out_shape=[jax.ShapeDtypeStruct((length, CONV_ALL), F32),
                   jax.ShapeDtypeStruct((CONV_W, CONV_ALL), F32),
                   jax.ShapeDtypeStruct((1, CONV_ALL), F32)],
        compiler_params=_cparams(("parallel", "arbitrary"), VMEM_LIMIT),
    )(proj, proj, w, b, dxc_a, dxc_b)


def _conv_bwd_x(dpre, w, dproj, length, tm, tc):
    hb = tm // SUBLANE
    n_t = length // tm
    last_blk = length // SUBLANE - 1
    j0 = CONV_OFF // tc

    def body(d_ref, halo_ref, w_ref, _, o_ref):
        halo = jnp.where(pl.program_id(0) == n_t - 1, 0.0, halo_ref[...])
        ext = jnp.concatenate([d_ref[...], halo], axis=0)
        w = w_ref[...]
        acc = jnp.zeros_like(d_ref[...])
        for k in range(CONV_W):
            acc = acc + _conv_shift(ext, k, tm, False) * w[k:k + 1, :]
        o_ref[...] = acc.astype(o_ref.dtype)

    return pl.pallas_call(
        body, name="conv_bwd_x", grid=(n_t, CONV_ALL // tc),
        in_specs=[pl.BlockSpec((tm, tc), lambda i, j: (i, j)),
                  pl.BlockSpec((SUBLANE, tc), lambda i, j: (jnp.minimum((i + 1) * hb, last_blk), j)),
                  pl.BlockSpec((CONV_W, tc), lambda i, j: (0, j)),
                  pl.BlockSpec(memory_space=pl.ANY)],
        out_specs=pl.BlockSpec((tm, tc), lambda i, j: (i, j0 + j)),
        out_shape=jax.ShapeDtypeStruct(dproj.shape, dproj.dtype), input_output_aliases={3: 0},
        compiler_params=_cparams(("parallel", "parallel"), VMEM_LIMIT),
    )(dpre, dpre, w, dproj)


GDN_CHUNKS_PER_STEP = 4


def _gdn_steps(length):
    nc = length // CHUNK
    gs = GDN_CHUNKS_PER_STEP if nc % GDN_CHUNKS_PER_STEP == 0 else 1
    return gs, nc // gs, gs * CHUNK


def _gdn_tiles(ref, gs):
    return [[ref[j * CHUNK:(j + 1) * CHUNK, h * LANE:(h + 1) * LANE] for h in range(GDN_HEADS)] for j in range(gs)]


def _chunk_rows(ref, gs):
    return [ref[j * CHUNK:(j + 1) * CHUNK, :] for j in range(gs)]


def _gdn_fwd(xc, proj, alog, dtb, nw, length, comm=None):
    gs, steps, rows = _gdn_steps(length)
    h_ = GDN_HEADS
    hw = h_ * LANE
    tn_ = h_ * CHUNK
    n_comm = len(comm[1]) if comm else 0

    def body(*refs):
        q_ref, k_ref, v_ref, gz_ref, sm_ref, al_ref, dt_ref, nw_ref = refs[:8]
        c_in = refs[8:8 + n_comm]
        on_ref, st_ref, t_ref = refs[8 + n_comm:11 + n_comm]
        c_out = refs[11 + n_comm:11 + 2 * n_comm]
        s_ref = refs[11 + 2 * n_comm]
        sems = refs[12 + 2 * n_comm:]

        @pl.when(pl.program_id(0) == 0)
        def _():
            s_ref[...] = jnp.zeros_like(s_ref)
            if comm:
                _comm_plan(comm[0], c_in, c_out, *sems)[0]()

        st_ref[...] = s_ref[...]
        lead = lambda ref: [ref[h] for h in range(h_)]
        on, s_new, t = _gdn_chunk(_gdn_tiles(q_ref, gs), _gdn_tiles(k_ref, gs), _gdn_tiles(v_ref, gs),
                                  _gdn_tiles(gz_ref, gs), _chunk_rows(sm_ref, gs), al_ref[...], dt_ref[...],
                                  nw_ref[...], lead(s_ref))
        for j in range(gs):
            t_ref[j] = t[j]
        for h in range(h_):
            for j in range(gs):
                on_ref[j * CHUNK:(j + 1) * CHUNK, h * LANE:(h + 1) * LANE] = on[j][h].astype(on_ref.dtype)
            s_ref[h] = s_new[h]

        if comm:
            @pl.when(pl.program_id(0) == steps - 1)
            def _():
                _comm_plan(comm[0], c_in, c_out, *sems)[1]()

    blk = lambda col: pl.BlockSpec((rows, hw), lambda c: (c, col // hw))
    one = pl.BlockSpec((1, LANE), lambda c: (0, 0))
    hbm = pl.BlockSpec(memory_space=pl.ANY)
    return pl.pallas_call(
        body, name="gdn_fwd_" + comm[0] if comm else "gdn_fwd", grid=(steps,),
        in_specs=[blk(XC("gq")), blk(XC("gk")), blk(XC("gv")), blk(_MINE["gz"]),
                  pl.BlockSpec((rows, LANE), lambda c: (c, SMALL_OFF // LANE)), one, one,
                  pl.BlockSpec((1, LANE), lambda c: (0, 0))] + [hbm] * n_comm,
        out_specs=[pl.BlockSpec((rows, hw), lambda c: (c, 0)),
                   pl.BlockSpec((h_, None, LANE, LANE), lambda c: (0, c, 0, 0)),
                   pl.BlockSpec((gs, tn_, tn_), lambda c: (c, 0, 0))] + [hbm] * n_comm,
        out_shape=[jax.ShapeDtypeStruct((length, MIX_W), BF16),
                   jax.ShapeDtypeStruct((h_, steps, LANE, LANE), F32),
                   jax.ShapeDtypeStruct((length // CHUNK, tn_, tn_), F32)]
                  + (_comm_out_shapes(*comm) if comm else []),
        scratch_shapes=[pltpu.VMEM((h_, LANE, LANE), F32)] + (_comm_scratch(n_comm) if comm else []),
        compiler_params=_cparams(("arbitrary",), VMEM_LIMIT),
    )(xc, xc, xc, proj, proj, alog, dtb, nw, *(comm[1] if comm else []))


def _gdn_bwd(xc, proj, alog, dtb, nw, st, tinv, dmix, length, comm=None):
    gs, steps, rows = _gdn_steps(length)
    h_ = GDN_HEADS
    hw = h_ * LANE
    tn_ = h_ * CHUNK
    n_comm = len(comm[1]) if comm else 0
    n_in, n_out = 11, 6

    def body(*refs):
        q_ref, k_ref, v_ref, gz_ref, sm_ref, al_ref, dt_ref, nw_ref, st_ref, t_ref, do_ref = refs[:n_in]
        c_in = refs[n_in:n_in + n_comm]
        o0 = n_in + n_comm
        dqkv_ref, dgz_ref, dsm_ref, dal_ref, ddt_ref, dnw_ref = refs[o0:o0 + n_out]
        c_out = refs[o0 + n_out:o0 + n_out + n_comm]
        ds_ref = refs[o0 + n_out + n_comm]
        sems = refs[o0 + n_out + n_comm + 1:]

        @pl.when(pl.program_id(0) == 0)
        def _():
            ds_ref[...] = jnp.zeros_like(ds_ref)
            dal_ref[...] = jnp.zeros_like(dal_ref)
            ddt_ref[...] = jnp.zeros_like(ddt_ref)
            dnw_ref[...] = jnp.zeros_like(dnw_ref)
            if comm:
                _comm_plan(comm[0], c_in, c_out, *sems)[0]()

        lead = lambda ref: [ref[h] for h in range(h_)]
        ins = (_gdn_tiles(q_ref, gs), _gdn_tiles(k_ref, gs), _gdn_tiles(v_ref, gs), _gdn_tiles(gz_ref, gs),
               _chunk_rows(sm_ref, gs), al_ref[...], dt_ref[...], nw_ref[...], lead(st_ref))
        t_saved = [t_ref[j] for j in range(gs)]
        _, vjp = jax.vjp(lambda *a: _gdn_chunk(*a, t_saved=t_saved)[:2], *ins)
        dq, dk, dv, dgz, dsm, dal, ddt, dnw, ds = vjp((_gdn_tiles(do_ref, gs), lead(ds_ref)))
        for j in range(gs):
            rws = slice(j * CHUNK, (j + 1) * CHUNK)
            for h in range(h_):
                cols = slice(h * LANE, (h + 1) * LANE)
                for part, d in enumerate((dq, dk, dv)):
                    dqkv_ref[rws, part * hw + h * LANE:part * hw + (h + 1) * LANE] = d[j][h]
                dgz_ref[rws, cols] = dgz[j][h].astype(dgz_ref.dtype)
            dsm_ref[rws, :] = dsm[j]
        for h in range(h_):
            ds_ref[h] = ds[h]
        dal_ref[...] += dal
        ddt_ref[...] += ddt
        dnw_ref[...] += dnw

        if comm:
            @pl.when(pl.program_id(0) == steps - 1)
            def _():
                _comm_plan(comm[0], c_in, c_out, *sems)[1]()

    r = lambda c: steps - 1 - c
    blk = lambda col: pl.BlockSpec((rows, hw), lambda c: (r(c), col // hw))
    one = pl.BlockSpec((1, LANE), lambda c: (0, 0))
    hbm = pl.BlockSpec(memory_space=pl.ANY)
    return pl.pallas_call(
        body, name="gdn_bwd_" + comm[0] if comm else "gdn_bwd", grid=(steps,),
        in_specs=[blk(XC("gq")), blk(XC("gk")), blk(XC("gv")), blk(_MINE["gz"]),
                  pl.BlockSpec((rows, LANE), lambda c: (r(c), SMALL_OFF // LANE)), one, one,
                  pl.BlockSpec((1, LANE), lambda c: (0, 0)),
                  pl.BlockSpec((h_, None, LANE, LANE), lambda c: (0, r(c), 0, 0)),
                  pl.BlockSpec((gs, tn_, tn_), lambda c: (r(c), 0, 0)),
                  blk(0)] + [hbm] * n_comm,
        out_specs=[pl.BlockSpec((rows, 3 * hw), lambda c: (r(c), 0)), blk(_MINE["gz"]),
                   pl.BlockSpec((rows, LANE), lambda c: (r(c), 0)), one, one,
                   pl.BlockSpec((1, LANE), lambda c: (0, 0))] + [hbm] * n_comm,
        out_shape=[jax.ShapeDtypeStruct((length, 3 * hw), F32), jax.ShapeDtypeStruct((length, N_PAD), BF16),
                   jax.ShapeDtypeStruct((length, LANE), F32),
                   jax.ShapeDtypeStruct((1, LANE), F32), jax.ShapeDtypeStruct((1, LANE), F32),
                   jax.ShapeDtypeStruct((1, LANE), F32)] + (_comm_out_shapes(*comm) if comm else []),
        scratch_shapes=[pltpu.VMEM((h_, LANE, LANE), F32)] + (_comm_scratch(n_comm) if comm else []),
        compiler_params=_cparams(("arbitrary",), VMEM_LIMIT),
    )(xc, xc, xc, proj, proj, alog, dtb, nw, st, tinv, dmix, *(comm[1] if comm else []))


SSD_CHUNKS_PER_STEP = 4


def _ssd_steps(length):
    nc = length // CHUNK
    gs = SSD_CHUNKS_PER_STEP if nc % SSD_CHUNKS_PER_STEP == 0 else 1
    return gs, nc // gs, gs * CHUNK


def _ssd_lanes(g):
    base = _MINE["sdt"] - SMALL_OFF
    return [[base + (g * PAIRS_PER_GROUP + p) * 2 + j for j in range(2)] for p in range(PAIRS_PER_GROUP)]


def _ssd_tiles(ref, gs):
    return [[ref[j * CHUNK:(j + 1) * CHUNK, p * LANE:(p + 1) * LANE] for p in range(PAIRS_PER_GROUP)]
            for j in range(gs)]


def _ssd_fwd(xc, proj, alog_row, dtb_row, d_row, length):
    gs, steps, rows = _ssd_steps(length)
    ppg = PAIRS_PER_GROUP
    gw = ppg * LANE

    def body(x_ref, b_ref, c_ref, sm_ref, al_ref, db_ref, dp_ref, y_ref, st_ref, hs_ref):
        @pl.when(pl.program_id(1) == 0)
        def _():
            hs_ref[...] = jnp.zeros_like(hs_ref)

        st_ref[...] = hs_ref[...]
        y, hs_new = _ssd_chunk(_ssd_lanes(pl.program_id(0)), _ssd_tiles(x_ref, gs), _chunk_rows(b_ref, gs),
                               _chunk_rows(c_ref, gs), _chunk_rows(sm_ref, gs), al_ref[...], db_ref[...],
                               dp_ref[...], [hs_ref[p] for p in range(ppg)])
        for p in range(ppg):
            for j in range(gs):
                y_ref[j * CHUNK:(j + 1) * CHUNK, p * LANE:(p + 1) * LANE] = y[j][p]
            hs_ref[p] = hs_new[p]

    one = pl.BlockSpec((1, LANE), lambda g, c: (0, 0))
    return pl.pallas_call(
        body, name="ssd_fwd", grid=(SSD_GROUPS, steps),
        in_specs=[pl.BlockSpec((rows, gw), lambda g, c: (c, XC("sx") // gw + g)),
                  pl.BlockSpec((rows, LANE), lambda g, c: (c, XC("sB") // LANE + g)),
                  pl.BlockSpec((rows, LANE), lambda g, c: (c, XC("sC") // LANE + g)),
                  pl.BlockSpec((rows, LANE), lambda g, c: (c, SMALL_OFF // LANE)),
                  one, one, one],
        out_specs=[pl.BlockSpec((rows, gw), lambda g, c: (c, g)),
                   pl.BlockSpec((None, ppg, None, LANE, SSD_N), lambda g, c: (g, 0, c, 0, 0))],
        out_shape=[jax.ShapeDtypeStruct((length, SSD_GROUPS * gw), F32),
                   jax.ShapeDtypeStruct((SSD_GROUPS, ppg, steps, LANE, SSD_N), F32)],
        scratch_shapes=[pltpu.VMEM((ppg, LANE, SSD_N), F32)],
        compiler_params=_cparams(("parallel", "arbitrary"), VMEM_LIMIT),
    )(xc, xc, xc, proj, alog_row, dtb_row, d_row)


def _ssd_bwd(xc, proj, alog_row, dtb_row, d_row, st, dy, length):
    gs, steps, rows = _ssd_steps(length)
    ppg = PAIRS_PER_GROUP
    gw = ppg * LANE

    def body(x_ref, b_ref, c_ref, sm_ref, al_ref, db_ref, dp_ref, st_ref, dy_ref,
             dxc_ref, dsm_ref, dal_ref, ddb_ref, ddp_ref, dhs_ref):
        g = pl.program_id(1)

        @pl.when(jnp.logical_and(pl.program_id(0) == 0, g == 0))
        def _():
            dhs_ref[...] = jnp.zeros_like(dhs_ref)
            dal_ref[...] = jnp.zeros_like(dal_ref)
            ddb_ref[...] = jnp.zeros_like(ddb_ref)
            ddp_ref[...] = jnp.zeros_like(ddp_ref)

        pairs = range(ppg)
        ins = (_ssd_tiles(x_ref, gs), _chunk_rows(b_ref, gs), _chunk_rows(c_ref, gs), _chunk_rows(sm_ref, gs),
               al_ref[...], db_ref[...], dp_ref[...], [st_ref[p] for p in pairs])
        _, vjp = jax.vjp(functools.partial(_ssd_chunk, _ssd_lanes(g)), *ins)
        dx, dbm, dcm, dsm, dal, ddb, ddp, dhs = vjp((_ssd_tiles(dy_ref, gs), [dhs_ref[g, p] for p in pairs]))
        for grp in range(SSD_GROUPS):
            @pl.when(g == grp)
            def _(grp=grp):
                for j in range(gs):
                    rws = slice(j * CHUNK, (j + 1) * CHUNK)
                    for p in pairs:
                        col = grp * gw + p * LANE
                        dxc_ref[rws, col:col + LANE] = dx[j][p]
                    cb_ = XC("sB") - XC("sx") + grp * SSD_N
                    cc_ = XC("sC") - XC("sx") + grp * SSD_N
                    dxc_ref[rws, cb_:cb_ + SSD_N] = dbm[j]
                    dxc_ref[rws, cc_:cc_ + SSD_N] = dcm[j]
        for j in range(gs):
            dsm_ref[j * CHUNK:(j + 1) * CHUNK, :] = dsm[j]
        for p in pairs:
            dhs_ref[g, p] = dhs[p]
        dal_ref[g] += dal
        ddb_ref[g] += ddb
        ddp_ref[g] += ddp

    r = lambda c: steps - 1 - c
    one = pl.BlockSpec((1, LANE), lambda c, g: (0, 0))
    acc = pl.BlockSpec((SSD_GROUPS, 1, LANE), lambda c, g: (0, 0, 0))
    prm = jax.ShapeDtypeStruct((SSD_GROUPS, 1, LANE), F32)
    return pl.pallas_call(
        body, name="ssd_bwd", grid=(steps, SSD_GROUPS),
        in_specs=[pl.BlockSpec((rows, gw), lambda c, g: (r(c), XC("sx") // gw + g)),
                  pl.BlockSpec((rows, LANE), lambda c, g: (r(c), XC("sB") // LANE + g)),
                  pl.BlockSpec((rows, LANE), lambda c, g: (r(c), XC("sC") // LANE + g)),
                  pl.BlockSpec((rows, LANE), lambda c, g: (r(c), SMALL_OFF // LANE)),
                  one, one, one,
                  pl.BlockSpec((None, ppg, None, LANE, SSD_N), lambda c, g: (g, 0, r(c), 0, 0)),
                  pl.BlockSpec((rows, gw), lambda c, g: (r(c), g))],
        out_specs=[pl.BlockSpec((rows, CONV_CH), lambda c, g: (r(c), 0)),
                   pl.BlockSpec((None, rows, LANE), lambda c, g: (g, r(c), 0)), acc, acc, acc],
        out_shape=[jax.ShapeDtypeStruct((length, CONV_CH), F32),
                   jax.ShapeDtypeStruct((SSD_GROUPS, length, LANE), F32), prm, prm, prm],
        scratch_shapes=[pltpu.VMEM((SSD_GROUPS, ppg, LANE, SSD_N), F32)],
        compiler_params=_cparams(("arbitrary", "arbitrary"), VMEM_LIMIT),
    )(xc, xc, xc, proj, alog_row, dtb_row, d_row, st, dy)


def _ret_fwd(proj, cos2, sin2, dmat, qdec, kdec, cdec, nw, mixed, length):
    gs, steps, rows = _gdn_steps(length)
    h_ = RET_HEADS

    def body(q_ref, k_ref, v_ref, g_ref, cos_ref, sin_ref, dm_ref, qd_ref, kd_ref, cd_ref, nw_ref, _,
             on_ref, st_ref, r_ref):
        @pl.when(pl.program_id(0) == 0)
        def _():
            r_ref[...] = jnp.zeros_like(r_ref)

        st_ref[...] = r_ref[...]
        lead = lambda ref: [ref[h] for h in range(h_)]
        on, r_new = _ret_chunk(_gdn_tiles(q_ref, gs), _gdn_tiles(k_ref, gs), _gdn_tiles(v_ref, gs),
                               _gdn_tiles(g_ref, gs), _chunk_rows(cos_ref, gs), _chunk_rows(sin_ref, gs),
                               lead(dm_ref), lead(qd_ref), lead(kd_ref), lead(cd_ref), nw_ref[...], lead(r_ref))
        for h in range(h_):
            for j in range(gs):
                on_ref[j * CHUNK:(j + 1) * CHUNK, h * LANE:(h + 1) * LANE] = on[j][h].astype(on_ref.dtype)
            r_ref[h] = r_new[h]

    hw = h_ * LANE
    blk = lambda name: pl.BlockSpec((rows, hw), lambda c: (c, _MINE[name] // hw))
    tab = pl.BlockSpec((rows, LANE), lambda c: (c, 0))
    full = lambda *s: pl.BlockSpec(s, lambda c: (0,) * len(s))
    return pl.pallas_call(
        body, name="ret_fwd", grid=(steps,),
        in_specs=[blk("rq"), blk("rk"), blk("rv"), blk("rg"), tab, tab,
                  full(h_, CHUNK, CHUNK), full(h_, CHUNK, 1), full(h_, CHUNK, 1), full(h_, 1, 1), full(1, LANE),
                  pl.BlockSpec(memory_space=pl.ANY)],
        out_specs=[pl.BlockSpec((rows, hw), lambda c: (c, (MIX_W - hw) // hw)),
                   pl.BlockSpec((h_, None, LANE, LANE), lambda c: (0, c, 0, 0))],
        out_shape=[jax.ShapeDtypeStruct(mixed.shape, mixed.dtype),
                   jax.ShapeDtypeStruct((h_, steps, LANE, LANE), F32)],
        input_output_aliases={11: 0},
        scratch_shapes=[pltpu.VMEM((h_, LANE, LANE), F32)],
        compiler_params=_cparams(("arbitrary",), VMEM_LIMIT),
    )(proj, proj, proj, proj, cos2, sin2, dmat, qdec, kdec, cdec, nw, mixed)


def _ret_bwd(proj, cos2, sin2, dmat, qdec, kdec, cdec, nw, st, dmix, dproj, length):
    gs, steps, rows = _gdn_steps(length)
    h_ = RET_HEADS
    hw = h_ * LANE

    def body(q_ref, k_ref, v_ref, g_ref, cos_ref, sin_ref, dm_ref, qd_ref, kd_ref, cd_ref, nw_ref, st_ref,
             do_ref, _, dqkvg_ref, dnw_ref, dr_ref):
        @pl.when(pl.program_id(0) == 0)
        def _():
            dr_ref[...] = jnp.zeros_like(dr_ref)
            dnw_ref[...] = jnp.zeros_like(dnw_ref)

        lead = lambda ref: [ref[h] for h in range(h_)]
        consts = (_chunk_rows(cos_ref, gs), _chunk_rows(sin_ref, gs), lead(dm_ref), lead(qd_ref), lead(kd_ref),
                  lead(cd_ref))
        f = lambda q, k, v, g, w_, r_: _ret_chunk(q, k, v, g, *consts, w_, r_)
        _, vjp = jax.vjp(f, _gdn_tiles(q_ref, gs), _gdn_tiles(k_ref, gs), _gdn_tiles(v_ref, gs),
                         _gdn_tiles(g_ref, gs), nw_ref[...], lead(st_ref))
        dq, dk, dv, dg, dnw, dr = vjp((_gdn_tiles(do_ref, gs), lead(dr_ref)))
        for h in range(h_):
            for j in range(gs):
                rws = slice(j * CHUNK, (j + 1) * CHUNK)
                for part, d in enumerate((dq, dk, dv, dg)):
                    col = part * hw + h * LANE
                    dqkvg_ref[rws, col:col + LANE] = d[j][h].astype(dqkvg_ref.dtype)
            dr_ref[h] = dr[h]
        dnw_ref[...] += dnw

    r = lambda c: steps - 1 - c
    blk = lambda name: pl.BlockSpec((rows, hw), lambda c: (r(c), _MINE[name] // hw))
    tab = pl.BlockSpec((rows, LANE), lambda c: (r(c), 0))
    full = lambda *s: pl.BlockSpec(s, lambda c: (0,) * len(s))
    assert _MINE["rq"] % (4 * hw) == 0 and [_MINE[n] - _MINE["rq"] for n in ("rk", "rv", "rg")] == [hw, 2 * hw, 3 * hw]
    return pl.pallas_call(
        body, name="ret_bwd", grid=(steps,),
        in_specs=[blk("rq"), blk("rk"), blk("rv"), blk("rg"), tab, tab,
                  full(h_, CHUNK, CHUNK), full(h_, CHUNK, 1), full(h_, CHUNK, 1), full(h_, 1, 1), full(1, LANE),
                  pl.BlockSpec((h_, None, LANE, LANE), lambda c: (0, r(c), 0, 0)),
                  pl.BlockSpec((rows, hw), lambda c: (r(c), (MIX_W - hw) // hw)),
                  pl.BlockSpec(memory_space=pl.ANY)],
        out_specs=[pl.BlockSpec((rows, 4 * hw), lambda c: (r(c), _MINE["rq"] // (4 * hw))), full(1, LANE)],
        out_shape=[jax.ShapeDtypeStruct(dproj.shape, dproj.dtype), jax.ShapeDtypeStruct((1, LANE), F32)],
        input_output_aliases={13: 0},
        scratch_shapes=[pltpu.VMEM((h_, LANE, LANE), F32)],
        compiler_params=_cparams(("arbitrary",), VMEM_LIMIT),
    )(proj, proj, proj, proj, cos2, sin2, dmat, qdec, kdec, cdec, nw, st, dmix, dproj)


def _mesh_pos():
    x, y, c = lax.axis_index("x"), lax.axis_index("y"), lax.axis_index("c")
    return x, y, c, 4 * x + 2 * y + c


def _peer(x, y, c, mask):
    return (x ^ ((mask >> 2) & 1), y ^ ((mask >> 1) & 1), c ^ (mask & 1))


def _comm_plan(kind, ins, outs, send_sems, recv_sems, local_sems):
    x, y, c, me = _mesh_pos()
    n = len(ins)
    src = (lambda a, idx: ins[a]) if kind == "gather" else (lambda a, idx: ins[a].at[idx])
    local = [pltpu.make_async_copy(src(a, me), outs[a].at[me], local_sems.at[a]) for a in range(n)]
    sends, recvs = [], []
    for mask in range(1, N_DEV):
        px, py, pc = _peer(x, y, c, mask)
        pidx = 4 * px + 2 * py + pc
        for a in range(n):
            k = (mask - 1) * n + a
            copy = lambda s, d, k=k: pltpu.make_async_remote_copy(
                src_ref=s, dst_ref=d, send_sem=send_sems.at[k], recv_sem=recv_sems.at[k],
                device_id=(px, py, pc), device_id_type=pl.DeviceIdType.MESH)
            sends.append(copy(src(a, pidx), outs[a].at[me]))
            recvs.append(copy(src(a, me), outs[a].at[pidx]))

    def start():
        for cp in local + sends:
            cp.start()

    def wait():
        for cp in sends:
            cp.wait_send()
        for cp in recvs:
            cp.wait_recv()
        for cp in local:
            cp.wait()

    return start, wait


def _comm_out_shapes(kind, arrays):
    return [jax.ShapeDtypeStruct(((N_DEV,) + s.shape) if kind == "gather" else s.shape, s.dtype) for s in arrays]


def _comm_scratch(n):
    k_tot = (N_DEV - 1) * n
    return [pltpu.SemaphoreType.DMA((k_tot,)), pltpu.SemaphoreType.DMA((k_tot,)), pltpu.SemaphoreType.DMA((n,))]


def _comm_call(name, kind, arrays):
    n = len(arrays)

    def body(*refs):
        start, wait = _comm_plan(kind, refs[:n], refs[n:2 * n], *refs[2 * n:])
        start()
        wait()

    hbm = pl.BlockSpec(memory_space=pl.ANY)
    return pl.pallas_call(
        body, name=name, in_specs=[hbm] * n, out_specs=[hbm] * n, out_shape=_comm_out_shapes(kind, arrays),
        scratch_shapes=_comm_scratch(n), compiler_params=pltpu.CompilerParams(has_side_effects=True),
    )(*arrays)


def _adam(name, parts, w, m, v, tr, layer=None, prev=None):
    p_, r_, cw = parts.shape
    c1 = 1.0 - ADAM_B1 ** ADAM_STEP
    c2 = 1.0 - ADAM_B2 ** ADAM_STEP
    n_prev = 4 if prev else 0

    def body(p_ref, w_ref, m_ref, v_ref, *rest):
        g_ref, d_ref, nm_ref, nv_ref = rest[n_prev:]
        g = p_ref[0].astype(F32)
        for i in range(1, p_):
            g = g + p_ref[i].astype(F32)
        nm = ADAM_B1 * m_ref[...] + (1.0 - ADAM_B1) * g
        nv = ADAM_B2 * v_ref[...] + (1.0 - ADAM_B2) * (g * g)
        d_ref[...] = -ADAM_LR * ((nm / c1) / (jnp.sqrt(nv / c2) + ADAM_EPS) + ADAM_WD * w_ref[...])
        g_ref[...] = g
        nm_ref[...] = nm
        nv_ref[...] = nv

    if layer is None:
        spec = pl.BlockSpec((tr, cw), lambda i: (i, 0))
        shp = jax.ShapeDtypeStruct((r_, cw), F32)
    else:
        spec = pl.BlockSpec((None, tr, cw), lambda i: (layer, i, 0))
        shp = jax.ShapeDtypeStruct((DEPTH, r_, cw), F32)
    return pl.pallas_call(
        body, name=name, grid=(r_ // tr,),
        in_specs=[pl.BlockSpec((p_, tr, cw), lambda i: (0, i, 0)), spec, spec, spec]
                 + [pl.BlockSpec(memory_space=pl.ANY)] * n_prev,
        out_specs=[spec] * 4, out_shape=[shp] * 4,
        input_output_aliases={4 + i: i for i in range(n_prev)},
        compiler_params=_cparams(("parallel",), VMEM_LIMIT),
    )(parts, w, m, v, *(prev or ()))


def _perm_cols(w_full):
    parts = [w_full[:, _ORIG[n][0]:_ORIG[n][0] + _ORIG[n][1]] for n in _ORDER]
    parts.append(jnp.zeros((w_full.shape[0], N_PAD - N_USED), w_full.dtype))
    return jnp.concatenate(parts, axis=1)


def _unperm_cols(g):
    by_orig = sorted(_ORDER, key=lambda n: _ORIG[n][0])
    return jnp.concatenate([g[:, _MINE[n]:_MINE[n] + _ORIG[n][1]] for n in by_orig], axis=1)


def _ret_consts(length):
    lg = jnp.log(1.0 - 2.0 ** (-5.0 - jnp.arange(RET_HEADS, dtype=F32)))
    idx = jnp.arange(CHUNK, dtype=F32)
    rel = idx[:, None] - idx[None, :]
    dmat = jnp.where(rel[None] >= 0, jnp.exp(jnp.maximum(rel, 0.0)[None] * lg[:, None, None]), 0.0)
    qdec = jnp.exp((idx[:, None] + 1.0) * lg[None, :]).T[:, :, None]
    kdec = jnp.exp((CHUNK - 1.0 - idx)[:, None] * lg[None, :]).T[:, :, None]
    cdec = jnp.exp(CHUNK * lg)[:, None, None]
    half = RET_DK // 2
    inv = ROPE_BASE ** (-jnp.arange(half, dtype=F32) / half)
    ang = jnp.arange(length, dtype=jnp.int32).astype(F32)[:, None] * inv[None, :]
    cos, sin = jnp.cos(ang), jnp.sin(ang)
    return (jnp.concatenate([cos, cos], axis=1), jnp.concatenate([-sin, sin], axis=1), dmat, qdec, kdec, cdec)


def _tiles(length, pref=512):
    return pref if length % pref == 0 else length


def _layer_fwd(x, p, consts, length, last, target, comm=None):
    tm = _tiles(length)
    (h,), _ = _rowmap("pre_norm", lambda x_, w_: ((_rms(x_, w_),), ()), length, tm,
                      [(x, D_MODEL, 0)], [p["pre_norm"]], out_ws=[D_MODEL], out_dtypes=[BF16])
    proj = _matmul("in_proj", h, p["w_in"], "nn", length, N_PAD, D_MODEL)
    xc = _conv_fwd(proj, p["conv_w"], p["conv_b"], length, _tiles(length, 1024), 512)
    mixed, st_a, tinv, *got = _gdn_fwd(xc, proj, p["gdn_A_log"], p["gdn_dt_bias"], p["gdn_norm"], length, comm)
    y_b, st_b = _ssd_fwd(xc, proj, p["ssd_A_log"], p["ssd_dt_bias"], p["ssd_D"], length)
    for g in range(SSD_GROUPS):
        (mixed,), _ = _rowmap(f"ssd_gate{g}", lambda y_, z_, w_: ((_rms(y_ * _silu(z_), w_),), ()), length, tm,
                              [(y_b, 512, g), (proj, 512, _MINE["sz"] // 512 + g)], [p["ssd_norm"][g]],
                              out_ws=[512], out_dtypes=[BF16], place={0: (mixed, MIX_W, 1 + g)})
    mixed, st_c = _ret_fwd(proj, *consts, p["ret_norm"], mixed, length)
    w_out = p["w_out"](got) if callable(p["w_out"]) else p["w_out"]
    out = _matmul("out_proj", mixed, w_out, "nn", length, D_MODEL, MIX_W)
    res = dict(x=x, h=h, proj=proj, xc=xc, st_a=st_a, tinv=tinv, st_b=st_b, st_c=st_c, y_b=y_b, mixed=mixed,
               out=out, w_out=w_out)
    if not last:
        (y,), _ = _rowmap("post_norm", lambda x_, o_, w_: ((x_ + _rms(o_, w_),), ()), length, tm,
                          [(x, D_MODEL, 0), (out, D_MODEL, 0)], [p["post_norm"]], out_ws=[D_MODEL])
        return y, res, None, got

    def head(x_, o_, t_, w_):
        e = x_ + _rms(o_, w_) - t_
        row = jnp.mean(e * e, axis=-1, keepdims=True)
        loss = 0.5 * jnp.sum(row, axis=0, keepdims=True)
        return (e * (1.0 / D_MODEL),), (loss + jnp.zeros((1, LANE), F32),)

    (dy,), (loss,) = _rowmap("loss_head", head, length, tm,
                             [(x, D_MODEL, 0), (out, D_MODEL, 0), (target, D_MODEL, 0)], [p["post_norm"]],
                             out_ws=[D_MODEL], acc_ws=[LANE])
    return None, res, (dy, loss[0, 0]), got


def _layer_bwd(dy, p, res, consts, length, comm_scan=None, comm_dx=None):
    tm = _tiles(length)
    g = {}
    (d_out,), (g["post_norm"],) = _rowmap("post_norm_bwd", lambda o_, w_: ((_rms(o_, w_),), ()), length, tm,
                                          [(res["out"], D_MODEL, 0)], [p["post_norm"]], cts=[(dy, D_MODEL, 0)])
    dmix = _matmul("out_proj_dx", d_out, res["w_out"], "nt", length, MIX_W, D_MODEL)
    g["w_out"] = _matmul("out_proj_dw", res["mixed"], d_out, "tn", MIX_W, D_MODEL, length)
    comm = comm_scan(g) if comm_scan else None
    proj, xc = res["proj"], res["xc"]
    (dxc_a, dproj, dsm_a, g["gdn_A_log"], g["gdn_dt_bias"], g["gdn_norm"], *got) = _gdn_bwd(
        xc, proj, p["gdn_A_log"], p["gdn_dt_bias"], p["gdn_norm"], res["st_a"], res["tinv"], dmix, length, comm)
    dyb, dsn = None, []
    for gi in range(SSD_GROUPS):
        (dyb, dproj), (dn_g,) = _rowmap(
            f"ssd_gate_bwd{gi}", lambda y_, z_, w_: ((_rms(y_ * _silu(z_), w_),), ()), length, tm,
            [(res["y_b"], 512, gi), (proj, 512, _MINE["sz"] // 512 + gi)], [p["ssd_norm"][gi]],
            cts=[(dmix, 512, 1 + gi)], d_dtypes=[F32, BF16],
            place={0: (dyb, SSD_GROUPS * 512, gi), 1: (dproj, N_PAD, _MINE["sz"] // 512 + gi)})
        dsn.append(dn_g)
    g["ssd_norm"] = dsn
    dxc_b, dsm_b, g["ssd_A_log"], g["ssd_dt_bias"], g["ssd_D"] = _ssd_bwd(
        xc, proj, p["ssd_A_log"], p["ssd_dt_bias"], p["ssd_D"], res["st_b"], dyb, length)
    dproj, g["ret_norm"] = _ret_bwd(proj, *consts, p["ret_norm"], res["st_c"], dmix, dproj, length)
    tmc = _tiles(length, 1024)
    dpre, g["conv_w"], g["conv_b"] = _conv_bwd_pre(proj, p["conv_w"], p["conv_b"], dxc_a, dxc_b, length, tmc, 512)
    dproj = _conv_bwd_x(dpre, p["conv_w"], dproj, length, tmc, 512)
    pad = SMALL_OFF % 512 + 512 - LANE
    (dproj,), _ = _rowmap("dsmall", lambda a_, b_, c_: ((jnp.concatenate([a_ + b_ + c_, jnp.zeros((a_.shape[0], pad), F32)], axis=1),), ()),
                          length, tm, [(dsm_a, LANE, 0), (dsm_b[0], LANE, 0), (dsm_b[1], LANE, 0)], [],
                          out_ws=[512], out_dtypes=[BF16], place={0: (dproj, N_PAD, SMALL_OFF // 512)})
    g["w_in"] = _matmul("in_proj_dw", res["h"], dproj, "tn", D_MODEL, N_PAD, length)
    if comm_dx:
        dh, *got_dx = _matmul("in_proj_dx_scatter", dproj, p["w_in"], "nt", length, D_MODEL, N_PAD, comm=comm_dx(g))
    else:
        dh, got_dx = _matmul("in_proj_dx", dproj, p["w_in"], "nt", length, D_MODEL, N_PAD), []
    (dx,), (g["pre_norm"],) = _rowmap(
        "pre_norm_bwd", lambda x_, w_: ((_rms(x_, w_), x_), ()), length, tm,
        [(res["x"], D_MODEL, 0)], [p["pre_norm"]], cts=[(dh, D_MODEL, 0), (dy, D_MODEL, 0)])
    return dx, g, got, got_dx


def _lane_row(vals, piece):
    lo = _MINE[piece] - SMALL_OFF
    return jnp.pad(vals[None], ((0, 0), (lo, LANE - lo - vals.shape[0])))


def _from_lane_row(row, piece, n):
    lo = _MINE[piece] - SMALL_OFF
    return row.reshape(-1, LANE).sum(axis=0)[lo:lo + n]


def _make_layer(l, w_in_full, w_out_full, conv_full, sw):
    return dict(
        pre_norm=sw["pre_norm"][l][None], post_norm=sw["post_norm"][l][None], w_in=_perm_cols(w_in_full),
        w_out=w_out_full,
        conv_w=jnp.concatenate([conv_full[0], conv_full[1]], axis=1),
        conv_b=jnp.concatenate([jnp.zeros((CONV_CH,), F32), sw["ssd_conv_b"][l]])[None],
        gdn_A_log=_lane_row(sw["gdn_A_log"][l], "ga"), gdn_dt_bias=_lane_row(sw["gdn_dt_bias"][l], "ga"),
        gdn_norm=sw["gdn_norm"][l][None],
        ssd_A_log=_lane_row(sw["ssd_A_log"][l], "sdt"), ssd_dt_bias=_lane_row(sw["ssd_dt_bias"][l], "sdt"),
        ssd_D=_lane_row(sw["ssd_D"][l], "sdt"),
        ssd_norm=[sw["ssd_norm"][l][None, 512 * g:512 * (g + 1)] for g in range(SSD_GROUPS)],
        ret_norm=sw["ret_norm"][l][None])


def _local_step(xb, tgt, layer0, make_layer1, length, fwd_comm=None, comm_scan=None, comm_dx=None):
    consts = _ret_consts(length)
    y0, res0, _, got = _layer_fwd(xb, layer0, consts, length, False, tgt, fwd_comm)
    layer1 = make_layer1(got)
    _, res1, (dy, loss_local), _ = _layer_fwd(y0, layer1, consts, length, True, tgt)
    dy, grads1, _, _ = _layer_bwd(dy, layer1, res1, consts, length)
    dx, grads0, recv_scan, recv_dx = _layer_bwd(
        dy, layer0, res0, consts, length,
        (lambda g0: comm_scan(grads1, g0)) if comm_scan else None, comm_dx)
    return loss_local, dx, [grads0, grads1], recv_scan, recv_dx


_SMALL = ["pre_norm", "post_norm", "gdn_A_log", "gdn_dt_bias", "gdn_norm", "ssd_conv_b", "ssd_A_log",
          "ssd_dt_bias", "ssd_D", "ssd_norm", "ret_norm"]


def _pack_small(arrs):
    rows = []
    for n in _SMALL:
        flat = arrs[n].reshape(-1)
        pad = (-flat.shape[0]) % LANE
        rows.append(jnp.pad(flat, (0, pad)).reshape(-1, LANE))
    out = jnp.concatenate(rows, axis=0)
    return jnp.pad(out, ((0, (-out.shape[0]) % SUBLANE), (0, 0)))


def _unpack_small(packed, like):
    out, r = {}, 0
    for n in _SMALL:
        cnt = like[n].size
        nrow = -(-cnt // LANE)
        out[n] = packed[r:r + nrow].reshape(-1)[:cnt].reshape(like[n].shape)
        r += nrow
    return out


def kernel(x, pre_norm, post_norm, w_in, gdn_conv, gdn_A_log, gdn_dt_bias, gdn_norm, ssd_conv, ssd_conv_b, ssd_A_log, ssd_dt_bias, ssd_D, ssd_norm, ret_norm, w_out, loss_target, m_pre_norm, m_post_norm, m_w_in, m_gdn_conv, m_gdn_A_log, m_gdn_dt_bias, m_gdn_norm, m_ssd_conv, m_ssd_conv_b, m_ssd_A_log, m_ssd_dt_bias, m_ssd_D, m_ssd_norm, m_ret_norm, m_w_out, v_pre_norm, v_post_norm, v_w_in, v_gdn_conv, v_gdn_A_log, v_gdn_dt_bias, v_gdn_norm, v_ssd_conv, v_ssd_conv_b, v_ssd_A_log, v_ssd_dt_bias, v_ssd_D, v_ssd_norm, v_ret_norm, v_w_out):
    length = x.shape[1]
    xb = x[0]
    tgt = loss_target[0]
    small_w = dict(pre_norm=pre_norm, post_norm=post_norm, gdn_A_log=gdn_A_log, gdn_dt_bias=gdn_dt_bias,
                   gdn_norm=gdn_norm, ssd_conv_b=ssd_conv_b, ssd_A_log=ssd_A_log, ssd_dt_bias=ssd_dt_bias,
                   ssd_D=ssd_D, ssd_norm=ssd_norm, ret_norm=ret_norm)
    small_m = dict(pre_norm=m_pre_norm, post_norm=m_post_norm, gdn_A_log=m_gdn_A_log, gdn_dt_bias=m_gdn_dt_bias,
                   gdn_norm=m_gdn_norm, ssd_conv_b=m_ssd_conv_b, ssd_A_log=m_ssd_A_log, ssd_dt_bias=m_ssd_dt_bias,
                   ssd_D=m_ssd_D, ssd_norm=m_ssd_norm, ret_norm=m_ret_norm)
    small_v = dict(pre_norm=v_pre_norm, post_norm=v_post_norm, gdn_A_log=v_gdn_A_log, gdn_dt_bias=v_gdn_dt_bias,
                   gdn_norm=v_gdn_norm, ssd_conv_b=v_ssd_conv_b, ssd_A_log=v_ssd_A_log, ssd_dt_bias=v_ssd_dt_bias,
                   ssd_D=v_ssd_D, ssd_norm=v_ssd_norm, ret_norm=v_ret_norm)

    w_in_b, w_out_b = w_in.astype(BF16), w_out.astype(BF16)
    conv_shard = jnp.stack([gdn_conv, ssd_conv], axis=1)
    full_out = lambda g_out: g_out.reshape(MIX_W, D_MODEL)

    def assemble(l, g_in, w_out_full, g_conv):
        w_in_full = g_in.transpose(1, 0, 2).reshape(D_MODEL, N_IN)
        conv_full = g_conv.transpose(1, 2, 0, 3).reshape(2, CONV_W, CONV_CH)
        return _make_layer(l, w_in_full, w_out_full, conv_full, small_w)

    slab_in = lambda g: _unperm_cols(g["w_in"]).reshape(D_MODEL, N_DEV, SHARD_IN).transpose(1, 0, 2).astype(BF16)
    slab_out = lambda g: g["w_out"].reshape(N_DEV, SHARD_OUT, D_MODEL).astype(BF16)
    slab_conv = lambda g: jnp.stack(
        [g["conv_w"][:, k * CONV_CH:(k + 1) * CONV_CH].reshape(CONV_W, N_DEV, SHARD_CONV).transpose(1, 0, 2)
         for k in range(2)], axis=1).reshape(N_DEV, 2 * CONV_W, SHARD_CONV)

    g_in0, g_conv0 = _comm_call("gather_layer0", "gather", [w_in_b[0], conv_shard[0]])
    layer0 = assemble(0, g_in0, lambda got: full_out(got[3]), g_conv0)
    loss_local, dx, grads, recv_scan, recv_dx = _local_step(
        xb, tgt, layer0, lambda got: assemble(1, got[0], full_out(got[1]), got[2]), length,
        fwd_comm=("gather", [w_in_b[1], w_out_b[1], conv_shard[1], w_out_b[0]]),
        comm_scan=lambda g1, g0: ("scatter", [slab_in(g1), slab_out(g1), slab_conv(g1), slab_out(g0)]),
        comm_dx=lambda g0: ("scatter", [slab_in(g0), slab_conv(g0)]))
    grad_x = dx[None]
    loss = lax.psum(loss_local, ("x", "y", "c"))

    small_g = dict(
        pre_norm=jnp.concatenate([grads[l]["pre_norm"] for l in range(DEPTH)], axis=0),
        post_norm=jnp.concatenate([grads[l]["post_norm"] for l in range(DEPTH)], axis=0),
        gdn_A_log=jnp.stack([_from_lane_row(grads[l]["gdn_A_log"], "ga", GDN_HEADS) for l in range(DEPTH)]),
        gdn_dt_bias=jnp.stack([_from_lane_row(grads[l]["gdn_dt_bias"], "ga", GDN_HEADS) for l in range(DEPTH)]),
        gdn_norm=jnp.concatenate([grads[l]["gdn_norm"] for l in range(DEPTH)], axis=0),
        ssd_conv_b=jnp.concatenate([grads[l]["conv_b"][:, CONV_CH:] for l in range(DEPTH)], axis=0),
        ssd_A_log=jnp.stack([_from_lane_row(grads[l]["ssd_A_log"], "sdt", SSD_HEADS) for l in range(DEPTH)]),
        ssd_dt_bias=jnp.stack([_from_lane_row(grads[l]["ssd_dt_bias"], "sdt", SSD_HEADS) for l in range(DEPTH)]),
        ssd_D=jnp.stack([_from_lane_row(grads[l]["ssd_D"], "sdt", SSD_HEADS) for l in range(DEPTH)]),
        ssd_norm=jnp.concatenate([jnp.concatenate(grads[l]["ssd_norm"], axis=1) for l in range(DEPTH)], axis=0),
        ret_norm=jnp.concatenate([grads[l]["ret_norm"] for l in range(DEPTH)], axis=0))
    gs = _pack_small(small_g)
    gs8 = jnp.broadcast_to(gs[None], (N_DEV,) + gs.shape)
    (r_small,) = _comm_call("exchange_small", "scatter", [gs8])
    recv = [[recv_dx[0], recv_scan[3], recv_dx[1]], recv_scan[:3]]

    conv_w = lambda g_, s_, l: jnp.stack([g_[l], s_[l]], axis=0).reshape(2 * CONV_W, SHARD_CONV)
    o_in, o_out, o_conv = None, None, []
    for l in range(DEPTH):
        o_in = _adam(f"adam_w_in{l}", recv[l][0], w_in, m_w_in, v_w_in, 128, l, o_in)
        o_out = _adam(f"adam_w_out{l}", recv[l][1], w_out, m_w_out, v_w_out, 128, l, o_out)
        o_conv.append(_adam(f"adam_conv{l}", recv[l][2], conv_w(gdn_conv, ssd_conv, l),
                            conv_w(m_gdn_conv, m_ssd_conv, l), conv_w(v_gdn_conv, v_ssd_conv, l), 2 * CONV_W))
    ps_w, ps_m, ps_v = _pack_small(small_w), _pack_small(small_m), _pack_small(small_v)
    o_small = _adam("adam_small", r_small, ps_w, ps_m, ps_v, ps_w.shape[0])

    names = ["pre_norm", "post_norm", "w_in", "gdn_conv", "gdn_A_log", "gdn_dt_bias", "gdn_norm", "ssd_conv",
             "ssd_conv_b", "ssd_A_log", "ssd_dt_bias", "ssd_D", "ssd_norm", "ret_norm", "w_out"]
    outs = []
    for kind in range(4):
        d = _unpack_small(o_small[kind], small_w)
        cv = jnp.stack([o_conv[l][kind].reshape(2, CONV_W, SHARD_CONV) for l in range(DEPTH)], axis=1)
        d["w_in"] = o_in[kind]
        d["w_out"] = o_out[kind]
        d["gdn_conv"] = cv[0]
        d["ssd_conv"] = cv[1]
        outs.extend(d[n] for n in names)
    return (loss, grad_x, *outs)
```

```python
import functools
import math

import numpy as np
import jax
import jax.numpy as jnp
from jax import lax
from jax.experimental import pallas as pl
from jax.experimental.pallas import tpu as pltpu

F32 = jnp.float32
BF16 = jnp.bfloat16

D_MODEL = 1024
DEPTH = 2
CHUNK = 64
CONV_W = 4
EPS = 1e-6
N_DEV = 8

GDN_HEADS = 4
GDN_DK = 128
SSD_HEADS = 16
SSD_P = 64
SSD_N = 128
SSD_GROUPS = 2
SSD_PAIRS = SSD_HEADS // 2
PAIRS_PER_GROUP = SSD_PAIRS // SSD_GROUPS
RET_HEADS = 4
RET_DK = 128
ROPE_BASE = 10000.0
MIX_W = 2048
N_IN = 6680
SHARD_IN = N_IN // N_DEV
SHARD_OUT = MIX_W // N_DEV
CONV_CH = 1536
SHARD_CONV = CONV_CH // N_DEV

ADAM_LR = 0.001
ADAM_B1 = 0.9
ADAM_B2 = 0.999
ADAM_EPS = 1e-08
ADAM_WD = 0.01
ADAM_STEP = 10

LANE = 128
SUBLANE = 8
VMEM_LIMIT = 56 * 1024 * 1024

_ORIG = dict(gq=(0, 512), gk=(512, 512), gv=(1024, 512), gz=(1536, 512), gb=(2048, 4), ga=(2052, 4),
             sx=(2056, 1024), sB=(3080, 256), sC=(3336, 256), sz=(3592, 1024), sdt=(4616, 16),
             rq=(4632, 512), rk=(5144, 512), rv=(5656, 512), rg=(6168, 512))
_ORDER = ["rq", "rk", "rv", "rg", "gq", "gk", "gv", "sx", "sB", "sC", "gz", "sz", "gb", "ga", "sdt"]
_MINE = {}
_off = 0
for _n in _ORDER:
    _MINE[_n] = _off
    _off += _ORIG[_n][1]
N_USED = _off
N_PAD = 7168
CONV_ALL = 2 * CONV_CH
SMALL_OFF = _MINE["gb"]
CONV_OFF = _MINE["gq"]
XC = lambda name: _MINE[name] - CONV_OFF


def _cparams(sem, vmem=None):
    return pltpu.CompilerParams(dimension_semantics=sem, vmem_limit_bytes=vmem)


def _split_bf16(a):
    hi = a.astype(BF16)
    return hi, (a - hi.astype(F32)).astype(BF16)


def _make_mm():
    def raw(a, b, ca, cb):
        return lax.dot_general(a.astype(BF16), b.astype(BF16), (((ca,), (cb,)), ((), ())),
                               preferred_element_type=F32)

    @jax.custom_vjp
    def nn(a, b):
        return raw(a, b, 1, 0)

    @jax.custom_vjp
    def nt(a, b):
        return raw(a, b, 1, 1)

    @jax.custom_vjp
    def tn(a, b):
        return raw(a, b, 0, 0)

    nn.defvjp(lambda a, b: (raw(a, b, 1, 0), (a, b)), lambda r, g: (nt(g, r[1]), tn(r[0], g)))
    nt.defvjp(lambda a, b: (raw(a, b, 1, 1), (a, b)), lambda r, g: (nn(g, r[1]), tn(g, r[0])))
    tn.defvjp(lambda a, b: (raw(a, b, 0, 0), (a, b)), lambda r, g: (nt(r[1], g), nn(r[0], g)))
    return nn, nt, tn


_nn, _nt, _tn = _make_mm()


@jax.custom_vjp
def _swap_halves(t):
    return pltpu.roll(t, LANE // 2, 1)


_swap_halves.defvjp(lambda t: (pltpu.roll(t, LANE // 2, 1), None),
                    lambda _, g: (pltpu.roll(g, LANE // 2, 1),))


@jax.custom_vjp
def _split_rows(x):
    return tuple(x[i * CHUNK:(i + 1) * CHUNK] for i in range(x.shape[0] // CHUNK))


_split_rows.defvjp(lambda x: (tuple(x[i * CHUNK:(i + 1) * CHUNK] for i in range(x.shape[0] // CHUNK)), None),
                   lambda _, gs: (jnp.concatenate(gs, axis=0),))


def _dot3(a, b, ca, cb):
    dot = lambda x, y: lax.dot_general(x, y, (((ca,), (cb,)), ((), ())), preferred_element_type=F32)
    return dot(a[0], b[0]) + (dot(a[0], b[1]) + dot(a[1], b[0]))


@jax.custom_vjp
def _tri_inv(mats):
    return _tri_inv_impl(mats)


def _tri_inv_impl(mats):
    ii = lax.broadcasted_iota(jnp.int32, mats[0].shape, 0)
    jj = lax.broadcasted_iota(jnp.int32, mats[0].shape, 1)
    eye = jnp.where(ii == jj, 1.0, 0.0).astype(F32)
    ts = [eye - a for a in mats]
    ps = [_split_bf16(-a) for a in mats]
    for _ in range(int(math.log2(CHUNK)) - 1):
        ps = [_split_bf16(_dot3(p, p, 1, 0)) for p in ps]
        ts = [t + _dot3(_split_bf16(t), p, 1, 0) for t, p in zip(ts, ps)]
    return ts


def _tri_inv_bwd(ts, gs):
    tsp = [_split_bf16(t) for t in ts]
    xs = [_dot3(t, _split_bf16(g), 0, 0) for t, g in zip(tsp, gs)]
    return ([-_dot3(_split_bf16(x), t, 1, 1) for x, t in zip(xs, tsp)],)


_tri_inv.defvjp(lambda mats: (lambda ts: (ts, ts))(_tri_inv_impl(mats)), _tri_inv_bwd)


@jax.custom_vjp
def _tri_inv_saved(mats, ts):
    return ts


_tri_inv_saved.defvjp(lambda mats, ts: (ts, ts),
                      lambda ts, gs: (_tri_inv_bwd(ts, gs)[0], [jnp.zeros_like(t) for t in ts]))


def _silu(x):
    return x * jax.nn.sigmoid(x)


@jax.custom_vjp
def _softplus(x):
    return jnp.maximum(x, 0.0) + jnp.log1p(jnp.exp(-jnp.abs(x)))


_softplus.defvjp(lambda x: (jnp.maximum(x, 0.0) + jnp.log1p(jnp.exp(-jnp.abs(x))), x),
                 lambda x, g: (g * jax.nn.sigmoid(x),))


def _rms(x, w):
    return x * lax.rsqrt(jnp.mean(x * x, axis=-1, keepdims=True) + EPS) * w


def _chunk_masks(n):
    ii = lax.broadcasted_iota(jnp.int32, (n, n), 0)
    jj = lax.broadcasted_iota(jnp.int32, (n, n), 1)
    return ii >= jj, ii > jj, ii == jj, ii <= jj


def _cumsum_col(g, causal, eye, upper):
    g_row = jnp.sum(jnp.where(eye, g, 0.0), axis=0, keepdims=True)
    col = jnp.sum(jnp.where(causal, g_row, 0.0), axis=1, keepdims=True)
    row = jnp.sum(jnp.where(upper, g, 0.0), axis=0, keepdims=True)
    return col, row


def _lane_col(block, lane):
    pick = lax.broadcasted_iota(jnp.int32, (1, block.shape[1]), 1) == lane
    return jnp.sum(jnp.where(pick, block, 0.0), axis=1, keepdims=True)


def _gdn_chunk(q, k, v, gz, sm, alog_row, dtb_row, nw, s, t_saved=None):
    subs, nh = range(len(q)), len(q[0])
    c = q[0][0].shape[0]
    causal, _, eye, upper = _chunk_masks(c)
    stack = lambda xs: jnp.concatenate(xs, axis=0)
    per = lambda f: [[f(j, h) for h in range(nh)] for j in subs]
    qn = per(lambda j, h: q[j][h] * lax.rsqrt(jnp.sum(q[j][h] * q[j][h], axis=-1, keepdims=True) + EPS)
             * (GDN_DK ** -0.5))
    kn = per(lambda j, h: k[j][h] * lax.rsqrt(jnp.sum(k[j][h] * k[j][h], axis=-1, keepdims=True) + EPS))
    beta_blk = [jax.nn.sigmoid(sm[j]) for j in subs]
    g_blk = [-jnp.exp(alog_row) * _softplus(sm[j] + dtb_row) for j in subs]
    beta = per(lambda j, h: _lane_col(beta_blk[j], _MINE["gb"] - SMALL_OFF + h))
    g = per(lambda j, h: _lane_col(g_blk[j], _MINE["ga"] - SMALL_OFF + h))
    gcum = per(lambda j, h: _cumsum_col(g[j][h], causal, eye, upper)[0])
    eg = per(lambda j, h: jnp.exp(gcum[j][h]))
    glast = per(lambda j, h: jnp.sum(g[j][h], axis=0, keepdims=True))
    kb = per(lambda j, h: kn[j][h] * beta[j][h])
    n = nh * c
    ii = lax.broadcasted_iota(jnp.int32, (n, n), 0)
    jj = lax.broadcasted_iota(jnp.int32, (n, n), 1)
    sh = int(math.log2(c))
    same = lax.shift_right_logical(ii, sh) == lax.shift_right_logical(jj, sh)
    causal_bd = jnp.logical_and(same, ii >= jj)
    strict_bd = jnp.logical_and(same, ii > jj)
    gcum_all = [stack(gcum[j]) for j in subs]
    gcum_row = [jnp.sum(jnp.where(ii == jj, gcum_all[j], 0.0), axis=0, keepdims=True) for j in subs]
    decay = [jnp.where(causal_bd, jnp.exp(jnp.where(causal_bd, gcum_all[j] - gcum_row[j], 0.0)), 0.0) for j in subs]
    kn_all = [stack(kn[j]) for j in subs]
    a_low = [jnp.where(strict_bd, _nt(stack(kb[j]), kn_all[j]) * decay[j], 0.0) for j in subs]
    t = _tri_inv(a_low) if t_saved is None else _tri_inv_saved(a_low, t_saved)
    u = [_nn(t[j], stack([v[j][h] * beta[j][h] for h in range(nh)])) for j in subs]
    w = [_split_rows(_nn(t[j], stack([kb[j][h] * eg[j][h] for h in range(nh)]))) for j in subs]
    attn = [_nt(stack(qn[j]), kn_all[j]) * decay[j] for j in subs]
    kdec = per(lambda j, h: kn[j][h] * jnp.exp(glast[j][h] - gcum[j][h]))
    on = []
    for j in subs:
        v_new_all = u[j] - stack([_nn(w[j][h], s[h]) for h in range(nh)])
        o = _split_rows(stack([_nn(qn[j][h] * eg[j][h], s[h]) for h in range(nh)]) + _nn(attn[j], v_new_all))
        v_new = _split_rows(v_new_all)
        s = [s[h] * jnp.exp(glast[j][h]) + _tn(kdec[j][h], v_new[h]) for h in range(nh)]
        on.append([_rms(o[h], nw) * _silu(gz[j][h]) for h in range(nh)])
    return on, s, t


def _ssd_chunk(lanes, x, bm, cm, sm, alog_row, dtb_row, d_row, hs):
    subs, pairs = range(len(x)), range(len(x[0]))
    c = x[0][0].shape[0]
    lane_i = lax.broadcasted_iota(jnp.int32, (c, LANE), 1)
    lane_lo = lane_i < SSD_P
    lane_lo1 = lax.broadcasted_iota(jnp.int32, (1, LANE), 1) < SSD_P
    causal2 = lax.broadcasted_iota(jnp.int32, (c, LANE), 0) >= jnp.bitwise_and(lane_i, c - 1)
    row_lo = lax.broadcasted_iota(jnp.int32, (LANE, 1), 0) < SSD_P
    eye2 = lax.broadcasted_iota(jnp.int32, (LANE, LANE), 0) == lax.broadcasted_iota(jnp.int32, (LANE, LANE), 1)
    to_row = lambda cols: jnp.sum(jnp.where(eye2, jnp.concatenate(cols, axis=0), 0.0), axis=0, keepdims=True)
    per = lambda f: [[f(j, p) for p in pairs] for j in subs]
    both = lambda blk, p: [_lane_col(blk, lanes[p][h]) for h in range(2)]
    dt_blk = [_softplus(sm[j] + dtb_row) for j in subs]
    a_blk = [dt_blk[j] * -jnp.exp(alog_row) for j in subs]
    alast_blk = [jnp.sum(a_blk[j], axis=0, keepdims=True) for j in subs]
    dt = per(lambda j, p: both(dt_blk[j], p))
    a = per(lambda j, p: both(a_blk[j], p))
    alast = per(lambda j, p: both(alast_blk[j], p))
    dp = [both(d_row, p) for p in pairs]
    a_row = per(lambda j, p: to_row(a[j][p]))
    part = per(lambda j, p: jnp.where(causal2, a_row[j][p], 0.0))
    acum = per(lambda j, p: [jnp.sum(jnp.where(lane_lo, part[j][p], 0.0), axis=1, keepdims=True),
                             jnp.sum(jnp.where(lane_lo, 0.0, part[j][p]), axis=1, keepdims=True)])
    acum_row = per(lambda j, p: to_row(acum[j][p]))
    acum_col = per(lambda j, p: jnp.where(lane_lo, acum[j][p][0], acum[j][p][1]))
    lmat = per(lambda j, p: jnp.where(causal2, jnp.exp(jnp.where(causal2, acum_col[j][p] - acum_row[j][p], 0.0)), 0.0))
    cb = [_nt(cm[j], jnp.concatenate([bm[j], bm[j]], axis=0)) for j in subs]
    xdt = per(lambda j, p: x[j][p] * jnp.where(lane_lo, dt[j][p][0], dt[j][p][1]))
    xdt_rows = per(lambda j, p: jnp.concatenate([jnp.where(lane_lo, xdt[j][p], 0.0),
                                                 jnp.where(lane_lo, 0.0, xdt[j][p])], axis=0))
    intra = per(lambda j, p: _nn(cb[j] * lmat[j][p], xdt_rows[j][p]))
    skip = per(lambda j, p: x[j][p] * jnp.where(lane_lo1, dp[p][0], dp[p][1]))
    eacc = per(lambda j, p: jnp.exp(acum_col[j][p]))
    wdec = per(lambda j, p: jnp.exp(jnp.where(lane_lo, alast[j][p][0] - acum[j][p][0],
                                              alast[j][p][1] - acum[j][p][1])))
    scale = per(lambda j, p: jnp.exp(jnp.where(row_lo, alast[j][p][0], alast[j][p][1])))
    upd = per(lambda j, p: _tn(xdt[j][p] * wdec[j][p], bm[j]))
    y = []
    for j in subs:
        y.append([skip[j][p] + intra[j][p] + _nt(cm[j], hs[p]) * eacc[j][p] for p in pairs])
        hs = [hs[p] * scale[j][p] + upd[j][p] for p in pairs]
    return y, hs


def _ret_chunk(rq, rk, rv, rg, cos2, sin2, dmat, qdec, kdec, cdec, nw, r):
    subs, hs = range(len(rq)), range(len(rq[0]))
    per = lambda f: [[f(j, h) for h in hs] for j in subs]
    q = per(lambda j, h: rq[j][h] * cos2[j] + _swap_halves(rq[j][h]) * sin2[j])
    k = per(lambda j, h: (rk[j][h] * cos2[j] + _swap_halves(rk[j][h]) * sin2[j]) * (RET_DK ** -0.5))
    s = per(lambda j, h: _nt(q[j][h], k[j][h]) * dmat[h])
    intra = per(lambda j, h: _nn(s[j][h], rv[j][h]))
    upd = per(lambda j, h: _tn(k[j][h] * kdec[h], rv[j][h]))
    gate = per(lambda j, h: _silu(rg[j][h]))
    on = []
    for j in subs:
        on.append([_rms(intra[j][h] + _nn(q[j][h], r[h]) * qdec[h], nw) * gate[j][h] for h in hs])
        r = [r[h] * cdec[h] + upd[j][h] for h in hs]
    return on, r


MM_TILE = 1024


def _tile(n, pref=MM_TILE):
    return pref if n % pref == 0 else n


def _matmul(name, a, b, mode, m, n, k, tk=None, comm=None):
    tm, tn = _tile(m), _tile(n)
    tk = _tile(k) if tk is None else tk
    nk = k // tk
    ca, cb = {"nn": (1, 0), "nt": (1, 1), "tn": (0, 0)}[mode]
    grid = (m // tm, n // tn, nk)
    n_comm = len(comm[1]) if comm else 0

    def at_step(which):
        return functools.reduce(jnp.logical_and, [pl.program_id(d) == (0 if which == 0 else grid[d] - 1)
                                                  for d in range(3)])

    def body(*refs):
        a_ref, b_ref = refs[:2]
        c_in = refs[2:2 + n_comm]
        o_ref = refs[2 + n_comm]
        c_out = refs[3 + n_comm:3 + 2 * n_comm]
        sems = refs[3 + 2 * n_comm:]
        if comm:
            @pl.when(at_step(0))
            def _():
                _comm_plan(comm[0], c_in, c_out, *sems)[0]()

        part = lax.dot_general(a_ref[...].astype(BF16), b_ref[...].astype(BF16),
                               (((ca,), (cb,)), ((), ())), preferred_element_type=F32)
        if nk == 1:
            o_ref[...] = part
        else:
            kk = pl.program_id(2)

            @pl.when(kk == 0)
            def _():
                o_ref[...] = part

            @pl.when(kk > 0)
            def _():
                o_ref[...] += part

        if comm:
            @pl.when(at_step(1))
            def _():
                _comm_plan(comm[0], c_in, c_out, *sems)[1]()

    a_spec = pl.BlockSpec((tk, tm), lambda i, j, kk: (kk, i)) if mode == "tn" else pl.BlockSpec((tm, tk), lambda i, j, kk: (i, kk))
    b_spec = pl.BlockSpec((tn, tk), lambda i, j, kk: (j, kk)) if mode == "nt" else pl.BlockSpec((tk, tn), lambda i, j, kk: (kk, j))
    hbm = pl.BlockSpec(memory_space=pl.ANY)
    res = pl.pallas_call(
        body, name=name, grid=grid,
        in_specs=[a_spec, b_spec] + [hbm] * n_comm,
        out_specs=[pl.BlockSpec((tm, tn), lambda i, j, kk: (i, j))] + [hbm] * n_comm,
        out_shape=[jax.ShapeDtypeStruct((m, n), F32)] + (_comm_out_shapes(*comm) if comm else []),
        scratch_shapes=_comm_scratch(n_comm) if comm else [],
        compiler_params=_cparams(("arbitrary",) * 3 if comm else ("parallel", "parallel", "arbitrary"), VMEM_LIMIT),
    )(a, b, *(comm[1] if comm else []))
    return res if comm else res[0]


def _rowmap(name, fn, length, tm, rows, params, out_ws=(), acc_ws=(), out_dtypes=None, cts=None, nd_rows=(),
            d_dtypes=None, place=None):
    nt_ = length // tm
    n_r, n_nd, n_p = len(rows), len(nd_rows), len(params)
    rspec = lambda bw, cbk: pl.BlockSpec((tm, bw), lambda i: (i, cbk))
    pspec = lambda w: pl.BlockSpec((1, w), lambda i: (0, 0))
    in_arrays = [r[0] for r in rows] + [r[0] for r in nd_rows] + list(params)
    in_specs = [rspec(r[1], r[2]) for r in rows] + [rspec(r[1], r[2]) for r in nd_rows] + [pspec(p.shape[1]) for p in params]
    place = place or {}

    def load(refs):
        return [r[...].astype(F32) for r in refs]

    def placed(widths, dtypes, n_before):
        specs, shapes, extra, alias = [], [], [], {}
        for k, (w, dt) in enumerate(zip(widths, dtypes)):
            arr, total, cbk = place.get(k, (None, w, 0))
            specs.append(rspec(w, cbk))
            shapes.append(jax.ShapeDtypeStruct((length, total), dt))
            if arr is not None:
                alias[n_before + len(extra)] = k
                extra.append(arr)
        return specs, shapes, extra, alias

    if cts is None:
        n_o, n_a = len(out_ws), len(acc_ws)
        out_dtypes_ = out_dtypes or [F32] * n_o

        o_specs, o_shapes, extra, alias = placed(out_ws, out_dtypes_, n_r + n_nd + n_p)

        def body(*refs):
            ins = load(refs[:n_r + n_nd + n_p])
            outs, accs = fn(*ins)
            o_refs = refs[n_r + n_nd + n_p + len(extra):]
            for o_ref, o in zip(o_refs[:n_o], outs):
                o_ref[...] = o.astype(o_ref.dtype)
            first = pl.program_id(0) == 0
            for a_ref, acc in zip(o_refs[n_o:], accs):
                @pl.when(first)
                def _(a_ref=a_ref):
                    a_ref[...] = jnp.zeros_like(a_ref)
                a_ref[...] += acc

        res = pl.pallas_call(
            body, name=name, grid=(nt_,), in_specs=in_specs + [pl.BlockSpec(memory_space=pl.ANY)] * len(extra),
            out_specs=o_specs + [pspec(w) for w in acc_ws],
            out_shape=o_shapes + [jax.ShapeDtypeStruct((1, w), F32) for w in acc_ws],
            input_output_aliases=alias,
            compiler_params=_cparams(("arbitrary",), VMEM_LIMIT),
        )(*in_arrays, *extra)
        return res[:n_o], res[n_o:]

    n_c = len(cts)
    o_specs, o_shapes, extra, alias = placed([r[1] for r in rows], d_dtypes or [F32] * n_r, n_r + n_nd + n_p + n_c)

    def body(*refs):
        ins = load(refs[:n_r + n_nd + n_p])
        ct_vals = load(refs[n_r + n_nd + n_p:n_r + n_nd + n_p + n_c])
        nd = ins[n_r:n_r + n_nd]
        f = lambda rs, ps: fn(*rs, *nd, *ps)[0]
        _, vjp = jax.vjp(f, ins[:n_r], ins[n_r + n_nd:])
        d_rows, d_params = vjp(tuple(ct_vals))
        o_refs = refs[n_r + n_nd + n_p + n_c + len(extra):]
        for o_ref, d in zip(o_refs[:n_r], d_rows):
            o_ref[...] = d.astype(o_ref.dtype)
        first = pl.program_id(0) == 0
        for a_ref, d in zip(o_refs[n_r:], d_params):
            @pl.when(first)
            def _(a_ref=a_ref):
                a_ref[...] = jnp.zeros_like(a_ref)
            a_ref[...] += d

    res = pl.pallas_call(
        body, name=name, grid=(nt_,),
        in_specs=in_specs + [rspec(c[1], c[2]) for c in cts] + [pl.BlockSpec(memory_space=pl.ANY)] * len(extra),
        out_specs=o_specs + [pspec(p.shape[1]) for p in params],
        out_shape=o_shapes + [jax.ShapeDtypeStruct((1, p.shape[1]), F32) for p in params],
        input_output_aliases=alias,
        compiler_params=_cparams(("arbitrary",), VMEM_LIMIT),
    )(*in_arrays, *[c[0] for c in cts], *extra)
    return res[:n_r], res[n_r:]


def _conv_shift(ext, k, tm, forward):
    s = CONV_W - 1 - k
    if forward:
        rolled = ext if s == 0 else pltpu.roll(ext, s, 0)
        return rolled[SUBLANE:, :]
    rolled = ext if s == 0 else pltpu.roll(ext, tm + SUBLANE - s, 0)
    return rolled[:tm, :]


def _conv_pre(x_ref, halo_ref, w_ref, b_ref, first, tm):
    halo = jnp.where(first, 0.0, halo_ref[...])
    ext = jnp.concatenate([halo, x_ref[...]], axis=0)
    w = w_ref[...]
    pre = b_ref[...] + jnp.zeros_like(x_ref[...])
    taps = []
    for k in range(CONV_W):
        tap = _conv_shift(ext, k, tm, True)
        taps.append(tap)
        pre = pre + tap * w[k:k + 1, :]
    return pre, taps


def _conv_fwd(proj, w, b, length, tm, tc):
    hb = tm // SUBLANE
    j0 = CONV_OFF // tc

    def body(x_ref, halo_ref, w_ref, b_ref, o_ref):
        pre, _ = _conv_pre(x_ref, halo_ref, w_ref, b_ref, pl.program_id(0) == 0, tm)
        o_ref[...] = _silu(pre)

    return pl.pallas_call(
        body, name="conv_fwd", grid=(length // tm, CONV_ALL // tc),
        in_specs=[pl.BlockSpec((tm, tc), lambda i, j: (i, j0 + j)),
                  pl.BlockSpec((SUBLANE, tc), lambda i, j: (jnp.maximum(i * hb - 1, 0), j0 + j)),
                  pl.BlockSpec((CONV_W, tc), lambda i, j: (0, j)),
                  pl.BlockSpec((1, tc), lambda i, j: (0, j))],
        out_specs=pl.BlockSpec((tm, tc), lambda i, j: (i, j)),
        out_shape=jax.ShapeDtypeStruct((length, CONV_ALL), F32),
        compiler_params=_cparams(("parallel", "parallel"), VMEM_LIMIT),
    )(proj, proj, w, b)


def _conv_bwd_pre(proj, w, b, dxc_a, dxc_b, length, tm, tc):
    hb = tm // SUBLANE
    j0 = CONV_OFF // tc
    jh = CONV_CH // tc

    def body(x_ref, halo_ref, w_ref, b_ref, dya_ref, dyb_ref, dpre_ref, dw_ref, db_ref):
        i = pl.program_id(1)
        pre, taps = _conv_pre(x_ref, halo_ref, w_ref, b_ref, i == 0, tm)
        sg = jax.nn.sigmoid(pre)
        dy = jnp.where(pl.program_id(0) < jh, dya_ref[...], dyb_ref[...])
        dpre = dy * (sg * (1.0 + pre * (1.0 - sg)))
        dpre_ref[...] = dpre

        @pl.when(i == 0)
        def _():
            dw_ref[...] = jnp.zeros_like(dw_ref)
            db_ref[...] = jnp.zeros_like(db_ref)

        for k in range(CONV_W):
            dw_ref[k:k + 1, :] += jnp.sum(dpre * taps[k], axis=0, keepdims=True)
        db_ref[...] += jnp.sum(dpre, axis=0, keepdims=True)

    return pl.pallas_call(
        body, name="conv_bwd_pre", grid=(CONV_ALL // tc, length // tm),
        in_specs=[pl.BlockSpec((tm, tc), lambda j, i: (i, j0 + j)),
                  pl.BlockSpec((SUBLANE, tc), lambda j, i: (jnp.maximum(i * hb - 1, 0), j0 + j)),
                  pl.BlockSpec((CONV_W, tc), lambda j, i: (0, j)),
                  pl.BlockSpec((1, tc), lambda j, i: (0, j)),
                  pl.BlockSpec((tm, tc), lambda j, i: (jnp.where(j < jh, i, 0), jnp.minimum(j, jh - 1))),
                  pl.BlockSpec((tm, tc), lambda j, i: (jnp.where(j < jh, 0, i), jnp.maximum(j - jh, 0)))],
        out_specs=[pl.BlockSpec((tm, tc), lambda j, i: (i, j)),
                   pl.BlockSpec((CONV_W, tc), lambda j, i: (0, j)),
                   pl.BlockSpec((1, tc), lambda j, i: (0, j))],
        out_shape=[jax.ShapeDtypeStruct((length, CONV_ALL), F32),
                   jax.ShapeDtypeStruct((CONV_W, CONV_ALL), F32),
                   jax.ShapeDtypeStruct((1, CONV_ALL), F32)],
        compiler_params=_cparams(("parallel", "arbitrary"), VMEM_LIMIT),
    )(proj, proj, w, b, dxc_a, dxc_b)


def _conv_bwd_x(dpre, w, dproj, length, tm, tc):
    hb = tm // SUBLANE
    n_t = length // tm
    last_blk = length // SUBLANE - 1
    j0 = CONV_OFF // tc

    def body(d_ref, halo_ref, w_ref, _, o_ref):
        halo = jnp.where(pl.program_id(0) == n_t - 1, 0.0, halo_ref[...])
        ext = jnp.concatenate([d_ref[...], halo], axis=0)
        w = w_ref[...]
        acc = jnp.zeros_like(d_ref[...])
        for k in range(CONV_W):
            acc = acc + _conv_shift(ext, k, tm, False) * w[k:k + 1, :]
        o_ref[...] = acc.astype(o_ref.dtype)

    return pl.pallas_call(
        body, name="conv_bwd_x", grid=(n_t, CONV_ALL // tc),
        in_specs=[pl.BlockSpec((tm, tc), lambda i, j: (i, j)),
                  pl.BlockSpec((SUBLANE, tc), lambda i, j: (jnp.minimum((i + 1) * hb, last_blk), j)),
                  pl.BlockSpec((CONV_W, tc), lambda i, j: (0, j)),
                  pl.BlockSpec(memory_space=pl.ANY)],
        out_specs=pl.BlockSpec((tm, tc), lambda i, j: (i, j0 + j)),
        out_shape=jax.ShapeDtypeStruct(dproj.shape, dproj.dtype), input_output_aliases={3: 0},
        compiler_params=_cparams(("parallel", "parallel"), VMEM_LIMIT),
    )(dpre, dpre, w, dproj)


GDN_CHUNKS_PER_STEP = 4


def _gdn_steps(length):
    nc = length // CHUNK
    gs = GDN_CHUNKS_PER_STEP if nc % GDN_CHUNKS_PER_STEP == 0 else 1
    return gs, nc // gs, gs * CHUNK


def _gdn_tiles(ref, gs):
    return [[ref[j * CHUNK:(j + 1) * CHUNK, h * LANE:(h + 1) * LANE] for h in range(GDN_HEADS)] for j in range(gs)]


def _chunk_rows(ref, gs):
    return [ref[j * CHUNK:(j + 1) * CHUNK, :] for j in range(gs)]


def _gdn_fwd(xc, proj, alog, dtb, nw, length, comm=None):
    gs, steps, rows = _gdn_steps(length)
    h_ = GDN_HEADS
    hw = h_ * LANE
    tn_ = h_ * CHUNK
    n_comm = len(comm[1]) if comm else 0

    def body(*refs):
        q_ref, k_ref, v_ref, gz_ref, sm_ref, al_ref, dt_ref, nw_ref = refs[:8]
        c_in = refs[8:8 + n_comm]
        on_ref, st_ref, t_ref = refs[8 + n_comm:11 + n_comm]
        c_out = refs[11 + n_comm:11 + 2 * n_comm]
        s_ref = refs[11 + 2 * n_comm]
        sems = refs[12 + 2 * n_comm:]

        @pl.when(pl.program_id(0) == 0)
        def _():
            s_ref[...] = jnp.zeros_like(s_ref)
            if comm:
                _comm_plan(comm[0], c_in, c_out, *sems)[0]()

        st_ref[...] = s_ref[...]
        lead = lambda ref: [ref[h] for h in range(h_)]
        on, s_new, t = _gdn_chunk(_gdn_tiles(q_ref, gs), _gdn_tiles(k_ref, gs), _gdn_tiles(v_ref, gs),
                                  _gdn_tiles(gz_ref, gs), _chunk_rows(sm_ref, gs), al_ref[...], dt_ref[...],
                                  nw_ref[...], lead(s_ref))
        for j in range(gs):
            t_ref[j] = t[j]
        for h in range(h_):
            for j in range(gs):
                on_ref[j * CHUNK:(j + 1) * CHUNK, h * LANE:(h + 1) * LANE] = on[j][h].astype(on_ref.dtype)
            s_ref[h] = s_new[h]

        if comm:
            @pl.when(pl.program_id(0) == steps - 1)
            def _():
                _comm_plan(comm[0], c_in, c_out, *sems)[1]()

    blk = lambda col: pl.BlockSpec((rows, hw), lambda c: (c, col // hw))
    one = pl.BlockSpec((1, LANE), lambda c: (0, 0))
    hbm = pl.BlockSpec(memory_space=pl.ANY)
    return pl.pallas_call(
        body, name="gdn_fwd_" + comm[0] if comm else "gdn_fwd", grid=(steps,),
        in_specs=[blk(XC("gq")), blk(XC("gk")), blk(XC("gv")), blk(_MINE["gz"]),
                  pl.BlockSpec((rows, LANE), lambda c: (c, SMALL_OFF // LANE)), one, one,
                  pl.BlockSpec((1, LANE), lambda c: (0, 0))] + [hbm] * n_comm,
        out_specs=[pl.BlockSpec((rows, hw), lambda c: (c, 0)),
                   pl.BlockSpec((h_, None, LANE, LANE), lambda c: (0, c, 0, 0)),
                   pl.BlockSpec((gs, tn_, tn_), lambda c: (c, 0, 0))] + [hbm] * n_comm,
        out_shape=[jax.ShapeDtypeStruct((length, MIX_W), BF16),
                   jax.ShapeDtypeStruct((h_, steps, LANE, LANE), F32),
                   jax.ShapeDtypeStruct((length // CHUNK, tn_, tn_), F32)]
                  + (_comm_out_shapes(*comm) if comm else []),
        scratch_shapes=[pltpu.VMEM((h_, LANE, LANE), F32)] + (_comm_scratch(n_comm) if comm else []),
        compiler_params=_cparams(("arbitrary",), VMEM_LIMIT),
    )(xc, xc, xc, proj, proj, alog, dtb, nw, *(comm[1] if comm else []))


def _gdn_bwd(xc, proj, alog, dtb, nw, st, tinv, dmix, length, comm=None):
    gs, steps, rows = _gdn_steps(length)
    h_ = GDN_HEADS
    hw = h_ * LANE
    tn_ = h_ * CHUNK
    n_comm = len(comm[1]) if comm else 0
    n_in, n_out = 11, 6

    def body(*refs):
        q_ref, k_ref, v_ref, gz_ref, sm_ref, al_ref, dt_ref, nw_ref, st_ref, t_ref, do_ref = refs[:n_in]
        c_in = refs[n_in:n_in + n_comm]
        o0 = n_in + n_comm
        dqkv_ref, dgz_ref, dsm_ref, dal_ref, ddt_ref, dnw_ref = refs[o0:o0 + n_out]
        c_out = refs[o0 + n_out:o0 + n_out + n_comm]
        ds_ref = refs[o0 + n_out + n_comm]
        sems = refs[o0 + n_out + n_comm + 1:]

        @pl.when(pl.program_id(0) == 0)
        def _():
            ds_ref[...] = jnp.zeros_like(ds_ref)
            dal_ref[...] = jnp.zeros_like(dal_ref)
            ddt_ref[...] = jnp.zeros_like(ddt_ref)
            dnw_ref[...] = jnp.zeros_like(dnw_ref)
            if comm:
                _comm_plan(comm[0], c_in, c_out, *sems)[0]()

        lead = lambda ref: [ref[h] for h in range(h_)]
        ins = (_gdn_tiles(q_ref, gs), _gdn_tiles(k_ref, gs), _gdn_tiles(v_ref, gs), _gdn_tiles(gz_ref, gs),
               _chunk_rows(sm_ref, gs), al_ref[...], dt_ref[...], nw_ref[...], lead(st_ref))
        t_saved = [t_ref[j] for j in range(gs)]
        _, vjp = jax.vjp(lambda *a: _gdn_chunk(*a, t_saved=t_saved)[:2], *ins)
        dq, dk, dv, dgz, dsm, dal, ddt, dnw, ds = vjp((_gdn_tiles(do_ref, gs), lead(ds_ref)))
        for j in range(gs):
            rws = slice(j * CHUNK, (j + 1) * CHUNK)
            for h in range(h_):
                cols = slice(h * LANE, (h + 1) * LANE)
                for part, d in enumerate((dq, dk, dv)):
                    dqkv_ref[rws, part * hw + h * LANE:part * hw + (h + 1) * LANE] = d[j][h]
                dgz_ref[rws, cols] = dgz[j][h].astype(dgz_ref.dtype)
            dsm_ref[rws, :] = dsm[j]
        for h in range(h_):
            ds_ref[h] = ds[h]
        dal_ref[...] += dal
        ddt_ref[...] += ddt
        dnw_ref[...] += dnw

        if comm:
            @pl.when(pl.program_id(0) == steps - 1)
            def _():
                _comm_plan(comm[0], c_in, c_out, *sems)[1]()

    r = lambda c: steps - 1 - c
    blk = lambda col: pl.BlockSpec((rows, hw), lambda c: (r(c), col // hw))
    one = pl.BlockSpec((1, LANE), lambda c: (0, 0))
    hbm = pl.BlockSpec(memory_space=pl.ANY)
    return pl.pallas_call(
        body, name="gdn_bwd_" + comm[0] if comm else "gdn_bwd", grid=(steps,),
        in_specs=[blk(XC("gq")), blk(XC("gk")), blk(XC("gv")), blk(_MINE["gz"]),
                  pl.BlockSpec((rows, LANE), lambda c: (r(c), SMALL_OFF // LANE)), one, one,
                  pl.BlockSpec((1, LANE), lambda c: (0, 0)),
                  pl.BlockSpec((h_, None, LANE, LANE), lambda c: (0, r(c), 0, 0)),
                  pl.BlockSpec((gs, tn_, tn_), lambda c: (r(c), 0, 0)),
                  blk(0)] + [hbm] * n_comm,
        out_specs=[pl.BlockSpec((rows, 3 * hw), lambda c: (r(c), 0)), blk(_MINE["gz"]),
                   pl.BlockSpec((rows, LANE), lambda c: (r(c), 0)), one, one,
                   pl.BlockSpec((1, LANE), lambda c: (0, 0))] + [hbm] * n_comm,
        out_shape=[jax.ShapeDtypeStruct((length, 3 * hw), F32), jax.ShapeDtypeStruct((length, N_PAD), BF16),
                   jax.ShapeDtypeStruct((length, LANE), F32),
                   jax.ShapeDtypeStruct((1, LANE), F32), jax.ShapeDtypeStruct((1, LANE), F32),
                   jax.ShapeDtypeStruct((1, LANE), F32)] + (_comm_out_shapes(*comm) if comm else []),
        scratch_shapes=[pltpu.VMEM((h_, LANE, LANE), F32)] + (_comm_scratch(n_comm) if comm else []),
        compiler_params=_cparams(("arbitrary",), VMEM_LIMIT),
    )(xc, xc, xc, proj, proj, alog, dtb, nw, st, tinv, dmix, *(comm[1] if comm else []))


SSD_CHUNKS_PER_STEP = 4


def _ssd_steps(length):
    nc = length // CHUNK
    gs = SSD_CHUNKS_PER_STEP if nc % SSD_CHUNKS_PER_STEP == 0 else 1
    return gs, nc // gs, gs * CHUNK


def _ssd_lanes(g):
    base = _MINE["sdt"] - SMALL_OFF
    return [[base + (g * PAIRS_PER_GROUP + p) * 2 + j for j in range(2)] for p in range(PAIRS_PER_GROUP)]


def _ssd_tiles(ref, gs):
    return [[ref[j * CHUNK:(j + 1) * CHUNK, p * LANE:(p + 1) * LANE] for p in range(PAIRS_PER_GROUP)]
            for j in range(gs)]


def _ssd_gate(y, z, w):
    return _rms(y * _silu(z), w)


def _ssd_fwd(xc, proj, alog_row, dtb_row, d_row, norm_w, mixed, length):
    gs, steps, rows = _ssd_steps(length)
    ppg = PAIRS_PER_GROUP
    gw = ppg * LANE

    def body(x_ref, b_ref, c_ref, sm_ref, al_ref, db_ref, dp_ref, z_ref, nw_ref, _, y_ref, o_ref, st_ref, hs_ref):
        @pl.when(pl.program_id(1) == 0)
        def _():
            hs_ref[...] = jnp.zeros_like(hs_ref)

        st_ref[...] = hs_ref[...]
        y, hs_new = _ssd_chunk(_ssd_lanes(pl.program_id(0)), _ssd_tiles(x_ref, gs), _chunk_rows(b_ref, gs),
                               _chunk_rows(c_ref, gs), _chunk_rows(sm_ref, gs), al_ref[...], db_ref[...],
                               dp_ref[...], [hs_ref[p] for p in range(ppg)])
        for p in range(ppg):
            hs_ref[p] = hs_new[p]
        for j in range(gs):
            rws = slice(j * CHUNK, (j + 1) * CHUNK)
            y_j = jnp.concatenate(y[j], axis=1)
            y_ref[rws, :] = y_j
            o_ref[rws, :] = _ssd_gate(y_j, z_ref[rws, :], nw_ref[...]).astype(o_ref.dtype)

    one = pl.BlockSpec((1, LANE), lambda g, c: (0, 0))
    return pl.pallas_call(
        body, name="ssd_fwd", grid=(SSD_GROUPS, steps),
        in_specs=[pl.BlockSpec((rows, gw), lambda g, c: (c, XC("sx") // gw + g)),
                  pl.BlockSpec((rows, LANE), lambda g, c: (c, XC("sB") // LANE + g)),
                  pl.BlockSpec((rows, LANE), lambda g, c: (c, XC("sC") // LANE + g)),
                  pl.BlockSpec((rows, LANE), lambda g, c: (c, SMALL_OFF // LANE)),
                  one, one, one,
                  pl.BlockSpec((rows, gw), lambda g, c: (c, _MINE["sz"] // gw + g)),
                  pl.BlockSpec((None, 1, gw), lambda g, c: (g, 0, 0)),
                  pl.BlockSpec(memory_space=pl.ANY)],
        out_specs=[pl.BlockSpec((rows, gw), lambda g, c: (c, g)),
                   pl.BlockSpec((rows, gw), lambda g, c: (c, 1 + g)),
                   pl.BlockSpec((None, ppg, None, LANE, SSD_N), lambda g, c: (g, 0, c, 0, 0))],
        out_shape=[jax.ShapeDtypeStruct((length, SSD_GROUPS * gw), F32),
                   jax.ShapeDtypeStruct(mixed.shape, mixed.dtype),
                   jax.ShapeDtypeStruct((SSD_GROUPS, ppg, steps, LANE, SSD_N), F32)],
        input_output_aliases={9: 1},
        scratch_shapes=[pltpu.VMEM((ppg, LANE, SSD_N), F32)],
        compiler_params=_cparams(("parallel", "arbitrary"), VMEM_LIMIT),
    )(xc, xc, xc, proj, alog_row, dtb_row, d_row, proj, norm_w, mixed)


def _ssd_bwd(xc, proj, alog_row, dtb_row, d_row, norm_w, st, y_b, dmix, dproj, length):
    gs, steps, rows = _ssd_steps(length)
    ppg = PAIRS_PER_GROUP
    gw = ppg * LANE

    def body(x_ref, b_ref, c_ref, sm_ref, al_ref, db_ref, dp_ref, z_ref, nw_ref, st_ref, y_ref, do_ref, _,
             dxc_ref, dz_ref, dsm_ref, dal_ref, ddb_ref, ddp_ref, dnw_ref, dhs_ref):
        g = pl.program_id(1)

        @pl.when(jnp.logical_and(pl.program_id(0) == 0, g == 0))
        def _():
            dhs_ref[...] = jnp.zeros_like(dhs_ref)
            dal_ref[...] = jnp.zeros_like(dal_ref)
            ddb_ref[...] = jnp.zeros_like(ddb_ref)
            ddp_ref[...] = jnp.zeros_like(ddp_ref)
            dnw_ref[...] = jnp.zeros_like(dnw_ref)

        pairs = range(ppg)
        dy, dnw = [], jnp.zeros((1, gw), F32)
        for j in range(gs):
            rws = slice(j * CHUNK, (j + 1) * CHUNK)
            _, gate_vjp = jax.vjp(_ssd_gate, y_ref[rws, :], z_ref[rws, :], nw_ref[...])
            dy_j, dz_j, dnw_j = gate_vjp(do_ref[rws, :].astype(F32))
            dz_ref[rws, :] = dz_j.astype(dz_ref.dtype)
            dy.append([dy_j[:, p * LANE:(p + 1) * LANE] for p in pairs])
            dnw = dnw + dnw_j
        dnw_ref[g] += dnw
        ins = (_ssd_tiles(x_ref, gs), _chunk_rows(b_ref, gs), _chunk_rows(c_ref, gs), _chunk_rows(sm_ref, gs),
               al_ref[...], db_ref[...], dp_ref[...], [st_ref[p] for p in pairs])
        _, vjp = jax.vjp(functools.partial(_ssd_chunk, _ssd_lanes(g)), *ins)
        dx, dbm, dcm, dsm, dal, ddb, ddp, dhs = vjp((dy, [dhs_ref[g, p] for p in pairs]))
        for grp in range(SSD_GROUPS):
            @pl.when(g == grp)
            def _(grp=grp):
                for j in range(gs):
                    rws = slice(j * CHUNK, (j + 1) * CHUNK)
                    for p in pairs:
                        col = grp * gw + p * LANE
                        dxc_ref[rws, col:col + LANE] = dx[j][p]
                    cb_ = XC("sB") - XC("sx") + grp * SSD_N
                    cc_ = XC("sC") - XC("sx") + grp * SSD_N
                    dxc_ref[rws, cb_:cb_ + SSD_N] = dbm[j]
                    dxc_ref[rws, cc_:cc_ + SSD_N] = dcm[j]
        for j in range(gs):
            dsm_ref[j * CHUNK:(j + 1) * CHUNK, :] = dsm[j]
        for p in pairs:
            dhs_ref[g, p] = dhs[p]
        dal_ref[g] += dal
        ddb_ref[g] += ddb
        ddp_ref[g] += ddp

    r = lambda c: steps - 1 - c
    one = pl.BlockSpec((1, LANE), lambda c, g: (0, 0))
    acc = pl.BlockSpec((SSD_GROUPS, 1, LANE), lambda c, g: (0, 0, 0))
    prm = jax.ShapeDtypeStruct((SSD_GROUPS, 1, LANE), F32)
    sz_blk = _MINE["sz"] // gw
    return pl.pallas_call(
        body, name="ssd_bwd", grid=(steps, SSD_GROUPS),
        in_specs=[pl.BlockSpec((rows, gw), lambda c, g: (r(c), XC("sx") // gw + g)),
                  pl.BlockSpec((rows, LANE), lambda c, g: (r(c), XC("sB") // LANE + g)),
                  pl.BlockSpec((rows, LANE), lambda c, g: (r(c), XC("sC") // LANE + g)),
                  pl.BlockSpec((rows, LANE), lambda c, g: (r(c), SMALL_OFF // LANE)),
                  one, one, one,
                  pl.BlockSpec((rows, gw), lambda c, g: (r(c), sz_blk + g)),
                  pl.BlockSpec((None, 1, gw), lambda c, g: (g, 0, 0)),
                  pl.BlockSpec((None, ppg, None, LANE, SSD_N), lambda c, g: (g, 0, r(c), 0, 0)),
                  pl.BlockSpec((rows, gw), lambda c, g: (r(c), g)),
                  pl.BlockSpec((rows, gw), lambda c, g: (r(c), 1 + g)),
                  pl.BlockSpec(memory_space=pl.ANY)],
        out_specs=[pl.BlockSpec((rows, CONV_CH), lambda c, g: (r(c), 0)),
                   pl.BlockSpec((rows, gw), lambda c, g: (r(c), sz_blk + g)),
                   pl.BlockSpec((None, rows, LANE), lambda c, g: (g, r(c), 0)), acc, acc, acc,
                   pl.BlockSpec((SSD_GROUPS, 1, gw), lambda c, g: (0, 0, 0))],
        out_shape=[jax.ShapeDtypeStruct((length, CONV_CH), F32), jax.ShapeDtypeStruct(dproj.shape, dproj.dtype),
                   jax.ShapeDtypeStruct((SSD_GROUPS, length, LANE), F32), prm, prm, prm,
                   jax.ShapeDtypeStruct((SSD_GROUPS, 1, gw), F32)],
        input_output_aliases={12: 1},
        scratch_shapes=[pltpu.VMEM((SSD_GROUPS, ppg, LANE, SSD_N), F32)],
        compiler_params=_cparams(("arbitrary", "arbitrary"), VMEM_LIMIT),
    )(xc, xc, xc, proj, alog_row, dtb_row, d_row, proj, norm_w, st, y_b, dmix, dproj)


def _ret_fwd(proj, cos2, sin2, dmat, qdec, kdec, cdec, nw, mixed, length):
    gs, steps, rows = _gdn_steps(length)
    h_ = RET_HEADS

    def body(q_ref, k_ref, v_ref, g_ref, cos_ref, sin_ref, dm_ref, qd_ref, kd_ref, cd_ref, nw_ref, _,
             on_ref, st_ref, r_ref):
        @pl.when(pl.program_id(0) == 0)
        def _():
            r_ref[...] = jnp.zeros_like(r_ref)

        st_ref[...] = r_ref[...]
        lead = lambda ref: [ref[h] for h in range(h_)]
        on, r_new = _ret_chunk(_gdn_tiles(q_ref, gs), _gdn_tiles(k_ref, gs), _gdn_tiles(v_ref, gs),
                               _gdn_tiles(g_ref, gs), _chunk_rows(cos_ref, gs), _chunk_rows(sin_ref, gs),
                               lead(dm_ref), lead(qd_ref), lead(kd_ref), lead(cd_ref), nw_ref[...], lead(r_ref))
        for h in range(h_):
            for j in range(gs):
                on_ref[j * CHUNK:(j + 1) * CHUNK, h * LANE:(h + 1) * LANE] = on[j][h].astype(on_ref.dtype)
            r_ref[h] = r_new[h]

    hw = h_ * LANE
    blk = lambda name: pl.BlockSpec((rows, hw), lambda c: (c, _MINE[name] // hw))
    tab = pl.BlockSpec((rows, LANE), lambda c: (c, 0))
    full = lambda *s: pl.BlockSpec(s, lambda c: (0,) * len(s))
    return pl.pallas_call(
        body, name="ret_fwd", grid=(steps,),
        in_specs=[blk("rq"), blk("rk"), blk("rv"), blk("rg"), tab, tab,
                  full(h_, CHUNK, CHUNK), full(h_, CHUNK, 1), full(h_, CHUNK, 1), full(h_, 1, 1), full(1, LANE),
                  pl.BlockSpec(memory_space=pl.ANY)],
        out_specs=[pl.BlockSpec((rows, hw), lambda c: (c, (MIX_W - hw) // hw)),
                   pl.BlockSpec((h_, None, LANE, LANE), lambda c: (0, c, 0, 0))],
        out_shape=[jax.ShapeDtypeStruct(mixed.shape, mixed.dtype),
                   jax.ShapeDtypeStruct((h_, steps, LANE, LANE), F32)],
        input_output_aliases={11: 0},
        scratch_shapes=[pltpu.VMEM((h_, LANE, LANE), F32)],
        compiler_params=_cparams(("arbitrary",), VMEM_LIMIT),
    )(proj, proj, proj, proj, cos2, sin2, dmat, qdec, kdec, cdec, nw, mixed)


def _ret_bwd(proj, cos2, sin2, dmat, qdec, kdec, cdec, nw, st, dmix, dproj, length):
    gs, steps, rows = _gdn_steps(length)
    h_ = RET_HEADS
    hw = h_ * LANE

    def body(q_ref, k_ref, v_ref, g_ref, cos_ref, sin_ref, dm_ref, qd_ref, kd_ref, cd_ref, nw_ref, st_ref,
             do_ref, _, dqkvg_ref, dnw_ref, dr_ref):
        @pl.when(pl.program_id(0) == 0)
        def _():
            dr_ref[...] = jnp.zeros_like(dr_ref)
            dnw_ref[...] = jnp.zeros_like(dnw_ref)

        lead = lambda ref: [ref[h] for h in range(h_)]
        consts = (_chunk_rows(cos_ref, gs), _chunk_rows(sin_ref, gs), lead(dm_ref), lead(qd_ref), lead(kd_ref),
                  lead(cd_ref))
        f = lambda q, k, v, g, w_, r_: _ret_chunk(q, k, v, g, *consts, w_, r_)
        _, vjp = jax.vjp(f, _gdn_tiles(q_ref, gs), _gdn_tiles(k_ref, gs), _gdn_tiles(v_ref, gs),
                         _gdn_tiles(g_ref, gs), nw_ref[...], lead(st_ref))
        dq, dk, dv, dg, dnw, dr = vjp((_gdn_tiles(do_ref, gs), lead(dr_ref)))
        for h in range(h_):
            for j in range(gs):
                rws = slice(j * CHUNK, (j + 1) * CHUNK)
                for part, d in enumerate((dq, dk, dv, dg)):
                    col = part * hw + h * LANE
                    dqkvg_ref[rws, col:col + LANE] = d[j][h].astype(dqkvg_ref.dtype)
            dr_ref[h] = dr[h]
        dnw_ref[...] += dnw

    r = lambda c: steps - 1 - c
    blk = lambda name: pl.BlockSpec((rows, hw), lambda c: (r(c), _MINE[name] // hw))
    tab = pl.BlockSpec((rows, LANE), lambda c: (r(c), 0))
    full = lambda *s: pl.BlockSpec(s, lambda c: (0,) * len(s))
    assert _MINE["rq"] % (4 * hw) == 0 and [_MINE[n] - _MINE["rq"] for n in ("rk", "rv", "rg")] == [hw, 2 * hw, 3 * hw]
    return pl.pallas_call(
        body, name="ret_bwd", grid=(steps,),
        in_specs=[blk("rq"), blk("rk"), blk("rv"), blk("rg"), tab, tab,
                  full(h_, CHUNK, CHUNK), full(h_, CHUNK, 1), full(h_, CHUNK, 1), full(h_, 1, 1), full(1, LANE),
                  pl.BlockSpec((h_, None, LANE, LANE), lambda c: (0, r(c), 0, 0)),
                  pl.BlockSpec((rows, hw), lambda c: (r(c), (MIX_W - hw) // hw)),
                  pl.BlockSpec(memory_space=pl.ANY)],
        out_specs=[pl.BlockSpec((rows, 4 * hw), lambda c: (r(c), _MINE["rq"] // (4 * hw))), full(1, LANE)],
        out_shape=[jax.ShapeDtypeStruct(dproj.shape, dproj.dtype), jax.ShapeDtypeStruct((1, LANE), F32)],
        input_output_aliases={13: 0},
        scratch_shapes=[pltpu.VMEM((h_, LANE, LANE), F32)],
        compiler_params=_cparams(("arbitrary",), VMEM_LIMIT),
    )(proj, proj, proj, proj, cos2, sin2, dmat, qdec, kdec, cdec, nw, st, dmix, dproj)


def _mesh_pos():
    x, y, c = lax.axis_index("x"), lax.axis_index("y"), lax.axis_index("c")
    return x, y, c, 4 * x + 2 * y + c


def _peer(x, y, c, mask):
    return (x ^ ((mask >> 2) & 1), y ^ ((mask >> 1) & 1), c ^ (mask & 1))


def _comm_plan(kind, ins, outs, send_sems, recv_sems, local_sems):
    x, y, c, me = _mesh_pos()
    n = len(ins)
    src = (lambda a, idx: ins[a]) if kind == "gather" else (lambda a, idx: ins[a].at[idx])
    local = [pltpu.make_async_copy(src(a, me), outs[a].at[me], local_sems.at[a]) for a in range(n)]
    sends, recvs = [], []
    for mask in range(1, N_DEV):
        px, py, pc = _peer(x, y, c, mask)
        pidx = 4 * px + 2 * py + pc
        for a in range(n):
            k = (mask - 1) * n + a
            copy = lambda s, d, k=k: pltpu.make_async_remote_copy(
                src_ref=s, dst_ref=d, send_sem=send_sems.at[k], recv_sem=recv_sems.at[k],
                device_id=(px, py, pc), device_id_type=pl.DeviceIdType.MESH)
            sends.append(copy(src(a, pidx), outs[a].at[me]))
            recvs.append(copy(src(a, me), outs[a].at[pidx]))

    def start():
        for cp in local + sends:
            cp.start()

    def wait():
        for cp in sends:
            cp.wait_send()
        for cp in recvs:
            cp.wait_recv()
        for cp in local:
            cp.wait()

    return start, wait


def _comm_out_shapes(kind, arrays):
    return [jax.ShapeDtypeStruct(((N_DEV,) + s.shape) if kind == "gather" else s.shape, s.dtype) for s in arrays]


def _comm_scratch(n):
    k_tot = (N_DEV - 1) * n
    return [pltpu.SemaphoreType.DMA((k_tot,)), pltpu.SemaphoreType.DMA((k_tot,)), pltpu.SemaphoreType.DMA((n,))]


def _comm_call(name, kind, arrays):
    n = len(arrays)

    def body(*refs):
        start, wait = _comm_plan(kind, refs[:n], refs[n:2 * n], *refs[2 * n:])
        start()
        wait()

    hbm = pl.BlockSpec(memory_space=pl.ANY)
    return pl.pallas_call(
        body, name=name, in_specs=[hbm] * n, out_specs=[hbm] * n, out_shape=_comm_out_shapes(kind, arrays),
        scratch_shapes=_comm_scratch(n), compiler_params=pltpu.CompilerParams(has_side_effects=True),
    )(*arrays)


def _adam(name, parts, w, m, v, tr, layer=None, prev=None):
    p_, r_, cw = parts.shape
    c1 = 1.0 - ADAM_B1 ** ADAM_STEP
    c2 = 1.0 - ADAM_B2 ** ADAM_STEP
    n_prev = 4 if prev else 0

    def body(p_ref, w_ref, m_ref, v_ref, *rest):
        g_ref, d_ref, nm_ref, nv_ref = rest[n_prev:]
        g = p_ref[0].astype(F32)
        for i in range(1, p_):
            g = g + p_ref[i].astype(F32)
        nm = ADAM_B1 * m_ref[...] + (1.0 - ADAM_B1) * g
        nv = ADAM_B2 * v_ref[...] + (1.0 - ADAM_B2) * (g * g)
        d_ref[...] = -ADAM_LR * ((nm / c1) / (jnp.sqrt(nv / c2) + ADAM_EPS) + ADAM_WD * w_ref[...])
        g_ref[...] = g
        nm_ref[...] = nm
        nv_ref[...] = nv

    if layer is None:
        spec = pl.BlockSpec((tr, cw), lambda i: (i, 0))
        shp = jax.ShapeDtypeStruct((r_, cw), F32)
    else:
        spec = pl.BlockSpec((None, tr, cw), lambda i: (layer, i, 0))
        shp = jax.ShapeDtypeStruct((DEPTH, r_, cw), F32)
    return pl.pallas_call(
        body, name=name, grid=(r_ // tr,),
        in_specs=[pl.BlockSpec((p_, tr, cw), lambda i: (0, i, 0)), spec, spec, spec]
                 + [pl.BlockSpec(memory_space=pl.ANY)] * n_prev,
        out_specs=[spec] * 4, out_shape=[shp] * 4,
        input_output_aliases={4 + i: i for i in range(n_prev)},
        compiler_params=_cparams(("parallel",), VMEM_LIMIT),
    )(parts, w, m, v, *(prev or ()))


def _perm_cols(w_full):
    parts = [w_full[:, _ORIG[n][0]:_ORIG[n][0] + _ORIG[n][1]] for n in _ORDER]
    parts.append(jnp.zeros((w_full.shape[0], N_PAD - N_USED), w_full.dtype))
    return jnp.concatenate(parts, axis=1)


def _unperm_cols(g):
    by_orig = sorted(_ORDER, key=lambda n: _ORIG[n][0])
    return jnp.concatenate([g[:, _MINE[n]:_MINE[n] + _ORIG[n][1]] for n in by_orig], axis=1)


def _ret_consts(length):
    lg = jnp.log(1.0 - 2.0 ** (-5.0 - jnp.arange(RET_HEADS, dtype=F32)))
    idx = jnp.arange(CHUNK, dtype=F32)
    rel = idx[:, None] - idx[None, :]
    dmat = jnp.where(rel[None] >= 0, jnp.exp(jnp.maximum(rel, 0.0)[None] * lg[:, None, None]), 0.0)
    qdec = jnp.exp((idx[:, None] + 1.0) * lg[None, :]).T[:, :, None]
    kdec = jnp.exp((CHUNK - 1.0 - idx)[:, None] * lg[None, :]).T[:, :, None]
    cdec = jnp.exp(CHUNK * lg)[:, None, None]
    half = RET_DK // 2
    inv = ROPE_BASE ** (-jnp.arange(half, dtype=F32) / half)
    ang = jnp.arange(length, dtype=jnp.int32).astype(F32)[:, None] * inv[None, :]
    cos, sin = jnp.cos(ang), jnp.sin(ang)
    return (jnp.concatenate([cos, cos], axis=1), jnp.concatenate([-sin, sin], axis=1), dmat, qdec, kdec, cdec)


def _tiles(length, pref=512):
    return pref if length % pref == 0 else length


def _layer_fwd(x, p, consts, length, last, target, comm=None):
    tm = _tiles(length)
    (h,), _ = _rowmap("pre_norm", lambda x_, w_: ((_rms(x_, w_),), ()), length, tm,
                      [(x, D_MODEL, 0)], [p["pre_norm"]], out_ws=[D_MODEL], out_dtypes=[BF16])
    proj = _matmul("in_proj", h, p["w_in"], "nn", length, N_PAD, D_MODEL)
    xc = _conv_fwd(proj, p["conv_w"], p["conv_b"], length, _tiles(length, 1024), 512)
    mixed, st_a, tinv, *got = _gdn_fwd(xc, proj, p["gdn_A_log"], p["gdn_dt_bias"], p["gdn_norm"], length, comm)
    y_b, mixed, st_b = _ssd_fwd(xc, proj, p["ssd_A_log"], p["ssd_dt_bias"], p["ssd_D"], p["ssd_norm"], mixed, length)
    mixed, st_c = _ret_fwd(proj, *consts, p["ret_norm"], mixed, length)
    w_out = p["w_out"](got) if callable(p["w_out"]) else p["w_out"]
    out = _matmul("out_proj", mixed, w_out, "nn", length, D_MODEL, MIX_W)
    res = dict(x=x, h=h, proj=proj, xc=xc, st_a=st_a, tinv=tinv, st_b=st_b, st_c=st_c, y_b=y_b, mixed=mixed,
               out=out, w_out=w_out)
    if not last:
        (y,), _ = _rowmap("post_norm", lambda x_, o_, w_: ((x_ + _rms(o_, w_),), ()), length, tm,
                          [(x, D_MODEL, 0), (out, D_MODEL, 0)], [p["post_norm"]], out_ws=[D_MODEL])
        return y, res, None, got

    def head(x_, o_, t_, w_):
        e = x_ + _rms(o_, w_) - t_
        row = jnp.mean(e * e, axis=-1, keepdims=True)
        loss = 0.5 * jnp.sum(row, axis=0, keepdims=True)
        return (e * (1.0 / D_MODEL),), (loss + jnp.zeros((1, LANE), F32),)

    (dy,), (loss,) = _rowmap("loss_head", head, length, tm,
                             [(x, D_MODEL, 0), (out, D_MODEL, 0), (target, D_MODEL, 0)], [p["post_norm"]],
                             out_ws=[D_MODEL], acc_ws=[LANE])
    return None, res, (dy, loss[0, 0]), got


def _layer_bwd(dy, p, res, consts, length, comm_scan=None, comm_dx=None):
    tm = _tiles(length)
    g = {}
    (d_out,), (g["post_norm"],) = _rowmap("post_norm_bwd", lambda o_, w_: ((_rms(o_, w_),), ()), length, tm,
                                          [(res["out"], D_MODEL, 0)], [p["post_norm"]], cts=[(dy, D_MODEL, 0)],
                                          d_dtypes=[BF16])
    dmix = _matmul("out_proj_dx", d_out, res["w_out"], "nt", length, MIX_W, D_MODEL)
    g["w_out"] = _matmul("out_proj_dw", res["mixed"], d_out, "tn", MIX_W, D_MODEL, length)
    comm = comm_scan(g) if comm_scan else None
    proj, xc = res["proj"], res["xc"]
    (dxc_a, dproj, dsm_a, g["gdn_A_log"], g["gdn_dt_bias"], g["gdn_norm"], *got) = _gdn_bwd(
        xc, proj, p["gdn_A_log"], p["gdn_dt_bias"], p["gdn_norm"], res["st_a"], res["tinv"], dmix, length, comm)
    dxc_b, dproj, dsm_b, g["ssd_A_log"], g["ssd_dt_bias"], g["ssd_D"], g["ssd_norm"] = _ssd_bwd(
        xc, proj, p["ssd_A_log"], p["ssd_dt_bias"], p["ssd_D"], p["ssd_norm"], res["st_b"], res["y_b"], dmix,
        dproj, length)
    dproj, g["ret_norm"] = _ret_bwd(proj, *consts, p["ret_norm"], res["st_c"], dmix, dproj, length)
    tmc = _tiles(length, 1024)
    dpre, g["conv_w"], g["conv_b"] = _conv_bwd_pre(proj, p["conv_w"], p["conv_b"], dxc_a, dxc_b, length, tmc, 512)
    dproj = _conv_bwd_x(dpre, p["conv_w"], dproj, length, tmc, 512)
    pad = SMALL_OFF % 512 + 512 - LANE
    (dproj,), _ = _rowmap("dsmall", lambda a_, b_, c_: ((jnp.concatenate([a_ + b_ + c_, jnp.zeros((a_.shape[0], pad), F32)], axis=1),), ()),
                          length, tm, [(dsm_a, LANE, 0), (dsm_b[0], LANE, 0), (dsm_b[1], LANE, 0)], [],
                          out_ws=[512], out_dtypes=[BF16], place={0: (dproj, N_PAD, SMALL_OFF // 512)})
    g["w_in"] = _matmul("in_proj_dw", res["h"], dproj, "tn", D_MODEL, N_PAD, length)
    if comm_dx:
        dh, *got_dx = _matmul("in_proj_dx_scatter", dproj, p["w_in"], "nt", length, D_MODEL, N_PAD, comm=comm_dx(g))
    else:
        dh, got_dx = _matmul("in_proj_dx", dproj, p["w_in"], "nt", length, D_MODEL, N_PAD), []
    (dx,), (g["pre_norm"],) = _rowmap(
        "pre_norm_bwd", lambda x_, w_: ((_rms(x_, w_), x_), ()), length, tm,
        [(res["x"], D_MODEL, 0)], [p["pre_norm"]], cts=[(dh, D_MODEL, 0), (dy, D_MODEL, 0)])
    return dx, g, got, got_dx


def _lane_row(vals, piece):
    lo = _MINE[piece] - SMALL_OFF
    return jnp.pad(vals[None], ((0, 0), (lo, LANE - lo - vals.shape[0])))


def _from_lane_row(row, piece, n):
    lo = _MINE[piece] - SMALL_OFF
    return row.reshape(-1, LANE).sum(axis=0)[lo:lo + n]


def _make_layer(l, w_in_full, w_out_full, conv_full, sw):
    return dict(
        pre_norm=sw["pre_norm"][l][None], post_norm=sw["post_norm"][l][None], w_in=_perm_cols(w_in_full),
        w_out=w_out_full,
        conv_w=jnp.concatenate([conv_full[0], conv_full[1]], axis=1),
        conv_b=jnp.concatenate([jnp.zeros((CONV_CH,), F32), sw["ssd_conv_b"][l]])[None],
        gdn_A_log=_lane_row(sw["gdn_A_log"][l], "ga"), gdn_dt_bias=_lane_row(sw["gdn_dt_bias"][l], "ga"),
        gdn_norm=sw["gdn_norm"][l][None],
        ssd_A_log=_lane_row(sw["ssd_A_log"][l], "sdt"), ssd_dt_bias=_lane_row(sw["ssd_dt_bias"][l], "sdt"),
        ssd_D=_lane_row(sw["ssd_D"][l], "sdt"),
        ssd_norm=sw["ssd_norm"][l].reshape(SSD_GROUPS, 1, -1),
        ret_norm=sw["ret_norm"][l][None])


def _local_step(xb, tgt, layer0, make_layer1, length, fwd_comm=None, comm_scan=None, comm_dx=None):
    consts = _ret_consts(length)
    y0, res0, _, got = _layer_fwd(xb, layer0, consts, length, False, tgt, fwd_comm)
    layer1 = make_layer1(got)
    _, res1, (dy, loss_local), _ = _layer_fwd(y0, layer1, consts, length, True, tgt)
    dy, grads1, _, _ = _layer_bwd(dy, layer1, res1, consts, length)
    dx, grads0, recv_scan, recv_dx = _layer_bwd(
        dy, layer0, res0, consts, length,
        (lambda g0: comm_scan(grads1, g0)) if comm_scan else None, comm_dx)
    return loss_local, dx, [grads0, grads1], recv_scan, recv_dx


_SMALL = ["pre_norm", "post_norm", "gdn_A_log", "gdn_dt_bias", "gdn_norm", "ssd_conv_b", "ssd_A_log",
          "ssd_dt_bias", "ssd_D", "ssd_norm", "ret_norm"]


def _pack_small(arrs):
    rows = []
    for n in _SMALL:
        flat = arrs[n].reshape(-1)
        pad = (-flat.shape[0]) % LANE
        rows.append(jnp.pad(flat, (0, pad)).reshape(-1, LANE))
    out = jnp.concatenate(rows, axis=0)
    return jnp.pad(out, ((0, (-out.shape[0]) % SUBLANE), (0, 0)))


def _unpack_small(packed, like):
    out, r = {}, 0
    for n in _SMALL:
        cnt = like[n].size
        nrow = -(-cnt // LANE)
        out[n] = packed[r:r + nrow].reshape(-1)[:cnt].reshape(like[n].shape)
        r += nrow
    return out


def kernel(x, pre_norm, post_norm, w_in, gdn_conv, gdn_A_log, gdn_dt_bias, gdn_norm, ssd_conv, ssd_conv_b, ssd_A_log, ssd_dt_bias, ssd_D, ssd_norm, ret_norm, w_out, loss_target, m_pre_norm, m_post_norm, m_w_in, m_gdn_conv, m_gdn_A_log, m_gdn_dt_bias, m_gdn_norm, m_ssd_conv, m_ssd_conv_b, m_ssd_A_log, m_ssd_dt_bias, m_ssd_D, m_ssd_norm, m_ret_norm, m_w_out, v_pre_norm, v_post_norm, v_w_in, v_gdn_conv, v_gdn_A_log, v_gdn_dt_bias, v_gdn_norm, v_ssd_conv, v_ssd_conv_b, v_ssd_A_log, v_ssd_dt_bias, v_ssd_D, v_ssd_norm, v_ret_norm, v_w_out):
    length = x.shape[1]
    xb = x[0]
    tgt = loss_target[0]
    small_w = dict(pre_norm=pre_norm, post_norm=post_norm, gdn_A_log=gdn_A_log, gdn_dt_bias=gdn_dt_bias,
                   gdn_norm=gdn_norm, ssd_conv_b=ssd_conv_b, ssd_A_log=ssd_A_log, ssd_dt_bias=ssd_dt_bias,
                   ssd_D=ssd_D, ssd_norm=ssd_norm, ret_norm=ret_norm)
    small_m = dict(pre_norm=m_pre_norm, post_norm=m_post_norm, gdn_A_log=m_gdn_A_log, gdn_dt_bias=m_gdn_dt_bias,
                   gdn_norm=m_gdn_norm, ssd_conv_b=m_ssd_conv_b, ssd_A_log=m_ssd_A_log, ssd_dt_bias=m_ssd_dt_bias,
                   ssd_D=m_ssd_D, ssd_norm=m_ssd_norm, ret_norm=m_ret_norm)
    small_v = dict(pre_norm=v_pre_norm, post_norm=v_post_norm, gdn_A_log=v_gdn_A_log, gdn_dt_bias=v_gdn_dt_bias,
                   gdn_norm=v_gdn_norm, ssd_conv_b=v_ssd_conv_b, ssd_A_log=v_ssd_A_log, ssd_dt_bias=v_ssd_dt_bias,
                   ssd_D=v_ssd_D, ssd_norm=v_ssd_norm, ret_norm=v_ret_norm)

    w_in_b, w_out_b = w_in.astype(BF16), w_out.astype(BF16)
    conv_shard = jnp.stack([gdn_conv, ssd_conv], axis=1)
    full_out = lambda g_out: g_out.reshape(MIX_W, D_MODEL)

    def assemble(l, g_in, w_out_full, g_conv):
        w_in_full = g_in.transpose(1, 0, 2).reshape(D_MODEL, N_IN)
        conv_full = g_conv.transpose(1, 2, 0, 3).reshape(2, CONV_W, CONV_CH)
        return _make_layer(l, w_in_full, w_out_full, conv_full, small_w)

    slab_in = lambda g: _unperm_cols(g["w_in"]).reshape(D_MODEL, N_DEV, SHARD_IN).transpose(1, 0, 2).astype(BF16)
    slab_out = lambda g: g["w_out"].reshape(N_DEV, SHARD_OUT, D_MODEL).astype(BF16)
    slab_conv = lambda g: jnp.stack(
        [g["conv_w"][:, k * CONV_CH:(k + 1) * CONV_CH].reshape(CONV_W, N_DEV, SHARD_CONV).transpose(1, 0, 2)
         for k in range(2)], axis=1).reshape(N_DEV, 2 * CONV_W, SHARD_CONV)

    g_in0, g_conv0 = _comm_call("gather_layer0", "gather", [w_in_b[0], conv_shard[0]])
    layer0 = assemble(0, g_in0, lambda got: full_out(got[3]), g_conv0)
    loss_local, dx, grads, recv_scan, recv_dx = _local_step(
        xb, tgt, layer0, lambda got: assemble(1, got[0], full_out(got[1]), got[2]), length,
        fwd_comm=("gather", [w_in_b[1], w_out_b[1], conv_shard[1], w_out_b[0]]),
        comm_scan=lambda g1, g0: ("scatter", [slab_in(g1), slab_out(g1), slab_conv(g1), slab_out(g0)]),
        comm_dx=lambda g0: ("scatter", [slab_in(g0), slab_conv(g0)]))
    grad_x = dx[None]
    loss = lax.psum(loss_local, ("x", "y", "c"))

    small_g = dict(
        pre_norm=jnp.concatenate([grads[l]["pre_norm"] for l in range(DEPTH)], axis=0),
        post_norm=jnp.concatenate([grads[l]["post_norm"] for l in range(DEPTH)], axis=0),
        gdn_A_log=jnp.stack([_from_lane_row(grads[l]["gdn_A_log"], "ga", GDN_HEADS) for l in range(DEPTH)]),
        gdn_dt_bias=jnp.stack([_from_lane_row(grads[l]["gdn_dt_bias"], "ga", GDN_HEADS) for l in range(DEPTH)]),
        gdn_norm=jnp.concatenate([grads[l]["gdn_norm"] for l in range(DEPTH)], axis=0),
        ssd_conv_b=jnp.concatenate([grads[l]["conv_b"][:, CONV_CH:] for l in range(DEPTH)], axis=0),
        ssd_A_log=jnp.stack([_from_lane_row(grads[l]["ssd_A_log"], "sdt", SSD_HEADS) for l in range(DEPTH)]),
        ssd_dt_bias=jnp.stack([_from_lane_row(grads[l]["ssd_dt_bias"], "sdt", SSD_HEADS) for l in range(DEPTH)]),
        ssd_D=jnp.stack([_from_lane_row(grads[l]["ssd_D"], "sdt", SSD_HEADS) for l in range(DEPTH)]),
        ssd_norm=jnp.concatenate([grads[l]["ssd_norm"].reshape(1, -1) for l in range(DEPTH)], axis=0),
        ret_norm=jnp.concatenate([grads[l]["ret_norm"] for l in range(DEPTH)], axis=0))
    gs = _pack_small(small_g)
    gs8 = jnp.broadcast_to(gs[None], (N_DEV,) + gs.shape)
    (r_small,) = _comm_call("exchange_small", "scatter", [gs8])
    recv = [[recv_dx[0], recv_scan[3], recv_dx[1]], recv_scan[:3]]

    conv_w = lambda g_, s_, l: jnp.stack([g_[l], s_[l]], axis=0).reshape(2 * CONV_W, SHARD_CONV)
    o_in, o_out, o_conv = None, None, []
    for l in range(DEPTH):
        o_in = _adam(f"adam_w_in{l}", recv[l][0], w_in, m_w_in, v_w_in, 128, l, o_in)
        o_out = _adam(f"adam_w_out{l}", recv[l][1], w_out, m_w_out, v_w_out, 128, l, o_out)
        o_conv.append(_adam(f"adam_conv{l}", recv[l][2], conv_w(gdn_conv, ssd_conv, l),
                            conv_w(m_gdn_conv, m_ssd_conv, l), conv_w(v_gdn_conv, v_ssd_conv, l), 2 * CONV_W))
    ps_w, ps_m, ps_v = _pack_small(small_w), _pack_small(small_m), _pack_small(small_v)
    o_small = _adam("adam_small", r_small, ps_w, ps_m, ps_v, ps_w.shape[0])

    names = ["pre_norm", "post_norm", "w_in", "gdn_conv", "gdn_A_log", "gdn_dt_bias", "gdn_norm", "ssd_conv",
             "ssd_conv_b", "ssd_A_log", "ssd_dt_bias", "ssd_D", "ssd_norm", "ret_norm", "w_out"]
    outs = []
    for kind in range(4):
        d = _unpack_small(o_small[kind], small_w)
        cv = jnp.stack([o_conv[l][kind].reshape(2, CONV_W, SHARD_CONV) for l in range(DEPTH)], axis=1)
        d["w_in"] = o_in[kind]
        d["w_out"] = o_out[kind]
        d["gdn_conv"] = cv[0]
        d["ssd_conv"] = cv[1]
        outs.extend(d[n] for n in names)
    return (loss, grad_x, *outs)
```

```python
import functools
import math

import numpy as np
import jax
import jax.numpy as jnp
from jax import lax
from jax.experimental import pallas as pl
from jax.experimental.pallas import tpu as pltpu

F32 = jnp.float32
BF16 = jnp.bfloat16

D_MODEL = 1024
DEPTH = 2
CHUNK = 64
CONV_W = 4
EPS = 1e-6
N_DEV = 8

GDN_HEADS = 4
GDN_DK = 128
SSD_HEADS = 16
SSD_P = 64
SSD_N = 128
SSD_GROUPS = 2
SSD_PAIRS = SSD_HEADS // 2
PAIRS_PER_GROUP = SSD_PAIRS // SSD_GROUPS
RET_HEADS = 4
RET_DK = 128
ROPE_BASE = 10000.0
MIX_W = 2048
N_IN = 6680
SHARD_IN = N_IN // N_DEV
SHARD_OUT = MIX_W // N_DEV
CONV_CH = 1536
SHARD_CONV = CONV_CH // N_DEV

ADAM_LR = 0.001
ADAM_B1 = 0.9
ADAM_B2 = 0.999
ADAM_EPS = 1e-08
ADAM_WD = 0.01
ADAM_STEP = 10

LANE = 128
SUBLANE = 8
VMEM_LIMIT = 56 * 1024 * 1024

_ORIG = dict(gq=(0, 512), gk=(512, 512), gv=(1024, 512), gz=(1536, 512), gb=(2048, 4), ga=(2052, 4),
             sx=(2056, 1024), sB=(3080, 256), sC=(3336, 256), sz=(3592, 1024), sdt=(4616, 16),
             rq=(4632, 512), rk=(5144, 512), rv=(5656, 512), rg=(6168, 512))
_ORDER = ["rq", "rk", "rv", "rg", "gq", "gk", "gv", "sx", "sB", "sC", "gz", "sz", "gb", "ga", "sdt"]
_MINE = {}
_off = 0
for _n in _ORDER:
    _MINE[_n] = _off
    _off += _ORIG[_n][1]
N_USED = _off
N_PAD = 7168
CONV_ALL = 2 * CONV_CH
SMALL_OFF = _MINE["gb"]
CONV_OFF = _MINE["gq"]
XC = lambda name: _MINE[name] - CONV_OFF


def _cparams(sem, vmem=None):
    return pltpu.CompilerParams(dimension_semantics=sem, vmem_limit_bytes=vmem)


def _split_bf16(a):
    hi = a.astype(BF16)
    return hi, (a - hi.astype(F32)).astype(BF16)


def _make_mm():
    def raw(a, b, ca, cb):
        return lax.dot_general(a.astype(BF16), b.astype(BF16), (((ca,), (cb,)), ((), ())),
                               preferred_element_type=F32)

    @jax.custom_vjp
    def nn(a, b):
        return raw(a, b, 1, 0)

    @jax.custom_vjp
    def nt(a, b):
        return raw(a, b, 1, 1)

    @jax.custom_vjp
    def tn(a, b):
        return raw(a, b, 0, 0)

    nn.defvjp(lambda a, b: (raw(a, b, 1, 0), (a, b)), lambda r, g: (nt(g, r[1]), tn(r[0], g)))
    nt.defvjp(lambda a, b: (raw(a, b, 1, 1), (a, b)), lambda r, g: (nn(g, r[1]), tn(g, r[0])))
    tn.defvjp(lambda a, b: (raw(a, b, 0, 0), (a, b)), lambda r, g: (nt(r[1], g), nn(r[0], g)))
    return nn, nt, tn


_nn, _nt, _tn = _make_mm()


@jax.custom_vjp
def _swap_halves(t):
    return pltpu.roll(t, LANE // 2, 1)


_swap_halves.defvjp(lambda t: (pltpu.roll(t, LANE // 2, 1), None),
                    lambda _, g: (pltpu.roll(g, LANE // 2, 1),))


@jax.custom_vjp
def _split_rows(x):
    return tuple(x[i * CHUNK:(i + 1) * CHUNK] for i in range(x.shape[0] // CHUNK))


_split_rows.defvjp(lambda x: (tuple(x[i * CHUNK:(i + 1) * CHUNK] for i in range(x.shape[0] // CHUNK)), None),
                   lambda _, gs: (jnp.concatenate(gs, axis=0),))


def _dot3(a, b, ca, cb):
    dot = lambda x, y: lax.dot_general(x, y, (((ca,), (cb,)), ((), ())), preferred_element_type=F32)
    return dot(a[0], b[0]) + (dot(a[0], b[1]) + dot(a[1], b[0]))


@jax.custom_vjp
def _tri_inv(mats):
    return _tri_inv_impl(mats)


def _tri_inv_impl(mats):
    ii = lax.broadcasted_iota(jnp.int32, mats[0].shape, 0)
    jj = lax.broadcasted_iota(jnp.int32, mats[0].shape, 1)
    eye = jnp.where(ii == jj, 1.0, 0.0).astype(F32)
    ts = [eye - a for a in mats]
    ps = [_split_bf16(-a) for a in mats]
    for _ in range(int(math.log2(CHUNK)) - 1):
        ps = [_split_bf16(_dot3(p, p, 1, 0)) for p in ps]
        ts = [t + _dot3(_split_bf16(t), p, 1, 0) for t, p in zip(ts, ps)]
    return ts


def _tri_inv_bwd(ts, gs):
    tsp = [_split_bf16(t) for t in ts]
    xs = [_dot3(t, _split_bf16(g), 0, 0) for t, g in zip(tsp, gs)]
    return ([-_dot3(_split_bf16(x), t, 1, 1) for x, t in zip(xs, tsp)],)


_tri_inv.defvjp(lambda mats: (lambda ts: (ts, ts))(_tri_inv_impl(mats)), _tri_inv_bwd)


@jax.custom_vjp
def _tri_inv_saved(mats, ts):
    return ts


_tri_inv_saved.defvjp(lambda mats, ts: (ts, ts),
                      lambda ts, gs: (_tri_inv_bwd(ts, gs)[0], [jnp.zeros_like(t) for t in ts]))


def _silu(x):
    return x * jax.nn.sigmoid(x)


@jax.custom_vjp
def _softplus(x):
    return jnp.maximum(x, 0.0) + jnp.log1p(jnp.exp(-jnp.abs(x)))


_softplus.defvjp(lambda x: (jnp.maximum(x, 0.0) + jnp.log1p(jnp.exp(-jnp.abs(x))), x),
                 lambda x, g: (g * jax.nn.sigmoid(x),))


def _rms(x, w):
    return x * lax.rsqrt(jnp.mean(x * x, axis=-1, keepdims=True) + EPS) * w


def _chunk_masks(n):
    ii = lax.broadcasted_iota(jnp.int32, (n, n), 0)
    jj = lax.broadcasted_iota(jnp.int32, (n, n), 1)
    return ii >= jj, ii > jj, ii == jj, ii <= jj


def _cumsum_col(g, causal, eye, upper):
    g_row = jnp.sum(jnp.where(eye, g, 0.0), axis=0, keepdims=True)
    col = jnp.sum(jnp.where(causal, g_row, 0.0), axis=1, keepdims=True)
    row = jnp.sum(jnp.where(upper, g, 0.0), axis=0, keepdims=True)
    return col, row


def _lane_col(block, lane):
    pick = lax.broadcasted_iota(jnp.int32, (1, block.shape[1]), 1) == lane
    return jnp.sum(jnp.where(pick, block, 0.0), axis=1, keepdims=True)


def _gdn_chunk(q, k, v, gz, sm, alog_row, dtb_row, nw, s, t_saved=None):
    subs, nh = range(len(q)), len(q[0])
    c = q[0][0].shape[0]
    causal, _, eye, upper = _chunk_masks(c)
    stack = lambda xs: jnp.concatenate(xs, axis=0)
    per = lambda f: [[f(j, h) for h in range(nh)] for j in subs]
    qn = per(lambda j, h: q[j][h] * lax.rsqrt(jnp.sum(q[j][h] * q[j][h], axis=-1, keepdims=True) + EPS)
             * (GDN_DK ** -0.5))
    kn = per(lambda j, h: k[j][h] * lax.rsqrt(jnp.sum(k[j][h] * k[j][h], axis=-1, keepdims=True) + EPS))
    beta_blk = [jax.nn.sigmoid(sm[j]) for j in subs]
    g_blk = [-jnp.exp(alog_row) * _softplus(sm[j] + dtb_row) for j in subs]
    beta = per(lambda j, h: _lane_col(beta_blk[j], _MINE["gb"] - SMALL_OFF + h))
    g = per(lambda j, h: _lane_col(g_blk[j], _MINE["ga"] - SMALL_OFF + h))
    gcum = per(lambda j, h: _cumsum_col(g[j][h], causal, eye, upper)[0])
    eg = per(lambda j, h: jnp.exp(gcum[j][h]))
    glast = per(lambda j, h: jnp.sum(g[j][h], axis=0, keepdims=True))
    kb = per(lambda j, h: kn[j][h] * beta[j][h])
    n = nh * c
    ii = lax.broadcasted_iota(jnp.int32, (n, n), 0)
    jj = lax.broadcasted_iota(jnp.int32, (n, n), 1)
    sh = int(math.log2(c))
    same = lax.shift_right_logical(ii, sh) == lax.shift_right_logical(jj, sh)
    causal_bd = jnp.logical_and(same, ii >= jj)
    strict_bd = jnp.logical_and(same, ii > jj)
    gcum_all = [stack(gcum[j]) for j in subs]
    gcum_row = [jnp.sum(jnp.where(ii == jj, gcum_all[j], 0.0), axis=0, keepdims=True) for j in subs]
    decay = [jnp.where(causal_bd, jnp.exp(jnp.where(causal_bd, gcum_all[j] - gcum_row[j], 0.0)), 0.0) for j in subs]
    kn_all = [stack(kn[j]) for j in subs]
    a_low = [jnp.where(strict_bd, _nt(stack(kb[j]), kn_all[j]) * decay[j], 0.0) for j in subs]
    t = _tri_inv(a_low) if t_saved is None else _tri_inv_saved(a_low, t_saved)
    u = [_nn(t[j], stack([v[j][h] * beta[j][h] for h in range(nh)])) for j in subs]
    w = [_split_rows(_nn(t[j], stack([kb[j][h] * eg[j][h] for h in range(nh)]))) for j in subs]
    attn = [_nt(stack(qn[j]), kn_all[j]) * decay[j] for j in subs]
    kdec = per(lambda j, h: kn[j][h] * jnp.exp(glast[j][h] - gcum[j][h]))
    on = []
    for j in subs:
        v_new_all = u[j] - stack([_nn(w[j][h], s[h]) for h in range(nh)])
        o = _split_rows(stack([_nn(qn[j][h] * eg[j][h], s[h]) for h in range(nh)]) + _nn(attn[j], v_new_all))
        v_new = _split_rows(v_new_all)
        s = [s[h] * jnp.exp(glast[j][h]) + _tn(kdec[j][h], v_new[h]) for h in range(nh)]
        on.append([_rms(o[h], nw) * _silu(gz[j][h]) for h in range(nh)])
    return on, s, t


def _ssd_chunk(lanes, x, bm, cm, sm, alog_row, dtb_row, d_row, hs):
    subs, pairs = range(len(x)), range(len(x[0]))
    c = x[0][0].shape[0]
    lane_i = lax.broadcasted_iota(jnp.int32, (c, LANE), 1)
    lane_lo = lane_i < SSD_P
    lane_lo1 = lax.broadcasted_iota(jnp.int32, (1, LANE), 1) < SSD_P
    causal2 = lax.broadcasted_iota(jnp.int32, (c, LANE), 0) >= jnp.bitwise_and(lane_i, c - 1)
    row_lo = lax.broadcasted_iota(jnp.int32, (LANE, 1), 0) < SSD_P
    eye2 = lax.broadcasted_iota(jnp.int32, (LANE, LANE), 0) == lax.broadcasted_iota(jnp.int32, (LANE, LANE), 1)
    to_row = lambda cols: jnp.sum(jnp.where(eye2, jnp.concatenate(cols, axis=0), 0.0), axis=0, keepdims=True)
    per = lambda f: [[f(j, p) for p in pairs] for j in subs]
    both = lambda blk, p: [_lane_col(blk, lanes[p][h]) for h in range(2)]
    dt_blk = [_softplus(sm[j] + dtb_row) for j in subs]
    a_blk = [dt_blk[j] * -jnp.exp(alog_row) for j in subs]
    alast_blk = [jnp.sum(a_blk[j], axis=0, keepdims=True) for j in subs]
    dt = per(lambda j, p: both(dt_blk[j], p))
    a = per(lambda j, p: both(a_blk[j], p))
    alast = per(lambda j, p: both(alast_blk[j], p))
    dp = [both(d_row, p) for p in pairs]
    a_row = per(lambda j, p: to_row(a[j][p]))
    part = per(lambda j, p: jnp.where(causal2, a_row[j][p], 0.0))
    acum = per(lambda j, p: [jnp.sum(jnp.where(lane_lo, part[j][p], 0.0), axis=1, keepdims=True),
                             jnp.sum(jnp.where(lane_lo, 0.0, part[j][p]), axis=1, keepdims=True)])
    acum_row = per(lambda j, p: to_row(acum[j][p]))
    acum_col = per(lambda j, p: jnp.where(lane_lo, acum[j][p][0], acum[j][p][1]))
    lmat = per(lambda j, p: jnp.where(causal2, jnp.exp(jnp.where(causal2, acum_col[j][p] - acum_row[j][p], 0.0)), 0.0))
    cb = [_nt(cm[j], jnp.concatenate([bm[j], bm[j]], axis=0)) for j in subs]
    xdt = per(lambda j, p: x[j][p] * jnp.where(lane_lo, dt[j][p][0], dt[j][p][1]))
    xdt_rows = per(lambda j, p: jnp.concatenate([jnp.where(lane_lo, xdt[j][p], 0.0),
                                                 jnp.where(lane_lo, 0.0, xdt[j][p])], axis=0))
    intra = per(lambda j, p: _nn(cb[j] * lmat[j][p], xdt_rows[j][p]))
    skip = per(lambda j, p: x[j][p] * jnp.where(lane_lo1, dp[p][0], dp[p][1]))
    eacc = per(lambda j, p: jnp.exp(acum_col[j][p]))
    wdec = per(lambda j, p: jnp.exp(jnp.where(lane_lo, alast[j][p][0] - acum[j][p][0],
                                              alast[j][p][1] - acum[j][p][1])))
    scale = per(lambda j, p: jnp.exp(jnp.where(row_lo, alast[j][p][0], alast[j][p][1])))
    upd = per(lambda j, p: _tn(xdt[j][p] * wdec[j][p], bm[j]))
    y = []
    for j in subs:
        y.append([skip[j][p] + intra[j][p] + _nt(cm[j], hs[p]) * eacc[j][p] for p in pairs])
        hs = [hs[p] * scale[j][p] + upd[j][p] for p in pairs]
    return y, hs


def _ret_chunk(rq, rk, rv, rg, cos2, sin2, dmat, qdec, kdec, cdec, nw, r):
    subs, hs = range(len(rq)), range(len(rq[0]))
    per = lambda f: [[f(j, h) for h in hs] for j in subs]
    q = per(lambda j, h: rq[j][h] * cos2[j] + _swap_halves(rq[j][h]) * sin2[j])
    k = per(lambda j, h: (rk[j][h] * cos2[j] + _swap_halves(rk[j][h]) * sin2[j]) * (RET_DK ** -0.5))
    s = per(lambda j, h: _nt(q[j][h], k[j][h]) * dmat[h])
    intra = per(lambda j, h: _nn(s[j][h], rv[j][h]))
    upd = per(lambda j, h: _tn(k[j][h] * kdec[h], rv[j][h]))
    gate = per(lambda j, h: _silu(rg[j][h]))
    on = []
    for j in subs:
        on.append([_rms(intra[j][h] + _nn(q[j][h], r[h]) * qdec[h], nw) * gate[j][h] for h in hs])
        r = [r[h] * cdec[h] + upd[j][h] for h in hs]
    return on, r


MM_TILE = 1024


def _tile(n, pref=MM_TILE):
    return pref if n % pref == 0 else n


def _matmul(name, a, b, mode, m, n, k, tk=None, comm=None, out_dtype=F32):
    tm, tn = _tile(m), _tile(n)
    tk = _tile(k) if tk is None else tk
    nk = k // tk
    ca, cb = {"nn": (1, 0), "nt": (1, 1), "tn": (0, 0)}[mode]
    grid = (m // tm, n // tn, nk)
    n_comm = len(comm[1]) if comm else 0
    use_acc = out_dtype != F32 and nk > 1

    def at_step(which):
        return functools.reduce(jnp.logical_and, [pl.program_id(d) == (0 if which == 0 else grid[d] - 1)
                                                  for d in range(3)])

    def body(*refs):
        a_ref, b_ref = refs[:2]
        c_in = refs[2:2 + n_comm]
        o_ref = refs[2 + n_comm]
        c_out = refs[3 + n_comm:3 + 2 * n_comm]
        acc_ref = refs[3 + 2 * n_comm] if use_acc else o_ref
        sems = refs[3 + 2 * n_comm + use_acc:]
        if comm:
            @pl.when(at_step(0))
            def _():
                _comm_plan(comm[0], c_in, c_out, *sems)[0]()

        part = lax.dot_general(a_ref[...].astype(BF16), b_ref[...].astype(BF16),
                               (((ca,), (cb,)), ((), ())), preferred_element_type=F32)
        if nk == 1:
            o_ref[...] = part.astype(o_ref.dtype)
        else:
            kk = pl.program_id(2)

            @pl.when(kk == 0)
            def _():
                acc_ref[...] = part

            @pl.when(kk > 0)
            def _():
                acc_ref[...] += part

            if use_acc:
                @pl.when(kk == nk - 1)
                def _():
                    o_ref[...] = acc_ref[...].astype(o_ref.dtype)

        if comm:
            @pl.when(at_step(1))
            def _():
                _comm_plan(comm[0], c_in, c_out, *sems)[1]()

    a_spec = pl.BlockSpec((tk, tm), lambda i, j, kk: (kk, i)) if mode == "tn" else pl.BlockSpec((tm, tk), lambda i, j, kk: (i, kk))
    b_spec = pl.BlockSpec((tn, tk), lambda i, j, kk: (j, kk)) if mode == "nt" else pl.BlockSpec((tk, tn), lambda i, j, kk: (kk, j))
    hbm = pl.BlockSpec(memory_space=pl.ANY)
    res = pl.pallas_call(
        body, name=name, grid=grid,
        in_specs=[a_spec, b_spec] + [hbm] * n_comm,
        out_specs=[pl.BlockSpec((tm, tn), lambda i, j, kk: (i, j))] + [hbm] * n_comm,
        out_shape=[jax.ShapeDtypeStruct((m, n), out_dtype)] + (_comm_out_shapes(*comm) if comm else []),
        scratch_shapes=([pltpu.VMEM((tm, tn), F32)] if use_acc else []) + (_comm_scratch(n_comm) if comm else []),
        compiler_params=_cparams(("arbitrary",) * 3 if comm else ("parallel", "parallel", "arbitrary"), VMEM_LIMIT),
    )(a, b, *(comm[1] if comm else []))
    return res if comm else res[0]


def _rowmap(name, fn, length, tm, rows, params, out_ws=(), acc_ws=(), out_dtypes=None, cts=None, nd_rows=(),
            d_dtypes=None, place=None):
    nt_ = length // tm
    n_r, n_nd, n_p = len(rows), len(nd_rows), len(params)
    rspec = lambda bw, cbk: pl.BlockSpec((tm, bw), lambda i: (i, cbk))
    pspec = lambda w: pl.BlockSpec((1, w), lambda i: (0, 0))
    in_arrays = [r[0] for r in rows] + [r[0] for r in nd_rows] + list(params)
    in_specs = [rspec(r[1], r[2]) for r in rows] + [rspec(r[1], r[2]) for r in nd_rows] + [pspec(p.shape[1]) for p in params]
    place = place or {}

    def load(refs):
        return [r[...].astype(F32) for r in refs]

    def placed(widths, dtypes, n_before):
        specs, shapes, extra, alias = [], [], [], {}
        for k, (w, dt) in enumerate(zip(widths, dtypes)):
            arr, total, cbk = place.get(k, (None, w, 0))
            specs.append(rspec(w, cbk))
            shapes.append(jax.ShapeDtypeStruct((length, total), dt))
            if arr is not None:
                alias[n_before + len(extra)] = k
                extra.append(arr)
        return specs, shapes, extra, alias

    if cts is None:
        n_o, n_a = len(out_ws), len(acc_ws)
        out_dtypes_ = out_dtypes or [F32] * n_o

        o_specs, o_shapes, extra, alias = placed(out_ws, out_dtypes_, n_r + n_nd + n_p)

        def body(*refs):
            ins = load(refs[:n_r + n_nd + n_p])
            outs, accs = fn(*ins)
            o_refs = refs[n_r + n_nd + n_p + len(extra):]
            for o_ref, o in zip(o_refs[:n_o], outs):
                o_ref[...] = o.astype(o_ref.dtype)
            first = pl.program_id(0) == 0
            for a_ref, acc in zip(o_refs[n_o:], accs):
                @pl.when(first)
                def _(a_ref=a_ref):
                    a_ref[...] = jnp.zeros_like(a_ref)
                a_ref[...] += acc

        res = pl.pallas_call(
            body, name=name, grid=(nt_,), in_specs=in_specs + [pl.BlockSpec(memory_space=pl.ANY)] * len(extra),
            out_specs=o_specs + [pspec(w) for w in acc_ws],
            out_shape=o_shapes + [jax.ShapeDtypeStruct((1, w), F32) for w in acc_ws],
            input_output_aliases=alias,
            compiler_params=_cparams(("arbitrary",), VMEM_LIMIT),
        )(*in_arrays, *extra)
        return res[:n_o], res[n_o:]

    n_c = len(cts)
    o_specs, o_shapes, extra, alias = placed([r[1] for r in rows], d_dtypes or [F32] * n_r, n_r + n_nd + n_p + n_c)

    def body(*refs):
        ins = load(refs[:n_r + n_nd + n_p])
        ct_vals = load(refs[n_r + n_nd + n_p:n_r + n_nd + n_p + n_c])
        nd = ins[n_r:n_r + n_nd]
        f = lambda rs, ps: fn(*rs, *nd, *ps)[0]
        _, vjp = jax.vjp(f, ins[:n_r], ins[n_r + n_nd:])
        d_rows, d_params = vjp(tuple(ct_vals))
        o_refs = refs[n_r + n_nd + n_p + n_c + len(extra):]
        for o_ref, d in zip(o_refs[:n_r], d_rows):
            o_ref[...] = d.astype(o_ref.dtype)
        first = pl.program_id(0) == 0
        for a_ref, d in zip(o_refs[n_r:], d_params):
            @pl.when(first)
            def _(a_ref=a_ref):
                a_ref[...] = jnp.zeros_like(a_ref)
            a_ref[...] += d

    res = pl.pallas_call(
        body, name=name, grid=(nt_,),
        in_specs=in_specs + [rspec(c[1], c[2]) for c in cts] + [pl.BlockSpec(memory_space=pl.ANY)] * len(extra),
        out_specs=o_specs + [pspec(p.shape[1]) for p in params],
        out_shape=o_shapes + [jax.ShapeDtypeStruct((1, p.shape[1]), F32) for p in params],
        input_output_aliases=alias,
        compiler_params=_cparams(("arbitrary",), VMEM_LIMIT),
    )(*in_arrays, *[c[0] for c in cts], *extra)
    return res[:n_r], res[n_r:]


def _conv_shift(ext, k, tm, forward):
    s = CONV_W - 1 - k
    if forward:
        rolled = ext if s == 0 else pltpu.roll(ext, s, 0)
        return rolled[SUBLANE:, :]
    rolled = ext if s == 0 else pltpu.roll(ext, tm + SUBLANE - s, 0)
    return rolled[:tm, :]


def _conv_pre(x_ref, halo_ref, w_ref, b_ref, first, tm):
    halo = jnp.where(first, 0.0, halo_ref[...])
    ext = jnp.concatenate([halo, x_ref[...]], axis=0)
    w = w_ref[...]
    pre = b_ref[...] + jnp.zeros_like(x_ref[...])
    taps = []
    for k in range(CONV_W):
        tap = _conv_shift(ext, k, tm, True)
        taps.append(tap)
        pre = pre + tap * w[k:k + 1, :]
    return pre, taps


def _conv_fwd(proj, w, b, length, tm, tc):
    hb = tm // SUBLANE
    j0 = CONV_OFF // tc

    def body(x_ref, halo_ref, w_ref, b_ref, o_ref):
        pre, _ = _conv_pre(x_ref, halo_ref, w_ref, b_ref, pl.program_id(0) == 0, tm)
        o_ref[...] = _silu(pre)

    return pl.pallas_call(
        body, name="conv_fwd", grid=(length // tm, CONV_ALL // tc),
        in_specs=[pl.BlockSpec((tm, tc), lambda i, j: (i, j0 + j)),
                  pl.BlockSpec((SUBLANE, tc), lambda i, j: (jnp.maximum(i * hb - 1, 0), j0 + j)),
                  pl.BlockSpec((CONV_W, tc), lambda i, j: (0, j)),
                  pl.BlockSpec((1, tc), lambda i, j: (0, j))],
        out_specs=pl.BlockSpec((tm, tc), lambda i, j: (i, j)),
        out_shape=jax.ShapeDtypeStruct((length, CONV_ALL), F32),
        compiler_params=_cparams(("parallel", "parallel"), VMEM_LIMIT),
    )(proj, proj, w, b)


def _conv_bwd_pre(proj, w, b, dxc_a, dxc_b, length, tm, tc):
    hb = tm // SUBLANE
    j0 = CONV_OFF // tc
    jh = CONV_CH // tc

    def body(x_ref, halo_ref, w_ref, b_ref, dya_ref, dyb_ref, dpre_ref, dw_ref, db_ref):
        i = pl.program_id(1)
        pre, taps = _conv_pre(x_ref, halo_ref, w_ref, b_ref, i == 0, tm)
        sg = jax.nn.sigmoid(pre)
        dy = jnp.where(pl.program_id(0) < jh, dya_ref[...], dyb_ref[...])
        dpre = dy * (sg * (1.0 + pre * (1.0 - sg)))
        dpre_ref[...] = dpre

        @pl.when(i == 0)
        def _():
            dw_ref[...] = jnp.zeros_like(dw_ref)
            db_ref[...] = jnp.zeros_like(db_ref)

        for k in range(CONV_W):
            dw_ref[k:k + 1, :] += jnp.sum(dpre * taps[k], axis=0, keepdims=True)
        db_ref[...] += jnp.sum(dpre, axis=0, keepdims=True)

    return pl.pallas_call(
        body, name="conv_bwd_pre", grid=(CONV_ALL // tc, length // tm),
        in_specs=[pl.BlockSpec((tm, tc), lambda j, i: (i, j0 + j)),
                  pl.BlockSpec((SUBLANE, tc), lambda j, i: (jnp.maximum(i * hb - 1, 0), j0 + j)),
                  pl.BlockSpec((CONV_W, tc), lambda j, i: (0, j)),
                  pl.BlockSpec((1, tc), lambda j, i: (0, j)),
                  pl.BlockSpec((tm, tc), lambda j, i: (jnp.where(j < jh, i, 0), jnp.minimum(j, jh - 1))),
                  pl.BlockSpec((tm, tc), lambda j, i: (jnp.where(j < jh, 0, i), jnp.maximum(j - jh, 0)))],
        out_specs=[pl.BlockSpec((tm, tc), lambda j, i: (i, j)),
                   pl.BlockSpec((CONV_W, tc), lambda j, i: (0, j)),
                   pl.BlockSpec((1, tc), lambda j, i: (0, j))],
        out_shape=[jax.ShapeDtypeStruct((length, CONV_ALL), F32),
                   jax.ShapeDtypeStruct((CONV_W, CONV_ALL), F32),
                   jax.ShapeDtypeStruct((1, CONV_ALL), F32)],
        compiler_params=_cparams(("parallel", "arbitrary"), VMEM_LIMIT),
    )(proj, proj, w, b, dxc_a, dxc_b)


def _conv_bwd_x(dpre, w, dproj, length, tm, tc):
    hb = tm // SUBLANE
    n_t = length // tm
    last_blk = length // SUBLANE - 1
    j0 = CONV_OFF // tc

    def body(d_ref, halo_ref, w_ref, _, o_ref):
        halo = jnp.where(pl.program_id(0) == n_t - 1, 0.0, halo_ref[...])
        ext = jnp.concatenate([d_ref[...], halo], axis=0)
        w = w_ref[...]
        acc = jnp.zeros_like(d_ref[...])
        for k in range(CONV_W):
            acc = acc + _conv_shift(ext, k, tm, False) * w[k:k + 1, :]
        o_ref[...] = acc.astype(o_ref.dtype)

    return pl.pallas_call(
        body, name="conv_bwd_x", grid=(n_t, CONV_ALL // tc),
        in_specs=[pl.BlockSpec((tm, tc), lambda i, j: (i, j)),
                  pl.BlockSpec((SUBLANE, tc), lambda i, j: (jnp.minimum((i + 1) * hb, last_blk), j)),
                  pl.BlockSpec((CONV_W, tc), lambda i, j: (0, j)),
                  pl.BlockSpec(memory_space=pl.ANY)],
        out_specs=pl.BlockSpec((tm, tc), lambda i, j: (i, j0 + j)),
        out_shape=jax.ShapeDtypeStruct(dproj.shape, dproj.dtype), input_output_aliases={3: 0},
        compiler_params=_cparams(("parallel", "parallel"), VMEM_LIMIT),
    )(dpre, dpre, w, dproj)


GDN_CHUNKS_PER_STEP = 4


def _gdn_steps(length):
    nc = length // CHUNK
    gs = GDN_CHUNKS_PER_STEP if nc % GDN_CHUNKS_PER_STEP == 0 else 1
    return gs, nc // gs, gs * CHUNK


def _gdn_tiles(ref, gs):
    return [[ref[j * CHUNK:(j + 1) * CHUNK, h * LANE:(h + 1) * LANE] for h in range(GDN_HEADS)] for j in range(gs)]


def _chunk_rows(ref, gs):
    return [ref[j * CHUNK:(j + 1) * CHUNK, :] for j in range(gs)]


def _gdn_fwd(xc, proj, alog, dtb, nw, length, comm=None):
    gs, steps, rows = _gdn_steps(length)
    h_ = GDN_HEADS
    hw = h_ * LANE
    tn_ = h_ * CHUNK
    n_comm = len(comm[1]) if comm else 0

    def body(*refs):
        q_ref, k_ref, v_ref, gz_ref, sm_ref, al_ref, dt_ref, nw_ref = refs[:8]
        c_in = refs[8:8 + n_comm]
        on_ref, st_ref, t_ref = refs[8 + n_comm:11 + n_comm]
        c_out = refs[11 + n_comm:11 + 2 * n_comm]
        s_ref = refs[11 + 2 * n_comm]
        sems = refs[12 + 2 * n_comm:]

        @pl.when(pl.program_id(0) == 0)
        def _():
            s_ref[...] = jnp.zeros_like(s_ref)
            if comm:
                _comm_plan(comm[0], c_in, c_out, *sems)[0]()

        st_ref[...] = s_ref[...]
        lead = lambda ref: [ref[h] for h in range(h_)]
        on, s_new, t = _gdn_chunk(_gdn_tiles(q_ref, gs), _gdn_tiles(k_ref, gs), _gdn_tiles(v_ref, gs),
                                  _gdn_tiles(gz_ref, gs), _chunk_rows(sm_ref, gs), al_ref[...], dt_ref[...],
                                  nw_ref[...], lead(s_ref))
        for j in range(gs):
            t_ref[j] = t[j]
        for h in range(h_):
            for j in range(gs):
                on_ref[j * CHUNK:(j + 1) * CHUNK, h * LANE:(h + 1) * LANE] = on[j][h].astype(on_ref.dtype)
            s_ref[h] = s_new[h]

        if comm:
            @pl.when(pl.program_id(0) == steps - 1)
            def _():
                _comm_plan(comm[0], c_in, c_out, *sems)[1]()

    blk = lambda col: pl.BlockSpec((rows, hw), lambda c: (c, col // hw))
    one = pl.BlockSpec((1, LANE), lambda c: (0, 0))
    hbm = pl.BlockSpec(memory_space=pl.ANY)
    return pl.pallas_call(
        body, name="gdn_fwd_" + comm[0] if comm else "gdn_fwd", grid=(steps,),
        in_specs=[blk(XC("gq")), blk(XC("gk")), blk(XC("gv")), blk(_MINE["gz"]),
                  pl.BlockSpec((rows, LANE), lambda c: (c, SMALL_OFF // LANE)), one, one,
                  pl.BlockSpec((1, LANE), lambda c: (0, 0))] + [hbm] * n_comm,
        out_specs=[pl.BlockSpec((rows, hw), lambda c: (c, 0)),
                   pl.BlockSpec((h_, None, LANE, LANE), lambda c: (0, c, 0, 0)),
                   pl.BlockSpec((gs, tn_, tn_), lambda c: (c, 0, 0))] + [hbm] * n_comm,
        out_shape=[jax.ShapeDtypeStruct((length, MIX_W), BF16),
                   jax.ShapeDtypeStruct((h_, steps, LANE, LANE), F32),
                   jax.ShapeDtypeStruct((length // CHUNK, tn_, tn_), F32)]
                  + (_comm_out_shapes(*comm) if comm else []),
        scratch_shapes=[pltpu.VMEM((h_, LANE, LANE), F32)] + (_comm_scratch(n_comm) if comm else []),
        compiler_params=_cparams(("arbitrary",), VMEM_LIMIT),
    )(xc, xc, xc, proj, proj, alog, dtb, nw, *(comm[1] if comm else []))


def _gdn_bwd(xc, proj, alog, dtb, nw, st, tinv, dmix, length, comm=None):
    gs, steps, rows = _gdn_steps(length)
    h_ = GDN_HEADS
    hw = h_ * LANE
    tn_ = h_ * CHUNK
    n_comm = len(comm[1]) if comm else 0
    n_in, n_out = 11, 6

    def body(*refs):
        q_ref, k_ref, v_ref, gz_ref, sm_ref, al_ref, dt_ref, nw_ref, st_ref, t_ref, do_ref = refs[:n_in]
        c_in = refs[n_in:n_in + n_comm]
        o0 = n_in + n_comm
        dqkv_ref, dgz_ref, dsm_ref, dal_ref, ddt_ref, dnw_ref = refs[o0:o0 + n_out]
        c_out = refs[o0 + n_out:o0 + n_out + n_comm]
        ds_ref = refs[o0 + n_out + n_comm]
        sems = refs[o0 + n_out + n_comm + 1:]

        @pl.when(pl.program_id(0) == 0)
        def _():
            ds_ref[...] = jnp.zeros_like(ds_ref)
            dal_ref[...] = jnp.zeros_like(dal_ref)
            ddt_ref[...] = jnp.zeros_like(ddt_ref)
            dnw_ref[...] = jnp.zeros_like(dnw_ref)
            if comm:
                _comm_plan(comm[0], c_in, c_out, *sems)[0]()

        lead = lambda ref: [ref[h] for h in range(h_)]
        ins = (_gdn_tiles(q_ref, gs), _gdn_tiles(k_ref, gs), _gdn_tiles(v_ref, gs), _gdn_tiles(gz_ref, gs),
               _chunk_rows(sm_ref, gs), al_ref[...], dt_ref[...], nw_ref[...], lead(st_ref))
        t_saved = [t_ref[j] for j in range(gs)]
        _, vjp = jax.vjp(lambda *a: _gdn_chunk(*a, t_saved=t_saved)[:2], *ins)
        dq, dk, dv, dgz, dsm, dal, ddt, dnw, ds = vjp((_gdn_tiles(do_ref, gs), lead(ds_ref)))
        for j in range(gs):
            rws = slice(j * CHUNK, (j + 1) * CHUNK)
            for h in range(h_):
                cols = slice(h * LANE, (h + 1) * LANE)
                for part, d in enumerate((dq, dk, dv)):
                    dqkv_ref[rws, part * hw + h * LANE:part * hw + (h + 1) * LANE] = d[j][h]
                dgz_ref[rws, cols] = dgz[j][h].astype(dgz_ref.dtype)
            dsm_ref[rws, :] = dsm[j]
        for h in range(h_):
            ds_ref[h] = ds[h]
        dal_ref[...] += dal
        ddt_ref[...] += ddt
        dnw_ref[...] += dnw

        if comm:
            @pl.when(pl.program_id(0) == steps - 1)
            def _():
                _comm_plan(comm[0], c_in, c_out, *sems)[1]()

    r = lambda c: steps - 1 - c
    blk = lambda col: pl.BlockSpec((rows, hw), lambda c: (r(c), col // hw))
    one = pl.BlockSpec((1, LANE), lambda c: (0, 0))
    hbm = pl.BlockSpec(memory_space=pl.ANY)
    return pl.pallas_call(
        body, name="gdn_bwd_" + comm[0] if comm else "gdn_bwd", grid=(steps,),
        in_specs=[blk(XC("gq")), blk(XC("gk")), blk(XC("gv")), blk(_MINE["gz"]),
                  pl.BlockSpec((rows, LANE), lambda c: (r(c), SMALL_OFF // LANE)), one, one,
                  pl.BlockSpec((1, LANE), lambda c: (0, 0)),
                  pl.BlockSpec((h_, None, LANE, LANE), lambda c: (0, r(c), 0, 0)),
                  pl.BlockSpec((gs, tn_, tn_), lambda c: (r(c), 0, 0)),
                  blk(0)] + [hbm] * n_comm,
        out_specs=[pl.BlockSpec((rows, 3 * hw), lambda c: (r(c), 0)), blk(_MINE["gz"]),
                   pl.BlockSpec((rows, LANE), lambda c: (r(c), 0)), one, one,
                   pl.BlockSpec((1, LANE), lambda c: (0, 0))] + [hbm] * n_comm,
        out_shape=[jax.ShapeDtypeStruct((length, 3 * hw), F32), jax.ShapeDtypeStruct((length, N_PAD), BF16),
                   jax.ShapeDtypeStruct((length, LANE), F32),
                   jax.ShapeDtypeStruct((1, LANE), F32), jax.ShapeDtypeStruct((1, LANE), F32),
                   jax.ShapeDtypeStruct((1, LANE), F32)] + (_comm_out_shapes(*comm) if comm else []),
        scratch_shapes=[pltpu.VMEM((h_, LANE, LANE), F32)] + (_comm_scratch(n_comm) if comm else []),
        compiler_params=_cparams(("arbitrary",), VMEM_LIMIT),
    )(xc, xc, xc, proj, proj, alog, dtb, nw, st, tinv, dmix, *(comm[1] if comm else []))


SSD_CHUNKS_PER_STEP = 8


def _ssd_steps(length):
    nc = length // CHUNK
    gs = SSD_CHUNKS_PER_STEP if nc % SSD_CHUNKS_PER_STEP == 0 else 1
    return gs, nc // gs, gs * CHUNK


def _ssd_lanes(g):
    base = _MINE["sdt"] - SMALL_OFF
    return [[base + (g * PAIRS_PER_GROUP + p) * 2 + j for j in range(2)] for p in range(PAIRS_PER_GROUP)]


def _ssd_tiles(ref, gs):
    return [[ref[j * CHUNK:(j + 1) * CHUNK, p * LANE:(p + 1) * LANE] for p in range(PAIRS_PER_GROUP)]
            for j in range(gs)]


def _ssd_gate(y, z, w):
    return _rms(y * _silu(z), w)


def _ssd_fwd(xc, proj, alog_row, dtb_row, d_row, norm_w, mixed, length):
    gs, steps, rows = _ssd_steps(length)
    ppg = PAIRS_PER_GROUP
    gw = ppg * LANE

    def body(x_ref, b_ref, c_ref, sm_ref, al_ref, db_ref, dp_ref, z_ref, nw_ref, _, y_ref, o_ref, st_ref, hs_ref):
        @pl.when(pl.program_id(1) == 0)
        def _():
            hs_ref[...] = jnp.zeros_like(hs_ref)

        st_ref[...] = hs_ref[...]
        y, hs_new = _ssd_chunk(_ssd_lanes(pl.program_id(0)), _ssd_tiles(x_ref, gs), _chunk_rows(b_ref, gs),
                               _chunk_rows(c_ref, gs), _chunk_rows(sm_ref, gs), al_ref[...], db_ref[...],
                               dp_ref[...], [hs_ref[p] for p in range(ppg)])
        for p in range(ppg):
            hs_ref[p] = hs_new[p]
        for j in range(gs):
            rws = slice(j * CHUNK, (j + 1) * CHUNK)
            y_j = jnp.concatenate(y[j], axis=1)
            y_ref[rws, :] = y_j
            o_ref[rws, :] = _ssd_gate(y_j, z_ref[rws, :], nw_ref[...]).astype(o_ref.dtype)

    one = pl.BlockSpec((1, LANE), lambda g, c: (0, 0))
    return pl.pallas_call(
        body, name="ssd_fwd", grid=(SSD_GROUPS, steps),
        in_specs=[pl.BlockSpec((rows, gw), lambda g, c: (c, XC("sx") // gw + g)),
                  pl.BlockSpec((rows, LANE), lambda g, c: (c, XC("sB") // LANE + g)),
                  pl.BlockSpec((rows, LANE), lambda g, c: (c, XC("sC") // LANE + g)),
                  pl.BlockSpec((rows, LANE), lambda g, c: (c, SMALL_OFF // LANE)),
                  one, one, one,
                  pl.BlockSpec((rows, gw), lambda g, c: (c, _MINE["sz"] // gw + g)),
                  pl.BlockSpec((None, 1, gw), lambda g, c: (g, 0, 0)),
                  pl.BlockSpec(memory_space=pl.ANY)],
        out_specs=[pl.BlockSpec((rows, gw), lambda g, c: (c, g)),
                   pl.BlockSpec((rows, gw), lambda g, c: (c, 1 + g)),
                   pl.BlockSpec((None, ppg, None, LANE, SSD_N), lambda g, c: (g, 0, c, 0, 0))],
        out_shape=[jax.ShapeDtypeStruct((length, SSD_GROUPS * gw), F32),
                   jax.ShapeDtypeStruct(mixed.shape, mixed.dtype),
                   jax.ShapeDtypeStruct((SSD_GROUPS, ppg, steps, LANE, SSD_N), F32)],
        input_output_aliases={9: 1},
        scratch_shapes=[pltpu.VMEM((ppg, LANE, SSD_N), F32)],
        compiler_params=_cparams(("parallel", "arbitrary"), VMEM_LIMIT),
    )(xc, xc, xc, proj, alog_row, dtb_row, d_row, proj, norm_w, mixed)


def _ssd_bwd(xc, proj, alog_row, dtb_row, d_row, norm_w, st, y_b, dmix, dproj, length):
    gs, steps, rows = _ssd_steps(length)
    ppg = PAIRS_PER_GROUP
    gw = ppg * LANE

    def body(x_ref, b_ref, c_ref, sm_ref, al_ref, db_ref, dp_ref, z_ref, nw_ref, st_ref, y_ref, do_ref, _,
             dxc_ref, dz_ref, dsm_ref, dal_ref, ddb_ref, ddp_ref, dnw_ref, dhs_ref):
        g = pl.program_id(1)

        @pl.when(jnp.logical_and(pl.program_id(0) == 0, g == 0))
        def _():
            dhs_ref[...] = jnp.zeros_like(dhs_ref)
            dal_ref[...] = jnp.zeros_like(dal_ref)
            ddb_ref[...] = jnp.zeros_like(ddb_ref)
            ddp_ref[...] = jnp.zeros_like(ddp_ref)
            dnw_ref[...] = jnp.zeros_like(dnw_ref)

        pairs = range(ppg)
        dy, dnw = [], jnp.zeros((1, gw), F32)
        for j in range(gs):
            rws = slice(j * CHUNK, (j + 1) * CHUNK)
            _, gate_vjp = jax.vjp(_ssd_gate, y_ref[rws, :], z_ref[rws, :], nw_ref[...])
            dy_j, dz_j, dnw_j = gate_vjp(do_ref[rws, :].astype(F32))
            dz_ref[rws, :] = dz_j.astype(dz_ref.dtype)
            dy.append([dy_j[:, p * LANE:(p + 1) * LANE] for p in pairs])
            dnw = dnw + dnw_j
        dnw_ref[g] += dnw
        ins = (_ssd_tiles(x_ref, gs), _chunk_rows(b_ref, gs), _chunk_rows(c_ref, gs), _chunk_rows(sm_ref, gs),
               al_ref[...], db_ref[...], dp_ref[...], [st_ref[p] for p in pairs])
        _, vjp = jax.vjp(functools.partial(_ssd_chunk, _ssd_lanes(g)), *ins)
        dx, dbm, dcm, dsm, dal, ddb, ddp, dhs = vjp((dy, [dhs_ref[g, p] for p in pairs]))
        for grp in range(SSD_GROUPS):
            @pl.when(g == grp)
            def _(grp=grp):
                for j in range(gs):
                    rws = slice(j * CHUNK, (j + 1) * CHUNK)
                    for p in pairs:
                        col = grp * gw + p * LANE
                        dxc_ref[rws, col:col + LANE] = dx[j][p]
                    cb_ = XC("sB") - XC("sx") + grp * SSD_N
                    cc_ = XC("sC") - XC("sx") + grp * SSD_N
                    dxc_ref[rws, cb_:cb_ + SSD_N] = dbm[j]
                    dxc_ref[rws, cc_:cc_ + SSD_N] = dcm[j]
        for j in range(gs):
            dsm_ref[j * CHUNK:(j + 1) * CHUNK, :] = dsm[j]
        for p in pairs:
            dhs_ref[g, p] = dhs[p]
        dal_ref[g] += dal
        ddb_ref[g] += ddb
        ddp_ref[g] += ddp

    r = lambda c: steps - 1 - c
    one = pl.BlockSpec((1, LANE), lambda c, g: (0, 0))
    acc = pl.BlockSpec((SSD_GROUPS, 1, LANE), lambda c, g: (0, 0, 0))
    prm = jax.ShapeDtypeStruct((SSD_GROUPS, 1, LANE), F32)
    sz_blk = _MINE["sz"] // gw
    return pl.pallas_call(
        body, name="ssd_bwd", grid=(steps, SSD_GROUPS),
        in_specs=[pl.BlockSpec((rows, gw), lambda c, g: (r(c), XC("sx") // gw + g)),
                  pl.BlockSpec((rows, LANE), lambda c, g: (r(c), XC("sB") // LANE + g)),
                  pl.BlockSpec((rows, LANE), lambda c, g: (r(c), XC("sC") // LANE + g)),
                  pl.BlockSpec((rows, LANE), lambda c, g: (r(c), SMALL_OFF // LANE)),
                  one, one, one,
                  pl.BlockSpec((rows, gw), lambda c, g: (r(c), sz_blk + g)),
                  pl.BlockSpec((None, 1, gw), lambda c, g: (g, 0, 0)),
                  pl.BlockSpec((None, ppg, None, LANE, SSD_N), lambda c, g: (g, 0, r(c), 0, 0)),
                  pl.BlockSpec((rows, gw), lambda c, g: (r(c), g)),
                  pl.BlockSpec((rows, gw), lambda c, g: (r(c), 1 + g)),
                  pl.BlockSpec(memory_space=pl.ANY)],
        out_specs=[pl.BlockSpec((rows, CONV_CH), lambda c, g: (r(c), 0)),
                   pl.BlockSpec((rows, gw), lambda c, g: (r(c), sz_blk + g)),
                   pl.BlockSpec((None, rows, LANE), lambda c, g: (g, r(c), 0)), acc, acc, acc,
                   pl.BlockSpec((SSD_GROUPS, 1, gw), lambda c, g: (0, 0, 0))],
        out_shape=[jax.ShapeDtypeStruct((length, CONV_CH), F32), jax.ShapeDtypeStruct(dproj.shape, dproj.dtype),
                   jax.ShapeDtypeStruct((SSD_GROUPS, length, LANE), F32), prm, prm, prm,
                   jax.ShapeDtypeStruct((SSD_GROUPS, 1, gw), F32)],
        input_output_aliases={12: 1},
        scratch_shapes=[pltpu.VMEM((SSD_GROUPS, ppg, LANE, SSD_N), F32)],
        compiler_params=_cparams(("arbitrary", "arbitrary"), VMEM_LIMIT),
    )(xc, xc, xc, proj, alog_row, dtb_row, d_row, proj, norm_w, st, y_b, dmix, dproj)


def _ret_fwd(proj, cos2, sin2, dmat, qdec, kdec, cdec, nw, mixed, length):
    gs, steps, rows = _gdn_steps(length)
    h_ = RET_HEADS

    def body(q_ref, k_ref, v_ref, g_ref, cos_ref, sin_ref, dm_ref, qd_ref, kd_ref, cd_ref, nw_ref, _,
             on_ref, st_ref, r_ref):
        @pl.when(pl.program_id(0) == 0)
        def _():
            r_ref[...] = jnp.zeros_like(r_ref)

        st_ref[...] = r_ref[...]
        lead = lambda ref: [ref[h] for h in range(h_)]
        on, r_new = _ret_chunk(_gdn_tiles(q_ref, gs), _gdn_tiles(k_ref, gs), _gdn_tiles(v_ref, gs),
                               _gdn_tiles(g_ref, gs), _chunk_rows(cos_ref, gs), _chunk_rows(sin_ref, gs),
                               lead(dm_ref), lead(qd_ref), lead(kd_ref), lead(cd_ref), nw_ref[...], lead(r_ref))
        for h in range(h_):
            for j in range(gs):
                on_ref[j * CHUNK:(j + 1) * CHUNK, h * LANE:(h + 1) * LANE] = on[j][h].astype(on_ref.dtype)
            r_ref[h] = r_new[h]

    hw = h_ * LANE
    blk = lambda name: pl.BlockSpec((rows, hw), lambda c: (c, _MINE[name] // hw))
    tab = pl.BlockSpec((rows, LANE), lambda c: (c, 0))
    full = lambda *s: pl.BlockSpec(s, lambda c: (0,) * len(s))
    return pl.pallas_call(
        body, name="ret_fwd", grid=(steps,),
        in_specs=[blk("rq"), blk("rk"), blk("rv"), blk("rg"), tab, tab,
                  full(h_, CHUNK, CHUNK), full(h_, CHUNK, 1), full(h_, CHUNK, 1), full(h_, 1, 1), full(1, LANE),
                  pl.BlockSpec(memory_space=pl.ANY)],
        out_specs=[pl.BlockSpec((rows, hw), lambda c: (c, (MIX_W - hw) // hw)),
                   pl.BlockSpec((h_, None, LANE, LANE), lambda c: (0, c, 0, 0))],
        out_shape=[jax.ShapeDtypeStruct(mixed.shape, mixed.dtype),
                   jax.ShapeDtypeStruct((h_, steps, LANE, LANE), F32)],
        input_output_aliases={11: 0},
        scratch_shapes=[pltpu.VMEM((h_, LANE, LANE), F32)],
        compiler_params=_cparams(("arbitrary",), VMEM_LIMIT),
    )(proj, proj, proj, proj, cos2, sin2, dmat, qdec, kdec, cdec, nw, mixed)


def _ret_bwd(proj, cos2, sin2, dmat, qdec, kdec, cdec, nw, st, dmix, dproj, length):
    gs, steps, rows = _gdn_steps(length)
    h_ = RET_HEADS
    hw = h_ * LANE

    def body(q_ref, k_ref, v_ref, g_ref, cos_ref, sin_ref, dm_ref, qd_ref, kd_ref, cd_ref, nw_ref, st_ref,
             do_ref, _, dqkvg_ref, dnw_ref, dr_ref):
        @pl.when(pl.program_id(0) == 0)
        def _():
            dr_ref[...] = jnp.zeros_like(dr_ref)
            dnw_ref[...] = jnp.zeros_like(dnw_ref)

        lead = lambda ref: [ref[h] for h in range(h_)]
        consts = (_chunk_rows(cos_ref, gs), _chunk_rows(sin_ref, gs), lead(dm_ref), lead(qd_ref), lead(kd_ref),
                  lead(cd_ref))
        f = lambda q, k, v, g, w_, r_: _ret_chunk(q, k, v, g, *consts, w_, r_)
        _, vjp = jax.vjp(f, _gdn_tiles(q_ref, gs), _gdn_tiles(k_ref, gs), _gdn_tiles(v_ref, gs),
                         _gdn_tiles(g_ref, gs), nw_ref[...], lead(st_ref))
        dq, dk, dv, dg, dnw, dr = vjp((_gdn_tiles(do_ref, gs), lead(dr_ref)))
        for h in range(h_):
            for j in range(gs):
                rws = slice(j * CHUNK, (j + 1) * CHUNK)
                for part, d in enumerate((dq, dk, dv, dg)):
                    col = part * hw + h * LANE
                    dqkvg_ref[rws, col:col + LANE] = d[j][h].astype(dqkvg_ref.dtype)
            dr_ref[h] = dr[h]
        dnw_ref[...] += dnw

    r = lambda c: steps - 1 - c
    blk = lambda name: pl.BlockSpec((rows, hw), lambda c: (r(c), _MINE[name] // hw))
    tab = pl.BlockSpec((rows, LANE), lambda c: (r(c), 0))
    full = lambda *s: pl.BlockSpec(s, lambda c: (0,) * len(s))
    assert _MINE["rq"] % (4 * hw) == 0 and [_MINE[n] - _MINE["rq"] for n in ("rk", "rv", "rg")] == [hw, 2 * hw, 3 * hw]
    return pl.pallas_call(
        body, name="ret_bwd", grid=(steps,),
        in_specs=[blk("rq"), blk("rk"), blk("rv"), blk("rg"), tab, tab,
                  full(h_, CHUNK, CHUNK), full(h_, CHUNK, 1), full(h_, CHUNK, 1), full(h_, 1, 1), full(1, LANE),
                  pl.BlockSpec((h_, None, LANE, LANE), lambda c: (0, r(c), 0, 0)),
                  pl.BlockSpec((rows, hw), lambda c: (r(c), (MIX_W - hw) // hw)),
                  pl.BlockSpec(memory_space=pl.ANY)],
        out_specs=[pl.BlockSpec((rows, 4 * hw), lambda c: (r(c), _MINE["rq"] // (4 * hw))), full(1, LANE)],
        out_shape=[jax.ShapeDtypeStruct(dproj.shape, dproj.dtype), jax.ShapeDtypeStruct((1, LANE), F32)],
        input_output_aliases={13: 0},
        scratch_shapes=[pltpu.VMEM((h_, LANE, LANE), F32)],
        compiler_params=_cparams(("arbitrary",), VMEM_LIMIT),
    )(proj, proj, proj, proj, cos2, sin2, dmat, qdec, kdec, cdec, nw, st, dmix, dproj)


def _mesh_pos():
    x, y, c = lax.axis_index("x"), lax.axis_index("y"), lax.axis_index("c")
    return x, y, c, 4 * x + 2 * y + c


def _peer(x, y, c, mask):
    return (x ^ ((mask >> 2) & 1), y ^ ((mask >> 1) & 1), c ^ (mask & 1))


def _gather_plan(ins, outs, send_sems, recv_sems, local_sems):
    x, y, c, me = _mesh_pos()
    n = len(ins)
    idx = lambda px, py, pc: 4 * px + 2 * py + pc
    sib = (x, y, 1 - c)
    chips = [(1 - x, y), (x, 1 - y), (1 - x, 1 - y)]
    local = [pltpu.make_async_copy(ins[a], outs[a].at[me], local_sems.at[a]) for a in range(n)]

    def copy(a, k, src, slab, to):
        return pltpu.make_async_remote_copy(
            src_ref=src, dst_ref=outs[a].at[slab], send_sem=send_sems.at[7 * a + k],
            recv_sem=recv_sems.at[7 * a + k], device_id=to, device_id_type=pl.DeviceIdType.MESH)

    first = [copy(a, 0, ins[a], me, sib) for a in range(n)]
    first += [copy(a, 1 + j, ins[a], me, (*chip, c)) for a in range(n) for j, chip in enumerate(chips)]
    landed = [[copy(a, 1 + j, ins[a], idx(*chip, c), (*chip, c)) for a in range(n)] for j, chip in enumerate(chips)]
    passed = [[copy(a, 4 + j, outs[a].at[idx(*chip, c)], idx(*chip, c), sib) for a in range(n)]
              for j, chip in enumerate(chips)]
    from_sib = [copy(a, 0, ins[a], idx(*sib), sib) for a in range(n)]
    from_sib += [copy(a, 4 + j, ins[a], idx(*chip, 1 - c), sib) for a in range(n) for j, chip in enumerate(chips)]

    def start():
        for cp in local + first:
            cp.start()

    def wait():
        for j in range(len(chips)):
            for a in range(n):
                landed[j][a].wait_recv()
                passed[j][a].start()
        for cp in from_sib:
            cp.wait_recv()
        for cp in first + [cp for row in passed for cp in row]:
            cp.wait_send()
        for cp in local:
            cp.wait()

    return start, wait


def _comm_plan(kind, ins, outs, send_sems, recv_sems, local_sems):
    if kind == "gather":
        return _gather_plan(ins, outs, send_sems, recv_sems, local_sems)
    x, y, c, me = _mesh_pos()
    n = len(ins)
    src = (lambda a, idx: ins[a]) if kind == "gather" else (lambda a, idx: ins[a].at[idx])
    local = [pltpu.make_async_copy(src(a, me), outs[a].at[me], local_sems.at[a]) for a in range(n)]
    sends, recvs = [], []
    for mask in range(1, N_DEV):
        px, py, pc = _peer(x, y, c, mask)
        pidx = 4 * px + 2 * py + pc
        for a in range(n):
            k = (mask - 1) * n + a
            copy = lambda s, d, k=k: pltpu.make_async_remote_copy(
                src_ref=s, dst_ref=d, send_sem=send_sems.at[k], recv_sem=recv_sems.at[k],
                device_id=(px, py, pc), device_id_type=pl.DeviceIdType.MESH)
            sends.append(copy(src(a, pidx), outs[a].at[me]))
            recvs.append(copy(src(a, me), outs[a].at[pidx]))

    def start():
        for cp in local + sends:
            cp.start()

    def wait():
        for cp in sends:
            cp.wait_send()
        for cp in recvs:
            cp.wait_recv()
        for cp in local:
            cp.wait()

    return start, wait


def _comm_out_shapes(kind, arrays):
    return [jax.ShapeDtypeStruct(((N_DEV,) + s.shape) if kind == "gather" else s.shape, s.dtype) for s in arrays]


def _comm_scratch(n):
    k_tot = (N_DEV - 1) * n
    return [pltpu.SemaphoreType.DMA((k_tot,)), pltpu.SemaphoreType.DMA((k_tot,)), pltpu.SemaphoreType.DMA((n,))]


def _comm_call(name, kind, arrays):
    n = len(arrays)

    def body(*refs):
        start, wait = _comm_plan(kind, refs[:n], refs[n:2 * n], *refs[2 * n:])
        start()
        wait()

    hbm = pl.BlockSpec(memory_space=pl.ANY)
    return pl.pallas_call(
        body, name=name, in_specs=[hbm] * n, out_specs=[hbm] * n, out_shape=_comm_out_shapes(kind, arrays),
        scratch_shapes=_comm_scratch(n), compiler_params=pltpu.CompilerParams(has_side_effects=True),
    )(*arrays)


def _adam(name, parts, w, m, v, tr, layer=None, prev=None):
    p_, r_, cw = parts.shape
    c1 = 1.0 - ADAM_B1 ** ADAM_STEP
    c2 = 1.0 - ADAM_B2 ** ADAM_STEP
    n_prev = 4 if prev else 0

    def body(p_ref, w_ref, m_ref, v_ref, *rest):
        g_ref, d_ref, nm_ref, nv_ref = rest[n_prev:]
        g = p_ref[0].astype(F32)
        for i in range(1, p_):
            g = g + p_ref[i].astype(F32)
        nm = ADAM_B1 * m_ref[...] + (1.0 - ADAM_B1) * g
        nv = ADAM_B2 * v_ref[...] + (1.0 - ADAM_B2) * (g * g)
        d_ref[...] = -ADAM_LR * ((nm / c1) / (jnp.sqrt(nv / c2) + ADAM_EPS) + ADAM_WD * w_ref[...])
        g_ref[...] = g
        nm_ref[...] = nm
        nv_ref[...] = nv

    if layer is None:
        spec = pl.BlockSpec((tr, cw), lambda i: (i, 0))
        shp = jax.ShapeDtypeStruct((r_, cw), F32)
    else:
        spec = pl.BlockSpec((None, tr, cw), lambda i: (layer, i, 0))
        shp = jax.ShapeDtypeStruct((DEPTH, r_, cw), F32)
    return pl.pallas_call(
        body, name=name, grid=(r_ // tr,),
        in_specs=[pl.BlockSpec((p_, tr, cw), lambda i: (0, i, 0)), spec, spec, spec]
                 + [pl.BlockSpec(memory_space=pl.ANY)] * n_prev,
        out_specs=[spec] * 4, out_shape=[shp] * 4,
        input_output_aliases={4 + i: i for i in range(n_prev)},
        compiler_params=_cparams(("parallel",), VMEM_LIMIT),
    )(parts, w, m, v, *(prev or ()))


def _perm_cols(w_full):
    parts = [w_full[:, _ORIG[n][0]:_ORIG[n][0] + _ORIG[n][1]] for n in _ORDER]
    parts.append(jnp.zeros((w_full.shape[0], N_PAD - N_USED), w_full.dtype))
    return jnp.concatenate(parts, axis=1)


def _unperm_cols(g):
    by_orig = sorted(_ORDER, key=lambda n: _ORIG[n][0])
    return jnp.concatenate([g[:, _MINE[n]:_MINE[n] + _ORIG[n][1]] for n in by_orig], axis=1)


def _ret_consts(length):
    lg = jnp.log(1.0 - 2.0 ** (-5.0 - jnp.arange(RET_HEADS, dtype=F32)))
    idx = jnp.arange(CHUNK, dtype=F32)
    rel = idx[:, None] - idx[None, :]
    dmat = jnp.where(rel[None] >= 0, jnp.exp(jnp.maximum(rel, 0.0)[None] * lg[:, None, None]), 0.0)
    qdec = jnp.exp((idx[:, None] + 1.0) * lg[None, :]).T[:, :, None]
    kdec = jnp.exp((CHUNK - 1.0 - idx)[:, None] * lg[None, :]).T[:, :, None]
    cdec = jnp.exp(CHUNK * lg)[:, None, None]
    half = RET_DK // 2
    inv = ROPE_BASE ** (-jnp.arange(half, dtype=F32) / half)
    ang = jnp.arange(length, dtype=jnp.int32).astype(F32)[:, None] * inv[None, :]
    cos, sin = jnp.cos(ang), jnp.sin(ang)
    return (jnp.concatenate([cos, cos], axis=1), jnp.concatenate([-sin, sin], axis=1), dmat, qdec, kdec, cdec)


def _tiles(length, pref=512):
    return pref if length % pref == 0 else length


def _layer_fwd(x, p, consts, length, last, target, comm=None):
    tm = _tiles(length)
    (h,), _ = _rowmap("pre_norm", lambda x_, w_: ((_rms(x_, w_),), ()), length, tm,
                      [(x, D_MODEL, 0)], [p["pre_norm"]], out_ws=[D_MODEL], out_dtypes=[BF16])
    proj = _matmul("in_proj", h, p["w_in"], "nn", length, N_PAD, D_MODEL)
    xc = _conv_fwd(proj, p["conv_w"], p["conv_b"], length, _tiles(length, 1024), 512)
    mixed, st_a, tinv, *got = _gdn_fwd(xc, proj, p["gdn_A_log"], p["gdn_dt_bias"], p["gdn_norm"], length, comm)
    y_b, mixed, st_b = _ssd_fwd(xc, proj, p["ssd_A_log"], p["ssd_dt_bias"], p["ssd_D"], p["ssd_norm"], mixed, length)
    mixed, st_c = _ret_fwd(proj, *consts, p["ret_norm"], mixed, length)
    w_out = p["w_out"](got) if callable(p["w_out"]) else p["w_out"]
    out = _matmul("out_proj", mixed, w_out, "nn", length, D_MODEL, MIX_W)
    res = dict(x=x, h=h, proj=proj, xc=xc, st_a=st_a, tinv=tinv, st_b=st_b, st_c=st_c, y_b=y_b, mixed=mixed,
               out=out, w_out=w_out)
    if not last:
        (y,), _ = _rowmap("post_norm", lambda x_, o_, w_: ((x_ + _rms(o_, w_),), ()), length, tm,
                          [(x, D_MODEL, 0), (out, D_MODEL, 0)], [p["post_norm"]], out_ws=[D_MODEL])
        return y, res, None, got

    def head(x_, o_, t_, w_):
        e = x_ + _rms(o_, w_) - t_
        row = jnp.mean(e * e, axis=-1, keepdims=True)
        loss = 0.5 * jnp.sum(row, axis=0, keepdims=True)
        return (e * (1.0 / D_MODEL),), (loss + jnp.zeros((1, LANE), F32),)

    (dy,), (loss,) = _rowmap("loss_head", head, length, tm,
                             [(x, D_MODEL, 0), (out, D_MODEL, 0), (target, D_MODEL, 0)], [p["post_norm"]],
                             out_ws=[D_MODEL], acc_ws=[LANE])
    return None, res, (dy, loss[0, 0]), got


def _layer_bwd(dy, p, res, consts, length, comm_scan=None, comm_dx=None):
    tm = _tiles(length)
    g = {}
    (d_out,), (g["post_norm"],) = _rowmap("post_norm_bwd", lambda o_, w_: ((_rms(o_, w_),), ()), length, tm,
                                          [(res["out"], D_MODEL, 0)], [p["post_norm"]], cts=[(dy, D_MODEL, 0)],
                                          d_dtypes=[BF16])
    dmix = _matmul("out_proj_dx", d_out, res["w_out"], "nt", length, MIX_W, D_MODEL)
    g["w_out"] = _matmul("out_proj_dw", res["mixed"], d_out, "tn", MIX_W, D_MODEL, length, out_dtype=BF16)
    comm = comm_scan(g) if comm_scan else None
    proj, xc = res["proj"], res["xc"]
    (dxc_a, dproj, dsm_a, g["gdn_A_log"], g["gdn_dt_bias"], g["gdn_norm"], *got) = _gdn_bwd(
        xc, proj, p["gdn_A_log"], p["gdn_dt_bias"], p["gdn_norm"], res["st_a"], res["tinv"], dmix, length, comm)
    dxc_b, dproj, dsm_b, g["ssd_A_log"], g["ssd_dt_bias"], g["ssd_D"], g["ssd_norm"] = _ssd_bwd(
        xc, proj, p["ssd_A_log"], p["ssd_dt_bias"], p["ssd_D"], p["ssd_norm"], res["st_b"], res["y_b"], dmix,
        dproj, length)
    dproj, g["ret_norm"] = _ret_bwd(proj, *consts, p["ret_norm"], res["st_c"], dmix, dproj, length)
    tmc = _tiles(length, 1024)
    dpre, g["conv_w"], g["conv_b"] = _conv_bwd_pre(proj, p["conv_w"], p["conv_b"], dxc_a, dxc_b, length, tmc, 512)
    dproj = _conv_bwd_x(dpre, p["conv_w"], dproj, length, tmc, 512)
    pad = SMALL_OFF % 512 + 512 - LANE
    (dproj,), _ = _rowmap("dsmall", lambda a_, b_, c_: ((jnp.concatenate([a_ + b_ + c_, jnp.zeros((a_.shape[0], pad), F32)], axis=1),), ()),
                          length, tm, [(dsm_a, LANE, 0), (dsm_b[0], LANE, 0), (dsm_b[1], LANE, 0)], [],
                          out_ws=[512], out_dtypes=[BF16], place={0: (dproj, N_PAD, SMALL_OFF // 512)})
    g["w_in"] = _matmul("in_proj_dw", res["h"], dproj, "tn", D_MODEL, N_PAD, length, out_dtype=BF16)
    if comm_dx:
        dh, *got_dx = _matmul("in_proj_dx_scatter", dproj, p["w_in"], "nt", length, D_MODEL, N_PAD, comm=comm_dx(g))
    else:
        dh, got_dx = _matmul("in_proj_dx", dproj, p["w_in"], "nt", length, D_MODEL, N_PAD), []
    (dx,), (g["pre_norm"],) = _rowmap(
        "pre_norm_bwd", lambda x_, w_: ((_rms(x_, w_), x_), ()), length, tm,
        [(res["x"], D_MODEL, 0)], [p["pre_norm"]], cts=[(dh, D_MODEL, 0), (dy, D_MODEL, 0)])
    return dx, g, got, got_dx


def _lane_row(vals, piece):
    lo = _MINE[piece] - SMALL_OFF
    return jnp.pad(vals[None], ((0, 0), (lo, LANE - lo - vals.shape[0])))


def _from_lane_row(row, piece, n):
    lo = _MINE[piece] - SMALL_OFF
    return row.reshape(-1, LANE).sum(axis=0)[lo:lo + n]


def _make_layer(l, w_in_full, w_out_full, conv_full, sw):
    return dict(
        pre_norm=sw["pre_norm"][l][None], post_norm=sw["post_norm"][l][None], w_in=_perm_cols(w_in_full),
        w_out=w_out_full,
        conv_w=jnp.concatenate([conv_full[0], conv_full[1]], axis=1),
        conv_b=jnp.concatenate([jnp.zeros((CONV_CH,), F32), sw["ssd_conv_b"][l]])[None],
        gdn_A_log=_lane_row(sw["gdn_A_log"][l], "ga"), gdn_dt_bias=_lane_row(sw["gdn_dt_bias"][l], "ga"),
        gdn_norm=sw["gdn_norm"][l][None],
        ssd_A_log=_lane_row(sw["ssd_A_log"][l], "sdt"), ssd_dt_bias=_lane_row(sw["ssd_dt_bias"][l], "sdt"),
        ssd_D=_lane_row(sw["ssd_D"][l], "sdt"),
        ssd_norm=sw["ssd_norm"][l].reshape(SSD_GROUPS, 1, -1),
        ret_norm=sw["ret_norm"][l][None])


def _local_step(xb, tgt, layer0, make_layer1, length, fwd_comm=None, comm_scan=None, comm_dx=None):
    consts = _ret_consts(length)
    y0, res0, _, got = _layer_fwd(xb, layer0, consts, length, False, tgt, fwd_comm)
    layer1 = make_layer1(got)
    _, res1, (dy, loss_local), _ = _layer_fwd(y0, layer1, consts, length, True, tgt)
    dy, grads1, _, _ = _layer_bwd(dy, layer1, res1, consts, length)
    dx, grads0, recv_scan, recv_dx = _layer_bwd(
        dy, layer0, res0, consts, length,
        (lambda g0: comm_scan(grads1, g0)) if comm_scan else None, comm_dx)
    return loss_local, dx, [grads0, grads1], recv_scan, recv_dx


_SMALL = ["pre_norm", "post_norm", "gdn_A_log", "gdn_dt_bias", "gdn_norm", "ssd_conv_b", "ssd_A_log",
          "ssd_dt_bias", "ssd_D", "ssd_norm", "ret_norm"]


def _pack_small(arrs):
    rows = []
    for n in _SMALL:
        flat = arrs[n].reshape(-1)
        pad = (-flat.shape[0]) % LANE
        rows.append(jnp.pad(flat, (0, pad)).reshape(-1, LANE))
    out = jnp.concatenate(rows, axis=0)
    return jnp.pad(out, ((0, (-out.shape[0]) % SUBLANE), (0, 0)))


def _unpack_small(packed, like):
    out, r = {}, 0
    for n in _SMALL:
        cnt = like[n].size
        nrow = -(-cnt // LANE)
        out[n] = packed[r:r + nrow].reshape(-1)[:cnt].reshape(like[n].shape)
        r += nrow
    return out


def kernel(x, pre_norm, post_norm, w_in, gdn_conv, gdn_A_log, gdn_dt_bias, gdn_norm, ssd_conv, ssd_conv_b, ssd_A_log, ssd_dt_bias, ssd_D, ssd_norm, ret_norm, w_out, loss_target, m_pre_norm, m_post_norm, m_w_in, m_gdn_conv, m_gdn_A_log, m_gdn_dt_bias, m_gdn_norm, m_ssd_conv, m_ssd_conv_b, m_ssd_A_log, m_ssd_dt_bias, m_ssd_D, m_ssd_norm, m_ret_norm, m_w_out, v_pre_norm, v_post_norm, v_w_in, v_gdn_conv, v_gdn_A_log, v_gdn_dt_bias, v_gdn_norm, v_ssd_conv, v_ssd_conv_b, v_ssd_A_log, v_ssd_dt_bias, v_ssd_D, v_ssd_norm, v_ret_norm, v_w_out):
    length = x.shape[1]
    xb = x[0]
    tgt = loss_target[0]
    small_w = dict(pre_norm=pre_norm, post_norm=post_norm, gdn_A_log=gdn_A_log, gdn_dt_bias=gdn_dt_bias,
                   gdn_norm=gdn_norm, ssd_conv_b=ssd_conv_b, ssd_A_log=ssd_A_log, ssd_dt_bias=ssd_dt_bias,
                   ssd_D=ssd_D, ssd_norm=ssd_norm, ret_norm=ret_norm)
    small_m = dict(pre_norm=m_pre_norm, post_norm=m_post_norm, gdn_A_log=m_gdn_A_log, gdn_dt_bias=m_gdn_dt_bias,
                   gdn_norm=m_gdn_norm, ssd_conv_b=m_ssd_conv_b, ssd_A_log=m_ssd_A_log, ssd_dt_bias=m_ssd_dt_bias,
                   ssd_D=m_ssd_D, ssd_norm=m_ssd_norm, ret_norm=m_ret_norm)
    small_v = dict(pre_norm=v_pre_norm, post_norm=v_post_norm, gdn_A_log=v_gdn_A_log, gdn_dt_bias=v_gdn_dt_bias,
                   gdn_norm=v_gdn_norm, ssd_conv_b=v_ssd_conv_b, ssd_A_log=v_ssd_A_log, ssd_dt_bias=v_ssd_dt_bias,
                   ssd_D=v_ssd_D, ssd_norm=v_ssd_norm, ret_norm=v_ret_norm)

    w_in_b, w_out_b = w_in.astype(BF16), w_out.astype(BF16)
    conv_shard = jnp.stack([gdn_conv, ssd_conv], axis=1)
    full_out = lambda g_out: g_out.reshape(MIX_W, D_MODEL)

    def assemble(l, g_in, w_out_full, g_conv):
        w_in_full = g_in.transpose(1, 0, 2).reshape(D_MODEL, N_IN)
        conv_full = g_conv.transpose(1, 2, 0, 3).reshape(2, CONV_W, CONV_CH)
        return _make_layer(l, w_in_full, w_out_full, conv_full, small_w)

    slab_in = lambda g: _unperm_cols(g["w_in"]).reshape(D_MODEL, N_DEV, SHARD_IN).transpose(1, 0, 2).astype(BF16)
    slab_out = lambda g: g["w_out"].reshape(N_DEV, SHARD_OUT, D_MODEL).astype(BF16)
    slab_conv = lambda g: jnp.stack(
        [g["conv_w"][:, k * CONV_CH:(k + 1) * CONV_CH].reshape(CONV_W, N_DEV, SHARD_CONV).transpose(1, 0, 2)
         for k in range(2)], axis=1).reshape(N_DEV, 2 * CONV_W, SHARD_CONV)

    g_in0, g_conv0 = _comm_call("gather_layer0", "gather", [w_in_b[0], conv_shard[0]])
    layer0 = assemble(0, g_in0, lambda got: full_out(got[3]), g_conv0)
    loss_local, dx, grads, recv_scan, recv_dx = _local_step(
        xb, tgt, layer0, lambda got: assemble(1, got[0], full_out(got[1]), got[2]), length,
        fwd_comm=("gather", [w_in_b[1], w_out_b[1], conv_shard[1], w_out_b[0]]),
        comm_scan=lambda g1, g0: ("scatter", [slab_in(g1), slab_out(g1), slab_conv(g1), slab_out(g0)]),
        comm_dx=lambda g0: ("scatter", [slab_in(g0), slab_conv(g0)]))
    grad_x = dx[None]
    loss = lax.psum(loss_local, ("x", "y", "c"))

    small_g = dict(
        pre_norm=jnp.concatenate([grads[l]["pre_norm"] for l in range(DEPTH)], axis=0),
        post_norm=jnp.concatenate([grads[l]["post_norm"] for l in range(DEPTH)], axis=0),
        gdn_A_log=jnp.stack([_from_lane_row(grads[l]["gdn_A_log"], "ga", GDN_HEADS) for l in range(DEPTH)]),
        gdn_dt_bias=jnp.stack([_from_lane_row(grads[l]["gdn_dt_bias"], "ga", GDN_HEADS) for l in range(DEPTH)]),
        gdn_norm=jnp.concatenate([grads[l]["gdn_norm"] for l in range(DEPTH)], axis=0),
        ssd_conv_b=jnp.concatenate([grads[l]["conv_b"][:, CONV_CH:] for l in range(DEPTH)], axis=0),
        ssd_A_log=jnp.stack([_from_lane_row(grads[l]["ssd_A_log"], "sdt", SSD_HEADS) for l in range(DEPTH)]),
        ssd_dt_bias=jnp.stack([_from_lane_row(grads[l]["ssd_dt_bias"], "sdt", SSD_HEADS) for l in range(DEPTH)]),
        ssd_D=jnp.stack([_from_lane_row(grads[l]["ssd_D"], "sdt", SSD_HEADS) for l in range(DEPTH)]),
        ssd_norm=jnp.concatenate([grads[l]["ssd_norm"].reshape(1, -1) for l in range(DEPTH)], axis=0),
        ret_norm=jnp.concatenate([grads[l]["ret_norm"] for l in range(DEPTH)], axis=0))
    gs = _pack_small(small_g)
    gs8 = jnp.broadcast_to(gs[None], (N_DEV,) + gs.shape)
    (r_small,) = _comm_call("exchange_small", "scatter", [gs8])
    recv = [[recv_dx[0], recv_scan[3], recv_dx[1]], recv_scan[:3]]

    conv_w = lambda g_, s_, l: jnp.stack([g_[l], s_[l]], axis=0).reshape(2 * CONV_W, SHARD_CONV)
    o_in, o_out, o_conv = None, None, []
    for l in range(DEPTH):
        o_in = _adam(f"adam_w_in{l}", recv[l][0], w_in, m_w_in, v_w_in, 128, l, o_in)
        o_out = _adam(f"adam_w_out{l}", recv[l][1], w_out, m_w_out, v_w_out, 128, l, o_out)
        o_conv.append(_adam(f"adam_conv{l}", recv[l][2], conv_w(gdn_conv, ssd_conv, l),
                            conv_w(m_gdn_conv, m_ssd_conv, l), conv_w(v_gdn_conv, v_ssd_conv, l), 2 * CONV_W))
    ps_w, ps_m, ps_v = _pack_small(small_w), _pack_small(small_m), _pack_small(small_v)
    o_small = _adam("adam_small", r_small, ps_w, ps_m, ps_v, ps_w.shape[0])

    names = ["pre_norm", "post_norm", "w_in", "gdn_conv", "gdn_A_log", "gdn_dt_bias", "gdn_norm", "ssd_conv",
             "ssd_conv_b", "ssd_A_log", "ssd_dt_bias", "ssd_D", "ssd_norm", "ret_norm", "w_out"]
    outs = []
    for kind in range(4):
        d = _unpack_small(o_small[kind], small_w)
        cv = jnp.stack([o_conv[l][kind].reshape(2, CONV_W, SHARD_CONV) for l in range(DEPTH)], axis=1)
        d["w_in"] = o_in[kind]
        d["w_out"] = o_out[kind]
        d["gdn_conv"] = cv[0]
        d["ssd_conv"] = cv[1]
        outs.extend(d[n] for n in names)
    return (loss, grad_x, *outs)
```

```python
import functools
import math

import jax
import jax.numpy as jnp
from jax import lax
from jax.experimental import pallas as pl
from jax.experimental.pallas import tpu as pltpu

F32 = jnp.float32
BF16 = jnp.bfloat16

D_MODEL = 1024
DEPTH = 2
CHUNK = 64
CONV_W = 4
EPS = 1e-6
N_DEV = 8

GDN_HEADS = 4
GDN_DK = 128
SSD_HEADS = 16
SSD_P = 64
SSD_N = 128
SSD_GROUPS = 2
SSD_PAIRS = SSD_HEADS // 2
PAIRS_PER_GROUP = SSD_PAIRS // SSD_GROUPS
RET_HEADS = 4
RET_DK = 128
ROPE_BASE = 10000.0
MIX_W = 2048
N_IN = 6680
SHARD_IN = N_IN // N_DEV
SHARD_OUT = MIX_W // N_DEV
CONV_CH = 1536
SHARD_CONV = CONV_CH // N_DEV

ADAM_LR = 0.001
ADAM_B1 = 0.9
ADAM_B2 = 0.999
ADAM_EPS = 1e-08
ADAM_WD = 0.01
ADAM_STEP = 10

LANE = 128
SUBLANE = 8
VMEM_LIMIT = 56 * 1024 * 1024

_ORIG = dict(gq=(0, 512), gk=(512, 512), gv=(1024, 512), gz=(1536, 512), gb=(2048, 4), ga=(2052, 4),
             sx=(2056, 1024), sB=(3080, 256), sC=(3336, 256), sz=(3592, 1024), sdt=(4616, 16),
             rq=(4632, 512), rk=(5144, 512), rv=(5656, 512), rg=(6168, 512))
_ORDER = ["rq", "rk", "rv", "rg", "gq", "gk", "gv", "sx", "sB", "sC", "gz", "sz", "gb", "ga", "sdt"]
_MINE = {}
_off = 0
for _n in _ORDER:
    _MINE[_n] = _off
    _off += _ORIG[_n][1]
N_USED = _off
N_PAD = 7168
CONV_ALL = 2 * CONV_CH
SMALL_OFF = _MINE["gb"]
CONV_OFF = _MINE["gq"]
XC = lambda name: _MINE[name] - CONV_OFF


def _cparams(sem, vmem=None):
    return pltpu.CompilerParams(dimension_semantics=sem, vmem_limit_bytes=vmem)


def _split_bf16(a):
    hi = a.astype(BF16)
    return hi, (a - hi.astype(F32)).astype(BF16)


def _make_mm():
    def raw(a, b, ca, cb):
        return lax.dot_general(a.astype(BF16), b.astype(BF16), (((ca,), (cb,)), ((), ())),
                               preferred_element_type=F32)

    @jax.custom_vjp
    def nn(a, b):
        return raw(a, b, 1, 0)

    @jax.custom_vjp
    def nt(a, b):
        return raw(a, b, 1, 1)

    @jax.custom_vjp
    def tn(a, b):
        return raw(a, b, 0, 0)

    nn.defvjp(lambda a, b: (raw(a, b, 1, 0), (a, b)), lambda r, g: (nt(g, r[1]), tn(r[0], g)))
    nt.defvjp(lambda a, b: (raw(a, b, 1, 1), (a, b)), lambda r, g: (nn(g, r[1]), tn(g, r[0])))
    tn.defvjp(lambda a, b: (raw(a, b, 0, 0), (a, b)), lambda r, g: (nt(r[1], g), nn(r[0], g)))
    return nn, nt, tn


_nn, _nt, _tn = _make_mm()


@jax.custom_vjp
def _swap_halves(t):
    return pltpu.roll(t, LANE // 2, 1)


_swap_halves.defvjp(lambda t: (pltpu.roll(t, LANE // 2, 1), None),
                    lambda _, g: (pltpu.roll(g, LANE // 2, 1),))


@jax.custom_vjp
def _split_rows(x):
    return tuple(x[i * CHUNK:(i + 1) * CHUNK] for i in range(x.shape[0] // CHUNK))


_split_rows.defvjp(lambda x: (tuple(x[i * CHUNK:(i + 1) * CHUNK] for i in range(x.shape[0] // CHUNK)), None),
                   lambda _, gs: (jnp.concatenate(gs, axis=0),))


def _dot3(a, b, ca, cb):
    dot = lambda x, y: lax.dot_general(x, y, (((ca,), (cb,)), ((), ())), preferred_element_type=F32)
    return dot(a[0], b[0]) + (dot(a[0], b[1]) + dot(a[1], b[0]))


@jax.custom_vjp
def _tri_inv(mats):
    return _tri_inv_impl(mats)


def _tri_inv_impl(mats):
    ii = lax.broadcasted_iota(jnp.int32, mats[0].shape, 0)
    jj = lax.broadcasted_iota(jnp.int32, mats[0].shape, 1)
    eye = jnp.where(ii == jj, 1.0, 0.0).astype(F32)
    ts = [eye - a for a in mats]
    ps = [_split_bf16(-a) for a in mats]
    for _ in range(int(math.log2(CHUNK)) - 1):
        ps = [_split_bf16(_dot3(p, p, 1, 0)) for p in ps]
        ts = [t + _dot3(_split_bf16(t), p, 1, 0) for t, p in zip(ts, ps)]
    return ts


def _tri_inv_bwd(ts, gs):
    tsp = [_split_bf16(t) for t in ts]
    xs = [_dot3(t, _split_bf16(g), 0, 0) for t, g in zip(tsp, gs)]
    return ([-_dot3(_split_bf16(x), t, 1, 1) for x, t in zip(xs, tsp)],)


_tri_inv.defvjp(lambda mats: (lambda ts: (ts, ts))(_tri_inv_impl(mats)), _tri_inv_bwd)


@jax.custom_vjp
def _tri_inv_saved(mats, ts):
    return ts


_tri_inv_saved.defvjp(lambda mats, ts: (ts, ts),
                      lambda ts, gs: (_tri_inv_bwd(ts, gs)[0], [jnp.zeros_like(t) for t in ts]))


def _silu(x):
    return x * jax.nn.sigmoid(x)


@jax.custom_vjp
def _softplus(x):
    return jnp.maximum(x, 0.0) + jnp.log1p(jnp.exp(-jnp.abs(x)))


_softplus.defvjp(lambda x: (jnp.maximum(x, 0.0) + jnp.log1p(jnp.exp(-jnp.abs(x))), x),
                 lambda x, g: (g * jax.nn.sigmoid(x),))


def _rms(x, w):
    return x * lax.rsqrt(jnp.mean(x * x, axis=-1, keepdims=True) + EPS) * w


def _chunk_masks(n):
    ii = lax.broadcasted_iota(jnp.int32, (n, n), 0)
    jj = lax.broadcasted_iota(jnp.int32, (n, n), 1)
    return ii >= jj, ii > jj, ii == jj, ii <= jj


def _cumsum_col(g, causal, eye, upper):
    g_row = jnp.sum(jnp.where(eye, g, 0.0), axis=0, keepdims=True)
    col = jnp.sum(jnp.where(causal, g_row, 0.0), axis=1, keepdims=True)
    row = jnp.sum(jnp.where(upper, g, 0.0), axis=0, keepdims=True)
    return col, row


def _lane_col(block, lane):
    pick = lax.broadcasted_iota(jnp.int32, (1, block.shape[1]), 1) == lane
    return jnp.sum(jnp.where(pick, block, 0.0), axis=1, keepdims=True)


def _gdn_chunk(q, k, v, gz, sm, alog_row, dtb_row, nw, s, t_saved=None):
    subs, nh = range(len(q)), len(q[0])
    c = q[0][0].shape[0]
    causal, _, eye, upper = _chunk_masks(c)
    stack = lambda xs: jnp.concatenate(xs, axis=0)
    per = lambda f: [[f(j, h) for h in range(nh)] for j in subs]
    qn = per(lambda j, h: q[j][h] * lax.rsqrt(jnp.sum(q[j][h] * q[j][h], axis=-1, keepdims=True) + EPS)
             * (GDN_DK ** -0.5))
    kn = per(lambda j, h: k[j][h] * lax.rsqrt(jnp.sum(k[j][h] * k[j][h], axis=-1, keepdims=True) + EPS))
    beta_blk = [jax.nn.sigmoid(sm[j]) for j in subs]
    g_blk = [-jnp.exp(alog_row) * _softplus(sm[j] + dtb_row) for j in subs]
    beta = per(lambda j, h: _lane_col(beta_blk[j], _MINE["gb"] - SMALL_OFF + h))
    g = per(lambda j, h: _lane_col(g_blk[j], _MINE["ga"] - SMALL_OFF + h))
    gcum = per(lambda j, h: _cumsum_col(g[j][h], causal, eye, upper)[0])
    eg = per(lambda j, h: jnp.exp(gcum[j][h]))
    glast = per(lambda j, h: jnp.sum(g[j][h], axis=0, keepdims=True))
    kb = per(lambda j, h: kn[j][h] * beta[j][h])
    n = nh * c
    ii = lax.broadcasted_iota(jnp.int32, (n, n), 0)
    jj = lax.broadcasted_iota(jnp.int32, (n, n), 1)
    sh = int(math.log2(c))
    same = lax.shift_right_logical(ii, sh) == lax.shift_right_logical(jj, sh)
    causal_bd = jnp.logical_and(same, ii >= jj)
    strict_bd = jnp.logical_and(same, ii > jj)
    gcum_all = [stack(gcum[j]) for j in subs]
    gcum_row = [jnp.sum(jnp.where(ii == jj, gcum_all[j], 0.0), axis=0, keepdims=True) for j in subs]
    decay = [jnp.where(causal_bd, jnp.exp(jnp.where(causal_bd, gcum_all[j] - gcum_row[j], 0.0)), 0.0) for j in subs]
    kn_all = [stack(kn[j]) for j in subs]
    a_low = [jnp.where(strict_bd, _nt(stack(kb[j]), kn_all[j]) * decay[j], 0.0) for j in subs]
    t = _tri_inv(a_low) if t_saved is None else _tri_inv_saved(a_low, t_saved)
    u = [_nn(t[j], stack([v[j][h] * beta[j][h] for h in range(nh)])) for j in subs]
    w = [_split_rows(_nn(t[j], stack([kb[j][h] * eg[j][h] for h in range(nh)]))) for j in subs]
    attn = [_nt(stack(qn[j]), kn_all[j]) * decay[j] for j in subs]
    kdec = per(lambda j, h: kn[j][h] * jnp.exp(glast[j][h] - gcum[j][h]))
    on = []
    for j in subs:
        v_new_all = u[j] - stack([_nn(w[j][h], s[h]) for h in range(nh)])
        o = _split_rows(stack([_nn(qn[j][h] * eg[j][h], s[h]) for h in range(nh)]) + _nn(attn[j], v_new_all))
        v_new = _split_rows(v_new_all)
        s = [s[h] * jnp.exp(glast[j][h]) + _tn(kdec[j][h], v_new[h]) for h in range(nh)]
        on.append([_rms(o[h], nw) * _silu(gz[j][h]) for h in range(nh)])
    return on, s, t


def _ssd_chunk(lanes, x, bm, cm, sm, alog_row, dtb_row, d_row, hs):
    subs, pairs = range(len(x)), range(len(x[0]))
    c = x[0][0].shape[0]
    lane_i = lax.broadcasted_iota(jnp.int32, (c, LANE), 1)
    lane_lo = lane_i < SSD_P
    lane_lo1 = lax.broadcasted_iota(jnp.int32, (1, LANE), 1) < SSD_P
    causal2 = lax.broadcasted_iota(jnp.int32, (c, LANE), 0) >= jnp.bitwise_and(lane_i, c - 1)
    row_lo = lax.broadcasted_iota(jnp.int32, (LANE, 1), 0) < SSD_P
    eye2 = lax.broadcasted_iota(jnp.int32, (LANE, LANE), 0) == lax.broadcasted_iota(jnp.int32, (LANE, LANE), 1)
    to_row = lambda cols: jnp.sum(jnp.where(eye2, jnp.concatenate(cols, axis=0), 0.0), axis=0, keepdims=True)
    per = lambda f: [[f(j, p) for p in pairs] for j in subs]
    both = lambda blk, p: [_lane_col(blk, lanes[p][h]) for h in range(2)]
    dt_blk = [_softplus(sm[j] + dtb_row) for j in subs]
    a_blk = [dt_blk[j] * -jnp.exp(alog_row) for j in subs]
    alast_blk = [jnp.sum(a_blk[j], axis=0, keepdims=True) for j in subs]
    dt = per(lambda j, p: both(dt_blk[j], p))
    nexp = [both(-jnp.exp(alog_row), p) for p in pairs]
    a = per(lambda j, p: [dt[j][p][h] * nexp[p][h] for h in range(2)])
    alast = per(lambda j, p: both(alast_blk[j], p))
    dp = [both(d_row, p) for p in pairs]
    a_row = per(lambda j, p: to_row(a[j][p]))
    part = per(lambda j, p: jnp.where(causal2, a_row[j][p], 0.0))
    acum = per(lambda j, p: [jnp.sum(jnp.where(lane_lo, part[j][p], 0.0), axis=1, keepdims=True),
                             jnp.sum(jnp.where(lane_lo, 0.0, part[j][p]), axis=1, keepdims=True)])
    acum_row = per(lambda j, p: to_row(acum[j][p]))
    acum_col = per(lambda j, p: jnp.where(lane_lo, acum[j][p][0], acum[j][p][1]))
    lmat = per(lambda j, p: jnp.where(causal2, jnp.exp(jnp.where(causal2, acum_col[j][p] - acum_row[j][p], 0.0)), 0.0))
    cb = [_nt(cm[j], jnp.concatenate([bm[j], bm[j]], axis=0)) for j in subs]
    xdt = per(lambda j, p: x[j][p] * jnp.where(lane_lo, dt[j][p][0], dt[j][p][1]))
    xdt_rows = per(lambda j, p: jnp.concatenate([jnp.where(lane_lo, xdt[j][p], 0.0),
                                                 jnp.where(lane_lo, 0.0, xdt[j][p])], axis=0))
    intra = per(lambda j, p: _nn(cb[j] * lmat[j][p], xdt_rows[j][p]))
    skip = per(lambda j, p: x[j][p] * jnp.where(lane_lo1, dp[p][0], dp[p][1]))
    eacc = per(lambda j, p: jnp.exp(acum_col[j][p]))
    wdec = per(lambda j, p: jnp.exp(jnp.where(lane_lo, alast[j][p][0] - acum[j][p][0],
                                              alast[j][p][1] - acum[j][p][1])))
    scale = per(lambda j, p: jnp.exp(jnp.where(row_lo, alast[j][p][0], alast[j][p][1])))
    upd = per(lambda j, p: _tn(xdt[j][p] * wdec[j][p], bm[j]))
    y = []
    for j in subs:
        y.append([skip[j][p] + intra[j][p] + _nt(cm[j], hs[p]) * eacc[j][p] for p in pairs])
        hs = [hs[p] * scale[j][p] + upd[j][p] for p in pairs]
    return y, hs


def _ret_chunk(rq, rk, rv, rg, cos2, sin2, dmat, qdec, kdec, cdec, nw, r):
    subs, hs = range(len(rq)), range(len(rq[0]))
    per = lambda f: [[f(j, h) for h in hs] for j in subs]
    q = per(lambda j, h: rq[j][h] * cos2[j] + _swap_halves(rq[j][h]) * sin2[j])
    k = per(lambda j, h: (rk[j][h] * cos2[j] + _swap_halves(rk[j][h]) * sin2[j]) * (RET_DK ** -0.5))
    s = per(lambda j, h: _nt(q[j][h], k[j][h]) * dmat[h])
    intra = per(lambda j, h: _nn(s[j][h], rv[j][h]))
    upd = per(lambda j, h: _tn(k[j][h] * kdec[h], rv[j][h]))
    gate = per(lambda j, h: _silu(rg[j][h]))
    on = []
    for j in subs:
        on.append([_rms(intra[j][h] + _nn(q[j][h], r[h]) * qdec[h], nw) * gate[j][h] for h in hs])
        r = [r[h] * cdec[h] + upd[j][h] for h in hs]
    return on, r


MM_TILE = 1024


def _tile(n, pref=MM_TILE):
    return pref if n % pref == 0 else n


def _matmul(name, a, b, mode, m, n, k, tk=None, comm=None, out_dtype=F32):
    tm, tn = _tile(m), _tile(n)
    tk = _tile(k) if tk is None else tk
    nk = k // tk
    ca, cb = {"nn": (1, 0), "nt": (1, 1), "tn": (0, 0)}[mode]
    grid = (m // tm, n // tn, nk)
    n_comm = len(comm[1]) if comm else 0
    use_acc = out_dtype != F32 and nk > 1

    def at_step(which):
        return functools.reduce(jnp.logical_and, [pl.program_id(d) == (0 if which == 0 else grid[d] - 1)
                                                  for d in range(3)])

    def body(*refs):
        a_ref, b_ref = refs[:2]
        c_in = refs[2:2 + n_comm]
        o_ref = refs[2 + n_comm]
        c_out = refs[3 + n_comm:3 + 2 * n_comm]
        acc_ref = refs[3 + 2 * n_comm] if use_acc else o_ref
        sems = refs[3 + 2 * n_comm + use_acc:]
        if comm:
            @pl.when(at_step(0))
            def _():
                _comm_plan(comm[0], c_in, c_out, *sems)[0]()

        part = lax.dot_general(a_ref[...].astype(BF16), b_ref[...].astype(BF16),
                               (((ca,), (cb,)), ((), ())), preferred_element_type=F32)
        if nk == 1:
            o_ref[...] = part.astype(o_ref.dtype)
        else:
            kk = pl.program_id(2)

            @pl.when(kk == 0)
            def _():
                acc_ref[...] = part

            @pl.when(kk > 0)
            def _():
                acc_ref[...] += part

            if use_acc:
                @pl.when(kk == nk - 1)
                def _():
                    o_ref[...] = acc_ref[...].astype(o_ref.dtype)

        if comm:
            @pl.when(at_step(1))
            def _():
                _comm_plan(comm[0], c_in, c_out, *sems)[1]()

    a_spec = pl.BlockSpec((tk, tm), lambda i, j, kk: (kk, i)) if mode == "tn" else pl.BlockSpec((tm, tk), lambda i, j, kk: (i, kk))
    b_spec = pl.BlockSpec((tn, tk), lambda i, j, kk: (j, kk)) if mode == "nt" else pl.BlockSpec((tk, tn), lambda i, j, kk: (kk, j))
    hbm = pl.BlockSpec(memory_space=pl.ANY)
    res = pl.pallas_call(
        body, name=name, grid=grid,
        in_specs=[a_spec, b_spec] + [hbm] * n_comm,
        out_specs=[pl.BlockSpec((tm, tn), lambda i, j, kk: (i, j))] + [hbm] * n_comm,
        out_shape=[jax.ShapeDtypeStruct((m, n), out_dtype)] + (_comm_out_shapes(*comm) if comm else []),
        scratch_shapes=([pltpu.VMEM((tm, tn), F32)] if use_acc else []) + (_comm_scratch(n_comm) if comm else []),
        compiler_params=_cparams(("arbitrary",) * 3 if comm else ("parallel", "parallel", "arbitrary"), VMEM_LIMIT),
    )(a, b, *(comm[1] if comm else []))
    return res if comm else res[0]


def _rowmap(name, fn, length, tm, rows, params, out_ws=(), acc_ws=(), out_dtypes=None, cts=None, d_dtypes=None,
            place=None):
    nt_ = length // tm
    n_r, n_p = len(rows), len(params)
    n_in = n_r + n_p
    rspec = lambda bw, cbk: pl.BlockSpec((tm, bw), lambda i: (i, cbk))
    pspec = lambda w: pl.BlockSpec((1, w), lambda i: (0, 0))
    in_arrays = [r[0] for r in rows] + list(params)
    in_specs = [rspec(r[1], r[2]) for r in rows] + [pspec(p.shape[1]) for p in params]
    place = place or {}

    def load(refs):
        return [r[...].astype(F32) for r in refs]

    def placed(widths, dtypes, n_before):
        specs, shapes, extra, alias = [], [], [], {}
        for k, (w, dt) in enumerate(zip(widths, dtypes)):
            arr, total, cbk = place.get(k, (None, w, 0))
            specs.append(rspec(w, cbk))
            shapes.append(jax.ShapeDtypeStruct((length, total), dt))
            if arr is not None:
                alias[n_before + len(extra)] = k
                extra.append(arr)
        return specs, shapes, extra, alias

    def accumulate(a_refs, vals):
        first = pl.program_id(0) == 0
        for a_ref, v in zip(a_refs, vals):
            @pl.when(first)
            def _(a_ref=a_ref):
                a_ref[...] = jnp.zeros_like(a_ref)
            a_ref[...] += v

    hbm = pl.BlockSpec(memory_space=pl.ANY)
    if cts is None:
        n_o = len(out_ws)
        o_specs, o_shapes, extra, alias = placed(out_ws, out_dtypes or [F32] * n_o, n_in)

        def body(*refs):
            outs, accs = fn(*load(refs[:n_in]))
            o_refs = refs[n_in + len(extra):]
            for o_ref, o in zip(o_refs[:n_o], outs):
                o_ref[...] = o.astype(o_ref.dtype)
            accumulate(o_refs[n_o:], accs)

        res = pl.pallas_call(
            body, name=name, grid=(nt_,), in_specs=in_specs + [hbm] * len(extra),
            out_specs=o_specs + [pspec(w) for w in acc_ws],
            out_shape=o_shapes + [jax.ShapeDtypeStruct((1, w), F32) for w in acc_ws],
            input_output_aliases=alias, compiler_params=_cparams(("arbitrary",), VMEM_LIMIT),
        )(*in_arrays, *extra)
        return res[:n_o], res[n_o:]

    n_c = len(cts)
    o_specs, o_shapes, extra, alias = placed([r[1] for r in rows], d_dtypes or [F32] * n_r, n_in + n_c)

    def body(*refs):
        ins = load(refs[:n_in])
        _, vjp = jax.vjp(lambda rs, ps: fn(*rs, *ps)[0], ins[:n_r], ins[n_r:])
        d_rows, d_params = vjp(tuple(load(refs[n_in:n_in + n_c])))
        o_refs = refs[n_in + n_c + len(extra):]
        for o_ref, d in zip(o_refs[:n_r], d_rows):
            o_ref[...] = d.astype(o_ref.dtype)
        accumulate(o_refs[n_r:], d_params)

    res = pl.pallas_call(
        body, name=name, grid=(nt_,),
        in_specs=in_specs + [rspec(c[1], c[2]) for c in cts] + [hbm] * len(extra),
        out_specs=o_specs + [pspec(p.shape[1]) for p in params],
        out_shape=o_shapes + [jax.ShapeDtypeStruct((1, p.shape[1]), F32) for p in params],
        input_output_aliases=alias, compiler_params=_cparams(("arbitrary",), VMEM_LIMIT),
    )(*in_arrays, *[c[0] for c in cts], *extra)
    return res[:n_r], res[n_r:]


def _conv_shift(ext, k, tm, forward):
    s = CONV_W - 1 - k
    if forward:
        rolled = ext if s == 0 else pltpu.roll(ext, s, 0)
        return rolled[SUBLANE:, :]
    rolled = ext if s == 0 else pltpu.roll(ext, tm + SUBLANE - s, 0)
    return rolled[:tm, :]


def _conv_pre(x_ref, halo_ref, w_ref, b_ref, first, tm):
    halo = jnp.where(first, 0.0, halo_ref[...])
    ext = jnp.concatenate([halo, x_ref[...]], axis=0)
    w = w_ref[...]
    pre = b_ref[...] + jnp.zeros_like(x_ref[...])
    taps = []
    for k in range(CONV_W):
        tap = _conv_shift(ext, k, tm, True)
        taps.append(tap)
        pre = pre + tap * w[k:k + 1, :]
    return pre, taps


def _conv_fwd(proj, w, b, length, tm, tc):
    hb = tm // SUBLANE
    j0 = CONV_OFF // tc

    def body(x_ref, halo_ref, w_ref, b_ref, o_ref):
        pre, _ = _conv_pre(x_ref, halo_ref, w_ref, b_ref, pl.program_id(0) == 0, tm)
        o_ref[...] = _silu(pre)

    return pl.pallas_call(
        body, name="conv_fwd", grid=(length // tm, CONV_ALL // tc),
        in_specs=[pl.BlockSpec((tm, tc), lambda i, j: (i, j0 + j)),
                  pl.BlockSpec((SUBLANE, tc), lambda i, j: (jnp.maximum(i * hb - 1, 0), j0 + j)),
                  pl.BlockSpec((CONV_W, tc), lambda i, j: (0, j)),
                  pl.BlockSpec((1, tc), lambda i, j: (0, j))],
        out_specs=pl.BlockSpec((tm, tc), lambda i, j: (i, j)),
        out_shape=jax.ShapeDtypeStruct((length, CONV_ALL), F32),
        compiler_params=_cparams(("parallel", "parallel"), VMEM_LIMIT),
    )(proj, proj, w, b)


def _conv_bwd_pre(proj, w, b, dxc_a, dxc_b, length, tm, tc):
    hb = tm // SUBLANE
    j0 = CONV_OFF // tc
    jh = CONV_CH // tc

    def body(x_ref, halo_ref, w_ref, b_ref, dya_ref, dyb_ref, dpre_ref, dw_ref, db_ref):
        i = pl.program_id(1)
        pre, taps = _conv_pre(x_ref, halo_ref, w_ref, b_ref, i == 0, tm)
        sg = jax.nn.sigmoid(pre)
        dy = jnp.where(pl.program_id(0) < jh, dya_ref[...], dyb_ref[...])
        dpre = dy * (sg * (1.0 + pre * (1.0 - sg)))
        dpre_ref[...] = dpre

        @pl.when(i == 0)
        def _():
            dw_ref[...] = jnp.zeros_like(dw_ref)
            db_ref[...] = jnp.zeros_like(db_ref)

        for k in range(CONV_W):
            dw_ref[k:k + 1, :] += jnp.sum(dpre * taps[k], axis=0, keepdims=True)
        db_ref[...] += jnp.sum(dpre, axis=0, keepdims=True)

    return pl.pallas_call(
        body, name="conv_bwd_pre", grid=(CONV_ALL // tc, length // tm),
        in_specs=[pl.BlockSpec((tm, tc), lambda j, i: (i, j0 + j)),
                  pl.BlockSpec((SUBLANE, tc), lambda j, i: (jnp.maximum(i * hb - 1, 0), j0 + j)),
                  pl.BlockSpec((CONV_W, tc), lambda j, i: (0, j)),
                  pl.BlockSpec((1, tc), lambda j, i: (0, j)),
                  pl.BlockSpec((tm, tc), lambda j, i: (jnp.where(j < jh, i, 0), jnp.minimum(j, jh - 1))),
                  pl.BlockSpec((tm, tc), lambda j, i: (jnp.where(j < jh, 0, i), jnp.maximum(j - jh, 0)))],
        out_specs=[pl.BlockSpec((tm, tc), lambda j, i: (i, j)),
                   pl.BlockSpec((CONV_W, tc), lambda j, i: (0, j)),
                   pl.BlockSpec((1, tc), lambda j, i: (0, j))],
        out_shape=[jax.ShapeDtypeStruct((length, CONV_ALL), F32),
                   jax.ShapeDtypeStruct((CONV_W, CONV_ALL), F32),
                   jax.ShapeDtypeStruct((1, CONV_ALL), F32)],
        compiler_params=_cparams(("parallel", "arbitrary"), VMEM_LIMIT),
    )(proj, proj, w, b, dxc_a, dxc_b)


def _conv_bwd_x(dpre, w, dproj, length, tm, tc):
    hb = tm // SUBLANE
    n_t = length // tm
    last_blk = length // SUBLANE - 1
    j0 = CONV_OFF // tc

    def body(d_ref, halo_ref, w_ref, _, o_ref):
        halo = jnp.where(pl.program_id(0) == n_t - 1, 0.0, halo_ref[...])
        ext = jnp.concatenate([d_ref[...], halo], axis=0)
        w = w_ref[...]
        acc = jnp.zeros_like(d_ref[...])
        for k in range(CONV_W):
            acc = acc + _conv_shift(ext, k, tm, False) * w[k:k + 1, :]
        o_ref[...] = acc.astype(o_ref.dtype)

    return pl.pallas_call(
        body, name="conv_bwd_x", grid=(n_t, CONV_ALL // tc),
        in_specs=[pl.BlockSpec((tm, tc), lambda i, j: (i, j)),
                  pl.BlockSpec((SUBLANE, tc), lambda i, j: (jnp.minimum((i + 1) * hb, last_blk), j)),
                  pl.BlockSpec((CONV_W, tc), lambda i, j: (0, j)),
                  pl.BlockSpec(memory_space=pl.ANY)],
        out_specs=pl.BlockSpec((tm, tc), lambda i, j: (i, j0 + j)),
        out_shape=jax.ShapeDtypeStruct(dproj.shape, dproj.dtype), input_output_aliases={3: 0},
        compiler_params=_cparams(("parallel", "parallel"), VMEM_LIMIT),
    )(dpre, dpre, w, dproj)


HEAD_SCAN_CHUNKS_PER_STEP = 4
assert GDN_HEADS == RET_HEADS


def _scan_steps(length):
    nc = length // CHUNK
    gs = HEAD_SCAN_CHUNKS_PER_STEP if nc % HEAD_SCAN_CHUNKS_PER_STEP == 0 else 1
    return gs, nc // gs, gs * CHUNK


def _head_tiles(ref, gs):
    return [[ref[j * CHUNK:(j + 1) * CHUNK, h * LANE:(h + 1) * LANE] for h in range(GDN_HEADS)] for j in range(gs)]


def _chunk_rows(ref, gs):
    return [ref[j * CHUNK:(j + 1) * CHUNK, :] for j in range(gs)]


def _gdn_fwd(xc, proj, alog, dtb, nw, length, comm=None):
    gs, steps, rows = _scan_steps(length)
    h_ = GDN_HEADS
    hw = h_ * LANE
    tn_ = h_ * CHUNK
    n_comm = len(comm[1]) if comm else 0

    def body(*refs):
        q_ref, k_ref, v_ref, gz_ref, sm_ref, al_ref, dt_ref, nw_ref = refs[:8]
        c_in = refs[8:8 + n_comm]
        on_ref, st_ref, t_ref = refs[8 + n_comm:11 + n_comm]
        c_out = refs[11 + n_comm:11 + 2 * n_comm]
        s_ref = refs[11 + 2 * n_comm]
        sems = refs[12 + 2 * n_comm:]

        @pl.when(pl.program_id(0) == 0)
        def _():
            s_ref[...] = jnp.zeros_like(s_ref)
            if comm:
                _comm_plan(comm[0], c_in, c_out, *sems)[0]()

        st_ref[...] = s_ref[...]
        lead = lambda ref: [ref[h] for h in range(h_)]
        on, s_new, t = _gdn_chunk(_head_tiles(q_ref, gs), _head_tiles(k_ref, gs), _head_tiles(v_ref, gs),
                                  _head_tiles(gz_ref, gs), _chunk_rows(sm_ref, gs), al_ref[...], dt_ref[...],
                                  nw_ref[...], lead(s_ref))
        for j in range(gs):
            t_ref[j] = t[j]
        for h in range(h_):
            for j in range(gs):
                on_ref[j * CHUNK:(j + 1) * CHUNK, h * LANE:(h + 1) * LANE] = on[j][h].astype(on_ref.dtype)
            s_ref[h] = s_new[h]

        if comm:
            @pl.when(pl.program_id(0) == steps - 1)
            def _():
                _comm_plan(comm[0], c_in, c_out, *sems)[1]()

    blk = lambda col: pl.BlockSpec((rows, hw), lambda c: (c, col // hw))
    one = pl.BlockSpec((1, LANE), lambda c: (0, 0))
    hbm = pl.BlockSpec(memory_space=pl.ANY)
    return pl.pallas_call(
        body, name="gdn_fwd_" + comm[0] if comm else "gdn_fwd", grid=(steps,),
        in_specs=[blk(XC("gq")), blk(XC("gk")), blk(XC("gv")), blk(_MINE["gz"]),
                  pl.BlockSpec((rows, LANE), lambda c: (c, SMALL_OFF // LANE)), one, one,
                  pl.BlockSpec((1, LANE), lambda c: (0, 0))] + [hbm] * n_comm,
        out_specs=[pl.BlockSpec((rows, hw), lambda c: (c, 0)),
                   pl.BlockSpec((h_, None, LANE, LANE), lambda c: (0, c, 0, 0)),
                   pl.BlockSpec((gs, tn_, tn_), lambda c: (c, 0, 0))] + [hbm] * n_comm,
        out_shape=[jax.ShapeDtypeStruct((length, MIX_W), BF16),
                   jax.ShapeDtypeStruct((h_, steps, LANE, LANE), F32),
                   jax.ShapeDtypeStruct((length // CHUNK, tn_, tn_), F32)]
                  + (_comm_out_shapes(*comm) if comm else []),
        scratch_shapes=[pltpu.VMEM((h_, LANE, LANE), F32)] + (_comm_scratch(n_comm) if comm else []),
        compiler_params=_cparams(("arbitrary",), VMEM_LIMIT),
    )(xc, xc, xc, proj, proj, alog, dtb, nw, *(comm[1] if comm else []))


def _gdn_bwd(xc, proj, alog, dtb, nw, st, tinv, dmix, length, comm=None):
    gs, steps, rows = _scan_steps(length)
    h_ = GDN_HEADS
    hw = h_ * LANE
    tn_ = h_ * CHUNK
    n_comm = len(comm[1]) if comm else 0
    n_in, n_out = 11, 6

    def body(*refs):
        q_ref, k_ref, v_ref, gz_ref, sm_ref, al_ref, dt_ref, nw_ref, st_ref, t_ref, do_ref = refs[:n_in]
        c_in = refs[n_in:n_in + n_comm]
        o0 = n_in + n_comm
        dqkv_ref, dgz_ref, dsm_ref, dal_ref, ddt_ref, dnw_ref = refs[o0:o0 + n_out]
        c_out = refs[o0 + n_out:o0 + n_out + n_comm]
        ds_ref = refs[o0 + n_out + n_comm]
        sems = refs[o0 + n_out + n_comm + 1:]

        @pl.when(pl.program_id(0) == 0)
        def _():
            ds_ref[...] = jnp.zeros_like(ds_ref)
            dal_ref[...] = jnp.zeros_like(dal_ref)
            ddt_ref[...] = jnp.zeros_like(ddt_ref)
            dnw_ref[...] = jnp.zeros_like(dnw_ref)
            if comm:
                _comm_plan(comm[0], c_in, c_out, *sems)[0]()

        lead = lambda ref: [ref[h] for h in range(h_)]
        ins = (_head_tiles(q_ref, gs), _head_tiles(k_ref, gs), _head_tiles(v_ref, gs), _head_tiles(gz_ref, gs),
               _chunk_rows(sm_ref, gs), al_ref[...], dt_ref[...], nw_ref[...], lead(st_ref))
        t_saved = [t_ref[j] for j in range(gs)]
        _, vjp = jax.vjp(lambda *a: _gdn_chunk(*a, t_saved=t_saved)[:2], *ins)
        dq, dk, dv, dgz, dsm, dal, ddt, dnw, ds = vjp((_head_tiles(do_ref, gs), lead(ds_ref)))
        for j in range(gs):
            rws = slice(j * CHUNK, (j + 1) * CHUNK)
            for h in range(h_):
                cols = slice(h * LANE, (h + 1) * LANE)
                for part, d in enumerate((dq, dk, dv)):
                    dqkv_ref[rws, part * hw + h * LANE:part * hw + (h + 1) * LANE] = d[j][h]
                dgz_ref[rws, cols] = dgz[j][h].astype(dgz_ref.dtype)
            dsm_ref[rws, :] = dsm[j]
        for h in range(h_):
            ds_ref[h] = ds[h]
        dal_ref[...] += dal
        ddt_ref[...] += ddt
        dnw_ref[...] += dnw

        if comm:
            @pl.when(pl.program_id(0) == steps - 1)
            def _():
                _comm_plan(comm[0], c_in, c_out, *sems)[1]()

    r = lambda c: steps - 1 - c
    blk = lambda col: pl.BlockSpec((rows, hw), lambda c: (r(c), col // hw))
    one = pl.BlockSpec((1, LANE), lambda c: (0, 0))
    hbm = pl.BlockSpec(memory_space=pl.ANY)
    return pl.pallas_call(
        body, name="gdn_bwd_" + comm[0] if comm else "gdn_bwd", grid=(steps,),
        in_specs=[blk(XC("gq")), blk(XC("gk")), blk(XC("gv")), blk(_MINE["gz"]),
                  pl.BlockSpec((rows, LANE), lambda c: (r(c), SMALL_OFF // LANE)), one, one,
                  pl.BlockSpec((1, LANE), lambda c: (0, 0)),
                  pl.BlockSpec((h_, None, LANE, LANE), lambda c: (0, r(c), 0, 0)),
                  pl.BlockSpec((gs, tn_, tn_), lambda c: (r(c), 0, 0)),
                  blk(0)] + [hbm] * n_comm,
        out_specs=[pl.BlockSpec((rows, 3 * hw), lambda c: (r(c), 0)), blk(_MINE["gz"]),
                   pl.BlockSpec((rows, LANE), lambda c: (r(c), 0)), one, one,
                   pl.BlockSpec((1, LANE), lambda c: (0, 0))] + [hbm] * n_comm,
        out_shape=[jax.ShapeDtypeStruct((length, 3 * hw), F32), jax.ShapeDtypeStruct((length, N_PAD), BF16),
                   jax.ShapeDtypeStruct((length, LANE), F32),
                   jax.ShapeDtypeStruct((1, LANE), F32), jax.ShapeDtypeStruct((1, LANE), F32),
                   jax.ShapeDtypeStruct((1, LANE), F32)] + (_comm_out_shapes(*comm) if comm else []),
        scratch_shapes=[pltpu.VMEM((h_, LANE, LANE), F32)] + (_comm_scratch(n_comm) if comm else []),
        compiler_params=_cparams(("arbitrary",), VMEM_LIMIT),
    )(xc, xc, xc, proj, proj, alog, dtb, nw, st, tinv, dmix, *(comm[1] if comm else []))


SSD_CHUNKS_PER_STEP = 8


def _ssd_steps(length):
    nc = length // CHUNK
    gs = SSD_CHUNKS_PER_STEP if nc % SSD_CHUNKS_PER_STEP == 0 else 1
    return gs, nc // gs, gs * CHUNK


def _ssd_lanes(g):
    base = _MINE["sdt"] - SMALL_OFF
    return [[base + (g * PAIRS_PER_GROUP + p) * 2 + j for j in range(2)] for p in range(PAIRS_PER_GROUP)]


def _ssd_tiles(ref, gs):
    return [[ref[j * CHUNK:(j + 1) * CHUNK, p * LANE:(p + 1) * LANE] for p in range(PAIRS_PER_GROUP)]
            for j in range(gs)]


def _ssd_gate(y, z, w):
    return _rms(y * _silu(z), w)


def _ssd_fwd(xc, proj, alog_row, dtb_row, d_row, norm_w, mixed, length):
    gs, steps, rows = _ssd_steps(length)
    ppg = PAIRS_PER_GROUP
    gw = ppg * LANE

    def body(x_ref, b_ref, c_ref, sm_ref, al_ref, db_ref, dp_ref, z_ref, nw_ref, _, y_ref, o_ref, st_ref, hs_ref):
        @pl.when(pl.program_id(1) == 0)
        def _():
            hs_ref[...] = jnp.zeros_like(hs_ref)

        st_ref[...] = hs_ref[...]
        y, hs_new = _ssd_chunk(_ssd_lanes(pl.program_id(0)), _ssd_tiles(x_ref, gs), _chunk_rows(b_ref, gs),
                               _chunk_rows(c_ref, gs), _chunk_rows(sm_ref, gs), al_ref[...], db_ref[...],
                               dp_ref[...], [hs_ref[p] for p in range(ppg)])
        for p in range(ppg):
            hs_ref[p] = hs_new[p]
        for j in range(gs):
            rws = slice(j * CHUNK, (j + 1) * CHUNK)
            y_j = jnp.concatenate(y[j], axis=1)
            y_ref[rws, :] = y_j
            o_ref[rws, :] = _ssd_gate(y_j, z_ref[rws, :], nw_ref[...]).astype(o_ref.dtype)

    one = pl.BlockSpec((1, LANE), lambda g, c: (0, 0))
    return pl.pallas_call(
        body, name="ssd_fwd", grid=(SSD_GROUPS, steps),
        in_specs=[pl.BlockSpec((rows, gw), lambda g, c: (c, XC("sx") // gw + g)),
                  pl.BlockSpec((rows, LANE), lambda g, c: (c, XC("sB") // LANE + g)),
                  pl.BlockSpec((rows, LANE), lambda g, c: (c, XC("sC") // LANE + g)),
                  pl.BlockSpec((rows, LANE), lambda g, c: (c, SMALL_OFF // LANE)),
                  one, one, one,
                  pl.BlockSpec((rows, gw), lambda g, c: (c, _MINE["sz"] // gw + g)),
                  pl.BlockSpec((None, 1, gw), lambda g, c: (g, 0, 0)),
                  pl.BlockSpec(memory_space=pl.ANY)],
        out_specs=[pl.BlockSpec((rows, gw), lambda g, c: (c, g)),
                   pl.BlockSpec((rows, gw), lambda g, c: (c, 1 + g)),
                   pl.BlockSpec((None, ppg, None, LANE, SSD_N), lambda g, c: (g, 0, c, 0, 0))],
        out_shape=[jax.ShapeDtypeStruct((length, SSD_GROUPS * gw), F32),
                   jax.ShapeDtypeStruct(mixed.shape, mixed.dtype),
                   jax.ShapeDtypeStruct((SSD_GROUPS, ppg, steps, LANE, SSD_N), F32)],
        input_output_aliases={9: 1},
        scratch_shapes=[pltpu.VMEM((ppg, LANE, SSD_N), F32)],
        compiler_params=_cparams(("parallel", "arbitrary"), VMEM_LIMIT),
    )(xc, xc, xc, proj, alog_row, dtb_row, d_row, proj, norm_w, mixed)


def _ssd_bwd(xc, proj, alog_row, dtb_row, d_row, norm_w, st, y_b, dmix, dproj, length):
    gs, steps, rows = _ssd_steps(length)
    ppg = PAIRS_PER_GROUP
    gw = ppg * LANE

    def body(x_ref, b_ref, c_ref, sm_ref, al_ref, db_ref, dp_ref, z_ref, nw_ref, st_ref, y_ref, do_ref, _,
             dxc_ref, dz_ref, dsm_ref, dal_ref, ddb_ref, ddp_ref, dnw_ref, dhs_ref):
        g = pl.program_id(1)

        @pl.when(jnp.logical_and(pl.program_id(0) == 0, g == 0))
        def _():
            dhs_ref[...] = jnp.zeros_like(dhs_ref)
            dal_ref[...] = jnp.zeros_like(dal_ref)
            ddb_ref[...] = jnp.zeros_like(ddb_ref)
            ddp_ref[...] = jnp.zeros_like(ddp_ref)
            dnw_ref[...] = jnp.zeros_like(dnw_ref)

        pairs = range(ppg)
        dy, dnw = [], jnp.zeros((1, gw), F32)
        for j in range(gs):
            rws = slice(j * CHUNK, (j + 1) * CHUNK)
            _, gate_vjp = jax.vjp(_ssd_gate, y_ref[rws, :], z_ref[rws, :], nw_ref[...])
            dy_j, dz_j, dnw_j = gate_vjp(do_ref[rws, :].astype(F32))
            dz_ref[rws, :] = dz_j.astype(dz_ref.dtype)
            dy.append([dy_j[:, p * LANE:(p + 1) * LANE] for p in pairs])
            dnw = dnw + dnw_j
        dnw_ref[g] += dnw
        ins = (_ssd_tiles(x_ref, gs), _chunk_rows(b_ref, gs), _chunk_rows(c_ref, gs), _chunk_rows(sm_ref, gs),
               al_ref[...], db_ref[...], dp_ref[...], [st_ref[p] for p in pairs])
        _, vjp = jax.vjp(functools.partial(_ssd_chunk, _ssd_lanes(g)), *ins)
        dx, dbm, dcm, dsm, dal, ddb, ddp, dhs = vjp((dy, [dhs_ref[g, p] for p in pairs]))
        for grp in range(SSD_GROUPS):
            @pl.when(g == grp)
            def _(grp=grp):
                for j in range(gs):
                    rws = slice(j * CHUNK, (j + 1) * CHUNK)
                    for p in pairs:
                        col = grp * gw + p * LANE
                        dxc_ref[rws, col:col + LANE] = dx[j][p]
                    cb_ = XC("sB") - XC("sx") + grp * SSD_N
                    cc_ = XC("sC") - XC("sx") + grp * SSD_N
                    dxc_ref[rws, cb_:cb_ + SSD_N] = dbm[j]
                    dxc_ref[rws, cc_:cc_ + SSD_N] = dcm[j]
        for j in range(gs):
            dsm_ref[j * CHUNK:(j + 1) * CHUNK, :] = dsm[j]
        for p in pairs:
            dhs_ref[g, p] = dhs[p]
        dal_ref[g] += dal
        ddb_ref[g] += ddb
        ddp_ref[g] += ddp

    r = lambda c: steps - 1 - c
    one = pl.BlockSpec((1, LANE), lambda c, g: (0, 0))
    acc = pl.BlockSpec((SSD_GROUPS, 1, LANE), lambda c, g: (0, 0, 0))
    prm = jax.ShapeDtypeStruct((SSD_GROUPS, 1, LANE), F32)
    sz_blk = _MINE["sz"] // gw
    return pl.pallas_call(
        body, name="ssd_bwd", grid=(steps, SSD_GROUPS),
        in_specs=[pl.BlockSpec((rows, gw), lambda c, g: (r(c), XC("sx") // gw + g)),
                  pl.BlockSpec((rows, LANE), lambda c, g: (r(c), XC("sB") // LANE + g)),
                  pl.BlockSpec((rows, LANE), lambda c, g: (r(c), XC("sC") // LANE + g)),
                  pl.BlockSpec((rows, LANE), lambda c, g: (r(c), SMALL_OFF // LANE)),
                  one, one, one,
                  pl.BlockSpec((rows, gw), lambda c, g: (r(c), sz_blk + g)),
                  pl.BlockSpec((None, 1, gw), lambda c, g: (g, 0, 0)),
                  pl.BlockSpec((None, ppg, None, LANE, SSD_N), lambda c, g: (g, 0, r(c), 0, 0)),
                  pl.BlockSpec((rows, gw), lambda c, g: (r(c), g)),
                  pl.BlockSpec((rows, gw), lambda c, g: (r(c), 1 + g)),
                  pl.BlockSpec(memory_space=pl.ANY)],
        out_specs=[pl.BlockSpec((rows, CONV_CH), lambda c, g: (r(c), 0)),
                   pl.BlockSpec((rows, gw), lambda c, g: (r(c), sz_blk + g)),
                   pl.BlockSpec((None, rows, LANE), lambda c, g: (g, r(c), 0)), acc, acc, acc,
                   pl.BlockSpec((SSD_GROUPS, 1, gw), lambda c, g: (0, 0, 0))],
        out_shape=[jax.ShapeDtypeStruct((length, CONV_CH), F32), jax.ShapeDtypeStruct(dproj.shape, dproj.dtype),
                   jax.ShapeDtypeStruct((SSD_GROUPS, length, LANE), F32), prm, prm, prm,
                   jax.ShapeDtypeStruct((SSD_GROUPS, 1, gw), F32)],
        input_output_aliases={12: 1},
        scratch_shapes=[pltpu.VMEM((SSD_GROUPS, ppg, LANE, SSD_N), F32)],
        compiler_params=_cparams(("arbitrary", "arbitrary"), VMEM_LIMIT),
    )(xc, xc, xc, proj, alog_row, dtb_row, d_row, proj, norm_w, st, y_b, dmix, dproj)


def _ret_fwd(proj, cos2, sin2, dmat, qdec, kdec, cdec, nw, mixed, length):
    gs, steps, rows = _scan_steps(length)
    h_ = RET_HEADS

    def body(q_ref, k_ref, v_ref, g_ref, cos_ref, sin_ref, dm_ref, qd_ref, kd_ref, cd_ref, nw_ref, _,
             on_ref, st_ref, r_ref):
        @pl.when(pl.program_id(0) == 0)
        def _():
            r_ref[...] = jnp.zeros_like(r_ref)

        st_ref[...] = r_ref[...]
        lead = lambda ref: [ref[h] for h in range(h_)]
        on, r_new = _ret_chunk(_head_tiles(q_ref, gs), _head_tiles(k_ref, gs), _head_tiles(v_ref, gs),
                               _head_tiles(g_ref, gs), _chunk_rows(cos_ref, gs), _chunk_rows(sin_ref, gs),
                               lead(dm_ref), lead(qd_ref), lead(kd_ref), lead(cd_ref), nw_ref[...], lead(r_ref))
        for h in range(h_):
            for j in range(gs):
                on_ref[j * CHUNK:(j + 1) * CHUNK, h * LANE:(h + 1) * LANE] = on[j][h].astype(on_ref.dtype)
            r_ref[h] = r_new[h]

    hw = h_ * LANE
    blk = lambda name: pl.BlockSpec((rows, hw), lambda c: (c, _MINE[name] // hw))
    tab = pl.BlockSpec((rows, LANE), lambda c: (c, 0))
    full = lambda *s: pl.BlockSpec(s, lambda c: (0,) * len(s))
    return pl.pallas_call(
        body, name="ret_fwd", grid=(steps,),
        in_specs=[blk("rq"), blk("rk"), blk("rv"), blk("rg"), tab, tab,
                  full(h_, CHUNK, CHUNK), full(h_, CHUNK, 1), full(h_, CHUNK, 1), full(h_, 1, 1), full(1, LANE),
                  pl.BlockSpec(memory_space=pl.ANY)],
        out_specs=[pl.BlockSpec((rows, hw), lambda c: (c, (MIX_W - hw) // hw)),
                   pl.BlockSpec((h_, None, LANE, LANE), lambda c: (0, c, 0, 0))],
        out_shape=[jax.ShapeDtypeStruct(mixed.shape, mixed.dtype),
                   jax.ShapeDtypeStruct((h_, steps, LANE, LANE), F32)],
        input_output_aliases={11: 0},
        scratch_shapes=[pltpu.VMEM((h_, LANE, LANE), F32)],
        compiler_params=_cparams(("arbitrary",), VMEM_LIMIT),
    )(proj, proj, proj, proj, cos2, sin2, dmat, qdec, kdec, cdec, nw, mixed)


def _ret_bwd(proj, cos2, sin2, dmat, qdec, kdec, cdec, nw, st, dmix, dproj, length):
    gs, steps, rows = _scan_steps(length)
    h_ = RET_HEADS
    hw = h_ * LANE

    def body(q_ref, k_ref, v_ref, g_ref, cos_ref, sin_ref, dm_ref, qd_ref, kd_ref, cd_ref, nw_ref, st_ref,
             do_ref, _, dqkvg_ref, dnw_ref, dr_ref):
        @pl.when(pl.program_id(0) == 0)
        def _():
            dr_ref[...] = jnp.zeros_like(dr_ref)
            dnw_ref[...] = jnp.zeros_like(dnw_ref)

        lead = lambda ref: [ref[h] for h in range(h_)]
        consts = (_chunk_rows(cos_ref, gs), _chunk_rows(sin_ref, gs), lead(dm_ref), lead(qd_ref), lead(kd_ref),
                  lead(cd_ref))
        f = lambda q, k, v, g, w_, r_: _ret_chunk(q, k, v, g, *consts, w_, r_)
        _, vjp = jax.vjp(f, _head_tiles(q_ref, gs), _head_tiles(k_ref, gs), _head_tiles(v_ref, gs),
                         _head_tiles(g_ref, gs), nw_ref[...], lead(st_ref))
        dq, dk, dv, dg, dnw, dr = vjp((_head_tiles(do_ref, gs), lead(dr_ref)))
        for h in range(h_):
            for j in range(gs):
                rws = slice(j * CHUNK, (j + 1) * CHUNK)
                for part, d in enumerate((dq, dk, dv, dg)):
                    col = part * hw + h * LANE
                    dqkvg_ref[rws, col:col + LANE] = d[j][h].astype(dqkvg_ref.dtype)
            dr_ref[h] = dr[h]
        dnw_ref[...] += dnw

    r = lambda c: steps - 1 - c
    blk = lambda name: pl.BlockSpec((rows, hw), lambda c: (r(c), _MINE[name] // hw))
    tab = pl.BlockSpec((rows, LANE), lambda c: (r(c), 0))
    full = lambda *s: pl.BlockSpec(s, lambda c: (0,) * len(s))
    assert _MINE["rq"] % (4 * hw) == 0 and [_MINE[n] - _MINE["rq"] for n in ("rk", "rv", "rg")] == [hw, 2 * hw, 3 * hw]
    return pl.pallas_call(
        body, name="ret_bwd", grid=(steps,),
        in_specs=[blk("rq"), blk("rk"), blk("rv"), blk("rg"), tab, tab,
                  full(h_, CHUNK, CHUNK), full(h_, CHUNK, 1), full(h_, CHUNK, 1), full(h_, 1, 1), full(1, LANE),
                  pl.BlockSpec((h_, None, LANE, LANE), lambda c: (0, r(c), 0, 0)),
                  pl.BlockSpec((rows, hw), lambda c: (r(c), (MIX_W - hw) // hw)),
                  pl.BlockSpec(memory_space=pl.ANY)],
        out_specs=[pl.BlockSpec((rows, 4 * hw), lambda c: (r(c), _MINE["rq"] // (4 * hw))), full(1, LANE)],
        out_shape=[jax.ShapeDtypeStruct(dproj.shape, dproj.dtype), jax.ShapeDtypeStruct((1, LANE), F32)],
        input_output_aliases={13: 0},
        scratch_shapes=[pltpu.VMEM((h_, LANE, LANE), F32)],
        compiler_params=_cparams(("arbitrary",), VMEM_LIMIT),
    )(proj, proj, proj, proj, cos2, sin2, dmat, qdec, kdec, cdec, nw, st, dmix, dproj)


def _mesh_pos():
    x, y, c = lax.axis_index("x"), lax.axis_index("y"), lax.axis_index("c")
    return x, y, c, 4 * x + 2 * y + c


def _peer(x, y, c, mask):
    return (x ^ ((mask >> 2) & 1), y ^ ((mask >> 1) & 1), c ^ (mask & 1))


def _gather_plan(ins, outs, send_sems, recv_sems, local_sems):
    x, y, c, me = _mesh_pos()
    n = len(ins)
    idx = lambda px, py, pc: 4 * px + 2 * py + pc
    sib = (x, y, 1 - c)
    chips = [(1 - x, y), (x, 1 - y), (1 - x, 1 - y)]
    local = [pltpu.make_async_copy(ins[a], outs[a].at[me], local_sems.at[a]) for a in range(n)]

    def copy(a, k, src, slab, to):
        return pltpu.make_async_remote_copy(
            src_ref=src, dst_ref=outs[a].at[slab], send_sem=send_sems.at[7 * a + k],
            recv_sem=recv_sems.at[7 * a + k], device_id=to, device_id_type=pl.DeviceIdType.MESH)

    first = [copy(a, 0, ins[a], me, sib) for a in range(n)]
    first += [copy(a, 1 + j, ins[a], me, (*chip, c)) for a in range(n) for j, chip in enumerate(chips)]
    landed = [[copy(a, 1 + j, ins[a], idx(*chip, c), (*chip, c)) for a in range(n)] for j, chip in enumerate(chips)]
    passed = [[copy(a, 4 + j, outs[a].at[idx(*chip, c)], idx(*chip, c), sib) for a in range(n)]
              for j, chip in enumerate(chips)]
    from_sib = [copy(a, 0, ins[a], idx(*sib), sib) for a in range(n)]
    from_sib += [copy(a, 4 + j, ins[a], idx(*chip, 1 - c), sib) for a in range(n) for j, chip in enumerate(chips)]

    def start():
        for cp in local + first:
            cp.start()

    def wait():
        for j in range(len(chips)):
            for a in range(n):
                landed[j][a].wait_recv()
                passed[j][a].start()
        for cp in from_sib:
            cp.wait_recv()
        for cp in first + [cp for row in passed for cp in row]:
            cp.wait_send()
        for cp in local:
            cp.wait()

    return start, wait


def _comm_plan(kind, ins, outs, send_sems, recv_sems, local_sems):
    if kind == "gather":
        return _gather_plan(ins, outs, send_sems, recv_sems, local_sems)
    x, y, c, me = _mesh_pos()
    n = len(ins)
    src = (lambda a, idx: ins[a]) if kind == "gather" else (lambda a, idx: ins[a].at[idx])
    local = [pltpu.make_async_copy(src(a, me), outs[a].at[me], local_sems.at[a]) for a in range(n)]
    sends, recvs = [], []
    for mask in range(1, N_DEV):
        px, py, pc = _peer(x, y, c, mask)
        pidx = 4 * px + 2 * py + pc
        for a in range(n):
            k = (mask - 1) * n + a
            copy = lambda s, d, k=k: pltpu.make_async_remote_copy(
                src_ref=s, dst_ref=d, send_sem=send_sems.at[k], recv_sem=recv_sems.at[k],
                device_id=(px, py, pc), device_id_type=pl.DeviceIdType.MESH)
            sends.append(copy(src(a, pidx), outs[a].at[me]))
            recvs.append(copy(src(a, me), outs[a].at[pidx]))

    def start():
        for cp in local + sends:
            cp.start()

    def wait():
        for cp in sends:
            cp.wait_send()
        for cp in recvs:
            cp.wait_recv()
        for cp in local:
            cp.wait()

    return start, wait


def _comm_out_shapes(kind, arrays):
    return [jax.ShapeDtypeStruct(((N_DEV,) + s.shape) if kind == "gather" else s.shape, s.dtype) for s in arrays]


def _comm_scratch(n):
    k_tot = (N_DEV - 1) * n
    return [pltpu.SemaphoreType.DMA((k_tot,)), pltpu.SemaphoreType.DMA((k_tot,)), pltpu.SemaphoreType.DMA((n,))]


def _comm_call(name, kind, arrays):
    n = len(arrays)

    def body(*refs):
        start, wait = _comm_plan(kind, refs[:n], refs[n:2 * n], *refs[2 * n:])
        start()
        wait()

    hbm = pl.BlockSpec(memory_space=pl.ANY)
    return pl.pallas_call(
        body, name=name, in_specs=[hbm] * n, out_specs=[hbm] * n, out_shape=_comm_out_shapes(kind, arrays),
        scratch_shapes=_comm_scratch(n), compiler_params=pltpu.CompilerParams(has_side_effects=True),
    )(*arrays)


def _adam(name, parts, w, m, v, tr, layer=None, prev=None):
    p_, r_, cw = parts.shape
    c1 = 1.0 - ADAM_B1 ** ADAM_STEP
    c2 = 1.0 - ADAM_B2 ** ADAM_STEP
    n_prev = 4 if prev else 0

    def body(p_ref, w_ref, m_ref, v_ref, *rest):
        g_ref, d_ref, nm_ref, nv_ref = rest[n_prev:]
        g = p_ref[0].astype(F32)
        for i in range(1, p_):
            g = g + p_ref[i].astype(F32)
        nm = ADAM_B1 * m_ref[...] + (1.0 - ADAM_B1) * g
        nv = ADAM_B2 * v_ref[...] + (1.0 - ADAM_B2) * (g * g)
        d_ref[...] = -ADAM_LR * ((nm / c1) / (jnp.sqrt(nv / c2) + ADAM_EPS) + ADAM_WD * w_ref[...])
        g_ref[...] = g
        nm_ref[...] = nm
        nv_ref[...] = nv

    if layer is None:
        spec = pl.BlockSpec((tr, cw), lambda i: (i, 0))
        shp = jax.ShapeDtypeStruct((r_, cw), F32)
    else:
        spec = pl.BlockSpec((None, tr, cw), lambda i: (layer, i, 0))
        shp = jax.ShapeDtypeStruct((DEPTH, r_, cw), F32)
    return pl.pallas_call(
        body, name=name, grid=(r_ // tr,),
        in_specs=[pl.BlockSpec((p_, tr, cw), lambda i: (0, i, 0)), spec, spec, spec]
                 + [pl.BlockSpec(memory_space=pl.ANY)] * n_prev,
        out_specs=[spec] * 4, out_shape=[shp] * 4,
        input_output_aliases={4 + i: i for i in range(n_prev)},
        compiler_params=_cparams(("parallel",), VMEM_LIMIT),
    )(parts, w, m, v, *(prev or ()))


def _perm_cols(w_full):
    parts = [w_full[:, _ORIG[n][0]:_ORIG[n][0] + _ORIG[n][1]] for n in _ORDER]
    parts.append(jnp.zeros((w_full.shape[0], N_PAD - N_USED), w_full.dtype))
    return jnp.concatenate(parts, axis=1)


def _unperm_cols(g):
    by_orig = sorted(_ORDER, key=lambda n: _ORIG[n][0])
    return jnp.concatenate([g[:, _MINE[n]:_MINE[n] + _ORIG[n][1]] for n in by_orig], axis=1)


def _ret_consts(length):
    lg = jnp.log(1.0 - 2.0 ** (-5.0 - jnp.arange(RET_HEADS, dtype=F32)))
    idx = jnp.arange(CHUNK, dtype=F32)
    rel = idx[:, None] - idx[None, :]
    dmat = jnp.where(rel[None] >= 0, jnp.exp(jnp.maximum(rel, 0.0)[None] * lg[:, None, None]), 0.0)
    qdec = jnp.exp((idx[:, None] + 1.0) * lg[None, :]).T[:, :, None]
    kdec = jnp.exp((CHUNK - 1.0 - idx)[:, None] * lg[None, :]).T[:, :, None]
    cdec = jnp.exp(CHUNK * lg)[:, None, None]
    half = RET_DK // 2
    inv = ROPE_BASE ** (-jnp.arange(half, dtype=F32) / half)
    ang = jnp.arange(length, dtype=jnp.int32).astype(F32)[:, None] * inv[None, :]
    cos, sin = jnp.cos(ang), jnp.sin(ang)
    return (jnp.concatenate([cos, cos], axis=1), jnp.concatenate([-sin, sin], axis=1), dmat, qdec, kdec, cdec)


def _tiles(length, pref=512):
    return pref if length % pref == 0 else length


def _layer_fwd(x, p, consts, length, last, target, comm=None):
    tm = _tiles(length)
    (h,), _ = _rowmap("pre_norm", lambda x_, w_: ((_rms(x_, w_),), ()), length, tm,
                      [(x, D_MODEL, 0)], [p["pre_norm"]], out_ws=[D_MODEL], out_dtypes=[BF16])
    proj = _matmul("in_proj", h, p["w_in"], "nn", length, N_PAD, D_MODEL)
    xc = _conv_fwd(proj, p["conv_w"], p["conv_b"], length, _tiles(length, 1024), 512)
    mixed, st_a, tinv, *got = _gdn_fwd(xc, proj, p["gdn_A_log"], p["gdn_dt_bias"], p["gdn_norm"], length, comm)
    y_b, mixed, st_b = _ssd_fwd(xc, proj, p["ssd_A_log"], p["ssd_dt_bias"], p["ssd_D"], p["ssd_norm"], mixed, length)
    mixed, st_c = _ret_fwd(proj, *consts, p["ret_norm"], mixed, length)
    w_out = p["w_out"](got) if callable(p["w_out"]) else p["w_out"]
    out = _matmul("out_proj", mixed, w_out, "nn", length, D_MODEL, MIX_W)
    res = dict(x=x, h=h, proj=proj, xc=xc, st_a=st_a, tinv=tinv, st_b=st_b, st_c=st_c, y_b=y_b, mixed=mixed,
               out=out, w_out=w_out)
    if not last:
        (y,), _ = _rowmap("post_norm", lambda x_, o_, w_: ((x_ + _rms(o_, w_),), ()), length, tm,
                          [(x, D_MODEL, 0), (out, D_MODEL, 0)], [p["post_norm"]], out_ws=[D_MODEL])
        return y, res, None, got

    def head(x_, o_, t_, w_):
        e = x_ + _rms(o_, w_) - t_
        row = jnp.mean(e * e, axis=-1, keepdims=True)
        loss = 0.5 * jnp.sum(row, axis=0, keepdims=True)
        return (e * (1.0 / D_MODEL),), (loss + jnp.zeros((1, LANE), F32),)

    (dy,), (loss,) = _rowmap("loss_head", head, length, tm,
                             [(x, D_MODEL, 0), (out, D_MODEL, 0), (target, D_MODEL, 0)], [p["post_norm"]],
                             out_ws=[D_MODEL], acc_ws=[LANE])
    return None, res, (dy, loss[0, 0]), got


def _layer_bwd(dy, p, res, consts, length, comm_scan=None, comm_dx=None):
    tm = _tiles(length)
    g = {}
    (d_out,), (g["post_norm"],) = _rowmap("post_norm_bwd", lambda o_, w_: ((_rms(o_, w_),), ()), length, tm,
                                          [(res["out"], D_MODEL, 0)], [p["post_norm"]], cts=[(dy, D_MODEL, 0)],
                                          d_dtypes=[BF16])
    dmix = _matmul("out_proj_dx", d_out, res["w_out"], "nt", length, MIX_W, D_MODEL)
    g["w_out"] = _matmul("out_proj_dw", res["mixed"], d_out, "tn", MIX_W, D_MODEL, length, out_dtype=BF16)
    comm = comm_scan(g) if comm_scan else None
    proj, xc = res["proj"], res["xc"]
    (dxc_a, dproj, dsm_a, g["gdn_A_log"], g["gdn_dt_bias"], g["gdn_norm"], *got) = _gdn_bwd(
        xc, proj, p["gdn_A_log"], p["gdn_dt_bias"], p["gdn_norm"], res["st_a"], res["tinv"], dmix, length, comm)
    dxc_b, dproj, dsm_b, g["ssd_A_log"], g["ssd_dt_bias"], g["ssd_D"], g["ssd_norm"] = _ssd_bwd(
        xc, proj, p["ssd_A_log"], p["ssd_dt_bias"], p["ssd_D"], p["ssd_norm"], res["st_b"], res["y_b"], dmix,
        dproj, length)
    dproj, g["ret_norm"] = _ret_bwd(proj, *consts, p["ret_norm"], res["st_c"], dmix, dproj, length)
    tmc = _tiles(length, 1024)
    dpre, g["conv_w"], g["conv_b"] = _conv_bwd_pre(proj, p["conv_w"], p["conv_b"], dxc_a, dxc_b, length, tmc, 512)
    dproj = _conv_bwd_x(dpre, p["conv_w"], dproj, length, tmc, 512)
    pad = SMALL_OFF % 512 + 512 - LANE
    (dproj,), _ = _rowmap("dsmall", lambda a_, b_, c_: ((jnp.concatenate([a_ + b_ + c_, jnp.zeros((a_.shape[0], pad), F32)], axis=1),), ()),
                          length, tm, [(dsm_a, LANE, 0), (dsm_b[0], LANE, 0), (dsm_b[1], LANE, 0)], [],
                          out_ws=[512], out_dtypes=[BF16], place={0: (dproj, N_PAD, SMALL_OFF // 512)})
    g["w_in"] = _matmul("in_proj_dw", res["h"], dproj, "tn", D_MODEL, N_PAD, length, out_dtype=BF16)
    if comm_dx:
        dh, *got_dx = _matmul("in_proj_dx_scatter", dproj, p["w_in"], "nt", length, D_MODEL, N_PAD, comm=comm_dx(g))
    else:
        dh, got_dx = _matmul("in_proj_dx", dproj, p["w_in"], "nt", length, D_MODEL, N_PAD), []
    (dx,), (g["pre_norm"],) = _rowmap(
        "pre_norm_bwd", lambda x_, w_: ((_rms(x_, w_), x_), ()), length, tm,
        [(res["x"], D_MODEL, 0)], [p["pre_norm"]], cts=[(dh, D_MODEL, 0), (dy, D_MODEL, 0)])
    return dx, g, got, got_dx


def _lane_row(vals, piece):
    lo = _MINE[piece] - SMALL_OFF
    return jnp.pad(vals[None], ((0, 0), (lo, LANE - lo - vals.shape[0])))


def _from_lane_row(row, piece, n):
    lo = _MINE[piece] - SMALL_OFF
    return row.reshape(-1, LANE).sum(axis=0)[lo:lo + n]


def _make_layer(l, w_in_full, w_out_full, conv_full, sw):
    return dict(
        pre_norm=sw["pre_norm"][l][None], post_norm=sw["post_norm"][l][None], w_in=_perm_cols(w_in_full),
        w_out=w_out_full,
        conv_w=jnp.concatenate([conv_full[0], conv_full[1]], axis=1),
        conv_b=jnp.concatenate([jnp.zeros((CONV_CH,), F32), sw["ssd_conv_b"][l]])[None],
        gdn_A_log=_lane_row(sw["gdn_A_log"][l], "ga"), gdn_dt_bias=_lane_row(sw["gdn_dt_bias"][l], "ga"),
        gdn_norm=sw["gdn_norm"][l][None],
        ssd_A_log=_lane_row(sw["ssd_A_log"][l], "sdt"), ssd_dt_bias=_lane_row(sw["ssd_dt_bias"][l], "sdt"),
        ssd_D=_lane_row(sw["ssd_D"][l], "sdt"),
        ssd_norm=sw["ssd_norm"][l].reshape(SSD_GROUPS, 1, -1),
        ret_norm=sw["ret_norm"][l][None])


def _local_step(xb, tgt, layer0, make_layer1, length, fwd_comm=None, comm_scan=None, comm_dx=None):
    consts = _ret_consts(length)
    y0, res0, _, got = _layer_fwd(xb, layer0, consts, length, False, tgt, fwd_comm)
    layer1 = make_layer1(got)
    _, res1, (dy, loss_local), _ = _layer_fwd(y0, layer1, consts, length, True, tgt)
    dy, grads1, _, _ = _layer_bwd(dy, layer1, res1, consts, length)
    dx, grads0, recv_scan, recv_dx = _layer_bwd(
        dy, layer0, res0, consts, length,
        (lambda g0: comm_scan(grads1, g0)) if comm_scan else None, comm_dx)
    return loss_local, dx, [grads0, grads1], recv_scan, recv_dx


_SMALL = ["pre_norm", "post_norm", "gdn_A_log", "gdn_dt_bias", "gdn_norm", "ssd_conv_b", "ssd_A_log",
          "ssd_dt_bias", "ssd_D", "ssd_norm", "ret_norm"]


def _pack_small(arrs):
    rows = []
    for n in _SMALL:
        flat = arrs[n].reshape(-1)
        pad = (-flat.shape[0]) % LANE
        rows.append(jnp.pad(flat, (0, pad)).reshape(-1, LANE))
    out = jnp.concatenate(rows, axis=0)
    return jnp.pad(out, ((0, (-out.shape[0]) % SUBLANE), (0, 0)))


def _unpack_small(packed, like):
    out, r = {}, 0
    for n in _SMALL:
        cnt = like[n].size
        nrow = -(-cnt // LANE)
        out[n] = packed[r:r + nrow].reshape(-1)[:cnt].reshape(like[n].shape)
        r += nrow
    return out


def kernel(x, pre_norm, post_norm, w_in, gdn_conv, gdn_A_log, gdn_dt_bias, gdn_norm, ssd_conv, ssd_conv_b, ssd_A_log, ssd_dt_bias, ssd_D, ssd_norm, ret_norm, w_out, loss_target, m_pre_norm, m_post_norm, m_w_in, m_gdn_conv, m_gdn_A_log, m_gdn_dt_bias, m_gdn_norm, m_ssd_conv, m_ssd_conv_b, m_ssd_A_log, m_ssd_dt_bias, m_ssd_D, m_ssd_norm, m_ret_norm, m_w_out, v_pre_norm, v_post_norm, v_w_in, v_gdn_conv, v_gdn_A_log, v_gdn_dt_bias, v_gdn_norm, v_ssd_conv, v_ssd_conv_b, v_ssd_A_log, v_ssd_dt_bias, v_ssd_D, v_ssd_norm, v_ret_norm, v_w_out):
    length = x.shape[1]
    xb = x[0]
    tgt = loss_target[0]
    small_w = dict(pre_norm=pre_norm, post_norm=post_norm, gdn_A_log=gdn_A_log, gdn_dt_bias=gdn_dt_bias,
                   gdn_norm=gdn_norm, ssd_conv_b=ssd_conv_b, ssd_A_log=ssd_A_log, ssd_dt_bias=ssd_dt_bias,
                   ssd_D=ssd_D, ssd_norm=ssd_norm, ret_norm=ret_norm)
    small_m = dict(pre_norm=m_pre_norm, post_norm=m_post_norm, gdn_A_log=m_gdn_A_log, gdn_dt_bias=m_gdn_dt_bias,
                   gdn_norm=m_gdn_norm, ssd_conv_b=m_ssd_conv_b, ssd_A_log=m_ssd_A_log, ssd_dt_bias=m_ssd_dt_bias,
                   ssd_D=m_ssd_D, ssd_norm=m_ssd_norm, ret_norm=m_ret_norm)
    small_v = dict(pre_norm=v_pre_norm, post_norm=v_post_norm, gdn_A_log=v_gdn_A_log, gdn_dt_bias=v_gdn_dt_bias,
                   gdn_norm=v_gdn_norm, ssd_conv_b=v_ssd_conv_b, ssd_A_log=v_ssd_A_log, ssd_dt_bias=v_ssd_dt_bias,
                   ssd_D=v_ssd_D, ssd_norm=v_ssd_norm, ret_norm=v_ret_norm)

    w_in_b, w_out_b = w_in.astype(BF16), w_out.astype(BF16)
    conv_shard = jnp.stack([gdn_conv, ssd_conv], axis=1)
    full_out = lambda g_out: g_out.reshape(MIX_W, D_MODEL)

    def assemble(l, g_in, w_out_full, g_conv):
        w_in_full = g_in.transpose(1, 0, 2).reshape(D_MODEL, N_IN)
        conv_full = g_conv.transpose(1, 2, 0, 3).reshape(2, CONV_W, CONV_CH)
        return _make_layer(l, w_in_full, w_out_full, conv_full, small_w)

    slab_in = lambda g: _unperm_cols(g["w_in"]).reshape(D_MODEL, N_DEV, SHARD_IN).transpose(1, 0, 2).astype(BF16)
    slab_out = lambda g: g["w_out"].reshape(N_DEV, SHARD_OUT, D_MODEL).astype(BF16)
    slab_conv = lambda g: jnp.stack(
        [g["conv_w"][:, k * CONV_CH:(k + 1) * CONV_CH].reshape(CONV_W, N_DEV, SHARD_CONV).transpose(1, 0, 2)
         for k in range(2)], axis=1).reshape(N_DEV, 2 * CONV_W, SHARD_CONV)

    g_in0, g_conv0 = _comm_call("gather_layer0", "gather", [w_in_b[0], conv_shard[0]])
    layer0 = assemble(0, g_in0, lambda got: full_out(got[3]), g_conv0)
    loss_local, dx, grads, recv_scan, recv_dx = _local_step(
        xb, tgt, layer0, lambda got: assemble(1, got[0], full_out(got[1]), got[2]), length,
        fwd_comm=("gather", [w_in_b[1], w_out_b[1], conv_shard[1], w_out_b[0]]),
        comm_scan=lambda g1, g0: ("scatter", [slab_in(g1), slab_out(g1), slab_conv(g1), slab_out(g0)]),
        comm_dx=lambda g0: ("scatter", [slab_in(g0), slab_conv(g0)]))
    grad_x = dx[None]
    loss = lax.psum(loss_local, ("x", "y", "c"))

    small_g = dict(
        pre_norm=jnp.concatenate([grads[l]["pre_norm"] for l in range(DEPTH)], axis=0),
        post_norm=jnp.concatenate([grads[l]["post_norm"] for l in range(DEPTH)], axis=0),
        gdn_A_log=jnp.stack([_from_lane_row(grads[l]["gdn_A_log"], "ga", GDN_HEADS) for l in range(DEPTH)]),
        gdn_dt_bias=jnp.stack([_from_lane_row(grads[l]["gdn_dt_bias"], "ga", GDN_HEADS) for l in range(DEPTH)]),
        gdn_norm=jnp.concatenate([grads[l]["gdn_norm"] for l in range(DEPTH)], axis=0),
        ssd_conv_b=jnp.concatenate([grads[l]["conv_b"][:, CONV_CH:] for l in range(DEPTH)], axis=0),
        ssd_A_log=jnp.stack([_from_lane_row(grads[l]["ssd_A_log"], "sdt", SSD_HEADS) for l in range(DEPTH)]),
        ssd_dt_bias=jnp.stack([_from_lane_row(grads[l]["ssd_dt_bias"], "sdt", SSD_HEADS) for l in range(DEPTH)]),
        ssd_D=jnp.stack([_from_lane_row(grads[l]["ssd_D"], "sdt", SSD_HEADS) for l in range(DEPTH)]),
        ssd_norm=jnp.concatenate([grads[l]["ssd_norm"].reshape(1, -1) for l in range(DEPTH)], axis=0),
        ret_norm=jnp.concatenate([grads[l]["ret_norm"] for l in range(DEPTH)], axis=0))
    gs = _pack_small(small_g)
    gs8 = jnp.broadcast_to(gs[None], (N_DEV,) + gs.shape)
    (r_small,) = _comm_call("exchange_small", "scatter", [gs8])
    recv = [[recv_dx[0], recv_scan[3], recv_dx[1]], recv_scan[:3]]

    conv_w = lambda g_, s_, l: jnp.stack([g_[l], s_[l]], axis=0).reshape(2 * CONV_W, SHARD_CONV)
    o_in, o_out, o_conv = None, None, []
    for l in range(DEPTH):
        o_in = _adam(f"adam_w_in{l}", recv[l][0], w_in, m_w_in, v_w_in, 128, l, o_in)
        o_out = _adam(f"adam_w_out{l}", recv[l][1], w_out, m_w_out, v_w_out, 128, l, o_out)
        o_conv.append(_adam(f"adam_conv{l}", recv[l][2], conv_w(gdn_conv, ssd_conv, l),
                            conv_w(m_gdn_conv, m_ssd_conv, l), conv_w(v_gdn_conv, v_ssd_conv, l), 2 * CONV_W))
    ps_w, ps_m, ps_v = _pack_small(small_w), _pack_small(small_m), _pack_small(small_v)
    o_small = _adam("adam_small", r_small, ps_w, ps_m, ps_v, ps_w.shape[0])

    names = ["pre_norm", "post_norm", "w_in", "gdn_conv", "gdn_A_log", "gdn_dt_bias", "gdn_norm", "ssd_conv",
             "ssd_conv_b", "ssd_A_log", "ssd_dt_bias", "ssd_D", "ssd_norm", "ret_norm", "w_out"]
    outs = []
    for kind in range(4):
        d = _unpack_small(o_small[kind], small_w)
        cv = jnp.stack([o_conv[l][kind].reshape(2, CONV_W, SHARD_CONV) for l in range(DEPTH)], axis=1)
        d["w_in"] = o_in[kind]
        d["w_out"] = o_out[kind]
        d["gdn_conv"] = cv[0]
        d["ssd_conv"] = cv[1]
        outs.extend(d[n] for n in names)
    return (loss, grad_x, *outs)
```

```python
import functools
import math

import jax
import jax.numpy as jnp
from jax import lax
from jax.experimental import pallas as pl
from jax.experimental.pallas import tpu as pltpu

F32 = jnp.float32
BF16 = jnp.bfloat16

D_MODEL = 1024
DEPTH = 2
CHUNK = 64
CONV_W = 4
EPS = 1e-6
N_DEV = 8

GDN_HEADS = 4
GDN_DK = 128
SSD_HEADS = 16
SSD_P = 64
SSD_N = 128
SSD_GROUPS = 2
SSD_PAIRS = SSD_HEADS // 2
PAIRS_PER_GROUP = SSD_PAIRS // SSD_GROUPS
RET_HEADS = 4
RET_DK = 128
ROPE_BASE = 10000.0
MIX_W = 2048
N_IN = 6680
SHARD_IN = N_IN // N_DEV
SHARD_OUT = MIX_W // N_DEV
CONV_CH = 1536
SHARD_CONV = CONV_CH // N_DEV

ADAM_LR = 0.001
ADAM_B1 = 0.9
ADAM_B2 = 0.999
ADAM_EPS = 1e-08
ADAM_WD = 0.01
ADAM_STEP = 10

LANE = 128
SUBLANE = 8
VMEM_LIMIT = 56 * 1024 * 1024

_ORIG = dict(gq=(0, 512), gk=(512, 512), gv=(1024, 512), gz=(1536, 512), gb=(2048, 4), ga=(2052, 4),
             sx=(2056, 1024), sB=(3080, 256), sC=(3336, 256), sz=(3592, 1024), sdt=(4616, 16),
             rq=(4632, 512), rk=(5144, 512), rv=(5656, 512), rg=(6168, 512))
_ORDER = ["rq", "rk", "rv", "rg", "gq", "gk", "gv", "sx", "sB", "sC", "gz", "sz", "gb", "ga", "sdt"]
_MINE = {}
_off = 0
for _n in _ORDER:
    _MINE[_n] = _off
    _off += _ORIG[_n][1]
N_USED = _off
N_PAD = 7168
CONV_ALL = 2 * CONV_CH
SMALL_OFF = _MINE["gb"]
CONV_OFF = _MINE["gq"]
XC = lambda name: _MINE[name] - CONV_OFF


def _cparams(sem, vmem=None):
    return pltpu.CompilerParams(dimension_semantics=sem, vmem_limit_bytes=vmem)


def _split_bf16(a):
    hi = a.astype(BF16)
    return hi, (a - hi.astype(F32)).astype(BF16)


def _make_mm():
    def raw(a, b, ca, cb):
        return lax.dot_general(a.astype(BF16), b.astype(BF16), (((ca,), (cb,)), ((), ())),
                               preferred_element_type=F32)

    @jax.custom_vjp
    def nn(a, b):
        return raw(a, b, 1, 0)

    @jax.custom_vjp
    def nt(a, b):
        return raw(a, b, 1, 1)

    @jax.custom_vjp
    def tn(a, b):
        return raw(a, b, 0, 0)

    nn.defvjp(lambda a, b: (raw(a, b, 1, 0), (a, b)), lambda r, g: (nt(g, r[1]), tn(r[0], g)))
    nt.defvjp(lambda a, b: (raw(a, b, 1, 1), (a, b)), lambda r, g: (nn(g, r[1]), tn(g, r[0])))
    tn.defvjp(lambda a, b: (raw(a, b, 0, 0), (a, b)), lambda r, g: (nt(r[1], g), nn(r[0], g)))
    return nn, nt, tn


_nn, _nt, _tn = _make_mm()


@jax.custom_vjp
def _swap_halves(t):
    return pltpu.roll(t, LANE // 2, 1)


_swap_halves.defvjp(lambda t: (pltpu.roll(t, LANE // 2, 1), None),
                    lambda _, g: (pltpu.roll(g, LANE // 2, 1),))


@jax.custom_vjp
def _split_rows(x):
    return tuple(x[i * CHUNK:(i + 1) * CHUNK] for i in range(x.shape[0] // CHUNK))


_split_rows.defvjp(lambda x: (tuple(x[i * CHUNK:(i + 1) * CHUNK] for i in range(x.shape[0] // CHUNK)), None),
                   lambda _, gs: (jnp.concatenate(gs, axis=0),))


def _dot3(a, b, ca, cb):
    dot = lambda x, y: lax.dot_general(x, y, (((ca,), (cb,)), ((), ())), preferred_element_type=F32)
    return dot(a[0], b[0]) + (dot(a[0], b[1]) + dot(a[1], b[0]))


@jax.custom_vjp
def _tri_inv(mats):
    return _tri_inv_impl(mats)


def _tri_inv_impl(mats):
    ii = lax.broadcasted_iota(jnp.int32, mats[0].shape, 0)
    jj = lax.broadcasted_iota(jnp.int32, mats[0].shape, 1)
    eye = jnp.where(ii == jj, 1.0, 0.0).astype(F32)
    ts = [eye - a for a in mats]
    ps = [_split_bf16(-a) for a in mats]
    for _ in range(int(math.log2(CHUNK)) - 1):
        ps = [_split_bf16(_dot3(p, p, 1, 0)) for p in ps]
        ts = [t + _dot3(_split_bf16(t), p, 1, 0) for t, p in zip(ts, ps)]
    return ts


def _tri_inv_bwd(ts, gs):
    tsp = [_split_bf16(t) for t in ts]
    xs = [_dot3(t, _split_bf16(g), 0, 0) for t, g in zip(tsp, gs)]
    return ([-_dot3(_split_bf16(x), t, 1, 1) for x, t in zip(xs, tsp)],)


_tri_inv.defvjp(lambda mats: (lambda ts: (ts, ts))(_tri_inv_impl(mats)), _tri_inv_bwd)


@jax.custom_vjp
def _tri_inv_saved(mats, ts):
    return ts


_tri_inv_saved.defvjp(lambda mats, ts: (ts, ts),
                      lambda ts, gs: (_tri_inv_bwd(ts, gs)[0], [jnp.zeros_like(t) for t in ts]))


def _silu(x):
    return x * jax.nn.sigmoid(x)


@jax.custom_vjp
def _softplus(x):
    return jnp.maximum(x, 0.0) + jnp.log1p(jnp.exp(-jnp.abs(x)))


_softplus.defvjp(lambda x: (jnp.maximum(x, 0.0) + jnp.log1p(jnp.exp(-jnp.abs(x))), x),
                 lambda x, g: (g * jax.nn.sigmoid(x),))


def _rms(x, w):
    return x * lax.rsqrt(jnp.mean(x * x, axis=-1, keepdims=True) + EPS) * w


def _chunk_masks(n):
    ii = lax.broadcasted_iota(jnp.int32, (n, n), 0)
    jj = lax.broadcasted_iota(jnp.int32, (n, n), 1)
    return ii >= jj, ii > jj, ii == jj, ii <= jj


def _cumsum_col(g, causal, eye, upper):
    g_row = jnp.sum(jnp.where(eye, g, 0.0), axis=0, keepdims=True)
    col = jnp.sum(jnp.where(causal, g_row, 0.0), axis=1, keepdims=True)
    row = jnp.sum(jnp.where(upper, g, 0.0), axis=0, keepdims=True)
    return col, row


def _lane_col(block, lane):
    pick = lax.broadcasted_iota(jnp.int32, (1, block.shape[1]), 1) == lane
    return jnp.sum(jnp.where(pick, block, 0.0), axis=1, keepdims=True)


def _gdn_chunk(q, k, v, gz, sm, alog_row, dtb_row, nw, s, t_saved=None):
    subs, nh = range(len(q)), len(q[0])
    c = q[0][0].shape[0]
    causal, _, eye, upper = _chunk_masks(c)
    stack = lambda xs: jnp.concatenate(xs, axis=0)
    per = lambda f: [[f(j, h) for h in range(nh)] for j in subs]
    qn = per(lambda j, h: q[j][h] * lax.rsqrt(jnp.sum(q[j][h] * q[j][h], axis=-1, keepdims=True) + EPS)
             * (GDN_DK ** -0.5))
    kn = per(lambda j, h: k[j][h] * lax.rsqrt(jnp.sum(k[j][h] * k[j][h], axis=-1, keepdims=True) + EPS))
    beta_blk = [jax.nn.sigmoid(sm[j]) for j in subs]
    g_blk = [-jnp.exp(alog_row) * _softplus(sm[j] + dtb_row) for j in subs]
    beta = per(lambda j, h: _lane_col(beta_blk[j], _MINE["gb"] - SMALL_OFF + h))
    g = per(lambda j, h: _lane_col(g_blk[j], _MINE["ga"] - SMALL_OFF + h))
    gcum = per(lambda j, h: _cumsum_col(g[j][h], causal, eye, upper)[0])
    eg = per(lambda j, h: jnp.exp(gcum[j][h]))
    glast = per(lambda j, h: jnp.sum(g[j][h], axis=0, keepdims=True))
    kb = per(lambda j, h: kn[j][h] * beta[j][h])
    n = nh * c
    ii = lax.broadcasted_iota(jnp.int32, (n, n), 0)
    jj = lax.broadcasted_iota(jnp.int32, (n, n), 1)
    sh = int(math.log2(c))
    same = lax.shift_right_logical(ii, sh) == lax.shift_right_logical(jj, sh)
    causal_bd = jnp.logical_and(same, ii >= jj)
    strict_bd = jnp.logical_and(same, ii > jj)
    gcum_all = [stack(gcum[j]) for j in subs]
    gcum_row = [jnp.sum(jnp.where(ii == jj, gcum_all[j], 0.0), axis=0, keepdims=True) for j in subs]
    decay = [jnp.where(causal_bd, jnp.exp(jnp.where(causal_bd, gcum_all[j] - gcum_row[j], 0.0)), 0.0) for j in subs]
    kn_all = [stack(kn[j]) for j in subs]
    a_low = [jnp.where(strict_bd, _nt(stack(kb[j]), kn_all[j]) * decay[j], 0.0) for j in subs]
    t = _tri_inv(a_low) if t_saved is None else _tri_inv_saved(a_low, t_saved)
    u = [_nn(t[j], stack([v[j][h] * beta[j][h] for h in range(nh)])) for j in subs]
    w = [_split_rows(_nn(t[j], stack([kb[j][h] * eg[j][h] for h in range(nh)]))) for j in subs]
    attn = [_nt(stack(qn[j]), kn_all[j]) * decay[j] for j in subs]
    kdec = per(lambda j, h: kn[j][h] * jnp.exp(glast[j][h] - gcum[j][h]))
    on = []
    for j in subs:
        v_new_all = u[j] - stack([_nn(w[j][h], s[h]) for h in range(nh)])
        o = _split_rows(stack([_nn(qn[j][h] * eg[j][h], s[h]) for h in range(nh)]) + _nn(attn[j], v_new_all))
        v_new = _split_rows(v_new_all)
        s = [s[h] * jnp.exp(glast[j][h]) + _tn(kdec[j][h], v_new[h]) for h in range(nh)]
        on.append([_rms(o[h], nw) * _silu(gz[j][h]) for h in range(nh)])
    return on, s, t


def _ssd_chunk(lanes, x, bm, cm, sm, alog_row, dtb_row, d_row, hs):
    subs, pairs = range(len(x)), range(len(x[0]))
    c = x[0][0].shape[0]
    lane_i = lax.broadcasted_iota(jnp.int32, (c, LANE), 1)
    lane_lo = lane_i < SSD_P
    lane_lo1 = lax.broadcasted_iota(jnp.int32, (1, LANE), 1) < SSD_P
    causal2 = lax.broadcasted_iota(jnp.int32, (c, LANE), 0) >= jnp.bitwise_and(lane_i, c - 1)
    eye2 = lax.broadcasted_iota(jnp.int32, (LANE, LANE), 0) == lax.broadcasted_iota(jnp.int32, (LANE, LANE), 1)
    to_row = lambda cols: jnp.sum(jnp.where(eye2, jnp.concatenate(cols, axis=0), 0.0), axis=0, keepdims=True)
    per = lambda f: [[f(j, p) for p in pairs] for j in subs]
    both = lambda blk, p: [_lane_col(blk, lanes[p][h]) for h in range(2)]
    dt_blk = [_softplus(sm[j] + dtb_row) for j in subs]
    a_blk = [dt_blk[j] * -jnp.exp(alog_row) for j in subs]
    alast_blk = [jnp.sum(a_blk[j], axis=0, keepdims=True) for j in subs]
    dt = per(lambda j, p: both(dt_blk[j], p))
    nexp = [both(-jnp.exp(alog_row), p) for p in pairs]
    a = per(lambda j, p: [dt[j][p][h] * nexp[p][h] for h in range(2)])
    alast = per(lambda j, p: both(alast_blk[j], p))
    dp = [both(d_row, p) for p in pairs]
    a_row = per(lambda j, p: to_row(a[j][p]))
    part = per(lambda j, p: jnp.where(causal2, a_row[j][p], 0.0))
    acum = per(lambda j, p: [jnp.sum(jnp.where(lane_lo, part[j][p], 0.0), axis=1, keepdims=True),
                             jnp.sum(jnp.where(lane_lo, 0.0, part[j][p]), axis=1, keepdims=True)])
    acum_row = per(lambda j, p: to_row(acum[j][p]))
    acum_col = per(lambda j, p: jnp.where(lane_lo, acum[j][p][0], acum[j][p][1]))
    lmat = per(lambda j, p: jnp.where(causal2, jnp.exp(jnp.where(causal2, acum_col[j][p] - acum_row[j][p], 0.0)), 0.0))
    cb = [_nt(cm[j], jnp.concatenate([bm[j], bm[j]], axis=0)) for j in subs]
    xdt = per(lambda j, p: x[j][p] * jnp.where(lane_lo, dt[j][p][0], dt[j][p][1]))
    xdt_rows = per(lambda j, p: jnp.concatenate([jnp.where(lane_lo, xdt[j][p], 0.0),
                                                 jnp.where(lane_lo, 0.0, xdt[j][p])], axis=0))
    intra = per(lambda j, p: _nn(cb[j] * lmat[j][p], xdt_rows[j][p]))
    skip = per(lambda j, p: x[j][p] * jnp.where(lane_lo1, dp[p][0], dp[p][1]))
    eacc = per(lambda j, p: jnp.exp(acum_col[j][p]))
    alast_row = per(lambda j, p: jnp.where(lane_lo1, alast[j][p][0], alast[j][p][1]))
    wdec = per(lambda j, p: jnp.exp(alast_row[j][p] - acum_col[j][p]))
    scale = per(lambda j, p: jnp.exp(alast_row[j][p]))
    upd = per(lambda j, p: _tn(bm[j], xdt[j][p] * wdec[j][p]))
    y = []
    for j in subs:
        y.append([skip[j][p] + intra[j][p] + _nn(cm[j], hs[p]) * eacc[j][p] for p in pairs])
        hs = [hs[p] * scale[j][p] + upd[j][p] for p in pairs]
    return y, hs


def _ret_chunk(rq, rk, rv, rg, cos2, sin2, dmat, qdec, kdec, cdec, nw, r):
    subs, hs = range(len(rq)), range(len(rq[0]))
    per = lambda f: [[f(j, h) for h in hs] for j in subs]
    q = per(lambda j, h: rq[j][h] * cos2[j] + _swap_halves(rq[j][h]) * sin2[j])
    k = per(lambda j, h: (rk[j][h] * cos2[j] + _swap_halves(rk[j][h]) * sin2[j]) * (RET_DK ** -0.5))
    s = per(lambda j, h: _nt(q[j][h], k[j][h]) * dmat[h])
    intra = per(lambda j, h: _nn(s[j][h], rv[j][h]))
    upd = per(lambda j, h: _tn(k[j][h] * kdec[h], rv[j][h]))
    gate = per(lambda j, h: _silu(rg[j][h]))
    on = []
    for j in subs:
        on.append([_rms(intra[j][h] + _nn(q[j][h], r[h]) * qdec[h], nw) * gate[j][h] for h in hs])
        r = [r[h] * cdec[h] + upd[j][h] for h in hs]
    return on, r


MM_TILE = 1024


def _tile(n, pref=MM_TILE):
    return pref if n % pref == 0 else n


def _matmul(name, a, b, mode, m, n, k, tk=None, comm=None, out_dtype=F32):
    tm, tn = _tile(m), _tile(n)
    tk = _tile(k) if tk is None else tk
    nk = k // tk
    ca, cb = {"nn": (1, 0), "nt": (1, 1), "tn": (0, 0)}[mode]
    grid = (m // tm, n // tn, nk)
    n_comm = len(comm[1]) if comm else 0
    use_acc = out_dtype != F32 and nk > 1

    def at_step(which):
        return functools.reduce(jnp.logical_and, [pl.program_id(d) == (0 if which == 0 else grid[d] - 1)
                                                  for d in range(3)])

    def body(*refs):
        a_ref, b_ref = refs[:2]
        c_in = refs[2:2 + n_comm]
        o_ref = refs[2 + n_comm]
        c_out = refs[3 + n_comm:3 + 2 * n_comm]
        acc_ref = refs[3 + 2 * n_comm] if use_acc else o_ref
        sems = refs[3 + 2 * n_comm + use_acc:]
        if comm:
            @pl.when(at_step(0))
            def _():
                _comm_plan(comm[0], c_in, c_out, *sems)[0]()

        part = lax.dot_general(a_ref[...].astype(BF16), b_ref[...].astype(BF16),
                               (((ca,), (cb,)), ((), ())), preferred_element_type=F32)
        if nk == 1:
            o_ref[...] = part.astype(o_ref.dtype)
        else:
            kk = pl.program_id(2)

            @pl.when(kk == 0)
            def _():
                acc_ref[...] = part

            @pl.when(kk > 0)
            def _():
                acc_ref[...] += part

            if use_acc:
                @pl.when(kk == nk - 1)
                def _():
                    o_ref[...] = acc_ref[...].astype(o_ref.dtype)

        if comm:
            @pl.when(at_step(1))
            def _():
                _comm_plan(comm[0], c_in, c_out, *sems)[1]()

    a_spec = pl.BlockSpec((tk, tm), lambda i, j, kk: (kk, i)) if mode == "tn" else pl.BlockSpec((tm, tk), lambda i, j, kk: (i, kk))
    b_spec = pl.BlockSpec((tn, tk), lambda i, j, kk: (j, kk)) if mode == "nt" else pl.BlockSpec((tk, tn), lambda i, j, kk: (kk, j))
    hbm = pl.BlockSpec(memory_space=pl.ANY)
    res = pl.pallas_call(
        body, name=name, grid=grid,
        in_specs=[a_spec, b_spec] + [hbm] * n_comm,
        out_specs=[pl.BlockSpec((tm, tn), lambda i, j, kk: (i, j))] + [hbm] * n_comm,
        out_shape=[jax.ShapeDtypeStruct((m, n), out_dtype)] + (_comm_out_shapes(*comm) if comm else []),
        scratch_shapes=([pltpu.VMEM((tm, tn), F32)] if use_acc else []) + (_comm_scratch(n_comm) if comm else []),
        compiler_params=_cparams(("arbitrary",) * 3 if comm else ("parallel", "parallel", "arbitrary"), VMEM_LIMIT),
    )(a, b, *(comm[1] if comm else []))
    return res if comm else res[0]


def _rowmap(name, fn, length, tm, rows, params, out_ws=(), acc_ws=(), out_dtypes=None, cts=None, d_dtypes=None,
            place=None):
    nt_ = length // tm
    n_r, n_p = len(rows), len(params)
    n_in = n_r + n_p
    rspec = lambda bw, cbk: pl.BlockSpec((tm, bw), lambda i: (i, cbk))
    pspec = lambda w: pl.BlockSpec((1, w), lambda i: (0, 0))
    in_arrays = [r[0] for r in rows] + list(params)
    in_specs = [rspec(r[1], r[2]) for r in rows] + [pspec(p.shape[1]) for p in params]
    place = place or {}

    def load(refs):
        return [r[...].astype(F32) for r in refs]

    def placed(widths, dtypes, n_before):
        specs, shapes, extra, alias = [], [], [], {}
        for k, (w, dt) in enumerate(zip(widths, dtypes)):
            arr, total, cbk = place.get(k, (None, w, 0))
            specs.append(rspec(w, cbk))
            shapes.append(jax.ShapeDtypeStruct((length, total), dt))
            if arr is not None:
                alias[n_before + len(extra)] = k
                extra.append(arr)
        return specs, shapes, extra, alias

    def accumulate(a_refs, vals):
        first = pl.program_id(0) == 0
        for a_ref, v in zip(a_refs, vals):
            @pl.when(first)
            def _(a_ref=a_ref):
                a_ref[...] = jnp.zeros_like(a_ref)
            a_ref[...] += v

    hbm = pl.BlockSpec(memory_space=pl.ANY)
    if cts is None:
        n_o = len(out_ws)
        o_specs, o_shapes, extra, alias = placed(out_ws, out_dtypes or [F32] * n_o, n_in)

        def body(*refs):
            outs, accs = fn(*load(refs[:n_in]))
            o_refs = refs[n_in + len(extra):]
            for o_ref, o in zip(o_refs[:n_o], outs):
                o_ref[...] = o.astype(o_ref.dtype)
            accumulate(o_refs[n_o:], accs)

        res = pl.pallas_call(
            body, name=name, grid=(nt_,), in_specs=in_specs + [hbm] * len(extra),
            out_specs=o_specs + [pspec(w) for w in acc_ws],
            out_shape=o_shapes + [jax.ShapeDtypeStruct((1, w), F32) for w in acc_ws],
            input_output_aliases=alias, compiler_params=_cparams(("arbitrary",), VMEM_LIMIT),
        )(*in_arrays, *extra)
        return res[:n_o], res[n_o:]

    n_c = len(cts)
    o_specs, o_shapes, extra, alias = placed([r[1] for r in rows], d_dtypes or [F32] * n_r, n_in + n_c)

    def body(*refs):
        ins = load(refs[:n_in])
        _, vjp = jax.vjp(lambda rs, ps: fn(*rs, *ps)[0], ins[:n_r], ins[n_r:])
        d_rows, d_params = vjp(tuple(load(refs[n_in:n_in + n_c])))
        o_refs = refs[n_in + n_c + len(extra):]
        for o_ref, d in zip(o_refs[:n_r], d_rows):
            o_ref[...] = d.astype(o_ref.dtype)
        accumulate(o_refs[n_r:], d_params)

    res = pl.pallas_call(
        body, name=name, grid=(nt_,),
        in_specs=in_specs + [rspec(c[1], c[2]) for c in cts] + [hbm] * len(extra),
        out_specs=o_specs + [pspec(p.shape[1]) for p in params],
        out_shape=o_shapes + [jax.ShapeDtypeStruct((1, p.shape[1]), F32) for p in params],
        input_output_aliases=alias, compiler_params=_cparams(("arbitrary",), VMEM_LIMIT),
    )(*in_arrays, *[c[0] for c in cts], *extra)
    return res[:n_r], res[n_r:]


def _conv_shift(ext, k, tm, forward):
    s = CONV_W - 1 - k
    if forward:
        rolled = ext if s == 0 else pltpu.roll(ext, s, 0)
        return rolled[SUBLANE:, :]
    rolled = ext if s == 0 else pltpu.roll(ext, tm + SUBLANE - s, 0)
    return rolled[:tm, :]


def _conv_pre(x_ref, halo_ref, w_ref, b_ref, first, tm):
    halo = jnp.where(first, 0.0, halo_ref[...])
    ext = jnp.concatenate([halo, x_ref[...]], axis=0)
    w = w_ref[...]
    pre = b_ref[...] + jnp.zeros_like(x_ref[...])
    taps = []
    for k in range(CONV_W):
        tap = _conv_shift(ext, k, tm, True)
        taps.append(tap)
        pre = pre + tap * w[k:k + 1, :]
    return pre, taps


def _conv_fwd(proj, w, b, length, tm, tc):
    hb = tm // SUBLANE
    j0 = CONV_OFF // tc

    def body(x_ref, halo_ref, w_ref, b_ref, o_ref):
        pre, _ = _conv_pre(x_ref, halo_ref, w_ref, b_ref, pl.program_id(0) == 0, tm)
        o_ref[...] = _silu(pre)

    return pl.pallas_call(
        body, name="conv_fwd", grid=(length // tm, CONV_ALL // tc),
        in_specs=[pl.BlockSpec((tm, tc), lambda i, j: (i, j0 + j)),
                  pl.BlockSpec((SUBLANE, tc), lambda i, j: (jnp.maximum(i * hb - 1, 0), j0 + j)),
                  pl.BlockSpec((CONV_W, tc), lambda i, j: (0, j)),
                  pl.BlockSpec((1, tc), lambda i, j: (0, j))],
        out_specs=pl.BlockSpec((tm, tc), lambda i, j: (i, j)),
        out_shape=jax.ShapeDtypeStruct((length, CONV_ALL), F32),
        compiler_params=_cparams(("parallel", "parallel"), VMEM_LIMIT),
    )(proj, proj, w, b)


def _conv_bwd_pre(proj, w, b, dxc_a, dxc_b, length, tm, tc):
    hb = tm // SUBLANE
    j0 = CONV_OFF // tc
    jh = CONV_CH // tc

    def body(x_ref, halo_ref, w_ref, b_ref, dya_ref, dyb_ref, dpre_ref, dw_ref, db_ref):
        i = pl.program_id(1)
        pre, taps = _conv_pre(x_ref, halo_ref, w_ref, b_ref, i == 0, tm)
        sg = jax.nn.sigmoid(pre)
        dy = jnp.where(pl.program_id(0) < jh, dya_ref[...], dyb_ref[...])
        dpre = dy * (sg * (1.0 + pre * (1.0 - sg)))
        dpre_ref[...] = dpre

        @pl.when(i == 0)
        def _():
            dw_ref[...] = jnp.zeros_like(dw_ref)
            db_ref[...] = jnp.zeros_like(db_ref)

        for k in range(CONV_W):
            dw_ref[k:k + 1, :] += jnp.sum(dpre * taps[k], axis=0, keepdims=True)
        db_ref[...] += jnp.sum(dpre, axis=0, keepdims=True)

    return pl.pallas_call(
        body, name="conv_bwd_pre", grid=(CONV_ALL // tc, length // tm),
        in_specs=[pl.BlockSpec((tm, tc), lambda j, i: (i, j0 + j)),
                  pl.BlockSpec((SUBLANE, tc), lambda j, i: (jnp.maximum(i * hb - 1, 0), j0 + j)),
                  pl.BlockSpec((CONV_W, tc), lambda j, i: (0, j)),
                  pl.BlockSpec((1, tc), lambda j, i: (0, j)),
                  pl.BlockSpec((tm, tc), lambda j, i: (jnp.where(j < jh, i, 0), jnp.minimum(j, jh - 1))),
                  pl.BlockSpec((tm, tc), lambda j, i: (jnp.where(j < jh, 0, i), jnp.maximum(j - jh, 0)))],
        out_specs=[pl.BlockSpec((tm, tc), lambda j, i: (i, j)),
                   pl.BlockSpec((CONV_W, tc), lambda j, i: (0, j)),
                   pl.BlockSpec((1, tc), lambda j, i: (0, j))],
        out_shape=[jax.ShapeDtypeStruct((length, CONV_ALL), F32),
                   jax.ShapeDtypeStruct((CONV_W, CONV_ALL), F32),
                   jax.ShapeDtypeStruct((1, CONV_ALL), F32)],
        compiler_params=_cparams(("parallel", "arbitrary"), VMEM_LIMIT),
    )(proj, proj, w, b, dxc_a, dxc_b)


def _conv_bwd_x(dpre, w, dproj, length, tm, tc):
    hb = tm // SUBLANE
    n_t = length // tm
    last_blk = length // SUBLANE - 1
    j0 = CONV_OFF // tc

    def body(d_ref, halo_ref, w_ref, _, o_ref):
        halo = jnp.where(pl.program_id(0) == n_t - 1, 0.0, halo_ref[...])
        ext = jnp.concatenate([d_ref[...], halo], axis=0)
        w = w_ref[...]
        acc = jnp.zeros_like(d_ref[...])
        for k in range(CONV_W):
            acc = acc + _conv_shift(ext, k, tm, False) * w[k:k + 1, :]
        o_ref[...] = acc.astype(o_ref.dtype)

    return pl.pallas_call(
        body, name="conv_bwd_x", grid=(n_t, CONV_ALL // tc),
        in_specs=[pl.BlockSpec((tm, tc), lambda i, j: (i, j)),
                  pl.BlockSpec((SUBLANE, tc), lambda i, j: (jnp.minimum((i + 1) * hb, last_blk), j)),
                  pl.BlockSpec((CONV_W, tc), lambda i, j: (0, j)),
                  pl.BlockSpec(memory_space=pl.ANY)],
        out_specs=pl.BlockSpec((tm, tc), lambda i, j: (i, j0 + j)),
        out_shape=jax.ShapeDtypeStruct(dproj.shape, dproj.dtype), input_output_aliases={3: 0},
        compiler_params=_cparams(("parallel", "parallel"), VMEM_LIMIT),
    )(dpre, dpre, w, dproj)


HEAD_SCAN_CHUNKS_PER_STEP = 4
assert GDN_HEADS == RET_HEADS


def _scan_steps(length):
    nc = length // CHUNK
    gs = HEAD_SCAN_CHUNKS_PER_STEP if nc % HEAD_SCAN_CHUNKS_PER_STEP == 0 else 1
    return gs, nc // gs, gs * CHUNK


def _head_tiles(ref, gs):
    return [[ref[j * CHUNK:(j + 1) * CHUNK, h * LANE:(h + 1) * LANE] for h in range(GDN_HEADS)] for j in range(gs)]


def _chunk_rows(ref, gs):
    return [ref[j * CHUNK:(j + 1) * CHUNK, :] for j in range(gs)]


def _gdn_fwd(xc, proj, alog, dtb, nw, length, comm=None):
    gs, steps, rows = _scan_steps(length)
    h_ = GDN_HEADS
    hw = h_ * LANE
    tn_ = h_ * CHUNK
    n_comm = len(comm[1]) if comm else 0

    def body(*refs):
        q_ref, k_ref, v_ref, gz_ref, sm_ref, al_ref, dt_ref, nw_ref = refs[:8]
        c_in = refs[8:8 + n_comm]
        on_ref, st_ref, t_ref = refs[8 + n_comm:11 + n_comm]
        c_out = refs[11 + n_comm:11 + 2 * n_comm]
        s_ref = refs[11 + 2 * n_comm]
        sems = refs[12 + 2 * n_comm:]

        @pl.when(pl.program_id(0) == 0)
        def _():
            s_ref[...] = jnp.zeros_like(s_ref)
            if comm:
                _comm_plan(comm[0], c_in, c_out, *sems)[0]()

        st_ref[...] = s_ref[...]
        lead = lambda ref: [ref[h] for h in range(h_)]
        on, s_new, t = _gdn_chunk(_head_tiles(q_ref, gs), _head_tiles(k_ref, gs), _head_tiles(v_ref, gs),
                                  _head_tiles(gz_ref, gs), _chunk_rows(sm_ref, gs), al_ref[...], dt_ref[...],
                                  nw_ref[...], lead(s_ref))
        for j in range(gs):
            t_ref[j] = t[j]
        for h in range(h_):
            for j in range(gs):
                on_ref[j * CHUNK:(j + 1) * CHUNK, h * LANE:(h + 1) * LANE] = on[j][h].astype(on_ref.dtype)
            s_ref[h] = s_new[h]

        if comm:
            @pl.when(pl.program_id(0) == steps - 1)
            def _():
                _comm_plan(comm[0], c_in, c_out, *sems)[1]()

    blk = lambda col: pl.BlockSpec((rows, hw), lambda c: (c, col // hw))
    one = pl.BlockSpec((1, LANE), lambda c: (0, 0))
    hbm = pl.BlockSpec(memory_space=pl.ANY)
    return pl.pallas_call(
        body, name="gdn_fwd_" + comm[0] if comm else "gdn_fwd", grid=(steps,),
        in_specs=[blk(XC("gq")), blk(XC("gk")), blk(XC("gv")), blk(_MINE["gz"]),
                  pl.BlockSpec((rows, LANE), lambda c: (c, SMALL_OFF // LANE)), one, one,
                  pl.BlockSpec((1, LANE), lambda c: (0, 0))] + [hbm] * n_comm,
        out_specs=[pl.BlockSpec((rows, hw), lambda c: (c, 0)),
                   pl.BlockSpec((h_, None, LANE, LANE), lambda c: (0, c, 0, 0)),
                   pl.BlockSpec((gs, tn_, tn_), lambda c: (c, 0, 0))] + [hbm] * n_comm,
        out_shape=[jax.ShapeDtypeStruct((length, MIX_W), BF16),
                   jax.ShapeDtypeStruct((h_, steps, LANE, LANE), F32),
                   jax.ShapeDtypeStruct((length // CHUNK, tn_, tn_), F32)]
                  + (_comm_out_shapes(*comm) if comm else []),
        scratch_shapes=[pltpu.VMEM((h_, LANE, LANE), F32)] + (_comm_scratch(n_comm) if comm else []),
        compiler_params=_cparams(("arbitrary",), VMEM_LIMIT),
    )(xc, xc, xc, proj, proj, alog, dtb, nw, *(comm[1] if comm else []))


def _gdn_bwd(xc, proj, alog, dtb, nw, st, tinv, dmix, length, comm=None):
    gs, steps, rows = _scan_steps(length)
    h_ = GDN_HEADS
    hw = h_ * LANE
    tn_ = h_ * CHUNK
    n_comm = len(comm[1]) if comm else 0
    n_in, n_out = 11, 6

    def body(*refs):
        q_ref, k_ref, v_ref, gz_ref, sm_ref, al_ref, dt_ref, nw_ref, st_ref, t_ref, do_ref = refs[:n_in]
        c_in = refs[n_in:n_in + n_comm]
        o0 = n_in + n_comm
        dqkv_ref, dgz_ref, dsm_ref, dal_ref, ddt_ref, dnw_ref = refs[o0:o0 + n_out]
        c_out = refs[o0 + n_out:o0 + n_out + n_comm]
        ds_ref = refs[o0 + n_out + n_comm]
        sems = refs[o0 + n_out + n_comm + 1:]

        @pl.when(pl.program_id(0) == 0)
        def _():
            ds_ref[...] = jnp.zeros_like(ds_ref)
            dal_ref[...] = jnp.zeros_like(dal_ref)
            ddt_ref[...] = jnp.zeros_like(ddt_ref)
            dnw_ref[...] = jnp.zeros_like(dnw_ref)
            if comm:
                _comm_plan(comm[0], c_in, c_out, *sems)[0]()

        lead = lambda ref: [ref[h] for h in range(h_)]
        ins = (_head_tiles(q_ref, gs), _head_tiles(k_ref, gs), _head_tiles(v_ref, gs), _head_tiles(gz_ref, gs),
               _chunk_rows(sm_ref, gs), al_ref[...], dt_ref[...], nw_ref[...], lead(st_ref))
        t_saved = [t_ref[j] for j in range(gs)]
        _, vjp = jax.vjp(lambda *a: _gdn_chunk(*a, t_saved=t_saved)[:2], *ins)
        dq, dk, dv, dgz, dsm, dal, ddt, dnw, ds = vjp((_head_tiles(do_ref, gs), lead(ds_ref)))
        for j in range(gs):
            rws = slice(j * CHUNK, (j + 1) * CHUNK)
            for h in range(h_):
                cols = slice(h * LANE, (h + 1) * LANE)
                for part, d in enumerate((dq, dk, dv)):
                    dqkv_ref[rws, part * hw + h * LANE:part * hw + (h + 1) * LANE] = d[j][h]
                dgz_ref[rws, cols] = dgz[j][h].astype(dgz_ref.dtype)
            dsm_ref[rws, :] = dsm[j]
        for h in range(h_):
            ds_ref[h] = ds[h]
        dal_ref[...] += dal
        ddt_ref[...] += ddt
        dnw_ref[...] += dnw

        if comm:
            @pl.when(pl.program_id(0) == steps - 1)
            def _():
                _comm_plan(comm[0], c_in, c_out, *sems)[1]()

    r = lambda c: steps - 1 - c
    blk = lambda col: pl.BlockSpec((rows, hw), lambda c: (r(c), col // hw))
    one = pl.BlockSpec((1, LANE), lambda c: (0, 0))
    hbm = pl.BlockSpec(memory_space=pl.ANY)
    return pl.pallas_call(
        body, name="gdn_bwd_" + comm[0] if comm else "gdn_bwd", grid=(steps,),
        in_specs=[blk(XC("gq")), blk(XC("gk")), blk(XC("gv")), blk(_MINE["gz"]),
                  pl.BlockSpec((rows, LANE), lambda c: (r(c), SMALL_OFF // LANE)), one, one,
                  pl.BlockSpec((1, LANE), lambda c: (0, 0)),
                  pl.BlockSpec((h_, None, LANE, LANE), lambda c: (0, r(c), 0, 0)),
                  pl.BlockSpec((gs, tn_, tn_), lambda c: (r(c), 0, 0)),
                  blk(0)] + [hbm] * n_comm,
        out_specs=[pl.BlockSpec((rows, 3 * hw), lambda c: (r(c), 0)), blk(_MINE["gz"]),
                   pl.BlockSpec((rows, LANE), lambda c: (r(c), 0)), one, one,
                   pl.BlockSpec((1, LANE), lambda c: (0, 0))] + [hbm] * n_comm,
        out_shape=[jax.ShapeDtypeStruct((length, 3 * hw), F32), jax.ShapeDtypeStruct((length, N_PAD), BF16),
                   jax.ShapeDtypeStruct((length, LANE), F32),
                   jax.ShapeDtypeStruct((1, LANE), F32), jax.ShapeDtypeStruct((1, LANE), F32),
                   jax.ShapeDtypeStruct((1, LANE), F32)] + (_comm_out_shapes(*comm) if comm else []),
        scratch_shapes=[pltpu.VMEM((h_, LANE, LANE), F32)] + (_comm_scratch(n_comm) if comm else []),
        compiler_params=_cparams(("arbitrary",), VMEM_LIMIT),
    )(xc, xc, xc, proj, proj, alog, dtb, nw, st, tinv, dmix, *(comm[1] if comm else []))


SSD_CHUNKS_PER_STEP = 8


def _ssd_steps(length):
    nc = length // CHUNK
    gs = SSD_CHUNKS_PER_STEP if nc % SSD_CHUNKS_PER_STEP == 0 else 1
    return gs, nc // gs, gs * CHUNK


def _ssd_lanes(g):
    base = _MINE["sdt"] - SMALL_OFF
    return [[base + (g * PAIRS_PER_GROUP + p) * 2 + j for j in range(2)] for p in range(PAIRS_PER_GROUP)]


def _ssd_tiles(ref, gs):
    return [[ref[j * CHUNK:(j + 1) * CHUNK, p * LANE:(p + 1) * LANE] for p in range(PAIRS_PER_GROUP)]
            for j in range(gs)]


def _ssd_gate(y, z, w):
    return _rms(y * _silu(z), w)


def _ssd_fwd(xc, proj, alog_row, dtb_row, d_row, norm_w, mixed, length):
    gs, steps, rows = _ssd_steps(length)
    ppg = PAIRS_PER_GROUP
    gw = ppg * LANE

    def body(x_ref, b_ref, c_ref, sm_ref, al_ref, db_ref, dp_ref, z_ref, nw_ref, _, y_ref, o_ref, st_ref, hs_ref):
        @pl.when(pl.program_id(1) == 0)
        def _():
            hs_ref[...] = jnp.zeros_like(hs_ref)

        st_ref[...] = hs_ref[...]
        y, hs_new = _ssd_chunk(_ssd_lanes(pl.program_id(0)), _ssd_tiles(x_ref, gs), _chunk_rows(b_ref, gs),
                               _chunk_rows(c_ref, gs), _chunk_rows(sm_ref, gs), al_ref[...], db_ref[...],
                               dp_ref[...], [hs_ref[p] for p in range(ppg)])
        for p in range(ppg):
            hs_ref[p] = hs_new[p]
        for j in range(gs):
            rws = slice(j * CHUNK, (j + 1) * CHUNK)
            y_j = jnp.concatenate(y[j], axis=1)
            y_ref[rws, :] = y_j
            o_ref[rws, :] = _ssd_gate(y_j, z_ref[rws, :], nw_ref[...]).astype(o_ref.dtype)

    one = pl.BlockSpec((1, LANE), lambda g, c: (0, 0))
    return pl.pallas_call(
        body, name="ssd_fwd", grid=(SSD_GROUPS, steps),
        in_specs=[pl.BlockSpec((rows, gw), lambda g, c: (c, XC("sx") // gw + g)),
                  pl.BlockSpec((rows, LANE), lambda g, c: (c, XC("sB") // LANE + g)),
                  pl.BlockSpec((rows, LANE), lambda g, c: (c, XC("sC") // LANE + g)),
                  pl.BlockSpec((rows, LANE), lambda g, c: (c, SMALL_OFF // LANE)),
                  one, one, one,
                  pl.BlockSpec((rows, gw), lambda g, c: (c, _MINE["sz"] // gw + g)),
                  pl.BlockSpec((None, 1, gw), lambda g, c: (g, 0, 0)),
                  pl.BlockSpec(memory_space=pl.ANY)],
        out_specs=[pl.BlockSpec((rows, gw), lambda g, c: (c, g)),
                   pl.BlockSpec((rows, gw), lambda g, c: (c, 1 + g)),
                   pl.BlockSpec((None, ppg, None, LANE, SSD_N), lambda g, c: (g, 0, c, 0, 0))],
        out_shape=[jax.ShapeDtypeStruct((length, SSD_GROUPS * gw), F32),
                   jax.ShapeDtypeStruct(mixed.shape, mixed.dtype),
                   jax.ShapeDtypeStruct((SSD_GROUPS, ppg, steps, LANE, SSD_N), F32)],
        input_output_aliases={9: 1},
        scratch_shapes=[pltpu.VMEM((ppg, LANE, SSD_N), F32)],
        compiler_params=_cparams(("parallel", "arbitrary"), VMEM_LIMIT),
    )(xc, xc, xc, proj, alog_row, dtb_row, d_row, proj, norm_w, mixed)


def _ssd_bwd(xc, proj, alog_row, dtb_row, d_row, norm_w, st, y_b, dmix, dproj, length):
    gs, steps, rows = _ssd_steps(length)
    ppg = PAIRS_PER_GROUP
    gw = ppg * LANE

    def body(x_ref, b_ref, c_ref, sm_ref, al_ref, db_ref, dp_ref, z_ref, nw_ref, st_ref, y_ref, do_ref, _,
             dxc_ref, dz_ref, dsm_ref, dal_ref, ddb_ref, ddp_ref, dnw_ref, dhs_ref):
        g = pl.program_id(1)

        @pl.when(jnp.logical_and(pl.program_id(0) == 0, g == 0))
        def _():
            dhs_ref[...] = jnp.zeros_like(dhs_ref)
            dal_ref[...] = jnp.zeros_like(dal_ref)
            ddb_ref[...] = jnp.zeros_like(ddb_ref)
            ddp_ref[...] = jnp.zeros_like(ddp_ref)
            dnw_ref[...] = jnp.zeros_like(dnw_ref)

        pairs = range(ppg)
        dy, dnw = [], jnp.zeros((1, gw), F32)
        for j in range(gs):
            rws = slice(j * CHUNK, (j + 1) * CHUNK)
            _, gate_vjp = jax.vjp(_ssd_gate, y_ref[rws, :], z_ref[rws, :], nw_ref[...])
            dy_j, dz_j, dnw_j = gate_vjp(do_ref[rws, :].astype(F32))
            dz_ref[rws, :] = dz_j.astype(dz_ref.dtype)
            dy.append([dy_j[:, p * LANE:(p + 1) * LANE] for p in pairs])
            dnw = dnw + dnw_j
        dnw_ref[g] += dnw
        ins = (_ssd_tiles(x_ref, gs), _chunk_rows(b_ref, gs), _chunk_rows(c_ref, gs), _chunk_rows(sm_ref, gs),
               al_ref[...], db_ref[...], dp_ref[...], [st_ref[p] for p in pairs])
        _, vjp = jax.vjp(functools.partial(_ssd_chunk, _ssd_lanes(g)), *ins)
        dx, dbm, dcm, dsm, dal, ddb, ddp, dhs = vjp((dy, [dhs_ref[g, p] for p in pairs]))
        for grp in range(SSD_GROUPS):
            @pl.when(g == grp)
            def _(grp=grp):
                for j in range(gs):
                    rws = slice(j * CHUNK, (j + 1) * CHUNK)
                    for p in pairs:
                        col = grp * gw + p * LANE
                        dxc_ref[rws, col:col + LANE] = dx[j][p]
                    cb_ = XC("sB") - XC("sx") + grp * SSD_N
                    cc_ = XC("sC") - XC("sx") + grp * SSD_N
                    dxc_ref[rws, cb_:cb_ + SSD_N] = dbm[j]
                    dxc_ref[rws, cc_:cc_ + SSD_N] = dcm[j]
        for j in range(gs):
            dsm_ref[j * CHUNK:(j + 1) * CHUNK, :] = dsm[j]
        for p in pairs:
            dhs_ref[g, p] = dhs[p]
        dal_ref[g] += dal
        ddb_ref[g] += ddb
        ddp_ref[g] += ddp

    r = lambda c: steps - 1 - c
    one = pl.BlockSpec((1, LANE), lambda c, g: (0, 0))
    acc = pl.BlockSpec((SSD_GROUPS, 1, LANE), lambda c, g: (0, 0, 0))
    prm = jax.ShapeDtypeStruct((SSD_GROUPS, 1, LANE), F32)
    sz_blk = _MINE["sz"] // gw
    return pl.pallas_call(
        body, name="ssd_bwd", grid=(steps, SSD_GROUPS),
        in_specs=[pl.BlockSpec((rows, gw), lambda c, g: (r(c), XC("sx") // gw + g)),
                  pl.BlockSpec((rows, LANE), lambda c, g: (r(c), XC("sB") // LANE + g)),
                  pl.BlockSpec((rows, LANE), lambda c, g: (r(c), XC("sC") // LANE + g)),
                  pl.BlockSpec((rows, LANE), lambda c, g: (r(c), SMALL_OFF // LANE)),
                  one, one, one,
                  pl.BlockSpec((rows, gw), lambda c, g: (r(c), sz_blk + g)),
                  pl.BlockSpec((None, 1, gw), lambda c, g: (g, 0, 0)),
                  pl.BlockSpec((None, ppg, None, LANE, SSD_N), lambda c, g: (g, 0, r(c), 0, 0)),
                  pl.BlockSpec((rows, gw), lambda c, g: (r(c), g)),
                  pl.BlockSpec((rows, gw), lambda c, g: (r(c), 1 + g)),
                  pl.BlockSpec(memory_space=pl.ANY)],
        out_specs=[pl.BlockSpec((rows, CONV_CH), lambda c, g: (r(c), 0)),
                   pl.BlockSpec((rows, gw), lambda c, g: (r(c), sz_blk + g)),
                   pl.BlockSpec((None, rows, LANE), lambda c, g: (g, r(c), 0)), acc, acc, acc,
                   pl.BlockSpec((SSD_GROUPS, 1, gw), lambda c, g: (0, 0, 0))],
        out_shape=[jax.ShapeDtypeStruct((length, CONV_CH), F32), jax.ShapeDtypeStruct(dproj.shape, dproj.dtype),
                   jax.ShapeDtypeStruct((SSD_GROUPS, length, LANE), F32), prm, prm, prm,
                   jax.ShapeDtypeStruct((SSD_GROUPS, 1, gw), F32)],
        input_output_aliases={12: 1},
        scratch_shapes=[pltpu.VMEM((SSD_GROUPS, ppg, LANE, SSD_N), F32)],
        compiler_params=_cparams(("arbitrary", "arbitrary"), VMEM_LIMIT),
    )(xc, xc, xc, proj, alog_row, dtb_row, d_row, proj, norm_w, st, y_b, dmix, dproj)


def _ret_fwd(proj, cos2, sin2, dmat, qdec, kdec, cdec, nw, mixed, length):
    gs, steps, rows = _scan_steps(length)
    h_ = RET_HEADS

    def body(q_ref, k_ref, v_ref, g_ref, cos_ref, sin_ref, dm_ref, qd_ref, kd_ref, cd_ref, nw_ref, _,
             on_ref, st_ref, r_ref):
        @pl.when(pl.program_id(0) == 0)
        def _():
            r_ref[...] = jnp.zeros_like(r_ref)

        st_ref[...] = r_ref[...]
        lead = lambda ref: [ref[h] for h in range(h_)]
        on, r_new = _ret_chunk(_head_tiles(q_ref, gs), _head_tiles(k_ref, gs), _head_tiles(v_ref, gs),
                               _head_tiles(g_ref, gs), _chunk_rows(cos_ref, gs), _chunk_rows(sin_ref, gs),
                               lead(dm_ref), lead(qd_ref), lead(kd_ref), lead(cd_ref), nw_ref[...], lead(r_ref))
        for h in range(h_):
            for j in range(gs):
                on_ref[j * CHUNK:(j + 1) * CHUNK, h * LANE:(h + 1) * LANE] = on[j][h].astype(on_ref.dtype)
            r_ref[h] = r_new[h]

    hw = h_ * LANE
    blk = lambda name: pl.BlockSpec((rows, hw), lambda c: (c, _MINE[name] // hw))
    tab = pl.BlockSpec((rows, LANE), lambda c: (c, 0))
    full = lambda *s: pl.BlockSpec(s, lambda c: (0,) * len(s))
    return pl.pallas_call(
        body, name="ret_fwd", grid=(steps,),
        in_specs=[blk("rq"), blk("rk"), blk("rv"), blk("rg"), tab, tab,
                  full(h_, CHUNK, CHUNK), full(h_, CHUNK, 1), full(h_, CHUNK, 1), full(h_, 1, 1), full(1, LANE),
                  pl.BlockSpec(memory_space=pl.ANY)],
        out_specs=[pl.BlockSpec((rows, hw), lambda c: (c, (MIX_W - hw) // hw)),
                   pl.BlockSpec((h_, None, LANE, LANE), lambda c: (0, c, 0, 0))],
        out_shape=[jax.ShapeDtypeStruct(mixed.shape, mixed.dtype),
                   jax.ShapeDtypeStruct((h_, steps, LANE, LANE), F32)],
        input_output_aliases={11: 0},
        scratch_shapes=[pltpu.VMEM((h_, LANE, LANE), F32)],
        compiler_params=_cparams(("arbitrary",), VMEM_LIMIT),
    )(proj, proj, proj, proj, cos2, sin2, dmat, qdec, kdec, cdec, nw, mixed)


def _ret_bwd(proj, cos2, sin2, dmat, qdec, kdec, cdec, nw, st, dmix, dproj, length):
    gs, steps, rows = _scan_steps(length)
    h_ = RET_HEADS
    hw = h_ * LANE

    def body(q_ref, k_ref, v_ref, g_ref, cos_ref, sin_ref, dm_ref, qd_ref, kd_ref, cd_ref, nw_ref, st_ref,
             do_ref, _, dqkvg_ref, dnw_ref, dr_ref):
        @pl.when(pl.program_id(0) == 0)
        def _():
            dr_ref[...] = jnp.zeros_like(dr_ref)
            dnw_ref[...] = jnp.zeros_like(dnw_ref)

        lead = lambda ref: [ref[h] for h in range(h_)]
        consts = (_chunk_rows(cos_ref, gs), _chunk_rows(sin_ref, gs), lead(dm_ref), lead(qd_ref), lead(kd_ref),
                  lead(cd_ref))
        f = lambda q, k, v, g, w_, r_: _ret_chunk(q, k, v, g, *consts, w_, r_)
        _, vjp = jax.vjp(f, _head_tiles(q_ref, gs), _head_tiles(k_ref, gs), _head_tiles(v_ref, gs),
                         _head_tiles(g_ref, gs), nw_ref[...], lead(st_ref))
        dq, dk, dv, dg, dnw, dr = vjp((_head_tiles(do_ref, gs), lead(dr_ref)))
        for h in range(h_):
            for j in range(gs):
                rws = slice(j * CHUNK, (j + 1) * CHUNK)
                for part, d in enumerate((dq, dk, dv, dg)):
                    col = part * hw + h * LANE
                    dqkvg_ref[rws, col:col + LANE] = d[j][h].astype(dqkvg_ref.dtype)
            dr_ref[h] = dr[h]
        dnw_ref[...] += dnw

    r = lambda c: steps - 1 - c
    blk = lambda name: pl.BlockSpec((rows, hw), lambda c: (r(c), _MINE[name] // hw))
    tab = pl.BlockSpec((rows, LANE), lambda c: (r(c), 0))
    full = lambda *s: pl.BlockSpec(s, lambda c: (0,) * len(s))
    assert _MINE["rq"] % (4 * hw) == 0 and [_MINE[n] - _MINE["rq"] for n in ("rk", "rv", "rg")] == [hw, 2 * hw, 3 * hw]
    return pl.pallas_call(
        body, name="ret_bwd", grid=(steps,),
        in_specs=[blk("rq"), blk("rk"), blk("rv"), blk("rg"), tab, tab,
                  full(h_, CHUNK, CHUNK), full(h_, CHUNK, 1), full(h_, CHUNK, 1), full(h_, 1, 1), full(1, LANE),
                  pl.BlockSpec((h_, None, LANE, LANE), lambda c: (0, r(c), 0, 0)),
                  pl.BlockSpec((rows, hw), lambda c: (r(c), (MIX_W - hw) // hw)),
                  pl.BlockSpec(memory_space=pl.ANY)],
        out_specs=[pl.BlockSpec((rows, 4 * hw), lambda c: (r(c), _MINE["rq"] // (4 * hw))), full(1, LANE)],
        out_shape=[jax.ShapeDtypeStruct(dproj.shape, dproj.dtype), jax.ShapeDtypeStruct((1, LANE), F32)],
        input_output_aliases={13: 0},
        scratch_shapes=[pltpu.VMEM((h_, LANE, LANE), F32)],
        compiler_params=_cparams(("arbitrary",), VMEM_LIMIT),
    )(proj, proj, proj, proj, cos2, sin2, dmat, qdec, kdec, cdec, nw, st, dmix, dproj)


def _mesh_pos():
    x, y, c = lax.axis_index("x"), lax.axis_index("y"), lax.axis_index("c")
    return x, y, c, 4 * x + 2 * y + c


def _peer(x, y, c, mask):
    return (x ^ ((mask >> 2) & 1), y ^ ((mask >> 1) & 1), c ^ (mask & 1))


def _gather_plan(ins, outs, send_sems, recv_sems, local_sems):
    x, y, c, me = _mesh_pos()
    n = len(ins)
    idx = lambda px, py, pc: 4 * px + 2 * py + pc
    sib = (x, y, 1 - c)
    chips = [(1 - x, y), (x, 1 - y), (1 - x, 1 - y)]
    local = [pltpu.make_async_copy(ins[a], outs[a].at[me], local_sems.at[a]) for a in range(n)]

    def copy(a, k, src, slab, to):
        return pltpu.make_async_remote_copy(
            src_ref=src, dst_ref=outs[a].at[slab], send_sem=send_sems.at[7 * a + k],
            recv_sem=recv_sems.at[7 * a + k], device_id=to, device_id_type=pl.DeviceIdType.MESH)

    first = [copy(a, 0, ins[a], me, sib) for a in range(n)]
    first += [copy(a, 1 + j, ins[a], me, (*chip, c)) for a in range(n) for j, chip in enumerate(chips)]
    landed = [[copy(a, 1 + j, ins[a], idx(*chip, c), (*chip, c)) for a in range(n)] for j, chip in enumerate(chips)]
    passed = [[copy(a, 4 + j, outs[a].at[idx(*chip, c)], idx(*chip, c), sib) for a in range(n)]
              for j, chip in enumerate(chips)]
    from_sib = [copy(a, 0, ins[a], idx(*sib), sib) for a in range(n)]
    from_sib += [copy(a, 4 + j, ins[a], idx(*chip, 1 - c), sib) for a in range(n) for j, chip in enumerate(chips)]

    def start():
        for cp in local + first:
            cp.start()

    def wait():
        for j in range(len(chips)):
            for a in range(n):
                landed[j][a].wait_recv()
                passed[j][a].start()
        for cp in from_sib:
            cp.wait_recv()
        for cp in first + [cp for row in passed for cp in row]:
            cp.wait_send()
        for cp in local:
            cp.wait()

    return start, wait


def _comm_plan(kind, ins, outs, send_sems, recv_sems, local_sems):
    if kind == "gather":
        return _gather_plan(ins, outs, send_sems, recv_sems, local_sems)
    x, y, c, me = _mesh_pos()
    n = len(ins)
    src = (lambda a, idx: ins[a]) if kind == "gather" else (lambda a, idx: ins[a].at[idx])
    local = [pltpu.make_async_copy(src(a, me), outs[a].at[me], local_sems.at[a]) for a in range(n)]
    sends, recvs = [], []
    for mask in range(1, N_DEV):
        px, py, pc = _peer(x, y, c, mask)
        pidx = 4 * px + 2 * py + pc
        for a in range(n):
            k = (mask - 1) * n + a
            copy = lambda s, d, k=k: pltpu.make_async_remote_copy(
                src_ref=s, dst_ref=d, send_sem=send_sems.at[k], recv_sem=recv_sems.at[k],
                device_id=(px, py, pc), device_id_type=pl.DeviceIdType.MESH)
            sends.append(copy(src(a, pidx), outs[a].at[me]))
            recvs.append(copy(src(a, me), outs[a].at[pidx]))

    def start():
        for cp in local + sends:
            cp.start()

    def wait():
        for cp in sends:
            cp.wait_send()
        for cp in recvs:
            cp.wait_recv()
        for cp in local:
            cp.wait()

    return start, wait


def _comm_out_shapes(kind, arrays):
    return [jax.ShapeDtypeStruct(((N_DEV,) + s.shape) if kind == "gather" else s.shape, s.dtype) for s in arrays]


def _comm_scratch(n):
    k_tot = (N_DEV - 1) * n
    return [pltpu.SemaphoreType.DMA((k_tot,)), pltpu.SemaphoreType.DMA((k_tot,)), pltpu.SemaphoreType.DMA((n,))]


def _comm_call(name, kind, arrays):
    n = len(arrays)

    def body(*refs):
        start, wait = _comm_plan(kind, refs[:n], refs[n:2 * n], *refs[2 * n:])
        start()
        wait()

    hbm = pl.BlockSpec(memory_space=pl.ANY)
    return pl.pallas_call(
        body, name=name, in_specs=[hbm] * n, out_specs=[hbm] * n, out_shape=_comm_out_shapes(kind, arrays),
        scratch_shapes=_comm_scratch(n), compiler_params=pltpu.CompilerParams(has_side_effects=True),
    )(*arrays)


def _adam(name, parts, w, m, v, tr, layer=None, prev=None):
    p_, r_, cw = parts.shape
    c1 = 1.0 - ADAM_B1 ** ADAM_STEP
    c2 = 1.0 - ADAM_B2 ** ADAM_STEP
    n_prev = 4 if prev else 0

    def body(p_ref, w_ref, m_ref, v_ref, *rest):
        g_ref, d_ref, nm_ref, nv_ref = rest[n_prev:]
        g = p_ref[0].astype(F32)
        for i in range(1, p_):
            g = g + p_ref[i].astype(F32)
        nm = ADAM_B1 * m_ref[...] + (1.0 - ADAM_B1) * g
        nv = ADAM_B2 * v_ref[...] + (1.0 - ADAM_B2) * (g * g)
        d_ref[...] = -ADAM_LR * ((nm / c1) / (jnp.sqrt(nv / c2) + ADAM_EPS) + ADAM_WD * w_ref[...])
        g_ref[...] = g
        nm_ref[...] = nm
        nv_ref[...] = nv

    if layer is None:
        spec = pl.BlockSpec((tr, cw), lambda i: (i, 0))
        shp = jax.ShapeDtypeStruct((r_, cw), F32)
    else:
        spec = pl.BlockSpec((None, tr, cw), lambda i: (layer, i, 0))
        shp = jax.ShapeDtypeStruct((DEPTH, r_, cw), F32)
    return pl.pallas_call(
        body, name=name, grid=(r_ // tr,),
        in_specs=[pl.BlockSpec((p_, tr, cw), lambda i: (0, i, 0)), spec, spec, spec]
                 + [pl.BlockSpec(memory_space=pl.ANY)] * n_prev,
        out_specs=[spec] * 4, out_shape=[shp] * 4,
        input_output_aliases={4 + i: i for i in range(n_prev)},
        compiler_params=_cparams(("parallel",), VMEM_LIMIT),
    )(parts, w, m, v, *(prev or ()))


def _perm_cols(w_full):
    parts = [w_full[:, _ORIG[n][0]:_ORIG[n][0] + _ORIG[n][1]] for n in _ORDER]
    parts.append(jnp.zeros((w_full.shape[0], N_PAD - N_USED), w_full.dtype))
    return jnp.concatenate(parts, axis=1)


def _unperm_cols(g):
    by_orig = sorted(_ORDER, key=lambda n: _ORIG[n][0])
    return jnp.concatenate([g[:, _MINE[n]:_MINE[n] + _ORIG[n][1]] for n in by_orig], axis=1)


def _ret_consts(length):
    lg = jnp.log(1.0 - 2.0 ** (-5.0 - jnp.arange(RET_HEADS, dtype=F32)))
    idx = jnp.arange(CHUNK, dtype=F32)
    rel = idx[:, None] - idx[None, :]
    dmat = jnp.where(rel[None] >= 0, jnp.exp(jnp.maximum(rel, 0.0)[None] * lg[:, None, None]), 0.0)
    qdec = jnp.exp((idx[:, None] + 1.0) * lg[None, :]).T[:, :, None]
    kdec = jnp.exp((CHUNK - 1.0 - idx)[:, None] * lg[None, :]).T[:, :, None]
    cdec = jnp.exp(CHUNK * lg)[:, None, None]
    half = RET_DK // 2
    inv = ROPE_BASE ** (-jnp.arange(half, dtype=F32) / half)
    ang = jnp.arange(length, dtype=jnp.int32).astype(F32)[:, None] * inv[None, :]
    cos, sin = jnp.cos(ang), jnp.sin(ang)
    return (jnp.concatenate([cos, cos], axis=1), jnp.concatenate([-sin, sin], axis=1), dmat, qdec, kdec, cdec)


def _tiles(length, pref=512):
    return pref if length % pref == 0 else length


def _layer_fwd(x, p, consts, length, last, target, comm=None):
    tm = _tiles(length)
    (h,), _ = _rowmap("pre_norm", lambda x_, w_: ((_rms(x_, w_),), ()), length, tm,
                      [(x, D_MODEL, 0)], [p["pre_norm"]], out_ws=[D_MODEL], out_dtypes=[BF16])
    proj = _matmul("in_proj", h, p["w_in"], "nn", length, N_PAD, D_MODEL)
    xc = _conv_fwd(proj, p["conv_w"], p["conv_b"], length, _tiles(length, 1024), 512)
    mixed, st_a, tinv, *got = _gdn_fwd(xc, proj, p["gdn_A_log"], p["gdn_dt_bias"], p["gdn_norm"], length, comm)
    y_b, mixed, st_b = _ssd_fwd(xc, proj, p["ssd_A_log"], p["ssd_dt_bias"], p["ssd_D"], p["ssd_norm"], mixed, length)
    mixed, st_c = _ret_fwd(proj, *consts, p["ret_norm"], mixed, length)
    w_out = p["w_out"](got) if callable(p["w_out"]) else p["w_out"]
    out = _matmul("out_proj", mixed, w_out, "nn", length, D_MODEL, MIX_W)
    res = dict(x=x, h=h, proj=proj, xc=xc, st_a=st_a, tinv=tinv, st_b=st_b, st_c=st_c, y_b=y_b, mixed=mixed,
               out=out, w_out=w_out)
    if not last:
        (y,), _ = _rowmap("post_norm", lambda x_, o_, w_: ((x_ + _rms(o_, w_),), ()), length, tm,
                          [(x, D_MODEL, 0), (out, D_MODEL, 0)], [p["post_norm"]], out_ws=[D_MODEL])
        return y, res, None, got

    def head(x_, o_, t_, w_):
        normed, post_vjp = jax.vjp(_rms, o_, w_)
        e = x_ + normed - t_
        row = jnp.mean(e * e, axis=-1, keepdims=True)
        loss = 0.5 * jnp.sum(row, axis=0, keepdims=True)
        dy_ = e * (1.0 / D_MODEL)
        d_out, dw = post_vjp(dy_)
        return (dy_, d_out), (loss + jnp.zeros((1, LANE), F32), dw)

    (dy, d_out), (loss, dw_post) = _rowmap(
        "loss_head", head, length, tm, [(x, D_MODEL, 0), (out, D_MODEL, 0), (target, D_MODEL, 0)], [p["post_norm"]],
        out_ws=[D_MODEL, D_MODEL], out_dtypes=[F32, BF16], acc_ws=[LANE, D_MODEL])
    res["post_bwd"] = (d_out, dw_post)
    return None, res, (dy, loss[0, 0]), got


def _layer_bwd(dy, p, res, consts, length, comm_scan=None, comm_dx=None):
    tm = _tiles(length)
    g = {}
    if "post_bwd" in res:
        d_out, g["post_norm"] = res["post_bwd"]
    else:
        (d_out,), (g["post_norm"],) = _rowmap("post_norm_bwd", lambda o_, w_: ((_rms(o_, w_),), ()), length, tm,
                                              [(res["out"], D_MODEL, 0)], [p["post_norm"]], cts=[(dy, D_MODEL, 0)],
                                              d_dtypes=[BF16])
    dmix = _matmul("out_proj_dx", d_out, res["w_out"], "nt", length, MIX_W, D_MODEL)
    g["w_out"] = _matmul("out_proj_dw", res["mixed"], d_out, "tn", MIX_W, D_MODEL, length, out_dtype=BF16)
    comm = comm_scan(g) if comm_scan else None
    proj, xc = res["proj"], res["xc"]
    (dxc_a, dproj, dsm_a, g["gdn_A_log"], g["gdn_dt_bias"], g["gdn_norm"], *got) = _gdn_bwd(
        xc, proj, p["gdn_A_log"], p["gdn_dt_bias"], p["gdn_norm"], res["st_a"], res["tinv"], dmix, length, comm)
    dxc_b, dproj, dsm_b, g["ssd_A_log"], g["ssd_dt_bias"], g["ssd_D"], g["ssd_norm"] = _ssd_bwd(
        xc, proj, p["ssd_A_log"], p["ssd_dt_bias"], p["ssd_D"], p["ssd_norm"], res["st_b"], res["y_b"], dmix,
        dproj, length)
    dproj, g["ret_norm"] = _ret_bwd(proj, *consts, p["ret_norm"], res["st_c"], dmix, dproj, length)
    tmc = _tiles(length, 1024)
    dpre, g["conv_w"], g["conv_b"] = _conv_bwd_pre(proj, p["conv_w"], p["conv_b"], dxc_a, dxc_b, length, tmc, 512)
    dproj = _conv_bwd_x(dpre, p["conv_w"], dproj, length, tmc, 512)
    pad = SMALL_OFF % 512 + 512 - LANE
    (dproj,), _ = _rowmap("dsmall", lambda a_, b_, c_: ((jnp.concatenate([a_ + b_ + c_, jnp.zeros((a_.shape[0], pad), F32)], axis=1),), ()),
                          length, tm, [(dsm_a, LANE, 0), (dsm_b[0], LANE, 0), (dsm_b[1], LANE, 0)], [],
                          out_ws=[512], out_dtypes=[BF16], place={0: (dproj, N_PAD, SMALL_OFF // 512)})
    g["w_in"] = _matmul("in_proj_dw", res["h"], dproj, "tn", D_MODEL, N_PAD, length, out_dtype=BF16)
    if comm_dx:
        dh, *got_dx = _matmul("in_proj_dx_scatter", dproj, p["w_in"], "nt", length, D_MODEL, N_PAD, comm=comm_dx(g))
    else:
        dh, got_dx = _matmul("in_proj_dx", dproj, p["w_in"], "nt", length, D_MODEL, N_PAD), []
    (dx,), (g["pre_norm"],) = _rowmap(
        "pre_norm_bwd", lambda x_, w_: ((_rms(x_, w_), x_), ()), length, tm,
        [(res["x"], D_MODEL, 0)], [p["pre_norm"]], cts=[(dh, D_MODEL, 0), (dy, D_MODEL, 0)])
    return dx, g, got, got_dx


def _lane_row(vals, piece):
    lo = _MINE[piece] - SMALL_OFF
    return jnp.pad(vals[None], ((0, 0), (lo, LANE - lo - vals.shape[0])))


def _from_lane_row(row, piece, n):
    lo = _MINE[piece] - SMALL_OFF
    return row.reshape(-1, LANE).sum(axis=0)[lo:lo + n]


def _make_layer(l, w_in_full, w_out_full, conv_full, sw):
    return dict(
        pre_norm=sw["pre_norm"][l][None], post_norm=sw["post_norm"][l][None], w_in=_perm_cols(w_in_full),
        w_out=w_out_full,
        conv_w=jnp.concatenate([conv_full[0], conv_full[1]], axis=1),
        conv_b=jnp.concatenate([jnp.zeros((CONV_CH,), F32), sw["ssd_conv_b"][l]])[None],
        gdn_A_log=_lane_row(sw["gdn_A_log"][l], "ga"), gdn_dt_bias=_lane_row(sw["gdn_dt_bias"][l], "ga"),
        gdn_norm=sw["gdn_norm"][l][None],
        ssd_A_log=_lane_row(sw["ssd_A_log"][l], "sdt"), ssd_dt_bias=_lane_row(sw["ssd_dt_bias"][l], "sdt"),
        ssd_D=_lane_row(sw["ssd_D"][l], "sdt"),
        ssd_norm=sw["ssd_norm"][l].reshape(SSD_GROUPS, 1, -1),
        ret_norm=sw["ret_norm"][l][None])


def _local_step(xb, tgt, layer0, make_layer1, length, fwd_comm=None, comm_scan=None, comm_dx=None):
    consts = _ret_consts(length)
    y0, res0, _, got = _layer_fwd(xb, layer0, consts, length, False, tgt, fwd_comm)
    layer1 = make_layer1(got)
    _, res1, (dy, loss_local), _ = _layer_fwd(y0, layer1, consts, length, True, tgt)
    dy, grads1, _, _ = _layer_bwd(dy, layer1, res1, consts, length)
    dx, grads0, recv_scan, recv_dx = _layer_bwd(
        dy, layer0, res0, consts, length,
        (lambda g0: comm_scan(grads1, g0)) if comm_scan else None, comm_dx)
    return loss_local, dx, [grads0, grads1], recv_scan, recv_dx


_SMALL = ["pre_norm", "post_norm", "gdn_A_log", "gdn_dt_bias", "gdn_norm", "ssd_conv_b", "ssd_A_log",
          "ssd_dt_bias", "ssd_D", "ssd_norm", "ret_norm"]


def _pack_small(arrs):
    rows = []
    for n in _SMALL:
        flat = arrs[n].reshape(-1)
        pad = (-flat.shape[0]) % LANE
        rows.append(jnp.pad(flat, (0, pad)).reshape(-1, LANE))
    out = jnp.concatenate(rows, axis=0)
    return jnp.pad(out, ((0, (-out.shape[0]) % SUBLANE), (0, 0)))


def _unpack_small(packed, like):
    out, r = {}, 0
    for n in _SMALL:
        cnt = like[n].size
        nrow = -(-cnt // LANE)
        out[n] = packed[r:r + nrow].reshape(-1)[:cnt].reshape(like[n].shape)
        r += nrow
    return out


def kernel(x, pre_norm, post_norm, w_in, gdn_conv, gdn_A_log, gdn_dt_bias, gdn_norm, ssd_conv, ssd_conv_b, ssd_A_log, ssd_dt_bias, ssd_D, ssd_norm, ret_norm, w_out, loss_target, m_pre_norm, m_post_norm, m_w_in, m_gdn_conv, m_gdn_A_log, m_gdn_dt_bias, m_gdn_norm, m_ssd_conv, m_ssd_conv_b, m_ssd_A_log, m_ssd_dt_bias, m_ssd_D, m_ssd_norm, m_ret_norm, m_w_out, v_pre_norm, v_post_norm, v_w_in, v_gdn_conv, v_gdn_A_log, v_gdn_dt_bias, v_gdn_norm, v_ssd_conv, v_ssd_conv_b, v_ssd_A_log, v_ssd_dt_bias, v_ssd_D, v_ssd_norm, v_ret_norm, v_w_out):
    length = x.shape[1]
    xb = x[0]
    tgt = loss_target[0]
    small_w = dict(pre_norm=pre_norm, post_norm=post_norm, gdn_A_log=gdn_A_log, gdn_dt_bias=gdn_dt_bias,
                   gdn_norm=gdn_norm, ssd_conv_b=ssd_conv_b, ssd_A_log=ssd_A_log, ssd_dt_bias=ssd_dt_bias,
                   ssd_D=ssd_D, ssd_norm=ssd_norm, ret_norm=ret_norm)
    small_m = dict(pre_norm=m_pre_norm, post_norm=m_post_norm, gdn_A_log=m_gdn_A_log, gdn_dt_bias=m_gdn_dt_bias,
                   gdn_norm=m_gdn_norm, ssd_conv_b=m_ssd_conv_b, ssd_A_log=m_ssd_A_log, ssd_dt_bias=m_ssd_dt_bias,
                   ssd_D=m_ssd_D, ssd_norm=m_ssd_norm, ret_norm=m_ret_norm)
    small_v = dict(pre_norm=v_pre_norm, post_norm=v_post_norm, gdn_A_log=v_gdn_A_log, gdn_dt_bias=v_gdn_dt_bias,
                   gdn_norm=v_gdn_norm, ssd_conv_b=v_ssd_conv_b, ssd_A_log=v_ssd_A_log, ssd_dt_bias=v_ssd_dt_bias,
                   ssd_D=v_ssd_D, ssd_norm=v_ssd_norm, ret_norm=v_ret_norm)

    w_in_b, w_out_b = w_in.astype(BF16), w_out.astype(BF16)
    conv_shard = jnp.stack([gdn_conv, ssd_conv], axis=1)
    full_out = lambda g_out: g_out.reshape(MIX_W, D_MODEL)

    def assemble(l, g_in, w_out_full, g_conv):
        w_in_full = g_in.transpose(1, 0, 2).reshape(D_MODEL, N_IN)
        conv_full = g_conv.transpose(1, 2, 0, 3).reshape(2, CONV_W, CONV_CH)
        return _make_layer(l, w_in_full, w_out_full, conv_full, small_w)

    slab_in = lambda g: _unperm_cols(g["w_in"]).reshape(D_MODEL, N_DEV, SHARD_IN).transpose(1, 0, 2).astype(BF16)
    slab_out = lambda g: g["w_out"].reshape(N_DEV, SHARD_OUT, D_MODEL).astype(BF16)
    slab_conv = lambda g: jnp.stack(
        [g["conv_w"][:, k * CONV_CH:(k + 1) * CONV_CH].reshape(CONV_W, N_DEV, SHARD_CONV).transpose(1, 0, 2)
         for k in range(2)], axis=1).reshape(N_DEV, 2 * CONV_W, SHARD_CONV)

    g_in0, g_conv0 = _comm_call("gather_layer0", "gather", [w_in_b[0], conv_shard[0]])
    layer0 = assemble(0, g_in0, lambda got: full_out(got[3]), g_conv0)
    loss_local, dx, grads, recv_scan, recv_dx = _local_step(
        xb, tgt, layer0, lambda got: assemble(1, got[0], full_out(got[1]), got[2]), length,
        fwd_comm=("gather", [w_in_b[1], w_out_b[1], conv_shard[1], w_out_b[0]]),
        comm_scan=lambda g1, g0: ("scatter", [slab_in(g1), slab_out(g1), slab_conv(g1), slab_out(g0)]),
        comm_dx=lambda g0: ("scatter", [slab_in(g0), slab_conv(g0)]))
    grad_x = dx[None]
    loss = lax.psum(loss_local, ("x", "y", "c"))

    small_g = dict(
        pre_norm=jnp.concatenate([grads[l]["pre_norm"] for l in range(DEPTH)], axis=0),
        post_norm=jnp.concatenate([grads[l]["post_norm"] for l in range(DEPTH)], axis=0),
        gdn_A_log=jnp.stack([_from_lane_row(grads[l]["gdn_A_log"], "ga", GDN_HEADS) for l in range(DEPTH)]),
        gdn_dt_bias=jnp.stack([_from_lane_row(grads[l]["gdn_dt_bias"], "ga", GDN_HEADS) for l in range(DEPTH)]),
        gdn_norm=jnp.concatenate([grads[l]["gdn_norm"] for l in range(DEPTH)], axis=0),
        ssd_conv_b=jnp.concatenate([grads[l]["conv_b"][:, CONV_CH:] for l in range(DEPTH)], axis=0),
        ssd_A_log=jnp.stack([_from_lane_row(grads[l]["ssd_A_log"], "sdt", SSD_HEADS) for l in range(DEPTH)]),
        ssd_dt_bias=jnp.stack([_from_lane_row(grads[l]["ssd_dt_bias"], "sdt", SSD_HEADS) for l in range(DEPTH)]),
        ssd_D=jnp.stack([_from_lane_row(grads[l]["ssd_D"], "sdt", SSD_HEADS) for l in range(DEPTH)]),
        ssd_norm=jnp.concatenate([grads[l]["ssd_norm"].reshape(1, -1) for l in range(DEPTH)], axis=0),
        ret_norm=jnp.concatenate([grads[l]["ret_norm"] for l in range(DEPTH)], axis=0))
    gs = _pack_small(small_g)
    gs8 = jnp.broadcast_to(gs[None], (N_DEV,) + gs.shape)
    (r_small,) = _comm_call("exchange_small", "scatter", [gs8])
    recv = [[recv_dx[0], recv_scan[3], recv_dx[1]], recv_scan[:3]]

    conv_w = lambda g_, s_, l: jnp.stack([g_[l], s_[l]], axis=0).reshape(2 * CONV_W, SHARD_CONV)
    o_in, o_out, o_conv = None, None, []
    for l in range(DEPTH):
        o_in = _adam(f"adam_w_in{l}", recv[l][0], w_in, m_w_in, v_w_in, 128, l, o_in)
        o_out = _adam(f"adam_w_out{l}", recv[l][1], w_out, m_w_out, v_w_out, 128, l, o_out)
        o_conv.append(_adam(f"adam_conv{l}", recv[l][2], conv_w(gdn_conv, ssd_conv, l),
                            conv_w(m_gdn_conv, m_ssd_conv, l), conv_w(v_gdn_conv, v_ssd_conv, l), 2 * CONV_W))
    ps_w, ps_m, ps_v = _pack_small(small_w), _pack_small(small_m), _pack_small(small_v)
    o_small = _adam("adam_small", r_small, ps_w, ps_m, ps_v, ps_w.shape[0])

    names = ["pre_norm", "post_norm", "w_in", "gdn_conv", "gdn_A_log", "gdn_dt_bias", "gdn_norm", "ssd_conv",
             "ssd_conv_b", "ssd_A_log", "ssd_dt_bias", "ssd_D", "ssd_norm", "ret_norm", "w_out"]
    outs = []
    for kind in range(4):
        d = _unpack_small(o_small[kind], small_w)
        cv = jnp.stack([o_conv[l][kind].reshape(2, CONV_W, SHARD_CONV) for l in range(DEPTH)], axis=1)
        d["w_in"] = o_in[kind]
        d["w_out"] = o_out[kind]
        d["gdn_conv"] = cv[0]
        d["ssd_conv"] = cv[1]
        outs.extend(d[n] for n in names)
    return (loss, grad_x, *outs)
```

```python
import functools
import math

import jax
import jax.numpy as jnp
from jax import lax
from jax.experimental import pallas as pl
from jax.experimental.pallas import tpu as pltpu

F32 = jnp.float32
BF16 = jnp.bfloat16

D_MODEL = 1024
DEPTH = 2
CHUNK = 64
CONV_W = 4
EPS = 1e-6
N_DEV = 8

GDN_HEADS = 4
GDN_DK = 128
SSD_HEADS = 16
SSD_P = 64
SSD_N = 128
SSD_GROUPS = 2
SSD_PAIRS = SSD_HEADS // 2
PAIRS_PER_GROUP = SSD_PAIRS // SSD_GROUPS
RET_HEADS = 4
RET_DK = 128
ROPE_BASE = 10000.0
MIX_W = 2048
N_IN = 6680
SHARD_IN = N_IN // N_DEV
SHARD_OUT = MIX_W // N_DEV
CONV_CH = 1536
SHARD_CONV = CONV_CH // N_DEV

ADAM_LR = 0.001
ADAM_B1 = 0.9
ADAM_B2 = 0.999
ADAM_EPS = 1e-08
ADAM_WD = 0.01
ADAM_STEP = 10

LANE = 128
SUBLANE = 8
VMEM_LIMIT = 56 * 1024 * 1024

_ORIG = dict(gq=(0, 512), gk=(512, 512), gv=(1024, 512), gz=(1536, 512), gb=(2048, 4), ga=(2052, 4),
             sx=(2056, 1024), sB=(3080, 256), sC=(3336, 256), sz=(3592, 1024), sdt=(4616, 16),
             rq=(4632, 512), rk=(5144, 512), rv=(5656, 512), rg=(6168, 512))
_ORDER = ["rq", "rk", "rv", "rg", "gq", "gk", "gv", "sx", "sB", "sC", "gz", "sz", "gb", "ga", "sdt"]
_MINE = {}
_off = 0
for _n in _ORDER:
    _MINE[_n] = _off
    _off += _ORIG[_n][1]
N_USED = _off
N_PAD = 7168
CONV_ALL = 2 * CONV_CH
SMALL_OFF = _MINE["gb"]
CONV_OFF = _MINE["gq"]
XC = lambda name: _MINE[name] - CONV_OFF


def _cparams(sem, vmem=None):
    return pltpu.CompilerParams(dimension_semantics=sem, vmem_limit_bytes=vmem)


def _split_bf16(a):
    hi = a.astype(BF16)
    return hi, (a - hi.astype(F32)).astype(BF16)


def _make_mm():
    def raw(a, b, ca, cb):
        return lax.dot_general(a.astype(BF16), b.astype(BF16), (((ca,), (cb,)), ((), ())),
                               preferred_element_type=F32)

    @jax.custom_vjp
    def nn(a, b):
        return raw(a, b, 1, 0)

    @jax.custom_vjp
    def nt(a, b):
        return raw(a, b, 1, 1)

    @jax.custom_vjp
    def tn(a, b):
        return raw(a, b, 0, 0)

    nn.defvjp(lambda a, b: (raw(a, b, 1, 0), (a, b)), lambda r, g: (nt(g, r[1]), tn(r[0], g)))
    nt.defvjp(lambda a, b: (raw(a, b, 1, 1), (a, b)), lambda r, g: (nn(g, r[1]), tn(g, r[0])))
    tn.defvjp(lambda a, b: (raw(a, b, 0, 0), (a, b)), lambda r, g: (nt(r[1], g), nn(r[0], g)))
    return nn, nt, tn


_nn, _nt, _tn = _make_mm()


@jax.custom_vjp
def _swap_halves(t):
    return pltpu.roll(t, LANE // 2, 1)


_swap_halves.defvjp(lambda t: (pltpu.roll(t, LANE // 2, 1), None),
                    lambda _, g: (pltpu.roll(g, LANE // 2, 1),))


@jax.custom_vjp
def _split_rows(x):
    return tuple(x[i * CHUNK:(i + 1) * CHUNK] for i in range(x.shape[0] // CHUNK))


_split_rows.defvjp(lambda x: (tuple(x[i * CHUNK:(i + 1) * CHUNK] for i in range(x.shape[0] // CHUNK)), None),
                   lambda _, gs: (jnp.concatenate(gs, axis=0),))


def _dot3(a, b, ca, cb):
    dot = lambda x, y: lax.dot_general(x, y, (((ca,), (cb,)), ((), ())), preferred_element_type=F32)
    return dot(a[0], b[0]) + (dot(a[0], b[1]) + dot(a[1], b[0]))


@jax.custom_vjp
def _tri_inv(mats):
    return _tri_inv_impl(mats)


def _tri_inv_impl(mats):
    ii = lax.broadcasted_iota(jnp.int32, mats[0].shape, 0)
    jj = lax.broadcasted_iota(jnp.int32, mats[0].shape, 1)
    eye = jnp.where(ii == jj, 1.0, 0.0).astype(F32)
    ts = [eye - a for a in mats]
    ps = [_split_bf16(-a) for a in mats]
    for _ in range(int(math.log2(CHUNK)) - 1):
        ps = [_split_bf16(_dot3(p, p, 1, 0)) for p in ps]
        ts = [t + _dot3(_split_bf16(t), p, 1, 0) for t, p in zip(ts, ps)]
    return ts


def _tri_inv_bwd(ts, gs):
    tsp = [_split_bf16(t) for t in ts]
    xs = [_dot3(t, _split_bf16(g), 0, 0) for t, g in zip(tsp, gs)]
    return ([-_dot3(_split_bf16(x), t, 1, 1) for x, t in zip(xs, tsp)],)


_tri_inv.defvjp(lambda mats: (lambda ts: (ts, ts))(_tri_inv_impl(mats)), _tri_inv_bwd)


@jax.custom_vjp
def _tri_inv_saved(mats, ts):
    return ts


_tri_inv_saved.defvjp(lambda mats, ts: (ts, ts),
                      lambda ts, gs: (_tri_inv_bwd(ts, gs)[0], [jnp.zeros_like(t) for t in ts]))


def _silu(x):
    return x * jax.nn.sigmoid(x)


@jax.custom_vjp
def _softplus(x):
    return jnp.maximum(x, 0.0) + jnp.log1p(jnp.exp(-jnp.abs(x)))


_softplus.defvjp(lambda x: (jnp.maximum(x, 0.0) + jnp.log1p(jnp.exp(-jnp.abs(x))), x),
                 lambda x, g: (g * jax.nn.sigmoid(x),))


def _rms(x, w):
    return x * lax.rsqrt(jnp.mean(x * x, axis=-1, keepdims=True) + EPS) * w


def _chunk_masks(n):
    ii = lax.broadcasted_iota(jnp.int32, (n, n), 0)
    jj = lax.broadcasted_iota(jnp.int32, (n, n), 1)
    return ii >= jj, ii > jj, ii == jj, ii <= jj


def _cumsum_col(g, causal, eye, upper):
    g_row = jnp.sum(jnp.where(eye, g, 0.0), axis=0, keepdims=True)
    col = jnp.sum(jnp.where(causal, g_row, 0.0), axis=1, keepdims=True)
    row = jnp.sum(jnp.where(upper, g, 0.0), axis=0, keepdims=True)
    return col, row


def _lane_col(block, lane):
    pick = lax.broadcasted_iota(jnp.int32, (1, block.shape[1]), 1) == lane
    return jnp.sum(jnp.where(pick, block, 0.0), axis=1, keepdims=True)


def _gdn_chunk(q, k, v, gz, sm, alog_row, dtb_row, nw, s, t_saved=None):
    subs, nh = range(len(q)), len(q[0])
    c = q[0][0].shape[0]
    causal, _, eye, upper = _chunk_masks(c)
    stack = lambda xs: jnp.concatenate(xs, axis=0)
    per = lambda f: [[f(j, h) for h in range(nh)] for j in subs]
    qn = per(lambda j, h: q[j][h] * lax.rsqrt(jnp.sum(q[j][h] * q[j][h], axis=-1, keepdims=True) + EPS)
             * (GDN_DK ** -0.5))
    kn = per(lambda j, h: k[j][h] * lax.rsqrt(jnp.sum(k[j][h] * k[j][h], axis=-1, keepdims=True) + EPS))
    beta_blk = [jax.nn.sigmoid(sm[j]) for j in subs]
    g_blk = [-jnp.exp(alog_row) * _softplus(sm[j] + dtb_row) for j in subs]
    beta = per(lambda j, h: _lane_col(beta_blk[j], _MINE["gb"] - SMALL_OFF + h))
    g = per(lambda j, h: _lane_col(g_blk[j], _MINE["ga"] - SMALL_OFF + h))
    gcum = per(lambda j, h: _cumsum_col(g[j][h], causal, eye, upper)[0])
    eg = per(lambda j, h: jnp.exp(gcum[j][h]))
    glast = per(lambda j, h: jnp.sum(g[j][h], axis=0, keepdims=True))
    kb = per(lambda j, h: kn[j][h] * beta[j][h])
    n = nh * c
    ii = lax.broadcasted_iota(jnp.int32, (n, n), 0)
    jj = lax.broadcasted_iota(jnp.int32, (n, n), 1)
    sh = int(math.log2(c))
    same = lax.shift_right_logical(ii, sh) == lax.shift_right_logical(jj, sh)
    causal_bd = jnp.logical_and(same, ii >= jj)
    strict_bd = jnp.logical_and(same, ii > jj)
    gcum_all = [stack(gcum[j]) for j in subs]
    gcum_row = [jnp.sum(jnp.where(ii == jj, gcum_all[j], 0.0), axis=0, keepdims=True) for j in subs]
    decay = [jnp.where(causal_bd, jnp.exp(jnp.where(causal_bd, gcum_all[j] - gcum_row[j], 0.0)), 0.0) for j in subs]
    kn_all = [stack(kn[j]) for j in subs]
    a_low = [jnp.where(strict_bd, _nt(stack(kb[j]), kn_all[j]) * decay[j], 0.0) for j in subs]
    t = _tri_inv(a_low) if t_saved is None else _tri_inv_saved(a_low, t_saved)
    u = [_nn(t[j], stack([v[j][h] * beta[j][h] for h in range(nh)])) for j in subs]
    w = [_split_rows(_nn(t[j], stack([kb[j][h] * eg[j][h] for h in range(nh)]))) for j in subs]
    attn = [_nt(stack(qn[j]), kn_all[j]) * decay[j] for j in subs]
    kdec = per(lambda j, h: kn[j][h] * jnp.exp(glast[j][h] - gcum[j][h]))
    on = []
    for j in subs:
        v_new_all = u[j] - stack([_nn(w[j][h], s[h]) for h in range(nh)])
        o = _split_rows(stack([_nn(qn[j][h] * eg[j][h], s[h]) for h in range(nh)]) + _nn(attn[j], v_new_all))
        v_new = _split_rows(v_new_all)
        s = [s[h] * jnp.exp(glast[j][h]) + _tn(kdec[j][h], v_new[h]) for h in range(nh)]
        on.append([_rms(o[h], nw) * _silu(gz[j][h]) for h in range(nh)])
    return on, s, t


def _ssd_chunk(lanes, x, bm, cm, sm, alog_row, dtb_row, d_row, hs):
    subs, pairs = range(len(x)), range(len(x[0]))
    c = x[0][0].shape[0]
    lane_i = lax.broadcasted_iota(jnp.int32, (c, LANE), 1)
    lane_lo = lane_i < SSD_P
    lane_lo1 = lax.broadcasted_iota(jnp.int32, (1, LANE), 1) < SSD_P
    row_i = lax.broadcasted_iota(jnp.int32, (c, LANE), 0)
    causal2 = row_i >= jnp.bitwise_and(lane_i, c - 1)
    diag2 = row_i == jnp.bitwise_and(lane_i, c - 1)
    to_row = lambda blk: jnp.sum(jnp.where(diag2, blk, 0.0), axis=0, keepdims=True)
    per = lambda f: [[f(j, p) for p in pairs] for j in subs]
    both = lambda blk, p: [_lane_col(blk, lanes[p][h]) for h in range(2)]
    dt_blk = [_softplus(sm[j] + dtb_row) for j in subs]
    a_blk = [dt_blk[j] * -jnp.exp(alog_row) for j in subs]
    alast_blk = [jnp.sum(a_blk[j], axis=0, keepdims=True) for j in subs]
    dt = per(lambda j, p: both(dt_blk[j], p))
    nexp = [both(-jnp.exp(alog_row), p) for p in pairs]
    alast = per(lambda j, p: both(alast_blk[j], p))
    dp = [both(d_row, p) for p in pairs]
    dt_lane = per(lambda j, p: jnp.where(lane_lo, dt[j][p][0], dt[j][p][1]))
    a_lane = per(lambda j, p: dt_lane[j][p] * jnp.where(lane_lo1, nexp[p][0], nexp[p][1]))
    a_row = per(lambda j, p: to_row(a_lane[j][p]))
    part = per(lambda j, p: jnp.where(causal2, a_row[j][p], 0.0))
    acum = per(lambda j, p: [jnp.sum(jnp.where(lane_lo, part[j][p], 0.0), axis=1, keepdims=True),
                             jnp.sum(jnp.where(lane_lo, 0.0, part[j][p]), axis=1, keepdims=True)])
    acum_col = per(lambda j, p: jnp.where(lane_lo, acum[j][p][0], acum[j][p][1]))
    acum_row = per(lambda j, p: to_row(acum_col[j][p]))
    lmat = per(lambda j, p: jnp.where(causal2, jnp.exp(jnp.where(causal2, acum_col[j][p] - acum_row[j][p], 0.0)), 0.0))
    cb = [_nt(cm[j], jnp.concatenate([bm[j], bm[j]], axis=0)) for j in subs]
    xdt = per(lambda j, p: x[j][p] * dt_lane[j][p])
    xdt_rows = per(lambda j, p: jnp.concatenate([jnp.where(lane_lo, xdt[j][p], 0.0),
                                                 jnp.where(lane_lo, 0.0, xdt[j][p])], axis=0))
    intra = per(lambda j, p: _nn(cb[j] * lmat[j][p], xdt_rows[j][p]))
    skip = per(lambda j, p: x[j][p] * jnp.where(lane_lo1, dp[p][0], dp[p][1]))
    eacc = per(lambda j, p: jnp.exp(acum_col[j][p]))
    alast_row = per(lambda j, p: jnp.where(lane_lo1, alast[j][p][0], alast[j][p][1]))
    wdec = per(lambda j, p: jnp.exp(alast_row[j][p] - acum_col[j][p]))
    scale = per(lambda j, p: jnp.exp(alast_row[j][p]))
    upd = per(lambda j, p: _tn(bm[j], xdt[j][p] * wdec[j][p]))
    y = []
    for j in subs:
        y.append([skip[j][p] + intra[j][p] + _nn(cm[j], hs[p]) * eacc[j][p] for p in pairs])
        hs = [hs[p] * scale[j][p] + upd[j][p] for p in pairs]
    return y, hs


def _ret_chunk(rq, rk, rv, rg, cos2, sin2, dmat, qdec, kdec, cdec, nw, r):
    subs, hs = range(len(rq)), range(len(rq[0]))
    per = lambda f: [[f(j, h) for h in hs] for j in subs]
    q = per(lambda j, h: rq[j][h] * cos2[j] + _swap_halves(rq[j][h]) * sin2[j])
    k = per(lambda j, h: (rk[j][h] * cos2[j] + _swap_halves(rk[j][h]) * sin2[j]) * (RET_DK ** -0.5))
    s = per(lambda j, h: _nt(q[j][h], k[j][h]) * dmat[h])
    intra = per(lambda j, h: _nn(s[j][h], rv[j][h]))
    upd = per(lambda j, h: _tn(k[j][h] * kdec[h], rv[j][h]))
    gate = per(lambda j, h: _silu(rg[j][h]))
    on = []
    for j in subs:
        on.append([_rms(intra[j][h] + _nn(q[j][h], r[h]) * qdec[h], nw) * gate[j][h] for h in hs])
        r = [r[h] * cdec[h] + upd[j][h] for h in hs]
    return on, r


MM_TILE = 1024


def _tile(n, pref=MM_TILE):
    return pref if n % pref == 0 else n


def _matmul(name, a, b, mode, m, n, k, tk=None, comm=None, out_dtype=F32):
    tm, tn = _tile(m), _tile(n)
    tk = _tile(k) if tk is None else tk
    nk = k // tk
    ca, cb = {"nn": (1, 0), "nt": (1, 1), "tn": (0, 0)}[mode]
    grid = (m // tm, n // tn, nk)
    n_comm = len(comm[1]) if comm else 0
    use_acc = out_dtype != F32 and nk > 1

    def at_step(which):
        return functools.reduce(jnp.logical_and, [pl.program_id(d) == (0 if which == 0 else grid[d] - 1)
                                                  for d in range(3)])

    def body(*refs):
        a_ref, b_ref = refs[:2]
        c_in = refs[2:2 + n_comm]
        o_ref = refs[2 + n_comm]
        c_out = refs[3 + n_comm:3 + 2 * n_comm]
        acc_ref = refs[3 + 2 * n_comm] if use_acc else o_ref
        sems = refs[3 + 2 * n_comm + use_acc:]
        if comm:
            @pl.when(at_step(0))
            def _():
                _comm_plan(comm[0], c_in, c_out, *sems)[0]()

        part = lax.dot_general(a_ref[...].astype(BF16), b_ref[...].astype(BF16),
                               (((ca,), (cb,)), ((), ())), preferred_element_type=F32)
        if nk == 1:
            o_ref[...] = part.astype(o_ref.dtype)
        else:
            kk = pl.program_id(2)

            @pl.when(kk == 0)
            def _():
                acc_ref[...] = part

            @pl.when(kk > 0)
            def _():
                acc_ref[...] += part

            if use_acc:
                @pl.when(kk == nk - 1)
                def _():
                    o_ref[...] = acc_ref[...].astype(o_ref.dtype)

        if comm:
            @pl.when(at_step(1))
            def _():
                _comm_plan(comm[0], c_in, c_out, *sems)[1]()

    a_spec = pl.BlockSpec((tk, tm), lambda i, j, kk: (kk, i)) if mode == "tn" else pl.BlockSpec((tm, tk), lambda i, j, kk: (i, kk))
    b_spec = pl.BlockSpec((tn, tk), lambda i, j, kk: (j, kk)) if mode == "nt" else pl.BlockSpec((tk, tn), lambda i, j, kk: (kk, j))
    hbm = pl.BlockSpec(memory_space=pl.ANY)
    res = pl.pallas_call(
        body, name=name, grid=grid,
        in_specs=[a_spec, b_spec] + [hbm] * n_comm,
        out_specs=[pl.BlockSpec((tm, tn), lambda i, j, kk: (i, j))] + [hbm] * n_comm,
        out_shape=[jax.ShapeDtypeStruct((m, n), out_dtype)] + (_comm_out_shapes(*comm) if comm else []),
        scratch_shapes=([pltpu.VMEM((tm, tn), F32)] if use_acc else []) + (_comm_scratch(n_comm) if comm else []),
        compiler_params=_cparams(("arbitrary",) * 3 if comm else ("parallel", "parallel", "arbitrary"), VMEM_LIMIT),
    )(a, b, *(comm[1] if comm else []))
    return res if comm else res[0]


def _rowmap(name, fn, length, tm, rows, params, out_ws=(), acc_ws=(), out_dtypes=None, cts=None, d_dtypes=None,
            place=None):
    nt_ = length // tm
    n_r, n_p = len(rows), len(params)
    n_in = n_r + n_p
    rspec = lambda bw, cbk: pl.BlockSpec((tm, bw), lambda i: (i, cbk))
    pspec = lambda w: pl.BlockSpec((1, w), lambda i: (0, 0))
    in_arrays = [r[0] for r in rows] + list(params)
    in_specs = [rspec(r[1], r[2]) for r in rows] + [pspec(p.shape[1]) for p in params]
    place = place or {}

    def load(refs):
        return [r[...].astype(F32) for r in refs]

    def placed(widths, dtypes, n_before):
        specs, shapes, extra, alias = [], [], [], {}
        for k, (w, dt) in enumerate(zip(widths, dtypes)):
            arr, total, cbk = place.get(k, (None, w, 0))
            specs.append(rspec(w, cbk))
            shapes.append(jax.ShapeDtypeStruct((length, total), dt))
            if arr is not None:
                alias[n_before + len(extra)] = k
                extra.append(arr)
        return specs, shapes, extra, alias

    def accumulate(a_refs, vals):
        first = pl.program_id(0) == 0
        for a_ref, v in zip(a_refs, vals):
            @pl.when(first)
            def _(a_ref=a_ref):
                a_ref[...] = jnp.zeros_like(a_ref)
            a_ref[...] += v

    hbm = pl.BlockSpec(memory_space=pl.ANY)
    if cts is None:
        n_o = len(out_ws)
        o_specs, o_shapes, extra, alias = placed(out_ws, out_dtypes or [F32] * n_o, n_in)

        def body(*refs):
            outs, accs = fn(*load(refs[:n_in]))
            o_refs = refs[n_in + len(extra):]
            for o_ref, o in zip(o_refs[:n_o], outs):
                o_ref[...] = o.astype(o_ref.dtype)
            accumulate(o_refs[n_o:], accs)

        res = pl.pallas_call(
            body, name=name, grid=(nt_,), in_specs=in_specs + [hbm] * len(extra),
            out_specs=o_specs + [pspec(w) for w in acc_ws],
            out_shape=o_shapes + [jax.ShapeDtypeStruct((1, w), F32) for w in acc_ws],
            input_output_aliases=alias, compiler_params=_cparams(("arbitrary",), VMEM_LIMIT),
        )(*in_arrays, *extra)
        return res[:n_o], res[n_o:]

    n_c = len(cts)
    o_specs, o_shapes, extra, alias = placed([r[1] for r in rows], d_dtypes or [F32] * n_r, n_in + n_c)

    def body(*refs):
        ins = load(refs[:n_in])
        _, vjp = jax.vjp(lambda rs, ps: fn(*rs, *ps)[0], ins[:n_r], ins[n_r:])
        d_rows, d_params = vjp(tuple(load(refs[n_in:n_in + n_c])))
        o_refs = refs[n_in + n_c + len(extra):]
        for o_ref, d in zip(o_refs[:n_r], d_rows):
            o_ref[...] = d.astype(o_ref.dtype)
        accumulate(o_refs[n_r:], d_params)

    res = pl.pallas_call(
        body, name=name, grid=(nt_,),
        in_specs=in_specs + [rspec(c[1], c[2]) for c in cts] + [hbm] * len(extra),
        out_specs=o_specs + [pspec(p.shape[1]) for p in params],
        out_shape=o_shapes + [jax.ShapeDtypeStruct((1, p.shape[1]), F32) for p in params],
        input_output_aliases=alias, compiler_params=_cparams(("arbitrary",), VMEM_LIMIT),
    )(*in_arrays, *[c[0] for c in cts], *extra)
    return res[:n_r], res[n_r:]


def _conv_shift(ext, k, tm, forward):
    s = CONV_W - 1 - k
    if forward:
        rolled = ext if s == 0 else pltpu.roll(ext, s, 0)
        return rolled[SUBLANE:, :]
    rolled = ext if s == 0 else pltpu.roll(ext, tm + SUBLANE - s, 0)
    return rolled[:tm, :]


def _conv_pre(x_ref, halo_ref, w_ref, b_ref, first, tm):
    halo = jnp.where(first, 0.0, halo_ref[...])
    ext = jnp.concatenate([halo, x_ref[...]], axis=0)
    w = w_ref[...]
    pre = b_ref[...] + jnp.zeros_like(x_ref[...])
    taps = []
    for k in range(CONV_W):
        tap = _conv_shift(ext, k, tm, True)
        taps.append(tap)
        pre = pre + tap * w[k:k + 1, :]
    return pre, taps


def _conv_fwd(proj, w, b, length, tm, tc):
    hb = tm // SUBLANE
    j0 = CONV_OFF // tc

    def body(x_ref, halo_ref, w_ref, b_ref, o_ref):
        pre, _ = _conv_pre(x_ref, halo_ref, w_ref, b_ref, pl.program_id(0) == 0, tm)
        o_ref[...] = _silu(pre)

    return pl.pallas_call(
        body, name="conv_fwd", grid=(length // tm, CONV_ALL // tc),
        in_specs=[pl.BlockSpec((tm, tc), lambda i, j: (i, j0 + j)),
                  pl.BlockSpec((SUBLANE, tc), lambda i, j: (jnp.maximum(i * hb - 1, 0), j0 + j)),
                  pl.BlockSpec((CONV_W, tc), lambda i, j: (0, j)),
                  pl.BlockSpec((1, tc), lambda i, j: (0, j))],
        out_specs=pl.BlockSpec((tm, tc), lambda i, j: (i, j)),
        out_shape=jax.ShapeDtypeStruct((length, CONV_ALL), F32),
        compiler_params=_cparams(("parallel", "parallel"), VMEM_LIMIT),
    )(proj, proj, w, b)


def _conv_bwd_pre(proj, w, b, dxc, length, tm, tc):
    hb = tm // SUBLANE
    j0 = CONV_OFF // tc

    def body(x_ref, halo_ref, w_ref, b_ref, dy_ref, dpre_ref, dw_ref, db_ref):
        i = pl.program_id(1)
        pre, taps = _conv_pre(x_ref, halo_ref, w_ref, b_ref, i == 0, tm)
        sg = jax.nn.sigmoid(pre)
        dpre = dy_ref[...] * (sg * (1.0 + pre * (1.0 - sg)))
        dpre_ref[...] = dpre

        @pl.when(i == 0)
        def _():
            dw_ref[...] = jnp.zeros_like(dw_ref)
            db_ref[...] = jnp.zeros_like(db_ref)

        for k in range(CONV_W):
            dw_ref[k:k + 1, :] += jnp.sum(dpre * taps[k], axis=0, keepdims=True)
        db_ref[...] += jnp.sum(dpre, axis=0, keepdims=True)

    return pl.pallas_call(
        body, name="conv_bwd_pre", grid=(CONV_ALL // tc, length // tm),
        in_specs=[pl.BlockSpec((tm, tc), lambda j, i: (i, j0 + j)),
                  pl.BlockSpec((SUBLANE, tc), lambda j, i: (jnp.maximum(i * hb - 1, 0), j0 + j)),
                  pl.BlockSpec((CONV_W, tc), lambda j, i: (0, j)),
                  pl.BlockSpec((1, tc), lambda j, i: (0, j)),
                  pl.BlockSpec((tm, tc), lambda j, i: (i, j))],
        out_specs=[pl.BlockSpec((tm, tc), lambda j, i: (i, j)),
                   pl.BlockSpec((CONV_W, tc), lambda j, i: (0, j)),
                   pl.BlockSpec((1, tc), lambda j, i: (0, j))],
        out_shape=[jax.ShapeDtypeStruct((length, CONV_ALL), F32),
                   jax.ShapeDtypeStruct((CONV_W, CONV_ALL), F32),
                   jax.ShapeDtypeStruct((1, CONV_ALL), F32)],
        compiler_params=_cparams(("parallel", "arbitrary"), VMEM_LIMIT),
    )(proj, proj, w, b, dxc)


def _conv_bwd_x(dpre, w, dproj, length, tm, tc):
    hb = tm // SUBLANE
    n_t = length // tm
    last_blk = length // SUBLANE - 1
    j0 = CONV_OFF // tc

    def body(d_ref, halo_ref, w_ref, _, o_ref):
        halo = jnp.where(pl.program_id(0) == n_t - 1, 0.0, halo_ref[...])
        ext = jnp.concatenate([d_ref[...], halo], axis=0)
        w = w_ref[...]
        acc = jnp.zeros_like(d_ref[...])
        for k in range(CONV_W):
            acc = acc + _conv_shift(ext, k, tm, False) * w[k:k + 1, :]
        o_ref[...] = acc.astype(o_ref.dtype)

    return pl.pallas_call(
        body, name="conv_bwd_x", grid=(n_t, CONV_ALL // tc),
        in_specs=[pl.BlockSpec((tm, tc), lambda i, j: (i, j)),
                  pl.BlockSpec((SUBLANE, tc), lambda i, j: (jnp.minimum((i + 1) * hb, last_blk), j)),
                  pl.BlockSpec((CONV_W, tc), lambda i, j: (0, j)),
                  pl.BlockSpec(memory_space=pl.ANY)],
        out_specs=pl.BlockSpec((tm, tc), lambda i, j: (i, j0 + j)),
        out_shape=jax.ShapeDtypeStruct(dproj.shape, dproj.dtype), input_output_aliases={3: 0},
        compiler_params=_cparams(("parallel", "parallel"), VMEM_LIMIT),
    )(dpre, dpre, w, dproj)


HEAD_SCAN_CHUNKS_PER_STEP = 4
assert GDN_HEADS == RET_HEADS


def _scan_steps(length):
    nc = length // CHUNK
    gs = HEAD_SCAN_CHUNKS_PER_STEP if nc % HEAD_SCAN_CHUNKS_PER_STEP == 0 else 1
    return gs, nc // gs, gs * CHUNK


def _head_tiles(ref, gs):
    return [[ref[j * CHUNK:(j + 1) * CHUNK, h * LANE:(h + 1) * LANE] for h in range(GDN_HEADS)] for j in range(gs)]


def _chunk_rows(ref, gs):
    return [ref[j * CHUNK:(j + 1) * CHUNK, :] for j in range(gs)]


def _gdn_fwd(xc, proj, alog, dtb, nw, length, comm=None):
    gs, steps, rows = _scan_steps(length)
    h_ = GDN_HEADS
    hw = h_ * LANE
    tn_ = h_ * CHUNK
    n_comm = len(comm[1]) if comm else 0

    def body(*refs):
        q_ref, k_ref, v_ref, gz_ref, sm_ref, al_ref, dt_ref, nw_ref = refs[:8]
        c_in = refs[8:8 + n_comm]
        on_ref, st_ref, t_ref = refs[8 + n_comm:11 + n_comm]
        c_out = refs[11 + n_comm:11 + 2 * n_comm]
        s_ref = refs[11 + 2 * n_comm]
        sems = refs[12 + 2 * n_comm:]

        @pl.when(pl.program_id(0) == 0)
        def _():
            s_ref[...] = jnp.zeros_like(s_ref)
            if comm:
                _comm_plan(comm[0], c_in, c_out, *sems)[0]()

        st_ref[...] = s_ref[...]
        lead = lambda ref: [ref[h] for h in range(h_)]
        on, s_new, t = _gdn_chunk(_head_tiles(q_ref, gs), _head_tiles(k_ref, gs), _head_tiles(v_ref, gs),
                                  _head_tiles(gz_ref, gs), _chunk_rows(sm_ref, gs), al_ref[...], dt_ref[...],
                                  nw_ref[...], lead(s_ref))
        for j in range(gs):
            t_ref[j] = t[j]
        for h in range(h_):
            for j in range(gs):
                on_ref[j * CHUNK:(j + 1) * CHUNK, h * LANE:(h + 1) * LANE] = on[j][h].astype(on_ref.dtype)
            s_ref[h] = s_new[h]

        if comm:
            @pl.when(pl.program_id(0) == steps - 1)
            def _():
                _comm_plan(comm[0], c_in, c_out, *sems)[1]()

    blk = lambda col: pl.BlockSpec((rows, hw), lambda c: (c, col // hw))
    one = pl.BlockSpec((1, LANE), lambda c: (0, 0))
    hbm = pl.BlockSpec(memory_space=pl.ANY)
    return pl.pallas_call(
        body, name="gdn_fwd_" + comm[0] if comm else "gdn_fwd", grid=(steps,),
        in_specs=[blk(XC("gq")), blk(XC("gk")), blk(XC("gv")), blk(_MINE["gz"]),
                  pl.BlockSpec((rows, LANE), lambda c: (c, SMALL_OFF // LANE)), one, one,
                  pl.BlockSpec((1, LANE), lambda c: (0, 0))] + [hbm] * n_comm,
        out_specs=[pl.BlockSpec((rows, hw), lambda c: (c, 0)),
                   pl.BlockSpec((h_, None, LANE, LANE), lambda c: (0, c, 0, 0)),
                   pl.BlockSpec((gs, tn_, tn_), lambda c: (c, 0, 0))] + [hbm] * n_comm,
        out_shape=[jax.ShapeDtypeStruct((length, MIX_W), BF16),
                   jax.ShapeDtypeStruct((h_, steps, LANE, LANE), F32),
                   jax.ShapeDtypeStruct((length // CHUNK, tn_, tn_), F32)]
                  + (_comm_out_shapes(*comm) if comm else []),
        scratch_shapes=[pltpu.VMEM((h_, LANE, LANE), F32)] + (_comm_scratch(n_comm) if comm else []),
        compiler_params=_cparams(("arbitrary",), VMEM_LIMIT),
    )(xc, xc, xc, proj, proj, alog, dtb, nw, *(comm[1] if comm else []))


def _gdn_bwd(xc, proj, alog, dtb, nw, st, tinv, dmix, length, comm=None):
    gs, steps, rows = _scan_steps(length)
    h_ = GDN_HEADS
    hw = h_ * LANE
    tn_ = h_ * CHUNK
    n_comm = len(comm[1]) if comm else 0
    n_in, n_out = 11, 6

    def body(*refs):
        q_ref, k_ref, v_ref, gz_ref, sm_ref, al_ref, dt_ref, nw_ref, st_ref, t_ref, do_ref = refs[:n_in]
        c_in = refs[n_in:n_in + n_comm]
        o0 = n_in + n_comm
        dqkv_ref, dgz_ref, dsm_ref, dal_ref, ddt_ref, dnw_ref = refs[o0:o0 + n_out]
        c_out = refs[o0 + n_out:o0 + n_out + n_comm]
        ds_ref = refs[o0 + n_out + n_comm]
        sems = refs[o0 + n_out + n_comm + 1:]

        @pl.when(pl.program_id(0) == 0)
        def _():
            ds_ref[...] = jnp.zeros_like(ds_ref)
            dal_ref[...] = jnp.zeros_like(dal_ref)
            ddt_ref[...] = jnp.zeros_like(ddt_ref)
            dnw_ref[...] = jnp.zeros_like(dnw_ref)
            if comm:
                _comm_plan(comm[0], c_in, c_out, *sems)[0]()

        lead = lambda ref: [ref[h] for h in range(h_)]
        ins = (_head_tiles(q_ref, gs), _head_tiles(k_ref, gs), _head_tiles(v_ref, gs), _head_tiles(gz_ref, gs),
               _chunk_rows(sm_ref, gs), al_ref[...], dt_ref[...], nw_ref[...], lead(st_ref))
        t_saved = [t_ref[j] for j in range(gs)]
        _, vjp = jax.vjp(lambda *a: _gdn_chunk(*a, t_saved=t_saved)[:2], *ins)
        dq, dk, dv, dgz, dsm, dal, ddt, dnw, ds = vjp((_head_tiles(do_ref, gs), lead(ds_ref)))
        for j in range(gs):
            rws = slice(j * CHUNK, (j + 1) * CHUNK)
            for h in range(h_):
                cols = slice(h * LANE, (h + 1) * LANE)
                for part, d in enumerate((dq, dk, dv)):
                    dqkv_ref[rws, part * hw + h * LANE:part * hw + (h + 1) * LANE] = d[j][h]
                dgz_ref[rws, cols] = dgz[j][h].astype(dgz_ref.dtype)
            dsm_ref[rws, :] = dsm[j]
        for h in range(h_):
            ds_ref[h] = ds[h]
        dal_ref[...] += dal
        ddt_ref[...] += ddt
        dnw_ref[...] += dnw

        if comm:
            @pl.when(pl.program_id(0) == steps - 1)
            def _():
                _comm_plan(comm[0], c_in, c_out, *sems)[1]()

    r = lambda c: steps - 1 - c
    blk = lambda col: pl.BlockSpec((rows, hw), lambda c: (r(c), col // hw))
    one = pl.BlockSpec((1, LANE), lambda c: (0, 0))
    hbm = pl.BlockSpec(memory_space=pl.ANY)
    return pl.pallas_call(
        body, name="gdn_bwd_" + comm[0] if comm else "gdn_bwd", grid=(steps,),
        in_specs=[blk(XC("gq")), blk(XC("gk")), blk(XC("gv")), blk(_MINE["gz"]),
                  pl.BlockSpec((rows, LANE), lambda c: (r(c), SMALL_OFF // LANE)), one, one,
                  pl.BlockSpec((1, LANE), lambda c: (0, 0)),
                  pl.BlockSpec((h_, None, LANE, LANE), lambda c: (0, r(c), 0, 0)),
                  pl.BlockSpec((gs, tn_, tn_), lambda c: (r(c), 0, 0)),
                  blk(0)] + [hbm] * n_comm,
        out_specs=[pl.BlockSpec((rows, 3 * hw), lambda c: (r(c), 0)), blk(_MINE["gz"]),
                   pl.BlockSpec((rows, LANE), lambda c: (r(c), 0)), one, one,
                   pl.BlockSpec((1, LANE), lambda c: (0, 0))] + [hbm] * n_comm,
        out_shape=[jax.ShapeDtypeStruct((length, CONV_ALL), F32), jax.ShapeDtypeStruct((length, N_PAD), BF16),
                   jax.ShapeDtypeStruct((length, LANE), F32),
                   jax.ShapeDtypeStruct((1, LANE), F32), jax.ShapeDtypeStruct((1, LANE), F32),
                   jax.ShapeDtypeStruct((1, LANE), F32)] + (_comm_out_shapes(*comm) if comm else []),
        scratch_shapes=[pltpu.VMEM((h_, LANE, LANE), F32)] + (_comm_scratch(n_comm) if comm else []),
        compiler_params=_cparams(("arbitrary",), VMEM_LIMIT),
    )(xc, xc, xc, proj, proj, alog, dtb, nw, st, tinv, dmix, *(comm[1] if comm else []))


SSD_CHUNKS_PER_STEP = 8


def _ssd_steps(length):
    nc = length // CHUNK
    gs = SSD_CHUNKS_PER_STEP if nc % SSD_CHUNKS_PER_STEP == 0 else 1
    return gs, nc // gs, gs * CHUNK


def _ssd_lanes(g):
    base = _MINE["sdt"] - SMALL_OFF
    return [[base + (g * PAIRS_PER_GROUP + p) * 2 + j for j in range(2)] for p in range(PAIRS_PER_GROUP)]


def _ssd_tiles(ref, gs):
    return [[ref[j * CHUNK:(j + 1) * CHUNK, p * LANE:(p + 1) * LANE] for p in range(PAIRS_PER_GROUP)]
            for j in range(gs)]


def _ssd_gate(y, z, w):
    return _rms(y * _silu(z), w)


def _ssd_fwd(xc, proj, alog_row, dtb_row, d_row, norm_w, mixed, length):
    gs, steps, rows = _ssd_steps(length)
    ppg = PAIRS_PER_GROUP
    gw = ppg * LANE

    def body(x_ref, b_ref, c_ref, sm_ref, al_ref, db_ref, dp_ref, z_ref, nw_ref, _, y_ref, o_ref, st_ref, hs_ref):
        @pl.when(pl.program_id(1) == 0)
        def _():
            hs_ref[...] = jnp.zeros_like(hs_ref)

        st_ref[...] = hs_ref[...]
        y, hs_new = _ssd_chunk(_ssd_lanes(pl.program_id(0)), _ssd_tiles(x_ref, gs), _chunk_rows(b_ref, gs),
                               _chunk_rows(c_ref, gs), _chunk_rows(sm_ref, gs), al_ref[...], db_ref[...],
                               dp_ref[...], [hs_ref[p] for p in range(ppg)])
        for p in range(ppg):
            hs_ref[p] = hs_new[p]
        for j in range(gs):
            rws = slice(j * CHUNK, (j + 1) * CHUNK)
            y_j = jnp.concatenate(y[j], axis=1)
            y_ref[rws, :] = y_j
            o_ref[rws, :] = _ssd_gate(y_j, z_ref[rws, :], nw_ref[...]).astype(o_ref.dtype)

    one = pl.BlockSpec((1, LANE), lambda g, c: (0, 0))
    return pl.pallas_call(
        body, name="ssd_fwd", grid=(SSD_GROUPS, steps),
        in_specs=[pl.BlockSpec((rows, gw), lambda g, c: (c, XC("sx") // gw + g)),
                  pl.BlockSpec((rows, LANE), lambda g, c: (c, XC("sB") // LANE + g)),
                  pl.BlockSpec((rows, LANE), lambda g, c: (c, XC("sC") // LANE + g)),
                  pl.BlockSpec((rows, LANE), lambda g, c: (c, SMALL_OFF // LANE)),
                  one, one, one,
                  pl.BlockSpec((rows, gw), lambda g, c: (c, _MINE["sz"] // gw + g)),
                  pl.BlockSpec((None, 1, gw), lambda g, c: (g, 0, 0)),
                  pl.BlockSpec(memory_space=pl.ANY)],
        out_specs=[pl.BlockSpec((rows, gw), lambda g, c: (c, g)),
                   pl.BlockSpec((rows, gw), lambda g, c: (c, 1 + g)),
                   pl.BlockSpec((None, ppg, None, LANE, SSD_N), lambda g, c: (g, 0, c, 0, 0))],
        out_shape=[jax.ShapeDtypeStruct((length, SSD_GROUPS * gw), F32),
                   jax.ShapeDtypeStruct(mixed.shape, mixed.dtype),
                   jax.ShapeDtypeStruct((SSD_GROUPS, ppg, steps, LANE, SSD_N), F32)],
        input_output_aliases={9: 1},
        scratch_shapes=[pltpu.VMEM((ppg, LANE, SSD_N), F32)],
        compiler_params=_cparams(("parallel", "arbitrary"), VMEM_LIMIT),
    )(xc, xc, xc, proj, alog_row, dtb_row, d_row, proj, norm_w, mixed)


def _ssd_bwd(xc, proj, alog_row, dtb_row, d_row, norm_w, st, y_b, dmix, dproj, dxc, length):
    gs, steps, rows = _ssd_steps(length)
    ppg = PAIRS_PER_GROUP
    gw = ppg * LANE

    def body(x_ref, b_ref, c_ref, sm_ref, al_ref, db_ref, dp_ref, z_ref, nw_ref, st_ref, y_ref, do_ref, _, __,
             dxc_ref, dz_ref, dsm_ref, dal_ref, ddb_ref, ddp_ref, dnw_ref, dhs_ref):
        g = pl.program_id(1)

        @pl.when(jnp.logical_and(pl.program_id(0) == 0, g == 0))
        def _():
            dhs_ref[...] = jnp.zeros_like(dhs_ref)
            dal_ref[...] = jnp.zeros_like(dal_ref)
            ddb_ref[...] = jnp.zeros_like(ddb_ref)
            ddp_ref[...] = jnp.zeros_like(ddp_ref)
            dnw_ref[...] = jnp.zeros_like(dnw_ref)

        pairs = range(ppg)
        dy, dnw = [], jnp.zeros((1, gw), F32)
        for j in range(gs):
            rws = slice(j * CHUNK, (j + 1) * CHUNK)
            _, gate_vjp = jax.vjp(_ssd_gate, y_ref[rws, :], z_ref[rws, :], nw_ref[...])
            dy_j, dz_j, dnw_j = gate_vjp(do_ref[rws, :].astype(F32))
            dz_ref[rws, :] = dz_j.astype(dz_ref.dtype)
            dy.append([dy_j[:, p * LANE:(p + 1) * LANE] for p in pairs])
            dnw = dnw + dnw_j
        dnw_ref[g] += dnw
        ins = (_ssd_tiles(x_ref, gs), _chunk_rows(b_ref, gs), _chunk_rows(c_ref, gs), _chunk_rows(sm_ref, gs),
               al_ref[...], db_ref[...], dp_ref[...], [st_ref[p] for p in pairs])
        _, vjp = jax.vjp(functools.partial(_ssd_chunk, _ssd_lanes(g)), *ins)
        dx, dbm, dcm, dsm, dal, ddb, ddp, dhs = vjp((dy, [dhs_ref[g, p] for p in pairs]))
        for grp in range(SSD_GROUPS):
            @pl.when(g == grp)
            def _(grp=grp):
                for j in range(gs):
                    rws = slice(j * CHUNK, (j + 1) * CHUNK)
                    for p in pairs:
                        col = grp * gw + p * LANE
                        dxc_ref[rws, col:col + LANE] = dx[j][p]
                    cb_ = XC("sB") - XC("sx") + grp * SSD_N
                    cc_ = XC("sC") - XC("sx") + grp * SSD_N
                    dxc_ref[rws, cb_:cb_ + SSD_N] = dbm[j]
                    dxc_ref[rws, cc_:cc_ + SSD_N] = dcm[j]
        for j in range(gs):
            dsm_ref[j * CHUNK:(j + 1) * CHUNK, :] = dsm[j]
        for p in pairs:
            dhs_ref[g, p] = dhs[p]
        dal_ref[g] += dal
        ddb_ref[g] += ddb
        ddp_ref[g] += ddp

    r = lambda c: steps - 1 - c
    one = pl.BlockSpec((1, LANE), lambda c, g: (0, 0))
    acc = pl.BlockSpec((SSD_GROUPS, 1, LANE), lambda c, g: (0, 0, 0))
    prm = jax.ShapeDtypeStruct((SSD_GROUPS, 1, LANE), F32)
    sz_blk = _MINE["sz"] // gw
    return pl.pallas_call(
        body, name="ssd_bwd", grid=(steps, SSD_GROUPS),
        in_specs=[pl.BlockSpec((rows, gw), lambda c, g: (r(c), XC("sx") // gw + g)),
                  pl.BlockSpec((rows, LANE), lambda c, g: (r(c), XC("sB") // LANE + g)),
                  pl.BlockSpec((rows, LANE), lambda c, g: (r(c), XC("sC") // LANE + g)),
                  pl.BlockSpec((rows, LANE), lambda c, g: (r(c), SMALL_OFF // LANE)),
                  one, one, one,
                  pl.BlockSpec((rows, gw), lambda c, g: (r(c), sz_blk + g)),
                  pl.BlockSpec((None, 1, gw), lambda c, g: (g, 0, 0)),
                  pl.BlockSpec((None, ppg, None, LANE, SSD_N), lambda c, g: (g, 0, r(c), 0, 0)),
                  pl.BlockSpec((rows, gw), lambda c, g: (r(c), g)),
                  pl.BlockSpec((rows, gw), lambda c, g: (r(c), 1 + g)),
                  pl.BlockSpec(memory_space=pl.ANY), pl.BlockSpec(memory_space=pl.ANY)],
        out_specs=[pl.BlockSpec((rows, CONV_CH), lambda c, g: (r(c), 1)),
                   pl.BlockSpec((rows, gw), lambda c, g: (r(c), sz_blk + g)),
                   pl.BlockSpec((None, rows, LANE), lambda c, g: (g, r(c), 0)), acc, acc, acc,
                   pl.BlockSpec((SSD_GROUPS, 1, gw), lambda c, g: (0, 0, 0))],
        out_shape=[jax.ShapeDtypeStruct(dxc.shape, dxc.dtype), jax.ShapeDtypeStruct(dproj.shape, dproj.dtype),
                   jax.ShapeDtypeStruct((SSD_GROUPS, length, LANE), F32), prm, prm, prm,
                   jax.ShapeDtypeStruct((SSD_GROUPS, 1, gw), F32)],
        input_output_aliases={12: 1, 13: 0},
        scratch_shapes=[pltpu.VMEM((SSD_GROUPS, ppg, LANE, SSD_N), F32)],
        compiler_params=_cparams(("arbitrary", "arbitrary"), VMEM_LIMIT),
    )(xc, xc, xc, proj, alog_row, dtb_row, d_row, proj, norm_w, st, y_b, dmix, dproj, dxc)


def _ret_fwd(proj, cos2, sin2, dmat, qdec, kdec, cdec, nw, mixed, length):
    gs, steps, rows = _scan_steps(length)
    h_ = RET_HEADS

    def body(q_ref, k_ref, v_ref, g_ref, cos_ref, sin_ref, dm_ref, qd_ref, kd_ref, cd_ref, nw_ref, _,
             on_ref, st_ref, r_ref):
        @pl.when(pl.program_id(0) == 0)
        def _():
            r_ref[...] = jnp.zeros_like(r_ref)

        st_ref[...] = r_ref[...]
        lead = lambda ref: [ref[h] for h in range(h_)]
        on, r_new = _ret_chunk(_head_tiles(q_ref, gs), _head_tiles(k_ref, gs), _head_tiles(v_ref, gs),
                               _head_tiles(g_ref, gs), _chunk_rows(cos_ref, gs), _chunk_rows(sin_ref, gs),
                               lead(dm_ref), lead(qd_ref), lead(kd_ref), lead(cd_ref), nw_ref[...], lead(r_ref))
        for h in range(h_):
            for j in range(gs):
                on_ref[j * CHUNK:(j + 1) * CHUNK, h * LANE:(h + 1) * LANE] = on[j][h].astype(on_ref.dtype)
            r_ref[h] = r_new[h]

    hw = h_ * LANE
    blk = lambda name: pl.BlockSpec((rows, hw), lambda c: (c, _MINE[name] // hw))
    tab = pl.BlockSpec((rows, LANE), lambda c: (c, 0))
    full = lambda *s: pl.BlockSpec(s, lambda c: (0,) * len(s))
    return pl.pallas_call(
        body, name="ret_fwd", grid=(steps,),
        in_specs=[blk("rq"), blk("rk"), blk("rv"), blk("rg"), tab, tab,
                  full(h_, CHUNK, CHUNK), full(h_, CHUNK, 1), full(h_, CHUNK, 1), full(h_, 1, 1), full(1, LANE),
                  pl.BlockSpec(memory_space=pl.ANY)],
        out_specs=[pl.BlockSpec((rows, hw), lambda c: (c, (MIX_W - hw) // hw)),
                   pl.BlockSpec((h_, None, LANE, LANE), lambda c: (0, c, 0, 0))],
        out_shape=[jax.ShapeDtypeStruct(mixed.shape, mixed.dtype),
                   jax.ShapeDtypeStruct((h_, steps, LANE, LANE), F32)],
        input_output_aliases={11: 0},
        scratch_shapes=[pltpu.VMEM((h_, LANE, LANE), F32)],
        compiler_params=_cparams(("arbitrary",), VMEM_LIMIT),
    )(proj, proj, proj, proj, cos2, sin2, dmat, qdec, kdec, cdec, nw, mixed)


def _ret_bwd(proj, cos2, sin2, dmat, qdec, kdec, cdec, nw, st, dmix, dproj, length):
    gs, steps, rows = _scan_steps(length)
    h_ = RET_HEADS
    hw = h_ * LANE

    def body(q_ref, k_ref, v_ref, g_ref, cos_ref, sin_ref, dm_ref, qd_ref, kd_ref, cd_ref, nw_ref, st_ref,
             do_ref, _, dqkvg_ref, dnw_ref, dr_ref):
        @pl.when(pl.program_id(0) == 0)
        def _():
            dr_ref[...] = jnp.zeros_like(dr_ref)
            dnw_ref[...] = jnp.zeros_like(dnw_ref)

        lead = lambda ref: [ref[h] for h in range(h_)]
        consts = (_chunk_rows(cos_ref, gs), _chunk_rows(sin_ref, gs), lead(dm_ref), lead(qd_ref), lead(kd_ref),
                  lead(cd_ref))
        f = lambda q, k, v, g, w_, r_: _ret_chunk(q, k, v, g, *consts, w_, r_)
        _, vjp = jax.vjp(f, _head_tiles(q_ref, gs), _head_tiles(k_ref, gs), _head_tiles(v_ref, gs),
                         _head_tiles(g_ref, gs), nw_ref[...], lead(st_ref))
        dq, dk, dv, dg, dnw, dr = vjp((_head_tiles(do_ref, gs), lead(dr_ref)))
        for h in range(h_):
            for j in range(gs):
                rws = slice(j * CHUNK, (j + 1) * CHUNK)
                for part, d in enumerate((dq, dk, dv, dg)):
                    col = part * hw + h * LANE
                    dqkvg_ref[rws, col:col + LANE] = d[j][h].astype(dqkvg_ref.dtype)
            dr_ref[h] = dr[h]
        dnw_ref[...] += dnw

    r = lambda c: steps - 1 - c
    blk = lambda name: pl.BlockSpec((rows, hw), lambda c: (r(c), _MINE[name] // hw))
    tab = pl.BlockSpec((rows, LANE), lambda c: (r(c), 0))
    full = lambda *s: pl.BlockSpec(s, lambda c: (0,) * len(s))
    assert _MINE["rq"] % (4 * hw) == 0 and [_MINE[n] - _MINE["rq"] for n in ("rk", "rv", "rg")] == [hw, 2 * hw, 3 * hw]
    return pl.pallas_call(
        body, name="ret_bwd", grid=(steps,),
        in_specs=[blk("rq"), blk("rk"), blk("rv"), blk("rg"), tab, tab,
                  full(h_, CHUNK, CHUNK), full(h_, CHUNK, 1), full(h_, CHUNK, 1), full(h_, 1, 1), full(1, LANE),
                  pl.BlockSpec((h_, None, LANE, LANE), lambda c: (0, r(c), 0, 0)),
                  pl.BlockSpec((rows, hw), lambda c: (r(c), (MIX_W - hw) // hw)),
                  pl.BlockSpec(memory_space=pl.ANY)],
        out_specs=[pl.BlockSpec((rows, 4 * hw), lambda c: (r(c), _MINE["rq"] // (4 * hw))), full(1, LANE)],
        out_shape=[jax.ShapeDtypeStruct(dproj.shape, dproj.dtype), jax.ShapeDtypeStruct((1, LANE), F32)],
        input_output_aliases={13: 0},
        scratch_shapes=[pltpu.VMEM((h_, LANE, LANE), F32)],
        compiler_params=_cparams(("arbitrary",), VMEM_LIMIT),
    )(proj, proj, proj, proj, cos2, sin2, dmat, qdec, kdec, cdec, nw, st, dmix, dproj)


def _mesh_pos():
    x, y, c = lax.axis_index("x"), lax.axis_index("y"), lax.axis_index("c")
    return x, y, c, 4 * x + 2 * y + c


def _peer(x, y, c, mask):
    return (x ^ ((mask >> 2) & 1), y ^ ((mask >> 1) & 1), c ^ (mask & 1))


def _gather_plan(ins, outs, send_sems, recv_sems, local_sems):
    x, y, c, me = _mesh_pos()
    n = len(ins)
    idx = lambda px, py, pc: 4 * px + 2 * py + pc
    sib = (x, y, 1 - c)
    chips = [(1 - x, y), (x, 1 - y), (1 - x, 1 - y)]
    local = [pltpu.make_async_copy(ins[a], outs[a].at[me], local_sems.at[a]) for a in range(n)]

    def copy(a, k, src, slab, to):
        return pltpu.make_async_remote_copy(
            src_ref=src, dst_ref=outs[a].at[slab], send_sem=send_sems.at[7 * a + k],
            recv_sem=recv_sems.at[7 * a + k], device_id=to, device_id_type=pl.DeviceIdType.MESH)

    first = [copy(a, 0, ins[a], me, sib) for a in range(n)]
    first += [copy(a, 1 + j, ins[a], me, (*chip, c)) for a in range(n) for j, chip in enumerate(chips)]
    landed = [[copy(a, 1 + j, ins[a], idx(*chip, c), (*chip, c)) for a in range(n)] for j, chip in enumerate(chips)]
    passed = [[copy(a, 4 + j, outs[a].at[idx(*chip, c)], idx(*chip, c), sib) for a in range(n)]
              for j, chip in enumerate(chips)]
    from_sib = [copy(a, 0, ins[a], idx(*sib), sib) for a in range(n)]
    from_sib += [copy(a, 4 + j, ins[a], idx(*chip, 1 - c), sib) for a in range(n) for j, chip in enumerate(chips)]

    def start():
        for cp in local + first:
            cp.start()

    def wait():
        for j in range(len(chips)):
            for a in range(n):
                landed[j][a].wait_recv()
                passed[j][a].start()
        for cp in from_sib:
            cp.wait_recv()
        for cp in first + [cp for row in passed for cp in row]:
            cp.wait_send()
        for cp in local:
            cp.wait()

    return start, wait


def _comm_plan(kind, ins, outs, send_sems, recv_sems, local_sems):
    if kind == "gather":
        return _gather_plan(ins, outs, send_sems, recv_sems, local_sems)
    x, y, c, me = _mesh_pos()
    n = len(ins)
    src = (lambda a, idx: ins[a]) if kind == "gather" else (lambda a, idx: ins[a].at[idx])
    local = [pltpu.make_async_copy(src(a, me), outs[a].at[me], local_sems.at[a]) for a in range(n)]
    sends, recvs = [], []
    for mask in range(1, N_DEV):
        px, py, pc = _peer(x, y, c, mask)
        pidx = 4 * px + 2 * py + pc
        for a in range(n):
            k = (mask - 1) * n + a
            copy = lambda s, d, k=k: pltpu.make_async_remote_copy(
                src_ref=s, dst_ref=d, send_sem=send_sems.at[k], recv_sem=recv_sems.at[k],
                device_id=(px, py, pc), device_id_type=pl.DeviceIdType.MESH)
            sends.append(copy(src(a, pidx), outs[a].at[me]))
            recvs.append(copy(src(a, me), outs[a].at[pidx]))

    def start():
        for cp in local + sends:
            cp.start()

    def wait():
        for cp in sends:
            cp.wait_send()
        for cp in recvs:
            cp.wait_recv()
        for cp in local:
            cp.wait()

    return start, wait


def _comm_out_shapes(kind, arrays):
    return [jax.ShapeDtypeStruct(((N_DEV,) + s.shape) if kind == "gather" else s.shape, s.dtype) for s in arrays]


def _comm_scratch(n):
    k_tot = (N_DEV - 1) * n
    return [pltpu.SemaphoreType.DMA((k_tot,)), pltpu.SemaphoreType.DMA((k_tot,)), pltpu.SemaphoreType.DMA((n,))]


def _comm_call(name, kind, arrays):
    n = len(arrays)

    def body(*refs):
        start, wait = _comm_plan(kind, refs[:n], refs[n:2 * n], *refs[2 * n:])
        start()
        wait()

    hbm = pl.BlockSpec(memory_space=pl.ANY)
    return pl.pallas_call(
        body, name=name, in_specs=[hbm] * n, out_specs=[hbm] * n, out_shape=_comm_out_shapes(kind, arrays),
        scratch_shapes=_comm_scratch(n), compiler_params=pltpu.CompilerParams(has_side_effects=True),
    )(*arrays)


def _adam(name, parts, w, m, v, tr, layer=None, prev=None):
    p_, r_, cw = parts.shape
    c1 = 1.0 - ADAM_B1 ** ADAM_STEP
    c2 = 1.0 - ADAM_B2 ** ADAM_STEP
    n_prev = 4 if prev else 0

    def body(p_ref, w_ref, m_ref, v_ref, *rest):
        g_ref, d_ref, nm_ref, nv_ref = rest[n_prev:]
        g = p_ref[0].astype(F32)
        for i in range(1, p_):
            g = g + p_ref[i].astype(F32)
        nm = ADAM_B1 * m_ref[...] + (1.0 - ADAM_B1) * g
        nv = ADAM_B2 * v_ref[...] + (1.0 - ADAM_B2) * (g * g)
        d_ref[...] = -ADAM_LR * ((nm / c1) / (jnp.sqrt(nv / c2) + ADAM_EPS) + ADAM_WD * w_ref[...])
        g_ref[...] = g
        nm_ref[...] = nm
        nv_ref[...] = nv

    if layer is None:
        spec = pl.BlockSpec((tr, cw), lambda i: (i, 0))
        shp = jax.ShapeDtypeStruct((r_, cw), F32)
    else:
        spec = pl.BlockSpec((None, tr, cw), lambda i: (layer, i, 0))
        shp = jax.ShapeDtypeStruct((DEPTH, r_, cw), F32)
    return pl.pallas_call(
        body, name=name, grid=(r_ // tr,),
        in_specs=[pl.BlockSpec((p_, tr, cw), lambda i: (0, i, 0)), spec, spec, spec]
                 + [pl.BlockSpec(memory_space=pl.ANY)] * n_prev,
        out_specs=[spec] * 4, out_shape=[shp] * 4,
        input_output_aliases={4 + i: i for i in range(n_prev)},
        compiler_params=_cparams(("parallel",), VMEM_LIMIT),
    )(parts, w, m, v, *(prev or ()))


def _perm_cols(w_full):
    parts = [w_full[:, _ORIG[n][0]:_ORIG[n][0] + _ORIG[n][1]] for n in _ORDER]
    parts.append(jnp.zeros((w_full.shape[0], N_PAD - N_USED), w_full.dtype))
    return jnp.concatenate(parts, axis=1)


def _unperm_cols(g):
    by_orig = sorted(_ORDER, key=lambda n: _ORIG[n][0])
    return jnp.concatenate([g[:, _MINE[n]:_MINE[n] + _ORIG[n][1]] for n in by_orig], axis=1)


def _ret_consts(length):
    lg = jnp.log(1.0 - 2.0 ** (-5.0 - jnp.arange(RET_HEADS, dtype=F32)))
    idx = jnp.arange(CHUNK, dtype=F32)
    rel = idx[:, None] - idx[None, :]
    dmat = jnp.where(rel[None] >= 0, jnp.exp(jnp.maximum(rel, 0.0)[None] * lg[:, None, None]), 0.0)
    qdec = jnp.exp((idx[:, None] + 1.0) * lg[None, :]).T[:, :, None]
    kdec = jnp.exp((CHUNK - 1.0 - idx)[:, None] * lg[None, :]).T[:, :, None]
    cdec = jnp.exp(CHUNK * lg)[:, None, None]
    half = RET_DK // 2
    inv = ROPE_BASE ** (-jnp.arange(half, dtype=F32) / half)
    ang = jnp.arange(length, dtype=jnp.int32).astype(F32)[:, None] * inv[None, :]
    cos, sin = jnp.cos(ang), jnp.sin(ang)
    return (jnp.concatenate([cos, cos], axis=1), jnp.concatenate([-sin, sin], axis=1), dmat, qdec, kdec, cdec)


def _tiles(length, pref=512):
    return pref if length % pref == 0 else length


def _layer_fwd(x, p, consts, length, last, target, comm=None):
    tm = _tiles(length)
    (h,), _ = _rowmap("pre_norm", lambda x_, w_: ((_rms(x_, w_),), ()), length, tm,
                      [(x, D_MODEL, 0)], [p["pre_norm"]], out_ws=[D_MODEL], out_dtypes=[BF16])
    proj = _matmul("in_proj", h, p["w_in"], "nn", length, N_PAD, D_MODEL)
    xc = _conv_fwd(proj, p["conv_w"], p["conv_b"], length, _tiles(length, 1024), 512)
    mixed, st_a, tinv, *got = _gdn_fwd(xc, proj, p["gdn_A_log"], p["gdn_dt_bias"], p["gdn_norm"], length, comm)
    y_b, mixed, st_b = _ssd_fwd(xc, proj, p["ssd_A_log"], p["ssd_dt_bias"], p["ssd_D"], p["ssd_norm"], mixed, length)
    mixed, st_c = _ret_fwd(proj, *consts, p["ret_norm"], mixed, length)
    w_out = p["w_out"](got) if callable(p["w_out"]) else p["w_out"]
    out = _matmul("out_proj", mixed, w_out, "nn", length, D_MODEL, MIX_W)
    res = dict(x=x, h=h, proj=proj, xc=xc, st_a=st_a, tinv=tinv, st_b=st_b, st_c=st_c, y_b=y_b, mixed=mixed,
               out=out, w_out=w_out)
    if not last:
        (y,), _ = _rowmap("post_norm", lambda x_, o_, w_: ((x_ + _rms(o_, w_),), ()), length, tm,
                          [(x, D_MODEL, 0), (out, D_MODEL, 0)], [p["post_norm"]], out_ws=[D_MODEL])
        return y, res, None, got

    def head(x_, o_, t_, w_):
        normed, post_vjp = jax.vjp(_rms, o_, w_)
        e = x_ + normed - t_
        row = jnp.mean(e * e, axis=-1, keepdims=True)
        loss = 0.5 * jnp.sum(row, axis=0, keepdims=True)
        dy_ = e * (1.0 / D_MODEL)
        d_out, dw = post_vjp(dy_)
        return (dy_, d_out), (loss + jnp.zeros((1, LANE), F32), dw)

    (dy, d_out), (loss, dw_post) = _rowmap(
        "loss_head", head, length, tm, [(x, D_MODEL, 0), (out, D_MODEL, 0), (target, D_MODEL, 0)], [p["post_norm"]],
        out_ws=[D_MODEL, D_MODEL], out_dtypes=[F32, BF16], acc_ws=[LANE, D_MODEL])
    res["post_bwd"] = (d_out, dw_post)
    return None, res, (dy, loss[0, 0]), got


def _layer_bwd(dy, p, res, consts, length, comm_scan=None, comm_dx=None):
    tm = _tiles(length)
    g = {}
    if "post_bwd" in res:
        d_out, g["post_norm"] = res["post_bwd"]
    else:
        (d_out,), (g["post_norm"],) = _rowmap("post_norm_bwd", lambda o_, w_: ((_rms(o_, w_),), ()), length, tm,
                                              [(res["out"], D_MODEL, 0)], [p["post_norm"]], cts=[(dy, D_MODEL, 0)],
                                              d_dtypes=[BF16])
    dmix = _matmul("out_proj_dx", d_out, res["w_out"], "nt", length, MIX_W, D_MODEL)
    g["w_out"] = _matmul("out_proj_dw", res["mixed"], d_out, "tn", MIX_W, D_MODEL, length, out_dtype=BF16)
    comm = comm_scan(g) if comm_scan else None
    proj, xc = res["proj"], res["xc"]
    (dxc, dproj, dsm_a, g["gdn_A_log"], g["gdn_dt_bias"], g["gdn_norm"], *got) = _gdn_bwd(
        xc, proj, p["gdn_A_log"], p["gdn_dt_bias"], p["gdn_norm"], res["st_a"], res["tinv"], dmix, length, comm)
    dxc, dproj, dsm_b, g["ssd_A_log"], g["ssd_dt_bias"], g["ssd_D"], g["ssd_norm"] = _ssd_bwd(
        xc, proj, p["ssd_A_log"], p["ssd_dt_bias"], p["ssd_D"], p["ssd_norm"], res["st_b"], res["y_b"], dmix,
        dproj, dxc, length)
    dproj, g["ret_norm"] = _ret_bwd(proj, *consts, p["ret_norm"], res["st_c"], dmix, dproj, length)
    tmc = _tiles(length, 1024)
    dpre, g["conv_w"], g["conv_b"] = _conv_bwd_pre(proj, p["conv_w"], p["conv_b"], dxc, length, tmc, 512)
    dproj = _conv_bwd_x(dpre, p["conv_w"], dproj, length, tmc, 512)
    pad = SMALL_OFF % 512 + 512 - LANE
    (dproj,), _ = _rowmap("dsmall", lambda a_, b_, c_: ((jnp.concatenate([a_ + b_ + c_, jnp.zeros((a_.shape[0], pad), F32)], axis=1),), ()),
                          length, tm, [(dsm_a, LANE, 0), (dsm_b[0], LANE, 0), (dsm_b[1], LANE, 0)], [],
                          out_ws=[512], out_dtypes=[BF16], place={0: (dproj, N_PAD, SMALL_OFF // 512)})
    g["w_in"] = _matmul("in_proj_dw", res["h"], dproj, "tn", D_MODEL, N_PAD, length, out_dtype=BF16)
    if comm_dx:
        dh, *got_dx = _matmul("in_proj_dx_scatter", dproj, p["w_in"], "nt", length, D_MODEL, N_PAD, comm=comm_dx(g))
    else:
        dh, got_dx = _matmul("in_proj_dx", dproj, p["w_in"], "nt", length, D_MODEL, N_PAD), []
    (dx,), (g["pre_norm"],) = _rowmap(
        "pre_norm_bwd", lambda x_, w_: ((_rms(x_, w_), x_), ()), length, tm,
        [(res["x"], D_MODEL, 0)], [p["pre_norm"]], cts=[(dh, D_MODEL, 0), (dy, D_MODEL, 0)])
    return dx, g, got, got_dx


def _lane_row(vals, piece):
    lo = _MINE[piece] - SMALL_OFF
    return jnp.pad(vals[None], ((0, 0), (lo, LANE - lo - vals.shape[0])))


def _from_lane_row(row, piece, n):
    lo = _MINE[piece] - SMALL_OFF
    return row.reshape(-1, LANE).sum(axis=0)[lo:lo + n]


def _make_layer(l, w_in_full, w_out_full, conv_full, sw):
    return dict(
        pre_norm=sw["pre_norm"][l][None], post_norm=sw["post_norm"][l][None], w_in=_perm_cols(w_in_full),
        w_out=w_out_full,
        conv_w=jnp.concatenate([conv_full[0], conv_full[1]], axis=1),
        conv_b=jnp.concatenate([jnp.zeros((CONV_CH,), F32), sw["ssd_conv_b"][l]])[None],
        gdn_A_log=_lane_row(sw["gdn_A_log"][l], "ga"), gdn_dt_bias=_lane_row(sw["gdn_dt_bias"][l], "ga"),
        gdn_norm=sw["gdn_norm"][l][None],
        ssd_A_log=_lane_row(sw["ssd_A_log"][l], "sdt"), ssd_dt_bias=_lane_row(sw["ssd_dt_bias"][l], "sdt"),
        ssd_D=_lane_row(sw["ssd_D"][l], "sdt"),
        ssd_norm=sw["ssd_norm"][l].reshape(SSD_GROUPS, 1, -1),
        ret_norm=sw["ret_norm"][l][None])


def _local_step(xb, tgt, layer0, make_layer1, length, fwd_comm=None, comm_scan=None, comm_dx=None):
    consts = _ret_consts(length)
    y0, res0, _, got = _layer_fwd(xb, layer0, consts, length, False, tgt, fwd_comm)
    layer1 = make_layer1(got)
    _, res1, (dy, loss_local), _ = _layer_fwd(y0, layer1, consts, length, True, tgt)
    dy, grads1, _, _ = _layer_bwd(dy, layer1, res1, consts, length)
    dx, grads0, recv_scan, recv_dx = _layer_bwd(
        dy, layer0, res0, consts, length,
        (lambda g0: comm_scan(grads1, g0)) if comm_scan else None, comm_dx)
    return loss_local, dx, [grads0, grads1], recv_scan, recv_dx


_SMALL = ["pre_norm", "post_norm", "gdn_A_log", "gdn_dt_bias", "gdn_norm", "ssd_conv_b", "ssd_A_log",
          "ssd_dt_bias", "ssd_D", "ssd_norm", "ret_norm"]


def _pack_small(arrs):
    rows = []
    for n in _SMALL:
        flat = arrs[n].reshape(-1)
        pad = (-flat.shape[0]) % LANE
        rows.append(jnp.pad(flat, (0, pad)).reshape(-1, LANE))
    out = jnp.concatenate(rows, axis=0)
    return jnp.pad(out, ((0, (-out.shape[0]) % SUBLANE), (0, 0)))


def _unpack_small(packed, like):
    out, r = {}, 0
    for n in _SMALL:
        cnt = like[n].size
        nrow = -(-cnt // LANE)
        out[n] = packed[r:r + nrow].reshape(-1)[:cnt].reshape(like[n].shape)
        r += nrow
    return out


def kernel(x, pre_norm, post_norm, w_in, gdn_conv, gdn_A_log, gdn_dt_bias, gdn_norm, ssd_conv, ssd_conv_b, ssd_A_log, ssd_dt_bias, ssd_D, ssd_norm, ret_norm, w_out, loss_target, m_pre_norm, m_post_norm, m_w_in, m_gdn_conv, m_gdn_A_log, m_gdn_dt_bias, m_gdn_norm, m_ssd_conv, m_ssd_conv_b, m_ssd_A_log, m_ssd_dt_bias, m_ssd_D, m_ssd_norm, m_ret_norm, m_w_out, v_pre_norm, v_post_norm, v_w_in, v_gdn_conv, v_gdn_A_log, v_gdn_dt_bias, v_gdn_norm, v_ssd_conv, v_ssd_conv_b, v_ssd_A_log, v_ssd_dt_bias, v_ssd_D, v_ssd_norm, v_ret_norm, v_w_out):
    length = x.shape[1]
    xb = x[0]
    tgt = loss_target[0]
    small_w = dict(pre_norm=pre_norm, post_norm=post_norm, gdn_A_log=gdn_A_log, gdn_dt_bias=gdn_dt_bias,
                   gdn_norm=gdn_norm, ssd_conv_b=ssd_conv_b, ssd_A_log=ssd_A_log, ssd_dt_bias=ssd_dt_bias,
                   ssd_D=ssd_D, ssd_norm=ssd_norm, ret_norm=ret_norm)
    small_m = dict(pre_norm=m_pre_norm, post_norm=m_post_norm, gdn_A_log=m_gdn_A_log, gdn_dt_bias=m_gdn_dt_bias,
                   gdn_norm=m_gdn_norm, ssd_conv_b=m_ssd_conv_b, ssd_A_log=m_ssd_A_log, ssd_dt_bias=m_ssd_dt_bias,
                   ssd_D=m_ssd_D, ssd_norm=m_ssd_norm, ret_norm=m_ret_norm)
    small_v = dict(pre_norm=v_pre_norm, post_norm=v_post_norm, gdn_A_log=v_gdn_A_log, gdn_dt_bias=v_gdn_dt_bias,
                   gdn_norm=v_gdn_norm, ssd_conv_b=v_ssd_conv_b, ssd_A_log=v_ssd_A_log, ssd_dt_bias=v_ssd_dt_bias,
                   ssd_D=v_ssd_D, ssd_norm=v_ssd_norm, ret_norm=v_ret_norm)

    w_in_b, w_out_b = w_in.astype(BF16), w_out.astype(BF16)
    conv_shard = jnp.stack([gdn_conv, ssd_conv], axis=1)
    full_out = lambda g_out: g_out.reshape(MIX_W, D_MODEL)

    def assemble(l, g_in, w_out_full, g_conv):
        w_in_full = g_in.transpose(1, 0, 2).reshape(D_MODEL, N_IN)
        conv_full = g_conv.transpose(1, 2, 0, 3).reshape(2, CONV_W, CONV_CH)
        return _make_layer(l, w_in_full, w_out_full, conv_full, small_w)

    slab_in = lambda g: _unperm_cols(g["w_in"]).reshape(D_MODEL, N_DEV, SHARD_IN).transpose(1, 0, 2).astype(BF16)
    slab_out = lambda g: g["w_out"].reshape(N_DEV, SHARD_OUT, D_MODEL).astype(BF16)
    slab_conv = lambda g: jnp.stack(
        [g["conv_w"][:, k * CONV_CH:(k + 1) * CONV_CH].reshape(CONV_W, N_DEV, SHARD_CONV).transpose(1, 0, 2)
         for k in range(2)], axis=1).reshape(N_DEV, 2 * CONV_W, SHARD_CONV)

    g_in0, g_conv0 = _comm_call("gather_layer0", "gather", [w_in_b[0], conv_shard[0]])
    layer0 = assemble(0, g_in0, lambda got: full_out(got[3]), g_conv0)
    loss_local, dx, grads, recv_scan, recv_dx = _local_step(
        xb, tgt, layer0, lambda got: assemble(1, got[0], full_out(got[1]), got[2]), length,
        fwd_comm=("gather", [w_in_b[1], w_out_b[1], conv_shard[1], w_out_b[0]]),
        comm_scan=lambda g1, g0: ("scatter", [slab_in(g1), slab_out(g1), slab_conv(g1), slab_out(g0)]),
        comm_dx=lambda g0: ("scatter", [slab_in(g0), slab_conv(g0)]))
    grad_x = dx[None]
    loss = lax.psum(loss_local, ("x", "y", "c"))

    small_g = dict(
        pre_norm=jnp.concatenate([grads[l]["pre_norm"] for l in range(DEPTH)], axis=0),
        post_norm=jnp.concatenate([grads[l]["post_norm"] for l in range(DEPTH)], axis=0),
        gdn_A_log=jnp.stack([_from_lane_row(grads[l]["gdn_A_log"], "ga", GDN_HEADS) for l in range(DEPTH)]),
        gdn_dt_bias=jnp.stack([_from_lane_row(grads[l]["gdn_dt_bias"], "ga", GDN_HEADS) for l in range(DEPTH)]),
        gdn_norm=jnp.concatenate([grads[l]["gdn_norm"] for l in range(DEPTH)], axis=0),
        ssd_conv_b=jnp.concatenate([grads[l]["conv_b"][:, CONV_CH:] for l in range(DEPTH)], axis=0),
        ssd_A_log=jnp.stack([_from_lane_row(grads[l]["ssd_A_log"], "sdt", SSD_HEADS) for l in range(DEPTH)]),
        ssd_dt_bias=jnp.stack([_from_lane_row(grads[l]["ssd_dt_bias"], "sdt", SSD_HEADS) for l in range(DEPTH)]),
        ssd_D=jnp.stack([_from_lane_row(grads[l]["ssd_D"], "sdt", SSD_HEADS) for l in range(DEPTH)]),
        ssd_norm=jnp.concatenate([grads[l]["ssd_norm"].reshape(1, -1) for l in range(DEPTH)], axis=0),
        ret_norm=jnp.concatenate([grads[l]["ret_norm"] for l in range(DEPTH)], axis=0))
    gs = _pack_small(small_g)
    gs8 = jnp.broadcast_to(gs[None], (N_DEV,) + gs.shape)
    (r_small,) = _comm_call("exchange_small", "scatter", [gs8])
    recv = [[recv_dx[0], recv_scan[3], recv_dx[1]], recv_scan[:3]]

    conv_w = lambda g_, s_, l: jnp.stack([g_[l], s_[l]], axis=0).reshape(2 * CONV_W, SHARD_CONV)
    o_in, o_out, o_conv = None, None, []
    for l in range(DEPTH):
        o_in = _adam(f"adam_w_in{l}", recv[l][0], w_in, m_w_in, v_w_in, 128, l, o_in)
        o_out = _adam(f"adam_w_out{l}", recv[l][1], w_out, m_w_out, v_w_out, 128, l, o_out)
        o_conv.append(_adam(f"adam_conv{l}", recv[l][2], conv_w(gdn_conv, ssd_conv, l),
                            conv_w(m_gdn_conv, m_ssd_conv, l), conv_w(v_gdn_conv, v_ssd_conv, l), 2 * CONV_W))
    ps_w, ps_m, ps_v = _pack_small(small_w), _pack_small(small_m), _pack_small(small_v)
    o_small = _adam("adam_small", r_small, ps_w, ps_m, ps_v, ps_w.shape[0])

    names = ["pre_norm", "post_norm", "w_in", "gdn_conv", "gdn_A_log", "gdn_dt_bias", "gdn_norm", "ssd_conv",
             "ssd_conv_b", "ssd_A_log", "ssd_dt_bias", "ssd_D", "ssd_norm", "ret_norm", "w_out"]
    outs = []
    for kind in range(4):
        d = _unpack_small(o_small[kind], small_w)
        cv = jnp.stack([o_conv[l][kind].reshape(2, CONV_W, SHARD_CONV) for l in range(DEPTH)], axis=1)
        d["w_in"] = o_in[kind]
        d["w_out"] = o_out[kind]
        d["gdn_conv"] = cv[0]
        d["ssd_conv"] = cv[1]
        outs.extend(d[n] for n in names)
    return (loss, grad_x, *outs)
```

```python
import functools
import math

import jax
import jax.numpy as jnp
from jax import lax
from jax.experimental import pallas as pl
from jax.experimental.pallas import tpu as pltpu

F32 = jnp.float32
BF16 = jnp.bfloat16

D_MODEL = 1024
DEPTH = 2
CHUNK = 64
CONV_W = 4
EPS = 1e-6
N_DEV = 8

GDN_HEADS = 4
GDN_DK = 128
SSD_HEADS = 16
SSD_P = 64
SSD_N = 128
SSD_GROUPS = 2
SSD_PAIRS = SSD_HEADS // 2
PAIRS_PER_GROUP = SSD_PAIRS // SSD_GROUPS
RET_HEADS = 4
RET_DK = 128
ROPE_BASE = 10000.0
MIX_W = 2048
N_IN = 6680
SHARD_IN = N_IN // N_DEV
SHARD_OUT = MIX_W // N_DEV
CONV_CH = 1536
SHARD_CONV = CONV_CH // N_DEV

ADAM_LR = 0.001
ADAM_B1 = 0.9
ADAM_B2 = 0.999
ADAM_EPS = 1e-08
ADAM_WD = 0.01
ADAM_STEP = 10

LANE = 128
SUBLANE = 8
VMEM_LIMIT = 56 * 1024 * 1024

_ORIG = dict(gq=(0, 512), gk=(512, 512), gv=(1024, 512), gz=(1536, 512), gb=(2048, 4), ga=(2052, 4),
             sx=(2056, 1024), sB=(3080, 256), sC=(3336, 256), sz=(3592, 1024), sdt=(4616, 16),
             rq=(4632, 512), rk=(5144, 512), rv=(5656, 512), rg=(6168, 512))
_ORDER = ["rq", "rk", "rv", "rg", "gq", "gk", "gv", "sx", "sB", "sC", "gz", "sz", "gb", "ga", "sdt"]
_MINE = {}
_off = 0
for _n in _ORDER:
    _MINE[_n] = _off
    _off += _ORIG[_n][1]
N_USED = _off
N_PAD = 7168
CONV_ALL = 2 * CONV_CH
SMALL_OFF = _MINE["gb"]
CONV_OFF = _MINE["gq"]
XC = lambda name: _MINE[name] - CONV_OFF


def _cparams(sem, vmem=None):
    return pltpu.CompilerParams(dimension_semantics=sem, vmem_limit_bytes=vmem)


def _split_bf16(a):
    hi = a.astype(BF16)
    return hi, (a - hi.astype(F32)).astype(BF16)


def _make_mm():
    def raw(a, b, ca, cb):
        return lax.dot_general(a.astype(BF16), b.astype(BF16), (((ca,), (cb,)), ((), ())),
                               preferred_element_type=F32)

    @jax.custom_vjp
    def nn(a, b):
        return raw(a, b, 1, 0)

    @jax.custom_vjp
    def nt(a, b):
        return raw(a, b, 1, 1)

    @jax.custom_vjp
    def tn(a, b):
        return raw(a, b, 0, 0)

    nn.defvjp(lambda a, b: (raw(a, b, 1, 0), (a, b)), lambda r, g: (nt(g, r[1]), tn(r[0], g)))
    nt.defvjp(lambda a, b: (raw(a, b, 1, 1), (a, b)), lambda r, g: (nn(g, r[1]), tn(g, r[0])))
    tn.defvjp(lambda a, b: (raw(a, b, 0, 0), (a, b)), lambda r, g: (nt(r[1], g), nn(r[0], g)))
    return nn, nt, tn


_nn, _nt, _tn = _make_mm()


@jax.custom_vjp
def _swap_halves(t):
    return pltpu.roll(t, LANE // 2, 1)


_swap_halves.defvjp(lambda t: (pltpu.roll(t, LANE // 2, 1), None),
                    lambda _, g: (pltpu.roll(g, LANE // 2, 1),))


@jax.custom_vjp
def _split_rows(x):
    return tuple(x[i * CHUNK:(i + 1) * CHUNK] for i in range(x.shape[0] // CHUNK))


_split_rows.defvjp(lambda x: (tuple(x[i * CHUNK:(i + 1) * CHUNK] for i in range(x.shape[0] // CHUNK)), None),
                   lambda _, gs: (jnp.concatenate(gs, axis=0),))


def _dot3(a, b, ca, cb):
    dot = lambda x, y: lax.dot_general(x, y, (((ca,), (cb,)), ((), ())), preferred_element_type=F32)
    return dot(a[0], b[0]) + (dot(a[0], b[1]) + dot(a[1], b[0]))


@jax.custom_vjp
def _tri_inv(mats):
    return _tri_inv_impl(mats)


def _tri_inv_impl(mats):
    ii = lax.broadcasted_iota(jnp.int32, mats[0].shape, 0)
    jj = lax.broadcasted_iota(jnp.int32, mats[0].shape, 1)
    eye = jnp.where(ii == jj, 1.0, 0.0).astype(F32)
    ts = [eye - a for a in mats]
    ps = [_split_bf16(-a) for a in mats]
    for _ in range(int(math.log2(CHUNK)) - 1):
        ps = [_split_bf16(_dot3(p, p, 1, 0)) for p in ps]
        ts = [t + _dot3(_split_bf16(t), p, 1, 0) for t, p in zip(ts, ps)]
    return ts


def _tri_inv_bwd(ts, gs):
    tsp = [_split_bf16(t) for t in ts]
    xs = [_dot3(t, _split_bf16(g), 0, 0) for t, g in zip(tsp, gs)]
    return ([-_dot3(_split_bf16(x), t, 1, 1) for x, t in zip(xs, tsp)],)


_tri_inv.defvjp(lambda mats: (lambda ts: (ts, ts))(_tri_inv_impl(mats)), _tri_inv_bwd)


@jax.custom_vjp
def _tri_inv_saved(mats, ts):
    return ts


_tri_inv_saved.defvjp(lambda mats, ts: (ts, ts),
                      lambda ts, gs: (_tri_inv_bwd(ts, gs)[0], [jnp.zeros_like(t) for t in ts]))


def _silu(x):
    return x * jax.nn.sigmoid(x)


@jax.custom_vjp
def _softplus(x):
    return jnp.maximum(x, 0.0) + jnp.log1p(jnp.exp(-jnp.abs(x)))


_softplus.defvjp(lambda x: (jnp.maximum(x, 0.0) + jnp.log1p(jnp.exp(-jnp.abs(x))), x),
                 lambda x, g: (g * jax.nn.sigmoid(x),))


def _rms(x, w):
    return x * lax.rsqrt(jnp.mean(x * x, axis=-1, keepdims=True) + EPS) * w


def _chunk_masks(n):
    ii = lax.broadcasted_iota(jnp.int32, (n, n), 0)
    jj = lax.broadcasted_iota(jnp.int32, (n, n), 1)
    return ii >= jj, ii > jj, ii == jj, ii <= jj


def _cumsum_col(g, causal, eye, upper):
    g_row = jnp.sum(jnp.where(eye, g, 0.0), axis=0, keepdims=True)
    col = jnp.sum(jnp.where(causal, g_row, 0.0), axis=1, keepdims=True)
    row = jnp.sum(jnp.where(upper, g, 0.0), axis=0, keepdims=True)
    return col, row


def _lane_col(block, lane):
    pick = lax.broadcasted_iota(jnp.int32, (1, block.shape[1]), 1) == lane
    return jnp.sum(jnp.where(pick, block, 0.0), axis=1, keepdims=True)


def _gdn_chunk(q, k, v, gz, sm, alog_row, dtb_row, nw, s, t_saved=None):
    subs, nh = range(len(q)), len(q[0])
    c = q[0][0].shape[0]
    causal, _, eye, upper = _chunk_masks(c)
    stack = lambda xs: jnp.concatenate(xs, axis=0)
    per = lambda f: [[f(j, h) for h in range(nh)] for j in subs]
    qn = per(lambda j, h: q[j][h] * lax.rsqrt(jnp.sum(q[j][h] * q[j][h], axis=-1, keepdims=True) + EPS)
             * (GDN_DK ** -0.5))
    kn = per(lambda j, h: k[j][h] * lax.rsqrt(jnp.sum(k[j][h] * k[j][h], axis=-1, keepdims=True) + EPS))
    beta_blk = [jax.nn.sigmoid(sm[j]) for j in subs]
    g_blk = [-jnp.exp(alog_row) * _softplus(sm[j] + dtb_row) for j in subs]
    beta = per(lambda j, h: _lane_col(beta_blk[j], _MINE["gb"] - SMALL_OFF + h))
    g = per(lambda j, h: _lane_col(g_blk[j], _MINE["ga"] - SMALL_OFF + h))
    gcum = per(lambda j, h: _cumsum_col(g[j][h], causal, eye, upper)[0])
    eg = per(lambda j, h: jnp.exp(gcum[j][h]))
    glast = per(lambda j, h: jnp.sum(g[j][h], axis=0, keepdims=True))
    kb = per(lambda j, h: kn[j][h] * beta[j][h])
    n = nh * c
    ii = lax.broadcasted_iota(jnp.int32, (n, n), 0)
    jj = lax.broadcasted_iota(jnp.int32, (n, n), 1)
    sh = int(math.log2(c))
    same = lax.shift_right_logical(ii, sh) == lax.shift_right_logical(jj, sh)
    causal_bd = jnp.logical_and(same, ii >= jj)
    strict_bd = jnp.logical_and(same, ii > jj)
    gcum_all = [stack(gcum[j]) for j in subs]
    gcum_row = [jnp.sum(jnp.where(ii == jj, gcum_all[j], 0.0), axis=0, keepdims=True) for j in subs]
    decay = [jnp.where(causal_bd, jnp.exp(jnp.where(causal_bd, gcum_all[j] - gcum_row[j], 0.0)), 0.0) for j in subs]
    kn_all = [stack(kn[j]) for j in subs]
    a_low = [jnp.where(strict_bd, _nt(stack(kb[j]), kn_all[j]) * decay[j], 0.0) for j in subs]
    t = _tri_inv(a_low) if t_saved is None else _tri_inv_saved(a_low, t_saved)
    u = [_nn(t[j], stack([v[j][h] * beta[j][h] for h in range(nh)])) for j in subs]
    w = [_split_rows(_nn(t[j], stack([kb[j][h] * eg[j][h] for h in range(nh)]))) for j in subs]
    attn = [_nt(stack(qn[j]), kn_all[j]) * decay[j] for j in subs]
    kdec = per(lambda j, h: kn[j][h] * jnp.exp(glast[j][h] - gcum[j][h]))
    on = []
    for j in subs:
        v_new_all = u[j] - stack([_nn(w[j][h], s[h]) for h in range(nh)])
        o = _split_rows(stack([_nn(qn[j][h] * eg[j][h], s[h]) for h in range(nh)]) + _nn(attn[j], v_new_all))
        v_new = _split_rows(v_new_all)
        s = [s[h] * jnp.exp(glast[j][h]) + _tn(kdec[j][h], v_new[h]) for h in range(nh)]
        on.append([_rms(o[h], nw) * _silu(gz[j][h]) for h in range(nh)])
    return on, s, t


def _ssd_chunk(lanes, x, bm, cm, sm, alog_row, dtb_row, d_row, hs):
    subs, pairs = range(len(x)), range(len(x[0]))
    c = x[0][0].shape[0]
    lane_i = lax.broadcasted_iota(jnp.int32, (c, LANE), 1)
    lane_lo = lane_i < SSD_P
    lane_lo1 = lax.broadcasted_iota(jnp.int32, (1, LANE), 1) < SSD_P
    row_i = lax.broadcasted_iota(jnp.int32, (c, LANE), 0)
    causal2 = row_i >= jnp.bitwise_and(lane_i, c - 1)
    diag2 = row_i == jnp.bitwise_and(lane_i, c - 1)
    to_row = lambda blk: jnp.sum(jnp.where(diag2, blk, 0.0), axis=0, keepdims=True)
    per = lambda f: [[f(j, p) for p in pairs] for j in subs]
    both = lambda blk, p: [_lane_col(blk, lanes[p][h]) for h in range(2)]
    dt_blk = [_softplus(sm[j] + dtb_row) for j in subs]
    a_blk = [dt_blk[j] * -jnp.exp(alog_row) for j in subs]
    alast_blk = [jnp.sum(a_blk[j], axis=0, keepdims=True) for j in subs]
    dt = per(lambda j, p: both(dt_blk[j], p))
    nexp = [both(-jnp.exp(alog_row), p) for p in pairs]
    alast = per(lambda j, p: both(alast_blk[j], p))
    dp = [both(d_row, p) for p in pairs]
    dt_lane = per(lambda j, p: jnp.where(lane_lo, dt[j][p][0], dt[j][p][1]))
    a_lane = per(lambda j, p: dt_lane[j][p] * jnp.where(lane_lo1, nexp[p][0], nexp[p][1]))
    a_row = per(lambda j, p: to_row(a_lane[j][p]))
    part = per(lambda j, p: jnp.where(causal2, a_row[j][p], 0.0))
    acum = per(lambda j, p: [jnp.sum(jnp.where(lane_lo, part[j][p], 0.0), axis=1, keepdims=True),
                             jnp.sum(jnp.where(lane_lo, 0.0, part[j][p]), axis=1, keepdims=True)])
    acum_col = per(lambda j, p: jnp.where(lane_lo, acum[j][p][0], acum[j][p][1]))
    acum_row = per(lambda j, p: to_row(acum_col[j][p]))
    lmat = per(lambda j, p: jnp.where(causal2, jnp.exp(jnp.where(causal2, acum_col[j][p] - acum_row[j][p], 0.0)), 0.0))
    cb = [_nt(cm[j], jnp.concatenate([bm[j], bm[j]], axis=0)) for j in subs]
    xdt = per(lambda j, p: x[j][p] * dt_lane[j][p])
    xdt_rows = per(lambda j, p: jnp.concatenate([jnp.where(lane_lo, xdt[j][p], 0.0),
                                                 jnp.where(lane_lo, 0.0, xdt[j][p])], axis=0))
    intra = per(lambda j, p: _nn(cb[j] * lmat[j][p], xdt_rows[j][p]))
    skip = per(lambda j, p: x[j][p] * jnp.where(lane_lo1, dp[p][0], dp[p][1]))
    eacc = per(lambda j, p: jnp.exp(acum_col[j][p]))
    alast_row = per(lambda j, p: jnp.where(lane_lo1, alast[j][p][0], alast[j][p][1]))
    wdec = per(lambda j, p: jnp.exp(alast_row[j][p] - acum_col[j][p]))
    scale = per(lambda j, p: jnp.exp(alast_row[j][p]))
    upd = per(lambda j, p: _tn(bm[j], xdt[j][p] * wdec[j][p]))
    y = []
    for j in subs:
        y.append([skip[j][p] + intra[j][p] + _nn(cm[j], hs[p]) * eacc[j][p] for p in pairs])
        hs = [hs[p] * scale[j][p] + upd[j][p] for p in pairs]
    return y, hs


def _ret_chunk(rq, rk, rv, rg, cos2, sin2, dmat, qdec, kdec, cdec, nw, r):
    subs, hs = range(len(rq)), range(len(rq[0]))
    per = lambda f: [[f(j, h) for h in hs] for j in subs]
    q = per(lambda j, h: rq[j][h] * cos2[j] + _swap_halves(rq[j][h]) * sin2[j])
    k = per(lambda j, h: (rk[j][h] * cos2[j] + _swap_halves(rk[j][h]) * sin2[j]) * (RET_DK ** -0.5))
    s = per(lambda j, h: _nt(q[j][h], k[j][h]) * dmat[h])
    intra = per(lambda j, h: _nn(s[j][h], rv[j][h]))
    upd = per(lambda j, h: _tn(k[j][h] * kdec[h], rv[j][h]))
    gate = per(lambda j, h: _silu(rg[j][h]))
    on = []
    for j in subs:
        on.append([_rms(intra[j][h] + _nn(q[j][h], r[h]) * qdec[h], nw) * gate[j][h] for h in hs])
        r = [r[h] * cdec[h] + upd[j][h] for h in hs]
    return on, r


MM_TILE = 1024


def _tile(n, pref=MM_TILE):
    return pref if n % pref == 0 else n


def _matmul(name, a, b, mode, m, n, k, tk=None, comm=None, out_dtype=F32):
    tm, tn = _tile(m), _tile(n)
    tk = _tile(k) if tk is None else tk
    nk = k // tk
    ca, cb = {"nn": (1, 0), "nt": (1, 1), "tn": (0, 0)}[mode]
    grid = (m // tm, n // tn, nk)
    n_comm = len(comm[1]) if comm else 0
    use_acc = out_dtype != F32 and nk > 1

    def at_step(which):
        return functools.reduce(jnp.logical_and, [pl.program_id(d) == (0 if which == 0 else grid[d] - 1)
                                                  for d in range(3)])

    def body(*refs):
        a_ref, b_ref = refs[:2]
        c_in = refs[2:2 + n_comm]
        o_ref = refs[2 + n_comm]
        c_out = refs[3 + n_comm:3 + 2 * n_comm]
        acc_ref = refs[3 + 2 * n_comm] if use_acc else o_ref
        sems = refs[3 + 2 * n_comm + use_acc:]
        if comm:
            @pl.when(at_step(0))
            def _():
                _comm_plan(comm[0], c_in, c_out, *sems)[0]()

        part = lax.dot_general(a_ref[...].astype(BF16), b_ref[...].astype(BF16),
                               (((ca,), (cb,)), ((), ())), preferred_element_type=F32)
        if nk == 1:
            o_ref[...] = part.astype(o_ref.dtype)
        else:
            kk = pl.program_id(2)

            @pl.when(kk == 0)
            def _():
                acc_ref[...] = part

            @pl.when(kk > 0)
            def _():
                acc_ref[...] += part

            if use_acc:
                @pl.when(kk == nk - 1)
                def _():
                    o_ref[...] = acc_ref[...].astype(o_ref.dtype)

        if comm:
            @pl.when(at_step(1))
            def _():
                _comm_plan(comm[0], c_in, c_out, *sems)[1]()

    a_spec = pl.BlockSpec((tk, tm), lambda i, j, kk: (kk, i)) if mode == "tn" else pl.BlockSpec((tm, tk), lambda i, j, kk: (i, kk))
    b_spec = pl.BlockSpec((tn, tk), lambda i, j, kk: (j, kk)) if mode == "nt" else pl.BlockSpec((tk, tn), lambda i, j, kk: (kk, j))
    hbm = pl.BlockSpec(memory_space=pl.ANY)
    res = pl.pallas_call(
        body, name=name, grid=grid,
        in_specs=[a_spec, b_spec] + [hbm] * n_comm,
        out_specs=[pl.BlockSpec((tm, tn), lambda i, j, kk: (i, j))] + [hbm] * n_comm,
        out_shape=[jax.ShapeDtypeStruct((m, n), out_dtype)] + (_comm_out_shapes(*comm) if comm else []),
        scratch_shapes=([pltpu.VMEM((tm, tn), F32)] if use_acc else []) + (_comm_scratch(n_comm) if comm else []),
        compiler_params=_cparams(("arbitrary",) * 3 if comm else ("parallel", "parallel", "arbitrary"), VMEM_LIMIT),
    )(a, b, *(comm[1] if comm else []))
    return res if comm else res[0]


def _rowmap(name, fn, length, tm, rows, params, out_ws=(), acc_ws=(), out_dtypes=None, cts=None, d_dtypes=None,
            place=None):
    nt_ = length // tm
    n_r, n_p = len(rows), len(params)
    n_in = n_r + n_p
    rspec = lambda bw, cbk: pl.BlockSpec((tm, bw), lambda i: (i, cbk))
    pspec = lambda w: pl.BlockSpec((1, w), lambda i: (0, 0))
    in_arrays = [r[0] for r in rows] + list(params)
    in_specs = [rspec(r[1], r[2]) for r in rows] + [pspec(p.shape[1]) for p in params]
    place = place or {}

    def load(refs):
        return [r[...].astype(F32) for r in refs]

    def placed(widths, dtypes, n_before):
        specs, shapes, extra, alias = [], [], [], {}
        for k, (w, dt) in enumerate(zip(widths, dtypes)):
            arr, total, cbk = place.get(k, (None, w, 0))
            specs.append(rspec(w, cbk))
            shapes.append(jax.ShapeDtypeStruct((length, total), dt))
            if arr is not None:
                alias[n_before + len(extra)] = k
                extra.append(arr)
        return specs, shapes, extra, alias

    def accumulate(a_refs, vals):
        first = pl.program_id(0) == 0
        for a_ref, v in zip(a_refs, vals):
            @pl.when(first)
            def _(a_ref=a_ref):
                a_ref[...] = jnp.zeros_like(a_ref)
            a_ref[...] += v

    hbm = pl.BlockSpec(memory_space=pl.ANY)
    if cts is None:
        n_o = len(out_ws)
        o_specs, o_shapes, extra, alias = placed(out_ws, out_dtypes or [F32] * n_o, n_in)

        def body(*refs):
            outs, accs = fn(*load(refs[:n_in]))
            o_refs = refs[n_in + len(extra):]
            for o_ref, o in zip(o_refs[:n_o], outs):
                o_ref[...] = o.astype(o_ref.dtype)
            accumulate(o_refs[n_o:], accs)

        res = pl.pallas_call(
            body, name=name, grid=(nt_,), in_specs=in_specs + [hbm] * len(extra),
            out_specs=o_specs + [pspec(w) for w in acc_ws],
            out_shape=o_shapes + [jax.ShapeDtypeStruct((1, w), F32) for w in acc_ws],
            input_output_aliases=alias, compiler_params=_cparams(("arbitrary",), VMEM_LIMIT),
        )(*in_arrays, *extra)
        return res[:n_o], res[n_o:]

    n_c = len(cts)
    o_specs, o_shapes, extra, alias = placed([r[1] for r in rows], d_dtypes or [F32] * n_r, n_in + n_c)

    def body(*refs):
        ins = load(refs[:n_in])
        _, vjp = jax.vjp(lambda rs, ps: fn(*rs, *ps)[0], ins[:n_r], ins[n_r:])
        d_rows, d_params = vjp(tuple(load(refs[n_in:n_in + n_c])))
        o_refs = refs[n_in + n_c + len(extra):]
        for o_ref, d in zip(o_refs[:n_r], d_rows):
            o_ref[...] = d.astype(o_ref.dtype)
        accumulate(o_refs[n_r:], d_params)

    res = pl.pallas_call(
        body, name=name, grid=(nt_,),
        in_specs=in_specs + [rspec(c[1], c[2]) for c in cts] + [hbm] * len(extra),
        out_specs=o_specs + [pspec(p.shape[1]) for p in params],
        out_shape=o_shapes + [jax.ShapeDtypeStruct((1, p.shape[1]), F32) for p in params],
        input_output_aliases=alias, compiler_params=_cparams(("arbitrary",), VMEM_LIMIT),
    )(*in_arrays, *[c[0] for c in cts], *extra)
    return res[:n_r], res[n_r:]


def _conv_shift(ext, k, tm, forward):
    s = CONV_W - 1 - k
    if forward:
        rolled = ext if s == 0 else pltpu.roll(ext, s, 0)
        return rolled[SUBLANE:, :]
    rolled = ext if s == 0 else pltpu.roll(ext, tm + SUBLANE - s, 0)
    return rolled[:tm, :]


def _conv_pre(x_ref, halo_ref, w_ref, b_ref, first, tm):
    halo = jnp.where(first, 0.0, halo_ref[...])
    ext = jnp.concatenate([halo, x_ref[...]], axis=0)
    w = w_ref[...]
    pre = b_ref[...] + jnp.zeros_like(x_ref[...])
    taps = []
    for k in range(CONV_W):
        tap = _conv_shift(ext, k, tm, True)
        taps.append(tap)
        pre = pre + tap * w[k:k + 1, :]
    return pre, taps


def _conv_fwd(proj, w, b, length, tm, tc):
    hb = tm // SUBLANE
    j0 = CONV_OFF // tc

    def body(x_ref, halo_ref, w_ref, b_ref, o_ref):
        pre, _ = _conv_pre(x_ref, halo_ref, w_ref, b_ref, pl.program_id(0) == 0, tm)
        o_ref[...] = _silu(pre)

    return pl.pallas_call(
        body, name="conv_fwd", grid=(length // tm, CONV_ALL // tc),
        in_specs=[pl.BlockSpec((tm, tc), lambda i, j: (i, j0 + j)),
                  pl.BlockSpec((SUBLANE, tc), lambda i, j: (jnp.maximum(i * hb - 1, 0), j0 + j)),
                  pl.BlockSpec((CONV_W, tc), lambda i, j: (0, j)),
                  pl.BlockSpec((1, tc), lambda i, j: (0, j))],
        out_specs=pl.BlockSpec((tm, tc), lambda i, j: (i, j)),
        out_shape=jax.ShapeDtypeStruct((length, CONV_ALL), F32),
        compiler_params=_cparams(("parallel", "parallel"), VMEM_LIMIT),
    )(proj, proj, w, b)


def _conv_bwd_pre(proj, w, b, dxc, length, tm, tc):
    hb = tm // SUBLANE
    j0 = CONV_OFF // tc

    def body(x_ref, halo_ref, w_ref, b_ref, dy_ref, dpre_ref, dw_ref, db_ref):
        i = pl.program_id(1)
        pre, taps = _conv_pre(x_ref, halo_ref, w_ref, b_ref, i == 0, tm)
        sg = jax.nn.sigmoid(pre)
        dpre = dy_ref[...] * (sg * (1.0 + pre * (1.0 - sg)))
        dpre_ref[...] = dpre

        @pl.when(i == 0)
        def _():
            dw_ref[...] = jnp.zeros_like(dw_ref)
            db_ref[...] = jnp.zeros_like(db_ref)

        for k in range(CONV_W):
            dw_ref[k:k + 1, :] += jnp.sum(dpre * taps[k], axis=0, keepdims=True)
        db_ref[...] += jnp.sum(dpre, axis=0, keepdims=True)

    return pl.pallas_call(
        body, name="conv_bwd_pre", grid=(CONV_ALL // tc, length // tm),
        in_specs=[pl.BlockSpec((tm, tc), lambda j, i: (i, j0 + j)),
                  pl.BlockSpec((SUBLANE, tc), lambda j, i: (jnp.maximum(i * hb - 1, 0), j0 + j)),
                  pl.BlockSpec((CONV_W, tc), lambda j, i: (0, j)),
                  pl.BlockSpec((1, tc), lambda j, i: (0, j)),
                  pl.BlockSpec((tm, tc), lambda j, i: (i, j))],
        out_specs=[pl.BlockSpec((tm, tc), lambda j, i: (i, j)),
                   pl.BlockSpec((CONV_W, tc), lambda j, i: (0, j)),
                   pl.BlockSpec((1, tc), lambda j, i: (0, j))],
        out_shape=[jax.ShapeDtypeStruct((length, CONV_ALL), F32),
                   jax.ShapeDtypeStruct((CONV_W, CONV_ALL), F32),
                   jax.ShapeDtypeStruct((1, CONV_ALL), F32)],
        compiler_params=_cparams(("parallel", "arbitrary"), VMEM_LIMIT),
    )(proj, proj, w, b, dxc)


def _conv_bwd_x(dpre, w, dproj, length, tm, tc):
    hb = tm // SUBLANE
    n_t = length // tm
    last_blk = length // SUBLANE - 1
    j0 = CONV_OFF // tc

    def body(d_ref, halo_ref, w_ref, _, o_ref):
        halo = jnp.where(pl.program_id(0) == n_t - 1, 0.0, halo_ref[...])
        ext = jnp.concatenate([d_ref[...], halo], axis=0)
        w = w_ref[...]
        acc = jnp.zeros_like(d_ref[...])
        for k in range(CONV_W):
            acc = acc + _conv_shift(ext, k, tm, False) * w[k:k + 1, :]
        o_ref[...] = acc.astype(o_ref.dtype)

    return pl.pallas_call(
        body, name="conv_bwd_x", grid=(n_t, CONV_ALL // tc),
        in_specs=[pl.BlockSpec((tm, tc), lambda i, j: (i, j)),
                  pl.BlockSpec((SUBLANE, tc), lambda i, j: (jnp.minimum((i + 1) * hb, last_blk), j)),
                  pl.BlockSpec((CONV_W, tc), lambda i, j: (0, j)),
                  pl.BlockSpec(memory_space=pl.ANY)],
        out_specs=pl.BlockSpec((tm, tc), lambda i, j: (i, j0 + j)),
        out_shape=jax.ShapeDtypeStruct(dproj.shape, dproj.dtype), input_output_aliases={3: 0},
        compiler_params=_cparams(("parallel", "parallel"), VMEM_LIMIT),
    )(dpre, dpre, w, dproj)


HEAD_SCAN_CHUNKS_PER_STEP = 4
assert GDN_HEADS == RET_HEADS


def _scan_steps(length):
    nc = length // CHUNK
    gs = HEAD_SCAN_CHUNKS_PER_STEP if nc % HEAD_SCAN_CHUNKS_PER_STEP == 0 else 1
    return gs, nc // gs, gs * CHUNK


def _head_tiles(ref, gs):
    return [[ref[j * CHUNK:(j + 1) * CHUNK, h * LANE:(h + 1) * LANE] for h in range(GDN_HEADS)] for j in range(gs)]


def _chunk_rows(ref, gs):
    return [ref[j * CHUNK:(j + 1) * CHUNK, :] for j in range(gs)]


def _gdn_fwd(xc, proj, alog, dtb, nw, length, comm=None):
    gs, steps, rows = _scan_steps(length)
    h_ = GDN_HEADS
    hw = h_ * LANE
    tn_ = h_ * CHUNK
    n_comm = len(comm[1]) if comm else 0

    def body(*refs):
        q_ref, k_ref, v_ref, gz_ref, sm_ref, al_ref, dt_ref, nw_ref = refs[:8]
        c_in = refs[8:8 + n_comm]
        on_ref, st_ref, t_ref = refs[8 + n_comm:11 + n_comm]
        c_out = refs[11 + n_comm:11 + 2 * n_comm]
        s_ref = refs[11 + 2 * n_comm]
        sems = refs[12 + 2 * n_comm:]

        @pl.when(pl.program_id(0) == 0)
        def _():
            s_ref[...] = jnp.zeros_like(s_ref)
            if comm:
                _comm_plan(comm[0], c_in, c_out, *sems)[0]()

        st_ref[...] = s_ref[...]
        lead = lambda ref: [ref[h] for h in range(h_)]
        on, s_new, t = _gdn_chunk(_head_tiles(q_ref, gs), _head_tiles(k_ref, gs), _head_tiles(v_ref, gs),
                                  _head_tiles(gz_ref, gs), _chunk_rows(sm_ref, gs), al_ref[...], dt_ref[...],
                                  nw_ref[...], lead(s_ref))
        for j in range(gs):
            t_ref[j] = t[j]
        for h in range(h_):
            for j in range(gs):
                on_ref[j * CHUNK:(j + 1) * CHUNK, h * LANE:(h + 1) * LANE] = on[j][h].astype(on_ref.dtype)
            s_ref[h] = s_new[h]

        if comm:
            @pl.when(pl.program_id(0) == steps - 1)
            def _():
                _comm_plan(comm[0], c_in, c_out, *sems)[1]()

    blk = lambda col: pl.BlockSpec((rows, hw), lambda c: (c, col // hw))
    one = pl.BlockSpec((1, LANE), lambda c: (0, 0))
    hbm = pl.BlockSpec(memory_space=pl.ANY)
    return pl.pallas_call(
        body, name="gdn_fwd_" + comm[0] if comm else "gdn_fwd", grid=(steps,),
        in_specs=[blk(XC("gq")), blk(XC("gk")), blk(XC("gv")), blk(_MINE["gz"]),
                  pl.BlockSpec((rows, LANE), lambda c: (c, SMALL_OFF // LANE)), one, one,
                  pl.BlockSpec((1, LANE), lambda c: (0, 0))] + [hbm] * n_comm,
        out_specs=[pl.BlockSpec((rows, hw), lambda c: (c, 0)),
                   pl.BlockSpec((h_, None, LANE, LANE), lambda c: (0, c, 0, 0)),
                   pl.BlockSpec((gs, tn_, tn_), lambda c: (c, 0, 0))] + [hbm] * n_comm,
        out_shape=[jax.ShapeDtypeStruct((length, MIX_W), BF16),
                   jax.ShapeDtypeStruct((h_, steps, LANE, LANE), F32),
                   jax.ShapeDtypeStruct((length // CHUNK, tn_, tn_), F32)]
                  + (_comm_out_shapes(*comm) if comm else []),
        scratch_shapes=[pltpu.VMEM((h_, LANE, LANE), F32)] + (_comm_scratch(n_comm) if comm else []),
        compiler_params=_cparams(("arbitrary",), VMEM_LIMIT),
    )(xc, xc, xc, proj, proj, alog, dtb, nw, *(comm[1] if comm else []))


def _gdn_bwd(xc, proj, alog, dtb, nw, st, tinv, dmix, length, comm=None):
    gs, steps, rows = _scan_steps(length)
    h_ = GDN_HEADS
    hw = h_ * LANE
    tn_ = h_ * CHUNK
    n_comm = len(comm[1]) if comm else 0
    n_in, n_out = 11, 6

    def body(*refs):
        q_ref, k_ref, v_ref, gz_ref, sm_ref, al_ref, dt_ref, nw_ref, st_ref, t_ref, do_ref = refs[:n_in]
        c_in = refs[n_in:n_in + n_comm]
        o0 = n_in + n_comm
        dqkv_ref, dgz_ref, dsm_ref, dal_ref, ddt_ref, dnw_ref = refs[o0:o0 + n_out]
        c_out = refs[o0 + n_out:o0 + n_out + n_comm]
        ds_ref = refs[o0 + n_out + n_comm]
        sems = refs[o0 + n_out + n_comm + 1:]

        @pl.when(pl.program_id(0) == 0)
        def _():
            ds_ref[...] = jnp.zeros_like(ds_ref)
            dal_ref[...] = jnp.zeros_like(dal_ref)
            ddt_ref[...] = jnp.zeros_like(ddt_ref)
            dnw_ref[...] = jnp.zeros_like(dnw_ref)
            if comm:
                _comm_plan(comm[0], c_in, c_out, *sems)[0]()

        lead = lambda ref: [ref[h] for h in range(h_)]
        ins = (_head_tiles(q_ref, gs), _head_tiles(k_ref, gs), _head_tiles(v_ref, gs), _head_tiles(gz_ref, gs),
               _chunk_rows(sm_ref, gs), al_ref[...], dt_ref[...], nw_ref[...], lead(st_ref))
        t_saved = [t_ref[j] for j in range(gs)]
        _, vjp = jax.vjp(lambda *a: _gdn_chunk(*a, t_saved=t_saved)[:2], *ins)
        dq, dk, dv, dgz, dsm, dal, ddt, dnw, ds = vjp((_head_tiles(do_ref, gs), lead(ds_ref)))
        for j in range(gs):
            rws = slice(j * CHUNK, (j + 1) * CHUNK)
            for h in range(h_):
                cols = slice(h * LANE, (h + 1) * LANE)
                for part, d in enumerate((dq, dk, dv)):
                    dqkv_ref[rws, part * hw + h * LANE:part * hw + (h + 1) * LANE] = d[j][h]
                dgz_ref[rws, cols] = dgz[j][h].astype(dgz_ref.dtype)
            dsm_ref[rws, :] = dsm[j]
        for h in range(h_):
            ds_ref[h] = ds[h]
        dal_ref[...] += dal
        ddt_ref[...] += ddt
        dnw_ref[...] += dnw

        if comm:
            @pl.when(pl.program_id(0) == steps - 1)
            def _():
                _comm_plan(comm[0], c_in, c_out, *sems)[1]()

    r = lambda c: steps - 1 - c
    blk = lambda col: pl.BlockSpec((rows, hw), lambda c: (r(c), col // hw))
    one = pl.BlockSpec((1, LANE), lambda c: (0, 0))
    hbm = pl.BlockSpec(memory_space=pl.ANY)
    return pl.pallas_call(
        body, name="gdn_bwd_" + comm[0] if comm else "gdn_bwd", grid=(steps,),
        in_specs=[blk(XC("gq")), blk(XC("gk")), blk(XC("gv")), blk(_MINE["gz"]),
                  pl.BlockSpec((rows, LANE), lambda c: (r(c), SMALL_OFF // LANE)), one, one,
                  pl.BlockSpec((1, LANE), lambda c: (0, 0)),
                  pl.BlockSpec((h_, None, LANE, LANE), lambda c: (0, r(c), 0, 0)),
                  pl.BlockSpec((gs, tn_, tn_), lambda c: (r(c), 0, 0)),
                  blk(0)] + [hbm] * n_comm,
        out_specs=[pl.BlockSpec((rows, 3 * hw), lambda c: (r(c), 0)), blk(_MINE["gz"]),
                   pl.BlockSpec((rows, LANE), lambda c: (r(c), 0)), one, one,
                   pl.BlockSpec((1, LANE), lambda c: (0, 0))] + [hbm] * n_comm,
        out_shape=[jax.ShapeDtypeStruct((length, CONV_ALL), F32), jax.ShapeDtypeStruct((length, N_PAD), BF16),
                   jax.ShapeDtypeStruct((length, LANE), F32),
                   jax.ShapeDtypeStruct((1, LANE), F32), jax.ShapeDtypeStruct((1, LANE), F32),
                   jax.ShapeDtypeStruct((1, LANE), F32)] + (_comm_out_shapes(*comm) if comm else []),
        scratch_shapes=[pltpu.VMEM((h_, LANE, LANE), F32)] + (_comm_scratch(n_comm) if comm else []),
        compiler_params=_cparams(("arbitrary",), VMEM_LIMIT),
    )(xc, xc, xc, proj, proj, alog, dtb, nw, st, tinv, dmix, *(comm[1] if comm else []))


SSD_CHUNKS_PER_STEP = 8


def _ssd_steps(length):
    nc = length // CHUNK
    gs = SSD_CHUNKS_PER_STEP if nc % SSD_CHUNKS_PER_STEP == 0 else 1
    return gs, nc // gs, gs * CHUNK


def _ssd_lanes(g):
    base = _MINE["sdt"] - SMALL_OFF
    return [[base + (g * PAIRS_PER_GROUP + p) * 2 + j for j in range(2)] for p in range(PAIRS_PER_GROUP)]


def _ssd_tiles(ref, gs):
    return [[ref[j * CHUNK:(j + 1) * CHUNK, p * LANE:(p + 1) * LANE] for p in range(PAIRS_PER_GROUP)]
            for j in range(gs)]


def _ssd_gate(y, z, w):
    return _rms(y * _silu(z), w)


def _ssd_fwd(xc, proj, alog_row, dtb_row, d_row, norm_w, mixed, length):
    gs, steps, rows = _ssd_steps(length)
    ppg = PAIRS_PER_GROUP
    gw = ppg * LANE

    def body(x_ref, b_ref, c_ref, sm_ref, al_ref, db_ref, dp_ref, z_ref, nw_ref, _, y_ref, o_ref, st_ref, hs_ref):
        @pl.when(pl.program_id(1) == 0)
        def _():
            hs_ref[...] = jnp.zeros_like(hs_ref)

        st_ref[...] = hs_ref[...]
        y, hs_new = _ssd_chunk(_ssd_lanes(pl.program_id(0)), _ssd_tiles(x_ref, gs), _chunk_rows(b_ref, gs),
                               _chunk_rows(c_ref, gs), _chunk_rows(sm_ref, gs), al_ref[...], db_ref[...],
                               dp_ref[...], [hs_ref[p] for p in range(ppg)])
        for p in range(ppg):
            hs_ref[p] = hs_new[p]
        for j in range(gs):
            rws = slice(j * CHUNK, (j + 1) * CHUNK)
            y_j = jnp.concatenate(y[j], axis=1)
            y_ref[rws, :] = y_j
            o_ref[rws, :] = _ssd_gate(y_j, z_ref[rws, :], nw_ref[...]).astype(o_ref.dtype)

    one = pl.BlockSpec((1, LANE), lambda g, c: (0, 0))
    return pl.pallas_call(
        body, name="ssd_fwd", grid=(SSD_GROUPS, steps),
        in_specs=[pl.BlockSpec((rows, gw), lambda g, c: (c, XC("sx") // gw + g)),
                  pl.BlockSpec((rows, LANE), lambda g, c: (c, XC("sB") // LANE + g)),
                  pl.BlockSpec((rows, LANE), lambda g, c: (c, XC("sC") // LANE + g)),
                  pl.BlockSpec((rows, LANE), lambda g, c: (c, SMALL_OFF // LANE)),
                  one, one, one,
                  pl.BlockSpec((rows, gw), lambda g, c: (c, _MINE["sz"] // gw + g)),
                  pl.BlockSpec((None, 1, gw), lambda g, c: (g, 0, 0)),
                  pl.BlockSpec(memory_space=pl.ANY)],
        out_specs=[pl.BlockSpec((rows, gw), lambda g, c: (c, g)),
                   pl.BlockSpec((rows, gw), lambda g, c: (c, 1 + g)),
                   pl.BlockSpec((None, ppg, None, LANE, SSD_N), lambda g, c: (g, 0, c, 0, 0))],
        out_shape=[jax.ShapeDtypeStruct((length, SSD_GROUPS * gw), F32),
                   jax.ShapeDtypeStruct(mixed.shape, mixed.dtype),
                   jax.ShapeDtypeStruct((SSD_GROUPS, ppg, steps, LANE, SSD_N), F32)],
        input_output_aliases={9: 1},
        scratch_shapes=[pltpu.VMEM((ppg, LANE, SSD_N), F32)],
        compiler_params=_cparams(("parallel", "arbitrary"), VMEM_LIMIT),
    )(xc, xc, xc, proj, alog_row, dtb_row, d_row, proj, norm_w, mixed)


def _ssd_bwd(xc, proj, alog_row, dtb_row, d_row, norm_w, st, y_b, dmix, dproj, dxc, length):
    gs, steps, rows = _ssd_steps(length)
    ppg = PAIRS_PER_GROUP
    gw = ppg * LANE

    def body(x_ref, b_ref, c_ref, sm_ref, al_ref, db_ref, dp_ref, z_ref, nw_ref, st_ref, y_ref, do_ref, _, __,
             dxc_ref, dz_ref, dsm_ref, dal_ref, ddb_ref, ddp_ref, dnw_ref, dhs_ref):
        g = pl.program_id(1)

        @pl.when(jnp.logical_and(pl.program_id(0) == 0, g == 0))
        def _():
            dhs_ref[...] = jnp.zeros_like(dhs_ref)
            dal_ref[...] = jnp.zeros_like(dal_ref)
            ddb_ref[...] = jnp.zeros_like(ddb_ref)
            ddp_ref[...] = jnp.zeros_like(ddp_ref)
            dnw_ref[...] = jnp.zeros_like(dnw_ref)

        pairs = range(ppg)
        dy, dnw = [], jnp.zeros((1, gw), F32)
        for j in range(gs):
            rws = slice(j * CHUNK, (j + 1) * CHUNK)
            _, gate_vjp = jax.vjp(_ssd_gate, y_ref[rws, :], z_ref[rws, :], nw_ref[...])
            dy_j, dz_j, dnw_j = gate_vjp(do_ref[rws, :].astype(F32))
            dz_ref[rws, :] = dz_j.astype(dz_ref.dtype)
            dy.append([dy_j[:, p * LANE:(p + 1) * LANE] for p in pairs])
            dnw = dnw + dnw_j
        dnw_ref[g] += dnw
        ins = (_ssd_tiles(x_ref, gs), _chunk_rows(b_ref, gs), _chunk_rows(c_ref, gs), _chunk_rows(sm_ref, gs),
               al_ref[...], db_ref[...], dp_ref[...], [st_ref[p] for p in pairs])
        _, vjp = jax.vjp(functools.partial(_ssd_chunk, _ssd_lanes(g)), *ins)
        dx, dbm, dcm, dsm, dal, ddb, ddp, dhs = vjp((dy, [dhs_ref[g, p] for p in pairs]))
        for grp in range(SSD_GROUPS):
            @pl.when(g == grp)
            def _(grp=grp):
                for j in range(gs):
                    rws = slice(j * CHUNK, (j + 1) * CHUNK)
                    for p in pairs:
                        col = grp * gw + p * LANE
                        dxc_ref[rws, col:col + LANE] = dx[j][p]
                    cb_ = XC("sB") - XC("sx") + grp * SSD_N
                    cc_ = XC("sC") - XC("sx") + grp * SSD_N
                    dxc_ref[rws, cb_:cb_ + SSD_N] = dbm[j]
                    dxc_ref[rws, cc_:cc_ + SSD_N] = dcm[j]
        for j in range(gs):
            dsm_ref[j * CHUNK:(j + 1) * CHUNK, :] = dsm[j]
        for p in pairs:
            dhs_ref[g, p] = dhs[p]
        dal_ref[g] += dal
        ddb_ref[g] += ddb
        ddp_ref[g] += ddp

    r = lambda c: steps - 1 - c
    one = pl.BlockSpec((1, LANE), lambda c, g: (0, 0))
    acc = pl.BlockSpec((SSD_GROUPS, 1, LANE), lambda c, g: (0, 0, 0))
    prm = jax.ShapeDtypeStruct((SSD_GROUPS, 1, LANE), F32)
    sz_blk = _MINE["sz"] // gw
    return pl.pallas_call(
        body, name="ssd_bwd", grid=(steps, SSD_GROUPS),
        in_specs=[pl.BlockSpec((rows, gw), lambda c, g: (r(c), XC("sx") // gw + g)),
                  pl.BlockSpec((rows, LANE), lambda c, g: (r(c), XC("sB") // LANE + g)),
                  pl.BlockSpec((rows, LANE), lambda c, g: (r(c), XC("sC") // LANE + g)),
                  pl.BlockSpec((rows, LANE), lambda c, g: (r(c), SMALL_OFF // LANE)),
                  one, one, one,
                  pl.BlockSpec((rows, gw), lambda c, g: (r(c), sz_blk + g)),
                  pl.BlockSpec((None, 1, gw), lambda c, g: (g, 0, 0)),
                  pl.BlockSpec((None, ppg, None, LANE, SSD_N), lambda c, g: (g, 0, r(c), 0, 0)),
                  pl.BlockSpec((rows, gw), lambda c, g: (r(c), g)),
                  pl.BlockSpec((rows, gw), lambda c, g: (r(c), 1 + g)),
                  pl.BlockSpec(memory_space=pl.ANY), pl.BlockSpec(memory_space=pl.ANY)],
        out_specs=[pl.BlockSpec((rows, CONV_CH), lambda c, g: (r(c), 1)),
                   pl.BlockSpec((rows, gw), lambda c, g: (r(c), sz_blk + g)),
                   pl.BlockSpec((None, rows, LANE), lambda c, g: (g, r(c), 0)), acc, acc, acc,
                   pl.BlockSpec((SSD_GROUPS, 1, gw), lambda c, g: (0, 0, 0))],
        out_shape=[jax.ShapeDtypeStruct(dxc.shape, dxc.dtype), jax.ShapeDtypeStruct(dproj.shape, dproj.dtype),
                   jax.ShapeDtypeStruct((SSD_GROUPS, length, LANE), F32), prm, prm, prm,
                   jax.ShapeDtypeStruct((SSD_GROUPS, 1, gw), F32)],
        input_output_aliases={12: 1, 13: 0},
        scratch_shapes=[pltpu.VMEM((SSD_GROUPS, ppg, LANE, SSD_N), F32)],
        compiler_params=_cparams(("arbitrary", "arbitrary"), VMEM_LIMIT),
    )(xc, xc, xc, proj, alog_row, dtb_row, d_row, proj, norm_w, st, y_b, dmix, dproj, dxc)


def _ret_fwd(proj, cos2, sin2, dmat, qdec, kdec, cdec, nw, mixed, length):
    gs, steps, rows = _scan_steps(length)
    h_ = RET_HEADS

    def body(q_ref, k_ref, v_ref, g_ref, cos_ref, sin_ref, dm_ref, qd_ref, kd_ref, cd_ref, nw_ref, _,
             on_ref, st_ref, r_ref):
        @pl.when(pl.program_id(0) == 0)
        def _():
            r_ref[...] = jnp.zeros_like(r_ref)

        st_ref[...] = r_ref[...]
        lead = lambda ref: [ref[h] for h in range(h_)]
        on, r_new = _ret_chunk(_head_tiles(q_ref, gs), _head_tiles(k_ref, gs), _head_tiles(v_ref, gs),
                               _head_tiles(g_ref, gs), _chunk_rows(cos_ref, gs), _chunk_rows(sin_ref, gs),
                               lead(dm_ref), lead(qd_ref), lead(kd_ref), lead(cd_ref), nw_ref[...], lead(r_ref))
        for h in range(h_):
            for j in range(gs):
                on_ref[j * CHUNK:(j + 1) * CHUNK, h * LANE:(h + 1) * LANE] = on[j][h].astype(on_ref.dtype)
            r_ref[h] = r_new[h]

    hw = h_ * LANE
    blk = lambda name: pl.BlockSpec((rows, hw), lambda c: (c, _MINE[name] // hw))
    tab = pl.BlockSpec((rows, LANE), lambda c: (c, 0))
    full = lambda *s: pl.BlockSpec(s, lambda c: (0,) * len(s))
    return pl.pallas_call(
        body, name="ret_fwd", grid=(steps,),
        in_specs=[blk("rq"), blk("rk"), blk("rv"), blk("rg"), tab, tab,
                  full(h_, CHUNK, CHUNK), full(h_, CHUNK, 1), full(h_, CHUNK, 1), full(h_, 1, 1), full(1, LANE),
                  pl.BlockSpec(memory_space=pl.ANY)],
        out_specs=[pl.BlockSpec((rows, hw), lambda c: (c, (MIX_W - hw) // hw)),
                   pl.BlockSpec((h_, None, LANE, LANE), lambda c: (0, c, 0, 0))],
        out_shape=[jax.ShapeDtypeStruct(mixed.shape, mixed.dtype),
                   jax.ShapeDtypeStruct((h_, steps, LANE, LANE), F32)],
        input_output_aliases={11: 0},
        scratch_shapes=[pltpu.VMEM((h_, LANE, LANE), F32)],
        compiler_params=_cparams(("arbitrary",), VMEM_LIMIT),
    )(proj, proj, proj, proj, cos2, sin2, dmat, qdec, kdec, cdec, nw, mixed)


def _ret_bwd(proj, cos2, sin2, dmat, qdec, kdec, cdec, nw, st, dmix, dproj, length):
    gs, steps, rows = _scan_steps(length)
    h_ = RET_HEADS
    hw = h_ * LANE

    def body(q_ref, k_ref, v_ref, g_ref, cos_ref, sin_ref, dm_ref, qd_ref, kd_ref, cd_ref, nw_ref, st_ref,
             do_ref, _, dqkvg_ref, dnw_ref, dr_ref):
        @pl.when(pl.program_id(0) == 0)
        def _():
            dr_ref[...] = jnp.zeros_like(dr_ref)
            dnw_ref[...] = jnp.zeros_like(dnw_ref)

        lead = lambda ref: [ref[h] for h in range(h_)]
        consts = (_chunk_rows(cos_ref, gs), _chunk_rows(sin_ref, gs), lead(dm_ref), lead(qd_ref), lead(kd_ref),
                  lead(cd_ref))
        f = lambda q, k, v, g, w_, r_: _ret_chunk(q, k, v, g, *consts, w_, r_)
        _, vjp = jax.vjp(f, _head_tiles(q_ref, gs), _head_tiles(k_ref, gs), _head_tiles(v_ref, gs),
                         _head_tiles(g_ref, gs), nw_ref[...], lead(st_ref))
        dq, dk, dv, dg, dnw, dr = vjp((_head_tiles(do_ref, gs), lead(dr_ref)))
        for h in range(h_):
            for j in range(gs):
                rws = slice(j * CHUNK, (j + 1) * CHUNK)
                for part, d in enumerate((dq, dk, dv, dg)):
                    col = part * hw + h * LANE
                    dqkvg_ref[rws, col:col + LANE] = d[j][h].astype(dqkvg_ref.dtype)
            dr_ref[h] = dr[h]
        dnw_ref[...] += dnw

    r = lambda c: steps - 1 - c
    blk = lambda name: pl.BlockSpec((rows, hw), lambda c: (r(c), _MINE[name] // hw))
    tab = pl.BlockSpec((rows, LANE), lambda c: (r(c), 0))
    full = lambda *s: pl.BlockSpec(s, lambda c: (0,) * len(s))
    assert _MINE["rq"] % (4 * hw) == 0 and [_MINE[n] - _MINE["rq"] for n in ("rk", "rv", "rg")] == [hw, 2 * hw, 3 * hw]
    return pl.pallas_call(
        body, name="ret_bwd", grid=(steps,),
        in_specs=[blk("rq"), blk("rk"), blk("rv"), blk("rg"), tab, tab,
                  full(h_, CHUNK, CHUNK), full(h_, CHUNK, 1), full(h_, CHUNK, 1), full(h_, 1, 1), full(1, LANE),
                  pl.BlockSpec((h_, None, LANE, LANE), lambda c: (0, r(c), 0, 0)),
                  pl.BlockSpec((rows, hw), lambda c: (r(c), (MIX_W - hw) // hw)),
                  pl.BlockSpec(memory_space=pl.ANY)],
        out_specs=[pl.BlockSpec((rows, 4 * hw), lambda c: (r(c), _MINE["rq"] // (4 * hw))), full(1, LANE)],
        out_shape=[jax.ShapeDtypeStruct(dproj.shape, dproj.dtype), jax.ShapeDtypeStruct((1, LANE), F32)],
        input_output_aliases={13: 0},
        scratch_shapes=[pltpu.VMEM((h_, LANE, LANE), F32)],
        compiler_params=_cparams(("arbitrary",), VMEM_LIMIT),
    )(proj, proj, proj, proj, cos2, sin2, dmat, qdec, kdec, cdec, nw, st, dmix, dproj)


def _mesh_pos():
    x, y, c = lax.axis_index("x"), lax.axis_index("y"), lax.axis_index("c")
    return x, y, c, 4 * x + 2 * y + c


def _peer(x, y, c, mask):
    return (x ^ ((mask >> 2) & 1), y ^ ((mask >> 1) & 1), c ^ (mask & 1))


def _gather_plan(ins, outs, send_sems, recv_sems, local_sems):
    x, y, c, me = _mesh_pos()
    n = len(ins)
    idx = lambda px, py, pc: 4 * px + 2 * py + pc
    sib = (x, y, 1 - c)
    chips = [(1 - x, y), (x, 1 - y), (1 - x, 1 - y)]
    local = [pltpu.make_async_copy(ins[a], outs[a].at[me], local_sems.at[a]) for a in range(n)]

    def copy(a, k, src, slab, to):
        return pltpu.make_async_remote_copy(
            src_ref=src, dst_ref=outs[a].at[slab], send_sem=send_sems.at[7 * a + k],
            recv_sem=recv_sems.at[7 * a + k], device_id=to, device_id_type=pl.DeviceIdType.MESH)

    first = [copy(a, 0, ins[a], me, sib) for a in range(n)]
    first += [copy(a, 1 + j, ins[a], me, (*chip, c)) for a in range(n) for j, chip in enumerate(chips)]
    landed = [[copy(a, 1 + j, ins[a], idx(*chip, c), (*chip, c)) for a in range(n)] for j, chip in enumerate(chips)]
    passed = [[copy(a, 4 + j, outs[a].at[idx(*chip, c)], idx(*chip, c), sib) for a in range(n)]
              for j, chip in enumerate(chips)]
    from_sib = [copy(a, 0, ins[a], idx(*sib), sib) for a in range(n)]
    from_sib += [copy(a, 4 + j, ins[a], idx(*chip, 1 - c), sib) for a in range(n) for j, chip in enumerate(chips)]

    def start():
        for cp in local + first:
            cp.start()

    def wait():
        for j in range(len(chips)):
            for a in range(n):
                landed[j][a].wait_recv()
                passed[j][a].start()
        for cp in from_sib:
            cp.wait_recv()
        for cp in first + [cp for row in passed for cp in row]:
            cp.wait_send()
        for cp in local:
            cp.wait()

    return start, wait


def _comm_plan(kind, ins, outs, send_sems, recv_sems, local_sems):
    if kind == "gather":
        return _gather_plan(ins, outs, send_sems, recv_sems, local_sems)
    x, y, c, me = _mesh_pos()
    n = len(ins)
    src = (lambda a, idx: ins[a]) if kind == "gather" else (lambda a, idx: ins[a].at[idx])
    local = [pltpu.make_async_copy(src(a, me), outs[a].at[me], local_sems.at[a]) for a in range(n)]
    sends, recvs = [], []
    for mask in range(1, N_DEV):
        px, py, pc = _peer(x, y, c, mask)
        pidx = 4 * px + 2 * py + pc
        for a in range(n):
            k = (mask - 1) * n + a
            copy = lambda s, d, k=k: pltpu.make_async_remote_copy(
                src_ref=s, dst_ref=d, send_sem=send_sems.at[k], recv_sem=recv_sems.at[k],
                device_id=(px, py, pc), device_id_type=pl.DeviceIdType.MESH)
            sends.append(copy(src(a, pidx), outs[a].at[me]))
            recvs.append(copy(src(a, me), outs[a].at[pidx]))

    def start():
        for cp in local + sends:
            cp.start()

    def wait():
        for cp in sends:
            cp.wait_send()
        for cp in recvs:
            cp.wait_recv()
        for cp in local:
            cp.wait()

    return start, wait


def _comm_out_shapes(kind, arrays):
    return [jax.ShapeDtypeStruct(((N_DEV,) + s.shape) if kind == "gather" else s.shape, s.dtype) for s in arrays]


def _comm_scratch(n):
    k_tot = (N_DEV - 1) * n
    return [pltpu.SemaphoreType.DMA((k_tot,)), pltpu.SemaphoreType.DMA((k_tot,)), pltpu.SemaphoreType.DMA((n,))]


def _comm_call(name, kind, arrays):
    n = len(arrays)

    def body(*refs):
        start, wait = _comm_plan(kind, refs[:n], refs[n:2 * n], *refs[2 * n:])
        start()
        wait()

    hbm = pl.BlockSpec(memory_space=pl.ANY)
    return pl.pallas_call(
        body, name=name, in_specs=[hbm] * n, out_specs=[hbm] * n, out_shape=_comm_out_shapes(kind, arrays),
        scratch_shapes=_comm_scratch(n), compiler_params=pltpu.CompilerParams(has_side_effects=True),
    )(*arrays)


def _adam(name, parts, w, m, v, tr, layer=None, prev=None):
    p_, r_, cw = parts.shape
    c1 = 1.0 - ADAM_B1 ** ADAM_STEP
    c2 = 1.0 - ADAM_B2 ** ADAM_STEP
    n_prev = 4 if prev else 0

    def body(p_ref, w_ref, m_ref, v_ref, *rest):
        g_ref, d_ref, nm_ref, nv_ref = rest[n_prev:]
        g = p_ref[0].astype(F32)
        for i in range(1, p_):
            g = g + p_ref[i].astype(F32)
        nm = ADAM_B1 * m_ref[...] + (1.0 - ADAM_B1) * g
        nv = ADAM_B2 * v_ref[...] + (1.0 - ADAM_B2) * (g * g)
        d_ref[...] = -ADAM_LR * ((nm / c1) / (jnp.sqrt(nv / c2) + ADAM_EPS) + ADAM_WD * w_ref[...])
        g_ref[...] = g
        nm_ref[...] = nm
        nv_ref[...] = nv

    if layer is None:
        spec = pl.BlockSpec((tr, cw), lambda i: (i, 0))
        shp = jax.ShapeDtypeStruct((r_, cw), F32)
    else:
        spec = pl.BlockSpec((None, tr, cw), lambda i: (layer, i, 0))
        shp = jax.ShapeDtypeStruct((DEPTH, r_, cw), F32)
    return pl.pallas_call(
        body, name=name, grid=(r_ // tr,),
        in_specs=[pl.BlockSpec((p_, tr, cw), lambda i: (0, i, 0)), spec, spec, spec]
                 + [pl.BlockSpec(memory_space=pl.ANY)] * n_prev,
        out_specs=[spec] * 4, out_shape=[shp] * 4,
        input_output_aliases={4 + i: i for i in range(n_prev)},
        compiler_params=_cparams(("parallel",), VMEM_LIMIT),
    )(parts, w, m, v, *(prev or ()))


def _shards_to_perm(g_in):
    parts = []
    for n in _ORDER:
        lo, hi = _ORIG[n][0], _ORIG[n][0] + _ORIG[n][1]
        for j in range(lo // SHARD_IN, (hi - 1) // SHARD_IN + 1):
            a, b = max(lo, j * SHARD_IN), min(hi, (j + 1) * SHARD_IN)
            parts.append(g_in[j][:, a - j * SHARD_IN:b - j * SHARD_IN])
    parts.append(jnp.zeros((g_in.shape[1], N_PAD - N_USED), g_in.dtype))
    return jnp.concatenate(parts, axis=1)


def _perm_to_shards(g):
    by_orig = sorted(_ORDER, key=lambda n: _ORIG[n][0])
    slabs = []
    for j in range(N_DEV):
        lo, hi = j * SHARD_IN, (j + 1) * SHARD_IN
        parts = []
        for n in by_orig:
            a, b = max(lo, _ORIG[n][0]), min(hi, _ORIG[n][0] + _ORIG[n][1])
            if a < b:
                parts.append(g[:, _MINE[n] + a - _ORIG[n][0]:_MINE[n] + b - _ORIG[n][0]])
        slabs.append(jnp.concatenate(parts, axis=1))
    return jnp.stack(slabs, axis=0)


def _ret_consts(length):
    lg = jnp.log(1.0 - 2.0 ** (-5.0 - jnp.arange(RET_HEADS, dtype=F32)))
    idx = jnp.arange(CHUNK, dtype=F32)
    rel = idx[:, None] - idx[None, :]
    dmat = jnp.where(rel[None] >= 0, jnp.exp(jnp.maximum(rel, 0.0)[None] * lg[:, None, None]), 0.0)
    qdec = jnp.exp((idx[:, None] + 1.0) * lg[None, :]).T[:, :, None]
    kdec = jnp.exp((CHUNK - 1.0 - idx)[:, None] * lg[None, :]).T[:, :, None]
    cdec = jnp.exp(CHUNK * lg)[:, None, None]
    half = RET_DK // 2
    inv = ROPE_BASE ** (-jnp.arange(half, dtype=F32) / half)
    ang = jnp.arange(length, dtype=jnp.int32).astype(F32)[:, None] * inv[None, :]
    cos, sin = jnp.cos(ang), jnp.sin(ang)
    return (jnp.concatenate([cos, cos], axis=1), jnp.concatenate([-sin, sin], axis=1), dmat, qdec, kdec, cdec)


def _tiles(length, pref=512):
    return pref if length % pref == 0 else length


def _layer_fwd(x, p, consts, length, last, target, comm=None):
    tm = _tiles(length)
    (h,), _ = _rowmap("pre_norm", lambda x_, w_: ((_rms(x_, w_),), ()), length, tm,
                      [(x, D_MODEL, 0)], [p["pre_norm"]], out_ws=[D_MODEL], out_dtypes=[BF16])
    proj = _matmul("in_proj", h, p["w_in"], "nn", length, N_PAD, D_MODEL)
    xc = _conv_fwd(proj, p["conv_w"], p["conv_b"], length, _tiles(length, 1024), 512)
    mixed, st_a, tinv, *got = _gdn_fwd(xc, proj, p["gdn_A_log"], p["gdn_dt_bias"], p["gdn_norm"], length, comm)
    y_b, mixed, st_b = _ssd_fwd(xc, proj, p["ssd_A_log"], p["ssd_dt_bias"], p["ssd_D"], p["ssd_norm"], mixed, length)
    mixed, st_c = _ret_fwd(proj, *consts, p["ret_norm"], mixed, length)
    w_out = p["w_out"](got) if callable(p["w_out"]) else p["w_out"]
    out = _matmul("out_proj", mixed, w_out, "nn", length, D_MODEL, MIX_W)
    res = dict(x=x, h=h, proj=proj, xc=xc, st_a=st_a, tinv=tinv, st_b=st_b, st_c=st_c, y_b=y_b, mixed=mixed,
               out=out, w_out=w_out)
    if not last:
        (y,), _ = _rowmap("post_norm", lambda x_, o_, w_: ((x_ + _rms(o_, w_),), ()), length, tm,
                          [(x, D_MODEL, 0), (out, D_MODEL, 0)], [p["post_norm"]], out_ws=[D_MODEL])
        return y, res, None, got

    def head(x_, o_, t_, w_):
        normed, post_vjp = jax.vjp(_rms, o_, w_)
        e = x_ + normed - t_
        row = jnp.mean(e * e, axis=-1, keepdims=True)
        loss = 0.5 * jnp.sum(row, axis=0, keepdims=True)
        dy_ = e * (1.0 / D_MODEL)
        d_out, dw = post_vjp(dy_)
        return (dy_, d_out), (loss + jnp.zeros((1, LANE), F32), dw)

    (dy, d_out), (loss, dw_post) = _rowmap(
        "loss_head", head, length, tm, [(x, D_MODEL, 0), (out, D_MODEL, 0), (target, D_MODEL, 0)], [p["post_norm"]],
        out_ws=[D_MODEL, D_MODEL], out_dtypes=[F32, BF16], acc_ws=[LANE, D_MODEL])
    res["post_bwd"] = (d_out, dw_post)
    return None, res, (dy, loss[0, 0]), got


def _layer_bwd(dy, p, res, consts, length, comm_scan=None, comm_dx=None):
    tm = _tiles(length)
    g = {}
    if "post_bwd" in res:
        d_out, g["post_norm"] = res["post_bwd"]
    else:
        (d_out,), (g["post_norm"],) = _rowmap("post_norm_bwd", lambda o_, w_: ((_rms(o_, w_),), ()), length, tm,
                                              [(res["out"], D_MODEL, 0)], [p["post_norm"]], cts=[(dy, D_MODEL, 0)],
                                              d_dtypes=[BF16])
    dmix = _matmul("out_proj_dx", d_out, res["w_out"], "nt", length, MIX_W, D_MODEL)
    g["w_out"] = _matmul("out_proj_dw", res["mixed"], d_out, "tn", MIX_W, D_MODEL, length, out_dtype=BF16)
    comm = comm_scan(g) if comm_scan else None
    proj, xc = res["proj"], res["xc"]
    (dxc, dproj, dsm_a, g["gdn_A_log"], g["gdn_dt_bias"], g["gdn_norm"], *got) = _gdn_bwd(
        xc, proj, p["gdn_A_log"], p["gdn_dt_bias"], p["gdn_norm"], res["st_a"], res["tinv"], dmix, length, comm)
    dxc, dproj, dsm_b, g["ssd_A_log"], g["ssd_dt_bias"], g["ssd_D"], g["ssd_norm"] = _ssd_bwd(
        xc, proj, p["ssd_A_log"], p["ssd_dt_bias"], p["ssd_D"], p["ssd_norm"], res["st_b"], res["y_b"], dmix,
        dproj, dxc, length)
    dproj, g["ret_norm"] = _ret_bwd(proj, *consts, p["ret_norm"], res["st_c"], dmix, dproj, length)
    tmc = _tiles(length, 1024)
    dpre, g["conv_w"], g["conv_b"] = _conv_bwd_pre(proj, p["conv_w"], p["conv_b"], dxc, length, tmc, 512)
    dproj = _conv_bwd_x(dpre, p["conv_w"], dproj, length, tmc, 512)
    pad = SMALL_OFF % 512 + 512 - LANE
    (dproj,), _ = _rowmap("dsmall", lambda a_, b_, c_: ((jnp.concatenate([a_ + b_ + c_, jnp.zeros((a_.shape[0], pad), F32)], axis=1),), ()),
                          length, tm, [(dsm_a, LANE, 0), (dsm_b[0], LANE, 0), (dsm_b[1], LANE, 0)], [],
                          out_ws=[512], out_dtypes=[BF16], place={0: (dproj, N_PAD, SMALL_OFF // 512)})
    g["w_in"] = _matmul("in_proj_dw", res["h"], dproj, "tn", D_MODEL, N_PAD, length, out_dtype=BF16)
    if comm_dx:
        dh, *got_dx = _matmul("in_proj_dx_scatter", dproj, p["w_in"], "nt", length, D_MODEL, N_PAD, comm=comm_dx(g))
    else:
        dh, got_dx = _matmul("in_proj_dx", dproj, p["w_in"], "nt", length, D_MODEL, N_PAD), []
    (dx,), (g["pre_norm"],) = _rowmap(
        "pre_norm_bwd", lambda x_, w_: ((_rms(x_, w_), x_), ()), length, tm,
        [(res["x"], D_MODEL, 0)], [p["pre_norm"]], cts=[(dh, D_MODEL, 0), (dy, D_MODEL, 0)])
    return dx, g, got, got_dx


def _lane_row(vals, piece):
    lo = _MINE[piece] - SMALL_OFF
    return jnp.pad(vals[None], ((0, 0), (lo, LANE - lo - vals.shape[0])))


def _from_lane_row(row, piece, n):
    lo = _MINE[piece] - SMALL_OFF
    return row.reshape(-1, LANE).sum(axis=0)[lo:lo + n]


def _make_layer(l, w_in_perm, w_out_full, conv_full, sw):
    return dict(
        pre_norm=sw["pre_norm"][l][None], post_norm=sw["post_norm"][l][None], w_in=w_in_perm,
        w_out=w_out_full,
        conv_w=jnp.concatenate([conv_full[0], conv_full[1]], axis=1),
        conv_b=jnp.concatenate([jnp.zeros((CONV_CH,), F32), sw["ssd_conv_b"][l]])[None],
        gdn_A_log=_lane_row(sw["gdn_A_log"][l], "ga"), gdn_dt_bias=_lane_row(sw["gdn_dt_bias"][l], "ga"),
        gdn_norm=sw["gdn_norm"][l][None],
        ssd_A_log=_lane_row(sw["ssd_A_log"][l], "sdt"), ssd_dt_bias=_lane_row(sw["ssd_dt_bias"][l], "sdt"),
        ssd_D=_lane_row(sw["ssd_D"][l], "sdt"),
        ssd_norm=sw["ssd_norm"][l].reshape(SSD_GROUPS, 1, -1),
        ret_norm=sw["ret_norm"][l][None])


def _local_step(xb, tgt, layer0, make_layer1, length, fwd_comm=None, comm_scan=None, comm_dx=None):
    consts = _ret_consts(length)
    y0, res0, _, got = _layer_fwd(xb, layer0, consts, length, False, tgt, fwd_comm)
    layer1 = make_layer1(got)
    _, res1, (dy, loss_local), _ = _layer_fwd(y0, layer1, consts, length, True, tgt)
    dy, grads1, _, _ = _layer_bwd(dy, layer1, res1, consts, length)
    dx, grads0, recv_scan, recv_dx = _layer_bwd(
        dy, layer0, res0, consts, length,
        (lambda g0: comm_scan(grads1, g0)) if comm_scan else None, comm_dx)
    return loss_local, dx, [grads0, grads1], recv_scan, recv_dx


_SMALL = ["pre_norm", "post_norm", "gdn_A_log", "gdn_dt_bias", "gdn_norm", "ssd_conv_b", "ssd_A_log",
          "ssd_dt_bias", "ssd_D", "ssd_norm", "ret_norm"]


def _pack_small(arrs):
    rows = []
    for n in _SMALL:
        flat = arrs[n].reshape(-1)
        pad = (-flat.shape[0]) % LANE
        rows.append(jnp.pad(flat, (0, pad)).reshape(-1, LANE))
    out = jnp.concatenate(rows, axis=0)
    return jnp.pad(out, ((0, (-out.shape[0]) % SUBLANE), (0, 0)))


def _unpack_small(packed, like):
    out, r = {}, 0
    for n in _SMALL:
        cnt = like[n].size
        nrow = -(-cnt // LANE)
        out[n] = packed[r:r + nrow].reshape(-1)[:cnt].reshape(like[n].shape)
        r += nrow
    return out


def kernel(x, pre_norm, post_norm, w_in, gdn_conv, gdn_A_log, gdn_dt_bias, gdn_norm, ssd_conv, ssd_conv_b, ssd_A_log, ssd_dt_bias, ssd_D, ssd_norm, ret_norm, w_out, loss_target, m_pre_norm, m_post_norm, m_w_in, m_gdn_conv, m_gdn_A_log, m_gdn_dt_bias, m_gdn_norm, m_ssd_conv, m_ssd_conv_b, m_ssd_A_log, m_ssd_dt_bias, m_ssd_D, m_ssd_norm, m_ret_norm, m_w_out, v_pre_norm, v_post_norm, v_w_in, v_gdn_conv, v_gdn_A_log, v_gdn_dt_bias, v_gdn_norm, v_ssd_conv, v_ssd_conv_b, v_ssd_A_log, v_ssd_dt_bias, v_ssd_D, v_ssd_norm, v_ret_norm, v_w_out):
    length = x.shape[1]
    xb = x[0]
    tgt = loss_target[0]
    small_w = dict(pre_norm=pre_norm, post_norm=post_norm, gdn_A_log=gdn_A_log, gdn_dt_bias=gdn_dt_bias,
                   gdn_norm=gdn_norm, ssd_conv_b=ssd_conv_b, ssd_A_log=ssd_A_log, ssd_dt_bias=ssd_dt_bias,
                   ssd_D=ssd_D, ssd_norm=ssd_norm, ret_norm=ret_norm)
    small_m = dict(pre_norm=m_pre_norm, post_norm=m_post_norm, gdn_A_log=m_gdn_A_log, gdn_dt_bias=m_gdn_dt_bias,
                   gdn_norm=m_gdn_norm, ssd_conv_b=m_ssd_conv_b, ssd_A_log=m_ssd_A_log, ssd_dt_bias=m_ssd_dt_bias,
                   ssd_D=m_ssd_D, ssd_norm=m_ssd_norm, ret_norm=m_ret_norm)
    small_v = dict(pre_norm=v_pre_norm, post_norm=v_post_norm, gdn_A_log=v_gdn_A_log, gdn_dt_bias=v_gdn_dt_bias,
                   gdn_norm=v_gdn_norm, ssd_conv_b=v_ssd_conv_b, ssd_A_log=v_ssd_A_log, ssd_dt_bias=v_ssd_dt_bias,
                   ssd_D=v_ssd_D, ssd_norm=v_ssd_norm, ret_norm=v_ret_norm)

    w_in_b, w_out_b = w_in.astype(BF16), w_out.astype(BF16)
    conv_shard = jnp.stack([gdn_conv, ssd_conv], axis=1)
    full_out = lambda g_out: g_out.reshape(MIX_W, D_MODEL)

    def assemble(l, g_in, w_out_full, g_conv):
        conv_full = g_conv.transpose(1, 2, 0, 3).reshape(2, CONV_W, CONV_CH)
        return _make_layer(l, _shards_to_perm(g_in), w_out_full, conv_full, small_w)

    slab_in = lambda g: _perm_to_shards(g["w_in"]).astype(BF16)
    slab_out = lambda g: g["w_out"].reshape(N_DEV, SHARD_OUT, D_MODEL).astype(BF16)
    slab_conv = lambda g: jnp.stack(
        [g["conv_w"][:, k * CONV_CH:(k + 1) * CONV_CH].reshape(CONV_W, N_DEV, SHARD_CONV).transpose(1, 0, 2)
         for k in range(2)], axis=1).reshape(N_DEV, 2 * CONV_W, SHARD_CONV)

    g_in0, g_conv0 = _comm_call("gather_layer0", "gather", [w_in_b[0], conv_shard[0]])
    layer0 = assemble(0, g_in0, lambda got: full_out(got[3]), g_conv0)
    loss_local, dx, grads, recv_scan, recv_dx = _local_step(
        xb, tgt, layer0, lambda got: assemble(1, got[0], full_out(got[1]), got[2]), length,
        fwd_comm=("gather", [w_in_b[1], w_out_b[1], conv_shard[1], w_out_b[0]]),
        comm_scan=lambda g1, g0: ("scatter", [slab_in(g1), slab_out(g1), slab_conv(g1), slab_out(g0)]),
        comm_dx=lambda g0: ("scatter", [slab_in(g0), slab_conv(g0)]))
    grad_x = dx[None]
    loss = lax.psum(loss_local, ("x", "y", "c"))

    small_g = dict(
        pre_norm=jnp.concatenate([grads[l]["pre_norm"] for l in range(DEPTH)], axis=0),
        post_norm=jnp.concatenate([grads[l]["post_norm"] for l in range(DEPTH)], axis=0),
        gdn_A_log=jnp.stack([_from_lane_row(grads[l]["gdn_A_log"], "ga", GDN_HEADS) for l in range(DEPTH)]),
        gdn_dt_bias=jnp.stack([_from_lane_row(grads[l]["gdn_dt_bias"], "ga", GDN_HEADS) for l in range(DEPTH)]),
        gdn_norm=jnp.concatenate([grads[l]["gdn_norm"] for l in range(DEPTH)], axis=0),
        ssd_conv_b=jnp.concatenate([grads[l]["conv_b"][:, CONV_CH:] for l in range(DEPTH)], axis=0),
        ssd_A_log=jnp.stack([_from_lane_row(grads[l]["ssd_A_log"], "sdt", SSD_HEADS) for l in range(DEPTH)]),
        ssd_dt_bias=jnp.stack([_from_lane_row(grads[l]["ssd_dt_bias"], "sdt", SSD_HEADS) for l in range(DEPTH)]),
        ssd_D=jnp.stack([_from_lane_row(grads[l]["ssd_D"], "sdt", SSD_HEADS) for l in range(DEPTH)]),
        ssd_norm=jnp.concatenate([grads[l]["ssd_norm"].reshape(1, -1) for l in range(DEPTH)], axis=0),
        ret_norm=jnp.concatenate([grads[l]["ret_norm"] for l in range(DEPTH)], axis=0))
    gs = _pack_small(small_g)
    gs8 = jnp.broadcast_to(gs[None], (N_DEV,) + gs.shape)
    (r_small,) = _comm_call("exchange_small", "scatter", [gs8])
    recv = [[recv_dx[0], recv_scan[3], recv_dx[1]], recv_scan[:3]]

    conv_w = lambda g_, s_, l: jnp.stack([g_[l], s_[l]], axis=0).reshape(2 * CONV_W, SHARD_CONV)
    o_in, o_out, o_conv = None, None, []
    for l in range(DEPTH):
        o_in = _adam(f"adam_w_in{l}", recv[l][0], w_in, m_w_in, v_w_in, 128, l, o_in)
        o_out = _adam(f"adam_w_out{l}", recv[l][1], w_out, m_w_out, v_w_out, 128, l, o_out)
        o_conv.append(_adam(f"adam_conv{l}", recv[l][2], conv_w(gdn_conv, ssd_conv, l),
                            conv_w(m_gdn_conv, m_ssd_conv, l), conv_w(v_gdn_conv, v_ssd_conv, l), 2 * CONV_W))
    ps_w, ps_m, ps_v = _pack_small(small_w), _pack_small(small_m), _pack_small(small_v)
    o_small = _adam("adam_small", r_small, ps_w, ps_m, ps_v, ps_w.shape[0])

    names = ["pre_norm", "post_norm", "w_in", "gdn_conv", "gdn_A_log", "gdn_dt_bias", "gdn_norm", "ssd_conv",
             "ssd_conv_b", "ssd_A_log", "ssd_dt_bias", "ssd_D", "ssd_norm", "ret_norm", "w_out"]
    outs = []
    for kind in range(4):
        d = _unpack_small(o_small[kind], small_w)
        cv = jnp.stack([o_conv[l][kind].reshape(2, CONV_W, SHARD_CONV) for l in range(DEPTH)], axis=1)
        d["w_in"] = o_in[kind]
        d["w_out"] = o_out[kind]
        d["gdn_conv"] = cv[0]
        d["ssd_conv"] = cv[1]
        outs.extend(d[n] for n in names)
    return (loss, grad_x, *outs)
```

```python
import functools
import math

import jax
import jax.numpy as jnp
from jax import lax
from jax.experimental import pallas as pl
from jax.experimental.pallas import tpu as pltpu

F32 = jnp.float32
BF16 = jnp.bfloat16

D_MODEL = 1024
DEPTH = 2
CHUNK = 64
CONV_W = 4
EPS = 1e-6
N_DEV = 8

GDN_HEADS = 4
GDN_DK = 128
SSD_HEADS = 16
SSD_P = 64
SSD_N = 128
SSD_GROUPS = 2
SSD_PAIRS = SSD_HEADS // 2
PAIRS_PER_GROUP = SSD_PAIRS // SSD_GROUPS
RET_HEADS = 4
RET_DK = 128
ROPE_BASE = 10000.0
MIX_W = 2048
N_IN = 6680
SHARD_IN = N_IN // N_DEV
SHARD_OUT = MIX_W // N_DEV
CONV_CH = 1536
SHARD_CONV = CONV_CH // N_DEV

ADAM_LR = 0.001
ADAM_B1 = 0.9
ADAM_B2 = 0.999
ADAM_EPS = 1e-08
ADAM_WD = 0.01
ADAM_STEP = 10

LANE = 128
SUBLANE = 8
VMEM_LIMIT = 56 * 1024 * 1024

_ORIG = dict(gq=(0, 512), gk=(512, 512), gv=(1024, 512), gz=(1536, 512), gb=(2048, 4), ga=(2052, 4),
             sx=(2056, 1024), sB=(3080, 256), sC=(3336, 256), sz=(3592, 1024), sdt=(4616, 16),
             rq=(4632, 512), rk=(5144, 512), rv=(5656, 512), rg=(6168, 512))
_ORDER = ["rq", "rk", "rv", "rg", "gq", "gk", "gv", "sx", "sB", "sC", "gz", "sz", "gb", "ga", "sdt"]
_MINE = {}
_off = 0
for _n in _ORDER:
    _MINE[_n] = _off
    _off += _ORIG[_n][1]
N_USED = _off
N_PAD = 7168
CONV_ALL = 2 * CONV_CH
SMALL_OFF = _MINE["gb"]
CONV_OFF = _MINE["gq"]
XC = lambda name: _MINE[name] - CONV_OFF


def _cparams(sem, vmem=None):
    return pltpu.CompilerParams(dimension_semantics=sem, vmem_limit_bytes=vmem)


def _split_bf16(a):
    hi = a.astype(BF16)
    return hi, (a - hi.astype(F32)).astype(BF16)


def _make_mm():
    def raw(a, b, ca, cb):
        return lax.dot_general(a.astype(BF16), b.astype(BF16), (((ca,), (cb,)), ((), ())),
                               preferred_element_type=F32)

    @jax.custom_vjp
    def nn(a, b):
        return raw(a, b, 1, 0)

    @jax.custom_vjp
    def nt(a, b):
        return raw(a, b, 1, 1)

    @jax.custom_vjp
    def tn(a, b):
        return raw(a, b, 0, 0)

    nn.defvjp(lambda a, b: (raw(a, b, 1, 0), (a, b)), lambda r, g: (nt(g, r[1]), tn(r[0], g)))
    nt.defvjp(lambda a, b: (raw(a, b, 1, 1), (a, b)), lambda r, g: (nn(g, r[1]), tn(g, r[0])))
    tn.defvjp(lambda a, b: (raw(a, b, 0, 0), (a, b)), lambda r, g: (nt(r[1], g), nn(r[0], g)))
    return nn, nt, tn


_nn, _nt, _tn = _make_mm()


@jax.custom_vjp
def _swap_halves(t):
    return pltpu.roll(t, LANE // 2, 1)


_swap_halves.defvjp(lambda t: (pltpu.roll(t, LANE // 2, 1), None),
                    lambda _, g: (pltpu.roll(g, LANE // 2, 1),))


@jax.custom_vjp
def _split_rows(x):
    return tuple(x[i * CHUNK:(i + 1) * CHUNK] for i in range(x.shape[0] // CHUNK))


_split_rows.defvjp(lambda x: (tuple(x[i * CHUNK:(i + 1) * CHUNK] for i in range(x.shape[0] // CHUNK)), None),
                   lambda _, gs: (jnp.concatenate(gs, axis=0),))


def _dot3(a, b, ca, cb):
    dot = lambda x, y: lax.dot_general(x, y, (((ca,), (cb,)), ((), ())), preferred_element_type=F32)
    return dot(a[0], b[0]) + (dot(a[0], b[1]) + dot(a[1], b[0]))


@jax.custom_vjp
def _tri_inv(mats):
    return _tri_inv_impl(mats)


def _tri_inv_impl(mats):
    ii = lax.broadcasted_iota(jnp.int32, mats[0].shape, 0)
    jj = lax.broadcasted_iota(jnp.int32, mats[0].shape, 1)
    eye = jnp.where(ii == jj, 1.0, 0.0).astype(F32)
    ts = [eye - a for a in mats]
    ps = [_split_bf16(-a) for a in mats]
    for _ in range(int(math.log2(CHUNK)) - 1):
        ps = [_split_bf16(_dot3(p, p, 1, 0)) for p in ps]
        ts = [t + _dot3(_split_bf16(t), p, 1, 0) for t, p in zip(ts, ps)]
    return ts


def _tri_inv_bwd(ts, gs):
    tsp = [_split_bf16(t) for t in ts]
    xs = [_dot3(t, _split_bf16(g), 0, 0) for t, g in zip(tsp, gs)]
    return ([-_dot3(_split_bf16(x), t, 1, 1) for x, t in zip(xs, tsp)],)


_tri_inv.defvjp(lambda mats: (lambda ts: (ts, ts))(_tri_inv_impl(mats)), _tri_inv_bwd)


@jax.custom_vjp
def _tri_inv_saved(mats, ts):
    return ts


_tri_inv_saved.defvjp(lambda mats, ts: (ts, ts),
                      lambda ts, gs: (_tri_inv_bwd(ts, gs)[0], [jnp.zeros_like(t) for t in ts]))


def _silu(x):
    return x * jax.nn.sigmoid(x)


@jax.custom_vjp
def _softplus(x):
    return jnp.maximum(x, 0.0) + jnp.log1p(jnp.exp(-jnp.abs(x)))


_softplus.defvjp(lambda x: (jnp.maximum(x, 0.0) + jnp.log1p(jnp.exp(-jnp.abs(x))), x),
                 lambda x, g: (g * jax.nn.sigmoid(x),))


def _rms(x, w):
    return x * lax.rsqrt(jnp.mean(x * x, axis=-1, keepdims=True) + EPS) * w


def _chunk_masks(n):
    ii = lax.broadcasted_iota(jnp.int32, (n, n), 0)
    jj = lax.broadcasted_iota(jnp.int32, (n, n), 1)
    return ii >= jj, ii > jj, ii == jj, ii <= jj


def _cumsum_col(g, causal, eye, upper):
    g_row = jnp.sum(jnp.where(eye, g, 0.0), axis=0, keepdims=True)
    col = jnp.sum(jnp.where(causal, g_row, 0.0), axis=1, keepdims=True)
    row = jnp.sum(jnp.where(upper, g, 0.0), axis=0, keepdims=True)
    return col, row


def _lane_col(block, lane):
    pick = lax.broadcasted_iota(jnp.int32, (1, block.shape[1]), 1) == lane
    return jnp.sum(jnp.where(pick, block, 0.0), axis=1, keepdims=True)


def _gdn_chunk(q, k, v, gz, sm, alog_row, dtb_row, nw, s, t_saved=None):
    subs, nh = range(len(q)), len(q[0])
    c = q[0][0].shape[0]
    causal, _, eye, upper = _chunk_masks(c)
    stack = lambda xs: jnp.concatenate(xs, axis=0)
    per = lambda f: [[f(j, h) for h in range(nh)] for j in subs]
    qn = per(lambda j, h: q[j][h] * lax.rsqrt(jnp.sum(q[j][h] * q[j][h], axis=-1, keepdims=True) + EPS)
             * (GDN_DK ** -0.5))
    kn = per(lambda j, h: k[j][h] * lax.rsqrt(jnp.sum(k[j][h] * k[j][h], axis=-1, keepdims=True) + EPS))
    beta_blk = [jax.nn.sigmoid(sm[j]) for j in subs]
    g_blk = [-jnp.exp(alog_row) * _softplus(sm[j] + dtb_row) for j in subs]
    beta = per(lambda j, h: _lane_col(beta_blk[j], _MINE["gb"] - SMALL_OFF + h))
    g = per(lambda j, h: _lane_col(g_blk[j], _MINE["ga"] - SMALL_OFF + h))
    gcum = per(lambda j, h: _cumsum_col(g[j][h], causal, eye, upper)[0])
    eg = per(lambda j, h: jnp.exp(gcum[j][h]))
    glast = per(lambda j, h: jnp.sum(g[j][h], axis=0, keepdims=True))
    kb = per(lambda j, h: kn[j][h] * beta[j][h])
    n = nh * c
    ii = lax.broadcasted_iota(jnp.int32, (n, n), 0)
    jj = lax.broadcasted_iota(jnp.int32, (n, n), 1)
    sh = int(math.log2(c))
    same = lax.shift_right_logical(ii, sh) == lax.shift_right_logical(jj, sh)
    causal_bd = jnp.logical_and(same, ii >= jj)
    strict_bd = jnp.logical_and(same, ii > jj)
    gcum_all = [stack(gcum[j]) for j in subs]
    gcum_row = [jnp.sum(jnp.where(ii == jj, gcum_all[j], 0.0), axis=0, keepdims=True) for j in subs]
    decay = [jnp.where(causal_bd, jnp.exp(jnp.where(causal_bd, gcum_all[j] - gcum_row[j], 0.0)), 0.0) for j in subs]
    kn_all = [stack(kn[j]) for j in subs]
    a_low = [jnp.where(strict_bd, _nt(stack(kb[j]), kn_all[j]) * decay[j], 0.0) for j in subs]
    t = _tri_inv(a_low) if t_saved is None else _tri_inv_saved(a_low, t_saved)
    u = [_nn(t[j], stack([v[j][h] * beta[j][h] for h in range(nh)])) for j in subs]
    w = [_split_rows(_nn(t[j], stack([kb[j][h] * eg[j][h] for h in range(nh)]))) for j in subs]
    attn = [_nt(stack(qn[j]), kn_all[j]) * decay[j] for j in subs]
    kdec = per(lambda j, h: kn[j][h] * jnp.exp(glast[j][h] - gcum[j][h]))
    on = []
    for j in subs:
        v_new_all = u[j] - stack([_nn(w[j][h], s[h]) for h in range(nh)])
        o = _split_rows(stack([_nn(qn[j][h] * eg[j][h], s[h]) for h in range(nh)]) + _nn(attn[j], v_new_all))
        v_new = _split_rows(v_new_all)
        s = [s[h] * jnp.exp(glast[j][h]) + _tn(kdec[j][h], v_new[h]) for h in range(nh)]
        on.append([_rms(o[h], nw) * _silu(gz[j][h]) for h in range(nh)])
    return on, s, t


def _ssd_chunk(lanes, x, bm, cm, sm, alog_row, dtb_row, d_row, hs):
    subs, pairs = range(len(x)), range(len(x[0]))
    c = x[0][0].shape[0]
    lane_i = lax.broadcasted_iota(jnp.int32, (c, LANE), 1)
    lane_lo = lane_i < SSD_P
    lane_lo1 = lax.broadcasted_iota(jnp.int32, (1, LANE), 1) < SSD_P
    row_i = lax.broadcasted_iota(jnp.int32, (c, LANE), 0)
    causal2 = row_i >= jnp.bitwise_and(lane_i, c - 1)
    diag2 = row_i == jnp.bitwise_and(lane_i, c - 1)
    to_row = lambda blk: jnp.sum(jnp.where(diag2, blk, 0.0), axis=0, keepdims=True)
    per = lambda f: [[f(j, p) for p in pairs] for j in subs]
    both = lambda blk, p: [_lane_col(blk, lanes[p][h]) for h in range(2)]
    dt_blk = [_softplus(sm[j] + dtb_row) for j in subs]
    a_blk = [dt_blk[j] * -jnp.exp(alog_row) for j in subs]
    alast_blk = [jnp.sum(a_blk[j], axis=0, keepdims=True) for j in subs]
    dt = per(lambda j, p: both(dt_blk[j], p))
    nexp = [both(-jnp.exp(alog_row), p) for p in pairs]
    alast = per(lambda j, p: both(alast_blk[j], p))
    dp = [both(d_row, p) for p in pairs]
    dt_lane = per(lambda j, p: jnp.where(lane_lo, dt[j][p][0], dt[j][p][1]))
    a_lane = per(lambda j, p: dt_lane[j][p] * jnp.where(lane_lo1, nexp[p][0], nexp[p][1]))
    a_row = per(lambda j, p: to_row(a_lane[j][p]))
    part = per(lambda j, p: jnp.where(causal2, a_row[j][p], 0.0))
    acum = per(lambda j, p: [jnp.sum(jnp.where(lane_lo, part[j][p], 0.0), axis=1, keepdims=True),
                             jnp.sum(jnp.where(lane_lo, 0.0, part[j][p]), axis=1, keepdims=True)])
    acum_col = per(lambda j, p: jnp.where(lane_lo, acum[j][p][0], acum[j][p][1]))
    acum_row = per(lambda j, p: to_row(acum_col[j][p]))
    lmat = per(lambda j, p: jnp.where(causal2, jnp.exp(jnp.where(causal2, acum_col[j][p] - acum_row[j][p], 0.0)), 0.0))
    cb = [_nt(cm[j], jnp.concatenate([bm[j], bm[j]], axis=0)) for j in subs]
    xdt = per(lambda j, p: x[j][p] * dt_lane[j][p])
    xdt_rows = per(lambda j, p: jnp.concatenate([jnp.where(lane_lo, xdt[j][p], 0.0),
                                                 jnp.where(lane_lo, 0.0, xdt[j][p])], axis=0))
    intra = per(lambda j, p: _nn(cb[j] * lmat[j][p], xdt_rows[j][p]))
    skip = per(lambda j, p: x[j][p] * jnp.where(lane_lo1, dp[p][0], dp[p][1]))
    eacc = per(lambda j, p: jnp.exp(acum_col[j][p]))
    alast_row = per(lambda j, p: jnp.where(lane_lo1, alast[j][p][0], alast[j][p][1]))
    wdec = per(lambda j, p: jnp.exp(alast_row[j][p] - acum_col[j][p]))
    scale = per(lambda j, p: jnp.exp(alast_row[j][p]))
    upd = per(lambda j, p: _tn(bm[j], xdt[j][p] * wdec[j][p]))
    y = []
    for j in subs:
        y.append([skip[j][p] + intra[j][p] + _nn(cm[j], hs[p]) * eacc[j][p] for p in pairs])
        hs = [hs[p] * scale[j][p] + upd[j][p] for p in pairs]
    return y, hs


def _ret_chunk(rq, rk, rv, rg, cos2, sin2, dmat, qdec, kdec, cdec, nw, r):
    subs, hs = range(len(rq)), range(len(rq[0]))
    per = lambda f: [[f(j, h) for h in hs] for j in subs]
    q = per(lambda j, h: rq[j][h] * cos2[j] + _swap_halves(rq[j][h]) * sin2[j])
    k = per(lambda j, h: (rk[j][h] * cos2[j] + _swap_halves(rk[j][h]) * sin2[j]) * (RET_DK ** -0.5))
    s = per(lambda j, h: _nt(q[j][h], k[j][h]) * dmat[h])
    intra = per(lambda j, h: _nn(s[j][h], rv[j][h]))
    upd = per(lambda j, h: _tn(k[j][h] * kdec[h], rv[j][h]))
    gate = per(lambda j, h: _silu(rg[j][h]))
    on = []
    for j in subs:
        on.append([_rms(intra[j][h] + _nn(q[j][h], r[h]) * qdec[h], nw) * gate[j][h] for h in hs])
        r = [r[h] * cdec[h] + upd[j][h] for h in hs]
    return on, r


MM_TILE = 1024


def _tile(n, pref=MM_TILE):
    return pref if n % pref == 0 else n


def _matmul(name, a, b, mode, m, n, k, tk=None, comm=None, out_dtype=F32):
    tm, tn = _tile(m), _tile(n)
    tk = _tile(k) if tk is None else tk
    nk = k // tk
    ca, cb = {"nn": (1, 0), "nt": (1, 1), "tn": (0, 0)}[mode]
    grid = (m // tm, n // tn, nk)
    n_comm = len(comm[1]) if comm else 0
    use_acc = out_dtype != F32 and nk > 1

    def at_step(which):
        return functools.reduce(jnp.logical_and, [pl.program_id(d) == (0 if which == 0 else grid[d] - 1)
                                                  for d in range(3)])

    def body(*refs):
        a_ref, b_ref = refs[:2]
        c_in = refs[2:2 + n_comm]
        o_ref = refs[2 + n_comm]
        c_out = refs[3 + n_comm:3 + 2 * n_comm]
        acc_ref = refs[3 + 2 * n_comm] if use_acc else o_ref
        sems = refs[3 + 2 * n_comm + use_acc:]
        if comm:
            @pl.when(at_step(0))
            def _():
                _comm_plan(comm[0], c_in, c_out, *sems)[0]()

        part = lax.dot_general(a_ref[...].astype(BF16), b_ref[...].astype(BF16),
                               (((ca,), (cb,)), ((), ())), preferred_element_type=F32)
        if nk == 1:
            o_ref[...] = part.astype(o_ref.dtype)
        else:
            kk = pl.program_id(2)

            @pl.when(kk == 0)
            def _():
                acc_ref[...] = part

            @pl.when(kk > 0)
            def _():
                acc_ref[...] += part

            if use_acc:
                @pl.when(kk == nk - 1)
                def _():
                    o_ref[...] = acc_ref[...].astype(o_ref.dtype)

        if comm:
            @pl.when(at_step(1))
            def _():
                _comm_plan(comm[0], c_in, c_out, *sems)[1]()

    a_spec = pl.BlockSpec((tk, tm), lambda i, j, kk: (kk, i)) if mode == "tn" else pl.BlockSpec((tm, tk), lambda i, j, kk: (i, kk))
    b_spec = pl.BlockSpec((tn, tk), lambda i, j, kk: (j, kk)) if mode == "nt" else pl.BlockSpec((tk, tn), lambda i, j, kk: (kk, j))
    hbm = pl.BlockSpec(memory_space=pl.ANY)
    res = pl.pallas_call(
        body, name=name, grid=grid,
        in_specs=[a_spec, b_spec] + [hbm] * n_comm,
        out_specs=[pl.BlockSpec((tm, tn), lambda i, j, kk: (i, j))] + [hbm] * n_comm,
        out_shape=[jax.ShapeDtypeStruct((m, n), out_dtype)] + (_comm_out_shapes(*comm) if comm else []),
        scratch_shapes=([pltpu.VMEM((tm, tn), F32)] if use_acc else []) + (_comm_scratch(n_comm) if comm else []),
        compiler_params=_cparams(("arbitrary",) * 3 if comm else ("parallel", "parallel", "arbitrary"), VMEM_LIMIT),
    )(a, b, *(comm[1] if comm else []))
    return res if comm else res[0]


def _rowmap(name, fn, length, tm, rows, params, out_ws=(), acc_ws=(), out_dtypes=None, cts=None, d_dtypes=None,
            place=None):
    nt_ = length // tm
    n_r, n_p = len(rows), len(params)
    n_in = n_r + n_p
    rspec = lambda bw, cbk: pl.BlockSpec((tm, bw), lambda i: (i, cbk))
    pspec = lambda w: pl.BlockSpec((1, w), lambda i: (0, 0))
    in_arrays = [r[0] for r in rows] + list(params)
    in_specs = [rspec(r[1], r[2]) for r in rows] + [pspec(p.shape[1]) for p in params]
    place = place or {}

    def load(refs):
        return [r[...].astype(F32) for r in refs]

    def placed(widths, dtypes, n_before):
        specs, shapes, extra, alias = [], [], [], {}
        for k, (w, dt) in enumerate(zip(widths, dtypes)):
            arr, total, cbk = place.get(k, (None, w, 0))
            specs.append(rspec(w, cbk))
            shapes.append(jax.ShapeDtypeStruct((length, total), dt))
            if arr is not None:
                alias[n_before + len(extra)] = k
                extra.append(arr)
        return specs, shapes, extra, alias

    def accumulate(a_refs, vals):
        first = pl.program_id(0) == 0
        for a_ref, v in zip(a_refs, vals):
            @pl.when(first)
            def _(a_ref=a_ref):
                a_ref[...] = jnp.zeros_like(a_ref)
            a_ref[...] += v

    hbm = pl.BlockSpec(memory_space=pl.ANY)
    if cts is None:
        n_o = len(out_ws)
        o_specs, o_shapes, extra, alias = placed(out_ws, out_dtypes or [F32] * n_o, n_in)

        def body(*refs):
            outs, accs = fn(*load(refs[:n_in]))
            o_refs = refs[n_in + len(extra):]
            for o_ref, o in zip(o_refs[:n_o], outs):
                o_ref[...] = o.astype(o_ref.dtype)
            accumulate(o_refs[n_o:], accs)

        res = pl.pallas_call(
            body, name=name, grid=(nt_,), in_specs=in_specs + [hbm] * len(extra),
            out_specs=o_specs + [pspec(w) for w in acc_ws],
            out_shape=o_shapes + [jax.ShapeDtypeStruct((1, w), F32) for w in acc_ws],
            input_output_aliases=alias, compiler_params=_cparams(("arbitrary",), VMEM_LIMIT),
        )(*in_arrays, *extra)
        return res[:n_o], res[n_o:]

    n_c = len(cts)
    o_specs, o_shapes, extra, alias = placed([r[1] for r in rows], d_dtypes or [F32] * n_r, n_in + n_c)

    def body(*refs):
        ins = load(refs[:n_in])
        _, vjp = jax.vjp(lambda rs, ps: fn(*rs, *ps)[0], ins[:n_r], ins[n_r:])
        d_rows, d_params = vjp(tuple(load(refs[n_in:n_in + n_c])))
        o_refs = refs[n_in + n_c + len(extra):]
        for o_ref, d in zip(o_refs[:n_r], d_rows):
            o_ref[...] = d.astype(o_ref.dtype)
        accumulate(o_refs[n_r:], d_params)

    res = pl.pallas_call(
        body, name=name, grid=(nt_,),
        in_specs=in_specs + [rspec(c[1], c[2]) for c in cts] + [hbm] * len(extra),
        out_specs=o_specs + [pspec(p.shape[1]) for p in params],
        out_shape=o_shapes + [jax.ShapeDtypeStruct((1, p.shape[1]), F32) for p in params],
        input_output_aliases=alias, compiler_params=_cparams(("arbitrary",), VMEM_LIMIT),
    )(*in_arrays, *[c[0] for c in cts], *extra)
    return res[:n_r], res[n_r:]


def _conv_shift(ext, k, tm, forward):
    s = CONV_W - 1 - k
    if forward:
        rolled = ext if s == 0 else pltpu.roll(ext, s, 0)
        return rolled[SUBLANE:, :]
    rolled = ext if s == 0 else pltpu.roll(ext, tm + SUBLANE - s, 0)
    return rolled[:tm, :]


def _conv_pre(x_ref, halo_ref, w_ref, b_ref, first, tm):
    halo = jnp.where(first, 0.0, halo_ref[...])
    ext = jnp.concatenate([halo, x_ref[...]], axis=0)
    w = w_ref[...]
    pre = b_ref[...] + jnp.zeros_like(x_ref[...])
    taps = []
    for k in range(CONV_W):
        tap = _conv_shift(ext, k, tm, True)
        taps.append(tap)
        pre = pre + tap * w[k:k + 1, :]
    return pre, taps


def _conv_fwd(proj, w, b, length, tm, tc):
    hb = tm // SUBLANE
    j0 = CONV_OFF // tc

    def body(x_ref, halo_ref, w_ref, b_ref, o_ref):
        pre, _ = _conv_pre(x_ref, halo_ref, w_ref, b_ref, pl.program_id(0) == 0, tm)
        o_ref[...] = _silu(pre)

    return pl.pallas_call(
        body, name="conv_fwd", grid=(length // tm, CONV_ALL // tc),
        in_specs=[pl.BlockSpec((tm, tc), lambda i, j: (i, j0 + j)),
                  pl.BlockSpec((SUBLANE, tc), lambda i, j: (jnp.maximum(i * hb - 1, 0), j0 + j)),
                  pl.BlockSpec((CONV_W, tc), lambda i, j: (0, j)),
                  pl.BlockSpec((1, tc), lambda i, j: (0, j))],
        out_specs=pl.BlockSpec((tm, tc), lambda i, j: (i, j)),
        out_shape=jax.ShapeDtypeStruct((length, CONV_ALL), F32),
        compiler_params=_cparams(("parallel", "parallel"), VMEM_LIMIT),
    )(proj, proj, w, b)


def _conv_bwd_pre(proj, w, b, dxc, length, tm, tc):
    hb = tm // SUBLANE
    j0 = CONV_OFF // tc

    def body(x_ref, halo_ref, w_ref, b_ref, dy_ref, dpre_ref, dw_ref, db_ref):
        i = pl.program_id(1)
        pre, taps = _conv_pre(x_ref, halo_ref, w_ref, b_ref, i == 0, tm)
        sg = jax.nn.sigmoid(pre)
        dpre = dy_ref[...] * (sg * (1.0 + pre * (1.0 - sg)))
        dpre_ref[...] = dpre

        @pl.when(i == 0)
        def _():
            dw_ref[...] = jnp.zeros_like(dw_ref)
            db_ref[...] = jnp.zeros_like(db_ref)

        for k in range(CONV_W):
            dw_ref[k:k + 1, :] += jnp.sum(dpre * taps[k], axis=0, keepdims=True)
        db_ref[...] += jnp.sum(dpre, axis=0, keepdims=True)

    return pl.pallas_call(
        body, name="conv_bwd_pre", grid=(CONV_ALL // tc, length // tm),
        in_specs=[pl.BlockSpec((tm, tc), lambda j, i: (i, j0 + j)),
                  pl.BlockSpec((SUBLANE, tc), lambda j, i: (jnp.maximum(i * hb - 1, 0), j0 + j)),
                  pl.BlockSpec((CONV_W, tc), lambda j, i: (0, j)),
                  pl.BlockSpec((1, tc), lambda j, i: (0, j)),
                  pl.BlockSpec((tm, tc), lambda j, i: (i, j))],
        out_specs=[pl.BlockSpec((tm, tc), lambda j, i: (i, j)),
                   pl.BlockSpec((CONV_W, tc), lambda j, i: (0, j)),
                   pl.BlockSpec((1, tc), lambda j, i: (0, j))],
        out_shape=[jax.ShapeDtypeStruct((length, CONV_ALL), F32),
                   jax.ShapeDtypeStruct((CONV_W, CONV_ALL), F32),
                   jax.ShapeDtypeStruct((1, CONV_ALL), F32)],
        compiler_params=_cparams(("parallel", "arbitrary"), VMEM_LIMIT),
    )(proj, proj, w, b, dxc)


def _conv_bwd_x(dpre, w, dproj, length, tm, tc):
    hb = tm // SUBLANE
    n_t = length // tm
    last_blk = length // SUBLANE - 1
    j0 = CONV_OFF // tc

    def body(d_ref, halo_ref, w_ref, _, o_ref):
        halo = jnp.where(pl.program_id(0) == n_t - 1, 0.0, halo_ref[...])
        ext = jnp.concatenate([d_ref[...], halo], axis=0)
        w = w_ref[...]
        acc = jnp.zeros_like(d_ref[...])
        for k in range(CONV_W):
            acc = acc + _conv_shift(ext, k, tm, False) * w[k:k + 1, :]
        o_ref[...] = acc.astype(o_ref.dtype)

    return pl.pallas_call(
        body, name="conv_bwd_x", grid=(n_t, CONV_ALL // tc),
        in_specs=[pl.BlockSpec((tm, tc), lambda i, j: (i, j)),
                  pl.BlockSpec((SUBLANE, tc), lambda i, j: (jnp.minimum((i + 1) * hb, last_blk), j)),
                  pl.BlockSpec((CONV_W, tc), lambda i, j: (0, j)),
                  pl.BlockSpec(memory_space=pl.ANY)],
        out_specs=pl.BlockSpec((tm, tc), lambda i, j: (i, j0 + j)),
        out_shape=jax.ShapeDtypeStruct(dproj.shape, dproj.dtype), input_output_aliases={3: 0},
        compiler_params=_cparams(("parallel", "parallel"), VMEM_LIMIT),
    )(dpre, dpre, w, dproj)


HEAD_SCAN_CHUNKS_PER_STEP = 4
assert GDN_HEADS == RET_HEADS


def _scan_steps(length):
    nc = length // CHUNK
    gs = HEAD_SCAN_CHUNKS_PER_STEP if nc % HEAD_SCAN_CHUNKS_PER_STEP == 0 else 1
    return gs, nc // gs, gs * CHUNK


def _head_tiles(ref, gs):
    return [[ref[j * CHUNK:(j + 1) * CHUNK, h * LANE:(h + 1) * LANE] for h in range(GDN_HEADS)] for j in range(gs)]


def _chunk_rows(ref, gs):
    return [ref[j * CHUNK:(j + 1) * CHUNK, :] for j in range(gs)]


def _gdn_fwd(xc, proj, alog, dtb, nw, length, comm=None):
    gs, steps, rows = _scan_steps(length)
    h_ = GDN_HEADS
    hw = h_ * LANE
    tn_ = h_ * CHUNK
    n_comm = len(comm[1]) if comm else 0

    def body(*refs):
        q_ref, k_ref, v_ref, gz_ref, sm_ref, al_ref, dt_ref, nw_ref = refs[:8]
        c_in = refs[8:8 + n_comm]
        on_ref, st_ref, t_ref = refs[8 + n_comm:11 + n_comm]
        c_out = refs[11 + n_comm:11 + 2 * n_comm]
        s_ref = refs[11 + 2 * n_comm]
        sems = refs[12 + 2 * n_comm:]

        @pl.when(pl.program_id(0) == 0)
        def _():
            s_ref[...] = jnp.zeros_like(s_ref)
            if comm:
                _comm_plan(comm[0], c_in, c_out, *sems)[0]()

        st_ref[...] = s_ref[...]
        lead = lambda ref: [ref[h] for h in range(h_)]
        on, s_new, t = _gdn_chunk(_head_tiles(q_ref, gs), _head_tiles(k_ref, gs), _head_tiles(v_ref, gs),
                                  _head_tiles(gz_ref, gs), _chunk_rows(sm_ref, gs), al_ref[...], dt_ref[...],
                                  nw_ref[...], lead(s_ref))
        for j in range(gs):
            t_ref[j] = t[j]
        for h in range(h_):
            for j in range(gs):
                on_ref[j * CHUNK:(j + 1) * CHUNK, h * LANE:(h + 1) * LANE] = on[j][h].astype(on_ref.dtype)
            s_ref[h] = s_new[h]

        if comm:
            @pl.when(pl.program_id(0) == steps - 1)
            def _():
                _comm_plan(comm[0], c_in, c_out, *sems)[1]()

    blk = lambda col: pl.BlockSpec((rows, hw), lambda c: (c, col // hw))
    one = pl.BlockSpec((1, LANE), lambda c: (0, 0))
    hbm = pl.BlockSpec(memory_space=pl.ANY)
    return pl.pallas_call(
        body, name="gdn_fwd_" + comm[0] if comm else "gdn_fwd", grid=(steps,),
        in_specs=[blk(XC("gq")), blk(XC("gk")), blk(XC("gv")), blk(_MINE["gz"]),
                  pl.BlockSpec((rows, LANE), lambda c: (c, SMALL_OFF // LANE)), one, one,
                  pl.BlockSpec((1, LANE), lambda c: (0, 0))] + [hbm] * n_comm,
        out_specs=[pl.BlockSpec((rows, hw), lambda c: (c, 0)),
                   pl.BlockSpec((h_, None, LANE, LANE), lambda c: (0, c, 0, 0)),
                   pl.BlockSpec((gs, tn_, tn_), lambda c: (c, 0, 0))] + [hbm] * n_comm,
        out_shape=[jax.ShapeDtypeStruct((length, MIX_W), BF16),
                   jax.ShapeDtypeStruct((h_, steps, LANE, LANE), F32),
                   jax.ShapeDtypeStruct((length // CHUNK, tn_, tn_), F32)]
                  + (_comm_out_shapes(*comm) if comm else []),
        scratch_shapes=[pltpu.VMEM((h_, LANE, LANE), F32)] + (_comm_scratch(n_comm) if comm else []),
        compiler_params=_cparams(("arbitrary",), VMEM_LIMIT),
    )(xc, xc, xc, proj, proj, alog, dtb, nw, *(comm[1] if comm else []))


def _gdn_bwd(xc, proj, alog, dtb, nw, st, tinv, dmix, length, comm=None):
    gs, steps, rows = _scan_steps(length)
    h_ = GDN_HEADS
    hw = h_ * LANE
    tn_ = h_ * CHUNK
    n_comm = len(comm[1]) if comm else 0
    n_in, n_out = 11, 6

    def body(*refs):
        q_ref, k_ref, v_ref, gz_ref, sm_ref, al_ref, dt_ref, nw_ref, st_ref, t_ref, do_ref = refs[:n_in]
        c_in = refs[n_in:n_in + n_comm]
        o0 = n_in + n_comm
        dqkv_ref, dgz_ref, dsm_ref, dal_ref, ddt_ref, dnw_ref = refs[o0:o0 + n_out]
        c_out = refs[o0 + n_out:o0 + n_out + n_comm]
        ds_ref = refs[o0 + n_out + n_comm]
        sems = refs[o0 + n_out + n_comm + 1:]

        @pl.when(pl.program_id(0) == 0)
        def _():
            ds_ref[...] = jnp.zeros_like(ds_ref)
            dal_ref[...] = jnp.zeros_like(dal_ref)
            ddt_ref[...] = jnp.zeros_like(ddt_ref)
            dnw_ref[...] = jnp.zeros_like(dnw_ref)
            if comm:
                _comm_plan(comm[0], c_in, c_out, *sems)[0]()

        lead = lambda ref: [ref[h] for h in range(h_)]
        ins = (_head_tiles(q_ref, gs), _head_tiles(k_ref, gs), _head_tiles(v_ref, gs), _head_tiles(gz_ref, gs),
               _chunk_rows(sm_ref, gs), al_ref[...], dt_ref[...], nw_ref[...], lead(st_ref))
        t_saved = [t_ref[j] for j in range(gs)]
        _, vjp = jax.vjp(lambda *a: _gdn_chunk(*a, t_saved=t_saved)[:2], *ins)
        dq, dk, dv, dgz, dsm, dal, ddt, dnw, ds = vjp((_head_tiles(do_ref, gs), lead(ds_ref)))
        for j in range(gs):
            rws = slice(j * CHUNK, (j + 1) * CHUNK)
            for h in range(h_):
                cols = slice(h * LANE, (h + 1) * LANE)
                for part, d in enumerate((dq, dk, dv)):
                    dqkv_ref[rws, part * hw + h * LANE:part * hw + (h + 1) * LANE] = d[j][h]
                dgz_ref[rws, cols] = dgz[j][h].astype(dgz_ref.dtype)
            dsm_ref[rws, :] = dsm[j]
        for h in range(h_):
            ds_ref[h] = ds[h]
        dal_ref[...] += dal
        ddt_ref[...] += ddt
        dnw_ref[...] += dnw

        if comm:
            @pl.when(pl.program_id(0) == steps - 1)
            def _():
                _comm_plan(comm[0], c_in, c_out, *sems)[1]()

    r = lambda c: steps - 1 - c
    blk = lambda col: pl.BlockSpec((rows, hw), lambda c: (r(c), col // hw))
    one = pl.BlockSpec((1, LANE), lambda c: (0, 0))
    hbm = pl.BlockSpec(memory_space=pl.ANY)
    return pl.pallas_call(
        body, name="gdn_bwd_" + comm[0] if comm else "gdn_bwd", grid=(steps,),
        in_specs=[blk(XC("gq")), blk(XC("gk")), blk(XC("gv")), blk(_MINE["gz"]),
                  pl.BlockSpec((rows, LANE), lambda c: (r(c), SMALL_OFF // LANE)), one, one,
                  pl.BlockSpec((1, LANE), lambda c: (0, 0)),
                  pl.BlockSpec((h_, None, LANE, LANE), lambda c: (0, r(c), 0, 0)),
                  pl.BlockSpec((gs, tn_, tn_), lambda c: (r(c), 0, 0)),
                  blk(0)] + [hbm] * n_comm,
        out_specs=[pl.BlockSpec((rows, 3 * hw), lambda c: (r(c), 0)), blk(_MINE["gz"]),
                   pl.BlockSpec((rows, LANE), lambda c: (r(c), 0)), one, one,
                   pl.BlockSpec((1, LANE), lambda c: (0, 0))] + [hbm] * n_comm,
        out_shape=[jax.ShapeDtypeStruct((length, CONV_ALL), F32), jax.ShapeDtypeStruct((length, N_PAD), BF16),
                   jax.ShapeDtypeStruct((length, LANE), F32),
                   jax.ShapeDtypeStruct((1, LANE), F32), jax.ShapeDtypeStruct((1, LANE), F32),
                   jax.ShapeDtypeStruct((1, LANE), F32)] + (_comm_out_shapes(*comm) if comm else []),
        scratch_shapes=[pltpu.VMEM((h_, LANE, LANE), F32)] + (_comm_scratch(n_comm) if comm else []),
        compiler_params=_cparams(("arbitrary",), VMEM_LIMIT),
    )(xc, xc, xc, proj, proj, alog, dtb, nw, st, tinv, dmix, *(comm[1] if comm else []))


SSD_CHUNKS_PER_STEP = 8


def _ssd_steps(length):
    nc = length // CHUNK
    gs = SSD_CHUNKS_PER_STEP if nc % SSD_CHUNKS_PER_STEP == 0 else 1
    return gs, nc // gs, gs * CHUNK


def _ssd_lanes(g):
    base = _MINE["sdt"] - SMALL_OFF
    return [[base + (g * PAIRS_PER_GROUP + p) * 2 + j for j in range(2)] for p in range(PAIRS_PER_GROUP)]


def _ssd_tiles(ref, gs):
    return [[ref[j * CHUNK:(j + 1) * CHUNK, p * LANE:(p + 1) * LANE] for p in range(PAIRS_PER_GROUP)]
            for j in range(gs)]


def _ssd_gate(y, z, w):
    return _rms(y * _silu(z), w)


def _ssd_fwd(xc, proj, alog_row, dtb_row, d_row, norm_w, mixed, length):
    gs, steps, rows = _ssd_steps(length)
    ppg = PAIRS_PER_GROUP
    gw = ppg * LANE

    def body(x_ref, b_ref, c_ref, sm_ref, al_ref, db_ref, dp_ref, z_ref, nw_ref, _, y_ref, o_ref, st_ref, hs_ref):
        @pl.when(pl.program_id(1) == 0)
        def _():
            hs_ref[...] = jnp.zeros_like(hs_ref)

        st_ref[...] = hs_ref[...]
        y, hs_new = _ssd_chunk(_ssd_lanes(pl.program_id(0)), _ssd_tiles(x_ref, gs), _chunk_rows(b_ref, gs),
                               _chunk_rows(c_ref, gs), _chunk_rows(sm_ref, gs), al_ref[...], db_ref[...],
                               dp_ref[...], [hs_ref[p] for p in range(ppg)])
        for p in range(ppg):
            hs_ref[p] = hs_new[p]
        for j in range(gs):
            rws = slice(j * CHUNK, (j + 1) * CHUNK)
            y_j = jnp.concatenate(y[j], axis=1)
            y_ref[rws, :] = y_j
            o_ref[rws, :] = _ssd_gate(y_j, z_ref[rws, :], nw_ref[...]).astype(o_ref.dtype)

    one = pl.BlockSpec((1, LANE), lambda g, c: (0, 0))
    return pl.pallas_call(
        body, name="ssd_fwd", grid=(SSD_GROUPS, steps),
        in_specs=[pl.BlockSpec((rows, gw), lambda g, c: (c, XC("sx") // gw + g)),
                  pl.BlockSpec((rows, LANE), lambda g, c: (c, XC("sB") // LANE + g)),
                  pl.BlockSpec((rows, LANE), lambda g, c: (c, XC("sC") // LANE + g)),
                  pl.BlockSpec((rows, LANE), lambda g, c: (c, SMALL_OFF // LANE)),
                  one, one, one,
                  pl.BlockSpec((rows, gw), lambda g, c: (c, _MINE["sz"] // gw + g)),
                  pl.BlockSpec((None, 1, gw), lambda g, c: (g, 0, 0)),
                  pl.BlockSpec(memory_space=pl.ANY)],
        out_specs=[pl.BlockSpec((rows, gw), lambda g, c: (c, g)),
                   pl.BlockSpec((rows, gw), lambda g, c: (c, 1 + g)),
                   pl.BlockSpec((None, ppg, None, LANE, SSD_N), lambda g, c: (g, 0, c, 0, 0))],
        out_shape=[jax.ShapeDtypeStruct((length, SSD_GROUPS * gw), F32),
                   jax.ShapeDtypeStruct(mixed.shape, mixed.dtype),
                   jax.ShapeDtypeStruct((SSD_GROUPS, ppg, steps, LANE, SSD_N), F32)],
        input_output_aliases={9: 1},
        scratch_shapes=[pltpu.VMEM((ppg, LANE, SSD_N), F32)],
        compiler_params=_cparams(("parallel", "arbitrary"), VMEM_LIMIT),
    )(xc, xc, xc, proj, alog_row, dtb_row, d_row, proj, norm_w, mixed)


def _ssd_bwd(xc, proj, alog_row, dtb_row, d_row, norm_w, st, y_b, dmix, dproj, dxc, length):
    gs, steps, rows = _ssd_steps(length)
    ppg = PAIRS_PER_GROUP
    gw = ppg * LANE

    def body(x_ref, b_ref, c_ref, sm_ref, al_ref, db_ref, dp_ref, z_ref, nw_ref, st_ref, y_ref, do_ref, _, __,
             dxc_ref, dz_ref, dsm_ref, dal_ref, ddb_ref, ddp_ref, dnw_ref, dhs_ref):
        g = pl.program_id(1)

        @pl.when(jnp.logical_and(pl.program_id(0) == 0, g == 0))
        def _():
            dhs_ref[...] = jnp.zeros_like(dhs_ref)
            dal_ref[...] = jnp.zeros_like(dal_ref)
            ddb_ref[...] = jnp.zeros_like(ddb_ref)
            ddp_ref[...] = jnp.zeros_like(ddp_ref)
            dnw_ref[...] = jnp.zeros_like(dnw_ref)

        pairs = range(ppg)
        dy, dnw = [], jnp.zeros((1, gw), F32)
        for j in range(gs):
            rws = slice(j * CHUNK, (j + 1) * CHUNK)
            _, gate_vjp = jax.vjp(_ssd_gate, y_ref[rws, :], z_ref[rws, :], nw_ref[...])
            dy_j, dz_j, dnw_j = gate_vjp(do_ref[rws, :].astype(F32))
            dz_ref[rws, :] = dz_j.astype(dz_ref.dtype)
            dy.append([dy_j[:, p * LANE:(p + 1) * LANE] for p in pairs])
            dnw = dnw + dnw_j
        dnw_ref[g] += dnw
        ins = (_ssd_tiles(x_ref, gs), _chunk_rows(b_ref, gs), _chunk_rows(c_ref, gs), _chunk_rows(sm_ref, gs),
               al_ref[...], db_ref[...], dp_ref[...], [st_ref[p] for p in pairs])
        _, vjp = jax.vjp(functools.partial(_ssd_chunk, _ssd_lanes(g)), *ins)
        dx, dbm, dcm, dsm, dal, ddb, ddp, dhs = vjp((dy, [dhs_ref[g, p] for p in pairs]))
        for grp in range(SSD_GROUPS):
            @pl.when(g == grp)
            def _(grp=grp):
                for j in range(gs):
                    rws = slice(j * CHUNK, (j + 1) * CHUNK)
                    for p in pairs:
                        col = grp * gw + p * LANE
                        dxc_ref[rws, col:col + LANE] = dx[j][p]
                    cb_ = XC("sB") - XC("sx") + grp * SSD_N
                    cc_ = XC("sC") - XC("sx") + grp * SSD_N
                    dxc_ref[rws, cb_:cb_ + SSD_N] = dbm[j]
                    dxc_ref[rws, cc_:cc_ + SSD_N] = dcm[j]
        for j in range(gs):
            dsm_ref[j * CHUNK:(j + 1) * CHUNK, :] = dsm[j]
        for p in pairs:
            dhs_ref[g, p] = dhs[p]
        dal_ref[g] += dal
        ddb_ref[g] += ddb
        ddp_ref[g] += ddp

    r = lambda c: steps - 1 - c
    one = pl.BlockSpec((1, LANE), lambda c, g: (0, 0))
    acc = pl.BlockSpec((SSD_GROUPS, 1, LANE), lambda c, g: (0, 0, 0))
    prm = jax.ShapeDtypeStruct((SSD_GROUPS, 1, LANE), F32)
    sz_blk = _MINE["sz"] // gw
    return pl.pallas_call(
        body, name="ssd_bwd", grid=(steps, SSD_GROUPS),
        in_specs=[pl.BlockSpec((rows, gw), lambda c, g: (r(c), XC("sx") // gw + g)),
                  pl.BlockSpec((rows, LANE), lambda c, g: (r(c), XC("sB") // LANE + g)),
                  pl.BlockSpec((rows, LANE), lambda c, g: (r(c), XC("sC") // LANE + g)),
                  pl.BlockSpec((rows, LANE), lambda c, g: (r(c), SMALL_OFF // LANE)),
                  one, one, one,
                  pl.BlockSpec((rows, gw), lambda c, g: (r(c), sz_blk + g)),
                  pl.BlockSpec((None, 1, gw), lambda c, g: (g, 0, 0)),
                  pl.BlockSpec((None, ppg, None, LANE, SSD_N), lambda c, g: (g, 0, r(c), 0, 0)),
                  pl.BlockSpec((rows, gw), lambda c, g: (r(c), g)),
                  pl.BlockSpec((rows, gw), lambda c, g: (r(c), 1 + g)),
                  pl.BlockSpec(memory_space=pl.ANY), pl.BlockSpec(memory_space=pl.ANY)],
        out_specs=[pl.BlockSpec((rows, CONV_CH), lambda c, g: (r(c), 1)),
                   pl.BlockSpec((rows, gw), lambda c, g: (r(c), sz_blk + g)),
                   pl.BlockSpec((None, rows, LANE), lambda c, g: (g, r(c), 0)), acc, acc, acc,
                   pl.BlockSpec((SSD_GROUPS, 1, gw), lambda c, g: (0, 0, 0))],
        out_shape=[jax.ShapeDtypeStruct(dxc.shape, dxc.dtype), jax.ShapeDtypeStruct(dproj.shape, dproj.dtype),
                   jax.ShapeDtypeStruct((SSD_GROUPS, length, LANE), F32), prm, prm, prm,
                   jax.ShapeDtypeStruct((SSD_GROUPS, 1, gw), F32)],
        input_output_aliases={12: 1, 13: 0},
        scratch_shapes=[pltpu.VMEM((SSD_GROUPS, ppg, LANE, SSD_N), F32)],
        compiler_params=_cparams(("arbitrary", "arbitrary"), VMEM_LIMIT),
    )(xc, xc, xc, proj, alog_row, dtb_row, d_row, proj, norm_w, st, y_b, dmix, dproj, dxc)


def _ret_fwd(proj, cos2, sin2, dmat, qdec, kdec, cdec, nw, mixed, length):
    gs, steps, rows = _scan_steps(length)
    h_ = RET_HEADS

    def body(q_ref, k_ref, v_ref, g_ref, cos_ref, sin_ref, dm_ref, qd_ref, kd_ref, cd_ref, nw_ref, _,
             on_ref, st_ref, r_ref):
        @pl.when(pl.program_id(0) == 0)
        def _():
            r_ref[...] = jnp.zeros_like(r_ref)

        st_ref[...] = r_ref[...]
        lead = lambda ref: [ref[h] for h in range(h_)]
        on, r_new = _ret_chunk(_head_tiles(q_ref, gs), _head_tiles(k_ref, gs), _head_tiles(v_ref, gs),
                               _head_tiles(g_ref, gs), _chunk_rows(cos_ref, gs), _chunk_rows(sin_ref, gs),
                               lead(dm_ref), lead(qd_ref), lead(kd_ref), lead(cd_ref), nw_ref[...], lead(r_ref))
        for h in range(h_):
            for j in range(gs):
                on_ref[j * CHUNK:(j + 1) * CHUNK, h * LANE:(h + 1) * LANE] = on[j][h].astype(on_ref.dtype)
            r_ref[h] = r_new[h]

    hw = h_ * LANE
    blk = lambda name: pl.BlockSpec((rows, hw), lambda c: (c, _MINE[name] // hw))
    tab = pl.BlockSpec((rows, LANE), lambda c: (c, 0))
    full = lambda *s: pl.BlockSpec(s, lambda c: (0,) * len(s))
    return pl.pallas_call(
        body, name="ret_fwd", grid=(steps,),
        in_specs=[blk("rq"), blk("rk"), blk("rv"), blk("rg"), tab, tab,
                  full(h_, CHUNK, CHUNK), full(h_, CHUNK, 1), full(h_, CHUNK, 1), full(h_, 1, 1), full(1, LANE),
                  pl.BlockSpec(memory_space=pl.ANY)],
        out_specs=[pl.BlockSpec((rows, hw), lambda c: (c, (MIX_W - hw) // hw)),
                   pl.BlockSpec((h_, None, LANE, LANE), lambda c: (0, c, 0, 0))],
        out_shape=[jax.ShapeDtypeStruct(mixed.shape, mixed.dtype),
                   jax.ShapeDtypeStruct((h_, steps, LANE, LANE), F32)],
        input_output_aliases={11: 0},
        scratch_shapes=[pltpu.VMEM((h_, LANE, LANE), F32)],
        compiler_params=_cparams(("arbitrary",), VMEM_LIMIT),
    )(proj, proj, proj, proj, cos2, sin2, dmat, qdec, kdec, cdec, nw, mixed)


def _ret_bwd(proj, cos2, sin2, dmat, qdec, kdec, cdec, nw, st, dmix, dproj, length):
    gs, steps, rows = _scan_steps(length)
    h_ = RET_HEADS
    hw = h_ * LANE

    def body(q_ref, k_ref, v_ref, g_ref, cos_ref, sin_ref, dm_ref, qd_ref, kd_ref, cd_ref, nw_ref, st_ref,
             do_ref, _, dqkvg_ref, dnw_ref, dr_ref):
        @pl.when(pl.program_id(0) == 0)
        def _():
            dr_ref[...] = jnp.zeros_like(dr_ref)
            dnw_ref[...] = jnp.zeros_like(dnw_ref)

        lead = lambda ref: [ref[h] for h in range(h_)]
        consts = (_chunk_rows(cos_ref, gs), _chunk_rows(sin_ref, gs), lead(dm_ref), lead(qd_ref), lead(kd_ref),
                  lead(cd_ref))
        f = lambda q, k, v, g, w_, r_: _ret_chunk(q, k, v, g, *consts, w_, r_)
        _, vjp = jax.vjp(f, _head_tiles(q_ref, gs), _head_tiles(k_ref, gs), _head_tiles(v_ref, gs),
                         _head_tiles(g_ref, gs), nw_ref[...], lead(st_ref))
        dq, dk, dv, dg, dnw, dr = vjp((_head_tiles(do_ref, gs), lead(dr_ref)))
        for h in range(h_):
            for j in range(gs):
                rws = slice(j * CHUNK, (j + 1) * CHUNK)
                for part, d in enumerate((dq, dk, dv, dg)):
                    col = part * hw + h * LANE
                    dqkvg_ref[rws, col:col + LANE] = d[j][h].astype(dqkvg_ref.dtype)
            dr_ref[h] = dr[h]
        dnw_ref[...] += dnw

    r = lambda c: steps - 1 - c
    blk = lambda name: pl.BlockSpec((rows, hw), lambda c: (r(c), _MINE[name] // hw))
    tab = pl.BlockSpec((rows, LANE), lambda c: (r(c), 0))
    full = lambda *s: pl.BlockSpec(s, lambda c: (0,) * len(s))
    assert _MINE["rq"] % (4 * hw) == 0 and [_MINE[n] - _MINE["rq"] for n in ("rk", "rv", "rg")] == [hw, 2 * hw, 3 * hw]
    return pl.pallas_call(
        body, name="ret_bwd", grid=(steps,),
        in_specs=[blk("rq"), blk("rk"), blk("rv"), blk("rg"), tab, tab,
                  full(h_, CHUNK, CHUNK), full(h_, CHUNK, 1), full(h_, CHUNK, 1), full(h_, 1, 1), full(1, LANE),
                  pl.BlockSpec((h_, None, LANE, LANE), lambda c: (0, r(c), 0, 0)),
                  pl.BlockSpec((rows, hw), lambda c: (r(c), (MIX_W - hw) // hw)),
                  pl.BlockSpec(memory_space=pl.ANY)],
        out_specs=[pl.BlockSpec((rows, 4 * hw), lambda c: (r(c), _MINE["rq"] // (4 * hw))), full(1, LANE)],
        out_shape=[jax.ShapeDtypeStruct(dproj.shape, dproj.dtype), jax.ShapeDtypeStruct((1, LANE), F32)],
        input_output_aliases={13: 0},
        scratch_shapes=[pltpu.VMEM((h_, LANE, LANE), F32)],
        compiler_params=_cparams(("arbitrary",), VMEM_LIMIT),
    )(proj, proj, proj, proj, cos2, sin2, dmat, qdec, kdec, cdec, nw, st, dmix, dproj)


def _mesh_pos():
    x, y, c = lax.axis_index("x"), lax.axis_index("y"), lax.axis_index("c")
    return x, y, c, 4 * x + 2 * y + c


def _peer(x, y, c, mask):
    return (x ^ ((mask >> 2) & 1), y ^ ((mask >> 1) & 1), c ^ (mask & 1))


def _gather_plan(ins, outs, send_sems, recv_sems, local_sems):
    x, y, c, me = _mesh_pos()
    n = len(ins)
    idx = lambda px, py, pc: 4 * px + 2 * py + pc
    sib = (x, y, 1 - c)
    chips = [(1 - x, y), (x, 1 - y), (1 - x, 1 - y)]
    local = [pltpu.make_async_copy(ins[a], outs[a].at[me], local_sems.at[a]) for a in range(n)]

    def copy(a, k, src, slab, to):
        return pltpu.make_async_remote_copy(
            src_ref=src, dst_ref=outs[a].at[slab], send_sem=send_sems.at[7 * a + k],
            recv_sem=recv_sems.at[7 * a + k], device_id=to, device_id_type=pl.DeviceIdType.MESH)

    first = [copy(a, 0, ins[a], me, sib) for a in range(n)]
    first += [copy(a, 1 + j, ins[a], me, (*chip, c)) for a in range(n) for j, chip in enumerate(chips)]
    landed = [[copy(a, 1 + j, ins[a], idx(*chip, c), (*chip, c)) for a in range(n)] for j, chip in enumerate(chips)]
    passed = [[copy(a, 4 + j, outs[a].at[idx(*chip, c)], idx(*chip, c), sib) for a in range(n)]
              for j, chip in enumerate(chips)]
    from_sib = [copy(a, 0, ins[a], idx(*sib), sib) for a in range(n)]
    from_sib += [copy(a, 4 + j, ins[a], idx(*chip, 1 - c), sib) for a in range(n) for j, chip in enumerate(chips)]

    def start():
        for cp in local + first:
            cp.start()

    def wait():
        for j in range(len(chips)):
            for a in range(n):
                landed[j][a].wait_recv()
                passed[j][a].start()
        for cp in from_sib:
            cp.wait_recv()
        for cp in first + [cp for row in passed for cp in row]:
            cp.wait_send()
        for cp in local:
            cp.wait()

    return start, wait


def _comm_plan(kind, ins, outs, send_sems, recv_sems, local_sems):
    if kind == "gather":
        return _gather_plan(ins, outs, send_sems, recv_sems, local_sems)
    x, y, c, me = _mesh_pos()
    n = len(ins)
    src = (lambda a, idx: ins[a]) if kind == "gather" else (lambda a, idx: ins[a].at[idx])
    local = [pltpu.make_async_copy(src(a, me), outs[a].at[me], local_sems.at[a]) for a in range(n)]
    sends, recvs = [], []
    for mask in range(1, N_DEV):
        px, py, pc = _peer(x, y, c, mask)
        pidx = 4 * px + 2 * py + pc
        for a in range(n):
            k = (mask - 1) * n + a
            copy = lambda s, d, k=k: pltpu.make_async_remote_copy(
                src_ref=s, dst_ref=d, send_sem=send_sems.at[k], recv_sem=recv_sems.at[k],
                device_id=(px, py, pc), device_id_type=pl.DeviceIdType.MESH)
            sends.append(copy(src(a, pidx), outs[a].at[me]))
            recvs.append(copy(src(a, me), outs[a].at[pidx]))

    def start():
        for cp in local + sends:
            cp.start()

    def wait():
        for cp in sends:
            cp.wait_send()
        for cp in recvs:
            cp.wait_recv()
        for cp in local:
            cp.wait()

    return start, wait


def _comm_out_shapes(kind, arrays):
    return [jax.ShapeDtypeStruct(((N_DEV,) + s.shape) if kind == "gather" else s.shape, s.dtype) for s in arrays]


def _comm_scratch(n):
    k_tot = (N_DEV - 1) * n
    return [pltpu.SemaphoreType.DMA((k_tot,)), pltpu.SemaphoreType.DMA((k_tot,)), pltpu.SemaphoreType.DMA((n,))]


def _comm_call(name, kind, arrays):
    n = len(arrays)

    def body(*refs):
        start, wait = _comm_plan(kind, refs[:n], refs[n:2 * n], *refs[2 * n:])
        start()
        wait()

    hbm = pl.BlockSpec(memory_space=pl.ANY)
    return pl.pallas_call(
        body, name=name, in_specs=[hbm] * n, out_specs=[hbm] * n, out_shape=_comm_out_shapes(kind, arrays),
        scratch_shapes=_comm_scratch(n), compiler_params=pltpu.CompilerParams(has_side_effects=True),
    )(*arrays)


def _adam(name, parts, w, m, v, tr, layer=None, prev=None, tc=None):
    p_, r_, cw = parts.shape
    tc = tc or cw
    c1 = 1.0 - ADAM_B1 ** ADAM_STEP
    c2 = 1.0 - ADAM_B2 ** ADAM_STEP
    n_prev = 4 if prev else 0

    def body(p_ref, w_ref, m_ref, v_ref, *rest):
        g_ref, d_ref, nm_ref, nv_ref = rest[n_prev:]
        g = p_ref[0].astype(F32)
        for i in range(1, p_):
            g = g + p_ref[i].astype(F32)
        nm = ADAM_B1 * m_ref[...] + (1.0 - ADAM_B1) * g
        nv = ADAM_B2 * v_ref[...] + (1.0 - ADAM_B2) * (g * g)
        d_ref[...] = -ADAM_LR * ((nm / c1) / (jnp.sqrt(nv / c2) + ADAM_EPS) + ADAM_WD * w_ref[...])
        g_ref[...] = g
        nm_ref[...] = nm
        nv_ref[...] = nv

    if layer is None:
        spec = pl.BlockSpec((tr, tc), lambda i, j: (i, j))
        shp = jax.ShapeDtypeStruct((r_, cw), F32)
    else:
        spec = pl.BlockSpec((None, tr, tc), lambda i, j: (layer, i, j))
        shp = jax.ShapeDtypeStruct((DEPTH, r_, cw), F32)
    return pl.pallas_call(
        body, name=name, grid=(r_ // tr, cw // tc),
        in_specs=[pl.BlockSpec((p_, tr, tc), lambda i, j: (0, i, j)), spec, spec, spec]
                 + [pl.BlockSpec(memory_space=pl.ANY)] * n_prev,
        out_specs=[spec] * 4, out_shape=[shp] * 4,
        input_output_aliases={4 + i: i for i in range(n_prev)},
        compiler_params=_cparams(("parallel", "parallel"), VMEM_LIMIT),
    )(parts, w, m, v, *(prev or ()))


def _shards_to_perm(g_in):
    parts = []
    for n in _ORDER:
        lo, hi = _ORIG[n][0], _ORIG[n][0] + _ORIG[n][1]
        for j in range(lo // SHARD_IN, (hi - 1) // SHARD_IN + 1):
            a, b = max(lo, j * SHARD_IN), min(hi, (j + 1) * SHARD_IN)
            parts.append(g_in[j][a - j * SHARD_IN:b - j * SHARD_IN])
    parts.append(jnp.zeros((N_PAD - N_USED, g_in.shape[2]), g_in.dtype))
    return jnp.concatenate(parts, axis=0)


def _perm_to_shards(g):
    by_orig = sorted(_ORDER, key=lambda n: _ORIG[n][0])
    slabs = []
    for j in range(N_DEV):
        lo, hi = j * SHARD_IN, (j + 1) * SHARD_IN
        parts = []
        for n in by_orig:
            a, b = max(lo, _ORIG[n][0]), min(hi, _ORIG[n][0] + _ORIG[n][1])
            if a < b:
                parts.append(g[_MINE[n] + a - _ORIG[n][0]:_MINE[n] + b - _ORIG[n][0]])
        slabs.append(jnp.concatenate(parts, axis=0))
    return jnp.stack(slabs, axis=0)


def _ret_consts(length):
    lg = jnp.log(1.0 - 2.0 ** (-5.0 - jnp.arange(RET_HEADS, dtype=F32)))
    idx = jnp.arange(CHUNK, dtype=F32)
    rel = idx[:, None] - idx[None, :]
    dmat = jnp.where(rel[None] >= 0, jnp.exp(jnp.maximum(rel, 0.0)[None] * lg[:, None, None]), 0.0)
    qdec = jnp.exp((idx[:, None] + 1.0) * lg[None, :]).T[:, :, None]
    kdec = jnp.exp((CHUNK - 1.0 - idx)[:, None] * lg[None, :]).T[:, :, None]
    cdec = jnp.exp(CHUNK * lg)[:, None, None]
    half = RET_DK // 2
    inv = ROPE_BASE ** (-jnp.arange(half, dtype=F32) / half)
    ang = jnp.arange(length, dtype=jnp.int32).astype(F32)[:, None] * inv[None, :]
    cos, sin = jnp.cos(ang), jnp.sin(ang)
    return (jnp.concatenate([cos, cos], axis=1), jnp.concatenate([-sin, sin], axis=1), dmat, qdec, kdec, cdec)


def _tiles(length, pref=512):
    return pref if length % pref == 0 else length


def _layer_fwd(x, p, consts, length, last, target, comm=None):
    tm = _tiles(length)
    (h,), _ = _rowmap("pre_norm", lambda x_, w_: ((_rms(x_, w_),), ()), length, tm,
                      [(x, D_MODEL, 0)], [p["pre_norm"]], out_ws=[D_MODEL], out_dtypes=[BF16])
    proj = _matmul("in_proj", h, p["w_in"], "nt", length, N_PAD, D_MODEL)
    xc = _conv_fwd(proj, p["conv_w"], p["conv_b"], length, _tiles(length, 1024), 512)
    mixed, st_a, tinv, *got = _gdn_fwd(xc, proj, p["gdn_A_log"], p["gdn_dt_bias"], p["gdn_norm"], length, comm)
    y_b, mixed, st_b = _ssd_fwd(xc, proj, p["ssd_A_log"], p["ssd_dt_bias"], p["ssd_D"], p["ssd_norm"], mixed, length)
    mixed, st_c = _ret_fwd(proj, *consts, p["ret_norm"], mixed, length)
    w_out = p["w_out"](got) if callable(p["w_out"]) else p["w_out"]
    out = _matmul("out_proj", mixed, w_out, "nn", length, D_MODEL, MIX_W)
    res = dict(x=x, h=h, proj=proj, xc=xc, st_a=st_a, tinv=tinv, st_b=st_b, st_c=st_c, y_b=y_b, mixed=mixed,
               out=out, w_out=w_out)
    if not last:
        (y,), _ = _rowmap("post_norm", lambda x_, o_, w_: ((x_ + _rms(o_, w_),), ()), length, tm,
                          [(x, D_MODEL, 0), (out, D_MODEL, 0)], [p["post_norm"]], out_ws=[D_MODEL])
        return y, res, None, got

    def head(x_, o_, t_, w_):
        normed, post_vjp = jax.vjp(_rms, o_, w_)
        e = x_ + normed - t_
        row = jnp.mean(e * e, axis=-1, keepdims=True)
        loss = 0.5 * jnp.sum(row, axis=0, keepdims=True)
        dy_ = e * (1.0 / D_MODEL)
        d_out, dw = post_vjp(dy_)
        return (dy_, d_out), (loss + jnp.zeros((1, LANE), F32), dw)

    (dy, d_out), (loss, dw_post) = _rowmap(
        "loss_head", head, length, tm, [(x, D_MODEL, 0), (out, D_MODEL, 0), (target, D_MODEL, 0)], [p["post_norm"]],
        out_ws=[D_MODEL, D_MODEL], out_dtypes=[F32, BF16], acc_ws=[LANE, D_MODEL])
    res["post_bwd"] = (d_out, dw_post)
    return None, res, (dy, loss[0, 0]), got


def _layer_bwd(dy, p, res, consts, length, comm_scan=None, comm_dx=None):
    tm = _tiles(length)
    g = {}
    if "post_bwd" in res:
        d_out, g["post_norm"] = res["post_bwd"]
    else:
        (d_out,), (g["post_norm"],) = _rowmap("post_norm_bwd", lambda o_, w_: ((_rms(o_, w_),), ()), length, tm,
                                              [(res["out"], D_MODEL, 0)], [p["post_norm"]], cts=[(dy, D_MODEL, 0)],
                                              d_dtypes=[BF16])
    dmix = _matmul("out_proj_dx", d_out, res["w_out"], "nt", length, MIX_W, D_MODEL)
    g["w_out"] = _matmul("out_proj_dw", res["mixed"], d_out, "tn", MIX_W, D_MODEL, length, out_dtype=BF16)
    comm = comm_scan(g) if comm_scan else None
    proj, xc = res["proj"], res["xc"]
    (dxc, dproj, dsm_a, g["gdn_A_log"], g["gdn_dt_bias"], g["gdn_norm"], *got) = _gdn_bwd(
        xc, proj, p["gdn_A_log"], p["gdn_dt_bias"], p["gdn_norm"], res["st_a"], res["tinv"], dmix, length, comm)
    dxc, dproj, dsm_b, g["ssd_A_log"], g["ssd_dt_bias"], g["ssd_D"], g["ssd_norm"] = _ssd_bwd(
        xc, proj, p["ssd_A_log"], p["ssd_dt_bias"], p["ssd_D"], p["ssd_norm"], res["st_b"], res["y_b"], dmix,
        dproj, dxc, length)
    dproj, g["ret_norm"] = _ret_bwd(proj, *consts, p["ret_norm"], res["st_c"], dmix, dproj, length)
    tmc = _tiles(length, 1024)
    dpre, g["conv_w"], g["conv_b"] = _conv_bwd_pre(proj, p["conv_w"], p["conv_b"], dxc, length, tmc, 512)
    dproj = _conv_bwd_x(dpre, p["conv_w"], dproj, length, tmc, 512)
    pad = SMALL_OFF % 512 + 512 - LANE
    (dproj,), _ = _rowmap("dsmall", lambda a_, b_, c_: ((jnp.concatenate([a_ + b_ + c_, jnp.zeros((a_.shape[0], pad), F32)], axis=1),), ()),
                          length, tm, [(dsm_a, LANE, 0), (dsm_b[0], LANE, 0), (dsm_b[1], LANE, 0)], [],
                          out_ws=[512], out_dtypes=[BF16], place={0: (dproj, N_PAD, SMALL_OFF // 512)})
    g["w_in"] = _matmul("in_proj_dw", dproj, res["h"], "tn", N_PAD, D_MODEL, length, out_dtype=BF16)
    if comm_dx:
        dh, *got_dx = _matmul("in_proj_dx_scatter", dproj, p["w_in"], "nn", length, D_MODEL, N_PAD, comm=comm_dx(g))
    else:
        dh, got_dx = _matmul("in_proj_dx", dproj, p["w_in"], "nn", length, D_MODEL, N_PAD), []
    (dx,), (g["pre_norm"],) = _rowmap(
        "pre_norm_bwd", lambda x_, w_: ((_rms(x_, w_), x_), ()), length, tm,
        [(res["x"], D_MODEL, 0)], [p["pre_norm"]], cts=[(dh, D_MODEL, 0), (dy, D_MODEL, 0)])
    return dx, g, got, got_dx


def _lane_row(vals, piece):
    lo = _MINE[piece] - SMALL_OFF
    return jnp.pad(vals[None], ((0, 0), (lo, LANE - lo - vals.shape[0])))


def _from_lane_row(row, piece, n):
    lo = _MINE[piece] - SMALL_OFF
    return row.reshape(-1, LANE).sum(axis=0)[lo:lo + n]


def _make_layer(l, w_in_perm, w_out_full, conv_full, sw):
    return dict(
        pre_norm=sw["pre_norm"][l][None], post_norm=sw["post_norm"][l][None], w_in=w_in_perm,
        w_out=w_out_full,
        conv_w=jnp.concatenate([conv_full[0], conv_full[1]], axis=1),
        conv_b=jnp.concatenate([jnp.zeros((CONV_CH,), F32), sw["ssd_conv_b"][l]])[None],
        gdn_A_log=_lane_row(sw["gdn_A_log"][l], "ga"), gdn_dt_bias=_lane_row(sw["gdn_dt_bias"][l], "ga"),
        gdn_norm=sw["gdn_norm"][l][None],
        ssd_A_log=_lane_row(sw["ssd_A_log"][l], "sdt"), ssd_dt_bias=_lane_row(sw["ssd_dt_bias"][l], "sdt"),
        ssd_D=_lane_row(sw["ssd_D"][l], "sdt"),
        ssd_norm=sw["ssd_norm"][l].reshape(SSD_GROUPS, 1, -1),
        ret_norm=sw["ret_norm"][l][None])


def _local_step(xb, tgt, layer0, make_layer1, length, fwd_comm=None, comm_scan=None, comm_dx=None):
    consts = _ret_consts(length)
    y0, res0, _, got = _layer_fwd(xb, layer0, consts, length, False, tgt, fwd_comm)
    layer1 = make_layer1(got)
    _, res1, (dy, loss_local), _ = _layer_fwd(y0, layer1, consts, length, True, tgt)
    dy, grads1, _, _ = _layer_bwd(dy, layer1, res1, consts, length)
    dx, grads0, recv_scan, recv_dx = _layer_bwd(
        dy, layer0, res0, consts, length,
        (lambda g0: comm_scan(grads1, g0)) if comm_scan else None, comm_dx)
    return loss_local, dx, [grads0, grads1], recv_scan, recv_dx


_SMALL = ["pre_norm", "post_norm", "gdn_A_log", "gdn_dt_bias", "gdn_norm", "ssd_conv_b", "ssd_A_log",
          "ssd_dt_bias", "ssd_D", "ssd_norm", "ret_norm"]


def _pack_small(arrs):
    rows = []
    for n in _SMALL:
        flat = arrs[n].reshape(-1)
        pad = (-flat.shape[0]) % LANE
        rows.append(jnp.pad(flat, (0, pad)).reshape(-1, LANE))
    out = jnp.concatenate(rows, axis=0)
    return jnp.pad(out, ((0, (-out.shape[0]) % SUBLANE), (0, 0)))


def _unpack_small(packed, like):
    out, r = {}, 0
    for n in _SMALL:
        cnt = like[n].size
        nrow = -(-cnt // LANE)
        out[n] = packed[r:r + nrow].reshape(-1)[:cnt].reshape(like[n].shape)
        r += nrow
    return out


def kernel(x, pre_norm, post_norm, w_in, gdn_conv, gdn_A_log, gdn_dt_bias, gdn_norm, ssd_conv, ssd_conv_b, ssd_A_log, ssd_dt_bias, ssd_D, ssd_norm, ret_norm, w_out, loss_target, m_pre_norm, m_post_norm, m_w_in, m_gdn_conv, m_gdn_A_log, m_gdn_dt_bias, m_gdn_norm, m_ssd_conv, m_ssd_conv_b, m_ssd_A_log, m_ssd_dt_bias, m_ssd_D, m_ssd_norm, m_ret_norm, m_w_out, v_pre_norm, v_post_norm, v_w_in, v_gdn_conv, v_gdn_A_log, v_gdn_dt_bias, v_gdn_norm, v_ssd_conv, v_ssd_conv_b, v_ssd_A_log, v_ssd_dt_bias, v_ssd_D, v_ssd_norm, v_ret_norm, v_w_out):
    length = x.shape[1]
    xb = x[0]
    tgt = loss_target[0]
    small_w = dict(pre_norm=pre_norm, post_norm=post_norm, gdn_A_log=gdn_A_log, gdn_dt_bias=gdn_dt_bias,
                   gdn_norm=gdn_norm, ssd_conv_b=ssd_conv_b, ssd_A_log=ssd_A_log, ssd_dt_bias=ssd_dt_bias,
                   ssd_D=ssd_D, ssd_norm=ssd_norm, ret_norm=ret_norm)
    small_m = dict(pre_norm=m_pre_norm, post_norm=m_post_norm, gdn_A_log=m_gdn_A_log, gdn_dt_bias=m_gdn_dt_bias,
                   gdn_norm=m_gdn_norm, ssd_conv_b=m_ssd_conv_b, ssd_A_log=m_ssd_A_log, ssd_dt_bias=m_ssd_dt_bias,
                   ssd_D=m_ssd_D, ssd_norm=m_ssd_norm, ret_norm=m_ret_norm)
    small_v = dict(pre_norm=v_pre_norm, post_norm=v_post_norm, gdn_A_log=v_gdn_A_log, gdn_dt_bias=v_gdn_dt_bias,
                   gdn_norm=v_gdn_norm, ssd_conv_b=v_ssd_conv_b, ssd_A_log=v_ssd_A_log, ssd_dt_bias=v_ssd_dt_bias,
                   ssd_D=v_ssd_D, ssd_norm=v_ssd_norm, ret_norm=v_ret_norm)

    tr_in = lambda a: jnp.swapaxes(a, 1, 2)
    w_in_b, w_out_b = tr_in(w_in).astype(BF16), w_out.astype(BF16)
    conv_shard = jnp.stack([gdn_conv, ssd_conv], axis=1)
    full_out = lambda g_out: g_out.reshape(MIX_W, D_MODEL)

    def assemble(l, g_in, w_out_full, g_conv):
        conv_full = g_conv.transpose(1, 2, 0, 3).reshape(2, CONV_W, CONV_CH)
        return _make_layer(l, _shards_to_perm(g_in), w_out_full, conv_full, small_w)

    slab_in = lambda g: _perm_to_shards(g["w_in"]).astype(BF16)
    slab_out = lambda g: g["w_out"].reshape(N_DEV, SHARD_OUT, D_MODEL).astype(BF16)
    slab_conv = lambda g: jnp.stack(
        [g["conv_w"][:, k * CONV_CH:(k + 1) * CONV_CH].reshape(CONV_W, N_DEV, SHARD_CONV).transpose(1, 0, 2)
         for k in range(2)], axis=1).reshape(N_DEV, 2 * CONV_W, SHARD_CONV)

    g_in0, g_conv0 = _comm_call("gather_layer0", "gather", [w_in_b[0], conv_shard[0]])
    layer0 = assemble(0, g_in0, lambda got: full_out(got[3]), g_conv0)
    loss_local, dx, grads, recv_scan, recv_dx = _local_step(
        xb, tgt, layer0, lambda got: assemble(1, got[0], full_out(got[1]), got[2]), length,
        fwd_comm=("gather", [w_in_b[1], w_out_b[1], conv_shard[1], w_out_b[0]]),
        comm_scan=lambda g1, g0: ("scatter", [slab_in(g1), slab_out(g1), slab_conv(g1), slab_out(g0)]),
        comm_dx=lambda g0: ("scatter", [slab_in(g0), slab_conv(g0)]))
    grad_x = dx[None]
    loss = lax.psum(loss_local, ("x", "y", "c"))

    small_g = dict(
        pre_norm=jnp.concatenate([grads[l]["pre_norm"] for l in range(DEPTH)], axis=0),
        post_norm=jnp.concatenate([grads[l]["post_norm"] for l in range(DEPTH)], axis=0),
        gdn_A_log=jnp.stack([_from_lane_row(grads[l]["gdn_A_log"], "ga", GDN_HEADS) for l in range(DEPTH)]),
        gdn_dt_bias=jnp.stack([_from_lane_row(grads[l]["gdn_dt_bias"], "ga", GDN_HEADS) for l in range(DEPTH)]),
        gdn_norm=jnp.concatenate([grads[l]["gdn_norm"] for l in range(DEPTH)], axis=0),
        ssd_conv_b=jnp.concatenate([grads[l]["conv_b"][:, CONV_CH:] for l in range(DEPTH)], axis=0),
        ssd_A_log=jnp.stack([_from_lane_row(grads[l]["ssd_A_log"], "sdt", SSD_HEADS) for l in range(DEPTH)]),
        ssd_dt_bias=jnp.stack([_from_lane_row(grads[l]["ssd_dt_bias"], "sdt", SSD_HEADS) for l in range(DEPTH)]),
        ssd_D=jnp.stack([_from_lane_row(grads[l]["ssd_D"], "sdt", SSD_HEADS) for l in range(DEPTH)]),
        ssd_norm=jnp.concatenate([grads[l]["ssd_norm"].reshape(1, -1) for l in range(DEPTH)], axis=0),
        ret_norm=jnp.concatenate([grads[l]["ret_norm"] for l in range(DEPTH)], axis=0))
    gs = _pack_small(small_g)
    gs8 = jnp.broadcast_to(gs[None], (N_DEV,) + gs.shape)
    (r_small,) = _comm_call("exchange_small", "scatter", [gs8])
    recv = [[recv_dx[0], recv_scan[3], recv_dx[1]], recv_scan[:3]]

    conv_w = lambda g_, s_, l: jnp.stack([g_[l], s_[l]], axis=0).reshape(2 * CONV_W, SHARD_CONV)
    o_in, o_out, o_conv = None, None, []
    for l in range(DEPTH):
        o_in = _adam(f"adam_w_in{l}", recv[l][0], tr_in(w_in), tr_in(m_w_in), tr_in(v_w_in), SHARD_IN, l, o_in,
                     tc=256)
        o_out = _adam(f"adam_w_out{l}", recv[l][1], w_out, m_w_out, v_w_out, 128, l, o_out)
        o_conv.append(_adam(f"adam_conv{l}", recv[l][2], conv_w(gdn_conv, ssd_conv, l),
                            conv_w(m_gdn_conv, m_ssd_conv, l), conv_w(v_gdn_conv, v_ssd_conv, l), 2 * CONV_W))
    ps_w, ps_m, ps_v = _pack_small(small_w), _pack_small(small_m), _pack_small(small_v)
    o_small = _adam("adam_small", r_small, ps_w, ps_m, ps_v, ps_w.shape[0])

    names = ["pre_norm", "post_norm", "w_in", "gdn_conv", "gdn_A_log", "gdn_dt_bias", "gdn_norm", "ssd_conv",
             "ssd_conv_b", "ssd_A_log", "ssd_dt_bias", "ssd_D", "ssd_norm", "ret_norm", "w_out"]
    outs = []
    for kind in range(4):
        d = _unpack_small(o_small[kind], small_w)
        cv = jnp.stack([o_conv[l][kind].reshape(2, CONV_W, SHARD_CONV) for l in range(DEPTH)], axis=1)
        d["w_in"] = tr_in(o_in[kind])
        d["w_out"] = o_out[kind]
        d["gdn_conv"] = cv[0]
        d["ssd_conv"] = cv[1]
        outs.extend(d[n] for n in names)
    return (loss, grad_x, *outs)
```

```python
import functools
import math

import jax
import jax.numpy as jnp
from jax import lax
from jax.experimental import pallas as pl
from jax.experimental.pallas import tpu as pltpu

F32 = jnp.float32
BF16 = jnp.bfloat16

D_MODEL = 1024
DEPTH = 2
CHUNK = 64
CONV_W = 4
EPS = 1e-6
N_DEV = 8

GDN_HEADS = 4
GDN_DK = 128
SSD_HEADS = 16
SSD_P = 64
SSD_N = 128
SSD_GROUPS = 2
SSD_PAIRS = SSD_HEADS // 2
PAIRS_PER_GROUP = SSD_PAIRS // SSD_GROUPS
RET_HEADS = 4
RET_DK = 128
ROPE_BASE = 10000.0
MIX_W = 2048
N_IN = 6680
SHARD_IN = N_IN // N_DEV
SHARD_OUT = MIX_W // N_DEV
CONV_CH = 1536
SHARD_CONV = CONV_CH // N_DEV

ADAM_LR = 0.001
ADAM_B1 = 0.9
ADAM_B2 = 0.999
ADAM_EPS = 1e-08
ADAM_WD = 0.01
ADAM_STEP = 10

LANE = 128
SUBLANE = 8
VMEM_LIMIT = 56 * 1024 * 1024

_ORIG = dict(gq=(0, 512), gk=(512, 512), gv=(1024, 512), gz=(1536, 512), gb=(2048, 4), ga=(2052, 4),
             sx=(2056, 1024), sB=(3080, 256), sC=(3336, 256), sz=(3592, 1024), sdt=(4616, 16),
             rq=(4632, 512), rk=(5144, 512), rv=(5656, 512), rg=(6168, 512))
_ORDER = ["rq", "rk", "rv", "rg", "gq", "gk", "gv", "sx", "sB", "sC", "gz", "sz", "gb", "ga", "sdt"]
_MINE = {}
_off = 0
for _n in _ORDER:
    _MINE[_n] = _off
    _off += _ORIG[_n][1]
N_USED = _off
N_PAD = 7168
CONV_ALL = 2 * CONV_CH
SMALL_OFF = _MINE["gb"]
CONV_OFF = _MINE["gq"]
XC = lambda name: _MINE[name] - CONV_OFF


def _cparams(sem, vmem=None):
    return pltpu.CompilerParams(dimension_semantics=sem, vmem_limit_bytes=vmem)


def _split_bf16(a):
    hi = a.astype(BF16)
    return hi, (a - hi.astype(F32)).astype(BF16)


def _make_mm():
    def raw(a, b, ca, cb):
        return lax.dot_general(a.astype(BF16), b.astype(BF16), (((ca,), (cb,)), ((), ())),
                               preferred_element_type=F32)

    @jax.custom_vjp
    def nn(a, b):
        return raw(a, b, 1, 0)

    @jax.custom_vjp
    def nt(a, b):
        return raw(a, b, 1, 1)

    @jax.custom_vjp
    def tn(a, b):
        return raw(a, b, 0, 0)

    nn.defvjp(lambda a, b: (raw(a, b, 1, 0), (a, b)), lambda r, g: (nt(g, r[1]), tn(r[0], g)))
    nt.defvjp(lambda a, b: (raw(a, b, 1, 1), (a, b)), lambda r, g: (nn(g, r[1]), tn(g, r[0])))
    tn.defvjp(lambda a, b: (raw(a, b, 0, 0), (a, b)), lambda r, g: (nt(r[1], g), nn(r[0], g)))
    return nn, nt, tn


_nn, _nt, _tn = _make_mm()


@jax.custom_vjp
def _swap_halves(t):
    return pltpu.roll(t, LANE // 2, 1)


_swap_halves.defvjp(lambda t: (pltpu.roll(t, LANE // 2, 1), None),
                    lambda _, g: (pltpu.roll(g, LANE // 2, 1),))


@jax.custom_vjp
def _split_rows(x):
    return tuple(x[i * CHUNK:(i + 1) * CHUNK] for i in range(x.shape[0] // CHUNK))


_split_rows.defvjp(lambda x: (tuple(x[i * CHUNK:(i + 1) * CHUNK] for i in range(x.shape[0] // CHUNK)), None),
                   lambda _, gs: (jnp.concatenate(gs, axis=0),))


def _dot3(a, b, ca, cb):
    dot = lambda x, y: lax.dot_general(x, y, (((ca,), (cb,)), ((), ())), preferred_element_type=F32)
    return dot(a[0], b[0]) + (dot(a[0], b[1]) + dot(a[1], b[0]))


@jax.custom_vjp
def _tri_inv(mats):
    return _tri_inv_impl(mats)


def _tri_inv_impl(mats):
    ii = lax.broadcasted_iota(jnp.int32, mats[0].shape, 0)
    jj = lax.broadcasted_iota(jnp.int32, mats[0].shape, 1)
    eye = jnp.where(ii == jj, 1.0, 0.0).astype(F32)
    ts = [eye - a for a in mats]
    ps = [_split_bf16(-a) for a in mats]
    for _ in range(int(math.log2(CHUNK)) - 1):
        ps = [_split_bf16(_dot3(p, p, 1, 0)) for p in ps]
        ts = [t + _dot3(_split_bf16(t), p, 1, 0) for t, p in zip(ts, ps)]
    return ts


def _tri_inv_bwd(ts, gs):
    tsp = [_split_bf16(t) for t in ts]
    xs = [_dot3(t, _split_bf16(g), 0, 0) for t, g in zip(tsp, gs)]
    return ([-_dot3(_split_bf16(x), t, 1, 1) for x, t in zip(xs, tsp)],)


_tri_inv.defvjp(lambda mats: (lambda ts: (ts, ts))(_tri_inv_impl(mats)), _tri_inv_bwd)


@jax.custom_vjp
def _tri_inv_saved(mats, ts):
    return ts


_tri_inv_saved.defvjp(lambda mats, ts: (ts, ts),
                      lambda ts, gs: (_tri_inv_bwd(ts, gs)[0], [jnp.zeros_like(t) for t in ts]))


def _silu(x):
    return x * jax.nn.sigmoid(x)


@jax.custom_vjp
def _softplus(x):
    return jnp.maximum(x, 0.0) + jnp.log1p(jnp.exp(-jnp.abs(x)))


_softplus.defvjp(lambda x: (jnp.maximum(x, 0.0) + jnp.log1p(jnp.exp(-jnp.abs(x))), x),
                 lambda x, g: (g * jax.nn.sigmoid(x),))


def _rms(x, w):
    return x * lax.rsqrt(jnp.mean(x * x, axis=-1, keepdims=True) + EPS) * w


def _chunk_masks(n):
    ii = lax.broadcasted_iota(jnp.int32, (n, n), 0)
    jj = lax.broadcasted_iota(jnp.int32, (n, n), 1)
    return ii >= jj, ii > jj, ii == jj, ii <= jj


def _cumsum_col(g, causal, eye, upper):
    g_row = jnp.sum(jnp.where(eye, g, 0.0), axis=0, keepdims=True)
    col = jnp.sum(jnp.where(causal, g_row, 0.0), axis=1, keepdims=True)
    row = jnp.sum(jnp.where(upper, g, 0.0), axis=0, keepdims=True)
    return col, row


def _lane_col(block, lane):
    pick = lax.broadcasted_iota(jnp.int32, (1, block.shape[1]), 1) == lane
    return jnp.sum(jnp.where(pick, block, 0.0), axis=1, keepdims=True)


def _gdn_chunk(q, k, v, gz, sm, alog_row, dtb_row, nw, s, t_saved=None):
    subs, nh = range(len(q)), len(q[0])
    c = q[0][0].shape[0]
    causal, _, eye, upper = _chunk_masks(c)
    stack = lambda xs: jnp.concatenate(xs, axis=0)
    per = lambda f: [[f(j, h) for h in range(nh)] for j in subs]
    qn = per(lambda j, h: q[j][h] * lax.rsqrt(jnp.sum(q[j][h] * q[j][h], axis=-1, keepdims=True) + EPS)
             * (GDN_DK ** -0.5))
    kn = per(lambda j, h: k[j][h] * lax.rsqrt(jnp.sum(k[j][h] * k[j][h], axis=-1, keepdims=True) + EPS))
    beta_blk = [jax.nn.sigmoid(sm[j]) for j in subs]
    g_blk = [-jnp.exp(alog_row) * _softplus(sm[j] + dtb_row) for j in subs]
    beta = per(lambda j, h: _lane_col(beta_blk[j], _MINE["gb"] - SMALL_OFF + h))
    g = per(lambda j, h: _lane_col(g_blk[j], _MINE["ga"] - SMALL_OFF + h))
    gcum = per(lambda j, h: _cumsum_col(g[j][h], causal, eye, upper)[0])
    eg = per(lambda j, h: jnp.exp(gcum[j][h]))
    glast = per(lambda j, h: jnp.sum(g[j][h], axis=0, keepdims=True))
    kb = per(lambda j, h: kn[j][h] * beta[j][h])
    n = nh * c
    ii = lax.broadcasted_iota(jnp.int32, (n, n), 0)
    jj = lax.broadcasted_iota(jnp.int32, (n, n), 1)
    sh = int(math.log2(c))
    same = lax.shift_right_logical(ii, sh) == lax.shift_right_logical(jj, sh)
    causal_bd = jnp.logical_and(same, ii >= jj)
    strict_bd = jnp.logical_and(same, ii > jj)
    gcum_all = [stack(gcum[j]) for j in subs]
    gcum_row = [jnp.sum(jnp.where(ii == jj, gcum_all[j], 0.0), axis=0, keepdims=True) for j in subs]
    decay = [jnp.where(causal_bd, jnp.exp(jnp.where(causal_bd, gcum_all[j] - gcum_row[j], 0.0)), 0.0) for j in subs]
    kn_all = [stack(kn[j]) for j in subs]
    a_low = [jnp.where(strict_bd, _nt(stack(kb[j]), kn_all[j]) * decay[j], 0.0) for j in subs]
    t = _tri_inv(a_low) if t_saved is None else _tri_inv_saved(a_low, t_saved)
    u = [_nn(t[j], stack([v[j][h] * beta[j][h] for h in range(nh)])) for j in subs]
    w = [_split_rows(_nn(t[j], stack([kb[j][h] * eg[j][h] for h in range(nh)]))) for j in subs]
    attn = [_nt(stack(qn[j]), kn_all[j]) * decay[j] for j in subs]
    kdec = per(lambda j, h: kn[j][h] * jnp.exp(glast[j][h] - gcum[j][h]))
    on = []
    for j in subs:
        v_new_all = u[j] - stack([_nn(w[j][h], s[h]) for h in range(nh)])
        o = _split_rows(stack([_nn(qn[j][h] * eg[j][h], s[h]) for h in range(nh)]) + _nn(attn[j], v_new_all))
        v_new = _split_rows(v_new_all)
        s = [s[h] * jnp.exp(glast[j][h]) + _tn(kdec[j][h], v_new[h]) for h in range(nh)]
        on.append([_rms(o[h], nw) * _silu(gz[j][h]) for h in range(nh)])
    return on, s, t


def _ssd_chunk(lanes, x, bm, cm, sm, alog_row, dtb_row, d_row, hs):
    subs, pairs = range(len(x)), range(len(x[0]))
    c = x[0][0].shape[0]
    lane_i = lax.broadcasted_iota(jnp.int32, (c, LANE), 1)
    lane_lo = lane_i < SSD_P
    lane_lo1 = lax.broadcasted_iota(jnp.int32, (1, LANE), 1) < SSD_P
    row_i = lax.broadcasted_iota(jnp.int32, (c, LANE), 0)
    causal2 = row_i >= jnp.bitwise_and(lane_i, c - 1)
    diag2 = row_i == jnp.bitwise_and(lane_i, c - 1)
    to_row = lambda blk: jnp.sum(jnp.where(diag2, blk, 0.0), axis=0, keepdims=True)
    per = lambda f: [[f(j, p) for p in pairs] for j in subs]
    both = lambda blk, p: [_lane_col(blk, lanes[p][h]) for h in range(2)]
    dt_blk = [_softplus(sm[j] + dtb_row) for j in subs]
    a_blk = [dt_blk[j] * -jnp.exp(alog_row) for j in subs]
    alast_blk = [jnp.sum(a_blk[j], axis=0, keepdims=True) for j in subs]
    dt = per(lambda j, p: both(dt_blk[j], p))
    nexp = [both(-jnp.exp(alog_row), p) for p in pairs]
    alast = per(lambda j, p: both(alast_blk[j], p))
    dp = [both(d_row, p) for p in pairs]
    dt_lane = per(lambda j, p: jnp.where(lane_lo, dt[j][p][0], dt[j][p][1]))
    a_lane = per(lambda j, p: dt_lane[j][p] * jnp.where(lane_lo1, nexp[p][0], nexp[p][1]))
    a_row = per(lambda j, p: to_row(a_lane[j][p]))
    part = per(lambda j, p: jnp.where(causal2, a_row[j][p], 0.0))
    acum = per(lambda j, p: [jnp.sum(jnp.where(lane_lo, part[j][p], 0.0), axis=1, keepdims=True),
                             jnp.sum(jnp.where(lane_lo, 0.0, part[j][p]), axis=1, keepdims=True)])
    acum_col = per(lambda j, p: jnp.where(lane_lo, acum[j][p][0], acum[j][p][1]))
    acum_row = per(lambda j, p: to_row(acum_col[j][p]))
    lmat = per(lambda j, p: jnp.where(causal2, jnp.exp(jnp.where(causal2, acum_col[j][p] - acum_row[j][p], 0.0)), 0.0))
    cb = [_nt(cm[j], jnp.concatenate([bm[j], bm[j]], axis=0)) for j in subs]
    xdt = per(lambda j, p: x[j][p] * dt_lane[j][p])
    xdt_rows = per(lambda j, p: jnp.concatenate([jnp.where(lane_lo, xdt[j][p], 0.0),
                                                 jnp.where(lane_lo, 0.0, xdt[j][p])], axis=0))
    intra = per(lambda j, p: _nn(cb[j] * lmat[j][p], xdt_rows[j][p]))
    skip = per(lambda j, p: x[j][p] * jnp.where(lane_lo1, dp[p][0], dp[p][1]))
    eacc = per(lambda j, p: jnp.exp(acum_col[j][p]))
    alast_row = per(lambda j, p: jnp.where(lane_lo1, alast[j][p][0], alast[j][p][1]))
    wdec = per(lambda j, p: jnp.exp(alast_row[j][p] - acum_col[j][p]))
    scale = per(lambda j, p: jnp.exp(alast_row[j][p]))
    upd = per(lambda j, p: _tn(bm[j], xdt[j][p] * wdec[j][p]))
    y = []
    for j in subs:
        y.append([skip[j][p] + intra[j][p] + _nn(cm[j], hs[p]) * eacc[j][p] for p in pairs])
        hs = [hs[p] * scale[j][p] + upd[j][p] for p in pairs]
    return y, hs


def _ret_chunk(rq, rk, rv, rg, cos2, sin2, dmat, qdec, kdec, cdec, nw, r):
    subs, hs = range(len(rq)), range(len(rq[0]))
    per = lambda f: [[f(j, h) for h in hs] for j in subs]
    q = per(lambda j, h: rq[j][h] * cos2[j] + _swap_halves(rq[j][h]) * sin2[j])
    k = per(lambda j, h: (rk[j][h] * cos2[j] + _swap_halves(rk[j][h]) * sin2[j]) * (RET_DK ** -0.5))
    s = per(lambda j, h: _nt(q[j][h], k[j][h]) * dmat[h])
    intra = per(lambda j, h: _nn(s[j][h], rv[j][h]))
    upd = per(lambda j, h: _tn(k[j][h] * kdec[h], rv[j][h]))
    gate = per(lambda j, h: _silu(rg[j][h]))
    on = []
    for j in subs:
        on.append([_rms(intra[j][h] + _nn(q[j][h], r[h]) * qdec[h], nw) * gate[j][h] for h in hs])
        r = [r[h] * cdec[h] + upd[j][h] for h in hs]
    return on, r


MM_TILE = 1024


def _tile(n, pref=MM_TILE):
    return pref if n % pref == 0 else n


def _matmul(name, a, b, mode, m, n, k, tk=None, comm=None, out_dtype=F32):
    tm, tn = _tile(m), _tile(n)
    tk = _tile(k) if tk is None else tk
    nk = k // tk
    ca, cb = {"nn": (1, 0), "nt": (1, 1), "tn": (0, 0)}[mode]
    grid = (m // tm, n // tn, nk)
    n_comm = len(comm[1]) if comm else 0
    use_acc = out_dtype != F32 and nk > 1

    def at_step(which):
        return functools.reduce(jnp.logical_and, [pl.program_id(d) == (0 if which == 0 else grid[d] - 1)
                                                  for d in range(3)])

    def body(*refs):
        a_ref, b_ref = refs[:2]
        c_in = refs[2:2 + n_comm]
        o_ref = refs[2 + n_comm]
        c_out = refs[3 + n_comm:3 + 2 * n_comm]
        acc_ref = refs[3 + 2 * n_comm] if use_acc else o_ref
        sems = refs[3 + 2 * n_comm + use_acc:]
        if comm:
            @pl.when(at_step(0))
            def _():
                _comm_plan(comm[0], c_in, c_out, *sems)[0]()

        part = lax.dot_general(a_ref[...].astype(BF16), b_ref[...].astype(BF16),
                               (((ca,), (cb,)), ((), ())), preferred_element_type=F32)
        if nk == 1:
            o_ref[...] = part.astype(o_ref.dtype)
        else:
            kk = pl.program_id(2)

            @pl.when(kk == 0)
            def _():
                acc_ref[...] = part

            @pl.when(kk > 0)
            def _():
                acc_ref[...] += part

            if use_acc:
                @pl.when(kk == nk - 1)
                def _():
                    o_ref[...] = acc_ref[...].astype(o_ref.dtype)

        if comm:
            @pl.when(at_step(1))
            def _():
                _comm_plan(comm[0], c_in, c_out, *sems)[1]()

    a_spec = pl.BlockSpec((tk, tm), lambda i, j, kk: (kk, i)) if mode == "tn" else pl.BlockSpec((tm, tk), lambda i, j, kk: (i, kk))
    b_spec = pl.BlockSpec((tn, tk), lambda i, j, kk: (j, kk)) if mode == "nt" else pl.BlockSpec((tk, tn), lambda i, j, kk: (kk, j))
    hbm = pl.BlockSpec(memory_space=pl.ANY)
    res = pl.pallas_call(
        body, name=name, grid=grid,
        in_specs=[a_spec, b_spec] + [hbm] * n_comm,
        out_specs=[pl.BlockSpec((tm, tn), lambda i, j, kk: (i, j))] + [hbm] * n_comm,
        out_shape=[jax.ShapeDtypeStruct((m, n), out_dtype)] + (_comm_out_shapes(*comm) if comm else []),
        scratch_shapes=([pltpu.VMEM((tm, tn), F32)] if use_acc else []) + (_comm_scratch(n_comm) if comm else []),
        compiler_params=_cparams(("arbitrary",) * 3 if comm else ("parallel", "parallel", "arbitrary"), VMEM_LIMIT),
    )(a, b, *(comm[1] if comm else []))
    return res if comm else res[0]


def _rowmap(name, fn, length, tm, rows, params, out_ws=(), acc_ws=(), out_dtypes=None, cts=None, d_dtypes=None,
            place=None):
    nt_ = length // tm
    n_r, n_p = len(rows), len(params)
    n_in = n_r + n_p
    rspec = lambda bw, cbk: pl.BlockSpec((tm, bw), lambda i: (i, cbk))
    pspec = lambda w: pl.BlockSpec((1, w), lambda i: (0, 0))
    in_arrays = [r[0] for r in rows] + list(params)
    in_specs = [rspec(r[1], r[2]) for r in rows] + [pspec(p.shape[1]) for p in params]
    place = place or {}

    def load(refs):
        return [r[...].astype(F32) for r in refs]

    def placed(widths, dtypes, n_before):
        specs, shapes, extra, alias = [], [], [], {}
        for k, (w, dt) in enumerate(zip(widths, dtypes)):
            arr, total, cbk = place.get(k, (None, w, 0))
            specs.append(rspec(w, cbk))
            shapes.append(jax.ShapeDtypeStruct((length, total), dt))
            if arr is not None:
                alias[n_before + len(extra)] = k
                extra.append(arr)
        return specs, shapes, extra, alias

    def accumulate(a_refs, vals):
        first = pl.program_id(0) == 0
        for a_ref, v in zip(a_refs, vals):
            @pl.when(first)
            def _(a_ref=a_ref):
                a_ref[...] = jnp.zeros_like(a_ref)
            a_ref[...] += v

    hbm = pl.BlockSpec(memory_space=pl.ANY)
    if cts is None:
        n_o = len(out_ws)
        o_specs, o_shapes, extra, alias = placed(out_ws, out_dtypes or [F32] * n_o, n_in)

        def body(*refs):
            outs, accs = fn(*load(refs[:n_in]))
            o_refs = refs[n_in + len(extra):]
            for o_ref, o in zip(o_refs[:n_o], outs):
                o_ref[...] = o.astype(o_ref.dtype)
            accumulate(o_refs[n_o:], accs)

        res = pl.pallas_call(
            body, name=name, grid=(nt_,), in_specs=in_specs + [hbm] * len(extra),
            out_specs=o_specs + [pspec(w) for w in acc_ws],
            out_shape=o_shapes + [jax.ShapeDtypeStruct((1, w), F32) for w in acc_ws],
            input_output_aliases=alias, compiler_params=_cparams(("arbitrary",), VMEM_LIMIT),
        )(*in_arrays, *extra)
        return res[:n_o], res[n_o:]

    n_c = len(cts)
    o_specs, o_shapes, extra, alias = placed([r[1] for r in rows], d_dtypes or [F32] * n_r, n_in + n_c)

    def body(*refs):
        ins = load(refs[:n_in])
        _, vjp = jax.vjp(lambda rs, ps: fn(*rs, *ps)[0], ins[:n_r], ins[n_r:])
        d_rows, d_params = vjp(tuple(load(refs[n_in:n_in + n_c])))
        o_refs = refs[n_in + n_c + len(extra):]
        for o_ref, d in zip(o_refs[:n_r], d_rows):
            o_ref[...] = d.astype(o_ref.dtype)
        accumulate(o_refs[n_r:], d_params)

    res = pl.pallas_call(
        body, name=name, grid=(nt_,),
        in_specs=in_specs + [rspec(c[1], c[2]) for c in cts] + [hbm] * len(extra),
        out_specs=o_specs + [pspec(p.shape[1]) for p in params],
        out_shape=o_shapes + [jax.ShapeDtypeStruct((1, p.shape[1]), F32) for p in params],
        input_output_aliases=alias, compiler_params=_cparams(("arbitrary",), VMEM_LIMIT),
    )(*in_arrays, *[c[0] for c in cts], *extra)
    return res[:n_r], res[n_r:]


def _conv_shift(ext, k, tm, forward):
    s = CONV_W - 1 - k
    if forward:
        rolled = ext if s == 0 else pltpu.roll(ext, s, 0)
        return rolled[SUBLANE:, :]
    rolled = ext if s == 0 else pltpu.roll(ext, tm + SUBLANE - s, 0)
    return rolled[:tm, :]


def _conv_pre(x_ref, halo_ref, w_ref, b_ref, first, tm):
    halo = jnp.where(first, 0.0, halo_ref[...])
    ext = jnp.concatenate([halo, x_ref[...]], axis=0)
    w = w_ref[...]
    pre = b_ref[...] + jnp.zeros_like(x_ref[...])
    taps = []
    for k in range(CONV_W):
        tap = _conv_shift(ext, k, tm, True)
        taps.append(tap)
        pre = pre + tap * w[k:k + 1, :]
    return pre, taps


def _conv_fwd(proj, w, b, length, tm, tc):
    hb = tm // SUBLANE
    j0 = CONV_OFF // tc

    def body(x_ref, halo_ref, w_ref, b_ref, o_ref):
        pre, _ = _conv_pre(x_ref, halo_ref, w_ref, b_ref, pl.program_id(0) == 0, tm)
        o_ref[...] = _silu(pre)

    return pl.pallas_call(
        body, name="conv_fwd", grid=(length // tm, CONV_ALL // tc),
        in_specs=[pl.BlockSpec((tm, tc), lambda i, j: (i, j0 + j)),
                  pl.BlockSpec((SUBLANE, tc), lambda i, j: (jnp.maximum(i * hb - 1, 0), j0 + j)),
                  pl.BlockSpec((CONV_W, tc), lambda i, j: (0, j)),
                  pl.BlockSpec((1, tc), lambda i, j: (0, j))],
        out_specs=pl.BlockSpec((tm, tc), lambda i, j: (i, j)),
        out_shape=jax.ShapeDtypeStruct((length, CONV_ALL), F32),
        compiler_params=_cparams(("parallel", "parallel"), VMEM_LIMIT),
    )(proj, proj, w, b)


def _conv_bwd_pre(proj, w, b, dxc, length, tm, tc):
    hb = tm // SUBLANE
    j0 = CONV_OFF // tc

    def body(x_ref, halo_ref, w_ref, b_ref, dy_ref, dpre_ref, dw_ref, db_ref):
        i = pl.program_id(1)
        pre, taps = _conv_pre(x_ref, halo_ref, w_ref, b_ref, i == 0, tm)
        sg = jax.nn.sigmoid(pre)
        dpre = dy_ref[...] * (sg * (1.0 + pre * (1.0 - sg)))
        dpre_ref[...] = dpre

        @pl.when(i == 0)
        def _():
            dw_ref[...] = jnp.zeros_like(dw_ref)
            db_ref[...] = jnp.zeros_like(db_ref)

        for k in range(CONV_W):
            dw_ref[k:k + 1, :] += jnp.sum(dpre * taps[k], axis=0, keepdims=True)
        db_ref[...] += jnp.sum(dpre, axis=0, keepdims=True)

    return pl.pallas_call(
        body, name="conv_bwd_pre", grid=(CONV_ALL // tc, length // tm),
        in_specs=[pl.BlockSpec((tm, tc), lambda j, i: (i, j0 + j)),
                  pl.BlockSpec((SUBLANE, tc), lambda j, i: (jnp.maximum(i * hb - 1, 0), j0 + j)),
                  pl.BlockSpec((CONV_W, tc), lambda j, i: (0, j)),
                  pl.BlockSpec((1, tc), lambda j, i: (0, j)),
                  pl.BlockSpec((tm, tc), lambda j, i: (i, j))],
        out_specs=[pl.BlockSpec((tm, tc), lambda j, i: (i, j)),
                   pl.BlockSpec((CONV_W, tc), lambda j, i: (0, j)),
                   pl.BlockSpec((1, tc), lambda j, i: (0, j))],
        out_shape=[jax.ShapeDtypeStruct((length, CONV_ALL), F32),
                   jax.ShapeDtypeStruct((CONV_W, CONV_ALL), F32),
                   jax.ShapeDtypeStruct((1, CONV_ALL), F32)],
        compiler_params=_cparams(("parallel", "arbitrary"), VMEM_LIMIT),
    )(proj, proj, w, b, dxc)


def _conv_bwd_x(dpre, w, dproj, length, tm, tc):
    hb = tm // SUBLANE
    n_t = length // tm
    last_blk = length // SUBLANE - 1
    j0 = CONV_OFF // tc

    def body(d_ref, halo_ref, w_ref, _, o_ref):
        halo = jnp.where(pl.program_id(0) == n_t - 1, 0.0, halo_ref[...])
        ext = jnp.concatenate([d_ref[...], halo], axis=0)
        w = w_ref[...]
        acc = jnp.zeros_like(d_ref[...])
        for k in range(CONV_W):
            acc = acc + _conv_shift(ext, k, tm, False) * w[k:k + 1, :]
        o_ref[...] = acc.astype(o_ref.dtype)

    return pl.pallas_call(
        body, name="conv_bwd_x", grid=(n_t, CONV_ALL // tc),
        in_specs=[pl.BlockSpec((tm, tc), lambda i, j: (i, j)),
                  pl.BlockSpec((SUBLANE, tc), lambda i, j: (jnp.minimum((i + 1) * hb, last_blk), j)),
                  pl.BlockSpec((CONV_W, tc), lambda i, j: (0, j)),
                  pl.BlockSpec(memory_space=pl.ANY)],
        out_specs=pl.BlockSpec((tm, tc), lambda i, j: (i, j0 + j)),
        out_shape=jax.ShapeDtypeStruct(dproj.shape, dproj.dtype), input_output_aliases={3: 0},
        compiler_params=_cparams(("parallel", "parallel"), VMEM_LIMIT),
    )(dpre, dpre, w, dproj)


HEAD_SCAN_CHUNKS_PER_STEP = 8
assert GDN_HEADS == RET_HEADS


def _scan_steps(length):
    nc = length // CHUNK
    gs = HEAD_SCAN_CHUNKS_PER_STEP if nc % HEAD_SCAN_CHUNKS_PER_STEP == 0 else 1
    return gs, nc // gs, gs * CHUNK


def _head_tiles(ref, gs):
    return [[ref[j * CHUNK:(j + 1) * CHUNK, h * LANE:(h + 1) * LANE] for h in range(GDN_HEADS)] for j in range(gs)]


def _chunk_rows(ref, gs):
    return [ref[j * CHUNK:(j + 1) * CHUNK, :] for j in range(gs)]


def _gdn_fwd(xc, proj, alog, dtb, nw, length, comm=None):
    gs, steps, rows = _scan_steps(length)
    h_ = GDN_HEADS
    hw = h_ * LANE
    tn_ = h_ * CHUNK
    n_comm = len(comm[1]) if comm else 0

    def body(*refs):
        q_ref, k_ref, v_ref, gz_ref, sm_ref, al_ref, dt_ref, nw_ref = refs[:8]
        c_in = refs[8:8 + n_comm]
        on_ref, st_ref, t_ref = refs[8 + n_comm:11 + n_comm]
        c_out = refs[11 + n_comm:11 + 2 * n_comm]
        s_ref = refs[11 + 2 * n_comm]
        sems = refs[12 + 2 * n_comm:]

        @pl.when(pl.program_id(0) == 0)
        def _():
            s_ref[...] = jnp.zeros_like(s_ref)
            if comm:
                _comm_plan(comm[0], c_in, c_out, *sems)[0]()

        st_ref[...] = s_ref[...]
        lead = lambda ref: [ref[h] for h in range(h_)]
        on, s_new, t = _gdn_chunk(_head_tiles(q_ref, gs), _head_tiles(k_ref, gs), _head_tiles(v_ref, gs),
                                  _head_tiles(gz_ref, gs), _chunk_rows(sm_ref, gs), al_ref[...], dt_ref[...],
                                  nw_ref[...], lead(s_ref))
        for j in range(gs):
            t_ref[j] = t[j]
        for h in range(h_):
            for j in range(gs):
                on_ref[j * CHUNK:(j + 1) * CHUNK, h * LANE:(h + 1) * LANE] = on[j][h].astype(on_ref.dtype)
            s_ref[h] = s_new[h]

        if comm:
            @pl.when(pl.program_id(0) == steps - 1)
            def _():
                _comm_plan(comm[0], c_in, c_out, *sems)[1]()

    blk = lambda col: pl.BlockSpec((rows, hw), lambda c: (c, col // hw))
    one = pl.BlockSpec((1, LANE), lambda c: (0, 0))
    hbm = pl.BlockSpec(memory_space=pl.ANY)
    return pl.pallas_call(
        body, name="gdn_fwd_" + comm[0] if comm else "gdn_fwd", grid=(steps,),
        in_specs=[blk(XC("gq")), blk(XC("gk")), blk(XC("gv")), blk(_MINE["gz"]),
                  pl.BlockSpec((rows, LANE), lambda c: (c, SMALL_OFF // LANE)), one, one,
                  pl.BlockSpec((1, LANE), lambda c: (0, 0))] + [hbm] * n_comm,
        out_specs=[pl.BlockSpec((rows, hw), lambda c: (c, 0)),
                   pl.BlockSpec((h_, None, LANE, LANE), lambda c: (0, c, 0, 0)),
                   pl.BlockSpec((gs, tn_, tn_), lambda c: (c, 0, 0))] + [hbm] * n_comm,
        out_shape=[jax.ShapeDtypeStruct((length, MIX_W), BF16),
                   jax.ShapeDtypeStruct((h_, steps, LANE, LANE), F32),
                   jax.ShapeDtypeStruct((length // CHUNK, tn_, tn_), F32)]
                  + (_comm_out_shapes(*comm) if comm else []),
        scratch_shapes=[pltpu.VMEM((h_, LANE, LANE), F32)] + (_comm_scratch(n_comm) if comm else []),
        compiler_params=_cparams(("arbitrary",), VMEM_LIMIT),
    )(xc, xc, xc, proj, proj, alog, dtb, nw, *(comm[1] if comm else []))


def _gdn_bwd(xc, proj, alog, dtb, nw, st, tinv, dmix, length, comm=None):
    gs, steps, rows = _scan_steps(length)
    h_ = GDN_HEADS
    hw = h_ * LANE
    tn_ = h_ * CHUNK
    n_comm = len(comm[1]) if comm else 0
    n_in, n_out = 11, 6

    def body(*refs):
        q_ref, k_ref, v_ref, gz_ref, sm_ref, al_ref, dt_ref, nw_ref, st_ref, t_ref, do_ref = refs[:n_in]
        c_in = refs[n_in:n_in + n_comm]
        o0 = n_in + n_comm
        dqkv_ref, dgz_ref, dsm_ref, dal_ref, ddt_ref, dnw_ref = refs[o0:o0 + n_out]
        c_out = refs[o0 + n_out:o0 + n_out + n_comm]
        ds_ref = refs[o0 + n_out + n_comm]
        sems = refs[o0 + n_out + n_comm + 1:]

        @pl.when(pl.program_id(0) == 0)
        def _():
            ds_ref[...] = jnp.zeros_like(ds_ref)
            dal_ref[...] = jnp.zeros_like(dal_ref)
            ddt_ref[...] = jnp.zeros_like(ddt_ref)
            dnw_ref[...] = jnp.zeros_like(dnw_ref)
            if comm:
                _comm_plan(comm[0], c_in, c_out, *sems)[0]()

        lead = lambda ref: [ref[h] for h in range(h_)]
        ins = (_head_tiles(q_ref, gs), _head_tiles(k_ref, gs), _head_tiles(v_ref, gs), _head_tiles(gz_ref, gs),
               _chunk_rows(sm_ref, gs), al_ref[...], dt_ref[...], nw_ref[...], lead(st_ref))
        t_saved = [t_ref[j] for j in range(gs)]
        _, vjp = jax.vjp(lambda *a: _gdn_chunk(*a, t_saved=t_saved)[:2], *ins)
        dq, dk, dv, dgz, dsm, dal, ddt, dnw, ds = vjp((_head_tiles(do_ref, gs), lead(ds_ref)))
        for j in range(gs):
            rws = slice(j * CHUNK, (j + 1) * CHUNK)
            for h in range(h_):
                cols = slice(h * LANE, (h + 1) * LANE)
                for part, d in enumerate((dq, dk, dv)):
                    dqkv_ref[rws, part * hw + h * LANE:part * hw + (h + 1) * LANE] = d[j][h]
                dgz_ref[rws, cols] = dgz[j][h].astype(dgz_ref.dtype)
            dsm_ref[rws, :] = dsm[j]
        for h in range(h_):
            ds_ref[h] = ds[h]
        dal_ref[...] += dal
        ddt_ref[...] += ddt
        dnw_ref[...] += dnw

        if comm:
            @pl.when(pl.program_id(0) == steps - 1)
            def _():
                _comm_plan(comm[0], c_in, c_out, *sems)[1]()

    r = lambda c: steps - 1 - c
    blk = lambda col: pl.BlockSpec((rows, hw), lambda c: (r(c), col // hw))
    one = pl.BlockSpec((1, LANE), lambda c: (0, 0))
    hbm = pl.BlockSpec(memory_space=pl.ANY)
    return pl.pallas_call(
        body, name="gdn_bwd_" + comm[0] if comm else "gdn_bwd", grid=(steps,),
        in_specs=[blk(XC("gq")), blk(XC("gk")), blk(XC("gv")), blk(_MINE["gz"]),
                  pl.BlockSpec((rows, LANE), lambda c: (r(c), SMALL_OFF // LANE)), one, one,
                  pl.BlockSpec((1, LANE), lambda c: (0, 0)),
                  pl.BlockSpec((h_, None, LANE, LANE), lambda c: (0, r(c), 0, 0)),
                  pl.BlockSpec((gs, tn_, tn_), lambda c: (r(c), 0, 0)),
                  blk(0)] + [hbm] * n_comm,
        out_specs=[pl.BlockSpec((rows, 3 * hw), lambda c: (r(c), 0)), blk(_MINE["gz"]),
                   pl.BlockSpec((rows, LANE), lambda c: (r(c), 0)), one, one,
                   pl.BlockSpec((1, LANE), lambda c: (0, 0))] + [hbm] * n_comm,
        out_shape=[jax.ShapeDtypeStruct((length, CONV_ALL), F32), jax.ShapeDtypeStruct((length, N_PAD), BF16),
                   jax.ShapeDtypeStruct((length, LANE), F32),
                   jax.ShapeDtypeStruct((1, LANE), F32), jax.ShapeDtypeStruct((1, LANE), F32),
                   jax.ShapeDtypeStruct((1, LANE), F32)] + (_comm_out_shapes(*comm) if comm else []),
        scratch_shapes=[pltpu.VMEM((h_, LANE, LANE), F32)] + (_comm_scratch(n_comm) if comm else []),
        compiler_params=_cparams(("arbitrary",), VMEM_LIMIT),
    )(xc, xc, xc, proj, proj, alog, dtb, nw, st, tinv, dmix, *(comm[1] if comm else []))


SSD_CHUNKS_PER_STEP = 8


def _ssd_steps(length):
    nc = length // CHUNK
    gs = SSD_CHUNKS_PER_STEP if nc % SSD_CHUNKS_PER_STEP == 0 else 1
    return gs, nc // gs, gs * CHUNK


def _ssd_lanes(g):
    base = _MINE["sdt"] - SMALL_OFF
    return [[base + (g * PAIRS_PER_GROUP + p) * 2 + j for j in range(2)] for p in range(PAIRS_PER_GROUP)]


def _ssd_tiles(ref, gs):
    return [[ref[j * CHUNK:(j + 1) * CHUNK, p * LANE:(p + 1) * LANE] for p in range(PAIRS_PER_GROUP)]
            for j in range(gs)]


def _ssd_gate(y, z, w):
    return _rms(y * _silu(z), w)


def _ssd_fwd(xc, proj, alog_row, dtb_row, d_row, norm_w, mixed, length):
    gs, steps, rows = _ssd_steps(length)
    ppg = PAIRS_PER_GROUP
    gw = ppg * LANE

    def body(x_ref, b_ref, c_ref, sm_ref, al_ref, db_ref, dp_ref, z_ref, nw_ref, _, y_ref, o_ref, st_ref, hs_ref):
        @pl.when(pl.program_id(1) == 0)
        def _():
            hs_ref[...] = jnp.zeros_like(hs_ref)

        st_ref[...] = hs_ref[...]
        y, hs_new = _ssd_chunk(_ssd_lanes(pl.program_id(0)), _ssd_tiles(x_ref, gs), _chunk_rows(b_ref, gs),
                               _chunk_rows(c_ref, gs), _chunk_rows(sm_ref, gs), al_ref[...], db_ref[...],
                               dp_ref[...], [hs_ref[p] for p in range(ppg)])
        for p in range(ppg):
            hs_ref[p] = hs_new[p]
        for j in range(gs):
            rws = slice(j * CHUNK, (j + 1) * CHUNK)
            y_j = jnp.concatenate(y[j], axis=1)
            y_ref[rws, :] = y_j
            o_ref[rws, :] = _ssd_gate(y_j, z_ref[rws, :], nw_ref[...]).astype(o_ref.dtype)

    one = pl.BlockSpec((1, LANE), lambda g, c: (0, 0))
    return pl.pallas_call(
        body, name="ssd_fwd", grid=(SSD_GROUPS, steps),
        in_specs=[pl.BlockSpec((rows, gw), lambda g, c: (c, XC("sx") // gw + g)),
                  pl.BlockSpec((rows, LANE), lambda g, c: (c, XC("sB") // LANE + g)),
                  pl.BlockSpec((rows, LANE), lambda g, c: (c, XC("sC") // LANE + g)),
                  pl.BlockSpec((rows, LANE), lambda g, c: (c, SMALL_OFF // LANE)),
                  one, one, one,
                  pl.BlockSpec((rows, gw), lambda g, c: (c, _MINE["sz"] // gw + g)),
                  pl.BlockSpec((None, 1, gw), lambda g, c: (g, 0, 0)),
                  pl.BlockSpec(memory_space=pl.ANY)],
        out_specs=[pl.BlockSpec((rows, gw), lambda g, c: (c, g)),
                   pl.BlockSpec((rows, gw), lambda g, c: (c, 1 + g)),
                   pl.BlockSpec((None, ppg, None, LANE, SSD_N), lambda g, c: (g, 0, c, 0, 0))],
        out_shape=[jax.ShapeDtypeStruct((length, SSD_GROUPS * gw), F32),
                   jax.ShapeDtypeStruct(mixed.shape, mixed.dtype),
                   jax.ShapeDtypeStruct((SSD_GROUPS, ppg, steps, LANE, SSD_N), F32)],
        input_output_aliases={9: 1},
        scratch_shapes=[pltpu.VMEM((ppg, LANE, SSD_N), F32)],
        compiler_params=_cparams(("parallel", "arbitrary"), VMEM_LIMIT),
    )(xc, xc, xc, proj, alog_row, dtb_row, d_row, proj, norm_w, mixed)


def _ssd_bwd(xc, proj, alog_row, dtb_row, d_row, norm_w, st, y_b, dmix, dproj, dxc, length):
    gs, steps, rows = _ssd_steps(length)
    ppg = PAIRS_PER_GROUP
    gw = ppg * LANE

    def body(x_ref, b_ref, c_ref, sm_ref, al_ref, db_ref, dp_ref, z_ref, nw_ref, st_ref, y_ref, do_ref, _, __,
             dxc_ref, dz_ref, dsm_ref, dal_ref, ddb_ref, ddp_ref, dnw_ref, dhs_ref):
        g = pl.program_id(1)

        @pl.when(jnp.logical_and(pl.program_id(0) == 0, g == 0))
        def _():
            dhs_ref[...] = jnp.zeros_like(dhs_ref)
            dal_ref[...] = jnp.zeros_like(dal_ref)
            ddb_ref[...] = jnp.zeros_like(ddb_ref)
            ddp_ref[...] = jnp.zeros_like(ddp_ref)
            dnw_ref[...] = jnp.zeros_like(dnw_ref)

        pairs = range(ppg)
        dy, dnw = [], jnp.zeros((1, gw), F32)
        for j in range(gs):
            rws = slice(j * CHUNK, (j + 1) * CHUNK)
            _, gate_vjp = jax.vjp(_ssd_gate, y_ref[rws, :], z_ref[rws, :], nw_ref[...])
            dy_j, dz_j, dnw_j = gate_vjp(do_ref[rws, :].astype(F32))
            dz_ref[rws, :] = dz_j.astype(dz_ref.dtype)
            dy.append([dy_j[:, p * LANE:(p + 1) * LANE] for p in pairs])
            dnw = dnw + dnw_j
        dnw_ref[g] += dnw
        ins = (_ssd_tiles(x_ref, gs), _chunk_rows(b_ref, gs), _chunk_rows(c_ref, gs), _chunk_rows(sm_ref, gs),
               al_ref[...], db_ref[...], dp_ref[...], [st_ref[p] for p in pairs])
        _, vjp = jax.vjp(functools.partial(_ssd_chunk, _ssd_lanes(g)), *ins)
        dx, dbm, dcm, dsm, dal, ddb, ddp, dhs = vjp((dy, [dhs_ref[g, p] for p in pairs]))
        for grp in range(SSD_GROUPS):
            @pl.when(g == grp)
            def _(grp=grp):
                for j in range(gs):
                    rws = slice(j * CHUNK, (j + 1) * CHUNK)
                    for p in pairs:
                        col = grp * gw + p * LANE
                        dxc_ref[rws, col:col + LANE] = dx[j][p]
                    cb_ = XC("sB") - XC("sx") + grp * SSD_N
                    cc_ = XC("sC") - XC("sx") + grp * SSD_N
                    dxc_ref[rws, cb_:cb_ + SSD_N] = dbm[j]
                    dxc_ref[rws, cc_:cc_ + SSD_N] = dcm[j]
        for j in range(gs):
            dsm_ref[j * CHUNK:(j + 1) * CHUNK, :] = dsm[j]
        for p in pairs:
            dhs_ref[g, p] = dhs[p]
        dal_ref[g] += dal
        ddb_ref[g] += ddb
        ddp_ref[g] += ddp

    r = lambda c: steps - 1 - c
    one = pl.BlockSpec((1, LANE), lambda c, g: (0, 0))
    acc = pl.BlockSpec((SSD_GROUPS, 1, LANE), lambda c, g: (0, 0, 0))
    prm = jax.ShapeDtypeStruct((SSD_GROUPS, 1, LANE), F32)
    sz_blk = _MINE["sz"] // gw
    return pl.pallas_call(
        body, name="ssd_bwd", grid=(steps, SSD_GROUPS),
        in_specs=[pl.BlockSpec((rows, gw), lambda c, g: (r(c), XC("sx") // gw + g)),
                  pl.BlockSpec((rows, LANE), lambda c, g: (r(c), XC("sB") // LANE + g)),
                  pl.BlockSpec((rows, LANE), lambda c, g: (r(c), XC("sC") // LANE + g)),
                  pl.BlockSpec((rows, LANE), lambda c, g: (r(c), SMALL_OFF // LANE)),
                  one, one, one,
                  pl.BlockSpec((rows, gw), lambda c, g: (r(c), sz_blk + g)),
                  pl.BlockSpec((None, 1, gw), lambda c, g: (g, 0, 0)),
                  pl.BlockSpec((None, ppg, None, LANE, SSD_N), lambda c, g: (g, 0, r(c), 0, 0)),
                  pl.BlockSpec((rows, gw), lambda c, g: (r(c), g)),
                  pl.BlockSpec((rows, gw), lambda c, g: (r(c), 1 + g)),
                  pl.BlockSpec(memory_space=pl.ANY), pl.BlockSpec(memory_space=pl.ANY)],
        out_specs=[pl.BlockSpec((rows, CONV_CH), lambda c, g: (r(c), 1)),
                   pl.BlockSpec((rows, gw), lambda c, g: (r(c), sz_blk + g)),
                   pl.BlockSpec((None, rows, LANE), lambda c, g: (g, r(c), 0)), acc, acc, acc,
                   pl.BlockSpec((SSD_GROUPS, 1, gw), lambda c, g: (0, 0, 0))],
        out_shape=[jax.ShapeDtypeStruct(dxc.shape, dxc.dtype), jax.ShapeDtypeStruct(dproj.shape, dproj.dtype),
                   jax.ShapeDtypeStruct((SSD_GROUPS, length, LANE), F32), prm, prm, prm,
                   jax.ShapeDtypeStruct((SSD_GROUPS, 1, gw), F32)],
        input_output_aliases={12: 1, 13: 0},
        scratch_shapes=[pltpu.VMEM((SSD_GROUPS, ppg, LANE, SSD_N), F32)],
        compiler_params=_cparams(("arbitrary", "arbitrary"), VMEM_LIMIT),
    )(xc, xc, xc, proj, alog_row, dtb_row, d_row, proj, norm_w, st, y_b, dmix, dproj, dxc)


def _ret_fwd(proj, cos2, sin2, dmat, qdec, kdec, cdec, nw, mixed, length):
    gs, steps, rows = _scan_steps(length)
    h_ = RET_HEADS

    def body(q_ref, k_ref, v_ref, g_ref, cos_ref, sin_ref, dm_ref, qd_ref, kd_ref, cd_ref, nw_ref, _,
             on_ref, st_ref, r_ref):
        @pl.when(pl.program_id(0) == 0)
        def _():
            r_ref[...] = jnp.zeros_like(r_ref)

        st_ref[...] = r_ref[...]
        lead = lambda ref: [ref[h] for h in range(h_)]
        on, r_new = _ret_chunk(_head_tiles(q_ref, gs), _head_tiles(k_ref, gs), _head_tiles(v_ref, gs),
                               _head_tiles(g_ref, gs), _chunk_rows(cos_ref, gs), _chunk_rows(sin_ref, gs),
                               lead(dm_ref), lead(qd_ref), lead(kd_ref), lead(cd_ref), nw_ref[...], lead(r_ref))
        for h in range(h_):
            for j in range(gs):
                on_ref[j * CHUNK:(j + 1) * CHUNK, h * LANE:(h + 1) * LANE] = on[j][h].astype(on_ref.dtype)
            r_ref[h] = r_new[h]

    hw = h_ * LANE
    blk = lambda name: pl.BlockSpec((rows, hw), lambda c: (c, _MINE[name] // hw))
    tab = pl.BlockSpec((rows, LANE), lambda c: (c, 0))
    full = lambda *s: pl.BlockSpec(s, lambda c: (0,) * len(s))
    return pl.pallas_call(
        body, name="ret_fwd", grid=(steps,),
        in_specs=[blk("rq"), blk("rk"), blk("rv"), blk("rg"), tab, tab,
                  full(h_, CHUNK, CHUNK), full(h_, CHUNK, 1), full(h_, CHUNK, 1), full(h_, 1, 1), full(1, LANE),
                  pl.BlockSpec(memory_space=pl.ANY)],
        out_specs=[pl.BlockSpec((rows, hw), lambda c: (c, (MIX_W - hw) // hw)),
                   pl.BlockSpec((h_, None, LANE, LANE), lambda c: (0, c, 0, 0))],
        out_shape=[jax.ShapeDtypeStruct(mixed.shape, mixed.dtype),
                   jax.ShapeDtypeStruct((h_, steps, LANE, LANE), F32)],
        input_output_aliases={11: 0},
        scratch_shapes=[pltpu.VMEM((h_, LANE, LANE), F32)],
        compiler_params=_cparams(("arbitrary",), VMEM_LIMIT),
    )(proj, proj, proj, proj, cos2, sin2, dmat, qdec, kdec, cdec, nw, mixed)


def _ret_bwd(proj, cos2, sin2, dmat, qdec, kdec, cdec, nw, st, dmix, dproj, length):
    gs, steps, rows = _scan_steps(length)
    h_ = RET_HEADS
    hw = h_ * LANE

    def body(q_ref, k_ref, v_ref, g_ref, cos_ref, sin_ref, dm_ref, qd_ref, kd_ref, cd_ref, nw_ref, st_ref,
             do_ref, _, dqkvg_ref, dnw_ref, dr_ref):
        @pl.when(pl.program_id(0) == 0)
        def _():
            dr_ref[...] = jnp.zeros_like(dr_ref)
            dnw_ref[...] = jnp.zeros_like(dnw_ref)

        lead = lambda ref: [ref[h] for h in range(h_)]
        consts = (_chunk_rows(cos_ref, gs), _chunk_rows(sin_ref, gs), lead(dm_ref), lead(qd_ref), lead(kd_ref),
                  lead(cd_ref))
        f = lambda q, k, v, g, w_, r_: _ret_chunk(q, k, v, g, *consts, w_, r_)
        _, vjp = jax.vjp(f, _head_tiles(q_ref, gs), _head_tiles(k_ref, gs), _head_tiles(v_ref, gs),
                         _head_tiles(g_ref, gs), nw_ref[...], lead(st_ref))
        dq, dk, dv, dg, dnw, dr = vjp((_head_tiles(do_ref, gs), lead(dr_ref)))
        for h in range(h_):
            for j in range(gs):
                rws = slice(j * CHUNK, (j + 1) * CHUNK)
                for part, d in enumerate((dq, dk, dv, dg)):
                    col = part * hw + h * LANE
                    dqkvg_ref[rws, col:col + LANE] = d[j][h].astype(dqkvg_ref.dtype)
            dr_ref[h] = dr[h]
        dnw_ref[...] += dnw

    r = lambda c: steps - 1 - c
    blk = lambda name: pl.BlockSpec((rows, hw), lambda c: (r(c), _MINE[name] // hw))
    tab = pl.BlockSpec((rows, LANE), lambda c: (r(c), 0))
    full = lambda *s: pl.BlockSpec(s, lambda c: (0,) * len(s))
    assert _MINE["rq"] % (4 * hw) == 0 and [_MINE[n] - _MINE["rq"] for n in ("rk", "rv", "rg")] == [hw, 2 * hw, 3 * hw]
    return pl.pallas_call(
        body, name="ret_bwd", grid=(steps,),
        in_specs=[blk("rq"), blk("rk"), blk("rv"), blk("rg"), tab, tab,
                  full(h_, CHUNK, CHUNK), full(h_, CHUNK, 1), full(h_, CHUNK, 1), full(h_, 1, 1), full(1, LANE),
                  pl.BlockSpec((h_, None, LANE, LANE), lambda c: (0, r(c), 0, 0)),
                  pl.BlockSpec((rows, hw), lambda c: (r(c), (MIX_W - hw) // hw)),
                  pl.BlockSpec(memory_space=pl.ANY)],
        out_specs=[pl.BlockSpec((rows, 4 * hw), lambda c: (r(c), _MINE["rq"] // (4 * hw))), full(1, LANE)],
        out_shape=[jax.ShapeDtypeStruct(dproj.shape, dproj.dtype), jax.ShapeDtypeStruct((1, LANE), F32)],
        input_output_aliases={13: 0},
        scratch_shapes=[pltpu.VMEM((h_, LANE, LANE), F32)],
        compiler_params=_cparams(("arbitrary",), VMEM_LIMIT),
    )(proj, proj, proj, proj, cos2, sin2, dmat, qdec, kdec, cdec, nw, st, dmix, dproj)


def _mesh_pos():
    x, y, c = lax.axis_index("x"), lax.axis_index("y"), lax.axis_index("c")
    return x, y, c, 4 * x + 2 * y + c


def _peer(x, y, c, mask):
    return (x ^ ((mask >> 2) & 1), y ^ ((mask >> 1) & 1), c ^ (mask & 1))


def _gather_plan(ins, outs, send_sems, recv_sems, local_sems):
    x, y, c, me = _mesh_pos()
    n = len(ins)
    idx = lambda px, py, pc: 4 * px + 2 * py + pc
    sib = (x, y, 1 - c)
    chips = [(1 - x, y), (x, 1 - y), (1 - x, 1 - y)]
    local = [pltpu.make_async_copy(ins[a], outs[a].at[me], local_sems.at[a]) for a in range(n)]

    def copy(a, k, src, slab, to):
        return pltpu.make_async_remote_copy(
            src_ref=src, dst_ref=outs[a].at[slab], send_sem=send_sems.at[7 * a + k],
            recv_sem=recv_sems.at[7 * a + k], device_id=to, device_id_type=pl.DeviceIdType.MESH)

    first = [copy(a, 0, ins[a], me, sib) for a in range(n)]
    first += [copy(a, 1 + j, ins[a], me, (*chip, c)) for a in range(n) for j, chip in enumerate(chips)]
    landed = [[copy(a, 1 + j, ins[a], idx(*chip, c), (*chip, c)) for a in range(n)] for j, chip in enumerate(chips)]
    passed = [[copy(a, 4 + j, outs[a].at[idx(*chip, c)], idx(*chip, c), sib) for a in range(n)]
              for j, chip in enumerate(chips)]
    from_sib = [copy(a, 0, ins[a], idx(*sib), sib) for a in range(n)]
    from_sib += [copy(a, 4 + j, ins[a], idx(*chip, 1 - c), sib) for a in range(n) for j, chip in enumerate(chips)]

    def start():
        for cp in local + first:
            cp.start()

    def wait():
        for j in range(len(chips)):
            for a in range(n):
                landed[j][a].wait_recv()
                passed[j][a].start()
        for cp in from_sib:
            cp.wait_recv()
        for cp in first + [cp for row in passed for cp in row]:
            cp.wait_send()
        for cp in local:
            cp.wait()

    return start, wait


def _comm_plan(kind, ins, outs, send_sems, recv_sems, local_sems):
    if kind == "gather":
        return _gather_plan(ins, outs, send_sems, recv_sems, local_sems)
    x, y, c, me = _mesh_pos()
    n = len(ins)
    src = (lambda a, idx: ins[a]) if kind == "gather" else (lambda a, idx: ins[a].at[idx])
    local = [pltpu.make_async_copy(src(a, me), outs[a].at[me], local_sems.at[a]) for a in range(n)]
    sends, recvs = [], []
    for mask in range(1, N_DEV):
        px, py, pc = _peer(x, y, c, mask)
        pidx = 4 * px + 2 * py + pc
        for a in range(n):
            k = (mask - 1) * n + a
            copy = lambda s, d, k=k: pltpu.make_async_remote_copy(
                src_ref=s, dst_ref=d, send_sem=send_sems.at[k], recv_sem=recv_sems.at[k],
                device_id=(px, py, pc), device_id_type=pl.DeviceIdType.MESH)
            sends.append(copy(src(a, pidx), outs[a].at[me]))
            recvs.append(copy(src(a, me), outs[a].at[pidx]))

    def start():
        for cp in local + sends:
            cp.start()

    def wait():
        for cp in sends:
            cp.wait_send()
        for cp in recvs:
            cp.wait_recv()
        for cp in local:
            cp.wait()

    return start, wait


def _comm_out_shapes(kind, arrays):
    return [jax.ShapeDtypeStruct(((N_DEV,) + s.shape) if kind == "gather" else s.shape, s.dtype) for s in arrays]


def _comm_scratch(n):
    k_tot = (N_DEV - 1) * n
    return [pltpu.SemaphoreType.DMA((k_tot,)), pltpu.SemaphoreType.DMA((k_tot,)), pltpu.SemaphoreType.DMA((n,))]


def _comm_call(name, kind, arrays):
    n = len(arrays)

    def body(*refs):
        start, wait = _comm_plan(kind, refs[:n], refs[n:2 * n], *refs[2 * n:])
        start()
        wait()

    hbm = pl.BlockSpec(memory_space=pl.ANY)
    return pl.pallas_call(
        body, name=name, in_specs=[hbm] * n, out_specs=[hbm] * n, out_shape=_comm_out_shapes(kind, arrays),
        scratch_shapes=_comm_scratch(n), compiler_params=pltpu.CompilerParams(has_side_effects=True),
    )(*arrays)


def _adam(name, parts, w, m, v, tr, layer=None, prev=None):
    p_, r_, cw = parts.shape
    c1 = 1.0 - ADAM_B1 ** ADAM_STEP
    c2 = 1.0 - ADAM_B2 ** ADAM_STEP
    n_prev = 4 if prev else 0

    def body(p_ref, w_ref, m_ref, v_ref, *rest):
        g_ref, d_ref, nm_ref, nv_ref = rest[n_prev:]
        g = p_ref[0].astype(F32)
        for i in range(1, p_):
            g = g + p_ref[i].astype(F32)
        nm = ADAM_B1 * m_ref[...] + (1.0 - ADAM_B1) * g
        nv = ADAM_B2 * v_ref[...] + (1.0 - ADAM_B2) * (g * g)
        d_ref[...] = -ADAM_LR * ((nm / c1) / (jnp.sqrt(nv / c2) + ADAM_EPS) + ADAM_WD * w_ref[...])
        g_ref[...] = g
        nm_ref[...] = nm
        nv_ref[...] = nv

    if layer is None:
        spec = pl.BlockSpec((tr, cw), lambda i: (i, 0))
        shp = jax.ShapeDtypeStruct((r_, cw), F32)
    else:
        spec = pl.BlockSpec((None, tr, cw), lambda i: (layer, i, 0))
        shp = jax.ShapeDtypeStruct((DEPTH, r_, cw), F32)
    return pl.pallas_call(
        body, name=name, grid=(r_ // tr,),
        in_specs=[pl.BlockSpec((p_, tr, cw), lambda i: (0, i, 0)), spec, spec, spec]
                 + [pl.BlockSpec(memory_space=pl.ANY)] * n_prev,
        out_specs=[spec] * 4, out_shape=[shp] * 4,
        input_output_aliases={4 + i: i for i in range(n_prev)},
        compiler_params=_cparams(("parallel",), VMEM_LIMIT),
    )(parts, w, m, v, *(prev or ()))


def _shards_to_perm(g_in):
    parts = []
    for n in _ORDER:
        lo, hi = _ORIG[n][0], _ORIG[n][0] + _ORIG[n][1]
        for j in range(lo // SHARD_IN, (hi - 1) // SHARD_IN + 1):
            a, b = max(lo, j * SHARD_IN), min(hi, (j + 1) * SHARD_IN)
            parts.append(g_in[j][:, a - j * SHARD_IN:b - j * SHARD_IN])
    parts.append(jnp.zeros((g_in.shape[1], N_PAD - N_USED), g_in.dtype))
    return jnp.concatenate(parts, axis=1)


def _perm_to_shards(g):
    by_orig = sorted(_ORDER, key=lambda n: _ORIG[n][0])
    slabs = []
    for j in range(N_DEV):
        lo, hi = j * SHARD_IN, (j + 1) * SHARD_IN
        parts = []
        for n in by_orig:
            a, b = max(lo, _ORIG[n][0]), min(hi, _ORIG[n][0] + _ORIG[n][1])
            if a < b:
                parts.append(g[:, _MINE[n] + a - _ORIG[n][0]:_MINE[n] + b - _ORIG[n][0]])
        slabs.append(jnp.concatenate(parts, axis=1))
    return jnp.stack(slabs, axis=0)


def _ret_consts(length):
    lg = jnp.log(1.0 - 2.0 ** (-5.0 - jnp.arange(RET_HEADS, dtype=F32)))
    idx = jnp.arange(CHUNK, dtype=F32)
    rel = idx[:, None] - idx[None, :]
    dmat = jnp.where(rel[None] >= 0, jnp.exp(jnp.maximum(rel, 0.0)[None] * lg[:, None, None]), 0.0)
    qdec = jnp.exp((idx[:, None] + 1.0) * lg[None, :]).T[:, :, None]
    kdec = jnp.exp((CHUNK - 1.0 - idx)[:, None] * lg[None, :]).T[:, :, None]
    cdec = jnp.exp(CHUNK * lg)[:, None, None]
    half = RET_DK // 2
    inv = ROPE_BASE ** (-jnp.arange(half, dtype=F32) / half)
    ang = jnp.arange(length, dtype=jnp.int32).astype(F32)[:, None] * inv[None, :]
    cos, sin = jnp.cos(ang), jnp.sin(ang)
    return (jnp.concatenate([cos, cos], axis=1), jnp.concatenate([-sin, sin], axis=1), dmat, qdec, kdec, cdec)


def _tiles(length, pref=512):
    return pref if length % pref == 0 else length


def _layer_fwd(x, p, consts, length, last, target, comm=None):
    tm = _tiles(length)
    (h,), _ = _rowmap("pre_norm", lambda x_, w_: ((_rms(x_, w_),), ()), length, tm,
                      [(x, D_MODEL, 0)], [p["pre_norm"]], out_ws=[D_MODEL], out_dtypes=[BF16])
    proj = _matmul("in_proj", h, p["w_in"], "nn", length, N_PAD, D_MODEL)
    xc = _conv_fwd(proj, p["conv_w"], p["conv_b"], length, _tiles(length, 1024), 512)
    mixed, st_a, tinv, *got = _gdn_fwd(xc, proj, p["gdn_A_log"], p["gdn_dt_bias"], p["gdn_norm"], length, comm)
    y_b, mixed, st_b = _ssd_fwd(xc, proj, p["ssd_A_log"], p["ssd_dt_bias"], p["ssd_D"], p["ssd_norm"], mixed, length)
    mixed, st_c = _ret_fwd(proj, *consts, p["ret_norm"], mixed, length)
    w_out = p["w_out"](got) if callable(p["w_out"]) else p["w_out"]
    out = _matmul("out_proj", mixed, w_out, "nn", length, D_MODEL, MIX_W)
    res = dict(x=x, h=h, proj=proj, xc=xc, st_a=st_a, tinv=tinv, st_b=st_b, st_c=st_c, y_b=y_b, mixed=mixed,
               out=out, w_out=w_out)
    if not last:
        (y,), _ = _rowmap("post_norm", lambda x_, o_, w_: ((x_ + _rms(o_, w_),), ()), length, tm,
                          [(x, D_MODEL, 0), (out, D_MODEL, 0)], [p["post_norm"]], out_ws=[D_MODEL])
        return y, res, None, got

    def head(x_, o_, t_, w_):
        normed, post_vjp = jax.vjp(_rms, o_, w_)
        e = x_ + normed - t_
        row = jnp.mean(e * e, axis=-1, keepdims=True)
        loss = 0.5 * jnp.sum(row, axis=0, keepdims=True)
        dy_ = e * (1.0 / D_MODEL)
        d_out, dw = post_vjp(dy_)
        return (dy_, d_out), (loss + jnp.zeros((1, LANE), F32), dw)

    (dy, d_out), (loss, dw_post) = _rowmap(
        "loss_head", head, length, tm, [(x, D_MODEL, 0), (out, D_MODEL, 0), (target, D_MODEL, 0)], [p["post_norm"]],
        out_ws=[D_MODEL, D_MODEL], out_dtypes=[F32, BF16], acc_ws=[LANE, D_MODEL])
    res["post_bwd"] = (d_out, dw_post)
    return None, res, (dy, loss[0, 0]), got


def _layer_bwd(dy, p, res, consts, length, comm_scan=None, comm_dx=None):
    tm = _tiles(length)
    g = {}
    if "post_bwd" in res:
        d_out, g["post_norm"] = res["post_bwd"]
    else:
        (d_out,), (g["post_norm"],) = _rowmap("post_norm_bwd", lambda o_, w_: ((_rms(o_, w_),), ()), length, tm,
                                              [(res["out"], D_MODEL, 0)], [p["post_norm"]], cts=[(dy, D_MODEL, 0)],
                                              d_dtypes=[BF16])
    dmix = _matmul("out_proj_dx", d_out, res["w_out"], "nt", length, MIX_W, D_MODEL)
    g["w_out"] = _matmul("out_proj_dw", res["mixed"], d_out, "tn", MIX_W, D_MODEL, length, out_dtype=BF16)
    comm = comm_scan(g) if comm_scan else None
    proj, xc = res["proj"], res["xc"]
    (dxc, dproj, dsm_a, g["gdn_A_log"], g["gdn_dt_bias"], g["gdn_norm"], *got) = _gdn_bwd(
        xc, proj, p["gdn_A_log"], p["gdn_dt_bias"], p["gdn_norm"], res["st_a"], res["tinv"], dmix, length, comm)
    dxc, dproj, dsm_b, g["ssd_A_log"], g["ssd_dt_bias"], g["ssd_D"], g["ssd_norm"] = _ssd_bwd(
        xc, proj, p["ssd_A_log"], p["ssd_dt_bias"], p["ssd_D"], p["ssd_norm"], res["st_b"], res["y_b"], dmix,
        dproj, dxc, length)
    dproj, g["ret_norm"] = _ret_bwd(proj, *consts, p["ret_norm"], res["st_c"], dmix, dproj, length)
    tmc = _tiles(length, 1024)
    dpre, g["conv_w"], g["conv_b"] = _conv_bwd_pre(proj, p["conv_w"], p["conv_b"], dxc, length, tmc, 512)
    dproj = _conv_bwd_x(dpre, p["conv_w"], dproj, length, tmc, 512)
    pad = SMALL_OFF % 512 + 512 - LANE
    (dproj,), _ = _rowmap("dsmall", lambda a_, b_, c_: ((jnp.concatenate([a_ + b_ + c_, jnp.zeros((a_.shape[0], pad), F32)], axis=1),), ()),
                          length, tm, [(dsm_a, LANE, 0), (dsm_b[0], LANE, 0), (dsm_b[1], LANE, 0)], [],
                          out_ws=[512], out_dtypes=[BF16], place={0: (dproj, N_PAD, SMALL_OFF // 512)})
    g["w_in"] = _matmul("in_proj_dw", res["h"], dproj, "tn", D_MODEL, N_PAD, length, out_dtype=BF16)
    if comm_dx:
        dh, *got_dx = _matmul("in_proj_dx_scatter", dproj, p["w_in"], "nt", length, D_MODEL, N_PAD, comm=comm_dx(g))
    else:
        dh, got_dx = _matmul("in_proj_dx", dproj, p["w_in"], "nt", length, D_MODEL, N_PAD), []
    (dx,), (g["pre_norm"],) = _rowmap(
        "pre_norm_bwd", lambda x_, w_: ((_rms(x_, w_), x_), ()), length, tm,
        [(res["x"], D_MODEL, 0)], [p["pre_norm"]], cts=[(dh, D_MODEL, 0), (dy, D_MODEL, 0)])
    return dx, g, got, got_dx


def _lane_row(vals, piece):
    lo = _MINE[piece] - SMALL_OFF
    return jnp.pad(vals[None], ((0, 0), (lo, LANE - lo - vals.shape[0])))


def _from_lane_row(row, piece, n):
    lo = _MINE[piece] - SMALL_OFF
    return row.reshape(-1, LANE).sum(axis=0)[lo:lo + n]


def _make_layer(l, w_in_perm, w_out_full, conv_full, sw):
    return dict(
        pre_norm=sw["pre_norm"][l][None], post_norm=sw["post_norm"][l][None], w_in=w_in_perm,
        w_out=w_out_full,
        conv_w=jnp.concatenate([conv_full[0], conv_full[1]], axis=1),
        conv_b=jnp.concatenate([jnp.zeros((CONV_CH,), F32), sw["ssd_conv_b"][l]])[None],
        gdn_A_log=_lane_row(sw["gdn_A_log"][l], "ga"), gdn_dt_bias=_lane_row(sw["gdn_dt_bias"][l], "ga"),
        gdn_norm=sw["gdn_norm"][l][None],
        ssd_A_log=_lane_row(sw["ssd_A_log"][l], "sdt"), ssd_dt_bias=_lane_row(sw["ssd_dt_bias"][l], "sdt"),
        ssd_D=_lane_row(sw["ssd_D"][l], "sdt"),
        ssd_norm=sw["ssd_norm"][l].reshape(SSD_GROUPS, 1, -1),
        ret_norm=sw["ret_norm"][l][None])


def _local_step(xb, tgt, layer0, make_layer1, length, fwd_comm=None, comm_scan=None, comm_dx=None):
    consts = _ret_consts(length)
    y0, res0, _, got = _layer_fwd(xb, layer0, consts, length, False, tgt, fwd_comm)
    layer1 = make_layer1(got)
    _, res1, (dy, loss_local), _ = _layer_fwd(y0, layer1, consts, length, True, tgt)
    dy, grads1, _, _ = _layer_bwd(dy, layer1, res1, consts, length)
    dx, grads0, recv_scan, recv_dx = _layer_bwd(
        dy, layer0, res0, consts, length,
        (lambda g0: comm_scan(grads1, g0)) if comm_scan else None, comm_dx)
    return loss_local, dx, [grads0, grads1], recv_scan, recv_dx


_SMALL = ["pre_norm", "post_norm", "gdn_A_log", "gdn_dt_bias", "gdn_norm", "ssd_conv_b", "ssd_A_log",
          "ssd_dt_bias", "ssd_D", "ssd_norm", "ret_norm"]


def _pack_small(arrs):
    rows = []
    for n in _SMALL:
        flat = arrs[n].reshape(-1)
        pad = (-flat.shape[0]) % LANE
        rows.append(jnp.pad(flat, (0, pad)).reshape(-1, LANE))
    out = jnp.concatenate(rows, axis=0)
    return jnp.pad(out, ((0, (-out.shape[0]) % SUBLANE), (0, 0)))


def _unpack_small(packed, like):
    out, r = {}, 0
    for n in _SMALL:
        cnt = like[n].size
        nrow = -(-cnt // LANE)
        out[n] = packed[r:r + nrow].reshape(-1)[:cnt].reshape(like[n].shape)
        r += nrow
    return out


def kernel(x, pre_norm, post_norm, w_in, gdn_conv, gdn_A_log, gdn_dt_bias, gdn_norm, ssd_conv, ssd_conv_b, ssd_A_log, ssd_dt_bias, ssd_D, ssd_norm, ret_norm, w_out, loss_target, m_pre_norm, m_post_norm, m_w_in, m_gdn_conv, m_gdn_A_log, m_gdn_dt_bias, m_gdn_norm, m_ssd_conv, m_ssd_conv_b, m_ssd_A_log, m_ssd_dt_bias, m_ssd_D, m_ssd_norm, m_ret_norm, m_w_out, v_pre_norm, v_post_norm, v_w_in, v_gdn_conv, v_gdn_A_log, v_gdn_dt_bias, v_gdn_norm, v_ssd_conv, v_ssd_conv_b, v_ssd_A_log, v_ssd_dt_bias, v_ssd_D, v_ssd_norm, v_ret_norm, v_w_out):
    length = x.shape[1]
    xb = x[0]
    tgt = loss_target[0]
    small_w = dict(pre_norm=pre_norm, post_norm=post_norm, gdn_A_log=gdn_A_log, gdn_dt_bias=gdn_dt_bias,
                   gdn_norm=gdn_norm, ssd_conv_b=ssd_conv_b, ssd_A_log=ssd_A_log, ssd_dt_bias=ssd_dt_bias,
                   ssd_D=ssd_D, ssd_norm=ssd_norm, ret_norm=ret_norm)
    small_m = dict(pre_norm=m_pre_norm, post_norm=m_post_norm, gdn_A_log=m_gdn_A_log, gdn_dt_bias=m_gdn_dt_bias,
                   gdn_norm=m_gdn_norm, ssd_conv_b=m_ssd_conv_b, ssd_A_log=m_ssd_A_log, ssd_dt_bias=m_ssd_dt_bias,
                   ssd_D=m_ssd_D, ssd_norm=m_ssd_norm, ret_norm=m_ret_norm)
    small_v = dict(pre_norm=v_pre_norm, post_norm=v_post_norm, gdn_A_log=v_gdn_A_log, gdn_dt_bias=v_gdn_dt_bias,
                   gdn_norm=v_gdn_norm, ssd_conv_b=v_ssd_conv_b, ssd_A_log=v_ssd_A_log, ssd_dt_bias=v_ssd_dt_bias,
                   ssd_D=v_ssd_D, ssd_norm=v_ssd_norm, ret_norm=v_ret_norm)

    w_in_b, w_out_b = w_in.astype(BF16), w_out.astype(BF16)
    conv_shard = jnp.stack([gdn_conv, ssd_conv], axis=1)
    full_out = lambda g_out: g_out.reshape(MIX_W, D_MODEL)

    def assemble(l, g_in, w_out_full, g_conv):
        conv_full = g_conv.transpose(1, 2, 0, 3).reshape(2, CONV_W, CONV_CH)
        return _make_layer(l, _shards_to_perm(g_in), w_out_full, conv_full, small_w)

    slab_in = lambda g: _perm_to_shards(g["w_in"]).astype(BF16)
    slab_out = lambda g: g["w_out"].reshape(N_DEV, SHARD_OUT, D_MODEL).astype(BF16)
    slab_conv = lambda g: jnp.stack(
        [g["conv_w"][:, k * CONV_CH:(k + 1) * CONV_CH].reshape(CONV_W, N_DEV, SHARD_CONV).transpose(1, 0, 2)
         for k in range(2)], axis=1).reshape(N_DEV, 2 * CONV_W, SHARD_CONV)

    g_in0, g_conv0 = _comm_call("gather_layer0", "gather", [w_in_b[0], conv_shard[0]])
    layer0 = assemble(0, g_in0, lambda got: full_out(got[3]), g_conv0)
    loss_local, dx, grads, recv_scan, recv_dx = _local_step(
        xb, tgt, layer0, lambda got: assemble(1, got[0], full_out(got[1]), got[2]), length,
        fwd_comm=("gather", [w_in_b[1], w_out_b[1], conv_shard[1], w_out_b[0]]),
        comm_scan=lambda g1, g0: ("scatter", [slab_in(g1), slab_out(g1), slab_conv(g1), slab_out(g0)]),
        comm_dx=lambda g0: ("scatter", [slab_in(g0), slab_conv(g0)]))
    grad_x = dx[None]
    loss = lax.psum(loss_local, ("x", "y", "c"))

    small_g = dict(
        pre_norm=jnp.concatenate([grads[l]["pre_norm"] for l in range(DEPTH)], axis=0),
        post_norm=jnp.concatenate([grads[l]["post_norm"] for l in range(DEPTH)], axis=0),
        gdn_A_log=jnp.stack([_from_lane_row(grads[l]["gdn_A_log"], "ga", GDN_HEADS) for l in range(DEPTH)]),
        gdn_dt_bias=jnp.stack([_from_lane_row(grads[l]["gdn_dt_bias"], "ga", GDN_HEADS) for l in range(DEPTH)]),
        gdn_norm=jnp.concatenate([grads[l]["gdn_norm"] for l in range(DEPTH)], axis=0),
        ssd_conv_b=jnp.concatenate([grads[l]["conv_b"][:, CONV_CH:] for l in range(DEPTH)], axis=0),
        ssd_A_log=jnp.stack([_from_lane_row(grads[l]["ssd_A_log"], "sdt", SSD_HEADS) for l in range(DEPTH)]),
        ssd_dt_bias=jnp.stack([_from_lane_row(grads[l]["ssd_dt_bias"], "sdt", SSD_HEADS) for l in range(DEPTH)]),
        ssd_D=jnp.stack([_from_lane_row(grads[l]["ssd_D"], "sdt", SSD_HEADS) for l in range(DEPTH)]),
        ssd_norm=jnp.concatenate([grads[l]["ssd_norm"].reshape(1, -1) for l in range(DEPTH)], axis=0),
        ret_norm=jnp.concatenate([grads[l]["ret_norm"] for l in range(DEPTH)], axis=0))
    gs = _pack_small(small_g)
    gs8 = jnp.broadcast_to(gs[None], (N_DEV,) + gs.shape)
    (r_small,) = _comm_call("exchange_small", "scatter", [gs8])
    recv = [[recv_dx[0], recv_scan[3], recv_dx[1]], recv_scan[:3]]

    conv_w = lambda g_, s_, l: jnp.stack([g_[l], s_[l]], axis=0).reshape(2 * CONV_W, SHARD_CONV)
    o_in, o_out, o_conv = None, None, []
    for l in range(DEPTH):
        o_in = _adam(f"adam_w_in{l}", recv[l][0], w_in, m_w_in, v_w_in, 128, l, o_in)
        o_out = _adam(f"adam_w_out{l}", recv[l][1], w_out, m_w_out, v_w_out, 128, l, o_out)
        o_conv.append(_adam(f"adam_conv{l}", recv[l][2], conv_w(gdn_conv, ssd_conv, l),
                            conv_w(m_gdn_conv, m_ssd_conv, l), conv_w(v_gdn_conv, v_ssd_conv, l), 2 * CONV_W))
    ps_w, ps_m, ps_v = _pack_small(small_w), _pack_small(small_m), _pack_small(small_v)
    o_small = _adam("adam_small", r_small, ps_w, ps_m, ps_v, ps_w.shape[0])

    names = ["pre_norm", "post_norm", "w_in", "gdn_conv", "gdn_A_log", "gdn_dt_bias", "gdn_norm", "ssd_conv",
             "ssd_conv_b", "ssd_A_log", "ssd_dt_bias", "ssd_D", "ssd_norm", "ret_norm", "w_out"]
    outs = []
    for kind in range(4):
        d = _unpack_small(o_small[kind], small_w)
        cv = jnp.stack([o_conv[l][kind].reshape(2, CONV_W, SHARD_CONV) for l in range(DEPTH)], axis=1)
        d["w_in"] = o_in[kind]
        d["w_out"] = o_out[kind]
        d["gdn_conv"] = cv[0]
        d["ssd_conv"] = cv[1]
        outs.extend(d[n] for n in names)
    return (loss, grad_x, *outs)
```

```python
import functools
import math

import jax
import jax.numpy as jnp
from jax import lax
from jax.experimental import pallas as pl
from jax.experimental.pallas import tpu as pltpu

F32 = jnp.float32
BF16 = jnp.bfloat16

D_MODEL = 1024
DEPTH = 2
CHUNK = 64
CONV_W = 4
EPS = 1e-6
N_DEV = 8

GDN_HEADS = 4
GDN_DK = 128
SSD_HEADS = 16
SSD_P = 64
SSD_N = 128
SSD_GROUPS = 2
SSD_PAIRS = SSD_HEADS // 2
PAIRS_PER_GROUP = SSD_PAIRS // SSD_GROUPS
RET_HEADS = 4
RET_DK = 128
ROPE_BASE = 10000.0
MIX_W = 2048
N_IN = 6680
SHARD_IN = N_IN // N_DEV
SHARD_OUT = MIX_W // N_DEV
CONV_CH = 1536
SHARD_CONV = CONV_CH // N_DEV

ADAM_LR = 0.001
ADAM_B1 = 0.9
ADAM_B2 = 0.999
ADAM_EPS = 1e-08
ADAM_WD = 0.01
ADAM_STEP = 10

LANE = 128
SUBLANE = 8
VMEM_LIMIT = 56 * 1024 * 1024

_ORIG = dict(gq=(0, 512), gk=(512, 512), gv=(1024, 512), gz=(1536, 512), gb=(2048, 4), ga=(2052, 4),
             sx=(2056, 1024), sB=(3080, 256), sC=(3336, 256), sz=(3592, 1024), sdt=(4616, 16),
             rq=(4632, 512), rk=(5144, 512), rv=(5656, 512), rg=(6168, 512))
_ORDER = ["rq", "rk", "rv", "rg", "gq", "gk", "gv", "sx", "sB", "sC", "gz", "sz", "gb", "ga", "sdt"]
_MINE = {}
_off = 0
for _n in _ORDER:
    _MINE[_n] = _off
    _off += _ORIG[_n][1]
N_USED = _off
N_PAD = 7168
CONV_ALL = 2 * CONV_CH
SMALL_OFF = _MINE["gb"]
CONV_OFF = _MINE["gq"]
XC = lambda name: _MINE[name] - CONV_OFF


def _cparams(sem, vmem=None):
    return pltpu.CompilerParams(dimension_semantics=sem, vmem_limit_bytes=vmem)


def _split_bf16(a):
    hi = a.astype(BF16)
    return hi, (a - hi.astype(F32)).astype(BF16)


def _make_mm():
    def raw(a, b, ca, cb):
        return lax.dot_general(a.astype(BF16), b.astype(BF16), (((ca,), (cb,)), ((), ())),
                               preferred_element_type=F32)

    @jax.custom_vjp
    def nn(a, b):
        return raw(a, b, 1, 0)

    @jax.custom_vjp
    def nt(a, b):
        return raw(a, b, 1, 1)

    @jax.custom_vjp
    def tn(a, b):
        return raw(a, b, 0, 0)

    nn.defvjp(lambda a, b: (raw(a, b, 1, 0), (a, b)), lambda r, g: (nt(g, r[1]), tn(r[0], g)))
    nt.defvjp(lambda a, b: (raw(a, b, 1, 1), (a, b)), lambda r, g: (nn(g, r[1]), tn(g, r[0])))
    tn.defvjp(lambda a, b: (raw(a, b, 0, 0), (a, b)), lambda r, g: (nt(r[1], g), nn(r[0], g)))
    return nn, nt, tn


_nn, _nt, _tn = _make_mm()


@jax.custom_vjp
def _swap_halves(t):
    return pltpu.roll(t, LANE // 2, 1)


_swap_halves.defvjp(lambda t: (pltpu.roll(t, LANE // 2, 1), None),
                    lambda _, g: (pltpu.roll(g, LANE // 2, 1),))


@jax.custom_vjp
def _split_rows(x):
    return tuple(x[i * CHUNK:(i + 1) * CHUNK] for i in range(x.shape[0] // CHUNK))


_split_rows.defvjp(lambda x: (tuple(x[i * CHUNK:(i + 1) * CHUNK] for i in range(x.shape[0] // CHUNK)), None),
                   lambda _, gs: (jnp.concatenate(gs, axis=0),))


def _dot3(a, b, ca, cb):
    dot = lambda x, y: lax.dot_general(x, y, (((ca,), (cb,)), ((), ())), preferred_element_type=F32)
    return dot(a[0], b[0]) + (dot(a[0], b[1]) + dot(a[1], b[0]))


@jax.custom_vjp
def _tri_inv(mats):
    return _tri_inv_impl(mats)


def _tri_inv_impl(mats):
    ii = lax.broadcasted_iota(jnp.int32, mats[0].shape, 0)
    jj = lax.broadcasted_iota(jnp.int32, mats[0].shape, 1)
    eye = jnp.where(ii == jj, 1.0, 0.0).astype(F32)
    ts = [eye - a for a in mats]
    ps = [_split_bf16(-a) for a in mats]
    for _ in range(int(math.log2(CHUNK)) - 1):
        ps = [_split_bf16(_dot3(p, p, 1, 0)) for p in ps]
        ts = [t + _dot3(_split_bf16(t), p, 1, 0) for t, p in zip(ts, ps)]
    return ts


def _tri_inv_bwd(ts, gs):
    tsp = [_split_bf16(t) for t in ts]
    xs = [_dot3(t, _split_bf16(g), 0, 0) for t, g in zip(tsp, gs)]
    return ([-_dot3(_split_bf16(x), t, 1, 1) for x, t in zip(xs, tsp)],)


_tri_inv.defvjp(lambda mats: (lambda ts: (ts, ts))(_tri_inv_impl(mats)), _tri_inv_bwd)


@jax.custom_vjp
def _tri_inv_saved(mats, ts):
    return ts


_tri_inv_saved.defvjp(lambda mats, ts: (ts, ts),
                      lambda ts, gs: (_tri_inv_bwd(ts, gs)[0], [jnp.zeros_like(t) for t in ts]))


def _silu(x):
    return x * jax.nn.sigmoid(x)


@jax.custom_vjp
def _softplus(x):
    return jnp.maximum(x, 0.0) + jnp.log1p(jnp.exp(-jnp.abs(x)))


_softplus.defvjp(lambda x: (jnp.maximum(x, 0.0) + jnp.log1p(jnp.exp(-jnp.abs(x))), x),
                 lambda x, g: (g * jax.nn.sigmoid(x),))


def _rms(x, w):
    return x * lax.rsqrt(jnp.mean(x * x, axis=-1, keepdims=True) + EPS) * w


def _chunk_masks(n):
    ii = lax.broadcasted_iota(jnp.int32, (n, n), 0)
    jj = lax.broadcasted_iota(jnp.int32, (n, n), 1)
    return ii >= jj, ii > jj, ii == jj, ii <= jj


def _cumsum_col(g, causal, eye, upper):
    g_row = jnp.sum(jnp.where(eye, g, 0.0), axis=0, keepdims=True)
    col = jnp.sum(jnp.where(causal, g_row, 0.0), axis=1, keepdims=True)
    row = jnp.sum(jnp.where(upper, g, 0.0), axis=0, keepdims=True)
    return col, row


def _lane_col(block, lane):
    pick = lax.broadcasted_iota(jnp.int32, (1, block.shape[1]), 1) == lane
    return jnp.sum(jnp.where(pick, block, 0.0), axis=1, keepdims=True)


def _gdn_chunk(q, k, v, gz, sm, alog_row, dtb_row, nw, s, t_saved=None):
    subs, nh = range(len(q)), len(q[0])
    c = q[0][0].shape[0]
    causal, _, eye, upper = _chunk_masks(c)
    stack = lambda xs: jnp.concatenate(xs, axis=0)
    per = lambda f: [[f(j, h) for h in range(nh)] for j in subs]
    qn = per(lambda j, h: q[j][h] * lax.rsqrt(jnp.sum(q[j][h] * q[j][h], axis=-1, keepdims=True) + EPS)
             * (GDN_DK ** -0.5))
    kn = per(lambda j, h: k[j][h] * lax.rsqrt(jnp.sum(k[j][h] * k[j][h], axis=-1, keepdims=True) + EPS))
    beta_blk = [jax.nn.sigmoid(sm[j]) for j in subs]
    g_blk = [-jnp.exp(alog_row) * _softplus(sm[j] + dtb_row) for j in subs]
    beta = per(lambda j, h: _lane_col(beta_blk[j], _MINE["gb"] - SMALL_OFF + h))
    g = per(lambda j, h: _lane_col(g_blk[j], _MINE["ga"] - SMALL_OFF + h))
    gcum = per(lambda j, h: _cumsum_col(g[j][h], causal, eye, upper)[0])
    eg = per(lambda j, h: jnp.exp(gcum[j][h]))
    glast = per(lambda j, h: jnp.sum(g[j][h], axis=0, keepdims=True))
    kb = per(lambda j, h: kn[j][h] * beta[j][h])
    n = nh * c
    ii = lax.broadcasted_iota(jnp.int32, (n, n), 0)
    jj = lax.broadcasted_iota(jnp.int32, (n, n), 1)
    sh = int(math.log2(c))
    same = lax.shift_right_logical(ii, sh) == lax.shift_right_logical(jj, sh)
    causal_bd = jnp.logical_and(same, ii >= jj)
    strict_bd = jnp.logical_and(same, ii > jj)
    gcum_all = [stack(gcum[j]) for j in subs]
    gcum_row = [jnp.sum(jnp.where(ii == jj, gcum_all[j], 0.0), axis=0, keepdims=True) for j in subs]
    decay = [jnp.where(causal_bd, jnp.exp(jnp.where(causal_bd, gcum_all[j] - gcum_row[j], 0.0)), 0.0) for j in subs]
    kn_all = [stack(kn[j]) for j in subs]
    a_low = [jnp.where(strict_bd, _nt(stack(kb[j]), kn_all[j]) * decay[j], 0.0) for j in subs]
    t = _tri_inv(a_low) if t_saved is None else _tri_inv_saved(a_low, t_saved)
    u = [_nn(t[j], stack([v[j][h] * beta[j][h] for h in range(nh)])) for j in subs]
    w = [_split_rows(_nn(t[j], stack([kb[j][h] * eg[j][h] for h in range(nh)]))) for j in subs]
    attn = [_nt(stack(qn[j]), kn_all[j]) * decay[j] for j in subs]
    kdec = per(lambda j, h: kn[j][h] * jnp.exp(glast[j][h] - gcum[j][h]))
    on = []
    for j in subs:
        v_new_all = u[j] - stack([_nn(w[j][h], s[h]) for h in range(nh)])
        o = _split_rows(stack([_nn(qn[j][h] * eg[j][h], s[h]) for h in range(nh)]) + _nn(attn[j], v_new_all))
        v_new = _split_rows(v_new_all)
        s = [s[h] * jnp.exp(glast[j][h]) + _tn(kdec[j][h], v_new[h]) for h in range(nh)]
        on.append([_rms(o[h], nw) * _silu(gz[j][h]) for h in range(nh)])
    return on, s, t


def _ssd_chunk(lanes, x, bm, cm, sm, alog_row, dtb_row, d_row, hs):
    subs, pairs = range(len(x)), range(len(x[0]))
    c = x[0][0].shape[0]
    lane_i = lax.broadcasted_iota(jnp.int32, (c, LANE), 1)
    lane_lo = lane_i < SSD_P
    lane_lo1 = lax.broadcasted_iota(jnp.int32, (1, LANE), 1) < SSD_P
    row_i = lax.broadcasted_iota(jnp.int32, (c, LANE), 0)
    causal2 = row_i >= jnp.bitwise_and(lane_i, c - 1)
    diag2 = row_i == jnp.bitwise_and(lane_i, c - 1)
    to_row = lambda blk: jnp.sum(jnp.where(diag2, blk, 0.0), axis=0, keepdims=True)
    per = lambda f: [[f(j, p) for p in pairs] for j in subs]
    both = lambda blk, p: [_lane_col(blk, lanes[p][h]) for h in range(2)]
    dt_blk = [_softplus(sm[j] + dtb_row) for j in subs]
    a_blk = [dt_blk[j] * -jnp.exp(alog_row) for j in subs]
    alast_blk = [jnp.sum(a_blk[j], axis=0, keepdims=True) for j in subs]
    dt = per(lambda j, p: both(dt_blk[j], p))
    nexp = [both(-jnp.exp(alog_row), p) for p in pairs]
    alast = per(lambda j, p: both(alast_blk[j], p))
    dp = [both(d_row, p) for p in pairs]
    dt_lane = per(lambda j, p: jnp.where(lane_lo, dt[j][p][0], dt[j][p][1]))
    a_lane = per(lambda j, p: dt_lane[j][p] * jnp.where(lane_lo1, nexp[p][0], nexp[p][1]))
    a_row = per(lambda j, p: to_row(a_lane[j][p]))
    part = per(lambda j, p: jnp.where(causal2, a_row[j][p], 0.0))
    acum = per(lambda j, p: [jnp.sum(jnp.where(lane_lo, part[j][p], 0.0), axis=1, keepdims=True),
                             jnp.sum(jnp.where(lane_lo, 0.0, part[j][p]), axis=1, keepdims=True)])
    acum_col = per(lambda j, p: jnp.where(lane_lo, acum[j][p][0], acum[j][p][1]))
    acum_row = per(lambda j, p: to_row(acum_col[j][p]))
    lmat = per(lambda j, p: jnp.where(causal2, jnp.exp(jnp.where(causal2, acum_col[j][p] - acum_row[j][p], 0.0)), 0.0))
    cb = [_nt(cm[j], jnp.concatenate([bm[j], bm[j]], axis=0)) for j in subs]
    xdt = per(lambda j, p: x[j][p] * dt_lane[j][p])
    xdt_rows = per(lambda j, p: jnp.concatenate([jnp.where(lane_lo, xdt[j][p], 0.0),
                                                 jnp.where(lane_lo, 0.0, xdt[j][p])], axis=0))
    intra = per(lambda j, p: _nn(cb[j] * lmat[j][p], xdt_rows[j][p]))
    skip = per(lambda j, p: x[j][p] * jnp.where(lane_lo1, dp[p][0], dp[p][1]))
    eacc = per(lambda j, p: jnp.exp(acum_col[j][p]))
    alast_row = per(lambda j, p: jnp.where(lane_lo1, alast[j][p][0], alast[j][p][1]))
    wdec = per(lambda j, p: jnp.exp(alast_row[j][p] - acum_col[j][p]))
    scale = per(lambda j, p: jnp.exp(alast_row[j][p]))
    upd = per(lambda j, p: _tn(bm[j], xdt[j][p] * wdec[j][p]))
    y = []
    for j in subs:
        y.append([skip[j][p] + intra[j][p] + _nn(cm[j], hs[p]) * eacc[j][p] for p in pairs])
        hs = [hs[p] * scale[j][p] + upd[j][p] for p in pairs]
    return y, hs


def _ret_chunk(rq, rk, rv, rg, cos2, sin2, dmat, qdec, kdec, cdec, nw, r):
    subs, hs = range(len(rq)), range(len(rq[0]))
    per = lambda f: [[f(j, h) for h in hs] for j in subs]
    q = per(lambda j, h: rq[j][h] * cos2[j] + _swap_halves(rq[j][h]) * sin2[j])
    k = per(lambda j, h: (rk[j][h] * cos2[j] + _swap_halves(rk[j][h]) * sin2[j]) * (RET_DK ** -0.5))
    s = per(lambda j, h: _nt(q[j][h], k[j][h]) * dmat[h])
    intra = per(lambda j, h: _nn(s[j][h], rv[j][h]))
    upd = per(lambda j, h: _tn(k[j][h] * kdec[h], rv[j][h]))
    gate = per(lambda j, h: _silu(rg[j][h]))
    on = []
    for j in subs:
        on.append([_rms(intra[j][h] + _nn(q[j][h], r[h]) * qdec[h], nw) * gate[j][h] for h in hs])
        r = [r[h] * cdec[h] + upd[j][h] for h in hs]
    return on, r


MM_TILE = 1024


def _tile(n, pref=MM_TILE):
    return pref if n % pref == 0 else n


def _matmul(name, a, b, mode, m, n, k, tk=None, comm=None, out_dtype=F32):
    tm, tn = _tile(m), _tile(n)
    tk = _tile(k) if tk is None else tk
    nk = k // tk
    ca, cb = {"nn": (1, 0), "nt": (1, 1), "tn": (0, 0)}[mode]
    grid = (m // tm, n // tn, nk)
    n_comm = len(comm[1]) if comm else 0
    use_acc = out_dtype != F32 and nk > 1

    def at_step(which):
        return functools.reduce(jnp.logical_and, [pl.program_id(d) == (0 if which == 0 else grid[d] - 1)
                                                  for d in range(3)])

    def body(*refs):
        a_ref, b_ref = refs[:2]
        c_in = refs[2:2 + n_comm]
        o_ref = refs[2 + n_comm]
        c_out = refs[3 + n_comm:3 + 2 * n_comm]
        acc_ref = refs[3 + 2 * n_comm] if use_acc else o_ref
        sems = refs[3 + 2 * n_comm + use_acc:]
        if comm:
            @pl.when(at_step(0))
            def _():
                _comm_plan(comm[0], c_in, c_out, *sems)[0]()

        part = lax.dot_general(a_ref[...].astype(BF16), b_ref[...].astype(BF16),
                               (((ca,), (cb,)), ((), ())), preferred_element_type=F32)
        if nk == 1:
            o_ref[...] = part.astype(o_ref.dtype)
        else:
            kk = pl.program_id(2)

            @pl.when(kk == 0)
            def _():
                acc_ref[...] = part

            @pl.when(kk > 0)
            def _():
                acc_ref[...] += part

            if use_acc:
                @pl.when(kk == nk - 1)
                def _():
                    o_ref[...] = acc_ref[...].astype(o_ref.dtype)

        if comm:
            @pl.when(at_step(1))
            def _():
                _comm_plan(comm[0], c_in, c_out, *sems)[1]()

    a_spec = pl.BlockSpec((tk, tm), lambda i, j, kk: (kk, i)) if mode == "tn" else pl.BlockSpec((tm, tk), lambda i, j, kk: (i, kk))
    b_spec = pl.BlockSpec((tn, tk), lambda i, j, kk: (j, kk)) if mode == "nt" else pl.BlockSpec((tk, tn), lambda i, j, kk: (kk, j))
    hbm = pl.BlockSpec(memory_space=pl.ANY)
    res = pl.pallas_call(
        body, name=name, grid=grid,
        in_specs=[a_spec, b_spec] + [hbm] * n_comm,
        out_specs=[pl.BlockSpec((tm, tn), lambda i, j, kk: (i, j))] + [hbm] * n_comm,
        out_shape=[jax.ShapeDtypeStruct((m, n), out_dtype)] + (_comm_out_shapes(*comm) if comm else []),
        scratch_shapes=([pltpu.VMEM((tm, tn), F32)] if use_acc else []) + (_comm_scratch(n_comm) if comm else []),
        compiler_params=_cparams(("arbitrary",) * 3 if comm else ("parallel", "parallel", "arbitrary"), VMEM_LIMIT),
    )(a, b, *(comm[1] if comm else []))
    return res if comm else res[0]


def _rowmap(name, fn, length, tm, rows, params, out_ws=(), acc_ws=(), out_dtypes=None, cts=None, d_dtypes=None,
            place=None):
    nt_ = length // tm
    n_r, n_p = len(rows), len(params)
    n_in = n_r + n_p
    rspec = lambda bw, cbk: pl.BlockSpec((tm, bw), lambda i: (i, cbk))
    pspec = lambda w: pl.BlockSpec((1, w), lambda i: (0, 0))
    in_arrays = [r[0] for r in rows] + list(params)
    in_specs = [rspec(r[1], r[2]) for r in rows] + [pspec(p.shape[1]) for p in params]
    place = place or {}

    def load(refs):
        return [r[...].astype(F32) for r in refs]

    def placed(widths, dtypes, n_before):
        specs, shapes, extra, alias = [], [], [], {}
        for k, (w, dt) in enumerate(zip(widths, dtypes)):
            arr, total, cbk = place.get(k, (None, w, 0))
            specs.append(rspec(w, cbk))
            shapes.append(jax.ShapeDtypeStruct((length, total), dt))
            if arr is not None:
                alias[n_before + len(extra)] = k
                extra.append(arr)
        return specs, shapes, extra, alias

    def accumulate(a_refs, vals):
        first = pl.program_id(0) == 0
        for a_ref, v in zip(a_refs, vals):
            @pl.when(first)
            def _(a_ref=a_ref):
                a_ref[...] = jnp.zeros_like(a_ref)
            a_ref[...] += v

    hbm = pl.BlockSpec(memory_space=pl.ANY)
    if cts is None:
        n_o = len(out_ws)
        o_specs, o_shapes, extra, alias = placed(out_ws, out_dtypes or [F32] * n_o, n_in)

        def body(*refs):
            outs, accs = fn(*load(refs[:n_in]))
            o_refs = refs[n_in + len(extra):]
            for o_ref, o in zip(o_refs[:n_o], outs):
                o_ref[...] = o.astype(o_ref.dtype)
            accumulate(o_refs[n_o:], accs)

        res = pl.pallas_call(
            body, name=name, grid=(nt_,), in_specs=in_specs + [hbm] * len(extra),
            out_specs=o_specs + [pspec(w) for w in acc_ws],
            out_shape=o_shapes + [jax.ShapeDtypeStruct((1, w), F32) for w in acc_ws],
            input_output_aliases=alias, compiler_params=_cparams(("arbitrary",), VMEM_LIMIT),
        )(*in_arrays, *extra)
        return res[:n_o], res[n_o:]

    n_c = len(cts)
    o_specs, o_shapes, extra, alias = placed([r[1] for r in rows], d_dtypes or [F32] * n_r, n_in + n_c)

    def body(*refs):
        ins = load(refs[:n_in])
        _, vjp = jax.vjp(lambda rs, ps: fn(*rs, *ps)[0], ins[:n_r], ins[n_r:])
        d_rows, d_params = vjp(tuple(load(refs[n_in:n_in + n_c])))
        o_refs = refs[n_in + n_c + len(extra):]
        for o_ref, d in zip(o_refs[:n_r], d_rows):
            o_ref[...] = d.astype(o_ref.dtype)
        accumulate(o_refs[n_r:], d_params)

    res = pl.pallas_call(
        body, name=name, grid=(nt_,),
        in_specs=in_specs + [rspec(c[1], c[2]) for c in cts] + [hbm] * len(extra),
        out_specs=o_specs + [pspec(p.shape[1]) for p in params],
        out_shape=o_shapes + [jax.ShapeDtypeStruct((1, p.shape[1]), F32) for p in params],
        input_output_aliases=alias, compiler_params=_cparams(("arbitrary",), VMEM_LIMIT),
    )(*in_arrays, *[c[0] for c in cts], *extra)
    return res[:n_r], res[n_r:]


def _conv_shift(ext, k, tm, forward):
    s = CONV_W - 1 - k
    if forward:
        rolled = ext if s == 0 else pltpu.roll(ext, s, 0)
        return rolled[SUBLANE:, :]
    rolled = ext if s == 0 else pltpu.roll(ext, tm + SUBLANE - s, 0)
    return rolled[:tm, :]


def _conv_pre(x_ref, halo_ref, w_ref, b_ref, first, tm):
    halo = jnp.where(first, 0.0, halo_ref[...])
    ext = jnp.concatenate([halo, x_ref[...]], axis=0)
    w = w_ref[...]
    pre = b_ref[...] + jnp.zeros_like(x_ref[...])
    taps = []
    for k in range(CONV_W):
        tap = _conv_shift(ext, k, tm, True)
        taps.append(tap)
        pre = pre + tap * w[k:k + 1, :]
    return pre, taps


def _conv_fwd(proj, w, b, length, tm, tc):
    hb = tm // SUBLANE
    j0 = CONV_OFF // tc

    def body(x_ref, halo_ref, w_ref, b_ref, o_ref):
        pre, _ = _conv_pre(x_ref, halo_ref, w_ref, b_ref, pl.program_id(0) == 0, tm)
        o_ref[...] = _silu(pre)

    return pl.pallas_call(
        body, name="conv_fwd", grid=(length // tm, CONV_ALL // tc),
        in_specs=[pl.BlockSpec((tm, tc), lambda i, j: (i, j0 + j)),
                  pl.BlockSpec((SUBLANE, tc), lambda i, j: (jnp.maximum(i * hb - 1, 0), j0 + j)),
                  pl.BlockSpec((CONV_W, tc), lambda i, j: (0, j)),
                  pl.BlockSpec((1, tc), lambda i, j: (0, j))],
        out_specs=pl.BlockSpec((tm, tc), lambda i, j: (i, j)),
        out_shape=jax.ShapeDtypeStruct((length, CONV_ALL), F32),
        compiler_params=_cparams(("parallel", "parallel"), VMEM_LIMIT),
    )(proj, proj, w, b)


def _conv_bwd_pre(proj, w, b, dxc, length, tm, tc):
    hb = tm // SUBLANE
    j0 = CONV_OFF // tc

    def body(x_ref, halo_ref, w_ref, b_ref, dy_ref, dpre_ref, dw_ref, db_ref):
        i = pl.program_id(1)
        pre, taps = _conv_pre(x_ref, halo_ref, w_ref, b_ref, i == 0, tm)
        sg = jax.nn.sigmoid(pre)
        dpre = dy_ref[...] * (sg * (1.0 + pre * (1.0 - sg)))
        dpre_ref[...] = dpre

        @pl.when(i == 0)
        def _():
            dw_ref[...] = jnp.zeros_like(dw_ref)
            db_ref[...] = jnp.zeros_like(db_ref)

        for k in range(CONV_W):
            dw_ref[k:k + 1, :] += jnp.sum(dpre * taps[k], axis=0, keepdims=True)
        db_ref[...] += jnp.sum(dpre, axis=0, keepdims=True)

    return pl.pallas_call(
        body, name="conv_bwd_pre", grid=(CONV_ALL // tc, length // tm),
        in_specs=[pl.BlockSpec((tm, tc), lambda j, i: (i, j0 + j)),
                  pl.BlockSpec((SUBLANE, tc), lambda j, i: (jnp.maximum(i * hb - 1, 0), j0 + j)),
                  pl.BlockSpec((CONV_W, tc), lambda j, i: (0, j)),
                  pl.BlockSpec((1, tc), lambda j, i: (0, j)),
                  pl.BlockSpec((tm, tc), lambda j, i: (i, j))],
        out_specs=[pl.BlockSpec((tm, tc), lambda j, i: (i, j)),
                   pl.BlockSpec((CONV_W, tc), lambda j, i: (0, j)),
                   pl.BlockSpec((1, tc), lambda j, i: (0, j))],
        out_shape=[jax.ShapeDtypeStruct((length, CONV_ALL), F32),
                   jax.ShapeDtypeStruct((CONV_W, CONV_ALL), F32),
                   jax.ShapeDtypeStruct((1, CONV_ALL), F32)],
        compiler_params=_cparams(("parallel", "arbitrary"), VMEM_LIMIT),
    )(proj, proj, w, b, dxc)


def _conv_bwd_x(dpre, w, dproj, length, tm, tc):
    hb = tm // SUBLANE
    n_t = length // tm
    last_blk = length // SUBLANE - 1
    j0 = CONV_OFF // tc

    def body(d_ref, halo_ref, w_ref, _, o_ref):
        halo = jnp.where(pl.program_id(0) == n_t - 1, 0.0, halo_ref[...])
        ext = jnp.concatenate([d_ref[...], halo], axis=0)
        w = w_ref[...]
        acc = jnp.zeros_like(d_ref[...])
        for k in range(CONV_W):
            acc = acc + _conv_shift(ext, k, tm, False) * w[k:k + 1, :]
        o_ref[...] = acc.astype(o_ref.dtype)

    return pl.pallas_call(
        body, name="conv_bwd_x", grid=(n_t, CONV_ALL // tc),
        in_specs=[pl.BlockSpec((tm, tc), lambda i, j: (i, j)),
                  pl.BlockSpec((SUBLANE, tc), lambda i, j: (jnp.minimum((i + 1) * hb, last_blk), j)),
                  pl.BlockSpec((CONV_W, tc), lambda i, j: (0, j)),
                  pl.BlockSpec(memory_space=pl.ANY)],
        out_specs=pl.BlockSpec((tm, tc), lambda i, j: (i, j0 + j)),
        out_shape=jax.ShapeDtypeStruct(dproj.shape, dproj.dtype), input_output_aliases={3: 0},
        compiler_params=_cparams(("parallel", "parallel"), VMEM_LIMIT),
    )(dpre, dpre, w, dproj)


HEAD_SCAN_CHUNKS_PER_STEP = 8
assert GDN_HEADS == RET_HEADS


def _scan_steps(length):
    nc = length // CHUNK
    gs = HEAD_SCAN_CHUNKS_PER_STEP if nc % HEAD_SCAN_CHUNKS_PER_STEP == 0 else 1
    return gs, nc // gs, gs * CHUNK


def _head_tiles(ref, gs):
    return [[ref[j * CHUNK:(j + 1) * CHUNK, h * LANE:(h + 1) * LANE] for h in range(GDN_HEADS)] for j in range(gs)]


def _chunk_rows(ref, gs):
    return [ref[j * CHUNK:(j + 1) * CHUNK, :] for j in range(gs)]


def _gdn_fwd(xc, proj, alog, dtb, nw, length, comm=None):
    gs, steps, rows = _scan_steps(length)
    h_ = GDN_HEADS
    hw = h_ * LANE
    tn_ = h_ * CHUNK
    n_comm = len(comm[1]) if comm else 0

    def body(*refs):
        q_ref, k_ref, v_ref, gz_ref, sm_ref, al_ref, dt_ref, nw_ref = refs[:8]
        c_in = refs[8:8 + n_comm]
        on_ref, st_ref, t_ref = refs[8 + n_comm:11 + n_comm]
        c_out = refs[11 + n_comm:11 + 2 * n_comm]
        s_ref = refs[11 + 2 * n_comm]
        sems = refs[12 + 2 * n_comm:]

        @pl.when(pl.program_id(0) == 0)
        def _():
            s_ref[...] = jnp.zeros_like(s_ref)
            if comm:
                _comm_plan(comm[0], c_in, c_out, *sems)[0]()

        st_ref[...] = s_ref[...]
        lead = lambda ref: [ref[h] for h in range(h_)]
        on, s_new, t = _gdn_chunk(_head_tiles(q_ref, gs), _head_tiles(k_ref, gs), _head_tiles(v_ref, gs),
                                  _head_tiles(gz_ref, gs), _chunk_rows(sm_ref, gs), al_ref[...], dt_ref[...],
                                  nw_ref[...], lead(s_ref))
        for j in range(gs):
            t_ref[j] = t[j]
        for h in range(h_):
            for j in range(gs):
                on_ref[j * CHUNK:(j + 1) * CHUNK, h * LANE:(h + 1) * LANE] = on[j][h].astype(on_ref.dtype)
            s_ref[h] = s_new[h]

        if comm:
            @pl.when(pl.program_id(0) == steps - 1)
            def _():
                _comm_plan(comm[0], c_in, c_out, *sems)[1]()

    blk = lambda col: pl.BlockSpec((rows, hw), lambda c: (c, col // hw))
    one = pl.BlockSpec((1, LANE), lambda c: (0, 0))
    hbm = pl.BlockSpec(memory_space=pl.ANY)
    return pl.pallas_call(
        body, name="gdn_fwd_" + comm[0] if comm else "gdn_fwd", grid=(steps,),
        in_specs=[blk(XC("gq")), blk(XC("gk")), blk(XC("gv")), blk(_MINE["gz"]),
                  pl.BlockSpec((rows, LANE), lambda c: (c, SMALL_OFF // LANE)), one, one,
                  pl.BlockSpec((1, LANE), lambda c: (0, 0))] + [hbm] * n_comm,
        out_specs=[pl.BlockSpec((rows, hw), lambda c: (c, 0)),
                   pl.BlockSpec((h_, None, LANE, LANE), lambda c: (0, c, 0, 0)),
                   pl.BlockSpec((gs, tn_, tn_), lambda c: (c, 0, 0))] + [hbm] * n_comm,
        out_shape=[jax.ShapeDtypeStruct((length, MIX_W), BF16),
                   jax.ShapeDtypeStruct((h_, steps, LANE, LANE), F32),
                   jax.ShapeDtypeStruct((length // CHUNK, tn_, tn_), F32)]
                  + (_comm_out_shapes(*comm) if comm else []),
        scratch_shapes=[pltpu.VMEM((h_, LANE, LANE), F32)] + (_comm_scratch(n_comm) if comm else []),
        compiler_params=_cparams(("arbitrary",), VMEM_LIMIT),
    )(xc, xc, xc, proj, proj, alog, dtb, nw, *(comm[1] if comm else []))


def _gdn_bwd(xc, proj, alog, dtb, nw, st, tinv, dmix, length, comm=None):
    gs, steps, rows = _scan_steps(length)
    h_ = GDN_HEADS
    hw = h_ * LANE
    tn_ = h_ * CHUNK
    n_comm = len(comm[1]) if comm else 0
    n_in, n_out = 11, 6

    def body(*refs):
        q_ref, k_ref, v_ref, gz_ref, sm_ref, al_ref, dt_ref, nw_ref, st_ref, t_ref, do_ref = refs[:n_in]
        c_in = refs[n_in:n_in + n_comm]
        o0 = n_in + n_comm
        dqkv_ref, dgz_ref, dsm_ref, dal_ref, ddt_ref, dnw_ref = refs[o0:o0 + n_out]
        c_out = refs[o0 + n_out:o0 + n_out + n_comm]
        ds_ref = refs[o0 + n_out + n_comm]
        sems = refs[o0 + n_out + n_comm + 1:]

        @pl.when(pl.program_id(0) == 0)
        def _():
            ds_ref[...] = jnp.zeros_like(ds_ref)
            dal_ref[...] = jnp.zeros_like(dal_ref)
            ddt_ref[...] = jnp.zeros_like(ddt_ref)
            dnw_ref[...] = jnp.zeros_like(dnw_ref)
            if comm:
                _comm_plan(comm[0], c_in, c_out, *sems)[0]()

        lead = lambda ref: [ref[h] for h in range(h_)]
        ins = (_head_tiles(q_ref, gs), _head_tiles(k_ref, gs), _head_tiles(v_ref, gs), _head_tiles(gz_ref, gs),
               _chunk_rows(sm_ref, gs), al_ref[...], dt_ref[...], nw_ref[...], lead(st_ref))
        t_saved = [t_ref[j] for j in range(gs)]
        _, vjp = jax.vjp(lambda *a: _gdn_chunk(*a, t_saved=t_saved)[:2], *ins)
        dq, dk, dv, dgz, dsm, dal, ddt, dnw, ds = vjp((_head_tiles(do_ref, gs), lead(ds_ref)))
        for j in range(gs):
            rws = slice(j * CHUNK, (j + 1) * CHUNK)
            for h in range(h_):
                cols = slice(h * LANE, (h + 1) * LANE)
                for part, d in enumerate((dq, dk, dv)):
                    dqkv_ref[rws, part * hw + h * LANE:part * hw + (h + 1) * LANE] = d[j][h]
                dgz_ref[rws, cols] = dgz[j][h].astype(dgz_ref.dtype)
            dsm_ref[rws, :] = dsm[j]
        for h in range(h_):
            ds_ref[h] = ds[h]
        dal_ref[...] += dal
        ddt_ref[...] += ddt
        dnw_ref[...] += dnw

        if comm:
            @pl.when(pl.program_id(0) == steps - 1)
            def _():
                _comm_plan(comm[0], c_in, c_out, *sems)[1]()

    r = lambda c: steps - 1 - c
    blk = lambda col: pl.BlockSpec((rows, hw), lambda c: (r(c), col // hw))
    one = pl.BlockSpec((1, LANE), lambda c: (0, 0))
    hbm = pl.BlockSpec(memory_space=pl.ANY)
    return pl.pallas_call(
        body, name="gdn_bwd_" + comm[0] if comm else "gdn_bwd", grid=(steps,),
        in_specs=[blk(XC("gq")), blk(XC("gk")), blk(XC("gv")), blk(_MINE["gz"]),
                  pl.BlockSpec((rows, LANE), lambda c: (r(c), SMALL_OFF // LANE)), one, one,
                  pl.BlockSpec((1, LANE), lambda c: (0, 0)),
                  pl.BlockSpec((h_, None, LANE, LANE), lambda c: (0, r(c), 0, 0)),
                  pl.BlockSpec((gs, tn_, tn_), lambda c: (r(c), 0, 0)),
                  blk(0)] + [hbm] * n_comm,
        out_specs=[pl.BlockSpec((rows, 3 * hw), lambda c: (r(c), 0)), blk(_MINE["gz"]),
                   pl.BlockSpec((rows, LANE), lambda c: (r(c), 0)), one, one,
                   pl.BlockSpec((1, LANE), lambda c: (0, 0))] + [hbm] * n_comm,
        out_shape=[jax.ShapeDtypeStruct((length, CONV_ALL), F32), jax.ShapeDtypeStruct((length, N_PAD), BF16),
                   jax.ShapeDtypeStruct((length, LANE), F32),
                   jax.ShapeDtypeStruct((1, LANE), F32), jax.ShapeDtypeStruct((1, LANE), F32),
                   jax.ShapeDtypeStruct((1, LANE), F32)] + (_comm_out_shapes(*comm) if comm else []),
        scratch_shapes=[pltpu.VMEM((h_, LANE, LANE), F32)] + (_comm_scratch(n_comm) if comm else []),
        compiler_params=_cparams(("arbitrary",), VMEM_LIMIT),
    )(xc, xc, xc, proj, proj, alog, dtb, nw, st, tinv, dmix, *(comm[1] if comm else []))


SSD_CHUNKS_PER_STEP = 8


def _ssd_steps(length):
    nc = length // CHUNK
    gs = SSD_CHUNKS_PER_STEP if nc % SSD_CHUNKS_PER_STEP == 0 else 1
    return gs, nc // gs, gs * CHUNK


def _ssd_lanes(g):
    base = _MINE["sdt"] - SMALL_OFF
    return [[base + (g * PAIRS_PER_GROUP + p) * 2 + j for j in range(2)] for p in range(PAIRS_PER_GROUP)]


def _ssd_tiles(ref, gs):
    return [[ref[j * CHUNK:(j + 1) * CHUNK, p * LANE:(p + 1) * LANE] for p in range(PAIRS_PER_GROUP)]
            for j in range(gs)]


def _ssd_gate(y, z, w):
    return _rms(y * _silu(z), w)


def _ssd_fwd(xc, proj, alog_row, dtb_row, d_row, norm_w, mixed, length):
    gs, steps, rows = _ssd_steps(length)
    ppg = PAIRS_PER_GROUP
    gw = ppg * LANE

    def body(x_ref, b_ref, c_ref, sm_ref, al_ref, db_ref, dp_ref, z_ref, nw_ref, _, y_ref, o_ref, st_ref, hs_ref):
        @pl.when(pl.program_id(1) == 0)
        def _():
            hs_ref[...] = jnp.zeros_like(hs_ref)

        st_ref[...] = hs_ref[...]
        y, hs_new = _ssd_chunk(_ssd_lanes(pl.program_id(0)), _ssd_tiles(x_ref, gs), _chunk_rows(b_ref, gs),
                               _chunk_rows(c_ref, gs), _chunk_rows(sm_ref, gs), al_ref[...], db_ref[...],
                               dp_ref[...], [hs_ref[p] for p in range(ppg)])
        for p in range(ppg):
            hs_ref[p] = hs_new[p]
        for j in range(gs):
            rws = slice(j * CHUNK, (j + 1) * CHUNK)
            y_j = jnp.concatenate(y[j], axis=1)
            y_ref[rws, :] = y_j
            o_ref[rws, :] = _ssd_gate(y_j, z_ref[rws, :], nw_ref[...]).astype(o_ref.dtype)

    one = pl.BlockSpec((1, LANE), lambda g, c: (0, 0))
    return pl.pallas_call(
        body, name="ssd_fwd", grid=(SSD_GROUPS, steps),
        in_specs=[pl.BlockSpec((rows, gw), lambda g, c: (c, XC("sx") // gw + g)),
                  pl.BlockSpec((rows, LANE), lambda g, c: (c, XC("sB") // LANE + g)),
                  pl.BlockSpec((rows, LANE), lambda g, c: (c, XC("sC") // LANE + g)),
                  pl.BlockSpec((rows, LANE), lambda g, c: (c, SMALL_OFF // LANE)),
                  one, one, one,
                  pl.BlockSpec((rows, gw), lambda g, c: (c, _MINE["sz"] // gw + g)),
                  pl.BlockSpec((None, 1, gw), lambda g, c: (g, 0, 0)),
                  pl.BlockSpec(memory_space=pl.ANY)],
        out_specs=[pl.BlockSpec((rows, gw), lambda g, c: (c, g)),
                   pl.BlockSpec((rows, gw), lambda g, c: (c, 1 + g)),
                   pl.BlockSpec((None, ppg, None, LANE, SSD_N), lambda g, c: (g, 0, c, 0, 0))],
        out_shape=[jax.ShapeDtypeStruct((length, SSD_GROUPS * gw), F32),
                   jax.ShapeDtypeStruct(mixed.shape, mixed.dtype),
                   jax.ShapeDtypeStruct((SSD_GROUPS, ppg, steps, LANE, SSD_N), F32)],
        input_output_aliases={9: 1},
        scratch_shapes=[pltpu.VMEM((ppg, LANE, SSD_N), F32)],
        compiler_params=_cparams(("parallel", "arbitrary"), VMEM_LIMIT),
    )(xc, xc, xc, proj, alog_row, dtb_row, d_row, proj, norm_w, mixed)


def _ssd_bwd(xc, proj, alog_row, dtb_row, d_row, norm_w, st, y_b, dmix, dproj, dxc, length):
    gs, steps, rows = _ssd_steps(length)
    ppg = PAIRS_PER_GROUP
    gw = ppg * LANE

    def body(x_ref, b_ref, c_ref, sm_ref, al_ref, db_ref, dp_ref, z_ref, nw_ref, st_ref, y_ref, do_ref, _, __,
             dxc_ref, dz_ref, dsm_ref, dal_ref, ddb_ref, ddp_ref, dnw_ref, dhs_ref):
        g = pl.program_id(1)

        @pl.when(jnp.logical_and(pl.program_id(0) == 0, g == 0))
        def _():
            dhs_ref[...] = jnp.zeros_like(dhs_ref)
            dal_ref[...] = jnp.zeros_like(dal_ref)
            ddb_ref[...] = jnp.zeros_like(ddb_ref)
            ddp_ref[...] = jnp.zeros_like(ddp_ref)
            dnw_ref[...] = jnp.zeros_like(dnw_ref)

        pairs = range(ppg)
        dy, dnw = [], jnp.zeros((1, gw), F32)
        for j in range(gs):
            rws = slice(j * CHUNK, (j + 1) * CHUNK)
            _, gate_vjp = jax.vjp(_ssd_gate, y_ref[rws, :], z_ref[rws, :], nw_ref[...])
            dy_j, dz_j, dnw_j = gate_vjp(do_ref[rws, :].astype(F32))
            dz_ref[rws, :] = dz_j.astype(dz_ref.dtype)
            dy.append([dy_j[:, p * LANE:(p + 1) * LANE] for p in pairs])
            dnw = dnw + dnw_j
        dnw_ref[g] += dnw
        ins = (_ssd_tiles(x_ref, gs), _chunk_rows(b_ref, gs), _chunk_rows(c_ref, gs), _chunk_rows(sm_ref, gs),
               al_ref[...], db_ref[...], dp_ref[...], [st_ref[p] for p in pairs])
        _, vjp = jax.vjp(functools.partial(_ssd_chunk, _ssd_lanes(g)), *ins)
        dx, dbm, dcm, dsm, dal, ddb, ddp, dhs = vjp((dy, [dhs_ref[g, p] for p in pairs]))
        for grp in range(SSD_GROUPS):
            @pl.when(g == grp)
            def _(grp=grp):
                for j in range(gs):
                    rws = slice(j * CHUNK, (j + 1) * CHUNK)
                    for p in pairs:
                        col = grp * gw + p * LANE
                        dxc_ref[rws, col:col + LANE] = dx[j][p]
                    cb_ = XC("sB") - XC("sx") + grp * SSD_N
                    cc_ = XC("sC") - XC("sx") + grp * SSD_N
                    dxc_ref[rws, cb_:cb_ + SSD_N] = dbm[j]
                    dxc_ref[rws, cc_:cc_ + SSD_N] = dcm[j]
        for j in range(gs):
            dsm_ref[j * CHUNK:(j + 1) * CHUNK, :] = dsm[j]
        for p in pairs:
            dhs_ref[g, p] = dhs[p]
        dal_ref[g] += dal
        ddb_ref[g] += ddb
        ddp_ref[g] += ddp

    r = lambda c: steps - 1 - c
    one = pl.BlockSpec((1, LANE), lambda c, g: (0, 0))
    acc = pl.BlockSpec((SSD_GROUPS, 1, LANE), lambda c, g: (0, 0, 0))
    prm = jax.ShapeDtypeStruct((SSD_GROUPS, 1, LANE), F32)
    sz_blk = _MINE["sz"] // gw
    return pl.pallas_call(
        body, name="ssd_bwd", grid=(steps, SSD_GROUPS),
        in_specs=[pl.BlockSpec((rows, gw), lambda c, g: (r(c), XC("sx") // gw + g)),
                  pl.BlockSpec((rows, LANE), lambda c, g: (r(c), XC("sB") // LANE + g)),
                  pl.BlockSpec((rows, LANE), lambda c, g: (r(c), XC("sC") // LANE + g)),
                  pl.BlockSpec((rows, LANE), lambda c, g: (r(c), SMALL_OFF // LANE)),
                  one, one, one,
                  pl.BlockSpec((rows, gw), lambda c, g: (r(c), sz_blk + g)),
                  pl.BlockSpec((None, 1, gw), lambda c, g: (g, 0, 0)),
                  pl.BlockSpec((None, ppg, None, LANE, SSD_N), lambda c, g: (g, 0, r(c), 0, 0)),
                  pl.BlockSpec((rows, gw), lambda c, g: (r(c), g)),
                  pl.BlockSpec((rows, gw), lambda c, g: (r(c), 1 + g)),
                  pl.BlockSpec(memory_space=pl.ANY), pl.BlockSpec(memory_space=pl.ANY)],
        out_specs=[pl.BlockSpec((rows, CONV_CH), lambda c, g: (r(c), 1)),
                   pl.BlockSpec((rows, gw), lambda c, g: (r(c), sz_blk + g)),
                   pl.BlockSpec((None, rows, LANE), lambda c, g: (g, r(c), 0)), acc, acc, acc,
                   pl.BlockSpec((SSD_GROUPS, 1, gw), lambda c, g: (0, 0, 0))],
        out_shape=[jax.ShapeDtypeStruct(dxc.shape, dxc.dtype), jax.ShapeDtypeStruct(dproj.shape, dproj.dtype),
                   jax.ShapeDtypeStruct((SSD_GROUPS, length, LANE), F32), prm, prm, prm,
                   jax.ShapeDtypeStruct((SSD_GROUPS, 1, gw), F32)],
        input_output_aliases={12: 1, 13: 0},
        scratch_shapes=[pltpu.VMEM((SSD_GROUPS, ppg, LANE, SSD_N), F32)],
        compiler_params=_cparams(("arbitrary", "arbitrary"), VMEM_LIMIT),
    )(xc, xc, xc, proj, alog_row, dtb_row, d_row, proj, norm_w, st, y_b, dmix, dproj, dxc)


def _ret_fwd(proj, cos2, sin2, dmat, qdec, kdec, cdec, nw, mixed, length):
    gs, steps, rows = _scan_steps(length)
    h_ = RET_HEADS

    def body(q_ref, k_ref, v_ref, g_ref, cos_ref, sin_ref, dm_ref, qd_ref, kd_ref, cd_ref, nw_ref, _,
             on_ref, st_ref, r_ref):
        @pl.when(pl.program_id(0) == 0)
        def _():
            r_ref[...] = jnp.zeros_like(r_ref)

        st_ref[...] = r_ref[...]
        lead = lambda ref: [ref[h] for h in range(h_)]
        on, r_new = _ret_chunk(_head_tiles(q_ref, gs), _head_tiles(k_ref, gs), _head_tiles(v_ref, gs),
                               _head_tiles(g_ref, gs), _chunk_rows(cos_ref, gs), _chunk_rows(sin_ref, gs),
                               lead(dm_ref), lead(qd_ref), lead(kd_ref), lead(cd_ref), nw_ref[...], lead(r_ref))
        for h in range(h_):
            for j in range(gs):
                on_ref[j * CHUNK:(j + 1) * CHUNK, h * LANE:(h + 1) * LANE] = on[j][h].astype(on_ref.dtype)
            r_ref[h] = r_new[h]

    hw = h_ * LANE
    blk = lambda name: pl.BlockSpec((rows, hw), lambda c: (c, _MINE[name] // hw))
    tab = pl.BlockSpec((rows, LANE), lambda c: (c, 0))
    full = lambda *s: pl.BlockSpec(s, lambda c: (0,) * len(s))
    return pl.pallas_call(
        body, name="ret_fwd", grid=(steps,),
        in_specs=[blk("rq"), blk("rk"), blk("rv"), blk("rg"), tab, tab,
                  full(h_, CHUNK, CHUNK), full(h_, CHUNK, 1), full(h_, CHUNK, 1), full(h_, 1, 1), full(1, LANE),
                  pl.BlockSpec(memory_space=pl.ANY)],
        out_specs=[pl.BlockSpec((rows, hw), lambda c: (c, (MIX_W - hw) // hw)),
                   pl.BlockSpec((h_, None, LANE, LANE), lambda c: (0, c, 0, 0))],
        out_shape=[jax.ShapeDtypeStruct(mixed.shape, mixed.dtype),
                   jax.ShapeDtypeStruct((h_, steps, LANE, LANE), F32)],
        input_output_aliases={11: 0},
        scratch_shapes=[pltpu.VMEM((h_, LANE, LANE), F32)],
        compiler_params=_cparams(("arbitrary",), VMEM_LIMIT),
    )(proj, proj, proj, proj, cos2, sin2, dmat, qdec, kdec, cdec, nw, mixed)


def _ret_bwd(proj, cos2, sin2, dmat, qdec, kdec, cdec, nw, st, dmix, dproj, length):
    gs, steps, rows = _scan_steps(length)
    h_ = RET_HEADS
    hw = h_ * LANE

    def body(q_ref, k_ref, v_ref, g_ref, cos_ref, sin_ref, dm_ref, qd_ref, kd_ref, cd_ref, nw_ref, st_ref,
             do_ref, _, dqkvg_ref, dnw_ref, dr_ref):
        @pl.when(pl.program_id(0) == 0)
        def _():
            dr_ref[...] = jnp.zeros_like(dr_ref)
            dnw_ref[...] = jnp.zeros_like(dnw_ref)

        lead = lambda ref: [ref[h] for h in range(h_)]
        consts = (_chunk_rows(cos_ref, gs), _chunk_rows(sin_ref, gs), lead(dm_ref), lead(qd_ref), lead(kd_ref),
                  lead(cd_ref))
        f = lambda q, k, v, g, w_, r_: _ret_chunk(q, k, v, g, *consts, w_, r_)
        _, vjp = jax.vjp(f, _head_tiles(q_ref, gs), _head_tiles(k_ref, gs), _head_tiles(v_ref, gs),
                         _head_tiles(g_ref, gs), nw_ref[...], lead(st_ref))
        dq, dk, dv, dg, dnw, dr = vjp((_head_tiles(do_ref, gs), lead(dr_ref)))
        for h in range(h_):
            for j in range(gs):
                rws = slice(j * CHUNK, (j + 1) * CHUNK)
                for part, d in enumerate((dq, dk, dv, dg)):
                    col = part * hw + h * LANE
                    dqkvg_ref[rws, col:col + LANE] = d[j][h].astype(dqkvg_ref.dtype)
            dr_ref[h] = dr[h]
        dnw_ref[...] += dnw

    r = lambda c: steps - 1 - c
    blk = lambda name: pl.BlockSpec((rows, hw), lambda c: (r(c), _MINE[name] // hw))
    tab = pl.BlockSpec((rows, LANE), lambda c: (r(c), 0))
    full = lambda *s: pl.BlockSpec(s, lambda c: (0,) * len(s))
    assert _MINE["rq"] % (4 * hw) == 0 and [_MINE[n] - _MINE["rq"] for n in ("rk", "rv", "rg")] == [hw, 2 * hw, 3 * hw]
    return pl.pallas_call(
        body, name="ret_bwd", grid=(steps,),
        in_specs=[blk("rq"), blk("rk"), blk("rv"), blk("rg"), tab, tab,
                  full(h_, CHUNK, CHUNK), full(h_, CHUNK, 1), full(h_, CHUNK, 1), full(h_, 1, 1), full(1, LANE),
                  pl.BlockSpec((h_, None, LANE, LANE), lambda c: (0, r(c), 0, 0)),
                  pl.BlockSpec((rows, hw), lambda c: (r(c), (MIX_W - hw) // hw)),
                  pl.BlockSpec(memory_space=pl.ANY)],
        out_specs=[pl.BlockSpec((rows, 4 * hw), lambda c: (r(c), _MINE["rq"] // (4 * hw))), full(1, LANE)],
        out_shape=[jax.ShapeDtypeStruct(dproj.shape, dproj.dtype), jax.ShapeDtypeStruct((1, LANE), F32)],
        input_output_aliases={13: 0},
        scratch_shapes=[pltpu.VMEM((h_, LANE, LANE), F32)],
        compiler_params=_cparams(("arbitrary",), VMEM_LIMIT),
    )(proj, proj, proj, proj, cos2, sin2, dmat, qdec, kdec, cdec, nw, st, dmix, dproj)


def _mesh_pos():
    x, y, c = lax.axis_index("x"), lax.axis_index("y"), lax.axis_index("c")
    return x, y, c, 4 * x + 2 * y + c


def _peer(x, y, c, mask):
    return (x ^ ((mask >> 2) & 1), y ^ ((mask >> 1) & 1), c ^ (mask & 1))


def _gather_plan(ins, outs, send_sems, recv_sems, local_sems):
    x, y, c, me = _mesh_pos()
    n = len(ins)
    idx = lambda px, py, pc: 4 * px + 2 * py + pc
    sib = (x, y, 1 - c)
    chips = [(1 - x, y), (x, 1 - y), (1 - x, 1 - y)]
    local = [pltpu.make_async_copy(ins[a], outs[a].at[me], local_sems.at[a]) for a in range(n)]

    def copy(a, k, src, slab, to):
        return pltpu.make_async_remote_copy(
            src_ref=src, dst_ref=outs[a].at[slab], send_sem=send_sems.at[7 * a + k],
            recv_sem=recv_sems.at[7 * a + k], device_id=to, device_id_type=pl.DeviceIdType.MESH)

    first = [copy(a, 0, ins[a], me, sib) for a in range(n)]
    first += [copy(a, 1 + j, ins[a], me, (*chip, c)) for a in range(n) for j, chip in enumerate(chips)]
    landed = [[copy(a, 1 + j, ins[a], idx(*chip, c), (*chip, c)) for a in range(n)] for j, chip in enumerate(chips)]
    passed = [[copy(a, 4 + j, outs[a].at[idx(*chip, c)], idx(*chip, c), sib) for a in range(n)]
              for j, chip in enumerate(chips)]
    from_sib = [copy(a, 0, ins[a], idx(*sib), sib) for a in range(n)]
    from_sib += [copy(a, 4 + j, ins[a], idx(*chip, 1 - c), sib) for a in range(n) for j, chip in enumerate(chips)]

    def start():
        for cp in local + first:
            cp.start()

    def wait():
        for j in range(len(chips)):
            for a in range(n):
                landed[j][a].wait_recv()
                passed[j][a].start()
        for cp in from_sib:
            cp.wait_recv()
        for cp in first + [cp for row in passed for cp in row]:
            cp.wait_send()
        for cp in local:
            cp.wait()

    return start, wait


def _comm_plan(kind, ins, outs, send_sems, recv_sems, local_sems):
    if kind == "gather":
        return _gather_plan(ins, outs, send_sems, recv_sems, local_sems)
    x, y, c, me = _mesh_pos()
    n = len(ins)
    src = (lambda a, idx: ins[a]) if kind == "gather" else (lambda a, idx: ins[a].at[idx])
    local = [pltpu.make_async_copy(src(a, me), outs[a].at[me], local_sems.at[a]) for a in range(n)]
    sends, recvs = [], []
    for mask in range(1, N_DEV):
        px, py, pc = _peer(x, y, c, mask)
        pidx = 4 * px + 2 * py + pc
        for a in range(n):
            k = (mask - 1) * n + a
            copy = lambda s, d, k=k: pltpu.make_async_remote_copy(
                src_ref=s, dst_ref=d, send_sem=send_sems.at[k], recv_sem=recv_sems.at[k],
                device_id=(px, py, pc), device_id_type=pl.DeviceIdType.MESH)
            sends.append(copy(src(a, pidx), outs[a].at[me]))
            recvs.append(copy(src(a, me), outs[a].at[pidx]))

    def start():
        for cp in local + sends:
            cp.start()

    def wait():
        for cp in sends:
            cp.wait_send()
        for cp in recvs:
            cp.wait_recv()
        for cp in local:
            cp.wait()

    return start, wait


def _comm_out_shapes(kind, arrays):
    return [jax.ShapeDtypeStruct(((N_DEV,) + s.shape) if kind == "gather" else s.shape, s.dtype) for s in arrays]


def _comm_scratch(n):
    k_tot = (N_DEV - 1) * n
    return [pltpu.SemaphoreType.DMA((k_tot,)), pltpu.SemaphoreType.DMA((k_tot,)), pltpu.SemaphoreType.DMA((n,))]


def _comm_call(name, kind, arrays):
    n = len(arrays)

    def body(*refs):
        start, wait = _comm_plan(kind, refs[:n], refs[n:2 * n], *refs[2 * n:])
        start()
        wait()

    hbm = pl.BlockSpec(memory_space=pl.ANY)
    return pl.pallas_call(
        body, name=name, in_specs=[hbm] * n, out_specs=[hbm] * n, out_shape=_comm_out_shapes(kind, arrays),
        scratch_shapes=_comm_scratch(n), compiler_params=pltpu.CompilerParams(has_side_effects=True),
    )(*arrays)


def _adam(name, parts, w, m, v, tr, layer=None, prev=None):
    p_, r_, cw = parts.shape
    c1 = 1.0 - ADAM_B1 ** ADAM_STEP
    c2 = 1.0 - ADAM_B2 ** ADAM_STEP
    n_prev = 4 if prev else 0

    def body(p_ref, w_ref, m_ref, v_ref, *rest):
        g_ref, d_ref, nm_ref, nv_ref = rest[n_prev:]
        g = p_ref[0].astype(F32)
        for i in range(1, p_):
            g = g + p_ref[i].astype(F32)
        nm = ADAM_B1 * m_ref[...] + (1.0 - ADAM_B1) * g
        nv = ADAM_B2 * v_ref[...] + (1.0 - ADAM_B2) * (g * g)
        d_ref[...] = -ADAM_LR * ((nm / c1) / (jnp.sqrt(nv / c2) + ADAM_EPS) + ADAM_WD * w_ref[...])
        g_ref[...] = g
        nm_ref[...] = nm
        nv_ref[...] = nv

    if layer is None:
        spec = pl.BlockSpec((tr, cw), lambda i: (i, 0))
        shp = jax.ShapeDtypeStruct((r_, cw), F32)
    else:
        spec = pl.BlockSpec((None, tr, cw), lambda i: (layer, i, 0))
        shp = jax.ShapeDtypeStruct((DEPTH, r_, cw), F32)
    return pl.pallas_call(
        body, name=name, grid=(r_ // tr,),
        in_specs=[pl.BlockSpec((p_, tr, cw), lambda i: (0, i, 0)), spec, spec, spec]
                 + [pl.BlockSpec(memory_space=pl.ANY)] * n_prev,
        out_specs=[spec] * 4, out_shape=[shp] * 4,
        input_output_aliases={4 + i: i for i in range(n_prev)},
        compiler_params=_cparams(("parallel",), VMEM_LIMIT),
    )(parts, w, m, v, *(prev or ()))


def _shards_to_perm(g_in):
    parts = []
    for n in _ORDER:
        lo, hi = _ORIG[n][0], _ORIG[n][0] + _ORIG[n][1]
        for j in range(lo // SHARD_IN, (hi - 1) // SHARD_IN + 1):
            a, b = max(lo, j * SHARD_IN), min(hi, (j + 1) * SHARD_IN)
            parts.append(g_in[j][:, a - j * SHARD_IN:b - j * SHARD_IN])
    parts.append(jnp.zeros((g_in.shape[1], N_PAD - N_USED), g_in.dtype))
    return jnp.concatenate(parts, axis=1)


def _perm_to_shards(g):
    by_orig = sorted(_ORDER, key=lambda n: _ORIG[n][0])
    slabs = []
    for j in range(N_DEV):
        lo, hi = j * SHARD_IN, (j + 1) * SHARD_IN
        parts = []
        for n in by_orig:
            a, b = max(lo, _ORIG[n][0]), min(hi, _ORIG[n][0] + _ORIG[n][1])
            if a < b:
                parts.append(g[:, _MINE[n] + a - _ORIG[n][0]:_MINE[n] + b - _ORIG[n][0]])
        slabs.append(jnp.concatenate(parts, axis=1))
    return jnp.stack(slabs, axis=0)


def _ret_consts(length):
    lg = jnp.log(1.0 - 2.0 ** (-5.0 - jnp.arange(RET_HEADS, dtype=F32)))
    idx = jnp.arange(CHUNK, dtype=F32)
    rel = idx[:, None] - idx[None, :]
    dmat = jnp.where(rel[None] >= 0, jnp.exp(jnp.maximum(rel, 0.0)[None] * lg[:, None, None]), 0.0)
    qdec = jnp.exp((idx[:, None] + 1.0) * lg[None, :]).T[:, :, None]
    kdec = jnp.exp((CHUNK - 1.0 - idx)[:, None] * lg[None, :]).T[:, :, None]
    cdec = jnp.exp(CHUNK * lg)[:, None, None]
    half = RET_DK // 2
    inv = ROPE_BASE ** (-jnp.arange(half, dtype=F32) / half)
    ang = jnp.arange(length, dtype=jnp.int32).astype(F32)[:, None] * inv[None, :]
    cos, sin = jnp.cos(ang), jnp.sin(ang)
    return (jnp.concatenate([cos, cos], axis=1), jnp.concatenate([-sin, sin], axis=1), dmat, qdec, kdec, cdec)


def _tiles(length, pref=512):
    return pref if length % pref == 0 else length


def _layer_fwd(x, p, consts, length, last, target, comm=None):
    tm = _tiles(length)
    (h,), _ = _rowmap("pre_norm", lambda x_, w_: ((_rms(x_, w_),), ()), length, tm,
                      [(x, D_MODEL, 0)], [p["pre_norm"]], out_ws=[D_MODEL], out_dtypes=[BF16])
    proj = _matmul("in_proj", h, p["w_in"], "nn", length, N_PAD, D_MODEL)
    xc = _conv_fwd(proj, p["conv_w"], p["conv_b"], length, _tiles(length, 2048), 512)
    mixed, st_a, tinv, *got = _gdn_fwd(xc, proj, p["gdn_A_log"], p["gdn_dt_bias"], p["gdn_norm"], length, comm)
    y_b, mixed, st_b = _ssd_fwd(xc, proj, p["ssd_A_log"], p["ssd_dt_bias"], p["ssd_D"], p["ssd_norm"], mixed, length)
    mixed, st_c = _ret_fwd(proj, *consts, p["ret_norm"], mixed, length)
    w_out = p["w_out"](got) if callable(p["w_out"]) else p["w_out"]
    out = _matmul("out_proj", mixed, w_out, "nn", length, D_MODEL, MIX_W)
    res = dict(x=x, h=h, proj=proj, xc=xc, st_a=st_a, tinv=tinv, st_b=st_b, st_c=st_c, y_b=y_b, mixed=mixed,
               out=out, w_out=w_out)
    if not last:
        (y,), _ = _rowmap("post_norm", lambda x_, o_, w_: ((x_ + _rms(o_, w_),), ()), length, tm,
                          [(x, D_MODEL, 0), (out, D_MODEL, 0)], [p["post_norm"]], out_ws=[D_MODEL])
        return y, res, None, got

    def head(x_, o_, t_, w_):
        normed, post_vjp = jax.vjp(_rms, o_, w_)
        e = x_ + normed - t_
        row = jnp.mean(e * e, axis=-1, keepdims=True)
        loss = 0.5 * jnp.sum(row, axis=0, keepdims=True)
        dy_ = e * (1.0 / D_MODEL)
        d_out, dw = post_vjp(dy_)
        return (dy_, d_out), (loss + jnp.zeros((1, LANE), F32), dw)

    (dy, d_out), (loss, dw_post) = _rowmap(
        "loss_head", head, length, tm, [(x, D_MODEL, 0), (out, D_MODEL, 0), (target, D_MODEL, 0)], [p["post_norm"]],
        out_ws=[D_MODEL, D_MODEL], out_dtypes=[F32, BF16], acc_ws=[LANE, D_MODEL])
    res["post_bwd"] = (d_out, dw_post)
    return None, res, (dy, loss[0, 0]), got


def _layer_bwd(dy, p, res, consts, length, comm_scan=None, comm_dx=None):
    tm = _tiles(length)
    g = {}
    if "post_bwd" in res:
        d_out, g["post_norm"] = res["post_bwd"]
    else:
        (d_out,), (g["post_norm"],) = _rowmap("post_norm_bwd", lambda o_, w_: ((_rms(o_, w_),), ()), length, tm,
                                              [(res["out"], D_MODEL, 0)], [p["post_norm"]], cts=[(dy, D_MODEL, 0)],
                                              d_dtypes=[BF16])
    dmix = _matmul("out_proj_dx", d_out, res["w_out"], "nt", length, MIX_W, D_MODEL)
    g["w_out"] = _matmul("out_proj_dw", res["mixed"], d_out, "tn", MIX_W, D_MODEL, length, out_dtype=BF16)
    comm = comm_scan(g) if comm_scan else None
    proj, xc = res["proj"], res["xc"]
    (dxc, dproj, dsm_a, g["gdn_A_log"], g["gdn_dt_bias"], g["gdn_norm"], *got) = _gdn_bwd(
        xc, proj, p["gdn_A_log"], p["gdn_dt_bias"], p["gdn_norm"], res["st_a"], res["tinv"], dmix, length, comm)
    dxc, dproj, dsm_b, g["ssd_A_log"], g["ssd_dt_bias"], g["ssd_D"], g["ssd_norm"] = _ssd_bwd(
        xc, proj, p["ssd_A_log"], p["ssd_dt_bias"], p["ssd_D"], p["ssd_norm"], res["st_b"], res["y_b"], dmix,
        dproj, dxc, length)
    dproj, g["ret_norm"] = _ret_bwd(proj, *consts, p["ret_norm"], res["st_c"], dmix, dproj, length)
    tmc = _tiles(length, 1024)
    dpre, g["conv_w"], g["conv_b"] = _conv_bwd_pre(proj, p["conv_w"], p["conv_b"], dxc, length, tmc, 512)
    dproj = _conv_bwd_x(dpre, p["conv_w"], dproj, length, _tiles(length, 2048), 512)
    pad = SMALL_OFF % 512 + 512 - LANE
    (dproj,), _ = _rowmap("dsmall", lambda a_, b_, c_: ((jnp.concatenate([a_ + b_ + c_, jnp.zeros((a_.shape[0], pad), F32)], axis=1),), ()),
                          length, tm, [(dsm_a, LANE, 0), (dsm_b[0], LANE, 0), (dsm_b[1], LANE, 0)], [],
                          out_ws=[512], out_dtypes=[BF16], place={0: (dproj, N_PAD, SMALL_OFF // 512)})
    g["w_in"] = _matmul("in_proj_dw", res["h"], dproj, "tn", D_MODEL, N_PAD, length, out_dtype=BF16)
    if comm_dx:
        dh, *got_dx = _matmul("in_proj_dx_scatter", dproj, p["w_in"], "nt", length, D_MODEL, N_PAD, comm=comm_dx(g))
    else:
        dh, got_dx = _matmul("in_proj_dx", dproj, p["w_in"], "nt", length, D_MODEL, N_PAD), []
    (dx,), (g["pre_norm"],) = _rowmap(
        "pre_norm_bwd", lambda x_, w_: ((_rms(x_, w_), x_), ()), length, tm,
        [(res["x"], D_MODEL, 0)], [p["pre_norm"]], cts=[(dh, D_MODEL, 0), (dy, D_MODEL, 0)])
    return dx, g, got, got_dx


def _lane_row(vals, piece):
    lo = _MINE[piece] - SMALL_OFF
    return jnp.pad(vals[None], ((0, 0), (lo, LANE - lo - vals.shape[0])))


def _from_lane_row(row, piece, n):
    lo = _MINE[piece] - SMALL_OFF
    return row.reshape(-1, LANE).sum(axis=0)[lo:lo + n]


def _make_layer(l, w_in_perm, w_out_full, conv_full, sw):
    return dict(
        pre_norm=sw["pre_norm"][l][None], post_norm=sw["post_norm"][l][None], w_in=w_in_perm,
        w_out=w_out_full,
        conv_w=jnp.concatenate([conv_full[0], conv_full[1]], axis=1),
        conv_b=jnp.concatenate([jnp.zeros((CONV_CH,), F32), sw["ssd_conv_b"][l]])[None],
        gdn_A_log=_lane_row(sw["gdn_A_log"][l], "ga"), gdn_dt_bias=_lane_row(sw["gdn_dt_bias"][l], "ga"),
        gdn_norm=sw["gdn_norm"][l][None],
        ssd_A_log=_lane_row(sw["ssd_A_log"][l], "sdt"), ssd_dt_bias=_lane_row(sw["ssd_dt_bias"][l], "sdt"),
        ssd_D=_lane_row(sw["ssd_D"][l], "sdt"),
        ssd_norm=sw["ssd_norm"][l].reshape(SSD_GROUPS, 1, -1),
        ret_norm=sw["ret_norm"][l][None])


def _local_step(xb, tgt, layer0, make_layer1, length, fwd_comm=None, comm_scan=None, comm_dx=None):
    consts = _ret_consts(length)
    y0, res0, _, got = _layer_fwd(xb, layer0, consts, length, False, tgt, fwd_comm)
    layer1 = make_layer1(got)
    _, res1, (dy, loss_local), _ = _layer_fwd(y0, layer1, consts, length, True, tgt)
    dy, grads1, _, _ = _layer_bwd(dy, layer1, res1, consts, length)
    dx, grads0, recv_scan, recv_dx = _layer_bwd(
        dy, layer0, res0, consts, length,
        (lambda g0: comm_scan(grads1, g0)) if comm_scan else None, comm_dx)
    return loss_local, dx, [grads0, grads1], recv_scan, recv_dx


_SMALL = ["pre_norm", "post_norm", "gdn_A_log", "gdn_dt_bias", "gdn_norm", "ssd_conv_b", "ssd_A_log",
          "ssd_dt_bias", "ssd_D", "ssd_norm", "ret_norm"]


def _pack_small(arrs):
    rows = []
    for n in _SMALL:
        flat = arrs[n].reshape(-1)
        pad = (-flat.shape[0]) % LANE
        rows.append(jnp.pad(flat, (0, pad)).reshape(-1, LANE))
    out = jnp.concatenate(rows, axis=0)
    return jnp.pad(out, ((0, (-out.shape[0]) % SUBLANE), (0, 0)))


def _unpack_small(packed, like):
    out, r = {}, 0
    for n in _SMALL:
        cnt = like[n].size
        nrow = -(-cnt // LANE)
        out[n] = packed[r:r + nrow].reshape(-1)[:cnt].reshape(like[n].shape)
        r += nrow
    return out


def kernel(x, pre_norm, post_norm, w_in, gdn_conv, gdn_A_log, gdn_dt_bias, gdn_norm, ssd_conv, ssd_conv_b, ssd_A_log, ssd_dt_bias, ssd_D, ssd_norm, ret_norm, w_out, loss_target, m_pre_norm, m_post_norm, m_w_in, m_gdn_conv, m_gdn_A_log, m_gdn_dt_bias, m_gdn_norm, m_ssd_conv, m_ssd_conv_b, m_ssd_A_log, m_ssd_dt_bias, m_ssd_D, m_ssd_norm, m_ret_norm, m_w_out, v_pre_norm, v_post_norm, v_w_in, v_gdn_conv, v_gdn_A_log, v_gdn_dt_bias, v_gdn_norm, v_ssd_conv, v_ssd_conv_b, v_ssd_A_log, v_ssd_dt_bias, v_ssd_D, v_ssd_norm, v_ret_norm, v_w_out):
    length = x.shape[1]
    xb = x[0]
    tgt = loss_target[0]
    small_w = dict(pre_norm=pre_norm, post_norm=post_norm, gdn_A_log=gdn_A_log, gdn_dt_bias=gdn_dt_bias,
                   gdn_norm=gdn_norm, ssd_conv_b=ssd_conv_b, ssd_A_log=ssd_A_log, ssd_dt_bias=ssd_dt_bias,
                   ssd_D=ssd_D, ssd_norm=ssd_norm, ret_norm=ret_norm)
    small_m = dict(pre_norm=m_pre_norm, post_norm=m_post_norm, gdn_A_log=m_gdn_A_log, gdn_dt_bias=m_gdn_dt_bias,
                   gdn_norm=m_gdn_norm, ssd_conv_b=m_ssd_conv_b, ssd_A_log=m_ssd_A_log, ssd_dt_bias=m_ssd_dt_bias,
                   ssd_D=m_ssd_D, ssd_norm=m_ssd_norm, ret_norm=m_ret_norm)
    small_v = dict(pre_norm=v_pre_norm, post_norm=v_post_norm, gdn_A_log=v_gdn_A_log, gdn_dt_bias=v_gdn_dt_bias,
                   gdn_norm=v_gdn_norm, ssd_conv_b=v_ssd_conv_b, ssd_A_log=v_ssd_A_log, ssd_dt_bias=v_ssd_dt_bias,
                   ssd_D=v_ssd_D, ssd_norm=v_ssd_norm, ret_norm=v_ret_norm)

    w_in_b, w_out_b = w_in.astype(BF16), w_out.astype(BF16)
    conv_shard = jnp.stack([gdn_conv, ssd_conv], axis=1)
    full_out = lambda g_out: g_out.reshape(MIX_W, D_MODEL)

    def assemble(l, g_in, w_out_full, g_conv):
        conv_full = g_conv.transpose(1, 2, 0, 3).reshape(2, CONV_W, CONV_CH)
        return _make_layer(l, _shards_to_perm(g_in), w_out_full, conv_full, small_w)

    slab_in = lambda g: _perm_to_shards(g["w_in"]).astype(BF16)
    slab_out = lambda g: g["w_out"].reshape(N_DEV, SHARD_OUT, D_MODEL).astype(BF16)
    slab_conv = lambda g: jnp.stack(
        [g["conv_w"][:, k * CONV_CH:(k + 1) * CONV_CH].reshape(CONV_W, N_DEV, SHARD_CONV).transpose(1, 0, 2)
         for k in range(2)], axis=1).reshape(N_DEV, 2 * CONV_W, SHARD_CONV)

    g_in0, g_conv0 = _comm_call("gather_layer0", "gather", [w_in_b[0], conv_shard[0]])
    layer0 = assemble(0, g_in0, lambda got: full_out(got[3]), g_conv0)
    loss_local, dx, grads, recv_scan, recv_dx = _local_step(
        xb, tgt, layer0, lambda got: assemble(1, got[0], full_out(got[1]), got[2]), length,
        fwd_comm=("gather", [w_in_b[1], w_out_b[1], conv_shard[1], w_out_b[0]]),
        comm_scan=lambda g1, g0: ("scatter", [slab_in(g1), slab_out(g1), slab_conv(g1), slab_out(g0)]),
        comm_dx=lambda g0: ("scatter", [slab_in(g0), slab_conv(g0)]))
    grad_x = dx[None]
    loss = lax.psum(loss_local, ("x", "y", "c"))

    small_g = dict(
        pre_norm=jnp.concatenate([grads[l]["pre_norm"] for l in range(DEPTH)], axis=0),
        post_norm=jnp.concatenate([grads[l]["post_norm"] for l in range(DEPTH)], axis=0),
        gdn_A_log=jnp.stack([_from_lane_row(grads[l]["gdn_A_log"], "ga", GDN_HEADS) for l in range(DEPTH)]),
        gdn_dt_bias=jnp.stack([_from_lane_row(grads[l]["gdn_dt_bias"], "ga", GDN_HEADS) for l in range(DEPTH)]),
        gdn_norm=jnp.concatenate([grads[l]["gdn_norm"] for l in range(DEPTH)], axis=0),
        ssd_conv_b=jnp.concatenate([grads[l]["conv_b"][:, CONV_CH:] for l in range(DEPTH)], axis=0),
        ssd_A_log=jnp.stack([_from_lane_row(grads[l]["ssd_A_log"], "sdt", SSD_HEADS) for l in range(DEPTH)]),
        ssd_dt_bias=jnp.stack([_from_lane_row(grads[l]["ssd_dt_bias"], "sdt", SSD_HEADS) for l in range(DEPTH)]),
        ssd_D=jnp.stack([_from_lane_row(grads[l]["ssd_D"], "sdt", SSD_HEADS) for l in range(DEPTH)]),
        ssd_norm=jnp.concatenate([grads[l]["ssd_norm"].reshape(1, -1) for l in range(DEPTH)], axis=0),
        ret_norm=jnp.concatenate([grads[l]["ret_norm"] for l in range(DEPTH)], axis=0))
    gs = _pack_small(small_g)
    gs8 = jnp.broadcast_to(gs[None], (N_DEV,) + gs.shape)
    (r_small,) = _comm_call("exchange_small", "scatter", [gs8])
    recv = [[recv_dx[0], recv_scan[3], recv_dx[1]], recv_scan[:3]]

    conv_w = lambda g_, s_, l: jnp.stack([g_[l], s_[l]], axis=0).reshape(2 * CONV_W, SHARD_CONV)
    o_in, o_out, o_conv = None, None, []
    for l in range(DEPTH):
        o_in = _adam(f"adam_w_in{l}", recv[l][0], w_in, m_w_in, v_w_in, 128, l, o_in)
        o_out = _adam(f"adam_w_out{l}", recv[l][1], w_out, m_w_out, v_w_out, 128, l, o_out)
        o_conv.append(_adam(f"adam_conv{l}", recv[l][2], conv_w(gdn_conv, ssd_conv, l),
                            conv_w(m_gdn_conv, m_ssd_conv, l), conv_w(v_gdn_conv, v_ssd_conv, l), 2 * CONV_W))
    ps_w, ps_m, ps_v = _pack_small(small_w), _pack_small(small_m), _pack_small(small_v)
    o_small = _adam("adam_small", r_small, ps_w, ps_m, ps_v, ps_w.shape[0])

    names = ["pre_norm", "post_norm", "w_in", "gdn_conv", "gdn_A_log", "gdn_dt_bias", "gdn_norm", "ssd_conv",
             "ssd_conv_b", "ssd_A_log", "ssd_dt_bias", "ssd_D", "ssd_norm", "ret_norm", "w_out"]
    outs = []
    for kind in range(4):
        d = _unpack_small(o_small[kind], small_w)
        cv = jnp.stack([o_conv[l][kind].reshape(2, CONV_W, SHARD_CONV) for l in range(DEPTH)], axis=1)
        d["w_in"] = o_in[kind]
        d["w_out"] = o_out[kind]
        d["gdn_conv"] = cv[0]
        d["ssd_conv"] = cv[1]
        outs.extend(d[n] for n in names)
    return (loss, grad_x, *outs)
```

```python
import functools
import math

import jax
import jax.numpy as jnp
from jax import lax
from jax.experimental import pallas as pl
from jax.experimental.pallas import tpu as pltpu

F32 = jnp.float32
BF16 = jnp.bfloat16

D_MODEL = 1024
DEPTH = 2
CHUNK = 64
CONV_W = 4
EPS = 1e-6
N_DEV = 8

GDN_HEADS = 4
GDN_DK = 128
SSD_HEADS = 16
SSD_P = 64
SSD_N = 128
SSD_GROUPS = 2
SSD_PAIRS = SSD_HEADS // 2
PAIRS_PER_GROUP = SSD_PAIRS // SSD_GROUPS
RET_HEADS = 4
RET_DK = 128
ROPE_BASE = 10000.0
MIX_W = 2048
N_IN = 6680
SHARD_IN = N_IN // N_DEV
SHARD_OUT = MIX_W // N_DEV
CONV_CH = 1536
SHARD_CONV = CONV_CH // N_DEV

ADAM_LR = 0.001
ADAM_B1 = 0.9
ADAM_B2 = 0.999
ADAM_EPS = 1e-08
ADAM_WD = 0.01
ADAM_STEP = 10

LANE = 128
SUBLANE = 8
VMEM_LIMIT = 56 * 1024 * 1024

_ORIG = dict(gq=(0, 512), gk=(512, 512), gv=(1024, 512), gz=(1536, 512), gb=(2048, 4), ga=(2052, 4),
             sx=(2056, 1024), sB=(3080, 256), sC=(3336, 256), sz=(3592, 1024), sdt=(4616, 16),
             rq=(4632, 512), rk=(5144, 512), rv=(5656, 512), rg=(6168, 512))
_ORDER = ["rq", "rk", "rv", "rg", "gq", "gk", "gv", "sx", "sB", "sC", "gz", "sz", "gb", "ga", "sdt"]
_MINE = {}
_off = 0
for _n in _ORDER:
    _MINE[_n] = _off
    _off += _ORIG[_n][1]
N_USED = _off
N_PAD = 7168
CONV_ALL = 2 * CONV_CH
SMALL_OFF = _MINE["gb"]
CONV_OFF = _MINE["gq"]
XC = lambda name: _MINE[name] - CONV_OFF


def _cparams(sem, vmem=None):
    return pltpu.CompilerParams(dimension_semantics=sem, vmem_limit_bytes=vmem)


def _split_bf16(a):
    hi = a.astype(BF16)
    return hi, (a - hi.astype(F32)).astype(BF16)


def _make_mm():
    def raw(a, b, ca, cb):
        return lax.dot_general(a.astype(BF16), b.astype(BF16), (((ca,), (cb,)), ((), ())),
                               preferred_element_type=F32)

    @jax.custom_vjp
    def nn(a, b):
        return raw(a, b, 1, 0)

    @jax.custom_vjp
    def nt(a, b):
        return raw(a, b, 1, 1)

    @jax.custom_vjp
    def tn(a, b):
        return raw(a, b, 0, 0)

    nn.defvjp(lambda a, b: (raw(a, b, 1, 0), (a, b)), lambda r, g: (nt(g, r[1]), tn(r[0], g)))
    nt.defvjp(lambda a, b: (raw(a, b, 1, 1), (a, b)), lambda r, g: (nn(g, r[1]), tn(g, r[0])))
    tn.defvjp(lambda a, b: (raw(a, b, 0, 0), (a, b)), lambda r, g: (nt(r[1], g), nn(r[0], g)))
    return nn, nt, tn


_nn, _nt, _tn = _make_mm()


@jax.custom_vjp
def _swap_halves(t):
    return pltpu.roll(t, LANE // 2, 1)


_swap_halves.defvjp(lambda t: (pltpu.roll(t, LANE // 2, 1), None),
                    lambda _, g: (pltpu.roll(g, LANE // 2, 1),))


@jax.custom_vjp
def _split_rows(x):
    return tuple(x[i * CHUNK:(i + 1) * CHUNK] for i in range(x.shape[0] // CHUNK))


_split_rows.defvjp(lambda x: (tuple(x[i * CHUNK:(i + 1) * CHUNK] for i in range(x.shape[0] // CHUNK)), None),
                   lambda _, gs: (jnp.concatenate(gs, axis=0),))


def _dot3(a, b, ca, cb):
    dot = lambda x, y: lax.dot_general(x, y, (((ca,), (cb,)), ((), ())), preferred_element_type=F32)
    return dot(a[0], b[0]) + (dot(a[0], b[1]) + dot(a[1], b[0]))


@jax.custom_vjp
def _tri_inv(mats):
    return _tri_inv_impl(mats)


def _tri_inv_impl(mats):
    ii = lax.broadcasted_iota(jnp.int32, mats[0].shape, 0)
    jj = lax.broadcasted_iota(jnp.int32, mats[0].shape, 1)
    eye = jnp.where(ii == jj, 1.0, 0.0).astype(F32)
    ts = [eye - a for a in mats]
    ps = [_split_bf16(-a) for a in mats]
    for _ in range(int(math.log2(CHUNK)) - 1):
        ps = [_split_bf16(_dot3(p, p, 1, 0)) for p in ps]
        ts = [t + _dot3(_split_bf16(t), p, 1, 0) for t, p in zip(ts, ps)]
    return ts


def _tri_inv_bwd(ts, gs):
    tsp = [_split_bf16(t) for t in ts]
    xs = [_dot3(t, _split_bf16(g), 0, 0) for t, g in zip(tsp, gs)]
    return ([-_dot3(_split_bf16(x), t, 1, 1) for x, t in zip(xs, tsp)],)


_tri_inv.defvjp(lambda mats: (lambda ts: (ts, ts))(_tri_inv_impl(mats)), _tri_inv_bwd)


@jax.custom_vjp
def _tri_inv_saved(mats, ts):
    return ts


_tri_inv_saved.defvjp(lambda mats, ts: (ts, ts),
                      lambda ts, gs: (_tri_inv_bwd(ts, gs)[0], [jnp.zeros_like(t) for t in ts]))


def _silu(x):
    return x * jax.nn.sigmoid(x)


@jax.custom_vjp
def _softplus(x):
    return jnp.maximum(x, 0.0) + jnp.log1p(jnp.exp(-jnp.abs(x)))


_softplus.defvjp(lambda x: (jnp.maximum(x, 0.0) + jnp.log1p(jnp.exp(-jnp.abs(x))), x),
                 lambda x, g: (g * jax.nn.sigmoid(x),))


def _rms(x, w):
    return x * lax.rsqrt(jnp.mean(x * x, axis=-1, keepdims=True) + EPS) * w


def _chunk_masks(n):
    ii = lax.broadcasted_iota(jnp.int32, (n, n), 0)
    jj = lax.broadcasted_iota(jnp.int32, (n, n), 1)
    return ii >= jj, ii > jj, ii == jj, ii <= jj


def _cumsum_col(g, causal, eye, upper):
    g_row = jnp.sum(jnp.where(eye, g, 0.0), axis=0, keepdims=True)
    col = jnp.sum(jnp.where(causal, g_row, 0.0), axis=1, keepdims=True)
    row = jnp.sum(jnp.where(upper, g, 0.0), axis=0, keepdims=True)
    return col, row


def _lane_col(block, lane):
    pick = lax.broadcasted_iota(jnp.int32, (1, block.shape[1]), 1) == lane
    return jnp.sum(jnp.where(pick, block, 0.0), axis=1, keepdims=True)


def _gdn_chunk(q, k, v, gz, sm, alog_row, dtb_row, nw, s, t_saved=None):
    subs, nh = range(len(q)), len(q[0])
    c = q[0][0].shape[0]
    causal, _, eye, upper = _chunk_masks(c)
    stack = lambda xs: jnp.concatenate(xs, axis=0)
    per = lambda f: [[f(j, h) for h in range(nh)] for j in subs]
    qn = per(lambda j, h: q[j][h] * lax.rsqrt(jnp.sum(q[j][h] * q[j][h], axis=-1, keepdims=True) + EPS)
             * (GDN_DK ** -0.5))
    kn = per(lambda j, h: k[j][h] * lax.rsqrt(jnp.sum(k[j][h] * k[j][h], axis=-1, keepdims=True) + EPS))
    beta_blk = [jax.nn.sigmoid(sm[j]) for j in subs]
    g_blk = [-jnp.exp(alog_row) * _softplus(sm[j] + dtb_row) for j in subs]
    beta = per(lambda j, h: _lane_col(beta_blk[j], _MINE["gb"] - SMALL_OFF + h))
    g = per(lambda j, h: _lane_col(g_blk[j], _MINE["ga"] - SMALL_OFF + h))
    gcum = per(lambda j, h: _cumsum_col(g[j][h], causal, eye, upper)[0])
    eg = per(lambda j, h: jnp.exp(gcum[j][h]))
    glast = per(lambda j, h: jnp.sum(g[j][h], axis=0, keepdims=True))
    kb = per(lambda j, h: kn[j][h] * beta[j][h])
    n = nh * c
    ii = lax.broadcasted_iota(jnp.int32, (n, n), 0)
    jj = lax.broadcasted_iota(jnp.int32, (n, n), 1)
    sh = int(math.log2(c))
    same = lax.shift_right_logical(ii, sh) == lax.shift_right_logical(jj, sh)
    causal_bd = jnp.logical_and(same, ii >= jj)
    strict_bd = jnp.logical_and(same, ii > jj)
    gcum_all = [stack(gcum[j]) for j in subs]
    gcum_row = [jnp.sum(jnp.where(ii == jj, gcum_all[j], 0.0), axis=0, keepdims=True) for j in subs]
    decay = [jnp.where(causal_bd, jnp.exp(jnp.where(causal_bd, gcum_all[j] - gcum_row[j], 0.0)), 0.0) for j in subs]
    kn_all = [stack(kn[j]) for j in subs]
    a_low = [jnp.where(strict_bd, _nt(stack(kb[j]), kn_all[j]) * decay[j], 0.0) for j in subs]
    t = _tri_inv(a_low) if t_saved is None else _tri_inv_saved(a_low, t_saved)
    u = [_nn(t[j], stack([v[j][h] * beta[j][h] for h in range(nh)])) for j in subs]
    w = [_split_rows(_nn(t[j], stack([kb[j][h] * eg[j][h] for h in range(nh)]))) for j in subs]
    attn = [_nt(stack(qn[j]), kn_all[j]) * decay[j] for j in subs]
    kdec = per(lambda j, h: kn[j][h] * jnp.exp(glast[j][h] - gcum[j][h]))
    on = []
    for j in subs:
        v_new_all = u[j] - stack([_nn(w[j][h], s[h]) for h in range(nh)])
        o = _split_rows(stack([_nn(qn[j][h] * eg[j][h], s[h]) for h in range(nh)]) + _nn(attn[j], v_new_all))
        v_new = _split_rows(v_new_all)
        s = [s[h] * jnp.exp(glast[j][h]) + _tn(kdec[j][h], v_new[h]) for h in range(nh)]
        on.append([_rms(o[h], nw) * _silu(gz[j][h]) for h in range(nh)])
    return on, s, t


def _ssd_chunk(lanes, x, bm, cm, sm, alog_row, dtb_row, d_row, hs):
    subs, pairs = range(len(x)), range(len(x[0]))
    c = x[0][0].shape[0]
    lane_i = lax.broadcasted_iota(jnp.int32, (c, LANE), 1)
    lane_lo = lane_i < SSD_P
    lane_lo1 = lax.broadcasted_iota(jnp.int32, (1, LANE), 1) < SSD_P
    row_i = lax.broadcasted_iota(jnp.int32, (c, LANE), 0)
    causal2 = row_i >= jnp.bitwise_and(lane_i, c - 1)
    diag2 = row_i == jnp.bitwise_and(lane_i, c - 1)
    to_row = lambda blk: jnp.sum(jnp.where(diag2, blk, 0.0), axis=0, keepdims=True)
    per = lambda f: [[f(j, p) for p in pairs] for j in subs]
    both = lambda blk, p: [_lane_col(blk, lanes[p][h]) for h in range(2)]
    dt_blk = [_softplus(sm[j] + dtb_row) for j in subs]
    a_blk = [dt_blk[j] * -jnp.exp(alog_row) for j in subs]
    alast_blk = [jnp.sum(a_blk[j], axis=0, keepdims=True) for j in subs]
    dt = per(lambda j, p: both(dt_blk[j], p))
    nexp = [both(-jnp.exp(alog_row), p) for p in pairs]
    alast = per(lambda j, p: both(alast_blk[j], p))
    dp = [both(d_row, p) for p in pairs]
    dt_lane = per(lambda j, p: jnp.where(lane_lo, dt[j][p][0], dt[j][p][1]))
    a_lane = per(lambda j, p: dt_lane[j][p] * jnp.where(lane_lo1, nexp[p][0], nexp[p][1]))
    a_row = per(lambda j, p: to_row(a_lane[j][p]))
    part = per(lambda j, p: jnp.where(causal2, a_row[j][p], 0.0))
    acum = per(lambda j, p: [jnp.sum(jnp.where(lane_lo, part[j][p], 0.0), axis=1, keepdims=True),
                             jnp.sum(jnp.where(lane_lo, 0.0, part[j][p]), axis=1, keepdims=True)])
    acum_col = per(lambda j, p: jnp.where(lane_lo, acum[j][p][0], acum[j][p][1]))
    acum_row = per(lambda j, p: to_row(acum_col[j][p]))
    lmat = per(lambda j, p: jnp.where(causal2, jnp.exp(jnp.where(causal2, acum_col[j][p] - acum_row[j][p], 0.0)), 0.0))
    cb = [_nt(cm[j], jnp.concatenate([bm[j], bm[j]], axis=0)) for j in subs]
    xdt = per(lambda j, p: x[j][p] * dt_lane[j][p])
    xdt_rows = per(lambda j, p: jnp.concatenate([jnp.where(lane_lo, xdt[j][p], 0.0),
                                                 jnp.where(lane_lo, 0.0, xdt[j][p])], axis=0))
    intra = per(lambda j, p: _nn(cb[j] * lmat[j][p], xdt_rows[j][p]))
    skip = per(lambda j, p: x[j][p] * jnp.where(lane_lo1, dp[p][0], dp[p][1]))
    eacc = per(lambda j, p: jnp.exp(acum_col[j][p]))
    alast_row = per(lambda j, p: jnp.where(lane_lo1, alast[j][p][0], alast[j][p][1]))
    wdec = per(lambda j, p: jnp.exp(alast_row[j][p] - acum_col[j][p]))
    scale = per(lambda j, p: jnp.exp(alast_row[j][p]))
    upd = per(lambda j, p: _tn(bm[j], xdt[j][p] * wdec[j][p]))
    y = []
    for j in subs:
        y.append([skip[j][p] + intra[j][p] + _nn(cm[j], hs[p]) * eacc[j][p] for p in pairs])
        hs = [hs[p] * scale[j][p] + upd[j][p] for p in pairs]
    return y, hs


def _ret_chunk(rq, rk, rv, rg, cos2, sin2, dmat, qdec, kdec, cdec, nw, r):
    subs, hs = range(len(rq)), range(len(rq[0]))
    per = lambda f: [[f(j, h) for h in hs] for j in subs]
    q = per(lambda j, h: rq[j][h] * cos2[j] + _swap_halves(rq[j][h]) * sin2[j])
    k = per(lambda j, h: (rk[j][h] * cos2[j] + _swap_halves(rk[j][h]) * sin2[j]) * (RET_DK ** -0.5))
    s = per(lambda j, h: _nt(q[j][h], k[j][h]) * dmat[h])
    intra = per(lambda j, h: _nn(s[j][h], rv[j][h]))
    upd = per(lambda j, h: _tn(k[j][h] * kdec[h], rv[j][h]))
    gate = per(lambda j, h: _silu(rg[j][h]))
    on = []
    for j in subs:
        on.append([_rms(intra[j][h] + _nn(q[j][h], r[h]) * qdec[h], nw) * gate[j][h] for h in hs])
        r = [r[h] * cdec[h] + upd[j][h] for h in hs]
    return on, r


MM_TILE = 1024


def _tile(n, pref=MM_TILE):
    return pref if n % pref == 0 else n


def _matmul(name, a, b, mode, m, n, k, tk=None, comm=None, out_dtype=F32):
    tm, tn = _tile(m), _tile(n)
    tk = _tile(k) if tk is None else tk
    nk = k // tk
    ca, cb = {"nn": (1, 0), "nt": (1, 1), "tn": (0, 0)}[mode]
    grid = (m // tm, n // tn, nk)
    n_comm = len(comm[1]) if comm else 0
    use_acc = out_dtype != F32 and nk > 1

    def at_step(which):
        return functools.reduce(jnp.logical_and, [pl.program_id(d) == (0 if which == 0 else grid[d] - 1)
                                                  for d in range(3)])

    def body(*refs):
        a_ref, b_ref = refs[:2]
        c_in = refs[2:2 + n_comm]
        o_ref = refs[2 + n_comm]
        c_out = refs[3 + n_comm:3 + 2 * n_comm]
        acc_ref = refs[3 + 2 * n_comm] if use_acc else o_ref
        sems = refs[3 + 2 * n_comm + use_acc:]
        if comm:
            @pl.when(at_step(0))
            def _():
                _comm_plan(comm[0], c_in, c_out, *sems)[0]()

        part = lax.dot_general(a_ref[...].astype(BF16), b_ref[...].astype(BF16),
                               (((ca,), (cb,)), ((), ())), preferred_element_type=F32)
        if nk == 1:
            o_ref[...] = part.astype(o_ref.dtype)
        else:
            kk = pl.program_id(2)

            @pl.when(kk == 0)
            def _():
                acc_ref[...] = part

            @pl.when(kk > 0)
            def _():
                acc_ref[...] += part

            if use_acc:
                @pl.when(kk == nk - 1)
                def _():
                    o_ref[...] = acc_ref[...].astype(o_ref.dtype)

        if comm:
            @pl.when(at_step(1))
            def _():
                _comm_plan(comm[0], c_in, c_out, *sems)[1]()

    a_spec = pl.BlockSpec((tk, tm), lambda i, j, kk: (kk, i)) if mode == "tn" else pl.BlockSpec((tm, tk), lambda i, j, kk: (i, kk))
    b_spec = pl.BlockSpec((tn, tk), lambda i, j, kk: (j, kk)) if mode == "nt" else pl.BlockSpec((tk, tn), lambda i, j, kk: (kk, j))
    hbm = pl.BlockSpec(memory_space=pl.ANY)
    res = pl.pallas_call(
        body, name=name, grid=grid,
        in_specs=[a_spec, b_spec] + [hbm] * n_comm,
        out_specs=[pl.BlockSpec((tm, tn), lambda i, j, kk: (i, j))] + [hbm] * n_comm,
        out_shape=[jax.ShapeDtypeStruct((m, n), out_dtype)] + (_comm_out_shapes(*comm) if comm else []),
        scratch_shapes=([pltpu.VMEM((tm, tn), F32)] if use_acc else []) + (_comm_scratch(n_comm) if comm else []),
        compiler_params=_cparams(("arbitrary",) * 3 if comm else ("parallel", "parallel", "arbitrary"), VMEM_LIMIT),
    )(a, b, *(comm[1] if comm else []))
    return res if comm else res[0]


def _rowmap(name, fn, length, tm, rows, params, out_ws=(), acc_ws=(), out_dtypes=None, cts=None, d_dtypes=None,
            place=None):
    nt_ = length // tm
    n_r, n_p = len(rows), len(params)
    n_in = n_r + n_p
    rspec = lambda bw, cbk: pl.BlockSpec((tm, bw), lambda i: (i, cbk))
    pspec = lambda w: pl.BlockSpec((1, w), lambda i: (0, 0))
    in_arrays = [r[0] for r in rows] + list(params)
    in_specs = [rspec(r[1], r[2]) for r in rows] + [pspec(p.shape[1]) for p in params]
    place = place or {}

    def load(refs):
        return [r[...].astype(F32) for r in refs]

    def placed(widths, dtypes, n_before):
        specs, shapes, extra, alias = [], [], [], {}
        for k, (w, dt) in enumerate(zip(widths, dtypes)):
            arr, total, cbk = place.get(k, (None, w, 0))
            specs.append(rspec(w, cbk))
            shapes.append(jax.ShapeDtypeStruct((length, total), dt))
            if arr is not None:
                alias[n_before + len(extra)] = k
                extra.append(arr)
        return specs, shapes, extra, alias

    def accumulate(a_refs, vals):
        first = pl.program_id(0) == 0
        for a_ref, v in zip(a_refs, vals):
            @pl.when(first)
            def _(a_ref=a_ref):
                a_ref[...] = jnp.zeros_like(a_ref)
            a_ref[...] += v

    hbm = pl.BlockSpec(memory_space=pl.ANY)
    if cts is None:
        n_o = len(out_ws)
        o_specs, o_shapes, extra, alias = placed(out_ws, out_dtypes or [F32] * n_o, n_in)

        def body(*refs):
            outs, accs = fn(*load(refs[:n_in]))
            o_refs = refs[n_in + len(extra):]
            for o_ref, o in zip(o_refs[:n_o], outs):
                o_ref[...] = o.astype(o_ref.dtype)
            accumulate(o_refs[n_o:], accs)

        res = pl.pallas_call(
            body, name=name, grid=(nt_,), in_specs=in_specs + [hbm] * len(extra),
            out_specs=o_specs + [pspec(w) for w in acc_ws],
            out_shape=o_shapes + [jax.ShapeDtypeStruct((1, w), F32) for w in acc_ws],
            input_output_aliases=alias, compiler_params=_cparams(("arbitrary",), VMEM_LIMIT),
        )(*in_arrays, *extra)
        return res[:n_o], res[n_o:]

    n_c = len(cts)
    o_specs, o_shapes, extra, alias = placed([r[1] for r in rows], d_dtypes or [F32] * n_r, n_in + n_c)

    def body(*refs):
        ins = load(refs[:n_in])
        _, vjp = jax.vjp(lambda rs, ps: fn(*rs, *ps)[0], ins[:n_r], ins[n_r:])
        d_rows, d_params = vjp(tuple(load(refs[n_in:n_in + n_c])))
        o_refs = refs[n_in + n_c + len(extra):]
        for o_ref, d in zip(o_refs[:n_r], d_rows):
            o_ref[...] = d.astype(o_ref.dtype)
        accumulate(o_refs[n_r:], d_params)

    res = pl.pallas_call(
        body, name=name, grid=(nt_,),
        in_specs=in_specs + [rspec(c[1], c[2]) for c in cts] + [hbm] * len(extra),
        out_specs=o_specs + [pspec(p.shape[1]) for p in params],
        out_shape=o_shapes + [jax.ShapeDtypeStruct((1, p.shape[1]), F32) for p in params],
        input_output_aliases=alias, compiler_params=_cparams(("arbitrary",), VMEM_LIMIT),
    )(*in_arrays, *[c[0] for c in cts], *extra)
    return res[:n_r], res[n_r:]


def _conv_shift(ext, k, tm, forward):
    s = CONV_W - 1 - k
    if forward:
        rolled = ext if s == 0 else pltpu.roll(ext, s, 0)
        return rolled[SUBLANE:, :]
    rolled = ext if s == 0 else pltpu.roll(ext, tm + SUBLANE - s, 0)
    return rolled[:tm, :]


def _conv_pre(x_ref, halo_ref, w_ref, b_ref, first, tm):
    halo = jnp.where(first, 0.0, halo_ref[...])
    ext = jnp.concatenate([halo, x_ref[...]], axis=0)
    w = w_ref[...]
    pre = b_ref[...] + jnp.zeros_like(x_ref[...])
    taps = []
    for k in range(CONV_W):
        tap = _conv_shift(ext, k, tm, True)
        taps.append(tap)
        pre = pre + tap * w[k:k + 1, :]
    return pre, taps


def _conv_fwd(proj, w, b, length, tm, tc):
    hb = tm // SUBLANE
    j0 = CONV_OFF // tc

    def body(x_ref, halo_ref, w_ref, b_ref, o_ref):
        pre, _ = _conv_pre(x_ref, halo_ref, w_ref, b_ref, pl.program_id(0) == 0, tm)
        o_ref[...] = _silu(pre)

    return pl.pallas_call(
        body, name="conv_fwd", grid=(length // tm, CONV_ALL // tc),
        in_specs=[pl.BlockSpec((tm, tc), lambda i, j: (i, j0 + j)),
                  pl.BlockSpec((SUBLANE, tc), lambda i, j: (jnp.maximum(i * hb - 1, 0), j0 + j)),
                  pl.BlockSpec((CONV_W, tc), lambda i, j: (0, j)),
                  pl.BlockSpec((1, tc), lambda i, j: (0, j))],
        out_specs=pl.BlockSpec((tm, tc), lambda i, j: (i, j)),
        out_shape=jax.ShapeDtypeStruct((length, CONV_ALL), F32),
        compiler_params=_cparams(("parallel", "parallel"), VMEM_LIMIT),
    )(proj, proj, w, b)


def _conv_bwd_pre(proj, w, b, dxc, length, tm, tc):
    hb = tm // SUBLANE
    j0 = CONV_OFF // tc

    def body(x_ref, halo_ref, w_ref, b_ref, dy_ref, dpre_ref, dw_ref, db_ref):
        i = pl.program_id(1)
        pre, taps = _conv_pre(x_ref, halo_ref, w_ref, b_ref, i == 0, tm)
        sg = jax.nn.sigmoid(pre)
        dpre = dy_ref[...] * (sg * (1.0 + pre * (1.0 - sg)))
        dpre_ref[...] = dpre

        @pl.when(i == 0)
        def _():
            dw_ref[...] = jnp.zeros_like(dw_ref)
            db_ref[...] = jnp.zeros_like(db_ref)

        for k in range(CONV_W):
            dw_ref[k:k + 1, :] += jnp.sum(dpre * taps[k], axis=0, keepdims=True)
        db_ref[...] += jnp.sum(dpre, axis=0, keepdims=True)

    return pl.pallas_call(
        body, name="conv_bwd_pre", grid=(CONV_ALL // tc, length // tm),
        in_specs=[pl.BlockSpec((tm, tc), lambda j, i: (i, j0 + j)),
                  pl.BlockSpec((SUBLANE, tc), lambda j, i: (jnp.maximum(i * hb - 1, 0), j0 + j)),
                  pl.BlockSpec((CONV_W, tc), lambda j, i: (0, j)),
                  pl.BlockSpec((1, tc), lambda j, i: (0, j)),
                  pl.BlockSpec((tm, tc), lambda j, i: (i, j))],
        out_specs=[pl.BlockSpec((tm, tc), lambda j, i: (i, j)),
                   pl.BlockSpec((CONV_W, tc), lambda j, i: (0, j)),
                   pl.BlockSpec((1, tc), lambda j, i: (0, j))],
        out_shape=[jax.ShapeDtypeStruct((length, CONV_ALL), F32),
                   jax.ShapeDtypeStruct((CONV_W, CONV_ALL), F32),
                   jax.ShapeDtypeStruct((1, CONV_ALL), F32)],
        compiler_params=_cparams(("parallel", "arbitrary"), VMEM_LIMIT),
    )(proj, proj, w, b, dxc)


def _conv_bwd_x(dpre, w, dproj, length, tm, tc):
    hb = tm // SUBLANE
    n_t = length // tm
    last_blk = length // SUBLANE - 1
    j0 = CONV_OFF // tc

    def body(d_ref, halo_ref, w_ref, _, o_ref):
        halo = jnp.where(pl.program_id(0) == n_t - 1, 0.0, halo_ref[...])
        ext = jnp.concatenate([d_ref[...], halo], axis=0)
        w = w_ref[...]
        acc = jnp.zeros_like(d_ref[...])
        for k in range(CONV_W):
            acc = acc + _conv_shift(ext, k, tm, False) * w[k:k + 1, :]
        o_ref[...] = acc.astype(o_ref.dtype)

    return pl.pallas_call(
        body, name="conv_bwd_x", grid=(n_t, CONV_ALL // tc),
        in_specs=[pl.BlockSpec((tm, tc), lambda i, j: (i, j)),
                  pl.BlockSpec((SUBLANE, tc), lambda i, j: (jnp.minimum((i + 1) * hb, last_blk), j)),
                  pl.BlockSpec((CONV_W, tc), lambda i, j: (0, j)),
                  pl.BlockSpec(memory_space=pl.ANY)],
        out_specs=pl.BlockSpec((tm, tc), lambda i, j: (i, j0 + j)),
        out_shape=jax.ShapeDtypeStruct(dproj.shape, dproj.dtype), input_output_aliases={3: 0},
        compiler_params=_cparams(("parallel", "parallel"), VMEM_LIMIT),
    )(dpre, dpre, w, dproj)


HEAD_SCAN_CHUNKS_PER_STEP = 8
assert GDN_HEADS == RET_HEADS


def _scan_steps(length):
    nc = length // CHUNK
    gs = HEAD_SCAN_CHUNKS_PER_STEP if nc % HEAD_SCAN_CHUNKS_PER_STEP == 0 else 1
    return gs, nc // gs, gs * CHUNK


def _head_tiles(ref, gs):
    return [[ref[j * CHUNK:(j + 1) * CHUNK, h * LANE:(h + 1) * LANE] for h in range(GDN_HEADS)] for j in range(gs)]


def _chunk_rows(ref, gs):
    return [ref[j * CHUNK:(j + 1) * CHUNK, :] for j in range(gs)]


def _gdn_fwd(xc, proj, alog, dtb, nw, length, comm=None):
    gs, steps, rows = _scan_steps(length)
    h_ = GDN_HEADS
    hw = h_ * LANE
    tn_ = h_ * CHUNK
    n_comm = len(comm[1]) if comm else 0

    def body(*refs):
        q_ref, k_ref, v_ref, gz_ref, sm_ref, al_ref, dt_ref, nw_ref = refs[:8]
        c_in = refs[8:8 + n_comm]
        on_ref, st_ref, t_ref = refs[8 + n_comm:11 + n_comm]
        c_out = refs[11 + n_comm:11 + 2 * n_comm]
        s_ref = refs[11 + 2 * n_comm]
        sems = refs[12 + 2 * n_comm:]

        @pl.when(pl.program_id(0) == 0)
        def _():
            s_ref[...] = jnp.zeros_like(s_ref)
            if comm:
                _comm_plan(comm[0], c_in, c_out, *sems)[0]()

        st_ref[...] = s_ref[...]
        lead = lambda ref: [ref[h] for h in range(h_)]
        on, s_new, t = _gdn_chunk(_head_tiles(q_ref, gs), _head_tiles(k_ref, gs), _head_tiles(v_ref, gs),
                                  _head_tiles(gz_ref, gs), _chunk_rows(sm_ref, gs), al_ref[...], dt_ref[...],
                                  nw_ref[...], lead(s_ref))
        for j in range(gs):
            t_ref[j] = t[j]
        for h in range(h_):
            for j in range(gs):
                on_ref[j * CHUNK:(j + 1) * CHUNK, h * LANE:(h + 1) * LANE] = on[j][h].astype(on_ref.dtype)
            s_ref[h] = s_new[h]

        if comm:
            @pl.when(pl.program_id(0) == steps - 1)
            def _():
                _comm_plan(comm[0], c_in, c_out, *sems)[1]()

    blk = lambda col: pl.BlockSpec((rows, hw), lambda c: (c, col // hw))
    one = pl.BlockSpec((1, LANE), lambda c: (0, 0))
    hbm = pl.BlockSpec(memory_space=pl.ANY)
    return pl.pallas_call(
        body, name="gdn_fwd_" + comm[0] if comm else "gdn_fwd", grid=(steps,),
        in_specs=[blk(XC("gq")), blk(XC("gk")), blk(XC("gv")), blk(_MINE["gz"]),
                  pl.BlockSpec((rows, LANE), lambda c: (c, SMALL_OFF // LANE)), one, one,
                  pl.BlockSpec((1, LANE), lambda c: (0, 0))] + [hbm] * n_comm,
        out_specs=[pl.BlockSpec((rows, hw), lambda c: (c, 0)),
                   pl.BlockSpec((h_, None, LANE, LANE), lambda c: (0, c, 0, 0)),
                   pl.BlockSpec((gs, tn_, tn_), lambda c: (c, 0, 0))] + [hbm] * n_comm,
        out_shape=[jax.ShapeDtypeStruct((length, MIX_W), BF16),
                   jax.ShapeDtypeStruct((h_, steps, LANE, LANE), F32),
                   jax.ShapeDtypeStruct((length // CHUNK, tn_, tn_), F32)]
                  + (_comm_out_shapes(*comm) if comm else []),
        scratch_shapes=[pltpu.VMEM((h_, LANE, LANE), F32)] + (_comm_scratch(n_comm) if comm else []),
        compiler_params=_cparams(("arbitrary",), VMEM_LIMIT),
    )(xc, xc, xc, proj, proj, alog, dtb, nw, *(comm[1] if comm else []))


def _gdn_bwd(xc, proj, alog, dtb, nw, st, tinv, dmix, length, comm=None):
    gs, steps, rows = _scan_steps(length)
    h_ = GDN_HEADS
    hw = h_ * LANE
    tn_ = h_ * CHUNK
    n_comm = len(comm[1]) if comm else 0
    n_in, n_out = 11, 6

    def body(*refs):
        q_ref, k_ref, v_ref, gz_ref, sm_ref, al_ref, dt_ref, nw_ref, st_ref, t_ref, do_ref = refs[:n_in]
        c_in = refs[n_in:n_in + n_comm]
        o0 = n_in + n_comm
        dqkv_ref, dgz_ref, dsm_ref, dal_ref, ddt_ref, dnw_ref = refs[o0:o0 + n_out]
        c_out = refs[o0 + n_out:o0 + n_out + n_comm]
        ds_ref = refs[o0 + n_out + n_comm]
        sems = refs[o0 + n_out + n_comm + 1:]

        @pl.when(pl.program_id(0) == 0)
        def _():
            ds_ref[...] = jnp.zeros_like(ds_ref)
            dal_ref[...] = jnp.zeros_like(dal_ref)
            ddt_ref[...] = jnp.zeros_like(ddt_ref)
            dnw_ref[...] = jnp.zeros_like(dnw_ref)
            if comm:
                _comm_plan(comm[0], c_in, c_out, *sems)[0]()

        lead = lambda ref: [ref[h] for h in range(h_)]
        ins = (_head_tiles(q_ref, gs), _head_tiles(k_ref, gs), _head_tiles(v_ref, gs), _head_tiles(gz_ref, gs),
               _chunk_rows(sm_ref, gs), al_ref[...], dt_ref[...], nw_ref[...], lead(st_ref))
        t_saved = [t_ref[j] for j in range(gs)]
        _, vjp = jax.vjp(lambda *a: _gdn_chunk(*a, t_saved=t_saved)[:2], *ins)
        dq, dk, dv, dgz, dsm, dal, ddt, dnw, ds = vjp((_head_tiles(do_ref, gs), lead(ds_ref)))
        for j in range(gs):
            rws = slice(j * CHUNK, (j + 1) * CHUNK)
            for h in range(h_):
                cols = slice(h * LANE, (h + 1) * LANE)
                for part, d in enumerate((dq, dk, dv)):
                    dqkv_ref[rws, part * hw + h * LANE:part * hw + (h + 1) * LANE] = d[j][h]
                dgz_ref[rws, cols] = dgz[j][h].astype(dgz_ref.dtype)
            dsm_ref[rws, :] = dsm[j]
        for h in range(h_):
            ds_ref[h] = ds[h]
        dal_ref[...] += dal
        ddt_ref[...] += ddt
        dnw_ref[...] += dnw

        if comm:
            @pl.when(pl.program_id(0) == steps - 1)
            def _():
                _comm_plan(comm[0], c_in, c_out, *sems)[1]()

    r = lambda c: steps - 1 - c
    blk = lambda col: pl.BlockSpec((rows, hw), lambda c: (r(c), col // hw))
    one = pl.BlockSpec((1, LANE), lambda c: (0, 0))
    hbm = pl.BlockSpec(memory_space=pl.ANY)
    return pl.pallas_call(
        body, name="gdn_bwd_" + comm[0] if comm else "gdn_bwd", grid=(steps,),
        in_specs=[blk(XC("gq")), blk(XC("gk")), blk(XC("gv")), blk(_MINE["gz"]),
                  pl.BlockSpec((rows, LANE), lambda c: (r(c), SMALL_OFF // LANE)), one, one,
                  pl.BlockSpec((1, LANE), lambda c: (0, 0)),
                  pl.BlockSpec((h_, None, LANE, LANE), lambda c: (0, r(c), 0, 0)),
                  pl.BlockSpec((gs, tn_, tn_), lambda c: (r(c), 0, 0)),
                  blk(0)] + [hbm] * n_comm,
        out_specs=[pl.BlockSpec((rows, 3 * hw), lambda c: (r(c), 0)), blk(_MINE["gz"]),
                   pl.BlockSpec((rows, LANE), lambda c: (r(c), 0)), one, one,
                   pl.BlockSpec((1, LANE), lambda c: (0, 0))] + [hbm] * n_comm,
        out_shape=[jax.ShapeDtypeStruct((length, CONV_ALL), F32), jax.ShapeDtypeStruct((length, N_PAD), BF16),
                   jax.ShapeDtypeStruct((length, LANE), F32),
                   jax.ShapeDtypeStruct((1, LANE), F32), jax.ShapeDtypeStruct((1, LANE), F32),
                   jax.ShapeDtypeStruct((1, LANE), F32)] + (_comm_out_shapes(*comm) if comm else []),
        scratch_shapes=[pltpu.VMEM((h_, LANE, LANE), F32)] + (_comm_scratch(n_comm) if comm else []),
        compiler_params=_cparams(("arbitrary",), VMEM_LIMIT),
    )(xc, xc, xc, proj, proj, alog, dtb, nw, st, tinv, dmix, *(comm[1] if comm else []))


SSD_CHUNKS_PER_STEP = 8


def _ssd_steps(length):
    nc = length // CHUNK
    gs = SSD_CHUNKS_PER_STEP if nc % SSD_CHUNKS_PER_STEP == 0 else 1
    return gs, nc // gs, gs * CHUNK


def _ssd_lanes(g):
    base = _MINE["sdt"] - SMALL_OFF
    return [[base + (g * PAIRS_PER_GROUP + p) * 2 + j for j in range(2)] for p in range(PAIRS_PER_GROUP)]


def _ssd_tiles(ref, gs):
    return [[ref[j * CHUNK:(j + 1) * CHUNK, p * LANE:(p + 1) * LANE] for p in range(PAIRS_PER_GROUP)]
            for j in range(gs)]


def _ssd_gate(y, z, w):
    return _rms(y * _silu(z), w)


def _ssd_fwd(xc, proj, alog_row, dtb_row, d_row, norm_w, mixed, length):
    gs, steps, rows = _ssd_steps(length)
    ppg = PAIRS_PER_GROUP
    gw = ppg * LANE

    def body(x_ref, b_ref, c_ref, sm_ref, al_ref, db_ref, dp_ref, z_ref, nw_ref, _, y_ref, o_ref, st_ref, hs_ref):
        @pl.when(pl.program_id(1) == 0)
        def _():
            hs_ref[...] = jnp.zeros_like(hs_ref)

        st_ref[...] = hs_ref[...]
        y, hs_new = _ssd_chunk(_ssd_lanes(pl.program_id(0)), _ssd_tiles(x_ref, gs), _chunk_rows(b_ref, gs),
                               _chunk_rows(c_ref, gs), _chunk_rows(sm_ref, gs), al_ref[...], db_ref[...],
                               dp_ref[...], [hs_ref[p] for p in range(ppg)])
        for p in range(ppg):
            hs_ref[p] = hs_new[p]
        for j in range(gs):
            rws = slice(j * CHUNK, (j + 1) * CHUNK)
            y_j = jnp.concatenate(y[j], axis=1)
            y_ref[rws, :] = y_j
            o_ref[rws, :] = _ssd_gate(y_j, z_ref[rws, :], nw_ref[...]).astype(o_ref.dtype)

    one = pl.BlockSpec((1, LANE), lambda g, c: (0, 0))
    return pl.pallas_call(
        body, name="ssd_fwd", grid=(SSD_GROUPS, steps),
        in_specs=[pl.BlockSpec((rows, gw), lambda g, c: (c, XC("sx") // gw + g)),
                  pl.BlockSpec((rows, LANE), lambda g, c: (c, XC("sB") // LANE + g)),
                  pl.BlockSpec((rows, LANE), lambda g, c: (c, XC("sC") // LANE + g)),
                  pl.BlockSpec((rows, LANE), lambda g, c: (c, SMALL_OFF // LANE)),
                  one, one, one,
                  pl.BlockSpec((rows, gw), lambda g, c: (c, _MINE["sz"] // gw + g)),
                  pl.BlockSpec((None, 1, gw), lambda g, c: (g, 0, 0)),
                  pl.BlockSpec(memory_space=pl.ANY)],
        out_specs=[pl.BlockSpec((rows, gw), lambda g, c: (c, g)),
                   pl.BlockSpec((rows, gw), lambda g, c: (c, 1 + g)),
                   pl.BlockSpec((None, ppg, None, LANE, SSD_N), lambda g, c: (g, 0, c, 0, 0))],
        out_shape=[jax.ShapeDtypeStruct((length, SSD_GROUPS * gw), F32),
                   jax.ShapeDtypeStruct(mixed.shape, mixed.dtype),
                   jax.ShapeDtypeStruct((SSD_GROUPS, ppg, steps, LANE, SSD_N), F32)],
        input_output_aliases={9: 1},
        scratch_shapes=[pltpu.VMEM((ppg, LANE, SSD_N), F32)],
        compiler_params=_cparams(("parallel", "arbitrary"), VMEM_LIMIT),
    )(xc, xc, xc, proj, alog_row, dtb_row, d_row, proj, norm_w, mixed)


def _ssd_bwd(xc, proj, alog_row, dtb_row, d_row, norm_w, st, y_b, dmix, dproj, dxc, length):
    gs, steps, rows = _ssd_steps(length)
    ppg = PAIRS_PER_GROUP
    gw = ppg * LANE

    def body(x_ref, b_ref, c_ref, sm_ref, al_ref, db_ref, dp_ref, z_ref, nw_ref, st_ref, y_ref, do_ref, _, __,
             dxc_ref, dz_ref, dsm_ref, dal_ref, ddb_ref, ddp_ref, dnw_ref, dhs_ref):
        g = pl.program_id(1)

        @pl.when(jnp.logical_and(pl.program_id(0) == 0, g == 0))
        def _():
            dhs_ref[...] = jnp.zeros_like(dhs_ref)
            dal_ref[...] = jnp.zeros_like(dal_ref)
            ddb_ref[...] = jnp.zeros_like(ddb_ref)
            ddp_ref[...] = jnp.zeros_like(ddp_ref)
            dnw_ref[...] = jnp.zeros_like(dnw_ref)

        pairs = range(ppg)
        dy, dnw = [], jnp.zeros((1, gw), F32)
        for j in range(gs):
            rws = slice(j * CHUNK, (j + 1) * CHUNK)
            _, gate_vjp = jax.vjp(_ssd_gate, y_ref[rws, :], z_ref[rws, :], nw_ref[...])
            dy_j, dz_j, dnw_j = gate_vjp(do_ref[rws, :].astype(F32))
            dz_ref[rws, :] = dz_j.astype(dz_ref.dtype)
            dy.append([dy_j[:, p * LANE:(p + 1) * LANE] for p in pairs])
            dnw = dnw + dnw_j
        dnw_ref[g] += dnw
        ins = (_ssd_tiles(x_ref, gs), _chunk_rows(b_ref, gs), _chunk_rows(c_ref, gs), _chunk_rows(sm_ref, gs),
               al_ref[...], db_ref[...], dp_ref[...], [st_ref[p] for p in pairs])
        _, vjp = jax.vjp(functools.partial(_ssd_chunk, _ssd_lanes(g)), *ins)
        dx, dbm, dcm, dsm, dal, ddb, ddp, dhs = vjp((dy, [dhs_ref[g, p] for p in pairs]))
        for grp in range(SSD_GROUPS):
            @pl.when(g == grp)
            def _(grp=grp):
                for j in range(gs):
                    rws = slice(j * CHUNK, (j + 1) * CHUNK)
                    for p in pairs:
                        col = grp * gw + p * LANE
                        dxc_ref[rws, col:col + LANE] = dx[j][p]
                    cb_ = XC("sB") - XC("sx") + grp * SSD_N
                    cc_ = XC("sC") - XC("sx") + grp * SSD_N
                    dxc_ref[rws, cb_:cb_ + SSD_N] = dbm[j]
                    dxc_ref[rws, cc_:cc_ + SSD_N] = dcm[j]
        for j in range(gs):
            dsm_ref[j * CHUNK:(j + 1) * CHUNK, :] = dsm[j]
        for p in pairs:
            dhs_ref[g, p] = dhs[p]
        dal_ref[g] += dal
        ddb_ref[g] += ddb
        ddp_ref[g] += ddp

    r = lambda c: steps - 1 - c
    one = pl.BlockSpec((1, LANE), lambda c, g: (0, 0))
    acc = pl.BlockSpec((SSD_GROUPS, 1, LANE), lambda c, g: (0, 0, 0))
    prm = jax.ShapeDtypeStruct((SSD_GROUPS, 1, LANE), F32)
    sz_blk = _MINE["sz"] // gw
    return pl.pallas_call(
        body, name="ssd_bwd", grid=(steps, SSD_GROUPS),
        in_specs=[pl.BlockSpec((rows, gw), lambda c, g: (r(c), XC("sx") // gw + g)),
                  pl.BlockSpec((rows, LANE), lambda c, g: (r(c), XC("sB") // LANE + g)),
                  pl.BlockSpec((rows, LANE), lambda c, g: (r(c), XC("sC") // LANE + g)),
                  pl.BlockSpec((rows, LANE), lambda c, g: (r(c), SMALL_OFF // LANE)),
                  one, one, one,
                  pl.BlockSpec((rows, gw), lambda c, g: (r(c), sz_blk + g)),
                  pl.BlockSpec((None, 1, gw), lambda c, g: (g, 0, 0)),
                  pl.BlockSpec((None, ppg, None, LANE, SSD_N), lambda c, g: (g, 0, r(c), 0, 0)),
                  pl.BlockSpec((rows, gw), lambda c, g: (r(c), g)),
                  pl.BlockSpec((rows, gw), lambda c, g: (r(c), 1 + g)),
                  pl.BlockSpec(memory_space=pl.ANY), pl.BlockSpec(memory_space=pl.ANY)],
        out_specs=[pl.BlockSpec((rows, CONV_CH), lambda c, g: (r(c), 1)),
                   pl.BlockSpec((rows, gw), lambda c, g: (r(c), sz_blk + g)),
                   pl.BlockSpec((None, rows, LANE), lambda c, g: (g, r(c), 0)), acc, acc, acc,
                   pl.BlockSpec((SSD_GROUPS, 1, gw), lambda c, g: (0, 0, 0))],
        out_shape=[jax.ShapeDtypeStruct(dxc.shape, dxc.dtype), jax.ShapeDtypeStruct(dproj.shape, dproj.dtype),
                   jax.ShapeDtypeStruct((SSD_GROUPS, length, LANE), F32), prm, prm, prm,
                   jax.ShapeDtypeStruct((SSD_GROUPS, 1, gw), F32)],
        input_output_aliases={12: 1, 13: 0},
        scratch_shapes=[pltpu.VMEM((SSD_GROUPS, ppg, LANE, SSD_N), F32)],
        compiler_params=_cparams(("arbitrary", "arbitrary"), VMEM_LIMIT),
    )(xc, xc, xc, proj, alog_row, dtb_row, d_row, proj, norm_w, st, y_b, dmix, dproj, dxc)


def _ret_fwd(proj, cos2, sin2, dmat, qdec, kdec, cdec, nw, mixed, length):
    gs, steps, rows = _scan_steps(length)
    h_ = RET_HEADS

    def body(q_ref, k_ref, v_ref, g_ref, cos_ref, sin_ref, dm_ref, qd_ref, kd_ref, cd_ref, nw_ref, _,
             on_ref, st_ref, r_ref):
        @pl.when(pl.program_id(0) == 0)
        def _():
            r_ref[...] = jnp.zeros_like(r_ref)

        st_ref[...] = r_ref[...]
        lead = lambda ref: [ref[h] for h in range(h_)]
        on, r_new = _ret_chunk(_head_tiles(q_ref, gs), _head_tiles(k_ref, gs), _head_tiles(v_ref, gs),
                               _head_tiles(g_ref, gs), _chunk_rows(cos_ref, gs), _chunk_rows(sin_ref, gs),
                               lead(dm_ref), lead(qd_ref), lead(kd_ref), lead(cd_ref), nw_ref[...], lead(r_ref))
        for h in range(h_):
            for j in range(gs):
                on_ref[j * CHUNK:(j + 1) * CHUNK, h * LANE:(h + 1) * LANE] = on[j][h].astype(on_ref.dtype)
            r_ref[h] = r_new[h]

    hw = h_ * LANE
    blk = lambda name: pl.BlockSpec((rows, hw), lambda c: (c, _MINE[name] // hw))
    tab = pl.BlockSpec((rows, LANE), lambda c: (c, 0))
    full = lambda *s: pl.BlockSpec(s, lambda c: (0,) * len(s))
    return pl.pallas_call(
        body, name="ret_fwd", grid=(steps,),
        in_specs=[blk("rq"), blk("rk"), blk("rv"), blk("rg"), tab, tab,
                  full(h_, CHUNK, CHUNK), full(h_, CHUNK, 1), full(h_, CHUNK, 1), full(h_, 1, 1), full(1, LANE),
                  pl.BlockSpec(memory_space=pl.ANY)],
        out_specs=[pl.BlockSpec((rows, hw), lambda c: (c, (MIX_W - hw) // hw)),
                   pl.BlockSpec((h_, None, LANE, LANE), lambda c: (0, c, 0, 0))],
        out_shape=[jax.ShapeDtypeStruct(mixed.shape, mixed.dtype),
                   jax.ShapeDtypeStruct((h_, steps, LANE, LANE), F32)],
        input_output_aliases={11: 0},
        scratch_shapes=[pltpu.VMEM((h_, LANE, LANE), F32)],
        compiler_params=_cparams(("arbitrary",), VMEM_LIMIT),
    )(proj, proj, proj, proj, cos2, sin2, dmat, qdec, kdec, cdec, nw, mixed)


def _ret_bwd(proj, cos2, sin2, dmat, qdec, kdec, cdec, nw, st, dmix, dproj, length):
    gs, steps, rows = _scan_steps(length)
    h_ = RET_HEADS
    hw = h_ * LANE

    def body(q_ref, k_ref, v_ref, g_ref, cos_ref, sin_ref, dm_ref, qd_ref, kd_ref, cd_ref, nw_ref, st_ref,
             do_ref, _, dqkvg_ref, dnw_ref, dr_ref):
        @pl.when(pl.program_id(0) == 0)
        def _():
            dr_ref[...] = jnp.zeros_like(dr_ref)
            dnw_ref[...] = jnp.zeros_like(dnw_ref)

        lead = lambda ref: [ref[h] for h in range(h_)]
        consts = (_chunk_rows(cos_ref, gs), _chunk_rows(sin_ref, gs), lead(dm_ref), lead(qd_ref), lead(kd_ref),
                  lead(cd_ref))
        f = lambda q, k, v, g, w_, r_: _ret_chunk(q, k, v, g, *consts, w_, r_)
        _, vjp = jax.vjp(f, _head_tiles(q_ref, gs), _head_tiles(k_ref, gs), _head_tiles(v_ref, gs),
                         _head_tiles(g_ref, gs), nw_ref[...], lead(st_ref))
        dq, dk, dv, dg, dnw, dr = vjp((_head_tiles(do_ref, gs), lead(dr_ref)))
        for h in range(h_):
            for j in range(gs):
                rws = slice(j * CHUNK, (j + 1) * CHUNK)
                for part, d in enumerate((dq, dk, dv, dg)):
                    col = part * hw + h * LANE
                    dqkvg_ref[rws, col:col + LANE] = d[j][h].astype(dqkvg_ref.dtype)
            dr_ref[h] = dr[h]
        dnw_ref[...] += dnw

    r = lambda c: steps - 1 - c
    blk = lambda name: pl.BlockSpec((rows, hw), lambda c: (r(c), _MINE[name] // hw))
    tab = pl.BlockSpec((rows, LANE), lambda c: (r(c), 0))
    full = lambda *s: pl.BlockSpec(s, lambda c: (0,) * len(s))
    assert _MINE["rq"] % (4 * hw) == 0 and [_MINE[n] - _MINE["rq"] for n in ("rk", "rv", "rg")] == [hw, 2 * hw, 3 * hw]
    return pl.pallas_call(
        body, name="ret_bwd", grid=(steps,),
        in_specs=[blk("rq"), blk("rk"), blk("rv"), blk("rg"), tab, tab,
                  full(h_, CHUNK, CHUNK), full(h_, CHUNK, 1), full(h_, CHUNK, 1), full(h_, 1, 1), full(1, LANE),
                  pl.BlockSpec((h_, None, LANE, LANE), lambda c: (0, r(c), 0, 0)),
                  pl.BlockSpec((rows, hw), lambda c: (r(c), (MIX_W - hw) // hw)),
                  pl.BlockSpec(memory_space=pl.ANY)],
        out_specs=[pl.BlockSpec((rows, 4 * hw), lambda c: (r(c), _MINE["rq"] // (4 * hw))), full(1, LANE)],
        out_shape=[jax.ShapeDtypeStruct(dproj.shape, dproj.dtype), jax.ShapeDtypeStruct((1, LANE), F32)],
        input_output_aliases={13: 0},
        scratch_shapes=[pltpu.VMEM((h_, LANE, LANE), F32)],
        compiler_params=_cparams(("arbitrary",), VMEM_LIMIT),
    )(proj, proj, proj, proj, cos2, sin2, dmat, qdec, kdec, cdec, nw, st, dmix, dproj)


def _mesh_pos():
    x, y, c = lax.axis_index("x"), lax.axis_index("y"), lax.axis_index("c")
    return x, y, c, 4 * x + 2 * y + c


def _peer(x, y, c, mask):
    return (x ^ ((mask >> 2) & 1), y ^ ((mask >> 1) & 1), c ^ (mask & 1))


def _gather_plan(ins, outs, send_sems, recv_sems, local_sems):
    x, y, c, me = _mesh_pos()
    n = len(ins)
    idx = lambda px, py, pc: 4 * px + 2 * py + pc
    sib = (x, y, 1 - c)
    chips = [(1 - x, y), (x, 1 - y), (1 - x, 1 - y)]
    local = [pltpu.make_async_copy(ins[a], outs[a].at[me], local_sems.at[a]) for a in range(n)]

    def copy(a, k, src, slab, to):
        return pltpu.make_async_remote_copy(
            src_ref=src, dst_ref=outs[a].at[slab], send_sem=send_sems.at[7 * a + k],
            recv_sem=recv_sems.at[7 * a + k], device_id=to, device_id_type=pl.DeviceIdType.MESH)

    first = [copy(a, 0, ins[a], me, sib) for a in range(n)]
    first += [copy(a, 1 + j, ins[a], me, (*chip, c)) for a in range(n) for j, chip in enumerate(chips)]
    landed = [[copy(a, 1 + j, ins[a], idx(*chip, c), (*chip, c)) for a in range(n)] for j, chip in enumerate(chips)]
    passed = [[copy(a, 4 + j, outs[a].at[idx(*chip, c)], idx(*chip, c), sib) for a in range(n)]
              for j, chip in enumerate(chips)]
    from_sib = [copy(a, 0, ins[a], idx(*sib), sib) for a in range(n)]
    from_sib += [copy(a, 4 + j, ins[a], idx(*chip, 1 - c), sib) for a in range(n) for j, chip in enumerate(chips)]

    def start():
        for cp in local + first:
            cp.start()

    def wait():
        for j in range(len(chips)):
            for a in range(n):
                landed[j][a].wait_recv()
                passed[j][a].start()
        for cp in from_sib:
            cp.wait_recv()
        for cp in first + [cp for row in passed for cp in row]:
            cp.wait_send()
        for cp in local:
            cp.wait()

    return start, wait


def _comm_plan(kind, ins, outs, send_sems, recv_sems, local_sems):
    if kind == "gather":
        return _gather_plan(ins, outs, send_sems, recv_sems, local_sems)
    x, y, c, me = _mesh_pos()
    n = len(ins)
    src = (lambda a, idx: ins[a]) if kind == "gather" else (lambda a, idx: ins[a].at[idx])
    local = [pltpu.make_async_copy(src(a, me), outs[a].at[me], local_sems.at[a]) for a in range(n)]
    sends, recvs = [], []
    for mask in range(1, N_DEV):
        px, py, pc = _peer(x, y, c, mask)
        pidx = 4 * px + 2 * py + pc
        for a in range(n):
            k = (mask - 1) * n + a
            copy = lambda s, d, k=k: pltpu.make_async_remote_copy(
                src_ref=s, dst_ref=d, send_sem=send_sems.at[k], recv_sem=recv_sems.at[k],
                device_id=(px, py, pc), device_id_type=pl.DeviceIdType.MESH)
            sends.append(copy(src(a, pidx), outs[a].at[me]))
            recvs.append(copy(src(a, me), outs[a].at[pidx]))

    def start():
        for cp in local + sends:
            cp.start()

    def wait():
        for cp in sends:
            cp.wait_send()
        for cp in recvs:
            cp.wait_recv()
        for cp in local:
            cp.wait()

    return start, wait


def _comm_out_shapes(kind, arrays):
    return [jax.ShapeDtypeStruct(((N_DEV,) + s.shape) if kind == "gather" else s.shape, s.dtype) for s in arrays]


def _comm_scratch(n):
    k_tot = (N_DEV - 1) * n
    return [pltpu.SemaphoreType.DMA((k_tot,)), pltpu.SemaphoreType.DMA((k_tot,)), pltpu.SemaphoreType.DMA((n,))]


def _comm_call(name, kind, arrays):
    n = len(arrays)

    def body(*refs):
        start, wait = _comm_plan(kind, refs[:n], refs[n:2 * n], *refs[2 * n:])
        start()
        wait()

    hbm = pl.BlockSpec(memory_space=pl.ANY)
    return pl.pallas_call(
        body, name=name, in_specs=[hbm] * n, out_specs=[hbm] * n, out_shape=_comm_out_shapes(kind, arrays),
        scratch_shapes=_comm_scratch(n), compiler_params=pltpu.CompilerParams(has_side_effects=True),
    )(*arrays)


def _adam(name, parts, w, m, v, tr, layer=None, prev=None):
    p_, r_, cw = parts.shape
    c1 = 1.0 - ADAM_B1 ** ADAM_STEP
    c2 = 1.0 - ADAM_B2 ** ADAM_STEP
    n_prev = 4 if prev else 0

    def body(p_ref, w_ref, m_ref, v_ref, *rest):
        g_ref, d_ref, nm_ref, nv_ref = rest[n_prev:]
        g = p_ref[0].astype(F32)
        for i in range(1, p_):
            g = g + p_ref[i].astype(F32)
        nm = ADAM_B1 * m_ref[...] + (1.0 - ADAM_B1) * g
        nv = ADAM_B2 * v_ref[...] + (1.0 - ADAM_B2) * (g * g)
        d_ref[...] = -ADAM_LR * ((nm / c1) / (jnp.sqrt(nv / c2) + ADAM_EPS) + ADAM_WD * w_ref[...])
        g_ref[...] = g
        nm_ref[...] = nm
        nv_ref[...] = nv

    if layer is None:
        spec = pl.BlockSpec((tr, cw), lambda i: (i, 0))
        shp = jax.ShapeDtypeStruct((r_, cw), F32)
    else:
        spec = pl.BlockSpec((None, tr, cw), lambda i: (layer, i, 0))
        shp = jax.ShapeDtypeStruct((DEPTH, r_, cw), F32)
    return pl.pallas_call(
        body, name=name, grid=(r_ // tr,),
        in_specs=[pl.BlockSpec((p_, tr, cw), lambda i: (0, i, 0)), spec, spec, spec]
                 + [pl.BlockSpec(memory_space=pl.ANY)] * n_prev,
        out_specs=[spec] * 4, out_shape=[shp] * 4,
        input_output_aliases={4 + i: i for i in range(n_prev)},
        compiler_params=_cparams(("parallel",), VMEM_LIMIT),
    )(parts, w, m, v, *(prev or ()))


def _shards_to_perm(g_in):
    parts = []
    for n in _ORDER:
        lo, hi = _ORIG[n][0], _ORIG[n][0] + _ORIG[n][1]
        for j in range(lo // SHARD_IN, (hi - 1) // SHARD_IN + 1):
            a, b = max(lo, j * SHARD_IN), min(hi, (j + 1) * SHARD_IN)
            parts.append(g_in[j][:, a - j * SHARD_IN:b - j * SHARD_IN])
    parts.append(jnp.zeros((g_in.shape[1], N_PAD - N_USED), g_in.dtype))
    return jnp.concatenate(parts, axis=1)


def _perm_to_shards(g):
    by_orig = sorted(_ORDER, key=lambda n: _ORIG[n][0])
    slabs = []
    for j in range(N_DEV):
        lo, hi = j * SHARD_IN, (j + 1) * SHARD_IN
        parts = []
        for n in by_orig:
            a, b = max(lo, _ORIG[n][0]), min(hi, _ORIG[n][0] + _ORIG[n][1])
            if a < b:
                parts.append(g[:, _MINE[n] + a - _ORIG[n][0]:_MINE[n] + b - _ORIG[n][0]])
        slabs.append(jnp.concatenate(parts, axis=1))
    return jnp.stack(slabs, axis=0)


def _ret_consts(length):
    lg = jnp.log(1.0 - 2.0 ** (-5.0 - jnp.arange(RET_HEADS, dtype=F32)))
    idx = jnp.arange(CHUNK, dtype=F32)
    rel = idx[:, None] - idx[None, :]
    dmat = jnp.where(rel[None] >= 0, jnp.exp(jnp.maximum(rel, 0.0)[None] * lg[:, None, None]), 0.0)
    qdec = jnp.exp((idx[:, None] + 1.0) * lg[None, :]).T[:, :, None]
    kdec = jnp.exp((CHUNK - 1.0 - idx)[:, None] * lg[None, :]).T[:, :, None]
    cdec = jnp.exp(CHUNK * lg)[:, None, None]
    half = RET_DK // 2
    inv = ROPE_BASE ** (-jnp.arange(half, dtype=F32) / half)
    ang = jnp.arange(length, dtype=jnp.int32).astype(F32)[:, None] * inv[None, :]
    cos, sin = jnp.cos(ang), jnp.sin(ang)
    return (jnp.concatenate([cos, cos], axis=1), jnp.concatenate([-sin, sin], axis=1), dmat, qdec, kdec, cdec)


def _tiles(length, pref=512):
    return pref if length % pref == 0 else length


def _layer_fwd(x, p, consts, length, last, target, comm=None):
    tm = _tiles(length)
    (h,), _ = _rowmap("pre_norm", lambda x_, w_: ((_rms(x_, w_),), ()), length, tm,
                      [(x, D_MODEL, 0)], [p["pre_norm"]], out_ws=[D_MODEL], out_dtypes=[BF16])
    proj = _matmul("in_proj", h, p["w_in"], "nn", length, N_PAD, D_MODEL)
    xc = _conv_fwd(proj, p["conv_w"], p["conv_b"], length, _tiles(length, 2048), 512)
    mixed, st_a, tinv, *got = _gdn_fwd(xc, proj, p["gdn_A_log"], p["gdn_dt_bias"], p["gdn_norm"], length, comm)
    y_b, mixed, st_b = _ssd_fwd(xc, proj, p["ssd_A_log"], p["ssd_dt_bias"], p["ssd_D"], p["ssd_norm"], mixed, length)
    mixed, st_c = _ret_fwd(proj, *consts, p["ret_norm"], mixed, length)
    w_out = p["w_out"](got) if callable(p["w_out"]) else p["w_out"]
    out = _matmul("out_proj", mixed, w_out, "nn", length, D_MODEL, MIX_W)
    res = dict(x=x, h=h, proj=proj, xc=xc, st_a=st_a, tinv=tinv, st_b=st_b, st_c=st_c, y_b=y_b, mixed=mixed,
               out=out, w_out=w_out)
    if not last:
        (y,), _ = _rowmap("post_norm", lambda x_, o_, w_: ((x_ + _rms(o_, w_),), ()), length, tm,
                          [(x, D_MODEL, 0), (out, D_MODEL, 0)], [p["post_norm"]], out_ws=[D_MODEL])
        return y, res, None, got

    def head(x_, o_, t_, w_):
        normed, post_vjp = jax.vjp(_rms, o_, w_)
        e = x_ + normed - t_
        row = jnp.mean(e * e, axis=-1, keepdims=True)
        loss = 0.5 * jnp.sum(row, axis=0, keepdims=True)
        dy_ = e * (1.0 / D_MODEL)
        d_out, dw = post_vjp(dy_)
        return (dy_, d_out), (loss + jnp.zeros((1, LANE), F32), dw)

    (dy, d_out), (loss, dw_post) = _rowmap(
        "loss_head", head, length, tm, [(x, D_MODEL, 0), (out, D_MODEL, 0), (target, D_MODEL, 0)], [p["post_norm"]],
        out_ws=[D_MODEL, D_MODEL], out_dtypes=[F32, BF16], acc_ws=[LANE, D_MODEL])
    res["post_bwd"] = (d_out, dw_post)
    return None, res, (dy, loss[0, 0]), got


def _layer_bwd(dy, p, res, consts, length, comm_scan=None, comm_dx=None):
    tm = _tiles(length)
    g = {}
    if "post_bwd" in res:
        d_out, g["post_norm"] = res["post_bwd"]
    else:
        (d_out,), (g["post_norm"],) = _rowmap("post_norm_bwd", lambda o_, w_: ((_rms(o_, w_),), ()), length, tm,
                                              [(res["out"], D_MODEL, 0)], [p["post_norm"]], cts=[(dy, D_MODEL, 0)],
                                              d_dtypes=[BF16])
    dmix = _matmul("out_proj_dx", d_out, res["w_out"], "nt", length, MIX_W, D_MODEL)
    g["w_out"] = _matmul("out_proj_dw", res["mixed"], d_out, "tn", MIX_W, D_MODEL, length, out_dtype=BF16)
    comm = comm_scan(g) if comm_scan else None
    proj, xc = res["proj"], res["xc"]
    (dxc, dproj, dsm_a, g["gdn_A_log"], g["gdn_dt_bias"], g["gdn_norm"], *got) = _gdn_bwd(
        xc, proj, p["gdn_A_log"], p["gdn_dt_bias"], p["gdn_norm"], res["st_a"], res["tinv"], dmix, length, comm)
    dxc, dproj, dsm_b, g["ssd_A_log"], g["ssd_dt_bias"], g["ssd_D"], g["ssd_norm"] = _ssd_bwd(
        xc, proj, p["ssd_A_log"], p["ssd_dt_bias"], p["ssd_D"], p["ssd_norm"], res["st_b"], res["y_b"], dmix,
        dproj, dxc, length)
    dproj, g["ret_norm"] = _ret_bwd(proj, *consts, p["ret_norm"], res["st_c"], dmix, dproj, length)
    tmc = _tiles(length, 2048)
    dpre, g["conv_w"], g["conv_b"] = _conv_bwd_pre(proj, p["conv_w"], p["conv_b"], dxc, length, tmc, 512)
    dproj = _conv_bwd_x(dpre, p["conv_w"], dproj, length, tmc, 512)
    pad = SMALL_OFF % 512 + 512 - LANE
    (dproj,), _ = _rowmap("dsmall", lambda a_, b_, c_: ((jnp.concatenate([a_ + b_ + c_, jnp.zeros((a_.shape[0], pad), F32)], axis=1),), ()),
                          length, tm, [(dsm_a, LANE, 0), (dsm_b[0], LANE, 0), (dsm_b[1], LANE, 0)], [],
                          out_ws=[512], out_dtypes=[BF16], place={0: (dproj, N_PAD, SMALL_OFF // 512)})
    g["w_in"] = _matmul("in_proj_dw", res["h"], dproj, "tn", D_MODEL, N_PAD, length, out_dtype=BF16)
    if comm_dx:
        dh, *got_dx = _matmul("in_proj_dx_scatter", dproj, p["w_in"], "nt", length, D_MODEL, N_PAD, comm=comm_dx(g))
    else:
        dh, got_dx = _matmul("in_proj_dx", dproj, p["w_in"], "nt", length, D_MODEL, N_PAD), []
    (dx,), (g["pre_norm"],) = _rowmap(
        "pre_norm_bwd", lambda x_, w_: ((_rms(x_, w_), x_), ()), length, tm,
        [(res["x"], D_MODEL, 0)], [p["pre_norm"]], cts=[(dh, D_MODEL, 0), (dy, D_MODEL, 0)])
    return dx, g, got, got_dx


def _lane_row(vals, piece):
    lo = _MINE[piece] - SMALL_OFF
    return jnp.pad(vals[None], ((0, 0), (lo, LANE - lo - vals.shape[0])))


def _from_lane_row(row, piece, n):
    lo = _MINE[piece] - SMALL_OFF
    return row.reshape(-1, LANE).sum(axis=0)[lo:lo + n]


def _make_layer(l, w_in_perm, w_out_full, conv_full, sw):
    return dict(
        pre_norm=sw["pre_norm"][l][None], post_norm=sw["post_norm"][l][None], w_in=w_in_perm,
        w_out=w_out_full,
        conv_w=jnp.concatenate([conv_full[0], conv_full[1]], axis=1),
        conv_b=jnp.concatenate([jnp.zeros((CONV_CH,), F32), sw["ssd_conv_b"][l]])[None],
        gdn_A_log=_lane_row(sw["gdn_A_log"][l], "ga"), gdn_dt_bias=_lane_row(sw["gdn_dt_bias"][l], "ga"),
        gdn_norm=sw["gdn_norm"][l][None],
        ssd_A_log=_lane_row(sw["ssd_A_log"][l], "sdt"), ssd_dt_bias=_lane_row(sw["ssd_dt_bias"][l], "sdt"),
        ssd_D=_lane_row(sw["ssd_D"][l], "sdt"),
        ssd_norm=sw["ssd_norm"][l].reshape(SSD_GROUPS, 1, -1),
        ret_norm=sw["ret_norm"][l][None])


def _local_step(xb, tgt, layer0, make_layer1, length, fwd_comm=None, comm_scan=None, comm_dx=None):
    consts = _ret_consts(length)
    y0, res0, _, got = _layer_fwd(xb, layer0, consts, length, False, tgt, fwd_comm)
    layer1 = make_layer1(got)
    _, res1, (dy, loss_local), _ = _layer_fwd(y0, layer1, consts, length, True, tgt)
    dy, grads1, _, _ = _layer_bwd(dy, layer1, res1, consts, length)
    dx, grads0, recv_scan, recv_dx = _layer_bwd(
        dy, layer0, res0, consts, length,
        (lambda g0: comm_scan(grads1, g0)) if comm_scan else None, comm_dx)
    return loss_local, dx, [grads0, grads1], recv_scan, recv_dx


_SMALL = ["pre_norm", "post_norm", "gdn_A_log", "gdn_dt_bias", "gdn_norm", "ssd_conv_b", "ssd_A_log",
          "ssd_dt_bias", "ssd_D", "ssd_norm", "ret_norm"]


def _pack_small(arrs):
    rows = []
    for n in _SMALL:
        flat = arrs[n].reshape(-1)
        pad = (-flat.shape[0]) % LANE
        rows.append(jnp.pad(flat, (0, pad)).reshape(-1, LANE))
    out = jnp.concatenate(rows, axis=0)
    return jnp.pad(out, ((0, (-out.shape[0]) % SUBLANE), (0, 0)))


def _unpack_small(packed, like):
    out, r = {}, 0
    for n in _SMALL:
        cnt = like[n].size
        nrow = -(-cnt // LANE)
        out[n] = packed[r:r + nrow].reshape(-1)[:cnt].reshape(like[n].shape)
        r += nrow
    return out


def kernel(x, pre_norm, post_norm, w_in, gdn_conv, gdn_A_log, gdn_dt_bias, gdn_norm, ssd_conv, ssd_conv_b, ssd_A_log, ssd_dt_bias, ssd_D, ssd_norm, ret_norm, w_out, loss_target, m_pre_norm, m_post_norm, m_w_in, m_gdn_conv, m_gdn_A_log, m_gdn_dt_bias, m_gdn_norm, m_ssd_conv, m_ssd_conv_b, m_ssd_A_log, m_ssd_dt_bias, m_ssd_D, m_ssd_norm, m_ret_norm, m_w_out, v_pre_norm, v_post_norm, v_w_in, v_gdn_conv, v_gdn_A_log, v_gdn_dt_bias, v_gdn_norm, v_ssd_conv, v_ssd_conv_b, v_ssd_A_log, v_ssd_dt_bias, v_ssd_D, v_ssd_norm, v_ret_norm, v_w_out):
    length = x.shape[1]
    xb = x[0]
    tgt = loss_target[0]
    small_w = dict(pre_norm=pre_norm, post_norm=post_norm, gdn_A_log=gdn_A_log, gdn_dt_bias=gdn_dt_bias,
                   gdn_norm=gdn_norm, ssd_conv_b=ssd_conv_b, ssd_A_log=ssd_A_log, ssd_dt_bias=ssd_dt_bias,
                   ssd_D=ssd_D, ssd_norm=ssd_norm, ret_norm=ret_norm)
    small_m = dict(pre_norm=m_pre_norm, post_norm=m_post_norm, gdn_A_log=m_gdn_A_log, gdn_dt_bias=m_gdn_dt_bias,
                   gdn_norm=m_gdn_norm, ssd_conv_b=m_ssd_conv_b, ssd_A_log=m_ssd_A_log, ssd_dt_bias=m_ssd_dt_bias,
                   ssd_D=m_ssd_D, ssd_norm=m_ssd_norm, ret_norm=m_ret_norm)
    small_v = dict(pre_norm=v_pre_norm, post_norm=v_post_norm, gdn_A_log=v_gdn_A_log, gdn_dt_bias=v_gdn_dt_bias,
                   gdn_norm=v_gdn_norm, ssd_conv_b=v_ssd_conv_b, ssd_A_log=v_ssd_A_log, ssd_dt_bias=v_ssd_dt_bias,
                   ssd_D=v_ssd_D, ssd_norm=v_ssd_norm, ret_norm=v_ret_norm)

    w_in_b, w_out_b = w_in.astype(BF16), w_out.astype(BF16)
    conv_shard = jnp.stack([gdn_conv, ssd_conv], axis=1)
    full_out = lambda g_out: g_out.reshape(MIX_W, D_MODEL)

    def assemble(l, g_in, w_out_full, g_conv):
        conv_full = g_conv.transpose(1, 2, 0, 3).reshape(2, CONV_W, CONV_CH)
        return _make_layer(l, _shards_to_perm(g_in), w_out_full, conv_full, small_w)

    slab_in = lambda g: _perm_to_shards(g["w_in"]).astype(BF16)
    slab_out = lambda g: g["w_out"].reshape(N_DEV, SHARD_OUT, D_MODEL).astype(BF16)
    slab_conv = lambda g: jnp.stack(
        [g["conv_w"][:, k * CONV_CH:(k + 1) * CONV_CH].reshape(CONV_W, N_DEV, SHARD_CONV).transpose(1, 0, 2)
         for k in range(2)], axis=1).reshape(N_DEV, 2 * CONV_W, SHARD_CONV)

    g_in0, g_conv0 = _comm_call("gather_layer0", "gather", [w_in_b[0], conv_shard[0]])
    layer0 = assemble(0, g_in0, lambda got: full_out(got[3]), g_conv0)
    loss_local, dx, grads, recv_scan, recv_dx = _local_step(
        xb, tgt, layer0, lambda got: assemble(1, got[0], full_out(got[1]), got[2]), length,
        fwd_comm=("gather", [w_in_b[1], w_out_b[1], conv_shard[1], w_out_b[0]]),
        comm_scan=lambda g1, g0: ("scatter", [slab_in(g1), slab_out(g1), slab_conv(g1), slab_out(g0)]),
        comm_dx=lambda g0: ("scatter", [slab_in(g0), slab_conv(g0)]))
    grad_x = dx[None]
    loss = lax.psum(loss_local, ("x", "y", "c"))

    small_g = dict(
        pre_norm=jnp.concatenate([grads[l]["pre_norm"] for l in range(DEPTH)], axis=0),
        post_norm=jnp.concatenate([grads[l]["post_norm"] for l in range(DEPTH)], axis=0),
        gdn_A_log=jnp.stack([_from_lane_row(grads[l]["gdn_A_log"], "ga", GDN_HEADS) for l in range(DEPTH)]),
        gdn_dt_bias=jnp.stack([_from_lane_row(grads[l]["gdn_dt_bias"], "ga", GDN_HEADS) for l in range(DEPTH)]),
        gdn_norm=jnp.concatenate([grads[l]["gdn_norm"] for l in range(DEPTH)], axis=0),
        ssd_conv_b=jnp.concatenate([grads[l]["conv_b"][:, CONV_CH:] for l in range(DEPTH)], axis=0),
        ssd_A_log=jnp.stack([_from_lane_row(grads[l]["ssd_A_log"], "sdt", SSD_HEADS) for l in range(DEPTH)]),
        ssd_dt_bias=jnp.stack([_from_lane_row(grads[l]["ssd_dt_bias"], "sdt", SSD_HEADS) for l in range(DEPTH)]),
        ssd_D=jnp.stack([_from_lane_row(grads[l]["ssd_D"], "sdt", SSD_HEADS) for l in range(DEPTH)]),
        ssd_norm=jnp.concatenate([grads[l]["ssd_norm"].reshape(1, -1) for l in range(DEPTH)], axis=0),
        ret_norm=jnp.concatenate([grads[l]["ret_norm"] for l in range(DEPTH)], axis=0))
    gs = _pack_small(small_g)
    gs8 = jnp.broadcast_to(gs[None], (N_DEV,) + gs.shape)
    (r_small,) = _comm_call("exchange_small", "scatter", [gs8])
    recv = [[recv_dx[0], recv_scan[3], recv_dx[1]], recv_scan[:3]]

    conv_w = lambda g_, s_, l: jnp.stack([g_[l], s_[l]], axis=0).reshape(2 * CONV_W, SHARD_CONV)
    o_in, o_out, o_conv = None, None, []
    for l in range(DEPTH):
        o_in = _adam(f"adam_w_in{l}", recv[l][0], w_in, m_w_in, v_w_in, 128, l, o_in)
        o_out = _adam(f"adam_w_out{l}", recv[l][1], w_out, m_w_out, v_w_out, 128, l, o_out)
        o_conv.append(_adam(f"adam_conv{l}", recv[l][2], conv_w(gdn_conv, ssd_conv, l),
                            conv_w(m_gdn_conv, m_ssd_conv, l), conv_w(v_gdn_conv, v_ssd_conv, l), 2 * CONV_W))
    ps_w, ps_m, ps_v = _pack_small(small_w), _pack_small(small_m), _pack_small(small_v)
    o_small = _adam("adam_small", r_small, ps_w, ps_m, ps_v, ps_w.shape[0])

    names = ["pre_norm", "post_norm", "w_in", "gdn_conv", "gdn_A_log", "gdn_dt_bias", "gdn_norm", "ssd_conv",
             "ssd_conv_b", "ssd_A_log", "ssd_dt_bias", "ssd_D", "ssd_norm", "ret_norm", "w_out"]
    outs = []
    for kind in range(4):
        d = _unpack_small(o_small[kind], small_w)
        cv = jnp.stack([o_conv[l][kind].reshape(2, CONV_W, SHARD_CONV) for l in range(DEPTH)], axis=1)
        d["w_in"] = o_in[kind]
        d["w_out"] = o_out[kind]
        d["gdn_conv"] = cv[0]
        d["ssd_conv"] = cv[1]
        outs.extend(d[n] for n in names)
    return (loss, grad_x, *outs)
```

```python
import functools
import math

import jax
import jax.numpy as jnp
from jax import lax
from jax.experimental import pallas as pl
from jax.experimental.pallas import tpu as pltpu

F32 = jnp.float32
BF16 = jnp.bfloat16

D_MODEL = 1024
DEPTH = 2
CHUNK = 64
CONV_W = 4
EPS = 1e-6
N_DEV = 8

GDN_HEADS = 4
GDN_DK = 128
SSD_HEADS = 16
SSD_P = 64
SSD_N = 128
SSD_GROUPS = 2
SSD_PAIRS = SSD_HEADS // 2
PAIRS_PER_GROUP = SSD_PAIRS // SSD_GROUPS
RET_HEADS = 4
RET_DK = 128
ROPE_BASE = 10000.0
MIX_W = 2048
N_IN = 6680
SHARD_IN = N_IN // N_DEV
SHARD_OUT = MIX_W // N_DEV
CONV_CH = 1536
SHARD_CONV = CONV_CH // N_DEV

ADAM_LR = 0.001
ADAM_B1 = 0.9
ADAM_B2 = 0.999
ADAM_EPS = 1e-08
ADAM_WD = 0.01
ADAM_STEP = 10

LANE = 128
SUBLANE = 8
VMEM_LIMIT = 56 * 1024 * 1024

_ORIG = dict(gq=(0, 512), gk=(512, 512), gv=(1024, 512), gz=(1536, 512), gb=(2048, 4), ga=(2052, 4),
             sx=(2056, 1024), sB=(3080, 256), sC=(3336, 256), sz=(3592, 1024), sdt=(4616, 16),
             rq=(4632, 512), rk=(5144, 512), rv=(5656, 512), rg=(6168, 512))
_ORDER = ["rq", "rk", "rv", "rg", "gq", "gk", "gv", "sx", "sB", "sC", "gz", "sz", "gb", "ga", "sdt"]
_MINE = {}
_off = 0
for _n in _ORDER:
    _MINE[_n] = _off
    _off += _ORIG[_n][1]
N_USED = _off
N_PAD = 7168
CONV_ALL = 2 * CONV_CH
SMALL_OFF = _MINE["gb"]
CONV_OFF = _MINE["gq"]
XC = lambda name: _MINE[name] - CONV_OFF


def _cparams(sem, vmem=None):
    return pltpu.CompilerParams(dimension_semantics=sem, vmem_limit_bytes=vmem)


def _split_bf16(a):
    hi = a.astype(BF16)
    return hi, (a - hi.astype(F32)).astype(BF16)


def _make_mm():
    def raw(a, b, ca, cb):
        return lax.dot_general(a.astype(BF16), b.astype(BF16), (((ca,), (cb,)), ((), ())),
                               preferred_element_type=F32)

    @jax.custom_vjp
    def nn(a, b):
        return raw(a, b, 1, 0)

    @jax.custom_vjp
    def nt(a, b):
        return raw(a, b, 1, 1)

    @jax.custom_vjp
    def tn(a, b):
        return raw(a, b, 0, 0)

    nn.defvjp(lambda a, b: (raw(a, b, 1, 0), (a, b)), lambda r, g: (nt(g, r[1]), tn(r[0], g)))
    nt.defvjp(lambda a, b: (raw(a, b, 1, 1), (a, b)), lambda r, g: (nn(g, r[1]), tn(g, r[0])))
    tn.defvjp(lambda a, b: (raw(a, b, 0, 0), (a, b)), lambda r, g: (nt(r[1], g), nn(r[0], g)))
    return nn, nt, tn


_nn, _nt, _tn = _make_mm()


@jax.custom_vjp
def _swap_halves(t):
    return pltpu.roll(t, LANE // 2, 1)


_swap_halves.defvjp(lambda t: (pltpu.roll(t, LANE // 2, 1), None),
                    lambda _, g: (pltpu.roll(g, LANE // 2, 1),))


@jax.custom_vjp
def _split_rows(x):
    return tuple(x[i * CHUNK:(i + 1) * CHUNK] for i in range(x.shape[0] // CHUNK))


_split_rows.defvjp(lambda x: (tuple(x[i * CHUNK:(i + 1) * CHUNK] for i in range(x.shape[0] // CHUNK)), None),
                   lambda _, gs: (jnp.concatenate(gs, axis=0),))


def _dot3(a, b, ca, cb):
    dot = lambda x, y: lax.dot_general(x, y, (((ca,), (cb,)), ((), ())), preferred_element_type=F32)
    return dot(a[0], b[0]) + (dot(a[0], b[1]) + dot(a[1], b[0]))


@jax.custom_vjp
def _tri_inv(mats):
    return _tri_inv_impl(mats)


def _tri_inv_impl(mats):
    ii = lax.broadcasted_iota(jnp.int32, mats[0].shape, 0)
    jj = lax.broadcasted_iota(jnp.int32, mats[0].shape, 1)
    eye = jnp.where(ii == jj, 1.0, 0.0).astype(F32)
    ts = [eye - a for a in mats]
    ps = [_split_bf16(-a) for a in mats]
    for _ in range(int(math.log2(CHUNK)) - 1):
        ps = [_split_bf16(_dot3(p, p, 1, 0)) for p in ps]
        ts = [t + _dot3(_split_bf16(t), p, 1, 0) for t, p in zip(ts, ps)]
    return ts


def _tri_inv_bwd(ts, gs):
    tsp = [_split_bf16(t) for t in ts]
    xs = [_dot3(t, _split_bf16(g), 0, 0) for t, g in zip(tsp, gs)]
    return ([-_dot3(_split_bf16(x), t, 1, 1) for x, t in zip(xs, tsp)],)


_tri_inv.defvjp(lambda mats: (lambda ts: (ts, ts))(_tri_inv_impl(mats)), _tri_inv_bwd)


@jax.custom_vjp
def _tri_inv_saved(mats, ts):
    return ts


_tri_inv_saved.defvjp(lambda mats, ts: (ts, ts),
                      lambda ts, gs: (_tri_inv_bwd(ts, gs)[0], [jnp.zeros_like(t) for t in ts]))


def _silu(x):
    return x * jax.nn.sigmoid(x)


@jax.custom_vjp
def _softplus(x):
    return jnp.maximum(x, 0.0) + jnp.log1p(jnp.exp(-jnp.abs(x)))


_softplus.defvjp(lambda x: (jnp.maximum(x, 0.0) + jnp.log1p(jnp.exp(-jnp.abs(x))), x),
                 lambda x, g: (g * jax.nn.sigmoid(x),))


def _rms(x, w):
    return x * lax.rsqrt(jnp.mean(x * x, axis=-1, keepdims=True) + EPS) * w


def _chunk_masks(n):
    ii = lax.broadcasted_iota(jnp.int32, (n, n), 0)
    jj = lax.broadcasted_iota(jnp.int32, (n, n), 1)
    return ii >= jj, ii > jj, ii == jj, ii <= jj


def _cumsum_col(g, causal, eye, upper):
    g_row = jnp.sum(jnp.where(eye, g, 0.0), axis=0, keepdims=True)
    col = jnp.sum(jnp.where(causal, g_row, 0.0), axis=1, keepdims=True)
    row = jnp.sum(jnp.where(upper, g, 0.0), axis=0, keepdims=True)
    return col, row


def _lane_col(block, lane):
    pick = lax.broadcasted_iota(jnp.int32, (1, block.shape[1]), 1) == lane
    return jnp.sum(jnp.where(pick, block, 0.0), axis=1, keepdims=True)


def _gdn_chunk(q, k, v, gz, sm, alog_row, dtb_row, nw, s, t_saved=None):
    subs, nh = range(len(q)), len(q[0])
    c = q[0][0].shape[0]
    causal, _, eye, upper = _chunk_masks(c)
    stack = lambda xs: jnp.concatenate(xs, axis=0)
    per = lambda f: [[f(j, h) for h in range(nh)] for j in subs]
    qn = per(lambda j, h: q[j][h] * lax.rsqrt(jnp.sum(q[j][h] * q[j][h], axis=-1, keepdims=True) + EPS)
             * (GDN_DK ** -0.5))
    kn = per(lambda j, h: k[j][h] * lax.rsqrt(jnp.sum(k[j][h] * k[j][h], axis=-1, keepdims=True) + EPS))
    beta_blk = [jax.nn.sigmoid(sm[j]) for j in subs]
    g_blk = [-jnp.exp(alog_row) * _softplus(sm[j] + dtb_row) for j in subs]
    beta = per(lambda j, h: _lane_col(beta_blk[j], _MINE["gb"] - SMALL_OFF + h))
    g = per(lambda j, h: _lane_col(g_blk[j], _MINE["ga"] - SMALL_OFF + h))
    gcum = per(lambda j, h: _cumsum_col(g[j][h], causal, eye, upper)[0])
    eg = per(lambda j, h: jnp.exp(gcum[j][h]))
    glast = per(lambda j, h: jnp.sum(g[j][h], axis=0, keepdims=True))
    kb = per(lambda j, h: kn[j][h] * beta[j][h])
    n = nh * c
    ii = lax.broadcasted_iota(jnp.int32, (n, n), 0)
    jj = lax.broadcasted_iota(jnp.int32, (n, n), 1)
    sh = int(math.log2(c))
    same = lax.shift_right_logical(ii, sh) == lax.shift_right_logical(jj, sh)
    causal_bd = jnp.logical_and(same, ii >= jj)
    strict_bd = jnp.logical_and(same, ii > jj)
    gcum_all = [stack(gcum[j]) for j in subs]
    gcum_row = [jnp.sum(jnp.where(ii == jj, gcum_all[j], 0.0), axis=0, keepdims=True) for j in subs]
    decay = [jnp.where(causal_bd, jnp.exp(jnp.where(causal_bd, gcum_all[j] - gcum_row[j], 0.0)), 0.0) for j in subs]
    kn_all = [stack(kn[j]) for j in subs]
    a_low = [jnp.where(strict_bd, _nt(stack(kb[j]), kn_all[j]) * decay[j], 0.0) for j in subs]
    t = _tri_inv(a_low) if t_saved is None else _tri_inv_saved(a_low, t_saved)
    u = [_nn(t[j], stack([v[j][h] * beta[j][h] for h in range(nh)])) for j in subs]
    w = [_split_rows(_nn(t[j], stack([kb[j][h] * eg[j][h] for h in range(nh)]))) for j in subs]
    attn = [_nt(stack(qn[j]), kn_all[j]) * decay[j] for j in subs]
    kdec = per(lambda j, h: kn[j][h] * jnp.exp(glast[j][h] - gcum[j][h]))
    on = []
    for j in subs:
        v_new_all = u[j] - stack([_nn(w[j][h], s[h]) for h in range(nh)])
        o = _split_rows(stack([_nn(qn[j][h] * eg[j][h], s[h]) for h in range(nh)]) + _nn(attn[j], v_new_all))
        v_new = _split_rows(v_new_all)
        s = [s[h] * jnp.exp(glast[j][h]) + _tn(kdec[j][h], v_new[h]) for h in range(nh)]
        on.append([_rms(o[h], nw) * _silu(gz[j][h]) for h in range(nh)])
    return on, s, t


def _ssd_chunk(lanes, x, bm, cm, sm, alog_row, dtb_row, d_row, hs):
    subs, pairs = range(len(x)), range(len(x[0]))
    c = x[0][0].shape[0]
    lane_i = lax.broadcasted_iota(jnp.int32, (c, LANE), 1)
    lane_lo = lane_i < SSD_P
    lane_lo1 = lax.broadcasted_iota(jnp.int32, (1, LANE), 1) < SSD_P
    row_i = lax.broadcasted_iota(jnp.int32, (c, LANE), 0)
    causal2 = row_i >= jnp.bitwise_and(lane_i, c - 1)
    diag2 = row_i == jnp.bitwise_and(lane_i, c - 1)
    to_row = lambda blk: jnp.sum(jnp.where(diag2, blk, 0.0), axis=0, keepdims=True)
    per = lambda f: [[f(j, p) for p in pairs] for j in subs]
    both = lambda blk, p: [_lane_col(blk, lanes[p][h]) for h in range(2)]
    dt_blk = [_softplus(sm[j] + dtb_row) for j in subs]
    a_blk = [dt_blk[j] * -jnp.exp(alog_row) for j in subs]
    alast_blk = [jnp.sum(a_blk[j], axis=0, keepdims=True) for j in subs]
    dt = per(lambda j, p: both(dt_blk[j], p))
    nexp = [both(-jnp.exp(alog_row), p) for p in pairs]
    alast = per(lambda j, p: both(alast_blk[j], p))
    dp = [both(d_row, p) for p in pairs]
    dt_lane = per(lambda j, p: jnp.where(lane_lo, dt[j][p][0], dt[j][p][1]))
    a_lane = per(lambda j, p: dt_lane[j][p] * jnp.where(lane_lo1, nexp[p][0], nexp[p][1]))
    a_row = per(lambda j, p: to_row(a_lane[j][p]))
    part = per(lambda j, p: jnp.where(causal2, a_row[j][p], 0.0))
    acum = per(lambda j, p: [jnp.sum(jnp.where(lane_lo, part[j][p], 0.0), axis=1, keepdims=True),
                             jnp.sum(jnp.where(lane_lo, 0.0, part[j][p]), axis=1, keepdims=True)])
    acum_col = per(lambda j, p: jnp.where(lane_lo, acum[j][p][0], acum[j][p][1]))
    acum_row = per(lambda j, p: to_row(acum_col[j][p]))
    lmat = per(lambda j, p: jnp.where(causal2, jnp.exp(jnp.where(causal2, acum_col[j][p] - acum_row[j][p], 0.0)), 0.0))
    cb = [_nt(cm[j], jnp.concatenate([bm[j], bm[j]], axis=0)) for j in subs]
    xdt = per(lambda j, p: x[j][p] * dt_lane[j][p])
    xdt_rows = per(lambda j, p: jnp.concatenate([jnp.where(lane_lo, xdt[j][p], 0.0),
                                                 jnp.where(lane_lo, 0.0, xdt[j][p])], axis=0))
    intra = per(lambda j, p: _nn(cb[j] * lmat[j][p], xdt_rows[j][p]))
    skip = per(lambda j, p: x[j][p] * jnp.where(lane_lo1, dp[p][0], dp[p][1]))
    eacc = per(lambda j, p: jnp.exp(acum_col[j][p]))
    alast_row = per(lambda j, p: jnp.where(lane_lo1, alast[j][p][0], alast[j][p][1]))
    wdec = per(lambda j, p: jnp.exp(alast_row[j][p] - acum_col[j][p]))
    scale = per(lambda j, p: jnp.exp(alast_row[j][p]))
    upd = per(lambda j, p: _tn(bm[j], xdt[j][p] * wdec[j][p]))
    y = []
    for j in subs:
        y.append([skip[j][p] + intra[j][p] + _nn(cm[j], hs[p]) * eacc[j][p] for p in pairs])
        hs = [hs[p] * scale[j][p] + upd[j][p] for p in pairs]
    return y, hs


def _ret_chunk(rq, rk, rv, rg, cos2, sin2, dmat, qdec, kdec, cdec, nw, r):
    subs, hs = range(len(rq)), range(len(rq[0]))
    per = lambda f: [[f(j, h) for h in hs] for j in subs]
    q = per(lambda j, h: rq[j][h] * cos2[j] + _swap_halves(rq[j][h]) * sin2[j])
    k = per(lambda j, h: (rk[j][h] * cos2[j] + _swap_halves(rk[j][h]) * sin2[j]) * (RET_DK ** -0.5))
    s = per(lambda j, h: _nt(q[j][h], k[j][h]) * dmat[h])
    intra = per(lambda j, h: _nn(s[j][h], rv[j][h]))
    upd = per(lambda j, h: _tn(k[j][h] * kdec[h], rv[j][h]))
    gate = per(lambda j, h: _silu(rg[j][h]))
    on = []
    for j in subs:
        on.append([_rms(intra[j][h] + _nn(q[j][h], r[h]) * qdec[h], nw) * gate[j][h] for h in hs])
        r = [r[h] * cdec[h] + upd[j][h] for h in hs]
    return on, r


MM_TILE = 1024


def _tile(n, pref=MM_TILE):
    return pref if n % pref == 0 else n


def _matmul(name, a, b, mode, m, n, k, tk=None, comm=None, out_dtype=F32, tm=None):
    tm, tn = _tile(m) if tm is None else tm, _tile(n)
    tk = _tile(k) if tk is None else tk
    nk = k // tk
    ca, cb = {"nn": (1, 0), "nt": (1, 1), "tn": (0, 0)}[mode]
    grid = (m // tm, n // tn, nk)
    n_comm = len(comm[1]) if comm else 0
    use_acc = out_dtype != F32 and nk > 1

    def at_step(which):
        return functools.reduce(jnp.logical_and, [pl.program_id(d) == (0 if which == 0 else grid[d] - 1)
                                                  for d in range(3)])

    def body(*refs):
        a_ref, b_ref = refs[:2]
        c_in = refs[2:2 + n_comm]
        o_ref = refs[2 + n_comm]
        c_out = refs[3 + n_comm:3 + 2 * n_comm]
        acc_ref = refs[3 + 2 * n_comm] if use_acc else o_ref
        sems = refs[3 + 2 * n_comm + use_acc:]
        if comm:
            @pl.when(at_step(0))
            def _():
                _comm_plan(comm[0], c_in, c_out, *sems)[0]()

        part = lax.dot_general(a_ref[...].astype(BF16), b_ref[...].astype(BF16),
                               (((ca,), (cb,)), ((), ())), preferred_element_type=F32)
        if nk == 1:
            o_ref[...] = part.astype(o_ref.dtype)
        else:
            kk = pl.program_id(2)

            @pl.when(kk == 0)
            def _():
                acc_ref[...] = part

            @pl.when(kk > 0)
            def _():
                acc_ref[...] += part

            if use_acc:
                @pl.when(kk == nk - 1)
                def _():
                    o_ref[...] = acc_ref[...].astype(o_ref.dtype)

        if comm:
            @pl.when(at_step(1))
            def _():
                _comm_plan(comm[0], c_in, c_out, *sems)[1]()

    a_spec = pl.BlockSpec((tk, tm), lambda i, j, kk: (kk, i)) if mode == "tn" else pl.BlockSpec((tm, tk), lambda i, j, kk: (i, kk))
    b_spec = pl.BlockSpec((tn, tk), lambda i, j, kk: (j, kk)) if mode == "nt" else pl.BlockSpec((tk, tn), lambda i, j, kk: (kk, j))
    hbm = pl.BlockSpec(memory_space=pl.ANY)
    res = pl.pallas_call(
        body, name=name, grid=grid,
        in_specs=[a_spec, b_spec] + [hbm] * n_comm,
        out_specs=[pl.BlockSpec((tm, tn), lambda i, j, kk: (i, j))] + [hbm] * n_comm,
        out_shape=[jax.ShapeDtypeStruct((m, n), out_dtype)] + (_comm_out_shapes(*comm) if comm else []),
        scratch_shapes=([pltpu.VMEM((tm, tn), F32)] if use_acc else []) + (_comm_scratch(n_comm) if comm else []),
        compiler_params=_cparams(("arbitrary",) * 3 if comm else ("parallel", "parallel", "arbitrary"), VMEM_LIMIT),
    )(a, b, *(comm[1] if comm else []))
    return res if comm else res[0]


def _rowmap(name, fn, length, tm, rows, params, out_ws=(), acc_ws=(), out_dtypes=None, cts=None, d_dtypes=None,
            place=None):
    nt_ = length // tm
    n_r, n_p = len(rows), len(params)
    n_in = n_r + n_p
    rspec = lambda bw, cbk: pl.BlockSpec((tm, bw), lambda i: (i, cbk))
    pspec = lambda w: pl.BlockSpec((1, w), lambda i: (0, 0))
    in_arrays = [r[0] for r in rows] + list(params)
    in_specs = [rspec(r[1], r[2]) for r in rows] + [pspec(p.shape[1]) for p in params]
    place = place or {}

    def load(refs):
        return [r[...].astype(F32) for r in refs]

    def placed(widths, dtypes, n_before):
        specs, shapes, extra, alias = [], [], [], {}
        for k, (w, dt) in enumerate(zip(widths, dtypes)):
            arr, total, cbk = place.get(k, (None, w, 0))
            specs.append(rspec(w, cbk))
            shapes.append(jax.ShapeDtypeStruct((length, total), dt))
            if arr is not None:
                alias[n_before + len(extra)] = k
                extra.append(arr)
        return specs, shapes, extra, alias

    def accumulate(a_refs, vals):
        first = pl.program_id(0) == 0
        for a_ref, v in zip(a_refs, vals):
            @pl.when(first)
            def _(a_ref=a_ref):
                a_ref[...] = jnp.zeros_like(a_ref)
            a_ref[...] += v

    hbm = pl.BlockSpec(memory_space=pl.ANY)
    if cts is None:
        n_o = len(out_ws)
        o_specs, o_shapes, extra, alias = placed(out_ws, out_dtypes or [F32] * n_o, n_in)

        def body(*refs):
            outs, accs = fn(*load(refs[:n_in]))
            o_refs = refs[n_in + len(extra):]
            for o_ref, o in zip(o_refs[:n_o], outs):
                o_ref[...] = o.astype(o_ref.dtype)
            accumulate(o_refs[n_o:], accs)

        res = pl.pallas_call(
            body, name=name, grid=(nt_,), in_specs=in_specs + [hbm] * len(extra),
            out_specs=o_specs + [pspec(w) for w in acc_ws],
            out_shape=o_shapes + [jax.ShapeDtypeStruct((1, w), F32) for w in acc_ws],
            input_output_aliases=alias, compiler_params=_cparams(("arbitrary",), VMEM_LIMIT),
        )(*in_arrays, *extra)
        return res[:n_o], res[n_o:]

    n_c = len(cts)
    o_specs, o_shapes, extra, alias = placed([r[1] for r in rows], d_dtypes or [F32] * n_r, n_in + n_c)

    def body(*refs):
        ins = load(refs[:n_in])
        _, vjp = jax.vjp(lambda rs, ps: fn(*rs, *ps)[0], ins[:n_r], ins[n_r:])
        d_rows, d_params = vjp(tuple(load(refs[n_in:n_in + n_c])))
        o_refs = refs[n_in + n_c + len(extra):]
        for o_ref, d in zip(o_refs[:n_r], d_rows):
            o_ref[...] = d.astype(o_ref.dtype)
        accumulate(o_refs[n_r:], d_params)

    res = pl.pallas_call(
        body, name=name, grid=(nt_,),
        in_specs=in_specs + [rspec(c[1], c[2]) for c in cts] + [hbm] * len(extra),
        out_specs=o_specs + [pspec(p.shape[1]) for p in params],
        out_shape=o_shapes + [jax.ShapeDtypeStruct((1, p.shape[1]), F32) for p in params],
        input_output_aliases=alias, compiler_params=_cparams(("arbitrary",), VMEM_LIMIT),
    )(*in_arrays, *[c[0] for c in cts], *extra)
    return res[:n_r], res[n_r:]


def _conv_shift(ext, k, tm, forward):
    s = CONV_W - 1 - k
    if forward:
        rolled = ext if s == 0 else pltpu.roll(ext, s, 0)
        return rolled[SUBLANE:, :]
    rolled = ext if s == 0 else pltpu.roll(ext, tm + SUBLANE - s, 0)
    return rolled[:tm, :]


def _conv_pre(x_ref, halo_ref, w_ref, b_ref, first, tm):
    halo = jnp.where(first, 0.0, halo_ref[...])
    ext = jnp.concatenate([halo, x_ref[...]], axis=0)
    w = w_ref[...]
    pre = b_ref[...] + jnp.zeros_like(x_ref[...])
    taps = []
    for k in range(CONV_W):
        tap = _conv_shift(ext, k, tm, True)
        taps.append(tap)
        pre = pre + tap * w[k:k + 1, :]
    return pre, taps


def _conv_fwd(proj, w, b, length, tm, tc):
    hb = tm // SUBLANE
    j0 = CONV_OFF // tc

    def body(x_ref, halo_ref, w_ref, b_ref, o_ref):
        pre, _ = _conv_pre(x_ref, halo_ref, w_ref, b_ref, pl.program_id(0) == 0, tm)
        o_ref[...] = _silu(pre)

    return pl.pallas_call(
        body, name="conv_fwd", grid=(length // tm, CONV_ALL // tc),
        in_specs=[pl.BlockSpec((tm, tc), lambda i, j: (i, j0 + j)),
                  pl.BlockSpec((SUBLANE, tc), lambda i, j: (jnp.maximum(i * hb - 1, 0), j0 + j)),
                  pl.BlockSpec((CONV_W, tc), lambda i, j: (0, j)),
                  pl.BlockSpec((1, tc), lambda i, j: (0, j))],
        out_specs=pl.BlockSpec((tm, tc), lambda i, j: (i, j)),
        out_shape=jax.ShapeDtypeStruct((length, CONV_ALL), F32),
        compiler_params=_cparams(("parallel", "parallel"), VMEM_LIMIT),
    )(proj, proj, w, b)


def _conv_bwd_pre(proj, w, b, dxc, length, tm, tc):
    hb = tm // SUBLANE
    j0 = CONV_OFF // tc

    def body(x_ref, halo_ref, w_ref, b_ref, dy_ref, dpre_ref, dw_ref, db_ref):
        i = pl.program_id(1)
        pre, taps = _conv_pre(x_ref, halo_ref, w_ref, b_ref, i == 0, tm)
        sg = jax.nn.sigmoid(pre)
        dpre = dy_ref[...] * (sg * (1.0 + pre * (1.0 - sg)))
        dpre_ref[...] = dpre

        @pl.when(i == 0)
        def _():
            dw_ref[...] = jnp.zeros_like(dw_ref)
            db_ref[...] = jnp.zeros_like(db_ref)

        for k in range(CONV_W):
            dw_ref[k:k + 1, :] += jnp.sum(dpre * taps[k], axis=0, keepdims=True)
        db_ref[...] += jnp.sum(dpre, axis=0, keepdims=True)

    return pl.pallas_call(
        body, name="conv_bwd_pre", grid=(CONV_ALL // tc, length // tm),
        in_specs=[pl.BlockSpec((tm, tc), lambda j, i: (i, j0 + j)),
                  pl.BlockSpec((SUBLANE, tc), lambda j, i: (jnp.maximum(i * hb - 1, 0), j0 + j)),
                  pl.BlockSpec((CONV_W, tc), lambda j, i: (0, j)),
                  pl.BlockSpec((1, tc), lambda j, i: (0, j)),
                  pl.BlockSpec((tm, tc), lambda j, i: (i, j))],
        out_specs=[pl.BlockSpec((tm, tc), lambda j, i: (i, j)),
                   pl.BlockSpec((CONV_W, tc), lambda j, i: (0, j)),
                   pl.BlockSpec((1, tc), lambda j, i: (0, j))],
        out_shape=[jax.ShapeDtypeStruct((length, CONV_ALL), F32),
                   jax.ShapeDtypeStruct((CONV_W, CONV_ALL), F32),
                   jax.ShapeDtypeStruct((1, CONV_ALL), F32)],
        compiler_params=_cparams(("parallel", "arbitrary"), VMEM_LIMIT),
    )(proj, proj, w, b, dxc)


def _conv_bwd_x(dpre, w, dproj, length, tm, tc):
    hb = tm // SUBLANE
    n_t = length // tm
    last_blk = length // SUBLANE - 1
    j0 = CONV_OFF // tc

    def body(d_ref, halo_ref, w_ref, _, o_ref):
        halo = jnp.where(pl.program_id(0) == n_t - 1, 0.0, halo_ref[...])
        ext = jnp.concatenate([d_ref[...], halo], axis=0)
        w = w_ref[...]
        acc = jnp.zeros_like(d_ref[...])
        for k in range(CONV_W):
            acc = acc + _conv_shift(ext, k, tm, False) * w[k:k + 1, :]
        o_ref[...] = acc.astype(o_ref.dtype)

    return pl.pallas_call(
        body, name="conv_bwd_x", grid=(n_t, CONV_ALL // tc),
        in_specs=[pl.BlockSpec((tm, tc), lambda i, j: (i, j)),
                  pl.BlockSpec((SUBLANE, tc), lambda i, j: (jnp.minimum((i + 1) * hb, last_blk), j)),
                  pl.BlockSpec((CONV_W, tc), lambda i, j: (0, j)),
                  pl.BlockSpec(memory_space=pl.ANY)],
        out_specs=pl.BlockSpec((tm, tc), lambda i, j: (i, j0 + j)),
        out_shape=jax.ShapeDtypeStruct(dproj.shape, dproj.dtype), input_output_aliases={3: 0},
        compiler_params=_cparams(("parallel", "parallel"), VMEM_LIMIT),
    )(dpre, dpre, w, dproj)


HEAD_SCAN_CHUNKS_PER_STEP = 8
assert GDN_HEADS == RET_HEADS


def _scan_steps(length):
    nc = length // CHUNK
    gs = HEAD_SCAN_CHUNKS_PER_STEP if nc % HEAD_SCAN_CHUNKS_PER_STEP == 0 else 1
    return gs, nc // gs, gs * CHUNK


def _head_tiles(ref, gs):
    return [[ref[j * CHUNK:(j + 1) * CHUNK, h * LANE:(h + 1) * LANE] for h in range(GDN_HEADS)] for j in range(gs)]


def _chunk_rows(ref, gs):
    return [ref[j * CHUNK:(j + 1) * CHUNK, :] for j in range(gs)]


def _gdn_fwd(xc, proj, alog, dtb, nw, length, comm=None):
    gs, steps, rows = _scan_steps(length)
    h_ = GDN_HEADS
    hw = h_ * LANE
    tn_ = h_ * CHUNK
    n_comm = len(comm[1]) if comm else 0

    def body(*refs):
        q_ref, k_ref, v_ref, gz_ref, sm_ref, al_ref, dt_ref, nw_ref = refs[:8]
        c_in = refs[8:8 + n_comm]
        on_ref, st_ref, t_ref = refs[8 + n_comm:11 + n_comm]
        c_out = refs[11 + n_comm:11 + 2 * n_comm]
        s_ref = refs[11 + 2 * n_comm]
        sems = refs[12 + 2 * n_comm:]

        @pl.when(pl.program_id(0) == 0)
        def _():
            s_ref[...] = jnp.zeros_like(s_ref)
            if comm:
                _comm_plan(comm[0], c_in, c_out, *sems)[0]()

        st_ref[...] = s_ref[...]
        lead = lambda ref: [ref[h] for h in range(h_)]
        on, s_new, t = _gdn_chunk(_head_tiles(q_ref, gs), _head_tiles(k_ref, gs), _head_tiles(v_ref, gs),
                                  _head_tiles(gz_ref, gs), _chunk_rows(sm_ref, gs), al_ref[...], dt_ref[...],
                                  nw_ref[...], lead(s_ref))
        for j in range(gs):
            t_ref[j] = t[j]
        for h in range(h_):
            for j in range(gs):
                on_ref[j * CHUNK:(j + 1) * CHUNK, h * LANE:(h + 1) * LANE] = on[j][h].astype(on_ref.dtype)
            s_ref[h] = s_new[h]

        if comm:
            @pl.when(pl.program_id(0) == steps - 1)
            def _():
                _comm_plan(comm[0], c_in, c_out, *sems)[1]()

    blk = lambda col: pl.BlockSpec((rows, hw), lambda c: (c, col // hw))
    one = pl.BlockSpec((1, LANE), lambda c: (0, 0))
    hbm = pl.BlockSpec(memory_space=pl.ANY)
    return pl.pallas_call(
        body, name="gdn_fwd_" + comm[0] if comm else "gdn_fwd", grid=(steps,),
        in_specs=[blk(XC("gq")), blk(XC("gk")), blk(XC("gv")), blk(_MINE["gz"]),
                  pl.BlockSpec((rows, LANE), lambda c: (c, SMALL_OFF // LANE)), one, one,
                  pl.BlockSpec((1, LANE), lambda c: (0, 0))] + [hbm] * n_comm,
        out_specs=[pl.BlockSpec((rows, hw), lambda c: (c, 0)),
                   pl.BlockSpec((h_, None, LANE, LANE), lambda c: (0, c, 0, 0)),
                   pl.BlockSpec((gs, tn_, tn_), lambda c: (c, 0, 0))] + [hbm] * n_comm,
        out_shape=[jax.ShapeDtypeStruct((length, MIX_W), BF16),
                   jax.ShapeDtypeStruct((h_, steps, LANE, LANE), F32),
                   jax.ShapeDtypeStruct((length // CHUNK, tn_, tn_), F32)]
                  + (_comm_out_shapes(*comm) if comm else []),
        scratch_shapes=[pltpu.VMEM((h_, LANE, LANE), F32)] + (_comm_scratch(n_comm) if comm else []),
        compiler_params=_cparams(("arbitrary",), VMEM_LIMIT),
    )(xc, xc, xc, proj, proj, alog, dtb, nw, *(comm[1] if comm else []))


def _gdn_bwd(xc, proj, alog, dtb, nw, st, tinv, dmix, length, comm=None):
    gs, steps, rows = _scan_steps(length)
    h_ = GDN_HEADS
    hw = h_ * LANE
    tn_ = h_ * CHUNK
    n_comm = len(comm[1]) if comm else 0
    n_in, n_out = 11, 6

    def body(*refs):
        q_ref, k_ref, v_ref, gz_ref, sm_ref, al_ref, dt_ref, nw_ref, st_ref, t_ref, do_ref = refs[:n_in]
        c_in = refs[n_in:n_in + n_comm]
        o0 = n_in + n_comm
        dqkv_ref, dgz_ref, dsm_ref, dal_ref, ddt_ref, dnw_ref = refs[o0:o0 + n_out]
        c_out = refs[o0 + n_out:o0 + n_out + n_comm]
        ds_ref = refs[o0 + n_out + n_comm]
        sems = refs[o0 + n_out + n_comm + 1:]

        @pl.when(pl.program_id(0) == 0)
        def _():
            ds_ref[...] = jnp.zeros_like(ds_ref)
            dal_ref[...] = jnp.zeros_like(dal_ref)
            ddt_ref[...] = jnp.zeros_like(ddt_ref)
            dnw_ref[...] = jnp.zeros_like(dnw_ref)
            if comm:
                _comm_plan(comm[0], c_in, c_out, *sems)[0]()

        lead = lambda ref: [ref[h] for h in range(h_)]
        ins = (_head_tiles(q_ref, gs), _head_tiles(k_ref, gs), _head_tiles(v_ref, gs), _head_tiles(gz_ref, gs),
               _chunk_rows(sm_ref, gs), al_ref[...], dt_ref[...], nw_ref[...], lead(st_ref))
        t_saved = [t_ref[j] for j in range(gs)]
        _, vjp = jax.vjp(lambda *a: _gdn_chunk(*a, t_saved=t_saved)[:2], *ins)
        dq, dk, dv, dgz, dsm, dal, ddt, dnw, ds = vjp((_head_tiles(do_ref, gs), lead(ds_ref)))
        for j in range(gs):
            rws = slice(j * CHUNK, (j + 1) * CHUNK)
            for h in range(h_):
                cols = slice(h * LANE, (h + 1) * LANE)
                for part, d in enumerate((dq, dk, dv)):
                    dqkv_ref[rws, part * hw + h * LANE:part * hw + (h + 1) * LANE] = d[j][h]
                dgz_ref[rws, cols] = dgz[j][h].astype(dgz_ref.dtype)
            dsm_ref[rws, :] = dsm[j]
        for h in range(h_):
            ds_ref[h] = ds[h]
        dal_ref[...] += dal
        ddt_ref[...] += ddt
        dnw_ref[...] += dnw

        if comm:
            @pl.when(pl.program_id(0) == steps - 1)
            def _():
                _comm_plan(comm[0], c_in, c_out, *sems)[1]()

    r = lambda c: steps - 1 - c
    blk = lambda col: pl.BlockSpec((rows, hw), lambda c: (r(c), col // hw))
    one = pl.BlockSpec((1, LANE), lambda c: (0, 0))
    hbm = pl.BlockSpec(memory_space=pl.ANY)
    return pl.pallas_call(
        body, name="gdn_bwd_" + comm[0] if comm else "gdn_bwd", grid=(steps,),
        in_specs=[blk(XC("gq")), blk(XC("gk")), blk(XC("gv")), blk(_MINE["gz"]),
                  pl.BlockSpec((rows, LANE), lambda c: (r(c), SMALL_OFF // LANE)), one, one,
                  pl.BlockSpec((1, LANE), lambda c: (0, 0)),
                  pl.BlockSpec((h_, None, LANE, LANE), lambda c: (0, r(c), 0, 0)),
                  pl.BlockSpec((gs, tn_, tn_), lambda c: (r(c), 0, 0)),
                  blk(0)] + [hbm] * n_comm,
        out_specs=[pl.BlockSpec((rows, 3 * hw), lambda c: (r(c), 0)), blk(_MINE["gz"]),
                   pl.BlockSpec((rows, LANE), lambda c: (r(c), 0)), one, one,
                   pl.BlockSpec((1, LANE), lambda c: (0, 0))] + [hbm] * n_comm,
        out_shape=[jax.ShapeDtypeStruct((length, CONV_ALL), F32), jax.ShapeDtypeStruct((length, N_PAD), BF16),
                   jax.ShapeDtypeStruct((length, LANE), F32),
                   jax.ShapeDtypeStruct((1, LANE), F32), jax.ShapeDtypeStruct((1, LANE), F32),
                   jax.ShapeDtypeStruct((1, LANE), F32)] + (_comm_out_shapes(*comm) if comm else []),
        scratch_shapes=[pltpu.VMEM((h_, LANE, LANE), F32)] + (_comm_scratch(n_comm) if comm else []),
        compiler_params=_cparams(("arbitrary",), VMEM_LIMIT),
    )(xc, xc, xc, proj, proj, alog, dtb, nw, st, tinv, dmix, *(comm[1] if comm else []))


SSD_CHUNKS_PER_STEP = 8


def _ssd_steps(length):
    nc = length // CHUNK
    gs = SSD_CHUNKS_PER_STEP if nc % SSD_CHUNKS_PER_STEP == 0 else 1
    return gs, nc // gs, gs * CHUNK


def _ssd_lanes(g):
    base = _MINE["sdt"] - SMALL_OFF
    return [[base + (g * PAIRS_PER_GROUP + p) * 2 + j for j in range(2)] for p in range(PAIRS_PER_GROUP)]


def _ssd_tiles(ref, gs):
    return [[ref[j * CHUNK:(j + 1) * CHUNK, p * LANE:(p + 1) * LANE] for p in range(PAIRS_PER_GROUP)]
            for j in range(gs)]


def _ssd_gate(y, z, w):
    return _rms(y * _silu(z), w)


def _ssd_fwd(xc, proj, alog_row, dtb_row, d_row, norm_w, mixed, length):
    gs, steps, rows = _ssd_steps(length)
    ppg = PAIRS_PER_GROUP
    gw = ppg * LANE

    def body(x_ref, b_ref, c_ref, sm_ref, al_ref, db_ref, dp_ref, z_ref, nw_ref, _, y_ref, o_ref, st_ref, hs_ref):
        @pl.when(pl.program_id(1) == 0)
        def _():
            hs_ref[...] = jnp.zeros_like(hs_ref)

        st_ref[...] = hs_ref[...]
        y, hs_new = _ssd_chunk(_ssd_lanes(pl.program_id(0)), _ssd_tiles(x_ref, gs), _chunk_rows(b_ref, gs),
                               _chunk_rows(c_ref, gs), _chunk_rows(sm_ref, gs), al_ref[...], db_ref[...],
                               dp_ref[...], [hs_ref[p] for p in range(ppg)])
        for p in range(ppg):
            hs_ref[p] = hs_new[p]
        for j in range(gs):
            rws = slice(j * CHUNK, (j + 1) * CHUNK)
            y_j = jnp.concatenate(y[j], axis=1)
            y_ref[rws, :] = y_j
            o_ref[rws, :] = _ssd_gate(y_j, z_ref[rws, :], nw_ref[...]).astype(o_ref.dtype)

    one = pl.BlockSpec((1, LANE), lambda g, c: (0, 0))
    return pl.pallas_call(
        body, name="ssd_fwd", grid=(SSD_GROUPS, steps),
        in_specs=[pl.BlockSpec((rows, gw), lambda g, c: (c, XC("sx") // gw + g)),
                  pl.BlockSpec((rows, LANE), lambda g, c: (c, XC("sB") // LANE + g)),
                  pl.BlockSpec((rows, LANE), lambda g, c: (c, XC("sC") // LANE + g)),
                  pl.BlockSpec((rows, LANE), lambda g, c: (c, SMALL_OFF // LANE)),
                  one, one, one,
                  pl.BlockSpec((rows, gw), lambda g, c: (c, _MINE["sz"] // gw + g)),
                  pl.BlockSpec((None, 1, gw), lambda g, c: (g, 0, 0)),
                  pl.BlockSpec(memory_space=pl.ANY)],
        out_specs=[pl.BlockSpec((rows, gw), lambda g, c: (c, g)),
                   pl.BlockSpec((rows, gw), lambda g, c: (c, 1 + g)),
                   pl.BlockSpec((None, ppg, None, LANE, SSD_N), lambda g, c: (g, 0, c, 0, 0))],
        out_shape=[jax.ShapeDtypeStruct((length, SSD_GROUPS * gw), F32),
                   jax.ShapeDtypeStruct(mixed.shape, mixed.dtype),
                   jax.ShapeDtypeStruct((SSD_GROUPS, ppg, steps, LANE, SSD_N), F32)],
        input_output_aliases={9: 1},
        scratch_shapes=[pltpu.VMEM((ppg, LANE, SSD_N), F32)],
        compiler_params=_cparams(("parallel", "arbitrary"), VMEM_LIMIT),
    )(xc, xc, xc, proj, alog_row, dtb_row, d_row, proj, norm_w, mixed)


def _ssd_bwd(xc, proj, alog_row, dtb_row, d_row, norm_w, st, y_b, dmix, dproj, dxc, length):
    gs, steps, rows = _ssd_steps(length)
    ppg = PAIRS_PER_GROUP
    gw = ppg * LANE

    def body(x_ref, b_ref, c_ref, sm_ref, al_ref, db_ref, dp_ref, z_ref, nw_ref, st_ref, y_ref, do_ref, _, __,
             dxc_ref, dz_ref, dsm_ref, dal_ref, ddb_ref, ddp_ref, dnw_ref, dhs_ref):
        g = pl.program_id(1)

        @pl.when(jnp.logical_and(pl.program_id(0) == 0, g == 0))
        def _():
            dhs_ref[...] = jnp.zeros_like(dhs_ref)
            dal_ref[...] = jnp.zeros_like(dal_ref)
            ddb_ref[...] = jnp.zeros_like(ddb_ref)
            ddp_ref[...] = jnp.zeros_like(ddp_ref)
            dnw_ref[...] = jnp.zeros_like(dnw_ref)

        pairs = range(ppg)
        dy, dnw = [], jnp.zeros((1, gw), F32)
        for j in range(gs):
            rws = slice(j * CHUNK, (j + 1) * CHUNK)
            _, gate_vjp = jax.vjp(_ssd_gate, y_ref[rws, :], z_ref[rws, :], nw_ref[...])
            dy_j, dz_j, dnw_j = gate_vjp(do_ref[rws, :].astype(F32))
            dz_ref[rws, :] = dz_j.astype(dz_ref.dtype)
            dy.append([dy_j[:, p * LANE:(p + 1) * LANE] for p in pairs])
            dnw = dnw + dnw_j
        dnw_ref[g] += dnw
        ins = (_ssd_tiles(x_ref, gs), _chunk_rows(b_ref, gs), _chunk_rows(c_ref, gs), _chunk_rows(sm_ref, gs),
               al_ref[...], db_ref[...], dp_ref[...], [st_ref[p] for p in pairs])
        _, vjp = jax.vjp(functools.partial(_ssd_chunk, _ssd_lanes(g)), *ins)
        dx, dbm, dcm, dsm, dal, ddb, ddp, dhs = vjp((dy, [dhs_ref[g, p] for p in pairs]))
        for grp in range(SSD_GROUPS):
            @pl.when(g == grp)
            def _(grp=grp):
                for j in range(gs):
                    rws = slice(j * CHUNK, (j + 1) * CHUNK)
                    for p in pairs:
                        col = grp * gw + p * LANE
                        dxc_ref[rws, col:col + LANE] = dx[j][p]
                    cb_ = XC("sB") - XC("sx") + grp * SSD_N
                    cc_ = XC("sC") - XC("sx") + grp * SSD_N
                    dxc_ref[rws, cb_:cb_ + SSD_N] = dbm[j]
                    dxc_ref[rws, cc_:cc_ + SSD_N] = dcm[j]
        for j in range(gs):
            dsm_ref[j * CHUNK:(j + 1) * CHUNK, :] = dsm[j]
        for p in pairs:
            dhs_ref[g, p] = dhs[p]
        dal_ref[g] += dal
        ddb_ref[g] += ddb
        ddp_ref[g] += ddp

    r = lambda c: steps - 1 - c
    one = pl.BlockSpec((1, LANE), lambda c, g: (0, 0))
    acc = pl.BlockSpec((SSD_GROUPS, 1, LANE), lambda c, g: (0, 0, 0))
    prm = jax.ShapeDtypeStruct((SSD_GROUPS, 1, LANE), F32)
    sz_blk = _MINE["sz"] // gw
    return pl.pallas_call(
        body, name="ssd_bwd", grid=(steps, SSD_GROUPS),
        in_specs=[pl.BlockSpec((rows, gw), lambda c, g: (r(c), XC("sx") // gw + g)),
                  pl.BlockSpec((rows, LANE), lambda c, g: (r(c), XC("sB") // LANE + g)),
                  pl.BlockSpec((rows, LANE), lambda c, g: (r(c), XC("sC") // LANE + g)),
                  pl.BlockSpec((rows, LANE), lambda c, g: (r(c), SMALL_OFF // LANE)),
                  one, one, one,
                  pl.BlockSpec((rows, gw), lambda c, g: (r(c), sz_blk + g)),
                  pl.BlockSpec((None, 1, gw), lambda c, g: (g, 0, 0)),
                  pl.BlockSpec((None, ppg, None, LANE, SSD_N), lambda c, g: (g, 0, r(c), 0, 0)),
                  pl.BlockSpec((rows, gw), lambda c, g: (r(c), g)),
                  pl.BlockSpec((rows, gw), lambda c, g: (r(c), 1 + g)),
                  pl.BlockSpec(memory_space=pl.ANY), pl.BlockSpec(memory_space=pl.ANY)],
        out_specs=[pl.BlockSpec((rows, CONV_CH), lambda c, g: (r(c), 1)),
                   pl.BlockSpec((rows, gw), lambda c, g: (r(c), sz_blk + g)),
                   pl.BlockSpec((None, rows, LANE), lambda c, g: (g, r(c), 0)), acc, acc, acc,
                   pl.BlockSpec((SSD_GROUPS, 1, gw), lambda c, g: (0, 0, 0))],
        out_shape=[jax.ShapeDtypeStruct(dxc.shape, dxc.dtype), jax.ShapeDtypeStruct(dproj.shape, dproj.dtype),
                   jax.ShapeDtypeStruct((SSD_GROUPS, length, LANE), F32), prm, prm, prm,
                   jax.ShapeDtypeStruct((SSD_GROUPS, 1, gw), F32)],
        input_output_aliases={12: 1, 13: 0},
        scratch_shapes=[pltpu.VMEM((SSD_GROUPS, ppg, LANE, SSD_N), F32)],
        compiler_params=_cparams(("arbitrary", "arbitrary"), VMEM_LIMIT),
    )(xc, xc, xc, proj, alog_row, dtb_row, d_row, proj, norm_w, st, y_b, dmix, dproj, dxc)


def _ret_fwd(proj, cos2, sin2, dmat, qdec, kdec, cdec, nw, mixed, length):
    gs, steps, rows = _scan_steps(length)
    h_ = RET_HEADS

    def body(q_ref, k_ref, v_ref, g_ref, cos_ref, sin_ref, dm_ref, qd_ref, kd_ref, cd_ref, nw_ref, _,
             on_ref, st_ref, r_ref):
        @pl.when(pl.program_id(0) == 0)
        def _():
            r_ref[...] = jnp.zeros_like(r_ref)

        st_ref[...] = r_ref[...]
        lead = lambda ref: [ref[h] for h in range(h_)]
        on, r_new = _ret_chunk(_head_tiles(q_ref, gs), _head_tiles(k_ref, gs), _head_tiles(v_ref, gs),
                               _head_tiles(g_ref, gs), _chunk_rows(cos_ref, gs), _chunk_rows(sin_ref, gs),
                               lead(dm_ref), lead(qd_ref), lead(kd_ref), lead(cd_ref), nw_ref[...], lead(r_ref))
        for h in range(h_):
            for j in range(gs):
                on_ref[j * CHUNK:(j + 1) * CHUNK, h * LANE:(h + 1) * LANE] = on[j][h].astype(on_ref.dtype)
            r_ref[h] = r_new[h]

    hw = h_ * LANE
    blk = lambda name: pl.BlockSpec((rows, hw), lambda c: (c, _MINE[name] // hw))
    tab = pl.BlockSpec((rows, LANE), lambda c: (c, 0))
    full = lambda *s: pl.BlockSpec(s, lambda c: (0,) * len(s))
    return pl.pallas_call(
        body, name="ret_fwd", grid=(steps,),
        in_specs=[blk("rq"), blk("rk"), blk("rv"), blk("rg"), tab, tab,
                  full(h_, CHUNK, CHUNK), full(h_, CHUNK, 1), full(h_, CHUNK, 1), full(h_, 1, 1), full(1, LANE),
                  pl.BlockSpec(memory_space=pl.ANY)],
        out_specs=[pl.BlockSpec((rows, hw), lambda c: (c, (MIX_W - hw) // hw)),
                   pl.BlockSpec((h_, None, LANE, LANE), lambda c: (0, c, 0, 0))],
        out_shape=[jax.ShapeDtypeStruct(mixed.shape, mixed.dtype),
                   jax.ShapeDtypeStruct((h_, steps, LANE, LANE), F32)],
        input_output_aliases={11: 0},
        scratch_shapes=[pltpu.VMEM((h_, LANE, LANE), F32)],
        compiler_params=_cparams(("arbitrary",), VMEM_LIMIT),
    )(proj, proj, proj, proj, cos2, sin2, dmat, qdec, kdec, cdec, nw, mixed)


def _ret_bwd(proj, cos2, sin2, dmat, qdec, kdec, cdec, nw, st, dmix, dproj, length):
    gs, steps, rows = _scan_steps(length)
    h_ = RET_HEADS
    hw = h_ * LANE

    def body(q_ref, k_ref, v_ref, g_ref, cos_ref, sin_ref, dm_ref, qd_ref, kd_ref, cd_ref, nw_ref, st_ref,
             do_ref, _, dqkvg_ref, dnw_ref, dr_ref):
        @pl.when(pl.program_id(0) == 0)
        def _():
            dr_ref[...] = jnp.zeros_like(dr_ref)
            dnw_ref[...] = jnp.zeros_like(dnw_ref)

        lead = lambda ref: [ref[h] for h in range(h_)]
        consts = (_chunk_rows(cos_ref, gs), _chunk_rows(sin_ref, gs), lead(dm_ref), lead(qd_ref), lead(kd_ref),
                  lead(cd_ref))
        f = lambda q, k, v, g, w_, r_: _ret_chunk(q, k, v, g, *consts, w_, r_)
        _, vjp = jax.vjp(f, _head_tiles(q_ref, gs), _head_tiles(k_ref, gs), _head_tiles(v_ref, gs),
                         _head_tiles(g_ref, gs), nw_ref[...], lead(st_ref))
        dq, dk, dv, dg, dnw, dr = vjp((_head_tiles(do_ref, gs), lead(dr_ref)))
        for h in range(h_):
            for j in range(gs):
                rws = slice(j * CHUNK, (j + 1) * CHUNK)
                for part, d in enumerate((dq, dk, dv, dg)):
                    col = part * hw + h * LANE
                    dqkvg_ref[rws, col:col + LANE] = d[j][h].astype(dqkvg_ref.dtype)
            dr_ref[h] = dr[h]
        dnw_ref[...] += dnw

    r = lambda c: steps - 1 - c
    blk = lambda name: pl.BlockSpec((rows, hw), lambda c: (r(c), _MINE[name] // hw))
    tab = pl.BlockSpec((rows, LANE), lambda c: (r(c), 0))
    full = lambda *s: pl.BlockSpec(s, lambda c: (0,) * len(s))
    assert _MINE["rq"] % (4 * hw) == 0 and [_MINE[n] - _MINE["rq"] for n in ("rk", "rv", "rg")] == [hw, 2 * hw, 3 * hw]
    return pl.pallas_call(
        body, name="ret_bwd", grid=(steps,),
        in_specs=[blk("rq"), blk("rk"), blk("rv"), blk("rg"), tab, tab,
                  full(h_, CHUNK, CHUNK), full(h_, CHUNK, 1), full(h_, CHUNK, 1), full(h_, 1, 1), full(1, LANE),
                  pl.BlockSpec((h_, None, LANE, LANE), lambda c: (0, r(c), 0, 0)),
                  pl.BlockSpec((rows, hw), lambda c: (r(c), (MIX_W - hw) // hw)),
                  pl.BlockSpec(memory_space=pl.ANY)],
        out_specs=[pl.BlockSpec((rows, 4 * hw), lambda c: (r(c), _MINE["rq"] // (4 * hw))), full(1, LANE)],
        out_shape=[jax.ShapeDtypeStruct(dproj.shape, dproj.dtype), jax.ShapeDtypeStruct((1, LANE), F32)],
        input_output_aliases={13: 0},
        scratch_shapes=[pltpu.VMEM((h_, LANE, LANE), F32)],
        compiler_params=_cparams(("arbitrary",), VMEM_LIMIT),
    )(proj, proj, proj, proj, cos2, sin2, dmat, qdec, kdec, cdec, nw, st, dmix, dproj)


def _mesh_pos():
    x, y, c = lax.axis_index("x"), lax.axis_index("y"), lax.axis_index("c")
    return x, y, c, 4 * x + 2 * y + c


def _peer(x, y, c, mask):
    return (x ^ ((mask >> 2) & 1), y ^ ((mask >> 1) & 1), c ^ (mask & 1))


def _gather_plan(ins, outs, send_sems, recv_sems, local_sems):
    x, y, c, me = _mesh_pos()
    n = len(ins)
    idx = lambda px, py, pc: 4 * px + 2 * py + pc
    sib = (x, y, 1 - c)
    chips = [(1 - x, y), (x, 1 - y), (1 - x, 1 - y)]
    local = [pltpu.make_async_copy(ins[a], outs[a].at[me], local_sems.at[a]) for a in range(n)]

    def copy(a, k, src, slab, to):
        return pltpu.make_async_remote_copy(
            src_ref=src, dst_ref=outs[a].at[slab], send_sem=send_sems.at[7 * a + k],
            recv_sem=recv_sems.at[7 * a + k], device_id=to, device_id_type=pl.DeviceIdType.MESH)

    first = [copy(a, 0, ins[a], me, sib) for a in range(n)]
    first += [copy(a, 1 + j, ins[a], me, (*chip, c)) for a in range(n) for j, chip in enumerate(chips)]
    landed = [[copy(a, 1 + j, ins[a], idx(*chip, c), (*chip, c)) for a in range(n)] for j, chip in enumerate(chips)]
    passed = [[copy(a, 4 + j, outs[a].at[idx(*chip, c)], idx(*chip, c), sib) for a in range(n)]
              for j, chip in enumerate(chips)]
    from_sib = [copy(a, 0, ins[a], idx(*sib), sib) for a in range(n)]
    from_sib += [copy(a, 4 + j, ins[a], idx(*chip, 1 - c), sib) for a in range(n) for j, chip in enumerate(chips)]

    def start():
        for cp in local + first:
            cp.start()

    def wait():
        for j in range(len(chips)):
            for a in range(n):
                landed[j][a].wait_recv()
                passed[j][a].start()
        for cp in from_sib:
            cp.wait_recv()
        for cp in first + [cp for row in passed for cp in row]:
            cp.wait_send()
        for cp in local:
            cp.wait()

    return start, wait


def _comm_plan(kind, ins, outs, send_sems, recv_sems, local_sems):
    if kind == "gather":
        return _gather_plan(ins, outs, send_sems, recv_sems, local_sems)
    x, y, c, me = _mesh_pos()
    n = len(ins)
    src = (lambda a, idx: ins[a]) if kind == "gather" else (lambda a, idx: ins[a].at[idx])
    local = [pltpu.make_async_copy(src(a, me), outs[a].at[me], local_sems.at[a]) for a in range(n)]
    sends, recvs = [], []
    for mask in range(1, N_DEV):
        px, py, pc = _peer(x, y, c, mask)
        pidx = 4 * px + 2 * py + pc
        for a in range(n):
            k = (mask - 1) * n + a
            copy = lambda s, d, k=k: pltpu.make_async_remote_copy(
                src_ref=s, dst_ref=d, send_sem=send_sems.at[k], recv_sem=recv_sems.at[k],
                device_id=(px, py, pc), device_id_type=pl.DeviceIdType.MESH)
            sends.append(copy(src(a, pidx), outs[a].at[me]))
            recvs.append(copy(src(a, me), outs[a].at[pidx]))

    def start():
        for cp in local + sends:
            cp.start()

    def wait():
        for cp in sends:
            cp.wait_send()
        for cp in recvs:
            cp.wait_recv()
        for cp in local:
            cp.wait()

    return start, wait


def _comm_out_shapes(kind, arrays):
    return [jax.ShapeDtypeStruct(((N_DEV,) + s.shape) if kind == "gather" else s.shape, s.dtype) for s in arrays]


def _comm_scratch(n):
    k_tot = (N_DEV - 1) * n
    return [pltpu.SemaphoreType.DMA((k_tot,)), pltpu.SemaphoreType.DMA((k_tot,)), pltpu.SemaphoreType.DMA((n,))]


def _comm_call(name, kind, arrays):
    n = len(arrays)

    def body(*refs):
        start, wait = _comm_plan(kind, refs[:n], refs[n:2 * n], *refs[2 * n:])
        start()
        wait()

    hbm = pl.BlockSpec(memory_space=pl.ANY)
    return pl.pallas_call(
        body, name=name, in_specs=[hbm] * n, out_specs=[hbm] * n, out_shape=_comm_out_shapes(kind, arrays),
        scratch_shapes=_comm_scratch(n), compiler_params=pltpu.CompilerParams(has_side_effects=True),
    )(*arrays)


def _adam(name, parts, w, m, v, tr, layer=None, prev=None):
    p_, r_, cw = parts.shape
    c1 = 1.0 - ADAM_B1 ** ADAM_STEP
    c2 = 1.0 - ADAM_B2 ** ADAM_STEP
    n_prev = 4 if prev else 0

    def body(p_ref, w_ref, m_ref, v_ref, *rest):
        g_ref, d_ref, nm_ref, nv_ref = rest[n_prev:]
        g = p_ref[0].astype(F32)
        for i in range(1, p_):
            g = g + p_ref[i].astype(F32)
        nm = ADAM_B1 * m_ref[...] + (1.0 - ADAM_B1) * g
        nv = ADAM_B2 * v_ref[...] + (1.0 - ADAM_B2) * (g * g)
        d_ref[...] = -ADAM_LR * ((nm / c1) / (jnp.sqrt(nv / c2) + ADAM_EPS) + ADAM_WD * w_ref[...])
        g_ref[...] = g
        nm_ref[...] = nm
        nv_ref[...] = nv

    if layer is None:
        spec = pl.BlockSpec((tr, cw), lambda i: (i, 0))
        shp = jax.ShapeDtypeStruct((r_, cw), F32)
    else:
        spec = pl.BlockSpec((None, tr, cw), lambda i: (layer, i, 0))
        shp = jax.ShapeDtypeStruct((DEPTH, r_, cw), F32)
    return pl.pallas_call(
        body, name=name, grid=(r_ // tr,),
        in_specs=[pl.BlockSpec((p_, tr, cw), lambda i: (0, i, 0)), spec, spec, spec]
                 + [pl.BlockSpec(memory_space=pl.ANY)] * n_prev,
        out_specs=[spec] * 4, out_shape=[shp] * 4,
        input_output_aliases={4 + i: i for i in range(n_prev)},
        compiler_params=_cparams(("parallel",), VMEM_LIMIT),
    )(parts, w, m, v, *(prev or ()))


def _shards_to_perm(g_in):
    parts = []
    for n in _ORDER:
        lo, hi = _ORIG[n][0], _ORIG[n][0] + _ORIG[n][1]
        for j in range(lo // SHARD_IN, (hi - 1) // SHARD_IN + 1):
            a, b = max(lo, j * SHARD_IN), min(hi, (j + 1) * SHARD_IN)
            parts.append(g_in[j][:, a - j * SHARD_IN:b - j * SHARD_IN])
    parts.append(jnp.zeros((g_in.shape[1], N_PAD - N_USED), g_in.dtype))
    return jnp.concatenate(parts, axis=1)


def _perm_to_shards(g):
    by_orig = sorted(_ORDER, key=lambda n: _ORIG[n][0])
    slabs = []
    for j in range(N_DEV):
        lo, hi = j * SHARD_IN, (j + 1) * SHARD_IN
        parts = []
        for n in by_orig:
            a, b = max(lo, _ORIG[n][0]), min(hi, _ORIG[n][0] + _ORIG[n][1])
            if a < b:
                parts.append(g[:, _MINE[n] + a - _ORIG[n][0]:_MINE[n] + b - _ORIG[n][0]])
        slabs.append(jnp.concatenate(parts, axis=1))
    return jnp.stack(slabs, axis=0)


def _ret_consts(length):
    lg = jnp.log(1.0 - 2.0 ** (-5.0 - jnp.arange(RET_HEADS, dtype=F32)))
    idx = jnp.arange(CHUNK, dtype=F32)
    rel = idx[:, None] - idx[None, :]
    dmat = jnp.where(rel[None] >= 0, jnp.exp(jnp.maximum(rel, 0.0)[None] * lg[:, None, None]), 0.0)
    qdec = jnp.exp((idx[:, None] + 1.0) * lg[None, :]).T[:, :, None]
    kdec = jnp.exp((CHUNK - 1.0 - idx)[:, None] * lg[None, :]).T[:, :, None]
    cdec = jnp.exp(CHUNK * lg)[:, None, None]
    half = RET_DK // 2
    inv = ROPE_BASE ** (-jnp.arange(half, dtype=F32) / half)
    ang = jnp.arange(length, dtype=jnp.int32).astype(F32)[:, None] * inv[None, :]
    cos, sin = jnp.cos(ang), jnp.sin(ang)
    return (jnp.concatenate([cos, cos], axis=1), jnp.concatenate([-sin, sin], axis=1), dmat, qdec, kdec, cdec)


def _tiles(length, pref=512):
    return pref if length % pref == 0 else length


def _layer_fwd(x, p, consts, length, last, target, comm=None):
    tm = _tiles(length)
    (h,), _ = _rowmap("pre_norm", lambda x_, w_: ((_rms(x_, w_),), ()), length, tm,
                      [(x, D_MODEL, 0)], [p["pre_norm"]], out_ws=[D_MODEL], out_dtypes=[BF16])
    big = _tile(length, 2 * MM_TILE)
    proj = _matmul("in_proj", h, p["w_in"], "nn", length, N_PAD, D_MODEL, tm=big)
    xc = _conv_fwd(proj, p["conv_w"], p["conv_b"], length, _tiles(length, 2048), 512)
    mixed, st_a, tinv, *got = _gdn_fwd(xc, proj, p["gdn_A_log"], p["gdn_dt_bias"], p["gdn_norm"], length, comm)
    y_b, mixed, st_b = _ssd_fwd(xc, proj, p["ssd_A_log"], p["ssd_dt_bias"], p["ssd_D"], p["ssd_norm"], mixed, length)
    mixed, st_c = _ret_fwd(proj, *consts, p["ret_norm"], mixed, length)
    w_out = p["w_out"](got) if callable(p["w_out"]) else p["w_out"]
    out = _matmul("out_proj", mixed, w_out, "nn", length, D_MODEL, MIX_W)
    res = dict(x=x, h=h, proj=proj, xc=xc, st_a=st_a, tinv=tinv, st_b=st_b, st_c=st_c, y_b=y_b, mixed=mixed,
               out=out, w_out=w_out)
    if not last:
        (y,), _ = _rowmap("post_norm", lambda x_, o_, w_: ((x_ + _rms(o_, w_),), ()), length, tm,
                          [(x, D_MODEL, 0), (out, D_MODEL, 0)], [p["post_norm"]], out_ws=[D_MODEL])
        return y, res, None, got

    def head(x_, o_, t_, w_):
        normed, post_vjp = jax.vjp(_rms, o_, w_)
        e = x_ + normed - t_
        row = jnp.mean(e * e, axis=-1, keepdims=True)
        loss = 0.5 * jnp.sum(row, axis=0, keepdims=True)
        dy_ = e * (1.0 / D_MODEL)
        d_out, dw = post_vjp(dy_)
        return (dy_, d_out), (loss + jnp.zeros((1, LANE), F32), dw)

    (dy, d_out), (loss, dw_post) = _rowmap(
        "loss_head", head, length, tm, [(x, D_MODEL, 0), (out, D_MODEL, 0), (target, D_MODEL, 0)], [p["post_norm"]],
        out_ws=[D_MODEL, D_MODEL], out_dtypes=[F32, BF16], acc_ws=[LANE, D_MODEL])
    res["post_bwd"] = (d_out, dw_post)
    return None, res, (dy, loss[0, 0]), got


def _layer_bwd(dy, p, res, consts, length, comm_scan=None, comm_dx=None):
    tm = _tiles(length)
    g = {}
    if "post_bwd" in res:
        d_out, g["post_norm"] = res["post_bwd"]
    else:
        (d_out,), (g["post_norm"],) = _rowmap("post_norm_bwd", lambda o_, w_: ((_rms(o_, w_),), ()), length, tm,
                                              [(res["out"], D_MODEL, 0)], [p["post_norm"]], cts=[(dy, D_MODEL, 0)],
                                              d_dtypes=[BF16])
    dmix = _matmul("out_proj_dx", d_out, res["w_out"], "nt", length, MIX_W, D_MODEL)
    g["w_out"] = _matmul("out_proj_dw", res["mixed"], d_out, "tn", MIX_W, D_MODEL, length, out_dtype=BF16)
    comm = comm_scan(g) if comm_scan else None
    proj, xc = res["proj"], res["xc"]
    (dxc, dproj, dsm_a, g["gdn_A_log"], g["gdn_dt_bias"], g["gdn_norm"], *got) = _gdn_bwd(
        xc, proj, p["gdn_A_log"], p["gdn_dt_bias"], p["gdn_norm"], res["st_a"], res["tinv"], dmix, length, comm)
    dxc, dproj, dsm_b, g["ssd_A_log"], g["ssd_dt_bias"], g["ssd_D"], g["ssd_norm"] = _ssd_bwd(
        xc, proj, p["ssd_A_log"], p["ssd_dt_bias"], p["ssd_D"], p["ssd_norm"], res["st_b"], res["y_b"], dmix,
        dproj, dxc, length)
    dproj, g["ret_norm"] = _ret_bwd(proj, *consts, p["ret_norm"], res["st_c"], dmix, dproj, length)
    tmc = _tiles(length, 2048)
    dpre, g["conv_w"], g["conv_b"] = _conv_bwd_pre(proj, p["conv_w"], p["conv_b"], dxc, length, tmc, 512)
    dproj = _conv_bwd_x(dpre, p["conv_w"], dproj, length, tmc, 512)
    pad = SMALL_OFF % 512 + 512 - LANE
    (dproj,), _ = _rowmap("dsmall", lambda a_, b_, c_: ((jnp.concatenate([a_ + b_ + c_, jnp.zeros((a_.shape[0], pad), F32)], axis=1),), ()),
                          length, tm, [(dsm_a, LANE, 0), (dsm_b[0], LANE, 0), (dsm_b[1], LANE, 0)], [],
                          out_ws=[512], out_dtypes=[BF16], place={0: (dproj, N_PAD, SMALL_OFF // 512)})
    big = _tile(length, 2 * MM_TILE)
    g["w_in"] = _matmul("in_proj_dw", res["h"], dproj, "tn", D_MODEL, N_PAD, length, tk=big, out_dtype=BF16)
    if comm_dx:
        dh, *got_dx = _matmul("in_proj_dx_scatter", dproj, p["w_in"], "nt", length, D_MODEL, N_PAD, comm=comm_dx(g),
                              tm=big)
    else:
        dh, got_dx = _matmul("in_proj_dx", dproj, p["w_in"], "nt", length, D_MODEL, N_PAD, tm=big), []
    (dx,), (g["pre_norm"],) = _rowmap(
        "pre_norm_bwd", lambda x_, w_: ((_rms(x_, w_), x_), ()), length, tm,
        [(res["x"], D_MODEL, 0)], [p["pre_norm"]], cts=[(dh, D_MODEL, 0), (dy, D_MODEL, 0)])
    return dx, g, got, got_dx


def _lane_row(vals, piece):
    lo = _MINE[piece] - SMALL_OFF
    return jnp.pad(vals[None], ((0, 0), (lo, LANE - lo - vals.shape[0])))


def _from_lane_row(row, piece, n):
    lo = _MINE[piece] - SMALL_OFF
    return row.reshape(-1, LANE).sum(axis=0)[lo:lo + n]


def _make_layer(l, w_in_perm, w_out_full, conv_full, sw):
    return dict(
        pre_norm=sw["pre_norm"][l][None], post_norm=sw["post_norm"][l][None], w_in=w_in_perm,
        w_out=w_out_full,
        conv_w=jnp.concatenate([conv_full[0], conv_full[1]], axis=1),
        conv_b=jnp.concatenate([jnp.zeros((CONV_CH,), F32), sw["ssd_conv_b"][l]])[None],
        gdn_A_log=_lane_row(sw["gdn_A_log"][l], "ga"), gdn_dt_bias=_lane_row(sw["gdn_dt_bias"][l], "ga"),
        gdn_norm=sw["gdn_norm"][l][None],
        ssd_A_log=_lane_row(sw["ssd_A_log"][l], "sdt"), ssd_dt_bias=_lane_row(sw["ssd_dt_bias"][l], "sdt"),
        ssd_D=_lane_row(sw["ssd_D"][l], "sdt"),
        ssd_norm=sw["ssd_norm"][l].reshape(SSD_GROUPS, 1, -1),
        ret_norm=sw["ret_norm"][l][None])


def _local_step(xb, tgt, layer0, make_layer1, length, fwd_comm=None, comm_scan=None, comm_dx=None):
    consts = _ret_consts(length)
    y0, res0, _, got = _layer_fwd(xb, layer0, consts, length, False, tgt, fwd_comm)
    layer1 = make_layer1(got)
    _, res1, (dy, loss_local), _ = _layer_fwd(y0, layer1, consts, length, True, tgt)
    dy, grads1, _, _ = _layer_bwd(dy, layer1, res1, consts, length)
    dx, grads0, recv_scan, recv_dx = _layer_bwd(
        dy, layer0, res0, consts, length,
        (lambda g0: comm_scan(grads1, g0)) if comm_scan else None, comm_dx)
    return loss_local, dx, [grads0, grads1], recv_scan, recv_dx


_SMALL = ["pre_norm", "post_norm", "gdn_A_log", "gdn_dt_bias", "gdn_norm", "ssd_conv_b", "ssd_A_log",
          "ssd_dt_bias", "ssd_D", "ssd_norm", "ret_norm"]


def _pack_small(arrs):
    rows = []
    for n in _SMALL:
        flat = arrs[n].reshape(-1)
        pad = (-flat.shape[0]) % LANE
        rows.append(jnp.pad(flat, (0, pad)).reshape(-1, LANE))
    out = jnp.concatenate(rows, axis=0)
    return jnp.pad(out, ((0, (-out.shape[0]) % SUBLANE), (0, 0)))


def _unpack_small(packed, like):
    out, r = {}, 0
    for n in _SMALL:
        cnt = like[n].size
        nrow = -(-cnt // LANE)
        out[n] = packed[r:r + nrow].reshape(-1)[:cnt].reshape(like[n].shape)
        r += nrow
    return out


def kernel(x, pre_norm, post_norm, w_in, gdn_conv, gdn_A_log, gdn_dt_bias, gdn_norm, ssd_conv, ssd_conv_b, ssd_A_log, ssd_dt_bias, ssd_D, ssd_norm, ret_norm, w_out, loss_target, m_pre_norm, m_post_norm, m_w_in, m_gdn_conv, m_gdn_A_log, m_gdn_dt_bias, m_gdn_norm, m_ssd_conv, m_ssd_conv_b, m_ssd_A_log, m_ssd_dt_bias, m_ssd_D, m_ssd_norm, m_ret_norm, m_w_out, v_pre_norm, v_post_norm, v_w_in, v_gdn_conv, v_gdn_A_log, v_gdn_dt_bias, v_gdn_norm, v_ssd_conv, v_ssd_conv_b, v_ssd_A_log, v_ssd_dt_bias, v_ssd_D, v_ssd_norm, v_ret_norm, v_w_out):
    length = x.shape[1]
    xb = x[0]
    tgt = loss_target[0]
    small_w = dict(pre_norm=pre_norm, post_norm=post_norm, gdn_A_log=gdn_A_log, gdn_dt_bias=gdn_dt_bias,
                   gdn_norm=gdn_norm, ssd_conv_b=ssd_conv_b, ssd_A_log=ssd_A_log, ssd_dt_bias=ssd_dt_bias,
                   ssd_D=ssd_D, ssd_norm=ssd_norm, ret_norm=ret_norm)
    small_m = dict(pre_norm=m_pre_norm, post_norm=m_post_norm, gdn_A_log=m_gdn_A_log, gdn_dt_bias=m_gdn_dt_bias,
                   gdn_norm=m_gdn_norm, ssd_conv_b=m_ssd_conv_b, ssd_A_log=m_ssd_A_log, ssd_dt_bias=m_ssd_dt_bias,
                   ssd_D=m_ssd_D, ssd_norm=m_ssd_norm, ret_norm=m_ret_norm)
    small_v = dict(pre_norm=v_pre_norm, post_norm=v_post_norm, gdn_A_log=v_gdn_A_log, gdn_dt_bias=v_gdn_dt_bias,
                   gdn_norm=v_gdn_norm, ssd_conv_b=v_ssd_conv_b, ssd_A_log=v_ssd_A_log, ssd_dt_bias=v_ssd_dt_bias,
                   ssd_D=v_ssd_D, ssd_norm=v_ssd_norm, ret_norm=v_ret_norm)

    w_in_b, w_out_b = w_in.astype(BF16), w_out.astype(BF16)
    conv_shard = jnp.stack([gdn_conv, ssd_conv], axis=1)
    full_out = lambda g_out: g_out.reshape(MIX_W, D_MODEL)

    def assemble(l, g_in, w_out_full, g_conv):
        conv_full = g_conv.transpose(1, 2, 0, 3).reshape(2, CONV_W, CONV_CH)
        return _make_layer(l, _shards_to_perm(g_in), w_out_full, conv_full, small_w)

    slab_in = lambda g: _perm_to_shards(g["w_in"]).astype(BF16)
    slab_out = lambda g: g["w_out"].reshape(N_DEV, SHARD_OUT, D_MODEL).astype(BF16)
    slab_conv = lambda g: jnp.stack(
        [g["conv_w"][:, k * CONV_CH:(k + 1) * CONV_CH].reshape(CONV_W, N_DEV, SHARD_CONV).transpose(1, 0, 2)
         for k in range(2)], axis=1).reshape(N_DEV, 2 * CONV_W, SHARD_CONV)

    g_in0, g_conv0 = _comm_call("gather_layer0", "gather", [w_in_b[0], conv_shard[0]])
    layer0 = assemble(0, g_in0, lambda got: full_out(got[3]), g_conv0)
    loss_local, dx, grads, recv_scan, recv_dx = _local_step(
        xb, tgt, layer0, lambda got: assemble(1, got[0], full_out(got[1]), got[2]), length,
        fwd_comm=("gather", [w_in_b[1], w_out_b[1], conv_shard[1], w_out_b[0]]),
        comm_scan=lambda g1, g0: ("scatter", [slab_in(g1), slab_out(g1), slab_conv(g1), slab_out(g0)]),
        comm_dx=lambda g0: ("scatter", [slab_in(g0), slab_conv(g0)]))
    grad_x = dx[None]
    loss = lax.psum(loss_local, ("x", "y", "c"))

    small_g = dict(
        pre_norm=jnp.concatenate([grads[l]["pre_norm"] for l in range(DEPTH)], axis=0),
        post_norm=jnp.concatenate([grads[l]["post_norm"] for l in range(DEPTH)], axis=0),
        gdn_A_log=jnp.stack([_from_lane_row(grads[l]["gdn_A_log"], "ga", GDN_HEADS) for l in range(DEPTH)]),
        gdn_dt_bias=jnp.stack([_from_lane_row(grads[l]["gdn_dt_bias"], "ga", GDN_HEADS) for l in range(DEPTH)]),
        gdn_norm=jnp.concatenate([grads[l]["gdn_norm"] for l in range(DEPTH)], axis=0),
        ssd_conv_b=jnp.concatenate([grads[l]["conv_b"][:, CONV_CH:] for l in range(DEPTH)], axis=0),
        ssd_A_log=jnp.stack([_from_lane_row(grads[l]["ssd_A_log"], "sdt", SSD_HEADS) for l in range(DEPTH)]),
        ssd_dt_bias=jnp.stack([_from_lane_row(grads[l]["ssd_dt_bias"], "sdt", SSD_HEADS) for l in range(DEPTH)]),
        ssd_D=jnp.stack([_from_lane_row(grads[l]["ssd_D"], "sdt", SSD_HEADS) for l in range(DEPTH)]),
        ssd_norm=jnp.concatenate([grads[l]["ssd_norm"].reshape(1, -1) for l in range(DEPTH)], axis=0),
        ret_norm=jnp.concatenate([grads[l]["ret_norm"] for l in range(DEPTH)], axis=0))
    gs = _pack_small(small_g)
    gs8 = jnp.broadcast_to(gs[None], (N_DEV,) + gs.shape)
    (r_small,) = _comm_call("exchange_small", "scatter", [gs8])
    recv = [[recv_dx[0], recv_scan[3], recv_dx[1]], recv_scan[:3]]

    conv_w = lambda g_, s_, l: jnp.stack([g_[l], s_[l]], axis=0).reshape(2 * CONV_W, SHARD_CONV)
    o_in, o_out, o_conv = None, None, []
    for l in range(DEPTH):
        o_in = _adam(f"adam_w_in{l}", recv[l][0], w_in, m_w_in, v_w_in, 128, l, o_in)
        o_out = _adam(f"adam_w_out{l}", recv[l][1], w_out, m_w_out, v_w_out, 128, l, o_out)
        o_conv.append(_adam(f"adam_conv{l}", recv[l][2], conv_w(gdn_conv, ssd_conv, l),
                            conv_w(m_gdn_conv, m_ssd_conv, l), conv_w(v_gdn_conv, v_ssd_conv, l), 2 * CONV_W))
    ps_w, ps_m, ps_v = _pack_small(small_w), _pack_small(small_m), _pack_small(small_v)
    o_small = _adam("adam_small", r_small, ps_w, ps_m, ps_v, ps_w.shape[0])

    names = ["pre_norm", "post_norm", "w_in", "gdn_conv", "gdn_A_log", "gdn_dt_bias", "gdn_norm", "ssd_conv",
             "ssd_conv_b", "ssd_A_log", "ssd_dt_bias", "ssd_D", "ssd_norm", "ret_norm", "w_out"]
    outs = []
    for kind in range(4):
        d = _unpack_small(o_small[kind], small_w)
        cv = jnp.stack([o_conv[l][kind].reshape(2, CONV_W, SHARD_CONV) for l in range(DEPTH)], axis=1)
        d["w_in"] = o_in[kind]
        d["w_out"] = o_out[kind]
        d["gdn_conv"] = cv[0]
        d["ssd_conv"] = cv[1]
        outs.extend(d[n] for n in names)
    return (loss, grad_x, *outs)
```
